```python
import math
import jax, jax.numpy as jnp
from jax import lax
import numpy as np

D_MODEL = 1024
BATCH = 8
SEQ = 8192
DEPTH = 1

SSD_D_INNER = D_MODEL
SSD_HEADDIM = 64
SSD_N_HEADS = SSD_D_INNER // SSD_HEADDIM
SSD_N_GROUPS = 4
SSD_HPG = SSD_N_HEADS // SSD_N_GROUPS
SSD_D_STATE = 128
SSD_CONV = 4
SSD_CHUNK = 128
SSD_CONV_DIM = SSD_D_INNER + 2 * SSD_N_GROUPS * SSD_D_STATE
S5_WIDTH = D_MODEL // 2
S5_GROUP = 16
S5_N_GROUPS = S5_WIDTH // S5_GROUP
S5_STATE = 64
D_FF = 4 * D_MODEL
N_BRANCHES = 2
OFF_Z = 0
OFF_XBC = OFF_Z + SSD_D_INNER
OFF_DT = OFF_XBC + SSD_CONV_DIM
OFF_U = OFF_DT + SSD_N_HEADS
OFF_G = OFF_U + S5_WIDTH
D_IN_PROJ = OFF_G + N_BRANCHES * D_MODEL
EPS = 1e-6
DT_MIN = 1e-3
DT_MAX = 1e-1

kernel_name = "hybrid_ssd_s5_gated_block"


def rms_norm(x, g):
    xf = x.astype(jnp.float32)
    y = xf * lax.rsqrt(jnp.mean(xf * xf, axis=-1, keepdims=True) + EPS)
    return y.astype(x.dtype) * g


def causal_dwconv(u, w, b):
    y = lax.conv_general_dilated(
        u, w.astype(u.dtype)[:, None, :], window_strides=(1,),
        padding=[(SSD_CONV - 1, 0)], dimension_numbers=('NWC', 'WIO', 'NWC'),
        feature_group_count=u.shape[-1])
    return y + b


def ssd_chunked(xh, dt, a, bm, cm):
    bsz, seqlen = xh.shape[0], xh.shape[1]
    nc = seqlen // SSD_CHUNK
    q, g, r, p, n = SSD_CHUNK, SSD_N_GROUPS, SSD_HPG, SSD_HEADDIM, SSD_D_STATE
    x = xh.astype(jnp.float32).reshape(bsz, nc, q, g, r, p)
    dtc = dt.astype(jnp.float32).reshape(bsz, nc, q, g, r)
    bc = bm.astype(jnp.float32).reshape(bsz, nc, q, g, n)
    cc = cm.astype(jnp.float32).reshape(bsz, nc, q, g, n)
    la = dtc * a.astype(jnp.float32).reshape(g, r)
    la_cum = jnp.cumsum(la, axis=2)
    xdt = x * dtc[..., None]
    seg = la_cum[:, :, :, None] - la_cum[:, :, None]
    causal = jnp.tril(jnp.ones((q, q), dtype=bool))[:, :, None, None]
    decay = jnp.exp(jnp.where(causal, seg, -jnp.inf))
    cb = jnp.einsum('bctgn,bcsgn->bctsg', cc, bc)
    y_diag = jnp.einsum('bctsg,bctsgr,bcsgrp->bctgrp', cb, decay, xdt)
    decay_end = jnp.exp(la_cum[:, :, -1:] - la_cum)
    states = jnp.einsum('bcsgn,bcsgr,bcsgrp->bcgrpn', bc, decay_end, xdt)
    chunk_decay = jnp.exp(la_cum[:, :, -1])

    def step(carry, inp):
        st, dec = inp
        return carry * dec[..., None, None] + st, carry

    init = jnp.zeros((bsz, g, r, p, n), jnp.float32)
    _, prev = lax.scan(step, init, (jnp.moveaxis(states, 1, 0), jnp.moveaxis(chunk_decay, 1, 0)))
    prev = jnp.moveaxis(prev, 0, 1)
    y_off = jnp.einsum('bctgn,bcgrpn,bctgr->bctgrp', cc, prev, jnp.exp(la_cum))
    return (y_diag + y_off).reshape(bsz, seqlen, SSD_N_HEADS, p)


def s5_mixer(u, a_re, a_im, log_dt, b_re, b_im, c_re, c_im, d):
    bsz, seqlen = u.shape[0], u.shape[1]
    ug = u.astype(jnp.float32).reshape(bsz, seqlen, S5_N_GROUPS, S5_GROUP)
    dt = jnp.exp(log_dt.astype(jnp.float32))[:, None]
    a_re = a_re.astype(jnp.float32)
    a_im = a_im.astype(jnp.float32)
    mag = jnp.exp(a_re * dt)
    ab_re = mag * jnp.cos(a_im * dt)
    ab_im = mag * jnp.sin(a_im * dt)
    den = a_re * a_re + a_im * a_im
    nr = ab_re - 1.0
    ni = ab_im
    coef_re = (nr * a_re + ni * a_im) / den
    coef_im = (ni * a_re - nr * a_im) / den
    bb_re = coef_re[..., None] * b_re - coef_im[..., None] * b_im
    bb_im = coef_re[..., None] * b_im + coef_im[..., None] * b_re
    bu_re = jnp.einsum('blgk,gpk->blgp', ug, bb_re)
    bu_im = jnp.einsum('blgk,gpk->blgp', ug, bb_im)
    a_seq_re = jnp.broadcast_to(ab_re, (1, seqlen, S5_N_GROUPS, S5_STATE))
    a_seq_im = jnp.broadcast_to(ab_im, (1, seqlen, S5_N_GROUPS, S5_STATE))

    def combine(e1, e2):
        a1r, a1i, b1r, b1i = e1
        a2r, a2i, b2r, b2i = e2
        return (a2r * a1r - a2i * a1i,
                a2r * a1i + a2i * a1r,
                a2r * b1r - a2i * b1i + b2r,
                a2r * b1i + a2i * b1r + b2i)

    _, _, s_re, s_im = lax.associative_scan(combine, (a_seq_re, a_seq_im, bu_re, bu_im), axis=1)
    y = jnp.einsum('blgp,gkp->blgk', s_re, c_re) - jnp.einsum('blgp,gkp->blgk', s_im, c_im)
    return y.reshape(bsz, seqlen, S5_WIDTH) + d * u


def _fwd_setup_inputs(seed: int = 0) -> dict:
    key = jax.random.key(seed)
    ks = jax.random.split(key, 32)
    f32 = jnp.float32
    nrm = lambda k, shp, s: jax.random.normal(k, shp, f32) * s
    x = jax.random.normal(ks[0], (BATCH, SEQ, D_MODEL), f32)
    norm_mix_g = 1.0 + nrm(ks[1], (D_MODEL,), 0.02)
    w_in = nrm(ks[2], (D_MODEL, D_IN_PROJ), D_MODEL ** -0.5)
    conv_w = nrm(ks[3], (SSD_CONV, SSD_CONV_DIM), SSD_CONV ** -0.5)
    conv_b = nrm(ks[4], (SSD_CONV_DIM,), 0.02)
    dt0 = jnp.exp(jax.random.uniform(ks[5], (SSD_N_HEADS,), f32, math.log(DT_MIN), math.log(DT_MAX)))
    dt_bias = dt0 + jnp.log(-jnp.expm1(-dt0))
    a_log = jnp.log(jax.random.uniform(ks[6], (SSD_N_HEADS,), f32, 1.0, 16.0))
    d_ssd = 1.0 + nrm(ks[7], (SSD_N_HEADS,), 0.02)
    ssd_norm_g = 1.0 + nrm(ks[8], (SSD_D_INNER,), 0.02)
    s5_a_re = -0.5 + nrm(ks[9], (S5_N_GROUPS, S5_STATE), 0.01)
    s5_a_im = (math.pi * jnp.arange(S5_STATE, dtype=f32))[None, :] + nrm(ks[10], (S5_N_GROUPS, S5_STATE), 0.01)
    s5_log_dt = jax.random.uniform(ks[11], (S5_N_GROUPS,), f32, math.log(DT_MIN), math.log(DT_MAX))
    s5_b_re = nrm(ks[12], (S5_N_GROUPS, S5_STATE, S5_GROUP), (2 * S5_GROUP) ** -0.5)
    s5_b_im = nrm(ks[13], (S5_N_GROUPS, S5_STATE, S5_GROUP), (2 * S5_GROUP) ** -0.5)
    s5_c_re = nrm(ks[14], (S5_N_GROUPS, S5_GROUP, S5_STATE), (2 * S5_STATE) ** -0.5)
    s5_c_im = nrm(ks[15], (S5_N_GROUPS, S5_GROUP, S5_STATE), (2 * S5_STATE) ** -0.5)
    s5_d = nrm(ks[16], (S5_WIDTH,), 1.0)
    s5_glu_w = nrm(ks[17], (S5_WIDTH, S5_WIDTH), S5_WIDTH ** -0.5)
    s5_glu_b = nrm(ks[18], (S5_WIDTH,), 0.02)
    w_branch = jnp.concatenate([
        nrm(ks[19], (SSD_D_INNER, D_MODEL), SSD_D_INNER ** -0.5),
        nrm(ks[20], (S5_WIDTH, D_MODEL), S5_WIDTH ** -0.5)], axis=0)
    w_out = nrm(ks[21], (D_MODEL, D_MODEL), D_MODEL ** -0.5)
    norm_mlp_g = 1.0 + nrm(ks[22], (D_MODEL,), 0.02)
    w_mlp_in = nrm(ks[23], (D_MODEL, D_FF), D_MODEL ** -0.5)
    w_mlp_out = nrm(ks[24], (D_FF, D_MODEL), D_FF ** -0.5)
    norm_final_g = 1.0 + nrm(ks[25], (D_MODEL,), 0.02)
    return {"x": x, "norm_mix_g": norm_mix_g, "w_in": w_in, "conv_w": conv_w, "conv_b": conv_b,
            "dt_bias": dt_bias, "a_log": a_log, "d_ssd": d_ssd, "ssd_norm_g": ssd_norm_g,
            "s5_a_re": s5_a_re, "s5_a_im": s5_a_im, "s5_log_dt": s5_log_dt,
            "s5_b_re": s5_b_re, "s5_b_im": s5_b_im, "s5_c_re": s5_c_re, "s5_c_im": s5_c_im,
            "s5_d": s5_d, "s5_glu_w": s5_glu_w, "s5_glu_b": s5_glu_b,
            "w_branch": w_branch, "w_out": w_out, "norm_mlp_g": norm_mlp_g,
            "w_mlp_in": w_mlp_in, "w_mlp_out": w_mlp_out, "norm_final_g": norm_final_g}


def _fwd_reference(x, norm_mix_g, w_in, conv_w, conv_b, dt_bias, a_log, d_ssd, ssd_norm_g,
              s5_a_re, s5_a_im, s5_log_dt, s5_b_re, s5_b_im, s5_c_re, s5_c_im, s5_d,
              s5_glu_w, s5_glu_b, w_branch, w_out, norm_mlp_g, w_mlp_in, w_mlp_out,
              norm_final_g):
    bsz, seqlen = x.shape[0], x.shape[1]
    for _ in range(DEPTH):
        h = rms_norm(x, norm_mix_g)
        proj = h @ w_in
        z = proj[..., OFF_Z:OFF_XBC]
        xbc = proj[..., OFF_XBC:OFF_DT]
        dt_raw = proj[..., OFF_DT:OFF_U]
        u5 = proj[..., OFF_U:OFF_G]
        gates = jax.nn.sigmoid(proj[..., OFF_G:].astype(jnp.float32)).reshape(bsz, seqlen, N_BRANCHES, D_MODEL)

        xbc = jax.nn.silu(causal_dwconv(xbc, conv_w, conv_b))
        xs = xbc[..., :SSD_D_INNER].reshape(bsz, seqlen, SSD_N_HEADS, SSD_HEADDIM)
        bm = xbc[..., SSD_D_INNER:SSD_D_INNER + SSD_N_GROUPS * SSD_D_STATE].reshape(bsz, seqlen, SSD_N_GROUPS, SSD_D_STATE)
        cm = xbc[..., SSD_D_INNER + SSD_N_GROUPS * SSD_D_STATE:].reshape(bsz, seqlen, SSD_N_GROUPS, SSD_D_STATE)
        dt = jax.nn.softplus(dt_raw.astype(jnp.float32) + dt_bias)
        a = -jnp.exp(a_log.astype(jnp.float32))
        y_a = ssd_chunked(xs, dt, a, bm, cm) + d_ssd[:, None] * xs
        y_a = y_a.reshape(bsz, seqlen, SSD_D_INNER) * jax.nn.silu(z)
        y_a = rms_norm(y_a.reshape(bsz, seqlen, SSD_N_GROUPS, SSD_D_INNER // SSD_N_GROUPS), 1.0)
        y_a = y_a.reshape(bsz, seqlen, SSD_D_INNER) * ssd_norm_g

        y_b = s5_mixer(u5, s5_a_re, s5_a_im, s5_log_dt, s5_b_re, s5_b_im, s5_c_re, s5_c_im, s5_d)
        y_b = jax.nn.gelu(y_b)
        y_b = y_b * jax.nn.sigmoid(y_b @ s5_glu_w + s5_glu_b)

        p_a = y_a @ w_branch[:SSD_D_INNER]
        p_b = y_b @ w_branch[SSD_D_INNER:]
        merged = gates[:, :, 0] * p_a + gates[:, :, 1] * p_b
        x = x + (merged @ w_out).astype(x.dtype)

        h2 = rms_norm(x, norm_mlp_g)
        x = x + (jnp.square(jax.nn.relu(h2 @ w_mlp_in)) @ w_mlp_out).astype(x.dtype)
    return rms_norm(x, norm_final_g)


import jax as _jax
import jax.numpy as _jnp

TWIN_FORMAT = 'train_step'
FWD_PARAMS = ['x', 'norm_mix_g', 'w_in', 'conv_w', 'conv_b', 'dt_bias', 'a_log', 'd_ssd', 'ssd_norm_g', 's5_a_re', 's5_a_im', 's5_log_dt', 's5_b_re', 's5_b_im', 's5_c_re', 's5_c_im', 's5_d', 's5_glu_w', 's5_glu_b', 'w_branch', 'w_out', 'norm_mlp_g', 'w_mlp_in', 'w_mlp_out', 'norm_final_g']
TWIN_WEIGHTS = ['norm_mix_g', 'w_in', 'conv_w', 'conv_b', 'dt_bias', 'a_log', 'd_ssd', 'ssd_norm_g', 's5_a_re', 's5_a_im', 's5_log_dt', 's5_b_re', 's5_b_im', 's5_c_re', 's5_c_im', 's5_d', 's5_glu_w', 's5_glu_b', 'w_branch', 'w_out', 'norm_mlp_g', 'w_mlp_in', 'w_mlp_out', 'norm_final_g']
TWIN_DIFF_INPUT = 'x'
TWIN_INPUTS = ['x', 'norm_mix_g', 'w_in', 'conv_w', 'conv_b', 'dt_bias', 'a_log', 'd_ssd', 'ssd_norm_g', 's5_a_re', 's5_a_im', 's5_log_dt', 's5_b_re', 's5_b_im', 's5_c_re', 's5_c_im', 's5_d', 's5_glu_w', 's5_glu_b', 'w_branch', 'w_out', 'norm_mlp_g', 'w_mlp_in', 'w_mlp_out', 'norm_final_g', 'loss_target', 'm_norm_mix_g', 'm_w_in', 'm_conv_w', 'm_conv_b', 'm_dt_bias', 'm_a_log', 'm_d_ssd', 'm_ssd_norm_g', 'm_s5_a_re', 'm_s5_a_im', 'm_s5_log_dt', 'm_s5_b_re', 'm_s5_b_im', 'm_s5_c_re', 'm_s5_c_im', 'm_s5_d', 'm_s5_glu_w', 'm_s5_glu_b', 'm_w_branch', 'm_w_out', 'm_norm_mlp_g', 'm_w_mlp_in', 'm_w_mlp_out', 'm_norm_final_g', 'v_norm_mix_g', 'v_w_in', 'v_conv_w', 'v_conv_b', 'v_dt_bias', 'v_a_log', 'v_d_ssd', 'v_ssd_norm_g', 'v_s5_a_re', 'v_s5_a_im', 'v_s5_log_dt', 'v_s5_b_re', 'v_s5_b_im', 'v_s5_c_re', 'v_s5_c_im', 'v_s5_d', 'v_s5_glu_w', 'v_s5_glu_b', 'v_w_branch', 'v_w_out', 'v_norm_mlp_g', 'v_w_mlp_in', 'v_w_mlp_out', 'v_norm_final_g']
TWIN_OUTPUTS = ['loss', 'grad_x', 'grad_norm_mix_g', 'grad_w_in', 'grad_conv_w', 'grad_conv_b', 'grad_dt_bias', 'grad_a_log', 'grad_d_ssd', 'grad_ssd_norm_g', 'grad_s5_a_re', 'grad_s5_a_im', 'grad_s5_log_dt', 'grad_s5_b_re', 'grad_s5_b_im', 'grad_s5_c_re', 'grad_s5_c_im', 'grad_s5_d', 'grad_s5_glu_w', 'grad_s5_glu_b', 'grad_w_branch', 'grad_w_out', 'grad_norm_mlp_g', 'grad_w_mlp_in', 'grad_w_mlp_out', 'grad_norm_final_g', 'delta_norm_mix_g', 'delta_w_in', 'delta_conv_w', 'delta_conv_b', 'delta_dt_bias', 'delta_a_log', 'delta_d_ssd', 'delta_ssd_norm_g', 'delta_s5_a_re', 'delta_s5_a_im', 'delta_s5_log_dt', 'delta_s5_b_re', 'delta_s5_b_im', 'delta_s5_c_re', 'delta_s5_c_im', 'delta_s5_d', 'delta_s5_glu_w', 'delta_s5_glu_b', 'delta_w_branch', 'delta_w_out', 'delta_norm_mlp_g', 'delta_w_mlp_in', 'delta_w_mlp_out', 'delta_norm_final_g', 'new_m_norm_mix_g', 'new_m_w_in', 'new_m_conv_w', 'new_m_conv_b', 'new_m_dt_bias', 'new_m_a_log', 'new_m_d_ssd', 'new_m_ssd_norm_g', 'new_m_s5_a_re', 'new_m_s5_a_im', 'new_m_s5_log_dt', 'new_m_s5_b_re', 'new_m_s5_b_im', 'new_m_s5_c_re', 'new_m_s5_c_im', 'new_m_s5_d', 'new_m_s5_glu_w', 'new_m_s5_glu_b', 'new_m_w_branch', 'new_m_w_out', 'new_m_norm_mlp_g', 'new_m_w_mlp_in', 'new_m_w_mlp_out', 'new_m_norm_final_g', 'new_v_norm_mix_g', 'new_v_w_in', 'new_v_conv_w', 'new_v_conv_b', 'new_v_dt_bias', 'new_v_a_log', 'new_v_d_ssd', 'new_v_ssd_norm_g', 'new_v_s5_a_re', 'new_v_s5_a_im', 'new_v_s5_log_dt', 'new_v_s5_b_re', 'new_v_s5_b_im', 'new_v_s5_c_re', 'new_v_s5_c_im', 'new_v_s5_d', 'new_v_s5_glu_w', 'new_v_s5_glu_b', 'new_v_w_branch', 'new_v_w_out', 'new_v_norm_mlp_g', 'new_v_w_mlp_in', 'new_v_w_mlp_out', 'new_v_norm_final_g']
TWIN_LEAF_KINDS = {'loss': 'loss', 'grad_x': 'grad_x', 'grad_norm_mix_g': 'grad_w', 'grad_w_in': 'grad_w', 'grad_conv_w': 'grad_w', 'grad_conv_b': 'grad_w', 'grad_dt_bias': 'grad_w', 'grad_a_log': 'grad_w', 'grad_d_ssd': 'grad_w', 'grad_ssd_norm_g': 'grad_w', 'grad_s5_a_re': 'grad_w', 'grad_s5_a_im': 'grad_w', 'grad_s5_log_dt': 'grad_w', 'grad_s5_b_re': 'grad_w', 'grad_s5_b_im': 'grad_w', 'grad_s5_c_re': 'grad_w', 'grad_s5_c_im': 'grad_w', 'grad_s5_d': 'grad_w', 'grad_s5_glu_w': 'grad_w', 'grad_s5_glu_b': 'grad_w', 'grad_w_branch': 'grad_w', 'grad_w_out': 'grad_w', 'grad_norm_mlp_g': 'grad_w', 'grad_w_mlp_in': 'grad_w', 'grad_w_mlp_out': 'grad_w', 'grad_norm_final_g': 'grad_w', 'delta_norm_mix_g': 'delta_w', 'delta_w_in': 'delta_w', 'delta_conv_w': 'delta_w', 'delta_conv_b': 'delta_w', 'delta_dt_bias': 'delta_w', 'delta_a_log': 'delta_w', 'delta_d_ssd': 'delta_w', 'delta_ssd_norm_g': 'delta_w', 'delta_s5_a_re': 'delta_w', 'delta_s5_a_im': 'delta_w', 'delta_s5_log_dt': 'delta_w', 'delta_s5_b_re': 'delta_w', 'delta_s5_b_im': 'delta_w', 'delta_s5_c_re': 'delta_w', 'delta_s5_c_im': 'delta_w', 'delta_s5_d': 'delta_w', 'delta_s5_glu_w': 'delta_w', 'delta_s5_glu_b': 'delta_w', 'delta_w_branch': 'delta_w', 'delta_w_out': 'delta_w', 'delta_norm_mlp_g': 'delta_w', 'delta_w_mlp_in': 'delta_w', 'delta_w_mlp_out': 'delta_w', 'delta_norm_final_g': 'delta_w', 'new_m_norm_mix_g': 'new_m', 'new_m_w_in': 'new_m', 'new_m_conv_w': 'new_m', 'new_m_conv_b': 'new_m', 'new_m_dt_bias': 'new_m', 'new_m_a_log': 'new_m', 'new_m_d_ssd': 'new_m', 'new_m_ssd_norm_g': 'new_m', 'new_m_s5_a_re': 'new_m', 'new_m_s5_a_im': 'new_m', 'new_m_s5_log_dt': 'new_m', 'new_m_s5_b_re': 'new_m', 'new_m_s5_b_im': 'new_m', 'new_m_s5_c_re': 'new_m', 'new_m_s5_c_im': 'new_m', 'new_m_s5_d': 'new_m', 'new_m_s5_glu_w': 'new_m', 'new_m_s5_glu_b': 'new_m', 'new_m_w_branch': 'new_m', 'new_m_w_out': 'new_m', 'new_m_norm_mlp_g': 'new_m', 'new_m_w_mlp_in': 'new_m', 'new_m_w_mlp_out': 'new_m', 'new_m_norm_final_g': 'new_m', 'new_v_norm_mix_g': 'new_v', 'new_v_w_in': 'new_v', 'new_v_conv_w': 'new_v', 'new_v_conv_b': 'new_v', 'new_v_dt_bias': 'new_v', 'new_v_a_log': 'new_v', 'new_v_d_ssd': 'new_v', 'new_v_ssd_norm_g': 'new_v', 'new_v_s5_a_re': 'new_v', 'new_v_s5_a_im': 'new_v', 'new_v_s5_log_dt': 'new_v', 'new_v_s5_b_re': 'new_v', 'new_v_s5_b_im': 'new_v', 'new_v_s5_c_re': 'new_v', 'new_v_s5_c_im': 'new_v', 'new_v_s5_d': 'new_v', 'new_v_s5_glu_w': 'new_v', 'new_v_s5_glu_b': 'new_v', 'new_v_w_branch': 'new_v', 'new_v_w_out': 'new_v', 'new_v_norm_mlp_g': 'new_v', 'new_v_w_mlp_in': 'new_v', 'new_v_w_mlp_out': 'new_v', 'new_v_norm_final_g': 'new_v'}


def _forward(args):
    return _fwd_reference(*[args[k] for k in FWD_PARAMS])


def _output_shape():
    def fwd():
        inp = _fwd_setup_inputs(0)
        return _fwd_reference(*[inp[k] for k in FWD_PARAMS])
    out = _jax.eval_shape(fwd)
    return out.shape, out.dtype

N_MICROBATCH = 1
ADAM_LR = 0.001
ADAM_B1 = 0.9
ADAM_B2 = 0.999
ADAM_EPS = 1e-08
ADAM_WD = 0.01
ADAM_STEP = 10
PER_EXAMPLE_BATCH_AXIS = {'x': 0, 'loss_target': 0}
SHARED_INPUTS = []
_WEIGHT_DTYPES = {'norm_mix_g': _jnp.float32, 'w_in': _jnp.float32, 'conv_w': _jnp.float32, 'conv_b': _jnp.float32, 'dt_bias': _jnp.float32, 'a_log': _jnp.float32, 'd_ssd': _jnp.float32, 'ssd_norm_g': _jnp.float32, 's5_a_re': _jnp.float32, 's5_a_im': _jnp.float32, 's5_log_dt': _jnp.float32, 's5_b_re': _jnp.float32, 's5_b_im': _jnp.float32, 's5_c_re': _jnp.float32, 's5_c_im': _jnp.float32, 's5_d': _jnp.float32, 's5_glu_w': _jnp.float32, 's5_glu_b': _jnp.float32, 'w_branch': _jnp.float32, 'w_out': _jnp.float32, 'norm_mlp_g': _jnp.float32, 'w_mlp_in': _jnp.float32, 'w_mlp_out': _jnp.float32, 'norm_final_g': _jnp.float32}
MOMENT_SCALE = {'norm_mix_g': 2.130298e-01, 'w_in': 8.910580e-02, 'conv_w': 9.969706e-02, 'conv_b': 1.337491e-01, 'dt_bias': 4.270458e-01, 'a_log': 2.480197e-01, 'd_ssd': 4.956980e-01, 'ssd_norm_g': 1.265273e-01, 's5_a_re': 3.189943e-03, 's5_a_im': 3.430591e-03, 's5_log_dt': 2.581944e+00, 's5_b_re': 2.165405e-03, 's5_b_im': 2.150796e-03, 's5_c_re': 4.455072e-03, 's5_c_im': 4.552137e-03, 's5_d': 7.056688e-02, 's5_glu_w': 1.794128e-02, 's5_glu_b': 2.874163e-02, 'w_branch': 1.112163e-01, 'w_out': 1.408233e-01, 'norm_mlp_g': 2.114167e-01, 'w_mlp_in': 1.062856e-01, 'w_mlp_out': 2.346141e-01, 'norm_final_g': 6.464623e+01}


def _to_microbatches(a, axis):
    t = _jnp.moveaxis(a, axis, 0)
    t = t.reshape((N_MICROBATCH, t.shape[0] // N_MICROBATCH) + t.shape[1:])
    return _jnp.moveaxis(t, 1, axis + 1)


def setup_inputs(seed: int = 0) -> dict:
    inp = _fwd_setup_inputs(seed)
    key = _jax.random.fold_in(_jax.random.key(seed), 7919)
    shape, _ = _output_shape()
    out = dict(inp)
    out["loss_target"] = _jax.random.normal(_jax.random.fold_in(key, 0), shape, _jnp.float32)
    for i, name in enumerate(TWIN_WEIGHTS):
        w = inp[name].astype(_jnp.float32)
        if MOMENT_SCALE is None:
            s = _jnp.sqrt(_jnp.mean(_jnp.square(w)) + 1e-30)
        else:
            s = MOMENT_SCALE[name]
        km, kv = _jax.random.split(_jax.random.fold_in(key, i + 1))
        out[name] = w
        out["m_" + name] = s * _jax.random.normal(km, w.shape, _jnp.float32)
        out["v_" + name] = (s * s) * _jax.random.uniform(kv, w.shape, _jnp.float32, 0.5, 1.5)
    if N_MICROBATCH > 1:
        for name, axis in PER_EXAMPLE_BATCH_AXIS.items():
            out[name] = _to_microbatches(out[name], axis)
    return {'x': out['x'], 'norm_mix_g': out['norm_mix_g'], 'w_in': out['w_in'], 'conv_w': out['conv_w'], 'conv_b': out['conv_b'], 'dt_bias': out['dt_bias'], 'a_log': out['a_log'], 'd_ssd': out['d_ssd'], 'ssd_norm_g': out['ssd_norm_g'], 's5_a_re': out['s5_a_re'], 's5_a_im': out['s5_a_im'], 's5_log_dt': out['s5_log_dt'], 's5_b_re': out['s5_b_re'], 's5_b_im': out['s5_b_im'], 's5_c_re': out['s5_c_re'], 's5_c_im': out['s5_c_im'], 's5_d': out['s5_d'], 's5_glu_w': out['s5_glu_w'], 's5_glu_b': out['s5_glu_b'], 'w_branch': out['w_branch'], 'w_out': out['w_out'], 'norm_mlp_g': out['norm_mlp_g'], 'w_mlp_in': out['w_mlp_in'], 'w_mlp_out': out['w_mlp_out'], 'norm_final_g': out['norm_final_g'], 'loss_target': out['loss_target'], 'm_norm_mix_g': out['m_norm_mix_g'], 'm_w_in': out['m_w_in'], 'm_conv_w': out['m_conv_w'], 'm_conv_b': out['m_conv_b'], 'm_dt_bias': out['m_dt_bias'], 'm_a_log': out['m_a_log'], 'm_d_ssd': out['m_d_ssd'], 'm_ssd_norm_g': out['m_ssd_norm_g'], 'm_s5_a_re': out['m_s5_a_re'], 'm_s5_a_im': out['m_s5_a_im'], 'm_s5_log_dt': out['m_s5_log_dt'], 'm_s5_b_re': out['m_s5_b_re'], 'm_s5_b_im': out['m_s5_b_im'], 'm_s5_c_re': out['m_s5_c_re'], 'm_s5_c_im': out['m_s5_c_im'], 'm_s5_d': out['m_s5_d'], 'm_s5_glu_w': out['m_s5_glu_w'], 'm_s5_glu_b': out['m_s5_glu_b'], 'm_w_branch': out['m_w_branch'], 'm_w_out': out['m_w_out'], 'm_norm_mlp_g': out['m_norm_mlp_g'], 'm_w_mlp_in': out['m_w_mlp_in'], 'm_w_mlp_out': out['m_w_mlp_out'], 'm_norm_final_g': out['m_norm_final_g'], 'v_norm_mix_g': out['v_norm_mix_g'], 'v_w_in': out['v_w_in'], 'v_conv_w': out['v_conv_w'], 'v_conv_b': out['v_conv_b'], 'v_dt_bias': out['v_dt_bias'], 'v_a_log': out['v_a_log'], 'v_d_ssd': out['v_d_ssd'], 'v_ssd_norm_g': out['v_ssd_norm_g'], 'v_s5_a_re': out['v_s5_a_re'], 'v_s5_a_im': out['v_s5_a_im'], 'v_s5_log_dt': out['v_s5_log_dt'], 'v_s5_b_re': out['v_s5_b_re'], 'v_s5_b_im': out['v_s5_b_im'], 'v_s5_c_re': out['v_s5_c_re'], 'v_s5_c_im': out['v_s5_c_im'], 'v_s5_d': out['v_s5_d'], 'v_s5_glu_w': out['v_s5_glu_w'], 'v_s5_glu_b': out['v_s5_glu_b'], 'v_w_branch': out['v_w_branch'], 'v_w_out': out['v_w_out'], 'v_norm_mlp_g': out['v_norm_mlp_g'], 'v_w_mlp_in': out['v_w_mlp_in'], 'v_w_mlp_out': out['v_w_mlp_out'], 'v_norm_final_g': out['v_norm_final_g']}


def _loss(weights, diff, rest, loss_target):
    with _jax.named_scope("forward"):
        args = {**rest, TWIN_DIFF_INPUT: diff, **{k: w.astype(_WEIGHT_DTYPES[k]) for k, w in weights.items()}}
        y = _forward(args)
    with _jax.named_scope("loss_head"):
        err = _jnp.square(y.astype(_jnp.float32) - loss_target)
        return 0.5 * _jnp.sum(_jnp.mean(err, axis=-1)) if err.ndim else 0.5 * err


def _adamw(w, g, m, v):
    m = ADAM_B1 * m + (1.0 - ADAM_B1) * g
    v = ADAM_B2 * v + (1.0 - ADAM_B2) * _jnp.square(g)
    m_hat = m / (1.0 - ADAM_B1 ** ADAM_STEP)
    v_hat = v / (1.0 - ADAM_B2 ** ADAM_STEP)
    delta = -ADAM_LR * (m_hat / (_jnp.sqrt(v_hat) + ADAM_EPS) + ADAM_WD * w)
    return delta, m, v


def reference(x, norm_mix_g, w_in, conv_w, conv_b, dt_bias, a_log, d_ssd, ssd_norm_g, s5_a_re, s5_a_im, s5_log_dt, s5_b_re, s5_b_im, s5_c_re, s5_c_im, s5_d, s5_glu_w, s5_glu_b, w_branch, w_out, norm_mlp_g, w_mlp_in, w_mlp_out, norm_final_g, loss_target, m_norm_mix_g, m_w_in, m_conv_w, m_conv_b, m_dt_bias, m_a_log, m_d_ssd, m_ssd_norm_g, m_s5_a_re, m_s5_a_im, m_s5_log_dt, m_s5_b_re, m_s5_b_im, m_s5_c_re, m_s5_c_im, m_s5_d, m_s5_glu_w, m_s5_glu_b, m_w_branch, m_w_out, m_norm_mlp_g, m_w_mlp_in, m_w_mlp_out, m_norm_final_g, v_norm_mix_g, v_w_in, v_conv_w, v_conv_b, v_dt_bias, v_a_log, v_d_ssd, v_ssd_norm_g, v_s5_a_re, v_s5_a_im, v_s5_log_dt, v_s5_b_re, v_s5_b_im, v_s5_c_re, v_s5_c_im, v_s5_d, v_s5_glu_w, v_s5_glu_b, v_w_branch, v_w_out, v_norm_mlp_g, v_w_mlp_in, v_w_mlp_out, v_norm_final_g):
    given = dict(x=x, norm_mix_g=norm_mix_g, w_in=w_in, conv_w=conv_w, conv_b=conv_b, dt_bias=dt_bias, a_log=a_log, d_ssd=d_ssd, ssd_norm_g=ssd_norm_g, s5_a_re=s5_a_re, s5_a_im=s5_a_im, s5_log_dt=s5_log_dt, s5_b_re=s5_b_re, s5_b_im=s5_b_im, s5_c_re=s5_c_re, s5_c_im=s5_c_im, s5_d=s5_d, s5_glu_w=s5_glu_w, s5_glu_b=s5_glu_b, w_branch=w_branch, w_out=w_out, norm_mlp_g=norm_mlp_g, w_mlp_in=w_mlp_in, w_mlp_out=w_mlp_out, norm_final_g=norm_final_g, loss_target=loss_target, m_norm_mix_g=m_norm_mix_g, m_w_in=m_w_in, m_conv_w=m_conv_w, m_conv_b=m_conv_b, m_dt_bias=m_dt_bias, m_a_log=m_a_log, m_d_ssd=m_d_ssd, m_ssd_norm_g=m_ssd_norm_g, m_s5_a_re=m_s5_a_re, m_s5_a_im=m_s5_a_im, m_s5_log_dt=m_s5_log_dt, m_s5_b_re=m_s5_b_re, m_s5_b_im=m_s5_b_im, m_s5_c_re=m_s5_c_re, m_s5_c_im=m_s5_c_im, m_s5_d=m_s5_d, m_s5_glu_w=m_s5_glu_w, m_s5_glu_b=m_s5_glu_b, m_w_branch=m_w_branch, m_w_out=m_w_out, m_norm_mlp_g=m_norm_mlp_g, m_w_mlp_in=m_w_mlp_in, m_w_mlp_out=m_w_mlp_out, m_norm_final_g=m_norm_final_g, v_norm_mix_g=v_norm_mix_g, v_w_in=v_w_in, v_conv_w=v_conv_w, v_conv_b=v_conv_b, v_dt_bias=v_dt_bias, v_a_log=v_a_log, v_d_ssd=v_d_ssd, v_ssd_norm_g=v_ssd_norm_g, v_s5_a_re=v_s5_a_re, v_s5_a_im=v_s5_a_im, v_s5_log_dt=v_s5_log_dt, v_s5_b_re=v_s5_b_re, v_s5_b_im=v_s5_b_im, v_s5_c_re=v_s5_c_re, v_s5_c_im=v_s5_c_im, v_s5_d=v_s5_d, v_s5_glu_w=v_s5_glu_w, v_s5_glu_b=v_s5_glu_b, v_w_branch=v_w_branch, v_w_out=v_w_out, v_norm_mlp_g=v_norm_mlp_g, v_w_mlp_in=v_w_mlp_in, v_w_mlp_out=v_w_mlp_out, v_norm_final_g=v_norm_final_g)
    weights = {n: given[n] for n in TWIN_WEIGHTS}
    shared = {n: given[n] for n in SHARED_INPUTS}
    per_example = {n: given[n] for n in ['x']}
    grad_fn = _jax.value_and_grad(_loss, argnums=(0, 1))

    def one_microbatch(ex, loss_target):
        ex = dict(ex)
        diff = ex.pop(TWIN_DIFF_INPUT)
        return grad_fn(weights, diff, {**shared, **ex}, loss_target)

    if N_MICROBATCH == 1:
        loss, (grad_w, grad_x) = one_microbatch(per_example, given["loss_target"])
    else:
        def body(carry, xs):
            loss_sum, grad_sum = carry
            l_k, (gw_k, gx_k) = one_microbatch(xs[0], xs[1])
            with _jax.named_scope("update"):
                return (loss_sum + l_k, _jax.tree.map(_jnp.add, grad_sum, gw_k)), gx_k

        init = (_jnp.zeros((), _jnp.float32), _jax.tree.map(_jnp.zeros_like, weights))
        (loss, grad_w), grad_x = _jax.lax.scan(body, init, (per_example, given["loss_target"]))
    with _jax.named_scope("update"):
        delta_w, new_m, new_v = {}, {}, {}
        for n in TWIN_WEIGHTS:
            delta_w[n], new_m[n], new_v[n] = _adamw(weights[n], grad_w[n], given["m_" + n], given["v_" + n])
    return (loss, grad_x, *[grad_w[n] for n in TWIN_WEIGHTS], *[delta_w[n] for n in TWIN_WEIGHTS],
            *[new_m[n] for n in TWIN_WEIGHTS], *[new_v[n] for n in TWIN_WEIGHTS])
```

```python
import functools
import math

import jax
import jax.numpy as jnp
from jax import lax
from jax.experimental import pallas as pl
from jax.experimental.pallas import tpu as pltpu

F32 = jnp.float32
BF16 = jnp.bfloat16
HI = lax.Precision.HIGHEST

N_DEV = 8
D_MODEL = 1024
SSD_HEADS = 16
SSD_HEADDIM = 64
SSD_GROUPS = 4
SSD_HPG = 4
SSD_STATE = 128
SSD_CHUNK = 128
SSD_CONV = 4
CONV_DIM = 2048
S5_WIDTH = 512
S5_GROUP = 16
S5_GROUPS = 32
S5_STATE = 64
S5_N = S5_GROUPS * S5_STATE
D_FF = 4096
D_IN_PROJ = 5648
IN_SPLITS = (2048, 2048, 1024, 512, 128)
D_IN_PAD = sum(IN_SPLITS)
EPS = 1e-6
LANES = 128
SUBLANES = 8
VMEM_LIMIT = 56 * 1024 * 1024

ADAM_LR = 0.001
ADAM_B1 = 0.9
ADAM_B2 = 0.999
ADAM_EPS = 1e-08
ADAM_WD = 0.01
ADAM_STEP = 10

GELU_C = math.sqrt(2.0 / math.pi)
GELU_K = 0.044715


def _params(*sem):
    return pltpu.CompilerParams(dimension_semantics=sem, vmem_limit_bytes=VMEM_LIMIT)


def _full(shape):
    nd = len(shape)
    return pl.BlockSpec(shape, lambda *_: (0,) * nd)


def _dot(a, b):
    return jnp.dot(a, b, preferred_element_type=F32)


def _dot_nt(a, b):
    return lax.dot_general(a, b, (((1,), (1,)), ((), ())), preferred_element_type=F32)


def _dot_tn(a, b):
    return lax.dot_general(a, b, (((0,), (0,)), ((), ())), preferred_element_type=F32)


def _dot_hi(a, b):
    return jnp.dot(a, b, preferred_element_type=F32, precision=HI)


def _dot_nt_hi(a, b):
    return lax.dot_general(a, b, (((1,), (1,)), ((), ())), preferred_element_type=F32, precision=HI)


def _dot_tn_hi(a, b):
    return lax.dot_general(a, b, (((0,), (0,)), ((), ())), preferred_element_type=F32, precision=HI)


def _sigmoid(x):
    return 1.0 / (1.0 + jnp.exp(-x))


def _softplus(x):
    return jnp.maximum(x, 0.0) + jnp.log(1.0 + jnp.exp(-jnp.abs(x)))


def _gelu(x):
    return 0.5 * x * (1.0 + jnp.tanh(GELU_C * (x + GELU_K * x * x * x)))


def _gelu_grad(x):
    th = jnp.tanh(GELU_C * (x + GELU_K * x * x * x))
    return 0.5 * (1.0 + th) + 0.5 * x * (1.0 - th * th) * GELU_C * (1.0 + 3.0 * GELU_K * x * x)


def rms_matmul(x, g, w, splits, *, tm, name):
    t, d = x.shape
    n = w.shape[1]
    assert sum(splits) == n

    def body(x_ref, g_ref, w_ref, h_ref, *o_refs):
        xv = x_ref[...]
        r = lax.rsqrt(jnp.mean(xv * xv, axis=-1, keepdims=True) + EPS)
        h = (xv * r * g_ref[...]).astype(BF16)
        h_ref[...] = h
        off = 0
        for o_ref, width in zip(o_refs, splits):
            o_ref[...] = _dot(h, w_ref[:, off:off + width])
            off += width

    tile = lambda c: pl.BlockSpec((tm, c), lambda i: (i, 0))
    return pl.pallas_call(
        body, name=name, grid=(t // tm,),
        in_specs=[tile(d), _full((1, d)), _full((d, n))],
        out_specs=[tile(d)] + [tile(c) for c in splits],
        out_shape=[jax.ShapeDtypeStruct((t, d), BF16)] + [jax.ShapeDtypeStruct((t, c), F32) for c in splits],
        compiler_params=_params("parallel"),
    )(x, g.reshape(1, d), w)


def _pick(n, cap):
    best = LANES
    for c in range(LANES, cap + 1, LANES):
        if n % c == 0:
            best = c
    return best


def matmul_tn(a, b, *, name, tk=1024):
    t, m = a.shape
    n = b.shape[1]
    tk = min(tk, t)
    tm = _pick(m, 1024)
    tn = _pick(n, 1280)
    nk = t // tk

    def body(a_ref, b_ref, o_ref):
        k = pl.program_id(2)

        @pl.when(k == 0)
        def _():
            o_ref[...] = jnp.zeros_like(o_ref)

        o_ref[...] += _dot_tn(a_ref[...], b_ref[...])

    return pl.pallas_call(
        body, name=name, grid=(m // tm, n // tn, nk),
        in_specs=[pl.BlockSpec((tk, tm), lambda i, j, k: (k, i)), pl.BlockSpec((tk, tn), lambda i, j, k: (k, j))],
        out_specs=pl.BlockSpec((tm, tn), lambda i, j, k: (i, j)),
        out_shape=jax.ShapeDtypeStruct((m, n), F32),
        compiler_params=_params("parallel", "parallel", "arbitrary"),
    )(a, b)


def matmul_nt_rms_bwd(dy, w, x, g, dres, *, tm, name):
    t, n = dy.shape
    d = w.shape[0]

    def body(dy_ref, w_ref, x_ref, g_ref, dres_ref, dx_ref, dxb_ref, dg_ref):
        i = pl.program_id(0)
        dh = _dot_nt(dy_ref[...], w_ref[...])
        xv = x_ref[...]
        r = lax.rsqrt(jnp.mean(xv * xv, axis=-1, keepdims=True) + EPS)
        xn = xv * r
        dxn = dh * g_ref[...]
        dx = dres_ref[...] + r * (dxn - xn * jnp.mean(dxn * xn, axis=-1, keepdims=True))
        dx_ref[...] = dx
        dxb_ref[...] = dx.astype(BF16)

        @pl.when(i == 0)
        def _():
            dg_ref[...] = jnp.zeros_like(dg_ref)

        dg_ref[...] += jnp.sum(dh * xn, axis=0, keepdims=True)

    return pl.pallas_call(
        body, name=name, grid=(t // tm,),
        in_specs=[pl.BlockSpec((tm, n), lambda i: (i, 0)), _full((d, n)), pl.BlockSpec((tm, d), lambda i: (i, 0)),
                  _full((1, d)), pl.BlockSpec((tm, d), lambda i: (i, 0))],
        out_specs=[pl.BlockSpec((tm, d), lambda i: (i, 0)), pl.BlockSpec((tm, d), lambda i: (i, 0)), _full((1, d))],
        out_shape=[jax.ShapeDtypeStruct((t, d), F32), jax.ShapeDtypeStruct((t, d), BF16),
                   jax.ShapeDtypeStruct((1, d), F32)],
        compiler_params=_params("arbitrary"),
    )(dy, w, x, g.reshape(1, d), dres)


def _conv_taps(prev8, cur, w_ref, tm):
    ext = jnp.concatenate([prev8, cur], axis=0)
    acc = cur * w_ref[3:4, :]
    for k in range(3):
        off = SUBLANES - 3 + k
        acc = acc + ext[off:off + tm, :] * w_ref[k:k + 1, :]
    return acc


def conv_silu_fwd(u, w8, b, *, tm, name):
    t, c = u.shape
    per = tm // SUBLANES

    def body(prev_ref, cur_ref, w_ref, b_ref, o_ref):
        i = pl.program_id(0)
        prev8 = jnp.where(i == 0, 0.0, prev_ref[...])
        xc = _conv_taps(prev8, cur_ref[...], w_ref, tm) + b_ref[...]
        o_ref[...] = xc * _sigmoid(xc)

    return pl.pallas_call(
        body, name=name, grid=(t // tm,),
        in_specs=[pl.BlockSpec((SUBLANES, c), lambda i: (jnp.maximum(i * per - 1, 0), 0)),
                  pl.BlockSpec((tm, c), lambda i: (i, 0)), _full((SUBLANES, c)), _full((1, c))],
        out_specs=pl.BlockSpec((tm, c), lambda i: (i, 0)),
        out_shape=jax.ShapeDtypeStruct((t, c), F32),
        compiler_params=_params("parallel"),
    )(u, u, w8, b.reshape(1, c))


def conv_silu_bwd_a(u, dy, w8, b, *, tm, name):
    t, c = u.shape
    per = tm // SUBLANES

    def body(prev_ref, cur_ref, dy_ref, w_ref, b_ref, dxc_ref, db_ref, dw_ref):
        i = pl.program_id(0)
        prev8 = jnp.where(i == 0, 0.0, prev_ref[...])
        cur = cur_ref[...]
        xc = _conv_taps(prev8, cur, w_ref, tm) + b_ref[...]
        sg = _sigmoid(xc)
        dxc = dy_ref[...] * (sg * (1.0 + xc * (1.0 - sg)))
        dxc_ref[...] = dxc

        @pl.when(i == 0)
        def _():
            db_ref[...] = jnp.zeros_like(db_ref)
            dw_ref[...] = jnp.zeros_like(dw_ref)

        db_ref[...] += jnp.sum(dxc, axis=0, keepdims=True)
        ext = jnp.concatenate([prev8, cur], axis=0)
        rows = []
        for k in range(SSD_CONV):
            off = SUBLANES - 3 + k
            rows.append(jnp.sum(dxc * ext[off:off + tm, :], axis=0, keepdims=True))
        rows.append(jnp.zeros((SUBLANES - SSD_CONV, c), F32))
        dw_ref[...] += jnp.concatenate(rows, axis=0)

    return pl.pallas_call(
        body, name=name, grid=(t // tm,),
        in_specs=[pl.BlockSpec((SUBLANES, c), lambda i: (jnp.maximum(i * per - 1, 0), 0)),
                  pl.BlockSpec((tm, c), lambda i: (i, 0)), pl.BlockSpec((tm, c), lambda i: (i, 0)),
                  _full((SUBLANES, c)), _full((1, c))],
        out_specs=[pl.BlockSpec((tm, c), lambda i: (i, 0)), _full((1, c)), _full((SUBLANES, c))],
        out_shape=[jax.ShapeDtypeStruct((t, c), F32), jax.ShapeDtypeStruct((1, c), F32),
                   jax.ShapeDtypeStruct((SUBLANES, c), F32)],
        compiler_params=_params("arbitrary"),
    )(u, u, dy, w8, b.reshape(1, c))


def conv_bwd_b(dxc, w8, *, tm, name):
    t, c = dxc.shape
    per = tm // SUBLANES
    last = t // SUBLANES - 1
    nt = t // tm

    def body(cur_ref, next_ref, w_ref, o_ref):
        i = pl.program_id(0)
        nxt = jnp.where(i == nt - 1, 0.0, next_ref[...])
        cur = cur_ref[...]
        ext = jnp.concatenate([cur, nxt], axis=0)
        acc = cur * w_ref[3:4, :]
        for j in range(1, SSD_CONV):
            acc = acc + ext[j:j + tm, :] * w_ref[3 - j:4 - j, :]
        o_ref[...] = acc.astype(BF16)

    return pl.pallas_call(
        body, name=name, grid=(nt,),
        in_specs=[pl.BlockSpec((tm, c), lambda i: (i, 0)),
                  pl.BlockSpec((SUBLANES, c), lambda i: (jnp.minimum((i + 1) * per, last), 0)), _full((SUBLANES, c))],
        out_specs=pl.BlockSpec((tm, c), lambda i: (i, 0)),
        out_shape=jax.ShapeDtypeStruct((t, c), BF16),
        compiler_params=_params("parallel"),
    )(dxc, dxc, w8)


def _ssd_chunk_terms(dtr, bias_row, alog_row):
    q = SSD_CHUNK
    row = lax.broadcasted_iota(jnp.int32, (q, q), 0)
    col = lax.broadcasted_iota(jnp.int32, (q, q), 1)
    ltri = (col <= row).astype(F32)
    dt = _softplus(dtr + bias_row)
    a_row = -jnp.exp(alog_row)
    la = dt * a_row
    cum = _dot_hi(ltri, la)
    cum_t = _dot_tn_hi(la, (row <= col).astype(F32))
    return dt, a_row, cum, cum_t, ltri, row, col


def ssd_fwd(xbc, dtr, z, bias_row, alog_row, d_row, norm_g, *, name):
    t = xbc.shape[0]
    q, p, n = SSD_CHUNK, SSD_HEADDIM, SSD_STATE
    nc = t // q
    di = SSD_HEADS * p
    gw = di // SSD_GROUPS

    def body(xs_ref, b_ref, c_ref, dtr_ref, z_ref, bias_ref, alog_ref, d_ref, g_ref,
             ya_ref, yssd_ref, st_ref, state):
        ci = pl.program_id(0)

        @pl.when(ci == 0)
        def _():
            state[...] = jnp.zeros_like(state)

        st_ref[0] = state[...]
        dt, _, cum, cum_t, _, row, col = _ssd_chunk_terms(dtr_ref[...], bias_ref[...], alog_ref[...])
        causal = row >= col
        for g in range(SSD_GROUPS):
            bg = b_ref[:, g * n:(g + 1) * n]
            cg = c_ref[:, g * n:(g + 1) * n]
            cb = _dot_nt_hi(cg, bg)
            for r in range(SSD_HPG):
                h = g * SSD_HPG + r
                hs = slice(h * p, (h + 1) * p)
                cum_h = cum[:, h:h + 1]
                seg = cum_h - cum_t[h:h + 1, :]
                lmat = jnp.where(causal, jnp.exp(jnp.where(causal, seg, 0.0)), 0.0)
                xs_h = xs_ref[:, hs]
                xdt = xs_h * dt[:, h:h + 1]
                sprev = state[hs, :]
                cum_last = cum[q - 1:q, h:h + 1]
                y = _dot_hi(cb * lmat, xdt)
                y = y + jnp.exp(cum_h) * _dot_nt_hi(cg, sprev)
                yssd_ref[:, hs] = y + d_ref[:, h:h + 1] * xs_h
                xd = xdt * jnp.exp(cum_last - cum_h)
                state[hs, :] = jnp.exp(cum_last) * sprev + _dot_tn_hi(xd, bg)
        zv = z_ref[...]
        ya1 = yssd_ref[...] * (zv * _sigmoid(zv))
        for g in range(SSD_GROUPS):
            gsl = slice(g * gw, (g + 1) * gw)
            sl = ya1[:, gsl]
            rg = lax.rsqrt(jnp.mean(sl * sl, axis=-1, keepdims=True) + EPS)
            ya_ref[:, gsl] = (sl * rg * g_ref[:, gsl]).astype(BF16)

    blk = lambda w, j: pl.BlockSpec((q, w), lambda c: (c, j))
    return pl.pallas_call(
        body, name=name, grid=(nc,),
        in_specs=[blk(di, 0), blk(512, 2), blk(512, 3), blk(LANES, 0), blk(di, 0),
                  _full((1, LANES)), _full((1, LANES)), _full((1, LANES)), _full((1, di))],
        out_specs=[blk(di, 0), blk(di, 0), pl.BlockSpec((1, di, n), lambda c: (c, 0, 0))],
        out_shape=[jax.ShapeDtypeStruct((t, di), BF16), jax.ShapeDtypeStruct((t, di), F32),
                   jax.ShapeDtypeStruct((nc, di, n), F32)],
        scratch_shapes=[pltpu.VMEM((di, n), F32)],
        compiler_params=_params("arbitrary"),
    )(xbc, xbc, xbc, dtr, z, bias_row, alog_row, d_row, norm_g.reshape(1, di))


def ssd_bwd(dya, yssd, z, xbc, dtr, states, bias_row, alog_row, d_row, norm_g, *, name):
    t = xbc.shape[0]
    q, p, n = SSD_CHUNK, SSD_HEADDIM, SSD_STATE
    nc = t // q
    di = SSD_HEADS * p
    gw = di // SSD_GROUPS

    def body(dya_ref, yssd_ref, z_ref, xs_ref, b_ref, c_ref, dtr_ref, st_ref, bias_ref, alog_ref, d_ref, g_ref,
             dz_ref, dxbc_ref, ddtr_ref, dng_ref, dd_ref, dalog_ref, dbias_ref, dstate, dyssd):
        ci = pl.program_id(0)

        @pl.when(ci == 0)
        def _():
            dstate[...] = jnp.zeros_like(dstate)
            dng_ref[...] = jnp.zeros_like(dng_ref)
            dd_ref[...] = jnp.zeros_like(dd_ref)
            dalog_ref[...] = jnp.zeros_like(dalog_ref)
            dbias_ref[...] = jnp.zeros_like(dbias_ref)

        zv = z_ref[...]
        sz = _sigmoid(zv)
        silu_z = zv * sz
        yv = yssd_ref[...]
        ya1 = yv * silu_z
        dyav = dya_ref[...]
        for g in range(SSD_GROUPS):
            gsl = slice(g * gw, (g + 1) * gw)
            sl = ya1[:, gsl]
            rg = lax.rsqrt(jnp.mean(sl * sl, axis=-1, keepdims=True) + EPS)
            ya2 = sl * rg
            dy_g = dyav[:, gsl]
            dng_ref[:, gsl] += jnp.sum(dy_g * ya2, axis=0, keepdims=True)
            dya2 = dy_g * g_ref[:, gsl]
            dya1 = rg * (dya2 - ya2 * jnp.mean(dya2 * ya2, axis=-1, keepdims=True))
            dyssd[:, gsl] = dya1 * silu_z[:, gsl]
            dz_ref[:, gsl] = (dya1 * yv[:, gsl] * (sz[:, gsl] * (1.0 + zv[:, gsl] * (1.0 - sz[:, gsl])))).astype(BF16)

        dtr_v = dtr_ref[...]
        dt, a_row, cum, cum_t, ltri, row, col = _ssd_chunk_terms(dtr_v, bias_ref[...], alog_ref[...])
        causal = row >= col
        lane = lax.broadcasted_iota(jnp.int32, (1, LANES), 1)
        is_last = lax.broadcasted_iota(jnp.int32, (q, 1), 0) == q - 1
        ones = jnp.ones((q, q), F32)
        dcum_mat = jnp.zeros((q, LANES), F32)
        ddt_mat = jnp.zeros((q, LANES), F32)
        dd_row = jnp.zeros((1, LANES), F32)
        for g in range(SSD_GROUPS):
            bg = b_ref[:, g * n:(g + 1) * n]
            cg = c_ref[:, g * n:(g + 1) * n]
            cb = _dot_nt_hi(cg, bg)
            dcb = jnp.zeros((q, q), F32)
            dbg = jnp.zeros((q, n), F32)
            dcg = jnp.zeros((q, n), F32)
            for r in range(SSD_HPG):
                h = g * SSD_HPG + r
                hs = slice(h * p, (h + 1) * p)
                onehot = (lane == h).astype(F32)
                cum_h = cum[:, h:h + 1]
                seg = cum_h - cum_t[h:h + 1, :]
                lmat = jnp.where(causal, jnp.exp(jnp.where(causal, seg, 0.0)), 0.0)
                m = cb * lmat
                xs_h = xs_ref[:, hs]
                dt_h = dt[:, h:h + 1]
                xdt = xs_h * dt_h
                dy = dyssd[:, hs]
                sprev = st_ref[0, hs, :]
                dsn = dstate[hs, :]
                cum_last = cum[q - 1:q, h:h + 1]
                e = jnp.exp(cum_h)
                e_end = jnp.exp(cum_last)
                d_end = jnp.exp(cum_last - cum_h)

                dd_row = dd_row + onehot * jnp.sum(dy * xs_h)
                dm = _dot_nt_hi(dy, xdt)
                dx = _dot_tn_hi(m, dy)
                w = dm * m
                dcum = jnp.sum(w, axis=1, keepdims=True) - _dot_tn_hi(w, ones)[:, 0:1]
                dcb = dcb + dm * lmat
                gmat = _dot_nt_hi(cg, sprev)
                dcum = dcum + jnp.sum(dy * gmat, axis=1, keepdims=True) * e
                dg = dy * e
                dcg = dcg + _dot_hi(dg, sprev)
                dsp = _dot_tn_hi(dg, cg) + e_end * dsn
                dlast = jnp.sum(dsn * sprev) * e_end
                bds = _dot_nt_hi(bg, dsn)
                dx = dx + d_end * bds
                dbg = dbg + d_end * _dot_hi(xdt, dsn)
                qv = jnp.sum(xdt * bds, axis=1, keepdims=True) * d_end
                dcum = dcum - qv
                dlast = dlast + jnp.sum(qv)
                dcum = dcum + jnp.where(is_last, dlast, 0.0)
                dcum_mat = dcum_mat + dcum * onehot
                ddt_mat = ddt_mat + jnp.sum(dx * xs_h, axis=1, keepdims=True) * onehot
                dxbc_ref[:, hs] = d_ref[:, h:h + 1] * dy + dx * dt_h
                dstate[hs, :] = dsp
            dcg = dcg + _dot_hi(dcb, bg)
            dbg = dbg + _dot_tn_hi(dcb, cg)
            dxbc_ref[:, di + g * n:di + (g + 1) * n] = dbg
            dxbc_ref[:, di + SSD_GROUPS * n + g * n:di + SSD_GROUPS * n + (g + 1) * n] = dcg
        dla = _dot_tn_hi(ltri, dcum_mat)
        ddt_mat = ddt_mat + dla * a_row
        ddtr = ddt_mat * _sigmoid(dtr_v + bias_ref[...])
        ddtr_ref[...] = ddtr.astype(BF16)
        dd_ref[...] += dd_row
        dalog_ref[...] += jnp.sum(dla * dt, axis=0, keepdims=True) * a_row
        dbias_ref[...] += jnp.sum(ddtr, axis=0, keepdims=True)

    blk = lambda w, j: pl.BlockSpec((q, w), lambda c: (nc - 1 - c, j))
    row_out = lambda w: jax.ShapeDtypeStruct((1, w), F32)
    return pl.pallas_call(
        body, name=name, grid=(nc,),
        in_specs=[blk(di, 0), blk(di, 0), blk(di, 0), blk(di, 0), blk(512, 2), blk(512, 3), blk(LANES, 0),
                  pl.BlockSpec((1, di, n), lambda c: (nc - 1 - c, 0, 0)),
                  _full((1, LANES)), _full((1, LANES)), _full((1, LANES)), _full((1, di))],
        out_specs=[blk(di, 0), blk(CONV_DIM, 0), blk(LANES, 0),
                   _full((1, di)), _full((1, LANES)), _full((1, LANES)), _full((1, LANES))],
        out_shape=[jax.ShapeDtypeStruct((t, di), BF16), jax.ShapeDtypeStruct((t, CONV_DIM), F32),
                   jax.ShapeDtypeStruct((t, LANES), BF16), row_out(di), row_out(LANES), row_out(LANES), row_out(LANES)],
        scratch_shapes=[pltpu.VMEM((di, n), F32), pltpu.VMEM((q, di), F32)],
        compiler_params=_params("arbitrary"),
    )(dya, yssd, z, xbc, xbc, xbc, dtr, states, bias_row, alog_row, d_row, norm_g.reshape(1, di))


S5_BLOCKS = 4
S5_BC = S5_WIDTH // S5_BLOCKS
S5_BS = S5_N // S5_BLOCKS


def _s5_tables(ar, ai, reverse):
    if reverse:
        ai = -ai
    pows = [(ar, ai)]
    for _ in range(SUBLANES - 1):
        pr, pi = pows[-1]
        pows.append((pr * ar - pi * ai, pr * ai + pi * ar))
    row = lax.broadcasted_iota(jnp.int32, (SUBLANES, ar.shape[1]), 0)
    out = []
    for k in (1, 2, 4):
        pr, pi = pows[k - 1]
        mask = (row < SUBLANES - k) if reverse else (row >= k)
        out += [jnp.where(mask, pr, 0.0), jnp.where(mask, pi, 0.0)]
    order = list(range(SUBLANES))
    if reverse:
        order = order[::-1]
    out.append(jnp.concatenate([pows[e][0] for e in order], axis=0))
    out.append(jnp.concatenate([pows[e][1] for e in order], axis=0))
    return out


def _s5_scan(s_scr, tab, carry, tm, reverse):
    ngrp = tm // SUBLANES
    for lc in range(S5_BLOCKS):
        lr = slice(lc * S5_BS, (lc + 1) * S5_BS)
        li = slice(S5_N + lc * S5_BS, S5_N + (lc + 1) * S5_BS)
        tb = [tab[i, :, lr] for i in range(8)]

        def step(i, c, lr=lr, li=li, tb=tb):
            grp = (ngrp - 1 - i) if reverse else i
            r0 = pl.multiple_of(grp * SUBLANES, SUBLANES)
            xr = s_scr[pl.ds(r0, SUBLANES), lr]
            xi = s_scr[pl.ds(r0, SUBLANES), li]
            for lvl, k in enumerate((1, 2, 4)):
                sh = (SUBLANES - k) if reverse else k
                pr, pi = tb[2 * lvl], tb[2 * lvl + 1]
                rr = pltpu.roll(xr, sh, 0)
                ri = pltpu.roll(xi, sh, 0)
                xr, xi = xr + pr * rr - pi * ri, xi + pr * ri + pi * rr
            cr, ci = c
            xr, xi = xr + tb[6] * cr - tb[7] * ci, xi + tb[6] * ci + tb[7] * cr
            s_scr[pl.ds(r0, SUBLANES), lr] = xr
            s_scr[pl.ds(r0, SUBLANES), li] = xi
            e = 0 if reverse else SUBLANES - 1
            return xr[e:e + 1, :], xi[e:e + 1, :]

        cf = lax.fori_loop(0, ngrp, step, (carry[0:1, lr], carry[0:1, li]))
        carry[0:1, lr] = cf[0]
        carry[0:1, li] = cf[1]


def s5_fwd(u5, bbc, ccc, ab8, d_row, wglu, bglu_row, *, tm, name):
    t, w = u5.shape

    def body(u_ref, bb_ref, cc_ref, ab_ref, d_ref, wg_ref, bg_ref, s_ref, ys_ref, yb_ref, s_scr, tab, carry):
        i = pl.program_id(0)

        @pl.when(i == 0)
        def _():
            carry[...] = jnp.zeros_like(carry)
            for k, tv in enumerate(_s5_tables(ab_ref[0:1, :], ab_ref[1:2, :], False)):
                tab[k] = tv

        u = u_ref[...]
        ub = u.astype(BF16)
        for j in range(S5_BLOCKS):
            uj = ub[:, j * S5_BC:(j + 1) * S5_BC]
            bj = bb_ref[j * S5_BC:(j + 1) * S5_BC, :]
            s_scr[:, j * S5_BS:(j + 1) * S5_BS] = _dot(uj, bj[:, :S5_BS])
            s_scr[:, S5_N + j * S5_BS:S5_N + (j + 1) * S5_BS] = _dot(uj, bj[:, S5_BS:])
        _s5_scan(s_scr, tab, carry, tm, False)
        s = s_scr[...]
        s_ref[...] = s
        sb = s.astype(BF16)
        ys = []
        for j in range(S5_BLOCKS):
            cj = cc_ref[:, j * S5_BC:(j + 1) * S5_BC]
            ys.append(_dot(sb[:, j * S5_BS:(j + 1) * S5_BS], cj[:S5_BS, :])
                      + _dot(sb[:, S5_N + j * S5_BS:S5_N + (j + 1) * S5_BS], cj[S5_BS:, :]))
        y = jnp.concatenate(ys, axis=1) + d_ref[...] * u
        ys_ref[...] = y
        yb1 = _gelu(y)
        tt = _dot(yb1.astype(BF16), wg_ref[...]) + bg_ref[...]
        yb_ref[...] = (yb1 * _sigmoid(tt)).astype(BF16)

    tile = lambda c: pl.BlockSpec((tm, c), lambda i: (i, 0))
    return pl.pallas_call(
        body, name=name, grid=(t // tm,),
        in_specs=[tile(w), _full((w, 2 * S5_BS)), _full((2 * S5_BS, w)), _full((SUBLANES, S5_N)), _full((1, w)),
                  _full((w, w)), _full((1, w))],
        out_specs=[tile(2 * S5_N), tile(w), tile(w)],
        out_shape=[jax.ShapeDtypeStruct((t, 2 * S5_N), F32), jax.ShapeDtypeStruct((t, w), F32),
                   jax.ShapeDtypeStruct((t, w), BF16)],
        scratch_shapes=[pltpu.VMEM((tm, 2 * S5_N), F32), pltpu.VMEM((8, SUBLANES, S5_N), F32),
                        pltpu.VMEM((SUBLANES, 2 * S5_N), F32)],
        compiler_params=_params("arbitrary"),
    )(u5, bbc, ccc, ab8, d_row, wglu, bglu_row)


def s5_bwd(dyb, ys5, u5, s, bbc, ccc, ab8, d_row, wglu, bglu_row, *, tm, name):
    t, w = u5.shape
    nt = t // tm
    per = tm // SUBLANES

    def body(dyb_ref, ys_ref, u_ref, s_ref, sprev_ref, bb_ref, cc_ref, ab_ref, d_ref, wg_ref, bg_ref,
             du_ref, dbb_ref, dcc_ref, dab_ref, dd_ref, dwg_ref, dbg_ref, g_scr, tab, carry):
        i = pl.program_id(0)

        @pl.when(i == 0)
        def _():
            carry[...] = jnp.zeros_like(carry)
            for k, tv in enumerate(_s5_tables(ab_ref[0:1, :], ab_ref[1:2, :], True)):
                tab[k] = tv
            for r in (dbb_ref, dcc_ref, dab_ref, dd_ref, dwg_ref, dbg_ref):
                r[...] = jnp.zeros_like(r)

        y = ys_ref[...]
        u = u_ref[...]
        yb1 = _gelu(y)
        yb1b = yb1.astype(BF16)
        sg = _sigmoid(_dot(yb1b, wg_ref[...]) + bg_ref[...])
        dybv = dyb_ref[...]
        dtt = dybv * yb1 * sg * (1.0 - sg)
        dttb = dtt.astype(BF16)
        dyb1 = dybv * sg + _dot_nt(dttb, wg_ref[...])
        dwg_ref[...] += _dot_tn(yb1b, dttb)
        dbg_ref[...] += jnp.sum(dtt, axis=0, keepdims=True)
        dy = dyb1 * _gelu_grad(y)
        dd_ref[...] += jnp.sum(dy * u, axis=0, keepdims=True)
        dyh = dy.astype(BF16)
        for j in range(S5_BLOCKS):
            cj = cc_ref[:, j * S5_BC:(j + 1) * S5_BC]
            dyj = dyh[:, j * S5_BC:(j + 1) * S5_BC]
            g_scr[:, j * S5_BS:(j + 1) * S5_BS] = _dot_nt(dyj, cj[:S5_BS, :])
            g_scr[:, S5_N + j * S5_BS:S5_N + (j + 1) * S5_BS] = _dot_nt(dyj, cj[S5_BS:, :])
        _s5_scan(g_scr, tab, carry, tm, True)
        gs = g_scr[...]
        gb = gs.astype(BF16)
        sv = s_ref[...]
        sb = sv.astype(BF16)
        ub = u.astype(BF16)
        dus = []
        for j in range(S5_BLOCKS):
            rs = slice(j * S5_BS, (j + 1) * S5_BS)
            ims = slice(S5_N + j * S5_BS, S5_N + (j + 1) * S5_BS)
            cs = slice(j * S5_BC, (j + 1) * S5_BC)
            bj = bb_ref[cs, :]
            dus.append(_dot_nt(gb[:, rs], bj[:, :S5_BS]) + _dot_nt(gb[:, ims], bj[:, S5_BS:]))
            dbb_ref[cs, :S5_BS] += _dot_tn(ub[:, cs], gb[:, rs])
            dbb_ref[cs, S5_BS:] += _dot_tn(ub[:, cs], gb[:, ims])
            dcc_ref[:S5_BS, cs] += _dot_tn(sb[:, rs], dyh[:, cs])
            dcc_ref[S5_BS:, cs] += _dot_tn(sb[:, ims], dyh[:, cs])
        du_ref[...] = (jnp.concatenate(dus, axis=1) + d_ref[...] * dy).astype(BF16)
        first = lax.broadcasted_iota(jnp.int32, (tm, 1), 0) == 0
        before = jnp.where(i == nt - 1, 0.0, sprev_ref[SUBLANES - 1:SUBLANES, :])
        sp = jnp.where(first, before, pltpu.roll(sv, 1, 0))
        g_re, g_im = gs[:, :S5_N], gs[:, S5_N:]
        sp_re, sp_im = sp[:, :S5_N], sp[:, S5_N:]
        dab_ref[0:1, :] += jnp.sum(g_re * sp_re + g_im * sp_im, axis=0, keepdims=True)
        dab_ref[1:2, :] += jnp.sum(g_im * sp_re - g_re * sp_im, axis=0, keepdims=True)

    tile = lambda c: pl.BlockSpec((tm, c), lambda i: (nt - 1 - i, 0))
    acc = lambda r, c: jax.ShapeDtypeStruct((r, c), F32)
    return pl.pallas_call(
        body, name=name, grid=(nt,),
        in_specs=[tile(w), tile(w), tile(w), tile(2 * S5_N),
                  pl.BlockSpec((SUBLANES, 2 * S5_N), lambda i: (jnp.maximum((nt - 1 - i) * per - 1, 0), 0)),
                  _full((w, 2 * S5_BS)), _full((2 * S5_BS, w)), _full((SUBLANES, S5_N)), _full((1, w)),
                  _full((w, w)), _full((1, w))],
        out_specs=[tile(w), _full((w, 2 * S5_BS)), _full((2 * S5_BS, w)), _full((SUBLANES, S5_N)), _full((1, w)),
                   _full((w, w)), _full((1, w))],
        out_shape=[jax.ShapeDtypeStruct((t, w), BF16), acc(w, 2 * S5_BS), acc(2 * S5_BS, w), acc(SUBLANES, S5_N),
                   acc(1, w), acc(w, w), acc(1, w)],
        scratch_shapes=[pltpu.VMEM((tm, 2 * S5_N), F32), pltpu.VMEM((8, SUBLANES, S5_N), F32),
                        pltpu.VMEM((SUBLANES, 2 * S5_N), F32)],
        compiler_params=_params("arbitrary"),
    )(dyb, ys5, u5, s, s, bbc, ccc, ab8, d_row, wglu, bglu_row)


def s5_discretize(a_re, a_im, log_dt, b_re, b_im, *, name):
    n = a_re.shape[0]

    def body(ar_ref, ai_ref, dt_ref, br_ref, bi_ref, ab_ref, bbr_ref, bbi_ref):
        ar, ai, dtv = ar_ref[...], ai_ref[...], jnp.exp(dt_ref[...])
        mag = jnp.exp(ar * dtv)
        abr = mag * jnp.cos(ai * dtv)
        abi = mag * jnp.sin(ai * dtv)
        den = ar * ar + ai * ai
        nr = abr - 1.0
        cr = (nr * ar + abi * ai) / den
        ci = (abi * ar - nr * ai) / den
        ab_ref[:, 0:1] = abr
        ab_ref[:, 1:2] = abi
        br, bi = br_ref[...], bi_ref[...]
        bbr_ref[...] = cr * br - ci * bi
        bbi_ref[...] = cr * bi + ci * br

    col = jax.ShapeDtypeStruct((n, 2), F32)
    mat = jax.ShapeDtypeStruct((n, S5_GROUP), F32)
    return pl.pallas_call(body, name=name, out_shape=[col, mat, mat])(a_re, a_im, log_dt, b_re, b_im)


def s5_discretize_bwd(a_re, a_im, log_dt, b_re, b_im, dab, dbb_re, dbb_im, *, name):
    n = a_re.shape[0]

    def body(ar_ref, ai_ref, dt_ref, br_ref, bi_ref, dab_ref, dbr_ref, dbi_ref, da_ref, ddt_ref, gbr_ref, gbi_ref):
        ar, ai, dtv = ar_ref[...], ai_ref[...], jnp.exp(dt_ref[...])
        mag = jnp.exp(ar * dtv)
        abr = mag * jnp.cos(ai * dtv)
        abi = mag * jnp.sin(ai * dtv)
        den = ar * ar + ai * ai
        nr = abr - 1.0
        cr = (nr * ar + abi * ai) / den
        ci = (abi * ar - nr * ai) / den
        br, bi = br_ref[...], bi_ref[...]
        dbr, dbi = dbr_ref[...], dbi_ref[...]
        gbr_ref[...] = cr * dbr + ci * dbi
        gbi_ref[...] = cr * dbi - ci * dbr
        gcr = jnp.sum(dbr * br + dbi * bi, axis=1, keepdims=True)
        gci = jnp.sum(dbi * br - dbr * bi, axis=1, keepdims=True)
        ilr, ili = ar / den, -ai / den
        gabr = dab_ref[:, 0:1] + (ilr * gcr + ili * gci)
        gabi = dab_ref[:, 1:2] + (ilr * gci - ili * gcr)
        t1r, t1i = dtv * abr, dtv * abi
        t2r, t2i = -(cr * ilr - ci * ili), -(cr * ili + ci * ilr)
        da_ref[:, 0:1] = (t1r * gabr + t1i * gabi) + (t2r * gcr + t2i * gci)
        da_ref[:, 1:2] = (t1r * gabi - t1i * gabr) + (t2r * gci - t2i * gcr)
        lr, li = ar * abr - ai * abi, ar * abi + ai * abr
        dlog = (lr * gabr + li * gabi) * dtv
        ddt_ref[...] = jnp.sum(dlog.reshape(S5_GROUPS, S5_STATE, 1), axis=1)

    col2 = jax.ShapeDtypeStruct((n, 2), F32)
    col1 = jax.ShapeDtypeStruct((S5_GROUPS, 1), F32)
    mat = jax.ShapeDtypeStruct((n, S5_GROUP), F32)
    return pl.pallas_call(body, name=name, out_shape=[col2, col1, mat, mat])(
        a_re, a_im, log_dt, b_re, b_im, dab, dbb_re, dbb_im)


def merge_fwd(ya, yb, graw, x, wba, wbb, wout, *, tm, name):
    t, d = x.shape

    def body(ya_ref, yb_ref, g_ref, x_ref, wba_ref, wbb_ref, wo_ref, x1_ref, pa_ref, pb_ref):
        pa = _dot(ya_ref[...], wba_ref[...])
        pb = _dot(yb_ref[...], wbb_ref[...])
        gate = _sigmoid(g_ref[...])
        merged = gate[:, :d] * pa + gate[:, d:] * pb
        x1_ref[...] = x_ref[...] + _dot(merged.astype(BF16), wo_ref[...])
        pa_ref[...] = pa
        pb_ref[...] = pb

    tile = lambda c: pl.BlockSpec((tm, c), lambda i: (i, 0))
    out = jax.ShapeDtypeStruct((t, d), F32)
    return pl.pallas_call(
        body, name=name, grid=(t // tm,),
        in_specs=[tile(d), tile(S5_WIDTH), tile(2 * d), tile(d), _full((d, d)), _full((S5_WIDTH, d)), _full((d, d))],
        out_specs=[tile(d), tile(d), tile(d)], out_shape=[out, out, out],
        compiler_params=_params("parallel"),
    )(ya, yb, graw, x, wba, wbb, wout)


def merge_bwd(dx1b, graw, pa, pb, wba, wbb, wout, *, tm, name):
    t, d = pa.shape

    def body(dx_ref, g_ref, pa_ref, pb_ref, wba_ref, wbb_ref, wo_ref,
             mg_ref, dg_ref, dpa_ref, dpb_ref, dya_ref, dyb_ref):
        dm = _dot_nt(dx_ref[...], wo_ref[...])
        gate = _sigmoid(g_ref[...])
        g0, g1 = gate[:, :d], gate[:, d:]
        pa, pb = pa_ref[...], pb_ref[...]
        mg_ref[...] = (g0 * pa + g1 * pb).astype(BF16)
        dg_ref[:, :d] = (dm * pa * g0 * (1.0 - g0)).astype(BF16)
        dg_ref[:, d:] = (dm * pb * g1 * (1.0 - g1)).astype(BF16)
        dpa = (dm * g0).astype(BF16)
        dpb = (dm * g1).astype(BF16)
        dpa_ref[...] = dpa
        dpb_ref[...] = dpb
        dya_ref[...] = _dot_nt(dpa, wba_ref[...])
        dyb_ref[...] = _dot_nt(dpb, wbb_ref[...])

    tile = lambda c: pl.BlockSpec((tm, c), lambda i: (i, 0))
    sds = jax.ShapeDtypeStruct
    return pl.pallas_call(
        body, name=name, grid=(t // tm,),
        in_specs=[tile(d), tile(2 * d), tile(d), tile(d), _full((d, d)), _full((S5_WIDTH, d)), _full((d, d))],
        out_specs=[tile(d), tile(2 * d), tile(d), tile(d), tile(d), tile(S5_WIDTH)],
        out_shape=[sds((t, d), BF16), sds((t, 2 * d), BF16), sds((t, d), BF16), sds((t, d), BF16),
                   sds((t, d), F32), sds((t, S5_WIDTH), F32)],
        compiler_params=_params("parallel"),
    )(dx1b, graw, pa, pb, wba, wbb, wout)


def mlp_out_loss(a1, x1, w2, gf, target, *, tm, name):
    t, d = x1.shape
    f = a1.shape[1]

    def body(a_ref, x_ref, w_ref, g_ref, tg_ref, act_ref, dx_ref, dxb_ref, loss_ref, dg_ref):
        i = pl.program_id(0)
        av = jnp.maximum(a_ref[...], 0.0)
        act = (av * av).astype(BF16)
        act_ref[...] = act
        x2 = x_ref[...] + _dot(act, w_ref[...])
        r = lax.rsqrt(jnp.mean(x2 * x2, axis=-1, keepdims=True) + EPS)
        xn = x2 * r
        err = xn * g_ref[...] - tg_ref[...]
        dyf = err * (1.0 / d)
        dxn = dyf * g_ref[...]
        dx = r * (dxn - xn * jnp.mean(dxn * xn, axis=-1, keepdims=True))
        dx_ref[...] = dx
        dxb_ref[...] = dx.astype(BF16)

        @pl.when(i == 0)
        def _():
            loss_ref[...] = jnp.zeros_like(loss_ref)
            dg_ref[...] = jnp.zeros_like(dg_ref)

        loss_ref[...] += jnp.sum(err * err) * (0.5 / d)
        dg_ref[...] += jnp.sum(dyf * xn, axis=0, keepdims=True)

    tile = lambda c: pl.BlockSpec((tm, c), lambda i: (i, 0))
    sds = jax.ShapeDtypeStruct
    return pl.pallas_call(
        body, name=name, grid=(t // tm,),
        in_specs=[tile(f), tile(d), _full((f, d)), _full((1, d)), tile(d)],
        out_specs=[tile(f), tile(d), tile(d), _full((1, LANES)), _full((1, d))],
        out_shape=[sds((t, f), BF16), sds((t, d), F32), sds((t, d), BF16), sds((1, LANES), F32), sds((1, d), F32)],
        compiler_params=_params("arbitrary"),
    )(a1, x1, w2, gf.reshape(1, d), target)


def mlp_bwd_act(dx2b, a1, w2, *, tm, name):
    t, f = a1.shape
    d = dx2b.shape[1]

    def body(dx_ref, a_ref, w_ref, o_ref):
        dact = _dot_nt(dx_ref[...], w_ref[...])
        o_ref[...] = (dact * 2.0 * jnp.maximum(a_ref[...], 0.0)).astype(BF16)

    tile = lambda c: pl.BlockSpec((tm, c), lambda i: (i, 0))
    return pl.pallas_call(
        body, name=name, grid=(t // tm,),
        in_specs=[tile(d), tile(f), _full((f, d))], out_specs=tile(f),
        out_shape=jax.ShapeDtypeStruct((t, f), BF16),
        compiler_params=_params("parallel"),
    )(dx2b, a1, w2)


def _row_tile(rows, cap):
    best = SUBLANES
    for c in range(SUBLANES, cap + 1, SUBLANES):
        if rows % c == 0:
            best = c
    return best


def adamw_packed(w, m, v, gparts, *, name):
    rows = w.shape[0]
    tr = _row_tile(rows, 1024)

    def body(w_ref, m_ref, v_ref, g_ref, go_ref, d_ref, mo_ref, vo_ref):
        g = g_ref[0]
        for k in range(1, N_DEV):
            g = g + g_ref[k]
        mn = ADAM_B1 * m_ref[...] + (1.0 - ADAM_B1) * g
        vn = ADAM_B2 * v_ref[...] + (1.0 - ADAM_B2) * (g * g)
        m_hat = mn / (1.0 - ADAM_B1 ** ADAM_STEP)
        v_hat = vn / (1.0 - ADAM_B2 ** ADAM_STEP)
        go_ref[...] = g
        d_ref[...] = -ADAM_LR * (m_hat / (jnp.sqrt(v_hat) + ADAM_EPS) + ADAM_WD * w_ref[...])
        mo_ref[...] = mn
        vo_ref[...] = vn

    tile = pl.BlockSpec((tr, LANES), lambda i: (i, 0))
    out = jax.ShapeDtypeStruct((rows, LANES), F32)
    return pl.pallas_call(
        body, name=name, grid=(rows // tr,),
        in_specs=[tile, tile, tile, pl.BlockSpec((N_DEV, tr, LANES), lambda i: (0, i, 0))],
        out_specs=[tile, tile, tile, tile], out_shape=[out, out, out, out],
        compiler_params=_params("parallel"),
    )(w, m, v, gparts)


_MESH = pl.DeviceIdType.MESH
_RELATIONS = [(dx, dy, dc) for dx in (0, 1) for dy in (0, 1) for dc in (0, 1)][1:]


def _place():
    x, y, c = lax.axis_index("x"), lax.axis_index("y"), lax.axis_index("c")
    return x, y, c, 4 * x + 2 * y + c


def all_gather_blocks(block, *, name):
    rows, lanes = block.shape

    def body(x_ref, out_ref, send_sems, recv_sems, local_sem):
        x, y, c, _ = _place()
        me, sibling = (x, y, c), (x, y, 1 - c)
        chips = [(1 - x, y), (x, 1 - y), (1 - x, 1 - y)]

        def slot(px, py, pc):
            return out_ref.at[4 * px + 2 * py + pc]

        def copy(k, blk, to, src=None):
            return pltpu.make_async_remote_copy(
                src_ref=slot(*blk) if src is None else src, dst_ref=slot(*blk),
                send_sem=send_sems.at[k], recv_sem=recv_sems.at[k], device_id=to, device_id_type=_MESH)

        mine = pltpu.make_async_copy(x_ref, slot(*me), local_sem)
        mine.start()
        first = [copy(0, me, sibling, src=x_ref)]
        first += [copy(1 + j, me, (*chip, c), src=x_ref) for j, chip in enumerate(chips)]
        for cp in first:
            cp.start()
        passed = [copy(4 + j, (*chip, c), sibling) for j, chip in enumerate(chips)]
        for j, chip in enumerate(chips):
            copy(1 + j, (*chip, c), me).wait_recv()
            passed[j].start()
        copy(0, sibling, me).wait_recv()
        for j, chip in enumerate(chips):
            copy(4 + j, (*chip, 1 - c), me).wait_recv()
        for cp in first + passed:
            cp.wait_send()
        mine.wait()

    return pl.pallas_call(
        body, name=name,
        in_specs=[pl.BlockSpec(memory_space=pl.ANY)], out_specs=pl.BlockSpec(memory_space=pl.ANY),
        out_shape=jax.ShapeDtypeStruct((N_DEV, rows, lanes), block.dtype),
        scratch_shapes=[pltpu.SemaphoreType.DMA((7,)), pltpu.SemaphoreType.DMA((7,)), pltpu.SemaphoreType.DMA],
        compiler_params=pltpu.CompilerParams(has_side_effects=True),
    )(block)


def exchange_partials(parts, *, name):
    _, rows, lanes = parts.shape

    def body(p_ref, out_ref, send_sems, recv_sems, local_sem):
        x, y, c, idx = _place()
        mine = pltpu.make_async_copy(p_ref.at[idx], out_ref.at[idx], local_sem)
        mine.start()
        copies = []
        for k, (dx, dy, dc) in enumerate(_RELATIONS):
            px, py, pc = x ^ dx, y ^ dy, c ^ dc
            copies.append(pltpu.make_async_remote_copy(
                src_ref=p_ref.at[4 * px + 2 * py + pc], dst_ref=out_ref.at[idx],
                send_sem=send_sems.at[k], recv_sem=recv_sems.at[k], device_id=(px, py, pc), device_id_type=_MESH))
        for cp in copies:
            cp.start()
        for cp in copies:
            cp.wait_recv()
        for cp in copies:
            cp.wait_send()
        mine.wait()

    return pl.pallas_call(
        body, name=name,
        in_specs=[pl.BlockSpec(memory_space=pl.ANY)], out_specs=pl.BlockSpec(memory_space=pl.ANY),
        out_shape=jax.ShapeDtypeStruct(parts.shape, parts.dtype),
        scratch_shapes=[pltpu.SemaphoreType.DMA((7,)), pltpu.SemaphoreType.DMA((7,)), pltpu.SemaphoreType.DMA],
        compiler_params=pltpu.CompilerParams(has_side_effects=True),
    )(parts)


def _lane_row(v):
    return jnp.pad(v.astype(F32), (0, LANES - v.shape[0])).reshape(1, LANES)


def _block_diag_b(bb):
    eye = jnp.eye(SUBLANES, dtype=F32)
    return jnp.einsum("jgpk,gh->jgkhp", bb.reshape(S5_BLOCKS, 8, S5_STATE, S5_GROUP), eye).reshape(S5_WIDTH, S5_BS)


def _block_diag_c(cc):
    eye = jnp.eye(SUBLANES, dtype=F32)
    return jnp.einsum("jgkp,gh->hpjgk", cc.reshape(S5_BLOCKS, 8, S5_GROUP, S5_STATE), eye).reshape(S5_BS, S5_WIDTH)


def _diag_blocks_b(m):
    return jnp.einsum("jgkgp->jgpk", m.reshape(S5_BLOCKS, 8, S5_GROUP, 8, S5_STATE)).reshape(S5_GROUPS, S5_STATE, S5_GROUP)


def _diag_blocks_c(m):
    return jnp.einsum("gpjgk->jgkp", m.reshape(8, S5_STATE, S5_BLOCKS, 8, S5_GROUP)).reshape(S5_GROUPS, S5_GROUP, S5_STATE)


def local_step(x, target, p, win_p, conv_w8, wglu, wba, wbb, wout, w1, w2):
    t = x.shape[0]
    tm = min(256, t)
    d = D_MODEL
    h, xbc_raw, graw, z, u5, dtr = rms_matmul(x, p["norm_mix_g"], win_p, IN_SPLITS, tm=tm, name="in_proj")
    xbc = conv_silu_fwd(xbc_raw, conv_w8, p["conv_b"], tm=tm, name="conv_fwd")
    bias_row, alog_row, d_row = _lane_row(p["dt_bias"]), _lane_row(p["a_log"]), _lane_row(p["d_ssd"])
    ya, yssd, states = ssd_fwd(xbc, dtr, z, bias_row, alog_row, d_row, p["ssd_norm_g"], name="ssd_fwd")

    n = S5_N
    a_re_c, a_im_c = p["s5_a_re"].reshape(n, 1), p["s5_a_im"].reshape(n, 1)
    dt_c = jnp.repeat(p["s5_log_dt"], S5_STATE).reshape(n, 1)
    b_re_m, b_im_m = p["s5_b_re"].reshape(n, S5_GROUP), p["s5_b_im"].reshape(n, S5_GROUP)
    ab, bb_re, bb_im = s5_discretize(a_re_c, a_im_c, dt_c, b_re_m, b_im_m, name="s5_disc")
    ab8 = jnp.pad(ab.T, ((0, SUBLANES - 2), (0, 0)))
    bbc = jnp.concatenate([_block_diag_b(bb_re.reshape(S5_GROUPS, S5_STATE, S5_GROUP)),
                           _block_diag_b(bb_im.reshape(S5_GROUPS, S5_STATE, S5_GROUP))], axis=1).astype(BF16)
    ccc = jnp.concatenate([_block_diag_c(p["s5_c_re"]), -_block_diag_c(p["s5_c_im"])], axis=0).astype(BF16)
    s5d_row = p["s5_d"].reshape(1, S5_WIDTH)
    bglu_row = p["s5_glu_b"].reshape(1, S5_WIDTH)
    s, ys5, yb = s5_fwd(u5, bbc, ccc, ab8, s5d_row, wglu, bglu_row, tm=tm, name="s5_fwd")

    x1, pa, pb = merge_fwd(ya, yb, graw, x, wba, wbb, wout, tm=tm, name="merge_fwd")
    h2, a1 = rms_matmul(x1, p["norm_mlp_g"], w1, (D_FF,), tm=tm, name="mlp_in")
    act, dx2, dx2b, loss_row, dgf = mlp_out_loss(a1, x1, w2, p["norm_final_g"], target, tm=tm, name="mlp_out_loss")

    g = {}
    g["norm_final_g"] = dgf[0]
    da1 = mlp_bwd_act(dx2b, a1, w2, tm=tm, name="mlp_bwd_act")
    g["w_mlp_out"] = matmul_tn(act, dx2b, name="dw_mlp_out")
    g["w_mlp_in"] = matmul_tn(h2, da1, name="dw_mlp_in")
    dx1, dx1b, dgm = matmul_nt_rms_bwd(da1, w1, x1, p["norm_mlp_g"], dx2, tm=tm, name="mlp_in_bwd")
    g["norm_mlp_g"] = dgm[0]
    merged, dgraw, dpa, dpb, dya, dyb = merge_bwd(dx1b, graw, pa, pb, wba, wbb, wout, tm=tm, name="merge_bwd")
    g["w_out"] = matmul_tn(merged, dx1b, name="dw_out")
    g["w_branch"] = jnp.concatenate([matmul_tn(ya, dpa, name="dw_branch_a"), matmul_tn(yb, dpb, name="dw_branch_b")], axis=0)

    du5, dbbc, dccc, dab, dd5, dwglu, dbglu = s5_bwd(dyb, ys5, u5, s, bbc, ccc, ab8, s5d_row, wglu, bglu_row,
                                                     tm=min(128, t), name="s5_bwd")
    g["s5_glu_w"], g["s5_glu_b"], g["s5_d"] = dwglu, dbglu[0], dd5[0]
    g["s5_c_re"] = _diag_blocks_c(dccc[:S5_BS])
    g["s5_c_im"] = -_diag_blocks_c(dccc[S5_BS:])
    dbb_re = _diag_blocks_b(dbbc[:, :S5_BS]).reshape(n, S5_GROUP)
    dbb_im = _diag_blocks_b(dbbc[:, S5_BS:]).reshape(n, S5_GROUP)
    da, dlogdt, gb_re, gb_im = s5_discretize_bwd(a_re_c, a_im_c, dt_c, b_re_m, b_im_m, dab[:2].T, dbb_re, dbb_im,
                                                  name="s5_disc_bwd")
    g["s5_a_re"] = da[:, 0].reshape(S5_GROUPS, S5_STATE)
    g["s5_a_im"] = da[:, 1].reshape(S5_GROUPS, S5_STATE)
    g["s5_log_dt"] = dlogdt[:, 0]
    g["s5_b_re"] = gb_re.reshape(S5_GROUPS, S5_STATE, S5_GROUP)
    g["s5_b_im"] = gb_im.reshape(S5_GROUPS, S5_STATE, S5_GROUP)

    dz, dxbc, ddtr, dng, dd_row, dalog_row, dbias_row = ssd_bwd(
        dya, yssd, z, xbc, dtr, states, bias_row, alog_row, d_row, p["ssd_norm_g"], name="ssd_bwd")
    g["ssd_norm_g"] = dng[0]
    g["d_ssd"], g["a_log"], g["dt_bias"] = dd_row[0, :SSD_HEADS], dalog_row[0, :SSD_HEADS], dbias_row[0, :SSD_HEADS]
    dxc, dconv_b, dconv_w8 = conv_silu_bwd_a(xbc_raw, dxbc, conv_w8, p["conv_b"], tm=tm, name="conv_bwd_a")
    g["conv_b"], g["conv_w"] = dconv_b[0], dconv_w8[:SSD_CONV]
    dxbc_raw = conv_bwd_b(dxc, conv_w8, tm=tm, name="conv_bwd_b")
    dproj = jnp.concatenate([dxbc_raw, dgraw, dz, du5, ddtr], axis=1)
    g["w_in_p"] = matmul_tn(h, dproj, name="dw_in")
    grad_x, _, dgx = matmul_nt_rms_bwd(dproj, win_p, x, p["norm_mix_g"], dx1, tm=tm, name="in_proj_bwd")
    g["norm_mix_g"] = dgx[0]
    return loss_row[0, 0], grad_x, g


WEIGHT_ORDER = ["norm_mix_g", "w_in", "conv_w", "conv_b", "dt_bias", "a_log", "d_ssd", "ssd_norm_g", "s5_a_re",
                "s5_a_im", "s5_log_dt", "s5_b_re", "s5_b_im", "s5_c_re", "s5_c_im", "s5_d", "s5_glu_w", "s5_glu_b",
                "w_branch", "w_out", "norm_mlp_g", "w_mlp_in", "w_mlp_out", "norm_final_g"]
SHARDED = {"w_in": ((D_MODEL, D_IN_PROJ), 1), "conv_w": ((SSD_CONV, CONV_DIM), 1), "s5_glu_w": ((S5_WIDTH, S5_WIDTH), 0),
           "w_branch": ((D_MODEL + S5_WIDTH, D_MODEL), 0), "w_out": ((D_MODEL, D_MODEL), 0),
           "w_mlp_in": ((D_MODEL, D_FF), 1), "w_mlp_out": ((D_FF, D_MODEL), 0)}
SHARDED_ORDER = ["w_in", "conv_w", "s5_glu_w", "w_branch", "w_out", "w_mlp_in", "w_mlp_out"]
SMALL_ORDER = [n for n in WEIGHT_ORDER if n not in SHARDED]


def _shard_shape(name):
    (r, c), ax = SHARDED[name]
    return (r, c // N_DEV) if ax == 1 else (r // N_DEV, c)


def _rows_of(a):
    return a.reshape(-1, LANES)


def _split_shards(name, full):
    (r, c), ax = SHARDED[name]
    if ax == 1:
        full = full.reshape(r, N_DEV, c // N_DEV).transpose(1, 0, 2)
    return full.reshape(N_DEV, -1, LANES)


def _join_shards(name, stacked):
    (r, c), ax = SHARDED[name]
    if ax == 1:
        return stacked.transpose(1, 0, 2).reshape(r, c)
    return stacked.reshape(r, c)


def _pack_small(values, loss):
    flat = jnp.concatenate([values[n].reshape(-1).astype(F32) for n in SMALL_ORDER] + [loss.reshape(1).astype(F32)])
    rows = -(-flat.shape[0] // (LANES * SUBLANES)) * SUBLANES
    return jnp.pad(flat, (0, rows * LANES - flat.shape[0])).reshape(rows, LANES)


def _unpack_small(rows, shapes):
    flat = rows.reshape(-1)
    out, off = {}, 0
    for n in SMALL_ORDER:
        size = math.prod(shapes[n])
        out[n] = flat[off:off + size].reshape(shapes[n])
        off += size
    return out, flat[off]


def kernel(x, norm_mix_g, w_in, conv_w, conv_b, dt_bias, a_log, d_ssd, ssd_norm_g, s5_a_re, s5_a_im, s5_log_dt, s5_b_re, s5_b_im, s5_c_re, s5_c_im, s5_d, s5_glu_w, s5_glu_b, w_branch, w_out, norm_mlp_g, w_mlp_in, w_mlp_out, norm_final_g, loss_target, m_norm_mix_g, m_w_in, m_conv_w, m_conv_b, m_dt_bias, m_a_log, m_d_ssd, m_ssd_norm_g, m_s5_a_re, m_s5_a_im, m_s5_log_dt, m_s5_b_re, m_s5_b_im, m_s5_c_re, m_s5_c_im, m_s5_d, m_s5_glu_w, m_s5_glu_b, m_w_branch, m_w_out, m_norm_mlp_g, m_w_mlp_in, m_w_mlp_out, m_norm_final_g, v_norm_mix_g, v_w_in, v_conv_w, v_conv_b, v_dt_bias, v_a_log, v_d_ssd, v_ssd_norm_g, v_s5_a_re, v_s5_a_im, v_s5_log_dt, v_s5_b_re, v_s5_b_im, v_s5_c_re, v_s5_c_im, v_s5_d, v_s5_glu_w, v_s5_glu_b, v_w_branch, v_w_out, v_norm_mlp_g, v_w_mlp_in, v_w_mlp_out, v_norm_final_g):
    w = dict(norm_mix_g=norm_mix_g, w_in=w_in, conv_w=conv_w, conv_b=conv_b, dt_bias=dt_bias, a_log=a_log, d_ssd=d_ssd,
             ssd_norm_g=ssd_norm_g, s5_a_re=s5_a_re, s5_a_im=s5_a_im, s5_log_dt=s5_log_dt, s5_b_re=s5_b_re,
             s5_b_im=s5_b_im, s5_c_re=s5_c_re, s5_c_im=s5_c_im, s5_d=s5_d, s5_glu_w=s5_glu_w, s5_glu_b=s5_glu_b,
             w_branch=w_branch, w_out=w_out, norm_mlp_g=norm_mlp_g, w_mlp_in=w_mlp_in, w_mlp_out=w_mlp_out,
             norm_final_g=norm_final_g)
    m = dict(norm_mix_g=m_norm_mix_g, w_in=m_w_in, conv_w=m_conv_w, conv_b=m_conv_b, dt_bias=m_dt_bias, a_log=m_a_log,
             d_ssd=m_d_ssd, ssd_norm_g=m_ssd_norm_g, s5_a_re=m_s5_a_re, s5_a_im=m_s5_a_im, s5_log_dt=m_s5_log_dt,
             s5_b_re=m_s5_b_re, s5_b_im=m_s5_b_im, s5_c_re=m_s5_c_re, s5_c_im=m_s5_c_im, s5_d=m_s5_d,
             s5_glu_w=m_s5_glu_w, s5_glu_b=m_s5_glu_b, w_branch=m_w_branch, w_out=m_w_out, norm_mlp_g=m_norm_mlp_g,
             w_mlp_in=m_w_mlp_in, w_mlp_out=m_w_mlp_out, norm_final_g=m_norm_final_g)
    v = dict(norm_mix_g=v_norm_mix_g, w_in=v_w_in, conv_w=v_conv_w, conv_b=v_conv_b, dt_bias=v_dt_bias, a_log=v_a_log,
             d_ssd=v_d_ssd, ssd_norm_g=v_ssd_norm_g, s5_a_re=v_s5_a_re, s5_a_im=v_s5_a_im, s5_log_dt=v_s5_log_dt,
             s5_b_re=v_s5_b_re, s5_b_im=v_s5_b_im, s5_c_re=v_s5_c_re, s5_c_im=v_s5_c_im, s5_d=v_s5_d,
             s5_glu_w=v_s5_glu_w, s5_glu_b=v_s5_glu_b, w_branch=v_w_branch, w_out=v_w_out, norm_mlp_g=v_norm_mlp_g,
             w_mlp_in=v_w_mlp_in, w_mlp_out=v_w_mlp_out, norm_final_g=v_norm_final_g)

    def payload(name):
        a = w[name]
        if name == "conv_w":
            return lax.bitcast_convert_type(a, BF16).reshape(-1)
        return a.astype(BF16).reshape(-1)

    sizes = [(2 if n == "conv_w" else 1) * math.prod(_shard_shape(n)) for n in SHARDED_ORDER]
    block = jnp.concatenate([payload(n) for n in SHARDED_ORDER]).reshape(-1, LANES)
    gathered = all_gather_blocks(block, name="gather_weights").reshape(N_DEV, -1)
    full, off = {}, 0
    for n, size in zip(SHARDED_ORDER, sizes):
        piece = gathered[:, off:off + size]
        off += size
        if n == "conv_w":
            piece = lax.bitcast_convert_type(piece.reshape(N_DEV, -1, 2), F32)
        full[n] = _join_shards(n, piece.reshape((N_DEV,) + _shard_shape(n)))
    wi = full["w_in"]
    win_p = jnp.concatenate([wi[:, 1024:3072], wi[:, 3600:5648], wi[:, 0:1024], wi[:, 3088:3600], wi[:, 3072:3088],
                             jnp.zeros((D_MODEL, LANES - SSD_HEADS), BF16)], axis=1)
    conv_w8 = jnp.pad(full["conv_w"], ((0, SUBLANES - SSD_CONV), (0, 0)))
    small = {n: w[n] for n in SMALL_ORDER}

    loss_part, grad_x, g = local_step(
        x[0], loss_target[0], small, win_p, conv_w8, full["s5_glu_w"], full["w_branch"][:D_MODEL],
        full["w_branch"][D_MODEL:], full["w_out"], full["w_mlp_in"], full["w_mlp_out"])

    gp = g.pop("w_in_p")
    g["w_in"] = jnp.concatenate([gp[:, 4096:5120], gp[:, 0:2048], gp[:, 5632:5648], gp[:, 5120:5632], gp[:, 2048:4096]], axis=1)
    small_rows = _pack_small(g, loss_part)
    parts = jnp.concatenate([_split_shards(n, g[n]) for n in SHARDED_ORDER]
                            + [jnp.broadcast_to(small_rows, (N_DEV,) + small_rows.shape)], axis=1)
    received = exchange_partials(parts, name="exchange_grads")

    zero = jnp.zeros((), F32)

    def pack_state(vals):
        return jnp.concatenate([_rows_of(vals[n]) for n in SHARDED_ORDER] + [_pack_small(vals, zero)], axis=0)

    outs = adamw_packed(pack_state(w), pack_state(m), pack_state(v), received, name="adamw")
    shapes = {n: w[n].shape for n in WEIGHT_ORDER}
    results = []
    loss = None
    for packed in outs:
        vals, off = {}, 0
        for n in SHARDED_ORDER:
            rows = math.prod(shapes[n]) // LANES
            vals[n] = packed[off:off + rows].reshape(shapes[n])
            off += rows
        sm, extra = _unpack_small(packed[off:], shapes)
        vals.update(sm)
        if loss is None:
            loss = extra
        results.append(vals)
    grads, deltas, new_m, new_v = results
    return (loss, grad_x.reshape(x.shape), *[grads[n] for n in WEIGHT_ORDER], *[deltas[n] for n in WEIGHT_ORDER],
            *[new_m[n] for n in WEIGHT_ORDER], *[new_v[n] for n in WEIGHT_ORDER])
```

```python
import functools
import math

import jax
import jax.numpy as jnp
from jax import lax
from jax.experimental import pallas as pl
from jax.experimental.pallas import tpu as pltpu

F32 = jnp.float32
BF16 = jnp.bfloat16
HI = lax.Precision.HIGHEST

N_DEV = 8
D_MODEL = 1024
SSD_HEADS = 16
SSD_HEADDIM = 64
SSD_GROUPS = 4
SSD_HPG = 4
SSD_STATE = 128
SSD_CHUNK = 128
SSD_CONV = 4
CONV_DIM = 2048
S5_WIDTH = 512
S5_GROUP = 16
S5_GROUPS = 32
S5_STATE = 64
S5_N = S5_GROUPS * S5_STATE
D_FF = 4096
D_IN_PROJ = 5648
IN_SPLITS = (2048, 2048, 1024, 512, 128)
D_IN_PAD = sum(IN_SPLITS)
EPS = 1e-6
LANES = 128
SUBLANES = 8
VMEM_LIMIT = 56 * 1024 * 1024

ADAM_LR = 0.001
ADAM_B1 = 0.9
ADAM_B2 = 0.999
ADAM_EPS = 1e-08
ADAM_WD = 0.01
ADAM_STEP = 10

GELU_C = math.sqrt(2.0 / math.pi)
GELU_K = 0.044715


def _params(*sem):
    return pltpu.CompilerParams(dimension_semantics=sem, vmem_limit_bytes=VMEM_LIMIT)


def _full(shape):
    nd = len(shape)
    return pl.BlockSpec(shape, lambda *_: (0,) * nd)


def _dot(a, b):
    return jnp.dot(a, b, preferred_element_type=F32)


def _dot_nt(a, b):
    return lax.dot_general(a, b, (((1,), (1,)), ((), ())), preferred_element_type=F32)


def _dot_tn(a, b):
    return lax.dot_general(a, b, (((0,), (0,)), ((), ())), preferred_element_type=F32)


def _dot_hi(a, b):
    return jnp.dot(a, b, preferred_element_type=F32, precision=HI)


def _dot_nt_hi(a, b):
    return lax.dot_general(a, b, (((1,), (1,)), ((), ())), preferred_element_type=F32, precision=HI)


def _dot_tn_hi(a, b):
    return lax.dot_general(a, b, (((0,), (0,)), ((), ())), preferred_element_type=F32, precision=HI)


def _sigmoid(x):
    return 1.0 / (1.0 + jnp.exp(-x))


def _softplus(x):
    return jnp.maximum(x, 0.0) + jnp.log(1.0 + jnp.exp(-jnp.abs(x)))


def _gelu(x):
    return 0.5 * x * (1.0 + jnp.tanh(GELU_C * (x + GELU_K * x * x * x)))


def _gelu_grad(x):
    th = jnp.tanh(GELU_C * (x + GELU_K * x * x * x))
    return 0.5 * (1.0 + th) + 0.5 * x * (1.0 - th * th) * GELU_C * (1.0 + 3.0 * GELU_K * x * x)


def rms_matmul(x, g, w, splits, *, tm, name):
    t, d = x.shape
    n = w.shape[1]
    assert sum(splits) == n

    def body(x_ref, g_ref, w_ref, h_ref, *o_refs):
        xv = x_ref[...]
        r = lax.rsqrt(jnp.mean(xv * xv, axis=-1, keepdims=True) + EPS)
        h = (xv * r * g_ref[...]).astype(BF16)
        h_ref[...] = h
        off = 0
        for o_ref, width in zip(o_refs, splits):
            o_ref[...] = _dot(h, w_ref[:, off:off + width])
            off += width

    tile = lambda c: pl.BlockSpec((tm, c), lambda i: (i, 0))
    return pl.pallas_call(
        body, name=name, grid=(t // tm,),
        in_specs=[tile(d), _full((1, d)), _full((d, n))],
        out_specs=[tile(d)] + [tile(c) for c in splits],
        out_shape=[jax.ShapeDtypeStruct((t, d), BF16)] + [jax.ShapeDtypeStruct((t, c), F32) for c in splits],
        compiler_params=_params("parallel"),
    )(x, g.reshape(1, d), w)


def _pick(n, cap):
    best = LANES
    for c in range(LANES, cap + 1, LANES):
        if n % c == 0:
            best = c
    return best


def matmul_tn(a, b, *, name, tk=1024):
    t, m = a.shape
    n = b.shape[1]
    tk = min(tk, t)
    tm = _pick(m, 1024)
    tn = _pick(n, 1280)
    nk = t // tk

    def body(a_ref, b_ref, o_ref):
        k = pl.program_id(2)

        @pl.when(k == 0)
        def _():
            o_ref[...] = jnp.zeros_like(o_ref)

        o_ref[...] += _dot_tn(a_ref[...], b_ref[...])

    return pl.pallas_call(
        body, name=name, grid=(m // tm, n // tn, nk),
        in_specs=[pl.BlockSpec((tk, tm), lambda i, j, k: (k, i)), pl.BlockSpec((tk, tn), lambda i, j, k: (k, j))],
        out_specs=pl.BlockSpec((tm, tn), lambda i, j, k: (i, j)),
        out_shape=jax.ShapeDtypeStruct((m, n), F32),
        compiler_params=_params("parallel", "parallel", "arbitrary"),
    )(a, b)


def matmul_nt_rms_bwd(dy, w, x, g, dres, *, tm, name):
    t, n = dy.shape
    d = w.shape[0]

    def body(dy_ref, w_ref, x_ref, g_ref, dres_ref, dx_ref, dxb_ref, dg_ref):
        i = pl.program_id(0)
        dh = _dot_nt(dy_ref[...], w_ref[...])
        xv = x_ref[...]
        r = lax.rsqrt(jnp.mean(xv * xv, axis=-1, keepdims=True) + EPS)
        xn = xv * r
        dxn = dh * g_ref[...]
        dx = dres_ref[...] + r * (dxn - xn * jnp.mean(dxn * xn, axis=-1, keepdims=True))
        dx_ref[...] = dx
        dxb_ref[...] = dx.astype(BF16)

        @pl.when(i == 0)
        def _():
            dg_ref[...] = jnp.zeros_like(dg_ref)

        dg_ref[...] += jnp.sum(dh * xn, axis=0, keepdims=True)

    return pl.pallas_call(
        body, name=name, grid=(t // tm,),
        in_specs=[pl.BlockSpec((tm, n), lambda i: (i, 0)), _full((d, n)), pl.BlockSpec((tm, d), lambda i: (i, 0)),
                  _full((1, d)), pl.BlockSpec((tm, d), lambda i: (i, 0))],
        out_specs=[pl.BlockSpec((tm, d), lambda i: (i, 0)), pl.BlockSpec((tm, d), lambda i: (i, 0)), _full((1, d))],
        out_shape=[jax.ShapeDtypeStruct((t, d), F32), jax.ShapeDtypeStruct((t, d), BF16),
                   jax.ShapeDtypeStruct((1, d), F32)],
        compiler_params=_params("arbitrary"),
    )(dy, w, x, g.reshape(1, d), dres)


def _conv_taps(prev8, cur, w_ref, tm):
    ext = jnp.concatenate([prev8, cur], axis=0)
    acc = cur * w_ref[3:4, :]
    for k in range(3):
        off = SUBLANES - 3 + k
        acc = acc + ext[off:off + tm, :] * w_ref[k:k + 1, :]
    return acc


def conv_silu_fwd(u, w8, b, *, tm, name):
    t, c = u.shape
    per = tm // SUBLANES

    def body(prev_ref, cur_ref, w_ref, b_ref, o_ref):
        i = pl.program_id(0)
        prev8 = jnp.where(i == 0, 0.0, prev_ref[...])
        xc = _conv_taps(prev8, cur_ref[...], w_ref, tm) + b_ref[...]
        o_ref[...] = xc * _sigmoid(xc)

    return pl.pallas_call(
        body, name=name, grid=(t // tm,),
        in_specs=[pl.BlockSpec((SUBLANES, c), lambda i: (jnp.maximum(i * per - 1, 0), 0)),
                  pl.BlockSpec((tm, c), lambda i: (i, 0)), _full((SUBLANES, c)), _full((1, c))],
        out_specs=pl.BlockSpec((tm, c), lambda i: (i, 0)),
        out_shape=jax.ShapeDtypeStruct((t, c), F32),
        compiler_params=_params("parallel"),
    )(u, u, w8, b.reshape(1, c))


def conv_silu_bwd_a(u, dy, w8, b, *, tm, name):
    t, c = u.shape
    per = tm // SUBLANES

    def body(prev_ref, cur_ref, dy_ref, w_ref, b_ref, dxc_ref, db_ref, dw_ref):
        i = pl.program_id(0)
        prev8 = jnp.where(i == 0, 0.0, prev_ref[...])
        cur = cur_ref[...]
        xc = _conv_taps(prev8, cur, w_ref, tm) + b_ref[...]
        sg = _sigmoid(xc)
        dxc = dy_ref[...] * (sg * (1.0 + xc * (1.0 - sg)))
        dxc_ref[...] = dxc

        @pl.when(i == 0)
        def _():
            db_ref[...] = jnp.zeros_like(db_ref)
            dw_ref[...] = jnp.zeros_like(dw_ref)

        db_ref[...] += jnp.sum(dxc, axis=0, keepdims=True)
        ext = jnp.concatenate([prev8, cur], axis=0)
        rows = []
        for k in range(SSD_CONV):
            off = SUBLANES - 3 + k
            rows.append(jnp.sum(dxc * ext[off:off + tm, :], axis=0, keepdims=True))
        rows.append(jnp.zeros((SUBLANES - SSD_CONV, c), F32))
        dw_ref[...] += jnp.concatenate(rows, axis=0)

    return pl.pallas_call(
        body, name=name, grid=(t // tm,),
        in_specs=[pl.BlockSpec((SUBLANES, c), lambda i: (jnp.maximum(i * per - 1, 0), 0)),
                  pl.BlockSpec((tm, c), lambda i: (i, 0)), pl.BlockSpec((tm, c), lambda i: (i, 0)),
                  _full((SUBLANES, c)), _full((1, c))],
        out_specs=[pl.BlockSpec((tm, c), lambda i: (i, 0)), _full((1, c)), _full((SUBLANES, c))],
        out_shape=[jax.ShapeDtypeStruct((t, c), F32), jax.ShapeDtypeStruct((1, c), F32),
                   jax.ShapeDtypeStruct((SUBLANES, c), F32)],
        compiler_params=_params("arbitrary"),
    )(u, u, dy, w8, b.reshape(1, c))


def conv_bwd_b(dxc, w8, *, tm, name):
    t, c = dxc.shape
    per = tm // SUBLANES
    last = t // SUBLANES - 1
    nt = t // tm

    def body(cur_ref, next_ref, w_ref, o_ref):
        i = pl.program_id(0)
        nxt = jnp.where(i == nt - 1, 0.0, next_ref[...])
        cur = cur_ref[...]
        ext = jnp.concatenate([cur, nxt], axis=0)
        acc = cur * w_ref[3:4, :]
        for j in range(1, SSD_CONV):
            acc = acc + ext[j:j + tm, :] * w_ref[3 - j:4 - j, :]
        o_ref[...] = acc.astype(BF16)

    return pl.pallas_call(
        body, name=name, grid=(nt,),
        in_specs=[pl.BlockSpec((tm, c), lambda i: (i, 0)),
                  pl.BlockSpec((SUBLANES, c), lambda i: (jnp.minimum((i + 1) * per, last), 0)), _full((SUBLANES, c))],
        out_specs=pl.BlockSpec((tm, c), lambda i: (i, 0)),
        out_shape=jax.ShapeDtypeStruct((t, c), BF16),
        compiler_params=_params("parallel"),
    )(dxc, dxc, w8)


def _ssd_chunk_terms(dtr, bias_row, alog_row):
    q = SSD_CHUNK
    row = lax.broadcasted_iota(jnp.int32, (q, q), 0)
    col = lax.broadcasted_iota(jnp.int32, (q, q), 1)
    ltri = (col <= row).astype(F32)
    dt = _softplus(dtr + bias_row)
    a_row = -jnp.exp(alog_row)
    la = dt * a_row
    cum = _dot_hi(ltri, la)
    cum_t = _dot_tn_hi(la, (row <= col).astype(F32))
    return dt, a_row, cum, cum_t, ltri, row, col


def _head_maps():
    di = SSD_HEADS * SSD_HEADDIM
    expand = (lax.broadcasted_iota(jnp.int32, (LANES, di), 1) // SSD_HEADDIM
              == lax.broadcasted_iota(jnp.int32, (LANES, di), 0)).astype(F32)
    collapse = (lax.broadcasted_iota(jnp.int32, (di, LANES), 0) // SSD_HEADDIM
                == lax.broadcasted_iota(jnp.int32, (di, LANES), 1)).astype(F32)
    return expand, collapse


def _ssd_decay(cum, cum_t, h, causal):
    seg = cum[:, h:h + 1] - cum_t[h:h + 1, :]
    return jnp.where(causal, jnp.exp(jnp.where(causal, seg, 0.0)), 0.0)


def _pair_block_diag(x2):
    low = lax.broadcasted_iota(jnp.int32, x2.shape, 1) < SSD_HEADDIM
    zero = jnp.zeros_like(x2)
    return jnp.concatenate([jnp.where(low, x2, zero), jnp.where(low, zero, x2)], axis=0)


def ssd_fwd(xbc, dtr, z, bias_row, alog_row, d_row, norm_g, *, name):
    t = xbc.shape[0]
    q, p, n = SSD_CHUNK, SSD_HEADDIM, SSD_STATE
    nc = t // q
    di = SSD_HEADS * p
    gw = di // SSD_GROUPS

    def body(xs_ref, b_ref, c_ref, dtr_ref, z_ref, bias_ref, alog_ref, d_ref, g_ref,
             ya_ref, yssd_ref, st_ref, state):
        ci = pl.program_id(0)

        @pl.when(ci == 0)
        def _():
            state[...] = jnp.zeros_like(state)

        st_ref[0] = state[...]
        dt, _, cum, cum_t, _, row, col = _ssd_chunk_terms(dtr_ref[...], bias_ref[...], alog_ref[...])
        causal = row >= col
        expand, _ = _head_maps()
        dt_full = _dot_hi(dt, expand)
        cum_full = _dot_hi(cum, expand)
        last_full = cum_full[q - 1:q, :]
        xs = xs_ref[...]
        xdt = xs * dt_full
        xd = (xdt * jnp.exp(last_full - cum_full)).astype(BF16)
        xdt_b = xdt.astype(BF16)
        e_full = jnp.exp(cum_full)
        eend_full = jnp.exp(last_full)
        for g in range(SSD_GROUPS):
            slab = slice(g * gw, (g + 1) * gw)
            bg = b_ref[:, g * n:(g + 1) * n].astype(BF16)
            cg = c_ref[:, g * n:(g + 1) * n].astype(BF16)
            cb = _dot_nt(cg, bg)
            st_g = state[:, slab]
            pairs = []
            for pr in range(SSD_HPG // 2):
                h0 = g * SSD_HPG + 2 * pr
                m0 = (cb * _ssd_decay(cum, cum_t, h0, causal)).astype(BF16)
                m1 = (cb * _ssd_decay(cum, cum_t, h0 + 1, causal)).astype(BF16)
                pairs.append(_dot(jnp.concatenate([m0, m1], axis=1), _pair_block_diag(xdt_b[:, h0 * p:(h0 + 2) * p])))
            y = jnp.concatenate(pairs, axis=1) + e_full[:, slab] * _dot(cg, st_g.astype(BF16))
            yssd_ref[:, slab] = y + d_ref[:, slab] * xs[:, slab]
            state[:, slab] = eend_full[:, slab] * st_g + _dot_tn(bg, xd[:, slab])
        zv = z_ref[...]
        ya1 = yssd_ref[...] * (zv * _sigmoid(zv))
        for g in range(SSD_GROUPS):
            gsl = slice(g * gw, (g + 1) * gw)
            sl = ya1[:, gsl]
            rg = lax.rsqrt(jnp.mean(sl * sl, axis=-1, keepdims=True) + EPS)
            ya_ref[:, gsl] = (sl * rg * g_ref[:, gsl]).astype(BF16)

    blk = lambda w, j: pl.BlockSpec((q, w), lambda c: (c, j))
    return pl.pallas_call(
        body, name=name, grid=(nc,),
        in_specs=[blk(di, 0), blk(512, 2), blk(512, 3), blk(LANES, 0), blk(di, 0),
                  _full((1, LANES)), _full((1, LANES)), _full((1, di)), _full((1, di))],
        out_specs=[blk(di, 0), blk(di, 0), pl.BlockSpec((1, n, di), lambda c: (c, 0, 0))],
        out_shape=[jax.ShapeDtypeStruct((t, di), BF16), jax.ShapeDtypeStruct((t, di), F32),
                   jax.ShapeDtypeStruct((nc, n, di), F32)],
        scratch_shapes=[pltpu.VMEM((n, di), F32)],
        compiler_params=_params("arbitrary"),
    )(xbc, xbc, xbc, dtr, z, bias_row, alog_row, d_row, norm_g.reshape(1, di))


def ssd_bwd(dya, yssd, z, xbc, dtr, states, bias_row, alog_row, d_row, norm_g, *, name):
    t = xbc.shape[0]
    q, p, n = SSD_CHUNK, SSD_HEADDIM, SSD_STATE
    nc = t // q
    di = SSD_HEADS * p
    gw = di // SSD_GROUPS

    def body(dya_ref, yssd_ref, z_ref, xs_ref, b_ref, c_ref, dtr_ref, st_ref, bias_ref, alog_ref, d_ref, g_ref,
             dz_ref, dxbc_ref, ddtr_ref, dng_ref, dd_ref, dalog_ref, dbias_ref, dstate, dyssd):
        ci = pl.program_id(0)

        @pl.when(ci == 0)
        def _():
            dstate[...] = jnp.zeros_like(dstate)
            dng_ref[...] = jnp.zeros_like(dng_ref)
            dd_ref[...] = jnp.zeros_like(dd_ref)
            dalog_ref[...] = jnp.zeros_like(dalog_ref)
            dbias_ref[...] = jnp.zeros_like(dbias_ref)

        zv = z_ref[...]
        sz = _sigmoid(zv)
        silu_z = zv * sz
        yv = yssd_ref[...]
        ya1 = yv * silu_z
        dyav = dya_ref[...]
        for g in range(SSD_GROUPS):
            gsl = slice(g * gw, (g + 1) * gw)
            sl = ya1[:, gsl]
            rg = lax.rsqrt(jnp.mean(sl * sl, axis=-1, keepdims=True) + EPS)
            ya2 = sl * rg
            dy_g = dyav[:, gsl]
            dng_ref[:, gsl] += jnp.sum(dy_g * ya2, axis=0, keepdims=True)
            dya2 = dy_g * g_ref[:, gsl]
            dya1 = rg * (dya2 - ya2 * jnp.mean(dya2 * ya2, axis=-1, keepdims=True))
            dyssd[:, gsl] = dya1 * silu_z[:, gsl]
            dz_ref[:, gsl] = (dya1 * yv[:, gsl] * (sz[:, gsl] * (1.0 + zv[:, gsl] * (1.0 - sz[:, gsl])))).astype(BF16)

        dtr_v = dtr_ref[...]
        dt, a_row, cum, cum_t, ltri, row, col = _ssd_chunk_terms(dtr_v, bias_ref[...], alog_ref[...])
        causal = row >= col
        expand, collapse = _head_maps()
        lane = lax.broadcasted_iota(jnp.int32, (1, LANES), 1)
        sub = lax.broadcasted_iota(jnp.int32, (LANES, 1), 0)
        is_last = lax.broadcasted_iota(jnp.int32, (q, 1), 0) == q - 1
        low = lax.broadcasted_iota(jnp.int32, (q, 2 * p), 1) < p
        dt_full = _dot_hi(dt, expand)
        cum_full = _dot_hi(cum, expand)
        last_full = cum_full[q - 1:q, :]
        e_full = jnp.exp(cum_full)
        eend_full = jnp.exp(last_full)
        dend_full = jnp.exp(last_full - cum_full)
        xs = xs_ref[...]
        xdt = xs * dt_full
        xdt_b = xdt.astype(BF16)
        xd_b = (xdt * dend_full).astype(BF16)
        dy_all = dyssd[...]
        dy_b = dy_all.astype(BF16)
        dg_b = (dy_all * e_full).astype(BF16)
        dcum_mat = jnp.zeros((q, LANES), F32)
        rmat = jnp.zeros((LANES, q), F32)
        yoff_prod, bds_list, dx_list, dlast_parts = [], [], [], []
        for g in range(SSD_GROUPS):
            slab = slice(g * gw, (g + 1) * gw)
            bg = b_ref[:, g * n:(g + 1) * n].astype(BF16)
            cg = c_ref[:, g * n:(g + 1) * n].astype(BF16)
            cb = _dot_nt(cg, bg)
            st_g = st_ref[0, :, slab]
            st_b = st_g.astype(BF16)
            dsn = dstate[:, slab]
            dsn_b = dsn.astype(BF16)
            yoff_prod.append(dy_all[:, slab] * e_full[:, slab] * _dot(cg, st_b))
            dcg = _dot_nt(dg_b[:, slab], st_b)
            dstate[:, slab] = _dot_tn(cg, dg_b[:, slab]) + eend_full[:, slab] * dsn
            bds_list.append(_dot(bg, dsn_b))
            dbg = _dot_nt(xd_b[:, slab], dsn_b)
            dlast_parts.append(jnp.sum(dsn * st_g, axis=0, keepdims=True))
            dcb = jnp.zeros((q, q), F32)
            dx_pairs = []
            for pr in range(SSD_HPG // 2):
                h0 = g * SSD_HPG + 2 * pr
                ps = slice(h0 * p, (h0 + 2) * p)
                l0 = _ssd_decay(cum, cum_t, h0, causal)
                l1 = _ssd_decay(cum, cum_t, h0 + 1, causal)
                m0, m1 = cb * l0, cb * l1
                dyp = dy_b[:, ps]
                zero = jnp.zeros_like(dyp)
                dy0, dy1 = jnp.where(low, dyp, zero), jnp.where(low, zero, dyp)
                dm0 = _dot_nt(dy0, xdt_b[:, ps])
                dm1 = _dot_nt(dy1, xdt_b[:, ps])
                w0, w1 = dm0 * m0, dm1 * m1
                dcum_mat = (dcum_mat + jnp.sum(w0, axis=1, keepdims=True) * (lane == h0).astype(F32)
                            + jnp.sum(w1, axis=1, keepdims=True) * (lane == h0 + 1).astype(F32))
                rmat = (rmat + jnp.where(sub == h0, jnp.sum(w0, axis=0, keepdims=True), 0.0)
                        + jnp.where(sub == h0 + 1, jnp.sum(w1, axis=0, keepdims=True), 0.0))
                dcb = dcb + dm0 * l0 + dm1 * l1
                dx_pairs.append(_dot_tn(jnp.concatenate([m0.astype(BF16), m1.astype(BF16)], axis=0),
                                        jnp.concatenate([dy0, dy1], axis=0)))
            dx_list.append(jnp.concatenate(dx_pairs, axis=1))
            dcb_b = dcb.astype(BF16)
            dxbc_ref[:, di + g * n:di + (g + 1) * n] = dbg + _dot_tn(dcb_b, cg)
            dxbc_ref[:, di + SSD_GROUPS * n + g * n:di + SSD_GROUPS * n + (g + 1) * n] = dcg + _dot(dcb_b, bg)
        bds_all = jnp.concatenate(bds_list, axis=1)
        dx_all = jnp.concatenate(dx_list, axis=1) + dend_full * bds_all
        last_row = cum[q - 1:q, :]
        qv = _dot_hi(xdt * bds_all, collapse) * jnp.exp(last_row - cum)
        dcum_mat = dcum_mat + _dot_hi(jnp.concatenate(yoff_prod, axis=1), collapse) - qv
        dlast_full = jnp.broadcast_to(jnp.concatenate(dlast_parts, axis=1), (SUBLANES, di))
        dlast_row = _dot_hi(dlast_full, collapse)[0:1, :] * jnp.exp(last_row) + jnp.sum(qv, axis=0, keepdims=True)
        dcum_mat = dcum_mat + jnp.where(is_last, dlast_row, 0.0)
        dla = _dot_tn_hi(ltri, dcum_mat) - _dot_nt_hi((row <= col).astype(F32), rmat)
        ddt_mat = _dot_hi(dx_all * xs, collapse) + dla * a_row
        dxbc_ref[:, 0:di] = d_ref[...] * dy_all + dx_all * dt_full
        ddtr = ddt_mat * _sigmoid(dtr_v + bias_ref[...])
        ddtr_ref[...] = ddtr.astype(BF16)
        dd_full = jnp.broadcast_to(jnp.sum(dy_all * xs, axis=0, keepdims=True), (SUBLANES, di))
        dd_ref[...] += _dot_hi(dd_full, collapse)[0:1, :]
        dalog_ref[...] += jnp.sum(dla * dt, axis=0, keepdims=True) * a_row
        dbias_ref[...] += jnp.sum(ddtr, axis=0, keepdims=True)

    blk = lambda w, j: pl.BlockSpec((q, w), lambda c: (nc - 1 - c, j))
    row_out = lambda w: jax.ShapeDtypeStruct((1, w), F32)
    return pl.pallas_call(
        body, name=name, grid=(nc,),
        in_specs=[blk(di, 0), blk(di, 0), blk(di, 0), blk(di, 0), blk(512, 2), blk(512, 3), blk(LANES, 0),
                  pl.BlockSpec((1, n, di), lambda c: (nc - 1 - c, 0, 0)),
                  _full((1, LANES)), _full((1, LANES)), _full((1, di)), _full((1, di))],
        out_specs=[blk(di, 0), blk(CONV_DIM, 0), blk(LANES, 0),
                   _full((1, di)), _full((1, LANES)), _full((1, LANES)), _full((1, LANES))],
        out_shape=[jax.ShapeDtypeStruct((t, di), BF16), jax.ShapeDtypeStruct((t, CONV_DIM), F32),
                   jax.ShapeDtypeStruct((t, LANES), BF16), row_out(di), row_out(LANES), row_out(LANES), row_out(LANES)],
        scratch_shapes=[pltpu.VMEM((n, di), F32), pltpu.VMEM((q, di), F32)],
        compiler_params=_params("arbitrary"),
    )(dya, yssd, z, xbc, xbc, xbc, dtr, states, bias_row, alog_row, d_row, norm_g.reshape(1, di))


S5_BLOCKS = 4
S5_BC = S5_WIDTH // S5_BLOCKS
S5_BS = S5_N // S5_BLOCKS


def _s5_tables(ar, ai, reverse):
    if reverse:
        ai = -ai
    pows = [(ar, ai)]
    for _ in range(SUBLANES - 1):
        pr, pi = pows[-1]
        pows.append((pr * ar - pi * ai, pr * ai + pi * ar))
    row = lax.broadcasted_iota(jnp.int32, (SUBLANES, ar.shape[1]), 0)
    out = []
    for k in (1, 2, 4):
        pr, pi = pows[k - 1]
        mask = (row < SUBLANES - k) if reverse else (row >= k)
        out += [jnp.where(mask, pr, 0.0), jnp.where(mask, pi, 0.0)]
    order = list(range(SUBLANES))
    if reverse:
        order = order[::-1]
    out.append(jnp.concatenate([pows[e][0] for e in order], axis=0))
    out.append(jnp.concatenate([pows[e][1] for e in order], axis=0))
    return out


def _s5_scan(s_scr, tab, carry, tm, reverse):
    ngrp = tm // SUBLANES
    for lc in range(S5_BLOCKS):
        lr = slice(lc * S5_BS, (lc + 1) * S5_BS)
        li = slice(S5_N + lc * S5_BS, S5_N + (lc + 1) * S5_BS)
        tb = [tab[i, :, lr] for i in range(8)]

        def step(i, c, lr=lr, li=li, tb=tb):
            grp = (ngrp - 1 - i) if reverse else i
            r0 = pl.multiple_of(grp * SUBLANES, SUBLANES)
            xr = s_scr[pl.ds(r0, SUBLANES), lr]
            xi = s_scr[pl.ds(r0, SUBLANES), li]
            for lvl, k in enumerate((1, 2, 4)):
                sh = (SUBLANES - k) if reverse else k
                pr, pi = tb[2 * lvl], tb[2 * lvl + 1]
                rr = pltpu.roll(xr, sh, 0)
                ri = pltpu.roll(xi, sh, 0)
                xr, xi = xr + pr * rr - pi * ri, xi + pr * ri + pi * rr
            cr, ci = c
            xr, xi = xr + tb[6] * cr - tb[7] * ci, xi + tb[6] * ci + tb[7] * cr
            s_scr[pl.ds(r0, SUBLANES), lr] = xr
            s_scr[pl.ds(r0, SUBLANES), li] = xi
            e = 0 if reverse else SUBLANES - 1
            return xr[e:e + 1, :], xi[e:e + 1, :]

        cf = lax.fori_loop(0, ngrp, step, (carry[0:1, lr], carry[0:1, li]))
        carry[0:1, lr] = cf[0]
        carry[0:1, li] = cf[1]


def s5_fwd(u5, bbc, ccc, ab8, d_row, wglu, bglu_row, *, tm, name):
    t, w = u5.shape

    def body(u_ref, bb_ref, cc_ref, ab_ref, d_ref, wg_ref, bg_ref, s_ref, ys_ref, yb_ref, s_scr, tab, carry):
        i = pl.program_id(0)

        @pl.when(i == 0)
        def _():
            carry[...] = jnp.zeros_like(carry)
            for k, tv in enumerate(_s5_tables(ab_ref[0:1, :], ab_ref[1:2, :], False)):
                tab[k] = tv

        u = u_ref[...]
        ub = u.astype(BF16)
        for j in range(S5_BLOCKS):
            uj = ub[:, j * S5_BC:(j + 1) * S5_BC]
            bj = bb_ref[j * S5_BC:(j + 1) * S5_BC, :]
            s_scr[:, j * S5_BS:(j + 1) * S5_BS] = _dot(uj, bj[:, :S5_BS])
            s_scr[:, S5_N + j * S5_BS:S5_N + (j + 1) * S5_BS] = _dot(uj, bj[:, S5_BS:])
        _s5_scan(s_scr, tab, carry, tm, False)
        s = s_scr[...]
        s_ref[...] = s
        sb = s.astype(BF16)
        ys = []
        for j in range(S5_BLOCKS):
            cj = cc_ref[:, j * S5_BC:(j + 1) * S5_BC]
            ys.append(_dot(sb[:, j * S5_BS:(j + 1) * S5_BS], cj[:S5_BS, :])
                      + _dot(sb[:, S5_N + j * S5_BS:S5_N + (j + 1) * S5_BS], cj[S5_BS:, :]))
        y = jnp.concatenate(ys, axis=1) + d_ref[...] * u
        ys_ref[...] = y
        yb1 = _gelu(y)
        tt = _dot(yb1.astype(BF16), wg_ref[...]) + bg_ref[...]
        yb_ref[...] = (yb1 * _sigmoid(tt)).astype(BF16)

    tile = lambda c: pl.BlockSpec((tm, c), lambda i: (i, 0))
    return pl.pallas_call(
        body, name=name, grid=(t // tm,),
        in_specs=[tile(w), _full((w, 2 * S5_BS)), _full((2 * S5_BS, w)), _full((SUBLANES, S5_N)), _full((1, w)),
                  _full((w, w)), _full((1, w))],
        out_specs=[tile(2 * S5_N), tile(w), tile(w)],
        out_shape=[jax.ShapeDtypeStruct((t, 2 * S5_N), F32), jax.ShapeDtypeStruct((t, w), F32),
                   jax.ShapeDtypeStruct((t, w), BF16)],
        scratch_shapes=[pltpu.VMEM((tm, 2 * S5_N), F32), pltpu.VMEM((8, SUBLANES, S5_N), F32),
                        pltpu.VMEM((SUBLANES, 2 * S5_N), F32)],
        compiler_params=_params("arbitrary"),
    )(u5, bbc, ccc, ab8, d_row, wglu, bglu_row)


def s5_bwd(dyb, ys5, u5, s, bbc, ccc, ab8, d_row, wglu, bglu_row, *, tm, name):
    t, w = u5.shape
    nt = t // tm
    per = tm // SUBLANES

    def body(dyb_ref, ys_ref, u_ref, s_ref, sprev_ref, bb_ref, cc_ref, ab_ref, d_ref, wg_ref, bg_ref,
             du_ref, dbb_ref, dcc_ref, dab_ref, dd_ref, dwg_ref, dbg_ref, g_scr, tab, carry):
        i = pl.program_id(0)

        @pl.when(i == 0)
        def _():
            carry[...] = jnp.zeros_like(carry)
            for k, tv in enumerate(_s5_tables(ab_ref[0:1, :], ab_ref[1:2, :], True)):
                tab[k] = tv
            for r in (dbb_ref, dcc_ref, dab_ref, dd_ref, dwg_ref, dbg_ref):
                r[...] = jnp.zeros_like(r)

        y = ys_ref[...]
        u = u_ref[...]
        yb1 = _gelu(y)
        yb1b = yb1.astype(BF16)
        sg = _sigmoid(_dot(yb1b, wg_ref[...]) + bg_ref[...])
        dybv = dyb_ref[...]
        dtt = dybv * yb1 * sg * (1.0 - sg)
        dttb = dtt.astype(BF16)
        dyb1 = dybv * sg + _dot_nt(dttb, wg_ref[...])
        dwg_ref[...] += _dot_tn(yb1b, dttb)
        dbg_ref[...] += jnp.sum(dtt, axis=0, keepdims=True)
        dy = dyb1 * _gelu_grad(y)
        dd_ref[...] += jnp.sum(dy * u, axis=0, keepdims=True)
        dyh = dy.astype(BF16)
        for j in range(S5_BLOCKS):
            cj = cc_ref[:, j * S5_BC:(j + 1) * S5_BC]
            dyj = dyh[:, j * S5_BC:(j + 1) * S5_BC]
            g_scr[:, j * S5_BS:(j + 1) * S5_BS] = _dot_nt(dyj, cj[:S5_BS, :])
            g_scr[:, S5_N + j * S5_BS:S5_N + (j + 1) * S5_BS] = _dot_nt(dyj, cj[S5_BS:, :])
        _s5_scan(g_scr, tab, carry, tm, True)
        gs = g_scr[...]
        gb = gs.astype(BF16)
        sv = s_ref[...]
        sb = sv.astype(BF16)
        ub = u.astype(BF16)
        dus = []
        for j in range(S5_BLOCKS):
            rs = slice(j * S5_BS, (j + 1) * S5_BS)
            ims = slice(S5_N + j * S5_BS, S5_N + (j + 1) * S5_BS)
            cs = slice(j * S5_BC, (j + 1) * S5_BC)
            bj = bb_ref[cs, :]
            dus.append(_dot_nt(gb[:, rs], bj[:, :S5_BS]) + _dot_nt(gb[:, ims], bj[:, S5_BS:]))
            dbb_ref[cs, :S5_BS] += _dot_tn(ub[:, cs], gb[:, rs])
            dbb_ref[cs, S5_BS:] += _dot_tn(ub[:, cs], gb[:, ims])
            dcc_ref[:S5_BS, cs] += _dot_tn(sb[:, rs], dyh[:, cs])
            dcc_ref[S5_BS:, cs] += _dot_tn(sb[:, ims], dyh[:, cs])
        du_ref[...] = (jnp.concatenate(dus, axis=1) + d_ref[...] * dy).astype(BF16)
        first = lax.broadcasted_iota(jnp.int32, (tm, 1), 0) == 0
        before = jnp.where(i == nt - 1, 0.0, sprev_ref[SUBLANES - 1:SUBLANES, :])
        sp = jnp.where(first, before, pltpu.roll(sv, 1, 0))
        g_re, g_im = gs[:, :S5_N], gs[:, S5_N:]
        sp_re, sp_im = sp[:, :S5_N], sp[:, S5_N:]
        dab_ref[0:1, :] += jnp.sum(g_re * sp_re + g_im * sp_im, axis=0, keepdims=True)
        dab_ref[1:2, :] += jnp.sum(g_im * sp_re - g_re * sp_im, axis=0, keepdims=True)

    tile = lambda c: pl.BlockSpec((tm, c), lambda i: (nt - 1 - i, 0))
    acc = lambda r, c: jax.ShapeDtypeStruct((r, c), F32)
    return pl.pallas_call(
        body, name=name, grid=(nt,),
        in_specs=[tile(w), tile(w), tile(w), tile(2 * S5_N),
                  pl.BlockSpec((SUBLANES, 2 * S5_N), lambda i: (jnp.maximum((nt - 1 - i) * per - 1, 0), 0)),
                  _full((w, 2 * S5_BS)), _full((2 * S5_BS, w)), _full((SUBLANES, S5_N)), _full((1, w)),
                  _full((w, w)), _full((1, w))],
        out_specs=[tile(w), _full((w, 2 * S5_BS)), _full((2 * S5_BS, w)), _full((SUBLANES, S5_N)), _full((1, w)),
                   _full((w, w)), _full((1, w))],
        out_shape=[jax.ShapeDtypeStruct((t, w), BF16), acc(w, 2 * S5_BS), acc(2 * S5_BS, w), acc(SUBLANES, S5_N),
                   acc(1, w), acc(w, w), acc(1, w)],
        scratch_shapes=[pltpu.VMEM((tm, 2 * S5_N), F32), pltpu.VMEM((8, SUBLANES, S5_N), F32),
                        pltpu.VMEM((SUBLANES, 2 * S5_N), F32)],
        compiler_params=_params("arbitrary"),
    )(dyb, ys5, u5, s, s, bbc, ccc, ab8, d_row, wglu, bglu_row)


def s5_discretize(a_re, a_im, log_dt, b_re, b_im, *, name):
    n = a_re.shape[0]

    def body(ar_ref, ai_ref, dt_ref, br_ref, bi_ref, ab_ref, bbr_ref, bbi_ref):
        ar, ai, dtv = ar_ref[...], ai_ref[...], jnp.exp(dt_ref[...])
        mag = jnp.exp(ar * dtv)
        abr = mag * jnp.cos(ai * dtv)
        abi = mag * jnp.sin(ai * dtv)
        den = ar * ar + ai * ai
        nr = abr - 1.0
        cr = (nr * ar + abi * ai) / den
        ci = (abi * ar - nr * ai) / den
        ab_ref[:, 0:1] = abr
        ab_ref[:, 1:2] = abi
        br, bi = br_ref[...], bi_ref[...]
        bbr_ref[...] = cr * br - ci * bi
        bbi_ref[...] = cr * bi + ci * br

    col = jax.ShapeDtypeStruct((n, 2), F32)
    mat = jax.ShapeDtypeStruct((n, S5_GROUP), F32)
    return pl.pallas_call(body, name=name, out_shape=[col, mat, mat])(a_re, a_im, log_dt, b_re, b_im)


def s5_discretize_bwd(a_re, a_im, log_dt, b_re, b_im, dab, dbb_re, dbb_im, *, name):
    n = a_re.shape[0]

    def body(ar_ref, ai_ref, dt_ref, br_ref, bi_ref, dab_ref, dbr_ref, dbi_ref, da_ref, ddt_ref, gbr_ref, gbi_ref):
        ar, ai, dtv = ar_ref[...], ai_ref[...], jnp.exp(dt_ref[...])
        mag = jnp.exp(ar * dtv)
        abr = mag * jnp.cos(ai * dtv)
        abi = mag * jnp.sin(ai * dtv)
        den = ar * ar + ai * ai
        nr = abr - 1.0
        cr = (nr * ar + abi * ai) / den
        ci = (abi * ar - nr * ai) / den
        br, bi = br_ref[...], bi_ref[...]
        dbr, dbi = dbr_ref[...], dbi_ref[...]
        gbr_ref[...] = cr * dbr + ci * dbi
        gbi_ref[...] = cr * dbi - ci * dbr
        gcr = jnp.sum(dbr * br + dbi * bi, axis=1, keepdims=True)
        gci = jnp.sum(dbi * br - dbr * bi, axis=1, keepdims=True)
        ilr, ili = ar / den, -ai / den
        gabr = dab_ref[:, 0:1] + (ilr * gcr + ili * gci)
        gabi = dab_ref[:, 1:2] + (ilr * gci - ili * gcr)
        t1r, t1i = dtv * abr, dtv * abi
        t2r, t2i = -(cr * ilr - ci * ili), -(cr * ili + ci * ilr)
        da_ref[:, 0:1] = (t1r * gabr + t1i * gabi) + (t2r * gcr + t2i * gci)
        da_ref[:, 1:2] = (t1r * gabi - t1i * gabr) + (t2r * gci - t2i * gcr)
        lr, li = ar * abr - ai * abi, ar * abi + ai * abr
        dlog = (lr * gabr + li * gabi) * dtv
        ddt_ref[...] = jnp.sum(dlog.reshape(S5_GROUPS, S5_STATE, 1), axis=1)

    col2 = jax.ShapeDtypeStruct((n, 2), F32)
    col1 = jax.ShapeDtypeStruct((S5_GROUPS, 1), F32)
    mat = jax.ShapeDtypeStruct((n, S5_GROUP), F32)
    return pl.pallas_call(body, name=name, out_shape=[col2, col1, mat, mat])(
        a_re, a_im, log_dt, b_re, b_im, dab, dbb_re, dbb_im)


def merge_fwd(ya, yb, graw, x, wba, wbb, wout, *, tm, name):
    t, d = x.shape

    def body(ya_ref, yb_ref, g_ref, x_ref, wba_ref, wbb_ref, wo_ref, x1_ref, pa_ref, pb_ref):
        pa = _dot(ya_ref[...], wba_ref[...])
        pb = _dot(yb_ref[...], wbb_ref[...])
        gate = _sigmoid(g_ref[...])
        merged = gate[:, :d] * pa + gate[:, d:] * pb
        x1_ref[...] = x_ref[...] + _dot(merged.astype(BF16), wo_ref[...])
        pa_ref[...] = pa
        pb_ref[...] = pb

    tile = lambda c: pl.BlockSpec((tm, c), lambda i: (i, 0))
    out = jax.ShapeDtypeStruct((t, d), F32)
    return pl.pallas_call(
        body, name=name, grid=(t // tm,),
        in_specs=[tile(d), tile(S5_WIDTH), tile(2 * d), tile(d), _full((d, d)), _full((S5_WIDTH, d)), _full((d, d))],
        out_specs=[tile(d), tile(d), tile(d)], out_shape=[out, out, out],
        compiler_params=_params("parallel"),
    )(ya, yb, graw, x, wba, wbb, wout)


def merge_bwd(dx1b, graw, pa, pb, wba, wbb, wout, *, tm, name):
    t, d = pa.shape

    def body(dx_ref, g_ref, pa_ref, pb_ref, wba_ref, wbb_ref, wo_ref,
             mg_ref, dg_ref, dpa_ref, dpb_ref, dya_ref, dyb_ref):
        dm = _dot_nt(dx_ref[...], wo_ref[...])
        gate = _sigmoid(g_ref[...])
        g0, g1 = gate[:, :d], gate[:, d:]
        pa, pb = pa_ref[...], pb_ref[...]
        mg_ref[...] = (g0 * pa + g1 * pb).astype(BF16)
        dg_ref[:, :d] = (dm * pa * g0 * (1.0 - g0)).astype(BF16)
        dg_ref[:, d:] = (dm * pb * g1 * (1.0 - g1)).astype(BF16)
        dpa = (dm * g0).astype(BF16)
        dpb = (dm * g1).astype(BF16)
        dpa_ref[...] = dpa
        dpb_ref[...] = dpb
        dya_ref[...] = _dot_nt(dpa, wba_ref[...])
        dyb_ref[...] = _dot_nt(dpb, wbb_ref[...])

    tile = lambda c: pl.BlockSpec((tm, c), lambda i: (i, 0))
    sds = jax.ShapeDtypeStruct
    return pl.pallas_call(
        body, name=name, grid=(t // tm,),
        in_specs=[tile(d), tile(2 * d), tile(d), tile(d), _full((d, d)), _full((S5_WIDTH, d)), _full((d, d))],
        out_specs=[tile(d), tile(2 * d), tile(d), tile(d), tile(d), tile(S5_WIDTH)],
        out_shape=[sds((t, d), BF16), sds((t, 2 * d), BF16), sds((t, d), BF16), sds((t, d), BF16),
                   sds((t, d), F32), sds((t, S5_WIDTH), F32)],
        compiler_params=_params("parallel"),
    )(dx1b, graw, pa, pb, wba, wbb, wout)


def mlp_out_loss(a1, x1, w2, gf, target, *, tm, name):
    t, d = x1.shape
    f = a1.shape[1]

    def body(a_ref, x_ref, w_ref, g_ref, tg_ref, act_ref, dx_ref, dxb_ref, loss_ref, dg_ref):
        i = pl.program_id(0)
        av = jnp.maximum(a_ref[...], 0.0)
        act = (av * av).astype(BF16)
        act_ref[...] = act
        x2 = x_ref[...] + _dot(act, w_ref[...])
        r = lax.rsqrt(jnp.mean(x2 * x2, axis=-1, keepdims=True) + EPS)
        xn = x2 * r
        err = xn * g_ref[...] - tg_ref[...]
        dyf = err * (1.0 / d)
        dxn = dyf * g_ref[...]
        dx = r * (dxn - xn * jnp.mean(dxn * xn, axis=-1, keepdims=True))
        dx_ref[...] = dx
        dxb_ref[...] = dx.astype(BF16)

        @pl.when(i == 0)
        def _():
            loss_ref[...] = jnp.zeros_like(loss_ref)
            dg_ref[...] = jnp.zeros_like(dg_ref)

        loss_ref[...] += jnp.sum(err * err) * (0.5 / d)
        dg_ref[...] += jnp.sum(dyf * xn, axis=0, keepdims=True)

    tile = lambda c: pl.BlockSpec((tm, c), lambda i: (i, 0))
    sds = jax.ShapeDtypeStruct
    return pl.pallas_call(
        body, name=name, grid=(t // tm,),
        in_specs=[tile(f), tile(d), _full((f, d)), _full((1, d)), tile(d)],
        out_specs=[tile(f), tile(d), tile(d), _full((1, LANES)), _full((1, d))],
        out_shape=[sds((t, f), BF16), sds((t, d), F32), sds((t, d), BF16), sds((1, LANES), F32), sds((1, d), F32)],
        compiler_params=_params("arbitrary"),
    )(a1, x1, w2, gf.reshape(1, d), target)


def mlp_bwd_act(dx2b, a1, w2, *, tm, name):
    t, f = a1.shape
    d = dx2b.shape[1]

    def body(dx_ref, a_ref, w_ref, o_ref):
        dact = _dot_nt(dx_ref[...], w_ref[...])
        o_ref[...] = (dact * 2.0 * jnp.maximum(a_ref[...], 0.0)).astype(BF16)

    tile = lambda c: pl.BlockSpec((tm, c), lambda i: (i, 0))
    return pl.pallas_call(
        body, name=name, grid=(t // tm,),
        in_specs=[tile(d), tile(f), _full((f, d))], out_specs=tile(f),
        out_shape=jax.ShapeDtypeStruct((t, f), BF16),
        compiler_params=_params("parallel"),
    )(dx2b, a1, w2)


def _row_tile(rows, cap):
    best = SUBLANES
    for c in range(SUBLANES, cap + 1, SUBLANES):
        if rows % c == 0:
            best = c
    return best


ADAM_ROW_TILE = 512


def adamw_packed(w, m, v, gparts, *, name):
    rows = w.shape[0]
    tr = _row_tile(rows, ADAM_ROW_TILE)

    def body(w_ref, m_ref, v_ref, g_ref, go_ref, d_ref, mo_ref, vo_ref):
        g = g_ref[0].astype(F32)
        for k in range(1, N_DEV):
            g = g + g_ref[k].astype(F32)
        mn = ADAM_B1 * m_ref[...] + (1.0 - ADAM_B1) * g
        vn = ADAM_B2 * v_ref[...] + (1.0 - ADAM_B2) * (g * g)
        m_hat = mn / (1.0 - ADAM_B1 ** ADAM_STEP)
        v_hat = vn / (1.0 - ADAM_B2 ** ADAM_STEP)
        go_ref[...] = g
        d_ref[...] = -ADAM_LR * (m_hat / (jnp.sqrt(v_hat) + ADAM_EPS) + ADAM_WD * w_ref[...])
        mo_ref[...] = mn
        vo_ref[...] = vn

    tile = pl.BlockSpec((tr, LANES), lambda i: (i, 0))
    out = jax.ShapeDtypeStruct((rows, LANES), F32)
    return pl.pallas_call(
        body, name=name, grid=(rows // tr,),
        in_specs=[tile, tile, tile, pl.BlockSpec((N_DEV, tr, LANES), lambda i: (0, i, 0))],
        out_specs=[tile, tile, tile, tile], out_shape=[out, out, out, out],
        compiler_params=_params("parallel"),
    )(w, m, v, gparts)


_MESH = pl.DeviceIdType.MESH
_RELATIONS = [(dx, dy, dc) for dx in (0, 1) for dy in (0, 1) for dc in (0, 1)][1:]


def _place():
    x, y, c = lax.axis_index("x"), lax.axis_index("y"), lax.axis_index("c")
    return x, y, c, 4 * x + 2 * y + c


def all_gather_blocks(block, *, name):
    rows, lanes = block.shape

    def body(x_ref, out_ref, send_sems, recv_sems, local_sem):
        x, y, c, _ = _place()
        me, sibling = (x, y, c), (x, y, 1 - c)
        chips = [(1 - x, y), (x, 1 - y), (1 - x, 1 - y)]

        def slot(px, py, pc):
            return out_ref.at[4 * px + 2 * py + pc]

        def copy(k, blk, to, src=None):
            return pltpu.make_async_remote_copy(
                src_ref=slot(*blk) if src is None else src, dst_ref=slot(*blk),
                send_sem=send_sems.at[k], recv_sem=recv_sems.at[k], device_id=to, device_id_type=_MESH)

        mine = pltpu.make_async_copy(x_ref, slot(*me), local_sem)
        mine.start()
        first = [copy(0, me, sibling, src=x_ref)]
        first += [copy(1 + j, me, (*chip, c), src=x_ref) for j, chip in enumerate(chips)]
        for cp in first:
            cp.start()
        passed = [copy(4 + j, (*chip, c), sibling) for j, chip in enumerate(chips)]
        for j, chip in enumerate(chips):
            copy(1 + j, (*chip, c), me).wait_recv()
            passed[j].start()
        copy(0, sibling, me).wait_recv()
        for j, chip in enumerate(chips):
            copy(4 + j, (*chip, 1 - c), me).wait_recv()
        for cp in first + passed:
            cp.wait_send()
        mine.wait()

    return pl.pallas_call(
        body, name=name,
        in_specs=[pl.BlockSpec(memory_space=pl.ANY)], out_specs=pl.BlockSpec(memory_space=pl.ANY),
        out_shape=jax.ShapeDtypeStruct((N_DEV, rows, lanes), block.dtype),
        scratch_shapes=[pltpu.SemaphoreType.DMA((7,)), pltpu.SemaphoreType.DMA((7,)), pltpu.SemaphoreType.DMA],
        compiler_params=pltpu.CompilerParams(has_side_effects=True),
    )(block)


def exchange_partials(big, small, *, name):
    n_rel = len(_RELATIONS)

    def body(b_ref, s_ref, bo_ref, so_ref, send_sems, recv_sems, local_sems):
        x, y, c, idx = _place()
        mine = [pltpu.make_async_copy(b_ref.at[idx], bo_ref.at[idx], local_sems.at[0]),
                pltpu.make_async_copy(s_ref.at[idx], so_ref.at[idx], local_sems.at[1])]
        for cp in mine:
            cp.start()
        copies = []
        for k, (dx, dy, dc) in enumerate(_RELATIONS):
            px, py, pc = x ^ dx, y ^ dy, c ^ dc
            for j, (src, dst) in enumerate(((b_ref, bo_ref), (s_ref, so_ref))):
                copies.append(pltpu.make_async_remote_copy(
                    src_ref=src.at[4 * px + 2 * py + pc], dst_ref=dst.at[idx],
                    send_sem=send_sems.at[j * n_rel + k], recv_sem=recv_sems.at[j * n_rel + k],
                    device_id=(px, py, pc), device_id_type=_MESH))
        for cp in copies:
            cp.start()
        for cp in copies:
            cp.wait_recv()
        for cp in copies:
            cp.wait_send()
        for cp in mine:
            cp.wait()

    any_spec = pl.BlockSpec(memory_space=pl.ANY)
    return pl.pallas_call(
        body, name=name, in_specs=[any_spec, any_spec], out_specs=[any_spec, any_spec],
        out_shape=[jax.ShapeDtypeStruct(big.shape, big.dtype), jax.ShapeDtypeStruct(small.shape, small.dtype)],
        scratch_shapes=[pltpu.SemaphoreType.DMA((2 * n_rel,)), pltpu.SemaphoreType.DMA((2 * n_rel,)),
                        pltpu.SemaphoreType.DMA((2,))],
        compiler_params=pltpu.CompilerParams(has_side_effects=True),
    )(big, small)


def _lane_row(v):
    return jnp.pad(v.astype(F32), (0, LANES - v.shape[0])).reshape(1, LANES)


def _block_diag_b(bb):
    eye = jnp.eye(SUBLANES, dtype=F32)
    bt = bb.reshape(S5_BLOCKS, 8, S5_STATE, S5_GROUP).transpose(0, 1, 3, 2)
    return (bt[:, :, :, None, :] * eye[None, :, None, :, None]).reshape(S5_WIDTH, S5_BS)


def _block_diag_c(cc):
    eye = jnp.eye(SUBLANES, dtype=F32)
    ct = cc.reshape(S5_BLOCKS, 8, S5_GROUP, S5_STATE).transpose(3, 0, 1, 2)
    return (ct[None, :, :, :, :] * eye[:, None, None, :, None]).reshape(S5_BS, S5_WIDTH)


def _diag_blocks_b(m):
    eye = jnp.eye(SUBLANES, dtype=F32)
    m5 = m.reshape(S5_BLOCKS, 8, S5_GROUP, 8, S5_STATE)
    return jnp.sum(m5 * eye[None, :, None, :, None], axis=3).transpose(0, 1, 3, 2).reshape(S5_GROUPS, S5_STATE, S5_GROUP)


def _diag_blocks_c(m):
    eye = jnp.eye(SUBLANES, dtype=F32)
    m5 = m.reshape(8, S5_STATE, S5_BLOCKS, 8, S5_GROUP)
    return jnp.sum(m5 * eye[:, None, None, :, None], axis=0).transpose(1, 2, 3, 0).reshape(S5_GROUPS, S5_GROUP, S5_STATE)


def local_step(x, target, p, win_p, conv_w8, wglu, wba, wbb, wout, w1, w2):
    t = x.shape[0]
    tm = min(256, t)
    d = D_MODEL
    h, xbc_raw, graw, z, u5, dtr = rms_matmul(x, p["norm_mix_g"], win_p, IN_SPLITS, tm=tm, name="in_proj")
    xbc = conv_silu_fwd(xbc_raw, conv_w8, p["conv_b"], tm=tm, name="conv_fwd")
    bias_row, alog_row = _lane_row(p["dt_bias"]), _lane_row(p["a_log"])
    d_row = jnp.repeat(p["d_ssd"], SSD_HEADDIM).reshape(1, SSD_HEADS * SSD_HEADDIM)
    ya, yssd, states = ssd_fwd(xbc, dtr, z, bias_row, alog_row, d_row, p["ssd_norm_g"], name="ssd_fwd")

    n = S5_N
    a_re_c, a_im_c = p["s5_a_re"].reshape(n, 1), p["s5_a_im"].reshape(n, 1)
    dt_c = jnp.repeat(p["s5_log_dt"], S5_STATE).reshape(n, 1)
    b_re_m, b_im_m = p["s5_b_re"].reshape(n, S5_GROUP), p["s5_b_im"].reshape(n, S5_GROUP)
    ab, bb_re, bb_im = s5_discretize(a_re_c, a_im_c, dt_c, b_re_m, b_im_m, name="s5_disc")
    ab8 = jnp.pad(ab.T, ((0, SUBLANES - 2), (0, 0)))
    bbc = jnp.concatenate([_block_diag_b(bb_re.reshape(S5_GROUPS, S5_STATE, S5_GROUP)),
                           _block_diag_b(bb_im.reshape(S5_GROUPS, S5_STATE, S5_GROUP))], axis=1).astype(BF16)
    ccc = jnp.concatenate([_block_diag_c(p["s5_c_re"]), -_block_diag_c(p["s5_c_im"])], axis=0).astype(BF16)
    s5d_row = p["s5_d"].reshape(1, S5_WIDTH)
    bglu_row = p["s5_glu_b"].reshape(1, S5_WIDTH)
    s, ys5, yb = s5_fwd(u5, bbc, ccc, ab8, s5d_row, wglu, bglu_row, tm=tm, name="s5_fwd")

    x1, pa, pb = merge_fwd(ya, yb, graw, x, wba, wbb, wout, tm=tm, name="merge_fwd")
    h2, a1 = rms_matmul(x1, p["norm_mlp_g"], w1, (D_FF,), tm=tm, name="mlp_in")
    act, dx2, dx2b, loss_row, dgf = mlp_out_loss(a1, x1, w2, p["norm_final_g"], target, tm=tm, name="mlp_out_loss")

    g = {}
    g["norm_final_g"] = dgf[0]
    da1 = mlp_bwd_act(dx2b, a1, w2, tm=tm, name="mlp_bwd_act")
    g["w_mlp_out"] = matmul_tn(act, dx2b, name="dw_mlp_out")
    g["w_mlp_in"] = matmul_tn(h2, da1, name="dw_mlp_in")
    dx1, dx1b, dgm = matmul_nt_rms_bwd(da1, w1, x1, p["norm_mlp_g"], dx2, tm=tm, name="mlp_in_bwd")
    g["norm_mlp_g"] = dgm[0]
    merged, dgraw, dpa, dpb, dya, dyb = merge_bwd(dx1b, graw, pa, pb, wba, wbb, wout, tm=tm, name="merge_bwd")
    g["w_out"] = matmul_tn(merged, dx1b, name="dw_out")
    g["w_branch"] = jnp.concatenate([matmul_tn(ya, dpa, name="dw_branch_a"), matmul_tn(yb, dpb, name="dw_branch_b")], axis=0)

    du5, dbbc, dccc, dab, dd5, dwglu, dbglu = s5_bwd(dyb, ys5, u5, s, bbc, ccc, ab8, s5d_row, wglu, bglu_row,
                                                     tm=min(128, t), name="s5_bwd")
    g["s5_glu_w"], g["s5_glu_b"], g["s5_d"] = dwglu, dbglu[0], dd5[0]
    g["s5_c_re"] = _diag_blocks_c(dccc[:S5_BS])
    g["s5_c_im"] = -_diag_blocks_c(dccc[S5_BS:])
    dbb_re = _diag_blocks_b(dbbc[:, :S5_BS]).reshape(n, S5_GROUP)
    dbb_im = _diag_blocks_b(dbbc[:, S5_BS:]).reshape(n, S5_GROUP)
    da, dlogdt, gb_re, gb_im = s5_discretize_bwd(a_re_c, a_im_c, dt_c, b_re_m, b_im_m, dab[:2].T, dbb_re, dbb_im,
                                                  name="s5_disc_bwd")
    g["s5_a_re"] = da[:, 0].reshape(S5_GROUPS, S5_STATE)
    g["s5_a_im"] = da[:, 1].reshape(S5_GROUPS, S5_STATE)
    g["s5_log_dt"] = dlogdt[:, 0]
    g["s5_b_re"] = gb_re.reshape(S5_GROUPS, S5_STATE, S5_GROUP)
    g["s5_b_im"] = gb_im.reshape(S5_GROUPS, S5_STATE, S5_GROUP)

    dz, dxbc, ddtr, dng, dd_row, dalog_row, dbias_row = ssd_bwd(
        dya, yssd, z, xbc, dtr, states, bias_row, alog_row, d_row, p["ssd_norm_g"], name="ssd_bwd")
    g["ssd_norm_g"] = dng[0]
    g["d_ssd"], g["a_log"], g["dt_bias"] = dd_row[0, :SSD_HEADS], dalog_row[0, :SSD_HEADS], dbias_row[0, :SSD_HEADS]
    dxc, dconv_b, dconv_w8 = conv_silu_bwd_a(xbc_raw, dxbc, conv_w8, p["conv_b"], tm=tm, name="conv_bwd_a")
    g["conv_b"], g["conv_w"] = dconv_b[0], dconv_w8[:SSD_CONV]
    dxbc_raw = conv_bwd_b(dxc, conv_w8, tm=tm, name="conv_bwd_b")
    dproj = jnp.concatenate([dxbc_raw, dgraw, dz, du5, ddtr], axis=1)
    g["w_in_p"] = matmul_tn(h, dproj, name="dw_in")
    grad_x, _, dgx = matmul_nt_rms_bwd(dproj, win_p, x, p["norm_mix_g"], dx1, tm=tm, name="in_proj_bwd")
    g["norm_mix_g"] = dgx[0]
    return loss_row[0, 0], grad_x, g


WEIGHT_ORDER = ["norm_mix_g", "w_in", "conv_w", "conv_b", "dt_bias", "a_log", "d_ssd", "ssd_norm_g", "s5_a_re",
                "s5_a_im", "s5_log_dt", "s5_b_re", "s5_b_im", "s5_c_re", "s5_c_im", "s5_d", "s5_glu_w", "s5_glu_b",
                "w_branch", "w_out", "norm_mlp_g", "w_mlp_in", "w_mlp_out", "norm_final_g"]
SHARDED = {"w_in": ((D_MODEL, D_IN_PROJ), 1), "conv_w": ((SSD_CONV, CONV_DIM), 1), "s5_glu_w": ((S5_WIDTH, S5_WIDTH), 0),
           "w_branch": ((D_MODEL + S5_WIDTH, D_MODEL), 0), "w_out": ((D_MODEL, D_MODEL), 0),
           "w_mlp_in": ((D_MODEL, D_FF), 1), "w_mlp_out": ((D_FF, D_MODEL), 0)}
SHARDED_ORDER = ["w_in", "conv_w", "s5_glu_w", "w_branch", "w_out", "w_mlp_in", "w_mlp_out"]
SMALL_ORDER = [n for n in WEIGHT_ORDER if n not in SHARDED]


def _shard_shape(name):
    (r, c), ax = SHARDED[name]
    return (r, c // N_DEV) if ax == 1 else (r // N_DEV, c)


def _rows_of(a):
    return a.reshape(-1, LANES)


def _split_shards(name, full):
    (r, c), ax = SHARDED[name]
    if ax == 1:
        full = full.reshape(r, N_DEV, c // N_DEV).transpose(1, 0, 2)
    return full.reshape(N_DEV, -1, LANES)


def _join_shards(name, stacked):
    (r, c), ax = SHARDED[name]
    if ax == 1:
        return stacked.transpose(1, 0, 2).reshape(r, c)
    return stacked.reshape(r, c)


def _pack_small(values, loss):
    flat = jnp.concatenate([values[n].reshape(-1).astype(F32) for n in SMALL_ORDER] + [loss.reshape(1).astype(F32)])
    rows = -(-flat.shape[0] // (LANES * SUBLANES)) * SUBLANES
    return jnp.pad(flat, (0, rows * LANES - flat.shape[0])).reshape(rows, LANES)


def _unpack_small(rows, shapes):
    flat = rows.reshape(-1)
    out, off = {}, 0
    for n in SMALL_ORDER:
        size = math.prod(shapes[n])
        out[n] = flat[off:off + size].reshape(shapes[n])
        off += size
    return out, flat[off]


def kernel(x, norm_mix_g, w_in, conv_w, conv_b, dt_bias, a_log, d_ssd, ssd_norm_g, s5_a_re, s5_a_im, s5_log_dt, s5_b_re, s5_b_im, s5_c_re, s5_c_im, s5_d, s5_glu_w, s5_glu_b, w_branch, w_out, norm_mlp_g, w_mlp_in, w_mlp_out, norm_final_g, loss_target, m_norm_mix_g, m_w_in, m_conv_w, m_conv_b, m_dt_bias, m_a_log, m_d_ssd, m_ssd_norm_g, m_s5_a_re, m_s5_a_im, m_s5_log_dt, m_s5_b_re, m_s5_b_im, m_s5_c_re, m_s5_c_im, m_s5_d, m_s5_glu_w, m_s5_glu_b, m_w_branch, m_w_out, m_norm_mlp_g, m_w_mlp_in, m_w_mlp_out, m_norm_final_g, v_norm_mix_g, v_w_in, v_conv_w, v_conv_b, v_dt_bias, v_a_log, v_d_ssd, v_ssd_norm_g, v_s5_a_re, v_s5_a_im, v_s5_log_dt, v_s5_b_re, v_s5_b_im, v_s5_c_re, v_s5_c_im, v_s5_d, v_s5_glu_w, v_s5_glu_b, v_w_branch, v_w_out, v_norm_mlp_g, v_w_mlp_in, v_w_mlp_out, v_norm_final_g):
    w = dict(norm_mix_g=norm_mix_g, w_in=w_in, conv_w=conv_w, conv_b=conv_b, dt_bias=dt_bias, a_log=a_log, d_ssd=d_ssd,
             ssd_norm_g=ssd_norm_g, s5_a_re=s5_a_re, s5_a_im=s5_a_im, s5_log_dt=s5_log_dt, s5_b_re=s5_b_re,
             s5_b_im=s5_b_im, s5_c_re=s5_c_re, s5_c_im=s5_c_im, s5_d=s5_d, s5_glu_w=s5_glu_w, s5_glu_b=s5_glu_b,
             w_branch=w_branch, w_out=w_out, norm_mlp_g=norm_mlp_g, w_mlp_in=w_mlp_in, w_mlp_out=w_mlp_out,
             norm_final_g=norm_final_g)
    m = dict(norm_mix_g=m_norm_mix_g, w_in=m_w_in, conv_w=m_conv_w, conv_b=m_conv_b, dt_bias=m_dt_bias, a_log=m_a_log,
             d_ssd=m_d_ssd, ssd_norm_g=m_ssd_norm_g, s5_a_re=m_s5_a_re, s5_a_im=m_s5_a_im, s5_log_dt=m_s5_log_dt,
             s5_b_re=m_s5_b_re, s5_b_im=m_s5_b_im, s5_c_re=m_s5_c_re, s5_c_im=m_s5_c_im, s5_d=m_s5_d,
             s5_glu_w=m_s5_glu_w, s5_glu_b=m_s5_glu_b, w_branch=m_w_branch, w_out=m_w_out, norm_mlp_g=m_norm_mlp_g,
             w_mlp_in=m_w_mlp_in, w_mlp_out=m_w_mlp_out, norm_final_g=m_norm_final_g)
    v = dict(norm_mix_g=v_norm_mix_g, w_in=v_w_in, conv_w=v_conv_w, conv_b=v_conv_b, dt_bias=v_dt_bias, a_log=v_a_log,
             d_ssd=v_d_ssd, ssd_norm_g=v_ssd_norm_g, s5_a_re=v_s5_a_re, s5_a_im=v_s5_a_im, s5_log_dt=v_s5_log_dt,
             s5_b_re=v_s5_b_re, s5_b_im=v_s5_b_im, s5_c_re=v_s5_c_re, s5_c_im=v_s5_c_im, s5_d=v_s5_d,
             s5_glu_w=v_s5_glu_w, s5_glu_b=v_s5_glu_b, w_branch=v_w_branch, w_out=v_w_out, norm_mlp_g=v_norm_mlp_g,
             w_mlp_in=v_w_mlp_in, w_mlp_out=v_w_mlp_out, norm_final_g=v_norm_final_g)

    def payload(name):
        a = w[name]
        if name == "conv_w":
            return lax.bitcast_convert_type(a, BF16).reshape(-1)
        return a.astype(BF16).reshape(-1)

    sizes = [(2 if n == "conv_w" else 1) * math.prod(_shard_shape(n)) for n in SHARDED_ORDER]
    block = jnp.concatenate([payload(n) for n in SHARDED_ORDER]).reshape(-1, LANES)
    gathered = all_gather_blocks(block, name="gather_weights").reshape(N_DEV, -1)
    full, off = {}, 0
    for n, size in zip(SHARDED_ORDER, sizes):
        piece = gathered[:, off:off + size]
        off += size
        if n == "conv_w":
            piece = lax.bitcast_convert_type(piece.reshape(N_DEV, -1, 2), F32)
        full[n] = _join_shards(n, piece.reshape((N_DEV,) + _shard_shape(n)))
    wi = full["w_in"]
    win_p = jnp.concatenate([wi[:, 1024:3072], wi[:, 3600:5648], wi[:, 0:1024], wi[:, 3088:3600], wi[:, 3072:3088],
                             jnp.zeros((D_MODEL, LANES - SSD_HEADS), BF16)], axis=1)
    conv_w8 = jnp.pad(full["conv_w"], ((0, SUBLANES - SSD_CONV), (0, 0)))
    small = {n: w[n] for n in SMALL_ORDER}

    loss_part, grad_x, g = local_step(
        x[0], loss_target[0], small, win_p, conv_w8, full["s5_glu_w"], full["w_branch"][:D_MODEL],
        full["w_branch"][D_MODEL:], full["w_out"], full["w_mlp_in"], full["w_mlp_out"])

    gp = g.pop("w_in_p")
    g["w_in"] = jnp.concatenate([gp[:, 4096:5120], gp[:, 0:2048], gp[:, 5632:5648], gp[:, 5120:5632], gp[:, 2048:4096]], axis=1)
    small_rows = _pack_small(g, loss_part)
    big_parts = jnp.concatenate([_split_shards(n, g[n]) for n in SHARDED_ORDER], axis=1).astype(BF16)
    pad_rows = -big_parts.shape[1] % ADAM_ROW_TILE
    big_parts = jnp.pad(big_parts, ((0, 0), (0, pad_rows), (0, 0)))
    small_parts = jnp.broadcast_to(small_rows, (N_DEV,) + small_rows.shape)
    got_big, got_small = exchange_partials(big_parts, small_parts, name="exchange_grads")

    zero = jnp.zeros((), F32)
    pack_big = lambda vals: jnp.pad(jnp.concatenate([_rows_of(vals[n]) for n in SHARDED_ORDER], axis=0),
                                    ((0, pad_rows), (0, 0)))
    outs_big = adamw_packed(pack_big(w), pack_big(m), pack_big(v), got_big, name="adamw_sharded")
    outs_small = adamw_packed(_pack_small(w, zero), _pack_small(m, zero), _pack_small(v, zero), got_small,
                              name="adamw_replicated")
    shapes = {n: w[n].shape for n in WEIGHT_ORDER}
    results = []
    loss = None
    for p_big, p_small in zip(outs_big, outs_small):
        vals, off = {}, 0
        for n in SHARDED_ORDER:
            rows = math.prod(shapes[n]) // LANES
            vals[n] = p_big[off:off + rows].reshape(shapes[n])
            off += rows
        sm, extra = _unpack_small(p_small, shapes)
        vals.update(sm)
        if loss is None:
            loss = extra
        results.append(vals)
    grads, deltas, new_m, new_v = results
    return (loss, grad_x.reshape(x.shape), *[grads[n] for n in WEIGHT_ORDER], *[deltas[n] for n in WEIGHT_ORDER],
            *[new_m[n] for n in WEIGHT_ORDER], *[new_v[n] for n in WEIGHT_ORDER])
```

```python
import functools
import math

import jax
import jax.numpy as jnp
from jax import lax
from jax.experimental import pallas as pl
from jax.experimental.pallas import tpu as pltpu

F32 = jnp.float32
BF16 = jnp.bfloat16
HI = lax.Precision.HIGHEST

N_DEV = 8
D_MODEL = 1024
SSD_HEADS = 16
SSD_HEADDIM = 64
SSD_GROUPS = 4
SSD_HPG = 4
SSD_STATE = 128
SSD_CHUNK = 128
SSD_CONV = 4
CONV_DIM = 2048
S5_WIDTH = 512
S5_GROUP = 16
S5_GROUPS = 32
S5_STATE = 64
S5_N = S5_GROUPS * S5_STATE
D_FF = 4096
D_IN_PROJ = 5648
IN_SPLITS = (2048, 2048, 1024, 512, 128)
D_IN_PAD = sum(IN_SPLITS)
EPS = 1e-6
LANES = 128
SUBLANES = 8
VMEM_LIMIT = 56 * 1024 * 1024

ADAM_LR = 0.001
ADAM_B1 = 0.9
ADAM_B2 = 0.999
ADAM_EPS = 1e-08
ADAM_WD = 0.01
ADAM_STEP = 10

GELU_C = math.sqrt(2.0 / math.pi)
GELU_K = 0.044715


def _params(*sem):
    return pltpu.CompilerParams(dimension_semantics=sem, vmem_limit_bytes=VMEM_LIMIT)


def _full(shape):
    nd = len(shape)
    return pl.BlockSpec(shape, lambda *_: (0,) * nd)


def _dot(a, b):
    return jnp.dot(a, b, preferred_element_type=F32)


def _dot_nt(a, b):
    return lax.dot_general(a, b, (((1,), (1,)), ((), ())), preferred_element_type=F32)


def _dot_tn(a, b):
    return lax.dot_general(a, b, (((0,), (0,)), ((), ())), preferred_element_type=F32)


def _dot_hi(a, b):
    return jnp.dot(a, b, preferred_element_type=F32, precision=HI)


def _dot_nt_hi(a, b):
    return lax.dot_general(a, b, (((1,), (1,)), ((), ())), preferred_element_type=F32, precision=HI)


def _dot_tn_hi(a, b):
    return lax.dot_general(a, b, (((0,), (0,)), ((), ())), preferred_element_type=F32, precision=HI)


def _sigmoid(x):
    return 1.0 / (1.0 + jnp.exp(-x))


def _softplus(x):
    return jnp.maximum(x, 0.0) + jnp.log(1.0 + jnp.exp(-jnp.abs(x)))


def _gelu(x):
    return 0.5 * x * (1.0 + jnp.tanh(GELU_C * (x + GELU_K * x * x * x)))


def _gelu_grad(x):
    th = jnp.tanh(GELU_C * (x + GELU_K * x * x * x))
    return 0.5 * (1.0 + th) + 0.5 * x * (1.0 - th * th) * GELU_C * (1.0 + 3.0 * GELU_K * x * x)


def rms_matmul(x, g, w, splits, *, tm, name):
    t, d = x.shape
    stacked = w.ndim == 3
    n = w.shape[0] * w.shape[2] if stacked else w.shape[1]
    assert sum(splits) == n and (len(splits) == 1 or not stacked)

    def body(x_ref, g_ref, w_ref, h_ref, *o_refs):
        xv = x_ref[...]
        r = lax.rsqrt(jnp.mean(xv * xv, axis=-1, keepdims=True) + EPS)
        h = (xv * r * g_ref[...]).astype(BF16)
        h_ref[...] = h
        if stacked:
            wk = w.shape[2]
            for k in range(w.shape[0]):
                o_refs[0][:, k * wk:(k + 1) * wk] = _dot(h, w_ref[k])
            return
        off = 0
        for o_ref, width in zip(o_refs, splits):
            o_ref[...] = _dot(h, w_ref[:, off:off + width])
            off += width

    tile = lambda c: pl.BlockSpec((tm, c), lambda i: (i, 0))
    return pl.pallas_call(
        body, name=name, grid=(t // tm,),
        in_specs=[tile(d), _full((1, d)), _full(w.shape)],
        out_specs=[tile(d)] + [tile(c) for c in splits],
        out_shape=[jax.ShapeDtypeStruct((t, d), BF16)] + [jax.ShapeDtypeStruct((t, c), F32) for c in splits],
        compiler_params=_params("parallel"),
    )(x, g.reshape(1, d), w)


def _pick(n, cap):
    best = LANES
    for c in range(LANES, cap + 1, LANES):
        if n % c == 0:
            best = c
    return best


def matmul_tn(a, b, *, name, out_dtype=F32, col_shards=None, tk=1024):
    t, m = a.shape
    n = b.shape[1]
    tk = min(tk, t)
    tm = _pick(m, 1024)
    tn = n // col_shards if col_shards else _pick(n, 1280)
    nk = t // tk

    def body(a_ref, b_ref, o_ref, acc):
        k = pl.program_id(2)

        @pl.when(k == 0)
        def _():
            acc[...] = jnp.zeros_like(acc)

        acc[...] += _dot_tn(a_ref[...], b_ref[...])

        @pl.when(k == nk - 1)
        def _():
            o_ref[...] = acc[...].reshape(o_ref.shape).astype(out_dtype)

    if col_shards:
        out_spec = pl.BlockSpec((1, tm, tn), lambda i, j, k: (j, i, 0))
        out_shape = jax.ShapeDtypeStruct((col_shards, m, tn), out_dtype)
    else:
        out_spec = pl.BlockSpec((tm, tn), lambda i, j, k: (i, j))
        out_shape = jax.ShapeDtypeStruct((m, n), out_dtype)
    return pl.pallas_call(
        body, name=name, grid=(m // tm, n // tn, nk),
        in_specs=[pl.BlockSpec((tk, tm), lambda i, j, k: (k, i)), pl.BlockSpec((tk, tn), lambda i, j, k: (k, j))],
        out_specs=out_spec, out_shape=out_shape,
        scratch_shapes=[pltpu.VMEM((tm, tn), F32)],
        compiler_params=_params("parallel", "parallel", "arbitrary"),
    )(a, b)


def matmul_nt_rms_bwd(dys, w, x, g, dres, *, tm, name):
    t = x.shape[0]
    stacked = w.ndim == 3
    d = w.shape[1] if stacked else w.shape[0]
    widths = [dy.shape[1] for dy in dys]
    n_dy = len(dys)

    def body(*refs):
        dy_refs = refs[:n_dy]
        w_ref, x_ref, g_ref, dres_ref, dx_ref, dxb_ref, dg_ref = refs[n_dy:]
        i = pl.program_id(0)
        if stacked:
            wk = w.shape[2]
            dh = _dot_nt(dy_refs[0][:, 0:wk], w_ref[0])
            for k in range(1, w.shape[0]):
                dh = dh + _dot_nt(dy_refs[0][:, k * wk:(k + 1) * wk], w_ref[k])
        else:
            dh, off = None, 0
            for dy_ref, width in zip(dy_refs, widths):
                part = _dot_nt(dy_ref[...], w_ref[:, off:off + width])
                dh = part if dh is None else dh + part
                off += width
        xv = x_ref[...]
        r = lax.rsqrt(jnp.mean(xv * xv, axis=-1, keepdims=True) + EPS)
        xn = xv * r
        dxn = dh * g_ref[...]
        dx = dres_ref[...] + r * (dxn - xn * jnp.mean(dxn * xn, axis=-1, keepdims=True))
        dx_ref[...] = dx
        dxb_ref[...] = dx.astype(BF16)

        @pl.when(i == 0)
        def _():
            dg_ref[...] = jnp.zeros_like(dg_ref)

        dg_ref[...] += jnp.sum(dh * xn, axis=0, keepdims=True)

    tile = lambda c: pl.BlockSpec((tm, c), lambda i: (i, 0))
    return pl.pallas_call(
        body, name=name, grid=(t // tm,),
        in_specs=[tile(c) for c in widths] + [_full(w.shape), tile(d), _full((1, d)), tile(d)],
        out_specs=[tile(d), tile(d), _full((1, d))],
        out_shape=[jax.ShapeDtypeStruct((t, d), F32), jax.ShapeDtypeStruct((t, d), BF16),
                   jax.ShapeDtypeStruct((1, d), F32)],
        compiler_params=_params("arbitrary"),
    )(*dys, w, x, g.reshape(1, d), dres)


def _conv_taps(prev8, cur, w_ref, tm):
    ext = jnp.concatenate([prev8, cur], axis=0)
    acc = cur * w_ref[3:4, :]
    for k in range(3):
        off = SUBLANES - 3 + k
        acc = acc + ext[off:off + tm, :] * w_ref[k:k + 1, :]
    return acc


def conv_silu_fwd(u, w8, b, *, tm, name):
    t, c = u.shape
    per = tm // SUBLANES

    def body(prev_ref, cur_ref, w_ref, b_ref, o_ref):
        i = pl.program_id(0)
        prev8 = jnp.where(i == 0, 0.0, prev_ref[...])
        xc = _conv_taps(prev8, cur_ref[...], w_ref, tm) + b_ref[...]
        o_ref[...] = xc * _sigmoid(xc)

    return pl.pallas_call(
        body, name=name, grid=(t // tm,),
        in_specs=[pl.BlockSpec((SUBLANES, c), lambda i: (jnp.maximum(i * per - 1, 0), 0)),
                  pl.BlockSpec((tm, c), lambda i: (i, 0)), _full((SUBLANES, c)), _full((1, c))],
        out_specs=pl.BlockSpec((tm, c), lambda i: (i, 0)),
        out_shape=jax.ShapeDtypeStruct((t, c), F32),
        compiler_params=_params("parallel"),
    )(u, u, w8, b.reshape(1, c))


def conv_silu_bwd_a(u, dy, w8, b, *, tm, name):
    t, c = u.shape
    per = tm // SUBLANES

    def body(prev_ref, cur_ref, dy_ref, w_ref, b_ref, dxc_ref, db_ref, dw_ref):
        i = pl.program_id(0)
        prev8 = jnp.where(i == 0, 0.0, prev_ref[...])
        cur = cur_ref[...]
        xc = _conv_taps(prev8, cur, w_ref, tm) + b_ref[...]
        sg = _sigmoid(xc)
        dxc = dy_ref[...] * (sg * (1.0 + xc * (1.0 - sg)))
        dxc_ref[...] = dxc

        @pl.when(i == 0)
        def _():
            db_ref[...] = jnp.zeros_like(db_ref)
            dw_ref[...] = jnp.zeros_like(dw_ref)

        db_ref[...] += jnp.sum(dxc, axis=0, keepdims=True)
        ext = jnp.concatenate([prev8, cur], axis=0)
        rows = []
        for k in range(SSD_CONV):
            off = SUBLANES - 3 + k
            rows.append(jnp.sum(dxc * ext[off:off + tm, :], axis=0, keepdims=True))
        rows.append(jnp.zeros((SUBLANES - SSD_CONV, c), F32))
        dw_ref[...] += jnp.concatenate(rows, axis=0)

    return pl.pallas_call(
        body, name=name, grid=(t // tm,),
        in_specs=[pl.BlockSpec((SUBLANES, c), lambda i: (jnp.maximum(i * per - 1, 0), 0)),
                  pl.BlockSpec((tm, c), lambda i: (i, 0)), pl.BlockSpec((tm, c), lambda i: (i, 0)),
                  _full((SUBLANES, c)), _full((1, c))],
        out_specs=[pl.BlockSpec((tm, c), lambda i: (i, 0)), _full((1, c)), _full((SUBLANES, c))],
        out_shape=[jax.ShapeDtypeStruct((t, c), F32), jax.ShapeDtypeStruct((1, c), F32),
                   jax.ShapeDtypeStruct((SUBLANES, c), F32)],
        compiler_params=_params("arbitrary"),
    )(u, u, dy, w8, b.reshape(1, c))


def conv_bwd_b(dxc, w8, *, tm, name):
    t, c = dxc.shape
    per = tm // SUBLANES
    last = t // SUBLANES - 1
    nt = t // tm

    def body(cur_ref, next_ref, w_ref, o_ref):
        i = pl.program_id(0)
        nxt = jnp.where(i == nt - 1, 0.0, next_ref[...])
        cur = cur_ref[...]
        ext = jnp.concatenate([cur, nxt], axis=0)
        acc = cur * w_ref[3:4, :]
        for j in range(1, SSD_CONV):
            acc = acc + ext[j:j + tm, :] * w_ref[3 - j:4 - j, :]
        o_ref[...] = acc.astype(BF16)

    return pl.pallas_call(
        body, name=name, grid=(nt,),
        in_specs=[pl.BlockSpec((tm, c), lambda i: (i, 0)),
                  pl.BlockSpec((SUBLANES, c), lambda i: (jnp.minimum((i + 1) * per, last), 0)), _full((SUBLANES, c))],
        out_specs=pl.BlockSpec((tm, c), lambda i: (i, 0)),
        out_shape=jax.ShapeDtypeStruct((t, c), BF16),
        compiler_params=_params("parallel"),
    )(dxc, dxc, w8)


def _ssd_chunk_terms(dtr, bias_row, alog_row):
    q = SSD_CHUNK
    row = lax.broadcasted_iota(jnp.int32, (q, q), 0)
    col = lax.broadcasted_iota(jnp.int32, (q, q), 1)
    ltri = (col <= row).astype(F32)
    dt = _softplus(dtr + bias_row)
    a_row = -jnp.exp(alog_row)
    la = dt * a_row
    cum = _dot_hi(ltri, la)
    cum_t = _dot_tn_hi(la, (row <= col).astype(F32))
    return dt, a_row, cum, cum_t, ltri, row, col


def _head_maps():
    di = SSD_HEADS * SSD_HEADDIM
    expand = (lax.broadcasted_iota(jnp.int32, (LANES, di), 1) // SSD_HEADDIM
              == lax.broadcasted_iota(jnp.int32, (LANES, di), 0)).astype(F32)
    collapse = (lax.broadcasted_iota(jnp.int32, (di, LANES), 0) // SSD_HEADDIM
                == lax.broadcasted_iota(jnp.int32, (di, LANES), 1)).astype(F32)
    return expand, collapse


def _ssd_decay(cum, cum_t, h, causal):
    seg = cum[:, h:h + 1] - cum_t[h:h + 1, :]
    return jnp.where(causal, jnp.exp(jnp.where(causal, seg, 0.0)), 0.0)


def _pair_block_diag(x2):
    low = lax.broadcasted_iota(jnp.int32, x2.shape, 1) < SSD_HEADDIM
    zero = jnp.zeros_like(x2)
    return jnp.concatenate([jnp.where(low, x2, zero), jnp.where(low, zero, x2)], axis=0)


def ssd_fwd(xbc, dtr, z, bias_row, alog_row, d_row, norm_g, *, name):
    t = xbc.shape[0]
    q, p, n = SSD_CHUNK, SSD_HEADDIM, SSD_STATE
    nc = t // q
    di = SSD_HEADS * p
    gw = di // SSD_GROUPS

    def body(xs_ref, b_ref, c_ref, dtr_ref, z_ref, bias_ref, alog_ref, d_ref, g_ref,
             ya_ref, yssd_ref, st_ref, state):
        ci = pl.program_id(0)

        @pl.when(ci == 0)
        def _():
            state[...] = jnp.zeros_like(state)

        st_ref[0] = state[...]
        dt, _, cum, cum_t, _, row, col = _ssd_chunk_terms(dtr_ref[...], bias_ref[...], alog_ref[...])
        causal = row >= col
        expand, _ = _head_maps()
        dt_full = _dot_hi(dt, expand)
        cum_full = _dot_hi(cum, expand)
        last_full = cum_full[q - 1:q, :]
        xs = xs_ref[...]
        xdt = xs * dt_full
        xd = (xdt * jnp.exp(last_full - cum_full)).astype(BF16)
        xdt_b = xdt.astype(BF16)
        e_full = jnp.exp(cum_full)
        eend_full = jnp.exp(last_full)
        for g in range(SSD_GROUPS):
            slab = slice(g * gw, (g + 1) * gw)
            bg = b_ref[:, g * n:(g + 1) * n].astype(BF16)
            cg = c_ref[:, g * n:(g + 1) * n].astype(BF16)
            cb = _dot_nt(cg, bg)
            st_g = state[:, slab]
            pairs = []
            for pr in range(SSD_HPG // 2):
                h0 = g * SSD_HPG + 2 * pr
                m0 = (cb * _ssd_decay(cum, cum_t, h0, causal)).astype(BF16)
                m1 = (cb * _ssd_decay(cum, cum_t, h0 + 1, causal)).astype(BF16)
                pairs.append(_dot(jnp.concatenate([m0, m1], axis=1), _pair_block_diag(xdt_b[:, h0 * p:(h0 + 2) * p])))
            y = jnp.concatenate(pairs, axis=1) + e_full[:, slab] * _dot(cg, st_g.astype(BF16))
            yssd_ref[:, slab] = y + d_ref[:, slab] * xs[:, slab]
            state[:, slab] = eend_full[:, slab] * st_g + _dot_tn(bg, xd[:, slab])
        zv = z_ref[...]
        ya1 = yssd_ref[...] * (zv * _sigmoid(zv))
        for g in range(SSD_GROUPS):
            gsl = slice(g * gw, (g + 1) * gw)
            sl = ya1[:, gsl]
            rg = lax.rsqrt(jnp.mean(sl * sl, axis=-1, keepdims=True) + EPS)
            ya_ref[:, gsl] = (sl * rg * g_ref[:, gsl]).astype(BF16)

    blk = lambda w, j: pl.BlockSpec((q, w), lambda c: (c, j))
    return pl.pallas_call(
        body, name=name, grid=(nc,),
        in_specs=[blk(di, 0), blk(512, 2), blk(512, 3), blk(LANES, 0), blk(di, 0),
                  _full((1, LANES)), _full((1, LANES)), _full((1, di)), _full((1, di))],
        out_specs=[blk(di, 0), blk(di, 0), pl.BlockSpec((1, n, di), lambda c: (c, 0, 0))],
        out_shape=[jax.ShapeDtypeStruct((t, di), BF16), jax.ShapeDtypeStruct((t, di), F32),
                   jax.ShapeDtypeStruct((nc, n, di), F32)],
        scratch_shapes=[pltpu.VMEM((n, di), F32)],
        compiler_params=_params("arbitrary"),
    )(xbc, xbc, xbc, dtr, z, bias_row, alog_row, d_row, norm_g.reshape(1, di))


def ssd_bwd(dya, yssd, z, xbc, dtr, states, bias_row, alog_row, d_row, norm_g, *, name):
    t = xbc.shape[0]
    q, p, n = SSD_CHUNK, SSD_HEADDIM, SSD_STATE
    nc = t // q
    di = SSD_HEADS * p
    gw = di // SSD_GROUPS

    def body(dya_ref, yssd_ref, z_ref, xs_ref, b_ref, c_ref, dtr_ref, st_ref, bias_ref, alog_ref, d_ref, g_ref,
             dz_ref, dxbc_ref, ddtr_ref, dng_ref, dd_ref, dalog_ref, dbias_ref, dstate, dyssd):
        ci = pl.program_id(0)

        @pl.when(ci == 0)
        def _():
            dstate[...] = jnp.zeros_like(dstate)
            dng_ref[...] = jnp.zeros_like(dng_ref)
            dd_ref[...] = jnp.zeros_like(dd_ref)
            dalog_ref[...] = jnp.zeros_like(dalog_ref)
            dbias_ref[...] = jnp.zeros_like(dbias_ref)

        zv = z_ref[...]
        sz = _sigmoid(zv)
        silu_z = zv * sz
        yv = yssd_ref[...]
        ya1 = yv * silu_z
        dyav = dya_ref[...]
        for g in range(SSD_GROUPS):
            gsl = slice(g * gw, (g + 1) * gw)
            sl = ya1[:, gsl]
            rg = lax.rsqrt(jnp.mean(sl * sl, axis=-1, keepdims=True) + EPS)
            ya2 = sl * rg
            dy_g = dyav[:, gsl]
            dng_ref[:, gsl] += jnp.sum(dy_g * ya2, axis=0, keepdims=True)
            dya2 = dy_g * g_ref[:, gsl]
            dya1 = rg * (dya2 - ya2 * jnp.mean(dya2 * ya2, axis=-1, keepdims=True))
            dyssd[:, gsl] = dya1 * silu_z[:, gsl]
            dz_ref[:, gsl] = (dya1 * yv[:, gsl] * (sz[:, gsl] * (1.0 + zv[:, gsl] * (1.0 - sz[:, gsl])))).astype(BF16)

        dtr_v = dtr_ref[...]
        dt, a_row, cum, cum_t, ltri, row, col = _ssd_chunk_terms(dtr_v, bias_ref[...], alog_ref[...])
        causal = row >= col
        expand, collapse = _head_maps()
        lane = lax.broadcasted_iota(jnp.int32, (1, LANES), 1)
        sub = lax.broadcasted_iota(jnp.int32, (LANES, 1), 0)
        is_last = lax.broadcasted_iota(jnp.int32, (q, 1), 0) == q - 1
        low = lax.broadcasted_iota(jnp.int32, (q, 2 * p), 1) < p
        dt_full = _dot_hi(dt, expand)
        cum_full = _dot_hi(cum, expand)
        last_full = cum_full[q - 1:q, :]
        e_full = jnp.exp(cum_full)
        eend_full = jnp.exp(last_full)
        dend_full = jnp.exp(last_full - cum_full)
        xs = xs_ref[...]
        xdt = xs * dt_full
        xdt_b = xdt.astype(BF16)
        xd_b = (xdt * dend_full).astype(BF16)
        dy_all = dyssd[...]
        dy_b = dy_all.astype(BF16)
        dg_b = (dy_all * e_full).astype(BF16)
        dcum_mat = jnp.zeros((q, LANES), F32)
        rmat = jnp.zeros((LANES, q), F32)
        yoff_prod, bds_list, dx_list, dlast_parts = [], [], [], []
        for g in range(SSD_GROUPS):
            slab = slice(g * gw, (g + 1) * gw)
            bg = b_ref[:, g * n:(g + 1) * n].astype(BF16)
            cg = c_ref[:, g * n:(g + 1) * n].astype(BF16)
            cb = _dot_nt(cg, bg)
            st_g = st_ref[0, :, slab]
            st_b = st_g.astype(BF16)
            dsn = dstate[:, slab]
            dsn_b = dsn.astype(BF16)
            yoff_prod.append(dy_all[:, slab] * e_full[:, slab] * _dot(cg, st_b))
            dcg = _dot_nt(dg_b[:, slab], st_b)
            dstate[:, slab] = _dot_tn(cg, dg_b[:, slab]) + eend_full[:, slab] * dsn
            bds_list.append(_dot(bg, dsn_b))
            dbg = _dot_nt(xd_b[:, slab], dsn_b)
            dlast_parts.append(jnp.sum(dsn * st_g, axis=0, keepdims=True))
            dcb = jnp.zeros((q, q), F32)
            dx_pairs = []
            for pr in range(SSD_HPG // 2):
                h0 = g * SSD_HPG + 2 * pr
                ps = slice(h0 * p, (h0 + 2) * p)
                l0 = _ssd_decay(cum, cum_t, h0, causal)
                l1 = _ssd_decay(cum, cum_t, h0 + 1, causal)
                m0, m1 = cb * l0, cb * l1
                dyp = dy_b[:, ps]
                zero = jnp.zeros_like(dyp)
                dy0, dy1 = jnp.where(low, dyp, zero), jnp.where(low, zero, dyp)
                dm0 = _dot_nt(dy0, xdt_b[:, ps])
                dm1 = _dot_nt(dy1, xdt_b[:, ps])
                w0, w1 = dm0 * m0, dm1 * m1
                dcum_mat = (dcum_mat + jnp.sum(w0, axis=1, keepdims=True) * (lane == h0).astype(F32)
                            + jnp.sum(w1, axis=1, keepdims=True) * (lane == h0 + 1).astype(F32))
                rmat = (rmat + jnp.where(sub == h0, jnp.sum(w0, axis=0, keepdims=True), 0.0)
                        + jnp.where(sub == h0 + 1, jnp.sum(w1, axis=0, keepdims=True), 0.0))
                dcb = dcb + dm0 * l0 + dm1 * l1
                dx_pairs.append(_dot_tn(jnp.concatenate([m0.astype(BF16), m1.astype(BF16)], axis=0),
                                        jnp.concatenate([dy0, dy1], axis=0)))
            dx_list.append(jnp.concatenate(dx_pairs, axis=1))
            dcb_b = dcb.astype(BF16)
            dxbc_ref[:, di + g * n:di + (g + 1) * n] = dbg + _dot_tn(dcb_b, cg)
            dxbc_ref[:, di + SSD_GROUPS * n + g * n:di + SSD_GROUPS * n + (g + 1) * n] = dcg + _dot(dcb_b, bg)
        bds_all = jnp.concatenate(bds_list, axis=1)
        dx_all = jnp.concatenate(dx_list, axis=1) + dend_full * bds_all
        last_row = cum[q - 1:q, :]
        qv = _dot_hi(xdt * bds_all, collapse) * jnp.exp(last_row - cum)
        dcum_mat = dcum_mat + _dot_hi(jnp.concatenate(yoff_prod, axis=1), collapse) - qv
        dlast_full = jnp.broadcast_to(jnp.concatenate(dlast_parts, axis=1), (SUBLANES, di))
        dlast_row = _dot_hi(dlast_full, collapse)[0:1, :] * jnp.exp(last_row) + jnp.sum(qv, axis=0, keepdims=True)
        dcum_mat = dcum_mat + jnp.where(is_last, dlast_row, 0.0)
        dla = _dot_tn_hi(ltri, dcum_mat) - _dot_nt_hi((row <= col).astype(F32), rmat)
        ddt_mat = _dot_hi(dx_all * xs, collapse) + dla * a_row
        dxbc_ref[:, 0:di] = d_ref[...] * dy_all + dx_all * dt_full
        ddtr = ddt_mat * _sigmoid(dtr_v + bias_ref[...])
        ddtr_ref[...] = ddtr.astype(BF16)
        dd_full = jnp.broadcast_to(jnp.sum(dy_all * xs, axis=0, keepdims=True), (SUBLANES, di))
        dd_ref[...] += _dot_hi(dd_full, collapse)[0:1, :]
        dalog_ref[...] += jnp.sum(dla * dt, axis=0, keepdims=True) * a_row
        dbias_ref[...] += jnp.sum(ddtr, axis=0, keepdims=True)

    blk = lambda w, j: pl.BlockSpec((q, w), lambda c: (nc - 1 - c, j))
    row_out = lambda w: jax.ShapeDtypeStruct((1, w), F32)
    return pl.pallas_call(
        body, name=name, grid=(nc,),
        in_specs=[blk(di, 0), blk(di, 0), blk(di, 0), blk(di, 0), blk(512, 2), blk(512, 3), blk(LANES, 0),
                  pl.BlockSpec((1, n, di), lambda c: (nc - 1 - c, 0, 0)),
                  _full((1, LANES)), _full((1, LANES)), _full((1, di)), _full((1, di))],
        out_specs=[blk(di, 0), blk(CONV_DIM, 0), blk(LANES, 0),
                   _full((1, di)), _full((1, LANES)), _full((1, LANES)), _full((1, LANES))],
        out_shape=[jax.ShapeDtypeStruct((t, di), BF16), jax.ShapeDtypeStruct((t, CONV_DIM), F32),
                   jax.ShapeDtypeStruct((t, LANES), BF16), row_out(di), row_out(LANES), row_out(LANES), row_out(LANES)],
        scratch_shapes=[pltpu.VMEM((n, di), F32), pltpu.VMEM((q, di), F32)],
        compiler_params=_params("arbitrary"),
    )(dya, yssd, z, xbc, xbc, xbc, dtr, states, bias_row, alog_row, d_row, norm_g.reshape(1, di))


S5_BLOCKS = 4
S5_BC = S5_WIDTH // S5_BLOCKS
S5_BS = S5_N // S5_BLOCKS


def _s5_tables(ar, ai, reverse):
    if reverse:
        ai = -ai
    pows = [(ar, ai)]
    for _ in range(SUBLANES - 1):
        pr, pi = pows[-1]
        pows.append((pr * ar - pi * ai, pr * ai + pi * ar))
    row = lax.broadcasted_iota(jnp.int32, (SUBLANES, ar.shape[1]), 0)
    out = []
    for k in (1, 2, 4):
        pr, pi = pows[k - 1]
        mask = (row < SUBLANES - k) if reverse else (row >= k)
        out += [jnp.where(mask, pr, 0.0), jnp.where(mask, pi, 0.0)]
    order = list(range(SUBLANES))
    if reverse:
        order = order[::-1]
    out.append(jnp.concatenate([pows[e][0] for e in order], axis=0))
    out.append(jnp.concatenate([pows[e][1] for e in order], axis=0))
    return out


def _s5_scan(s_scr, tab, carry, tm, reverse):
    ngrp = tm // SUBLANES
    for lc in range(S5_BLOCKS):
        lr = slice(lc * S5_BS, (lc + 1) * S5_BS)
        li = slice(S5_N + lc * S5_BS, S5_N + (lc + 1) * S5_BS)
        tb = [tab[i, :, lr] for i in range(8)]

        def step(i, c, lr=lr, li=li, tb=tb):
            grp = (ngrp - 1 - i) if reverse else i
            r0 = pl.multiple_of(grp * SUBLANES, SUBLANES)
            xr = s_scr[pl.ds(r0, SUBLANES), lr]
            xi = s_scr[pl.ds(r0, SUBLANES), li]
            for lvl, k in enumerate((1, 2, 4)):
                sh = (SUBLANES - k) if reverse else k
                pr, pi = tb[2 * lvl], tb[2 * lvl + 1]
                rr = pltpu.roll(xr, sh, 0)
                ri = pltpu.roll(xi, sh, 0)
                xr, xi = xr + pr * rr - pi * ri, xi + pr * ri + pi * rr
            cr, ci = c
            xr, xi = xr + tb[6] * cr - tb[7] * ci, xi + tb[6] * ci + tb[7] * cr
            s_scr[pl.ds(r0, SUBLANES), lr] = xr
            s_scr[pl.ds(r0, SUBLANES), li] = xi
            e = 0 if reverse else SUBLANES - 1
            return xr[e:e + 1, :], xi[e:e + 1, :]

        cf = lax.fori_loop(0, ngrp, step, (carry[0:1, lr], carry[0:1, li]))
        carry[0:1, lr] = cf[0]
        carry[0:1, li] = cf[1]


def s5_fwd(u5, bbc, ccc, ab8, d_row, wglu, bglu_row, *, tm, name):
    t, w = u5.shape

    def body(u_ref, bb_ref, cc_ref, ab_ref, d_ref, wg_ref, bg_ref, s_ref, ys_ref, yb_ref, s_scr, tab, carry):
        i = pl.program_id(0)

        @pl.when(i == 0)
        def _():
            carry[...] = jnp.zeros_like(carry)
            for k, tv in enumerate(_s5_tables(ab_ref[0:1, :], ab_ref[1:2, :], False)):
                tab[k] = tv

        u = u_ref[...]
        ub = u.astype(BF16)
        for j in range(S5_BLOCKS):
            uj = ub[:, j * S5_BC:(j + 1) * S5_BC]
            bj = bb_ref[j * S5_BC:(j + 1) * S5_BC, :]
            s_scr[:, j * S5_BS:(j + 1) * S5_BS] = _dot(uj, bj[:, :S5_BS])
            s_scr[:, S5_N + j * S5_BS:S5_N + (j + 1) * S5_BS] = _dot(uj, bj[:, S5_BS:])
        _s5_scan(s_scr, tab, carry, tm, False)
        s = s_scr[...]
        s_ref[...] = s
        sb = s.astype(BF16)
        ys = []
        for j in range(S5_BLOCKS):
            cj = cc_ref[:, j * S5_BC:(j + 1) * S5_BC]
            ys.append(_dot(sb[:, j * S5_BS:(j + 1) * S5_BS], cj[:S5_BS, :])
                      + _dot(sb[:, S5_N + j * S5_BS:S5_N + (j + 1) * S5_BS], cj[S5_BS:, :]))
        y = jnp.concatenate(ys, axis=1) + d_ref[...] * u
        ys_ref[...] = y
        yb1 = _gelu(y)
        tt = _dot(yb1.astype(BF16), wg_ref[...]) + bg_ref[...]
        yb_ref[...] = (yb1 * _sigmoid(tt)).astype(BF16)

    tile = lambda c: pl.BlockSpec((tm, c), lambda i: (i, 0))
    return pl.pallas_call(
        body, name=name, grid=(t // tm,),
        in_specs=[tile(w), _full((w, 2 * S5_BS)), _full((2 * S5_BS, w)), _full((SUBLANES, S5_N)), _full((1, w)),
                  _full((w, w)), _full((1, w))],
        out_specs=[tile(2 * S5_N), tile(w), tile(w)],
        out_shape=[jax.ShapeDtypeStruct((t, 2 * S5_N), F32), jax.ShapeDtypeStruct((t, w), F32),
                   jax.ShapeDtypeStruct((t, w), BF16)],
        scratch_shapes=[pltpu.VMEM((tm, 2 * S5_N), F32), pltpu.VMEM((8, SUBLANES, S5_N), F32),
                        pltpu.VMEM((SUBLANES, 2 * S5_N), F32)],
        compiler_params=_params("arbitrary"),
    )(u5, bbc, ccc, ab8, d_row, wglu, bglu_row)


def s5_bwd(dyb, ys5, u5, s, bbc, ccc, ab8, d_row, wglu, bglu_row, *, tm, name):
    t, w = u5.shape
    nt = t // tm
    per = tm // SUBLANES

    def body(dyb_ref, ys_ref, u_ref, s_ref, sprev_ref, bb_ref, cc_ref, ab_ref, d_ref, wg_ref, bg_ref,
             du_ref, dbb_ref, dcc_ref, dab_ref, dd_ref, dwg_ref, dbg_ref, g_scr, tab, carry):
        i = pl.program_id(0)

        @pl.when(i == 0)
        def _():
            carry[...] = jnp.zeros_like(carry)
            for k, tv in enumerate(_s5_tables(ab_ref[0:1, :], ab_ref[1:2, :], True)):
                tab[k] = tv
            for r in (dbb_ref, dcc_ref, dab_ref, dd_ref, dwg_ref, dbg_ref):
                r[...] = jnp.zeros_like(r)

        y = ys_ref[...]
        u = u_ref[...]
        yb1 = _gelu(y)
        yb1b = yb1.astype(BF16)
        sg = _sigmoid(_dot(yb1b, wg_ref[...]) + bg_ref[...])
        dybv = dyb_ref[...]
        dtt = dybv * yb1 * sg * (1.0 - sg)
        dttb = dtt.astype(BF16)
        dyb1 = dybv * sg + _dot_nt(dttb, wg_ref[...])
        dwg_ref[...] += _dot_tn(yb1b, dttb)
        dbg_ref[...] += jnp.sum(dtt, axis=0, keepdims=True)
        dy = dyb1 * _gelu_grad(y)
        dd_ref[...] += jnp.sum(dy * u, axis=0, keepdims=True)
        dyh = dy.astype(BF16)
        for j in range(S5_BLOCKS):
            cj = cc_ref[:, j * S5_BC:(j + 1) * S5_BC]
            dyj = dyh[:, j * S5_BC:(j + 1) * S5_BC]
            g_scr[:, j * S5_BS:(j + 1) * S5_BS] = _dot_nt(dyj, cj[:S5_BS, :])
            g_scr[:, S5_N + j * S5_BS:S5_N + (j + 1) * S5_BS] = _dot_nt(dyj, cj[S5_BS:, :])
        _s5_scan(g_scr, tab, carry, tm, True)
        gs = g_scr[...]
        gb = gs.astype(BF16)
        sv = s_ref[...]
        sb = sv.astype(BF16)
        ub = u.astype(BF16)
        dus = []
        for j in range(S5_BLOCKS):
            rs = slice(j * S5_BS, (j + 1) * S5_BS)
            ims = slice(S5_N + j * S5_BS, S5_N + (j + 1) * S5_BS)
            cs = slice(j * S5_BC, (j + 1) * S5_BC)
            bj = bb_ref[cs, :]
            dus.append(_dot_nt(gb[:, rs], bj[:, :S5_BS]) + _dot_nt(gb[:, ims], bj[:, S5_BS:]))
            dbb_ref[cs, :S5_BS] += _dot_tn(ub[:, cs], gb[:, rs])
            dbb_ref[cs, S5_BS:] += _dot_tn(ub[:, cs], gb[:, ims])
            dcc_ref[:S5_BS, cs] += _dot_tn(sb[:, rs], dyh[:, cs])
            dcc_ref[S5_BS:, cs] += _dot_tn(sb[:, ims], dyh[:, cs])
        du_ref[...] = (jnp.concatenate(dus, axis=1) + d_ref[...] * dy).astype(BF16)
        first = lax.broadcasted_iota(jnp.int32, (tm, 1), 0) == 0
        before = jnp.where(i == nt - 1, 0.0, sprev_ref[SUBLANES - 1:SUBLANES, :])
        sp = jnp.where(first, before, pltpu.roll(sv, 1, 0))
        g_re, g_im = gs[:, :S5_N], gs[:, S5_N:]
        sp_re, sp_im = sp[:, :S5_N], sp[:, S5_N:]
        dab_ref[0:1, :] += jnp.sum(g_re * sp_re + g_im * sp_im, axis=0, keepdims=True)
        dab_ref[1:2, :] += jnp.sum(g_im * sp_re - g_re * sp_im, axis=0, keepdims=True)

    tile = lambda c: pl.BlockSpec((tm, c), lambda i: (nt - 1 - i, 0))
    acc = lambda r, c: jax.ShapeDtypeStruct((r, c), F32)
    return pl.pallas_call(
        body, name=name, grid=(nt,),
        in_specs=[tile(w), tile(w), tile(w), tile(2 * S5_N),
                  pl.BlockSpec((SUBLANES, 2 * S5_N), lambda i: (jnp.maximum((nt - 1 - i) * per - 1, 0), 0)),
                  _full((w, 2 * S5_BS)), _full((2 * S5_BS, w)), _full((SUBLANES, S5_N)), _full((1, w)),
                  _full((w, w)), _full((1, w))],
        out_specs=[tile(w), _full((w, 2 * S5_BS)), _full((2 * S5_BS, w)), _full((SUBLANES, S5_N)), _full((1, w)),
                   _full((w, w)), _full((1, w))],
        out_shape=[jax.ShapeDtypeStruct((t, w), BF16), acc(w, 2 * S5_BS), acc(2 * S5_BS, w), acc(SUBLANES, S5_N),
                   acc(1, w), acc(w, w), acc(1, w)],
        scratch_shapes=[pltpu.VMEM((tm, 2 * S5_N), F32), pltpu.VMEM((8, SUBLANES, S5_N), F32),
                        pltpu.VMEM((SUBLANES, 2 * S5_N), F32)],
        compiler_params=_params("arbitrary"),
    )(dyb, ys5, u5, s, s, bbc, ccc, ab8, d_row, wglu, bglu_row)


def s5_discretize(a_re, a_im, log_dt, b_re, b_im, *, name):
    n = a_re.shape[0]

    def body(ar_ref, ai_ref, dt_ref, br_ref, bi_ref, ab_ref, bbr_ref, bbi_ref):
        ar, ai, dtv = ar_ref[...], ai_ref[...], jnp.exp(dt_ref[...])
        mag = jnp.exp(ar * dtv)
        abr = mag * jnp.cos(ai * dtv)
        abi = mag * jnp.sin(ai * dtv)
        den = ar * ar + ai * ai
        nr = abr - 1.0
        cr = (nr * ar + abi * ai) / den
        ci = (abi * ar - nr * ai) / den
        ab_ref[:, 0:1] = abr
        ab_ref[:, 1:2] = abi
        br, bi = br_ref[...], bi_ref[...]
        bbr_ref[...] = cr * br - ci * bi
        bbi_ref[...] = cr * bi + ci * br

    col = jax.ShapeDtypeStruct((n, 2), F32)
    mat = jax.ShapeDtypeStruct((n, S5_GROUP), F32)
    return pl.pallas_call(body, name=name, out_shape=[col, mat, mat])(a_re, a_im, log_dt, b_re, b_im)


def s5_discretize_bwd(a_re, a_im, log_dt, b_re, b_im, dab, dbb_re, dbb_im, *, name):
    n = a_re.shape[0]

    def body(ar_ref, ai_ref, dt_ref, br_ref, bi_ref, dab_ref, dbr_ref, dbi_ref, da_ref, ddt_ref, gbr_ref, gbi_ref):
        ar, ai, dtv = ar_ref[...], ai_ref[...], jnp.exp(dt_ref[...])
        mag = jnp.exp(ar * dtv)
        abr = mag * jnp.cos(ai * dtv)
        abi = mag * jnp.sin(ai * dtv)
        den = ar * ar + ai * ai
        nr = abr - 1.0
        cr = (nr * ar + abi * ai) / den
        ci = (abi * ar - nr * ai) / den
        br, bi = br_ref[...], bi_ref[...]
        dbr, dbi = dbr_ref[...], dbi_ref[...]
        gbr_ref[...] = cr * dbr + ci * dbi
        gbi_ref[...] = cr * dbi - ci * dbr
        gcr = jnp.sum(dbr * br + dbi * bi, axis=1, keepdims=True)
        gci = jnp.sum(dbi * br - dbr * bi, axis=1, keepdims=True)
        ilr, ili = ar / den, -ai / den
        gabr = dab_ref[:, 0:1] + (ilr * gcr + ili * gci)
        gabi = dab_ref[:, 1:2] + (ilr * gci - ili * gcr)
        t1r, t1i = dtv * abr, dtv * abi
        t2r, t2i = -(cr * ilr - ci * ili), -(cr * ili + ci * ilr)
        da_ref[:, 0:1] = (t1r * gabr + t1i * gabi) + (t2r * gcr + t2i * gci)
        da_ref[:, 1:2] = (t1r * gabi - t1i * gabr) + (t2r * gci - t2i * gcr)
        lr, li = ar * abr - ai * abi, ar * abi + ai * abr
        dlog = (lr * gabr + li * gabi) * dtv
        ddt_ref[...] = jnp.sum(dlog.reshape(S5_GROUPS, S5_STATE, 1), axis=1)

    col2 = jax.ShapeDtypeStruct((n, 2), F32)
    col1 = jax.ShapeDtypeStruct((S5_GROUPS, 1), F32)
    mat = jax.ShapeDtypeStruct((n, S5_GROUP), F32)
    return pl.pallas_call(body, name=name, out_shape=[col2, col1, mat, mat])(
        a_re, a_im, log_dt, b_re, b_im, dab, dbb_re, dbb_im)


def merge_fwd(ya, yb, graw, x, wba, wbb, wout, *, tm, name):
    t, d = x.shape

    def body(ya_ref, yb_ref, g_ref, x_ref, wba_ref, wbb_ref, wo_ref, x1_ref, pa_ref, pb_ref):
        pa = _dot(ya_ref[...], wba_ref[...])
        pb = _dot(yb_ref[...], wbb_ref[...])
        gate = _sigmoid(g_ref[...])
        merged = gate[:, :d] * pa + gate[:, d:] * pb
        x1_ref[...] = x_ref[...] + _dot(merged.astype(BF16), wo_ref[...])
        pa_ref[...] = pa
        pb_ref[...] = pb

    tile = lambda c: pl.BlockSpec((tm, c), lambda i: (i, 0))
    out = jax.ShapeDtypeStruct((t, d), F32)
    return pl.pallas_call(
        body, name=name, grid=(t // tm,),
        in_specs=[tile(d), tile(S5_WIDTH), tile(2 * d), tile(d), _full((d, d)), _full((S5_WIDTH, d)), _full((d, d))],
        out_specs=[tile(d), tile(d), tile(d)], out_shape=[out, out, out],
        compiler_params=_params("parallel"),
    )(ya, yb, graw, x, wba, wbb, wout)


def merge_bwd(dx1b, graw, pa, pb, wba, wbb, wout, *, tm, name):
    t, d = pa.shape

    def body(dx_ref, g_ref, pa_ref, pb_ref, wba_ref, wbb_ref, wo_ref,
             mg_ref, dg_ref, dpa_ref, dpb_ref, dya_ref, dyb_ref):
        dm = _dot_nt(dx_ref[...], wo_ref[...])
        gate = _sigmoid(g_ref[...])
        g0, g1 = gate[:, :d], gate[:, d:]
        pa, pb = pa_ref[...], pb_ref[...]
        mg_ref[...] = (g0 * pa + g1 * pb).astype(BF16)
        dg_ref[:, :d] = (dm * pa * g0 * (1.0 - g0)).astype(BF16)
        dg_ref[:, d:] = (dm * pb * g1 * (1.0 - g1)).astype(BF16)
        dpa = (dm * g0).astype(BF16)
        dpb = (dm * g1).astype(BF16)
        dpa_ref[...] = dpa
        dpb_ref[...] = dpb
        dya_ref[...] = _dot_nt(dpa, wba_ref[...])
        dyb_ref[...] = _dot_nt(dpb, wbb_ref[...])

    tile = lambda c: pl.BlockSpec((tm, c), lambda i: (i, 0))
    sds = jax.ShapeDtypeStruct
    return pl.pallas_call(
        body, name=name, grid=(t // tm,),
        in_specs=[tile(d), tile(2 * d), tile(d), tile(d), _full((d, d)), _full((S5_WIDTH, d)), _full((d, d))],
        out_specs=[tile(d), tile(2 * d), tile(d), tile(d), tile(d), tile(S5_WIDTH)],
        out_shape=[sds((t, d), BF16), sds((t, 2 * d), BF16), sds((t, d), BF16), sds((t, d), BF16),
                   sds((t, d), F32), sds((t, S5_WIDTH), F32)],
        compiler_params=_params("parallel"),
    )(dx1b, graw, pa, pb, wba, wbb, wout)


def mlp_out_loss(a1, x1, w2, gf, target, *, tm, name):
    t, d = x1.shape
    f = a1.shape[1]

    def body(a_ref, x_ref, w_ref, g_ref, tg_ref, act_ref, dx_ref, dxb_ref, loss_ref, dg_ref):
        i = pl.program_id(0)
        av = jnp.maximum(a_ref[...], 0.0)
        act = (av * av).astype(BF16)
        act_ref[...] = act
        x2 = x_ref[...] + _dot(act, w_ref[...])
        r = lax.rsqrt(jnp.mean(x2 * x2, axis=-1, keepdims=True) + EPS)
        xn = x2 * r
        err = xn * g_ref[...] - tg_ref[...]
        dyf = err * (1.0 / d)
        dxn = dyf * g_ref[...]
        dx = r * (dxn - xn * jnp.mean(dxn * xn, axis=-1, keepdims=True))
        dx_ref[...] = dx
        dxb_ref[...] = dx.astype(BF16)

        @pl.when(i == 0)
        def _():
            loss_ref[...] = jnp.zeros_like(loss_ref)
            dg_ref[...] = jnp.zeros_like(dg_ref)

        loss_ref[...] += jnp.sum(err * err) * (0.5 / d)
        dg_ref[...] += jnp.sum(dyf * xn, axis=0, keepdims=True)

    tile = lambda c: pl.BlockSpec((tm, c), lambda i: (i, 0))
    sds = jax.ShapeDtypeStruct
    return pl.pallas_call(
        body, name=name, grid=(t // tm,),
        in_specs=[tile(f), tile(d), _full((f, d)), _full((1, d)), tile(d)],
        out_specs=[tile(f), tile(d), tile(d), _full((1, LANES)), _full((1, d))],
        out_shape=[sds((t, f), BF16), sds((t, d), F32), sds((t, d), BF16), sds((1, LANES), F32), sds((1, d), F32)],
        compiler_params=_params("arbitrary"),
    )(a1, x1, w2, gf.reshape(1, d), target)


def mlp_bwd_act(dx2b, a1, w2, *, tm, name):
    t, f = a1.shape
    d = dx2b.shape[1]

    def body(dx_ref, a_ref, w_ref, o_ref):
        dact = _dot_nt(dx_ref[...], w_ref[...])
        o_ref[...] = (dact * 2.0 * jnp.maximum(a_ref[...], 0.0)).astype(BF16)

    tile = lambda c: pl.BlockSpec((tm, c), lambda i: (i, 0))
    return pl.pallas_call(
        body, name=name, grid=(t // tm,),
        in_specs=[tile(d), tile(f), _full((f, d))], out_specs=tile(f),
        out_shape=jax.ShapeDtypeStruct((t, f), BF16),
        compiler_params=_params("parallel"),
    )(dx2b, a1, w2)


ADAM_TILE_ELEMS = 128 * 1024


def _row_tile(rows, cap):
    if rows <= cap:
        return rows
    best = None
    for c in range(16, cap + 1, 16):
        if rows % c == 0:
            best = c
    assert best is not None, rows
    return best


def adamw(w, m, v, gparts, *, name):
    rows, cols = w.shape
    tr = _row_tile(rows, max(16, ADAM_TILE_ELEMS // cols))

    def body(w_ref, m_ref, v_ref, g_ref, go_ref, d_ref, mo_ref, vo_ref):
        g = g_ref[0].astype(F32)
        for k in range(1, N_DEV):
            g = g + g_ref[k].astype(F32)
        mn = ADAM_B1 * m_ref[...] + (1.0 - ADAM_B1) * g
        vn = ADAM_B2 * v_ref[...] + (1.0 - ADAM_B2) * (g * g)
        m_hat = mn / (1.0 - ADAM_B1 ** ADAM_STEP)
        v_hat = vn / (1.0 - ADAM_B2 ** ADAM_STEP)
        go_ref[...] = g
        d_ref[...] = -ADAM_LR * (m_hat / (jnp.sqrt(v_hat) + ADAM_EPS) + ADAM_WD * w_ref[...])
        mo_ref[...] = mn
        vo_ref[...] = vn

    tile = pl.BlockSpec((tr, cols), lambda i: (i, 0))
    out = jax.ShapeDtypeStruct((rows, cols), F32)
    return pl.pallas_call(
        body, name=name, grid=(rows // tr,),
        in_specs=[tile, tile, tile, pl.BlockSpec((N_DEV, tr, cols), lambda i: (0, i, 0))],
        out_specs=[tile, tile, tile, tile], out_shape=[out, out, out, out],
        compiler_params=_params("parallel"),
    )(w, m, v, gparts)


_MESH = pl.DeviceIdType.MESH
_RELATIONS = [(dx, dy, dc) for dx in (0, 1) for dy in (0, 1) for dc in (0, 1)][1:]


def _place():
    x, y, c = lax.axis_index("x"), lax.axis_index("y"), lax.axis_index("c")
    return x, y, c, 4 * x + 2 * y + c


def all_gather_arrays(blocks, *, name):
    na = len(blocks)

    def body(*refs):
        x_refs, out_refs = refs[:na], refs[na:2 * na]
        send_sems, recv_sems, local_sems = refs[2 * na:]
        x, y, c, _ = _place()
        me, sibling = (x, y, c), (x, y, 1 - c)
        chips = [(1 - x, y), (x, 1 - y), (1 - x, 1 - y)]

        def copy(a, k, blk, to, own=False):
            px, py, pc = blk
            dst = out_refs[a].at[4 * px + 2 * py + pc]
            return pltpu.make_async_remote_copy(
                src_ref=x_refs[a] if own else dst, dst_ref=dst,
                send_sem=send_sems.at[7 * a + k], recv_sem=recv_sems.at[7 * a + k], device_id=to, device_id_type=_MESH)

        mine = [pltpu.make_async_copy(x_refs[a], out_refs[a].at[4 * x + 2 * y + c], local_sems.at[a]) for a in range(na)]
        for cp in mine:
            cp.start()
        sent = []
        for a in range(na):
            first = [copy(a, 0, me, sibling, own=True)]
            first += [copy(a, 1 + j, me, (*chip, c), own=True) for j, chip in enumerate(chips)]
            for cp in first:
                cp.start()
            sent += first
        for a in range(na):
            for j, chip in enumerate(chips):
                copy(a, 1 + j, (*chip, c), me).wait_recv()
                fwd = copy(a, 4 + j, (*chip, c), sibling)
                fwd.start()
                sent.append(fwd)
        for a in range(na):
            copy(a, 0, sibling, me).wait_recv()
            for j, chip in enumerate(chips):
                copy(a, 4 + j, (*chip, 1 - c), me).wait_recv()
        for cp in sent:
            cp.wait_send()
        for cp in mine:
            cp.wait()

    any_spec = pl.BlockSpec(memory_space=pl.ANY)
    return pl.pallas_call(
        body, name=name, in_specs=[any_spec] * na, out_specs=[any_spec] * na,
        out_shape=[jax.ShapeDtypeStruct((N_DEV,) + b.shape, b.dtype) for b in blocks],
        scratch_shapes=[pltpu.SemaphoreType.DMA((7 * na,)), pltpu.SemaphoreType.DMA((7 * na,)),
                        pltpu.SemaphoreType.DMA((na,))],
        compiler_params=pltpu.CompilerParams(has_side_effects=True),
    )(*blocks)


def exchange_partials(parts, *, name):
    na = len(parts)
    n_rel = len(_RELATIONS)

    def body(*refs):
        p_refs, o_refs = refs[:na], refs[na:2 * na]
        send_sems, recv_sems, local_sems = refs[2 * na:]
        x, y, c, idx = _place()
        mine = [pltpu.make_async_copy(p_refs[a].at[idx], o_refs[a].at[idx], local_sems.at[a]) for a in range(na)]
        for cp in mine:
            cp.start()
        copies = []
        for a in range(na):
            for k, (dx, dy, dc) in enumerate(_RELATIONS):
                px, py, pc = x ^ dx, y ^ dy, c ^ dc
                copies.append(pltpu.make_async_remote_copy(
                    src_ref=p_refs[a].at[4 * px + 2 * py + pc], dst_ref=o_refs[a].at[idx],
                    send_sem=send_sems.at[a * n_rel + k], recv_sem=recv_sems.at[a * n_rel + k],
                    device_id=(px, py, pc), device_id_type=_MESH))
        for cp in copies:
            cp.start()
        for cp in copies:
            cp.wait_recv()
        for cp in copies:
            cp.wait_send()
        for cp in mine:
            cp.wait()

    any_spec = pl.BlockSpec(memory_space=pl.ANY)
    return pl.pallas_call(
        body, name=name, in_specs=[any_spec] * na, out_specs=[any_spec] * na,
        out_shape=[jax.ShapeDtypeStruct(p.shape, p.dtype) for p in parts],
        scratch_shapes=[pltpu.SemaphoreType.DMA((na * n_rel,)), pltpu.SemaphoreType.DMA((na * n_rel,)),
                        pltpu.SemaphoreType.DMA((na,))],
        compiler_params=pltpu.CompilerParams(has_side_effects=True),
    )(*parts)


def _lane_row(v):
    return jnp.pad(v.astype(F32), (0, LANES - v.shape[0])).reshape(1, LANES)


def _block_diag_b(bb):
    eye = jnp.eye(SUBLANES, dtype=F32)
    bt = bb.reshape(S5_BLOCKS, 8, S5_STATE, S5_GROUP).transpose(0, 1, 3, 2)
    return (bt[:, :, :, None, :] * eye[None, :, None, :, None]).reshape(S5_WIDTH, S5_BS)


def _block_diag_c(cc):
    eye = jnp.eye(SUBLANES, dtype=F32)
    ct = cc.reshape(S5_BLOCKS, 8, S5_GROUP, S5_STATE).transpose(3, 0, 1, 2)
    return (ct[None, :, :, :, :] * eye[:, None, None, :, None]).reshape(S5_BS, S5_WIDTH)


def _diag_blocks_b(m):
    eye = jnp.eye(SUBLANES, dtype=F32)
    m5 = m.reshape(S5_BLOCKS, 8, S5_GROUP, 8, S5_STATE)
    return jnp.sum(m5 * eye[None, :, None, :, None], axis=3).transpose(0, 1, 3, 2).reshape(S5_GROUPS, S5_STATE, S5_GROUP)


def _diag_blocks_c(m):
    eye = jnp.eye(SUBLANES, dtype=F32)
    m5 = m.reshape(8, S5_STATE, S5_BLOCKS, 8, S5_GROUP)
    return jnp.sum(m5 * eye[:, None, None, :, None], axis=0).transpose(1, 2, 3, 0).reshape(S5_GROUPS, S5_GROUP, S5_STATE)


def local_step(x, target, p, win_p, conv_w8, wglu, wba, wbb, wout, w1, w2):
    t = x.shape[0]
    tm = min(256, t)
    d = D_MODEL
    h, xbc_raw, graw, z, u5, dtr = rms_matmul(x, p["norm_mix_g"], win_p, IN_SPLITS, tm=tm, name="in_proj")
    xbc = conv_silu_fwd(xbc_raw, conv_w8, p["conv_b"], tm=tm, name="conv_fwd")
    bias_row, alog_row = _lane_row(p["dt_bias"]), _lane_row(p["a_log"])
    d_row = jnp.repeat(p["d_ssd"], SSD_HEADDIM).reshape(1, SSD_HEADS * SSD_HEADDIM)
    ya, yssd, states = ssd_fwd(xbc, dtr, z, bias_row, alog_row, d_row, p["ssd_norm_g"], name="ssd_fwd")

    n = S5_N
    a_re_c, a_im_c = p["s5_a_re"].reshape(n, 1), p["s5_a_im"].reshape(n, 1)
    dt_c = jnp.repeat(p["s5_log_dt"], S5_STATE).reshape(n, 1)
    b_re_m, b_im_m = p["s5_b_re"].reshape(n, S5_GROUP), p["s5_b_im"].reshape(n, S5_GROUP)
    ab, bb_re, bb_im = s5_discretize(a_re_c, a_im_c, dt_c, b_re_m, b_im_m, name="s5_disc")
    ab8 = jnp.pad(ab.T, ((0, SUBLANES - 2), (0, 0)))
    bbc = jnp.concatenate([_block_diag_b(bb_re.reshape(S5_GROUPS, S5_STATE, S5_GROUP)),
                           _block_diag_b(bb_im.reshape(S5_GROUPS, S5_STATE, S5_GROUP))], axis=1).astype(BF16)
    ccc = jnp.concatenate([_block_diag_c(p["s5_c_re"]), -_block_diag_c(p["s5_c_im"])], axis=0).astype(BF16)
    s5d_row = p["s5_d"].reshape(1, S5_WIDTH)
    bglu_row = p["s5_glu_b"].reshape(1, S5_WIDTH)
    s, ys5, yb = s5_fwd(u5, bbc, ccc, ab8, s5d_row, wglu, bglu_row, tm=tm, name="s5_fwd")

    x1, pa, pb = merge_fwd(ya, yb, graw, x, wba, wbb, wout, tm=tm, name="merge_fwd")
    h2, a1 = rms_matmul(x1, p["norm_mlp_g"], w1, (D_FF,), tm=tm, name="mlp_in")
    act, dx2, dx2b, loss_row, dgf = mlp_out_loss(a1, x1, w2, p["norm_final_g"], target, tm=tm, name="mlp_out_loss")

    g = {}
    g["norm_final_g"] = dgf[0]
    da1 = mlp_bwd_act(dx2b, a1, w2, tm=tm, name="mlp_bwd_act")
    g["w_mlp_out"] = matmul_tn(act, dx2b, out_dtype=BF16, name="dw_mlp_out")
    g["w_mlp_in"] = matmul_tn(h2, da1, out_dtype=BF16, col_shards=N_DEV, name="dw_mlp_in")
    dx1, dx1b, dgm = matmul_nt_rms_bwd([da1], w1, x1, p["norm_mlp_g"], dx2, tm=tm, name="mlp_in_bwd")
    g["norm_mlp_g"] = dgm[0]
    merged, dgraw, dpa, dpb, dya, dyb = merge_bwd(dx1b, graw, pa, pb, wba, wbb, wout, tm=tm, name="merge_bwd")
    g["w_out"] = matmul_tn(merged, dx1b, out_dtype=BF16, name="dw_out")
    g["w_branch"] = jnp.concatenate([matmul_tn(ya, dpa, out_dtype=BF16, name="dw_branch_a"),
                                     matmul_tn(yb, dpb, out_dtype=BF16, name="dw_branch_b")], axis=0)

    du5, dbbc, dccc, dab, dd5, dwglu, dbglu = s5_bwd(dyb, ys5, u5, s, bbc, ccc, ab8, s5d_row, wglu, bglu_row,
                                                     tm=min(128, t), name="s5_bwd")
    g["s5_glu_w"], g["s5_glu_b"], g["s5_d"] = dwglu, dbglu[0], dd5[0]
    g["s5_c_re"] = _diag_blocks_c(dccc[:S5_BS])
    g["s5_c_im"] = -_diag_blocks_c(dccc[S5_BS:])
    dbb_re = _diag_blocks_b(dbbc[:, :S5_BS]).reshape(n, S5_GROUP)
    dbb_im = _diag_blocks_b(dbbc[:, S5_BS:]).reshape(n, S5_GROUP)
    da, dlogdt, gb_re, gb_im = s5_discretize_bwd(a_re_c, a_im_c, dt_c, b_re_m, b_im_m, dab[:2].T, dbb_re, dbb_im,
                                                  name="s5_disc_bwd")
    g["s5_a_re"] = da[:, 0].reshape(S5_GROUPS, S5_STATE)
    g["s5_a_im"] = da[:, 1].reshape(S5_GROUPS, S5_STATE)
    g["s5_log_dt"] = dlogdt[:, 0]
    g["s5_b_re"] = gb_re.reshape(S5_GROUPS, S5_STATE, S5_GROUP)
    g["s5_b_im"] = gb_im.reshape(S5_GROUPS, S5_STATE, S5_GROUP)

    dz, dxbc, ddtr, dng, dd_row, dalog_row, dbias_row = ssd_bwd(
        dya, yssd, z, xbc, dtr, states, bias_row, alog_row, d_row, p["ssd_norm_g"], name="ssd_bwd")
    g["ssd_norm_g"] = dng[0]
    g["d_ssd"], g["a_log"], g["dt_bias"] = dd_row[0, :SSD_HEADS], dalog_row[0, :SSD_HEADS], dbias_row[0, :SSD_HEADS]
    dxc, dconv_b, dconv_w8 = conv_silu_bwd_a(xbc_raw, dxbc, conv_w8, p["conv_b"], tm=tm, name="conv_bwd_a")
    g["conv_b"], g["conv_w"] = dconv_b[0], dconv_w8[:SSD_CONV]
    dxbc_raw = conv_bwd_b(dxc, conv_w8, tm=tm, name="conv_bwd_b")
    dproj = [dxbc_raw, dgraw, dz, du5, ddtr]
    g["w_in_p"] = [matmul_tn(h, piece, out_dtype=BF16, name=f"dw_in_{k}") for k, piece in enumerate(dproj)]
    grad_x, _, dgx = matmul_nt_rms_bwd(dproj, win_p, x, p["norm_mix_g"], dx1, tm=tm, name="in_proj_bwd")
    g["norm_mix_g"] = dgx[0]
    return loss_row[0, 0], grad_x, g


WEIGHT_ORDER = ["norm_mix_g", "w_in", "conv_w", "conv_b", "dt_bias", "a_log", "d_ssd", "ssd_norm_g", "s5_a_re",
                "s5_a_im", "s5_log_dt", "s5_b_re", "s5_b_im", "s5_c_re", "s5_c_im", "s5_d", "s5_glu_w", "s5_glu_b",
                "w_branch", "w_out", "norm_mlp_g", "w_mlp_in", "w_mlp_out", "norm_final_g"]
SHARDED = {"w_in": ((D_MODEL, D_IN_PROJ), 1), "conv_w": ((SSD_CONV, CONV_DIM), 1), "s5_glu_w": ((S5_WIDTH, S5_WIDTH), 0),
           "w_branch": ((D_MODEL + S5_WIDTH, D_MODEL), 0), "w_out": ((D_MODEL, D_MODEL), 0),
           "w_mlp_in": ((D_MODEL, D_FF), 1), "w_mlp_out": ((D_FF, D_MODEL), 0)}
SHARDED_ORDER = ["w_in", "conv_w", "s5_glu_w", "w_branch", "w_out", "w_mlp_in", "w_mlp_out"]
SMALL_ORDER = [n for n in WEIGHT_ORDER if n not in SHARDED]


def _shard_shape(name):
    (r, c), ax = SHARDED[name]
    return (r, c // N_DEV) if ax == 1 else (r // N_DEV, c)


def _join_shards(name, stacked):
    (r, c), ax = SHARDED[name]
    if ax == 1:
        return stacked.transpose(1, 0, 2).reshape(r, c)
    return stacked.reshape(r, c)


def _pack_small(values, loss):
    flat = jnp.concatenate([values[n].reshape(-1).astype(F32) for n in SMALL_ORDER] + [loss.reshape(1).astype(F32)])
    rows = -(-flat.shape[0] // (LANES * SUBLANES)) * SUBLANES
    return jnp.pad(flat, (0, rows * LANES - flat.shape[0])).reshape(rows, LANES)


def _unpack_small(rows, shapes):
    flat = rows.reshape(-1)
    out, off = {}, 0
    for n in SMALL_ORDER:
        size = math.prod(shapes[n])
        out[n] = flat[off:off + size].reshape(shapes[n])
        off += size
    return out, flat[off]


def kernel(x, norm_mix_g, w_in, conv_w, conv_b, dt_bias, a_log, d_ssd, ssd_norm_g, s5_a_re, s5_a_im, s5_log_dt, s5_b_re, s5_b_im, s5_c_re, s5_c_im, s5_d, s5_glu_w, s5_glu_b, w_branch, w_out, norm_mlp_g, w_mlp_in, w_mlp_out, norm_final_g, loss_target, m_norm_mix_g, m_w_in, m_conv_w, m_conv_b, m_dt_bias, m_a_log, m_d_ssd, m_ssd_norm_g, m_s5_a_re, m_s5_a_im, m_s5_log_dt, m_s5_b_re, m_s5_b_im, m_s5_c_re, m_s5_c_im, m_s5_d, m_s5_glu_w, m_s5_glu_b, m_w_branch, m_w_out, m_norm_mlp_g, m_w_mlp_in, m_w_mlp_out, m_norm_final_g, v_norm_mix_g, v_w_in, v_conv_w, v_conv_b, v_dt_bias, v_a_log, v_d_ssd, v_ssd_norm_g, v_s5_a_re, v_s5_a_im, v_s5_log_dt, v_s5_b_re, v_s5_b_im, v_s5_c_re, v_s5_c_im, v_s5_d, v_s5_glu_w, v_s5_glu_b, v_w_branch, v_w_out, v_norm_mlp_g, v_w_mlp_in, v_w_mlp_out, v_norm_final_g):
    w = dict(norm_mix_g=norm_mix_g, w_in=w_in, conv_w=conv_w, conv_b=conv_b, dt_bias=dt_bias, a_log=a_log, d_ssd=d_ssd,
             ssd_norm_g=ssd_norm_g, s5_a_re=s5_a_re, s5_a_im=s5_a_im, s5_log_dt=s5_log_dt, s5_b_re=s5_b_re,
             s5_b_im=s5_b_im, s5_c_re=s5_c_re, s5_c_im=s5_c_im, s5_d=s5_d, s5_glu_w=s5_glu_w, s5_glu_b=s5_glu_b,
             w_branch=w_branch, w_out=w_out, norm_mlp_g=norm_mlp_g, w_mlp_in=w_mlp_in, w_mlp_out=w_mlp_out,
             norm_final_g=norm_final_g)
    m = dict(norm_mix_g=m_norm_mix_g, w_in=m_w_in, conv_w=m_conv_w, conv_b=m_conv_b, dt_bias=m_dt_bias, a_log=m_a_log,
             d_ssd=m_d_ssd, ssd_norm_g=m_ssd_norm_g, s5_a_re=m_s5_a_re, s5_a_im=m_s5_a_im, s5_log_dt=m_s5_log_dt,
             s5_b_re=m_s5_b_re, s5_b_im=m_s5_b_im, s5_c_re=m_s5_c_re, s5_c_im=m_s5_c_im, s5_d=m_s5_d,
             s5_glu_w=m_s5_glu_w, s5_glu_b=m_s5_glu_b, w_branch=m_w_branch, w_out=m_w_out, norm_mlp_g=m_norm_mlp_g,
             w_mlp_in=m_w_mlp_in, w_mlp_out=m_w_mlp_out, norm_final_g=m_norm_final_g)
    v = dict(norm_mix_g=v_norm_mix_g, w_in=v_w_in, conv_w=v_conv_w, conv_b=v_conv_b, dt_bias=v_dt_bias, a_log=v_a_log,
             d_ssd=v_d_ssd, ssd_norm_g=v_ssd_norm_g, s5_a_re=v_s5_a_re, s5_a_im=v_s5_a_im, s5_log_dt=v_s5_log_dt,
             s5_b_re=v_s5_b_re, s5_b_im=v_s5_b_im, s5_c_re=v_s5_c_re, s5_c_im=v_s5_c_im, s5_d=v_s5_d,
             s5_glu_w=v_s5_glu_w, s5_glu_b=v_s5_glu_b, w_branch=v_w_branch, w_out=v_w_out, norm_mlp_g=v_norm_mlp_g,
             w_mlp_in=v_w_mlp_in, w_mlp_out=v_w_mlp_out, norm_final_g=v_norm_final_g)

    blocks = [w[n] if n == "conv_w" else w[n].astype(BF16) for n in SHARDED_ORDER]
    stacked = dict(zip(SHARDED_ORDER, all_gather_arrays(blocks, name="gather_weights")))
    wi = _join_shards("w_in", stacked["w_in"])
    win_p = jnp.concatenate([wi[:, 1024:3072], wi[:, 3600:5648], wi[:, 0:1024], wi[:, 3088:3600], wi[:, 3072:3088],
                             jnp.zeros((D_MODEL, LANES - SSD_HEADS), BF16)], axis=1)
    conv_w8 = jnp.pad(_join_shards("conv_w", stacked["conv_w"]), ((0, SUBLANES - SSD_CONV), (0, 0)))
    w_branch = stacked["w_branch"].reshape(D_MODEL + S5_WIDTH, D_MODEL)
    small = {n: w[n] for n in SMALL_ORDER}

    loss_part, grad_x, g = local_step(
        x[0], loss_target[0], small, win_p, conv_w8, stacked["s5_glu_w"].reshape(S5_WIDTH, S5_WIDTH),
        w_branch[:D_MODEL], w_branch[D_MODEL:], stacked["w_out"].reshape(D_MODEL, D_MODEL),
        stacked["w_mlp_in"], stacked["w_mlp_out"].reshape(D_FF, D_MODEL))

    dxbc_w, dgate_w, dz_w, du5_w, ddt_w = g.pop("w_in_p")
    dw_in = jnp.concatenate([dz_w, dxbc_w, ddt_w[:, :SSD_HEADS], du5_w, dgate_w], axis=1)
    parts = {"w_in": dw_in.reshape(D_MODEL, N_DEV, D_IN_PROJ // N_DEV).transpose(1, 0, 2),
             "conv_w": g["conv_w"].reshape(SSD_CONV, N_DEV, CONV_DIM // N_DEV).transpose(1, 0, 2),
             "s5_glu_w": g["s5_glu_w"].astype(BF16).reshape((N_DEV,) + _shard_shape("s5_glu_w")),
             "w_branch": g["w_branch"].reshape((N_DEV,) + _shard_shape("w_branch")),
             "w_out": g["w_out"].reshape((N_DEV,) + _shard_shape("w_out")),
             "w_mlp_in": g["w_mlp_in"],
             "w_mlp_out": g["w_mlp_out"].reshape((N_DEV,) + _shard_shape("w_mlp_out"))}
    small_rows = _pack_small(g, loss_part)
    sent = [parts[n] for n in SHARDED_ORDER] + [jnp.broadcast_to(small_rows, (N_DEV,) + small_rows.shape)]
    got = exchange_partials(sent, name="exchange_grads")

    zero = jnp.zeros((), F32)
    outs = {n: adamw(w[n], m[n], v[n], got[k], name=f"adamw_{n}") for k, n in enumerate(SHARDED_ORDER)}
    outs_small = adamw(_pack_small(w, zero), _pack_small(m, zero), _pack_small(v, zero), got[-1], name="adamw_small")
    shapes = {n: w[n].shape for n in WEIGHT_ORDER}
    results = []
    loss = None
    for j, p_small in enumerate(outs_small):
        vals = {n: outs[n][j] for n in SHARDED_ORDER}
        sm, extra = _unpack_small(p_small, shapes)
        vals.update(sm)
        if loss is None:
            loss = extra
        results.append(vals)
    grads, deltas, new_m, new_v = results
    return (loss, grad_x.reshape(x.shape), *[grads[n] for n in WEIGHT_ORDER], *[deltas[n] for n in WEIGHT_ORDER],
            *[new_m[n] for n in WEIGHT_ORDER], *[new_v[n] for n in WEIGHT_ORDER])
```

```python
import functools
import math

import jax
import jax.numpy as jnp
from jax import lax
from jax.experimental import pallas as pl
from jax.experimental.pallas import tpu as pltpu

F32 = jnp.float32
BF16 = jnp.bfloat16
HI = lax.Precision.HIGHEST

N_DEV = 8
D_MODEL = 1024
SSD_HEADS = 16
SSD_HEADDIM = 64
SSD_GROUPS = 4
SSD_HPG = 4
SSD_STATE = 128
SSD_CHUNK = 128
SSD_CONV = 4
CONV_DIM = 2048
S5_WIDTH = 512
S5_GROUP = 16
S5_GROUPS = 32
S5_STATE = 64
S5_N = S5_GROUPS * S5_STATE
D_FF = 4096
D_IN_PROJ = 5648
IN_SPLITS = (2048, 2048, 1024, 512, 128)
D_IN_PAD = sum(IN_SPLITS)
EPS = 1e-6
LANES = 128
SUBLANES = 8
VMEM_LIMIT = 56 * 1024 * 1024

ADAM_LR = 0.001
ADAM_B1 = 0.9
ADAM_B2 = 0.999
ADAM_EPS = 1e-08
ADAM_WD = 0.01
ADAM_STEP = 10

GELU_C = math.sqrt(2.0 / math.pi)
GELU_K = 0.044715


def _params(*sem):
    return pltpu.CompilerParams(dimension_semantics=sem, vmem_limit_bytes=VMEM_LIMIT)


def _full(shape):
    nd = len(shape)
    return pl.BlockSpec(shape, lambda *_: (0,) * nd)


_MESH = pl.DeviceIdType.MESH
_RELATIONS = [(dx, dy, dc) for dx in (0, 1) for dy in (0, 1) for dc in (0, 1)][1:]
N_REL = len(_RELATIONS)


def _place():
    x, y, c = lax.axis_index("x"), lax.axis_index("y"), lax.axis_index("c")
    return x, y, c, 4 * x + 2 * y + c


def _comm_out_shapes(comm):
    kind, arrays = comm
    lead = (N_DEV,) if kind == "gather" else ()
    return [jax.ShapeDtypeStruct(lead + a.shape, a.dtype) for a in arrays]


def _comm_copies(kind, src_refs, dst_refs, send_sems, recv_sems, local_sems):
    x, y, c, idx = _place()
    local, remote = [], []
    for a, (src, dst) in enumerate(zip(src_refs, dst_refs)):
        local.append(pltpu.make_async_copy(src if kind == "gather" else src.at[idx], dst.at[idx], local_sems.at[a]))
        for k, (dx, dy, dc) in enumerate(_RELATIONS):
            px, py, pc = x ^ dx, y ^ dy, c ^ dc
            remote.append(pltpu.make_async_remote_copy(
                src_ref=src if kind == "gather" else src.at[4 * px + 2 * py + pc], dst_ref=dst.at[idx],
                send_sem=send_sems.at[N_REL * a + k], recv_sem=recv_sems.at[N_REL * a + k],
                device_id=(px, py, pc), device_id_type=_MESH))
    return local, remote


def _tiled_call(body, *, name, grid, in_specs, out_specs, out_shape, operands, scratch_shapes=(), comm=None):
    params = pltpu.CompilerParams(dimension_semantics=("arbitrary",), vmem_limit_bytes=VMEM_LIMIT,
                                  has_side_effects=comm is not None)
    if comm is None:
        outs = pl.pallas_call(body, name=name, grid=grid, in_specs=in_specs, out_specs=out_specs, out_shape=out_shape,
                              scratch_shapes=list(scratch_shapes), compiler_params=params)(*operands)
        return outs, []
    kind, arrays = comm
    na, n_in, n_out, n_scr = len(arrays), len(in_specs), len(out_specs), len(scratch_shapes)
    last = grid[0] - 1

    def hosted(*refs):
        ins, refs = refs[:n_in], refs[n_in:]
        cin, refs = refs[:na], refs[na:]
        outs, refs = refs[:n_out], refs[n_out:]
        cout, refs = refs[:na], refs[na:]
        scr, sems = refs[:n_scr], refs[n_scr:]
        i = pl.program_id(0)

        @pl.when(i == 0)
        def _():
            local, remote = _comm_copies(kind, cin, cout, *sems)
            for cp in local + remote:
                cp.start()

        body(*ins, *outs, *scr)

        @pl.when(i == last)
        def _():
            local, remote = _comm_copies(kind, cin, cout, *sems)
            for cp in remote:
                cp.wait_recv()
            for cp in remote:
                cp.wait_send()
            for cp in local:
                cp.wait()

    any_spec = pl.BlockSpec(memory_space=pl.ANY)
    res = pl.pallas_call(
        hosted, name=name, grid=grid,
        in_specs=list(in_specs) + [any_spec] * na, out_specs=list(out_specs) + [any_spec] * na,
        out_shape=list(out_shape) + _comm_out_shapes(comm),
        scratch_shapes=list(scratch_shapes) + [pltpu.SemaphoreType.DMA((N_REL * na,)), pltpu.SemaphoreType.DMA((N_REL * na,)),
                                               pltpu.SemaphoreType.DMA((na,))],
        compiler_params=params)(*operands, *arrays)
    return res[:n_out], res[n_out:]


def _dot(a, b):
    return jnp.dot(a, b, preferred_element_type=F32)


def _dot_nt(a, b):
    return lax.dot_general(a, b, (((1,), (1,)), ((), ())), preferred_element_type=F32)


def _dot_tn(a, b):
    return lax.dot_general(a, b, (((0,), (0,)), ((), ())), preferred_element_type=F32)


def _dot_hi(a, b):
    return jnp.dot(a, b, preferred_element_type=F32, precision=HI)


def _dot_nt_hi(a, b):
    return lax.dot_general(a, b, (((1,), (1,)), ((), ())), preferred_element_type=F32, precision=HI)


def _dot_tn_hi(a, b):
    return lax.dot_general(a, b, (((0,), (0,)), ((), ())), preferred_element_type=F32, precision=HI)


def _sigmoid(x):
    return 1.0 / (1.0 + jnp.exp(-x))


def _softplus(x):
    return jnp.maximum(x, 0.0) + jnp.log(1.0 + jnp.exp(-jnp.abs(x)))


def _gelu(x):
    return 0.5 * x * (1.0 + jnp.tanh(GELU_C * (x + GELU_K * x * x * x)))


def _gelu_grad(x):
    th = jnp.tanh(GELU_C * (x + GELU_K * x * x * x))
    return 0.5 * (1.0 + th) + 0.5 * x * (1.0 - th * th) * GELU_C * (1.0 + 3.0 * GELU_K * x * x)


def rms_matmul(x, g, w, splits, *, tm, name, comm=None):
    t, d = x.shape
    stacked = w.ndim == 3
    n = w.shape[0] * w.shape[2] if stacked else w.shape[1]
    assert sum(splits) == n and (len(splits) == 1 or not stacked)

    def body(x_ref, g_ref, w_ref, h_ref, *o_refs):
        xv = x_ref[...]
        r = lax.rsqrt(jnp.mean(xv * xv, axis=-1, keepdims=True) + EPS)
        h = (xv * r * g_ref[...]).astype(BF16)
        h_ref[...] = h
        if stacked:
            wk = w.shape[2]
            for k in range(w.shape[0]):
                o_refs[0][:, k * wk:(k + 1) * wk] = _dot(h, w_ref[k])
            return
        off = 0
        for o_ref, width in zip(o_refs, splits):
            o_ref[...] = _dot(h, w_ref[:, off:off + width])
            off += width

    tile = lambda c: pl.BlockSpec((tm, c), lambda i: (i, 0))
    return _tiled_call(
        body, name=name, grid=(t // tm,),
        in_specs=[tile(d), _full((1, d)), _full(w.shape)],
        out_specs=[tile(d)] + [tile(c) for c in splits],
        out_shape=[jax.ShapeDtypeStruct((t, d), BF16)] + [jax.ShapeDtypeStruct((t, c), F32) for c in splits],
        operands=(x, g.reshape(1, d), w), comm=comm)


def _pick(n, cap):
    best = LANES
    for c in range(LANES, cap + 1, LANES):
        if n % c == 0:
            best = c
    return best


def matmul_tn(a, b, *, name, out_dtype=F32, col_shards=None, tk=1024):
    t, m = a.shape
    n = b.shape[1]
    tk = min(tk, t)
    tm = _pick(m, 1024)
    tn = n // col_shards if col_shards else _pick(n, 1280)
    nk = t // tk

    def body(a_ref, b_ref, o_ref, acc):
        k = pl.program_id(2)

        @pl.when(k == 0)
        def _():
            acc[...] = jnp.zeros_like(acc)

        acc[...] += _dot_tn(a_ref[...], b_ref[...])

        @pl.when(k == nk - 1)
        def _():
            o_ref[...] = acc[...].reshape(o_ref.shape).astype(out_dtype)

    if col_shards:
        out_spec = pl.BlockSpec((1, tm, tn), lambda i, j, k: (j, i, 0))
        out_shape = jax.ShapeDtypeStruct((col_shards, m, tn), out_dtype)
    else:
        out_spec = pl.BlockSpec((tm, tn), lambda i, j, k: (i, j))
        out_shape = jax.ShapeDtypeStruct((m, n), out_dtype)
    return pl.pallas_call(
        body, name=name, grid=(m // tm, n // tn, nk),
        in_specs=[pl.BlockSpec((tk, tm), lambda i, j, k: (k, i)), pl.BlockSpec((tk, tn), lambda i, j, k: (k, j))],
        out_specs=out_spec, out_shape=out_shape,
        scratch_shapes=[pltpu.VMEM((tm, tn), F32)],
        compiler_params=_params("parallel", "parallel", "arbitrary"),
    )(a, b)


def matmul_nt_rms_bwd(dys, w, x, g, dres, *, tm, name, comm=None):
    t = x.shape[0]
    stacked = w.ndim == 3
    d = w.shape[1] if stacked else w.shape[0]
    widths = [dy.shape[1] for dy in dys]
    n_dy = len(dys)

    def body(*refs):
        dy_refs = refs[:n_dy]
        w_ref, x_ref, g_ref, dres_ref, dx_ref, dxb_ref, dg_ref = refs[n_dy:]
        i = pl.program_id(0)
        if stacked:
            wk = w.shape[2]
            dh = _dot_nt(dy_refs[0][:, 0:wk], w_ref[0])
            for k in range(1, w.shape[0]):
                dh = dh + _dot_nt(dy_refs[0][:, k * wk:(k + 1) * wk], w_ref[k])
        else:
            dh, off = None, 0
            for dy_ref, width in zip(dy_refs, widths):
                part = _dot_nt(dy_ref[...], w_ref[:, off:off + width])
                dh = part if dh is None else dh + part
                off += width
        xv = x_ref[...]
        r = lax.rsqrt(jnp.mean(xv * xv, axis=-1, keepdims=True) + EPS)
        xn = xv * r
        dxn = dh * g_ref[...]
        dx = dres_ref[...] + r * (dxn - xn * jnp.mean(dxn * xn, axis=-1, keepdims=True))
        dx_ref[...] = dx
        dxb_ref[...] = dx.astype(BF16)

        @pl.when(i == 0)
        def _():
            dg_ref[...] = jnp.zeros_like(dg_ref)

        dg_ref[...] += jnp.sum(dh * xn, axis=0, keepdims=True)

    tile = lambda c: pl.BlockSpec((tm, c), lambda i: (i, 0))
    return _tiled_call(
        body, name=name, grid=(t // tm,),
        in_specs=[tile(c) for c in widths] + [_full(w.shape), tile(d), _full((1, d)), tile(d)],
        out_specs=[tile(d), tile(d), _full((1, d))],
        out_shape=[jax.ShapeDtypeStruct((t, d), F32), jax.ShapeDtypeStruct((t, d), BF16),
                   jax.ShapeDtypeStruct((1, d), F32)],
        operands=(*dys, w, x, g.reshape(1, d), dres), comm=comm)


def _conv_taps(prev8, cur, w_ref, tm):
    ext = jnp.concatenate([prev8, cur], axis=0)
    acc = cur * w_ref[3:4, :]
    for k in range(3):
        off = SUBLANES - 3 + k
        acc = acc + ext[off:off + tm, :] * w_ref[k:k + 1, :]
    return acc


def conv_silu_fwd(u, w8, b, *, tm, name):
    t, c = u.shape
    per = tm // SUBLANES

    def body(prev_ref, cur_ref, w_ref, b_ref, o_ref):
        i = pl.program_id(0)
        prev8 = jnp.where(i == 0, 0.0, prev_ref[...])
        xc = _conv_taps(prev8, cur_ref[...], w_ref, tm) + b_ref[...]
        o_ref[...] = xc * _sigmoid(xc)

    return pl.pallas_call(
        body, name=name, grid=(t // tm,),
        in_specs=[pl.BlockSpec((SUBLANES, c), lambda i: (jnp.maximum(i * per - 1, 0), 0)),
                  pl.BlockSpec((tm, c), lambda i: (i, 0)), _full((SUBLANES, c)), _full((1, c))],
        out_specs=pl.BlockSpec((tm, c), lambda i: (i, 0)),
        out_shape=jax.ShapeDtypeStruct((t, c), F32),
        compiler_params=_params("parallel"),
    )(u, u, w8, b.reshape(1, c))


def conv_silu_bwd_a(u, dy, w8, b, *, tm, name):
    t, c = u.shape
    per = tm // SUBLANES

    def body(prev_ref, cur_ref, dy_ref, w_ref, b_ref, dxc_ref, db_ref, dw_ref):
        i = pl.program_id(0)
        prev8 = jnp.where(i == 0, 0.0, prev_ref[...])
        cur = cur_ref[...]
        xc = _conv_taps(prev8, cur, w_ref, tm) + b_ref[...]
        sg = _sigmoid(xc)
        dxc = dy_ref[...] * (sg * (1.0 + xc * (1.0 - sg)))
        dxc_ref[...] = dxc

        @pl.when(i == 0)
        def _():
            db_ref[...] = jnp.zeros_like(db_ref)
            dw_ref[...] = jnp.zeros_like(dw_ref)

        db_ref[...] += jnp.sum(dxc, axis=0, keepdims=True)
        ext = jnp.concatenate([prev8, cur], axis=0)
        rows = []
        for k in range(SSD_CONV):
            off = SUBLANES - 3 + k
            rows.append(jnp.sum(dxc * ext[off:off + tm, :], axis=0, keepdims=True))
        rows.append(jnp.zeros((SUBLANES - SSD_CONV, c), F32))
        dw_ref[...] += jnp.concatenate(rows, axis=0)

    return pl.pallas_call(
        body, name=name, grid=(t // tm,),
        in_specs=[pl.BlockSpec((SUBLANES, c), lambda i: (jnp.maximum(i * per - 1, 0), 0)),
                  pl.BlockSpec((tm, c), lambda i: (i, 0)), pl.BlockSpec((tm, c), lambda i: (i, 0)),
                  _full((SUBLANES, c)), _full((1, c))],
        out_specs=[pl.BlockSpec((tm, c), lambda i: (i, 0)), _full((1, c)), _full((SUBLANES, c))],
        out_shape=[jax.ShapeDtypeStruct((t, c), F32), jax.ShapeDtypeStruct((1, c), F32),
                   jax.ShapeDtypeStruct((SUBLANES, c), F32)],
        compiler_params=_params("arbitrary"),
    )(u, u, dy, w8, b.reshape(1, c))


def conv_bwd_b(dxc, w8, *, tm, name):
    t, c = dxc.shape
    per = tm // SUBLANES
    last = t // SUBLANES - 1
    nt = t // tm

    def body(cur_ref, next_ref, w_ref, o_ref):
        i = pl.program_id(0)
        nxt = jnp.where(i == nt - 1, 0.0, next_ref[...])
        cur = cur_ref[...]
        ext = jnp.concatenate([cur, nxt], axis=0)
        acc = cur * w_ref[3:4, :]
        for j in range(1, SSD_CONV):
            acc = acc + ext[j:j + tm, :] * w_ref[3 - j:4 - j, :]
        o_ref[...] = acc.astype(BF16)

    return pl.pallas_call(
        body, name=name, grid=(nt,),
        in_specs=[pl.BlockSpec((tm, c), lambda i: (i, 0)),
                  pl.BlockSpec((SUBLANES, c), lambda i: (jnp.minimum((i + 1) * per, last), 0)), _full((SUBLANES, c))],
        out_specs=pl.BlockSpec((tm, c), lambda i: (i, 0)),
        out_shape=jax.ShapeDtypeStruct((t, c), BF16),
        compiler_params=_params("parallel"),
    )(dxc, dxc, w8)


def _ssd_chunk_terms(dtr, bias_row, alog_row):
    q = SSD_CHUNK
    row = lax.broadcasted_iota(jnp.int32, (q, q), 0)
    col = lax.broadcasted_iota(jnp.int32, (q, q), 1)
    ltri = (col <= row).astype(F32)
    dt = _softplus(dtr + bias_row)
    a_row = -jnp.exp(alog_row)
    la = dt * a_row
    cum = _dot_hi(ltri, la)
    cum_t = _dot_tn_hi(la, (row <= col).astype(F32))
    return dt, a_row, cum, cum_t, ltri, row, col


def _head_maps():
    di = SSD_HEADS * SSD_HEADDIM
    expand = (lax.broadcasted_iota(jnp.int32, (LANES, di), 1) // SSD_HEADDIM
              == lax.broadcasted_iota(jnp.int32, (LANES, di), 0)).astype(F32)
    collapse = (lax.broadcasted_iota(jnp.int32, (di, LANES), 0) // SSD_HEADDIM
                == lax.broadcasted_iota(jnp.int32, (di, LANES), 1)).astype(F32)
    return expand, collapse


def _ssd_decay(cum, cum_t, h, causal):
    seg = cum[:, h:h + 1] - cum_t[h:h + 1, :]
    return jnp.where(causal, jnp.exp(jnp.where(causal, seg, 0.0)), 0.0)


def _pair_block_diag(x2):
    low = lax.broadcasted_iota(jnp.int32, x2.shape, 1) < SSD_HEADDIM
    zero = jnp.zeros_like(x2)
    return jnp.concatenate([jnp.where(low, x2, zero), jnp.where(low, zero, x2)], axis=0)


def ssd_fwd(xbc, dtr, z, bias_row, alog_row, d_row, norm_g, *, name, comm=None):
    t = xbc.shape[0]
    q, p, n = SSD_CHUNK, SSD_HEADDIM, SSD_STATE
    nc = t // q
    di = SSD_HEADS * p
    gw = di // SSD_GROUPS

    def body(xs_ref, b_ref, c_ref, dtr_ref, z_ref, bias_ref, alog_ref, d_ref, g_ref,
             ya_ref, yssd_ref, st_ref, state):
        ci = pl.program_id(0)

        @pl.when(ci == 0)
        def _():
            state[...] = jnp.zeros_like(state)

        st_ref[0] = state[...]
        dt, _, cum, cum_t, _, row, col = _ssd_chunk_terms(dtr_ref[...], bias_ref[...], alog_ref[...])
        causal = row >= col
        expand, _ = _head_maps()
        dt_full = _dot_hi(dt, expand)
        cum_full = _dot_hi(cum, expand)
        last_full = cum_full[q - 1:q, :]
        xs = xs_ref[...]
        xdt = xs * dt_full
        xd = (xdt * jnp.exp(last_full - cum_full)).astype(BF16)
        xdt_b = xdt.astype(BF16)
        e_full = jnp.exp(cum_full)
        eend_full = jnp.exp(last_full)
        for g in range(SSD_GROUPS):
            slab = slice(g * gw, (g + 1) * gw)
            bg = b_ref[:, g * n:(g + 1) * n].astype(BF16)
            cg = c_ref[:, g * n:(g + 1) * n].astype(BF16)
            cb = _dot_nt(cg, bg)
            st_g = state[:, slab]
            pairs = []
            for pr in range(SSD_HPG // 2):
                h0 = g * SSD_HPG + 2 * pr
                m0 = (cb * _ssd_decay(cum, cum_t, h0, causal)).astype(BF16)
                m1 = (cb * _ssd_decay(cum, cum_t, h0 + 1, causal)).astype(BF16)
                pairs.append(_dot(jnp.concatenate([m0, m1], axis=1), _pair_block_diag(xdt_b[:, h0 * p:(h0 + 2) * p])))
            y = jnp.concatenate(pairs, axis=1) + e_full[:, slab] * _dot(cg, st_g.astype(BF16))
            yssd_ref[:, slab] = y + d_ref[:, slab] * xs[:, slab]
            state[:, slab] = eend_full[:, slab] * st_g + _dot_tn(bg, xd[:, slab])
        zv = z_ref[...]
        ya1 = yssd_ref[...] * (zv * _sigmoid(zv))
        for g in range(SSD_GROUPS):
            gsl = slice(g * gw, (g + 1) * gw)
            sl = ya1[:, gsl]
            rg = lax.rsqrt(jnp.mean(sl * sl, axis=-1, keepdims=True) + EPS)
            ya_ref[:, gsl] = (sl * rg * g_ref[:, gsl]).astype(BF16)

    blk = lambda w, j: pl.BlockSpec((q, w), lambda c: (c, j))
    return _tiled_call(
        body, name=name, grid=(nc,),
        in_specs=[blk(di, 0), blk(512, 2), blk(512, 3), blk(LANES, 0), blk(di, 0),
                  _full((1, LANES)), _full((1, LANES)), _full((1, di)), _full((1, di))],
        out_specs=[blk(di, 0), blk(di, 0), pl.BlockSpec((1, n, di), lambda c: (c, 0, 0))],
        out_shape=[jax.ShapeDtypeStruct((t, di), BF16), jax.ShapeDtypeStruct((t, di), F32),
                   jax.ShapeDtypeStruct((nc, n, di), F32)],
        scratch_shapes=[pltpu.VMEM((n, di), F32)],
        operands=(xbc, xbc, xbc, dtr, z, bias_row, alog_row, d_row, norm_g.reshape(1, di)), comm=comm)


def ssd_bwd(dya, yssd, z, xbc, dtr, states, bias_row, alog_row, d_row, norm_g, *, name, comm=None):
    t = xbc.shape[0]
    q, p, n = SSD_CHUNK, SSD_HEADDIM, SSD_STATE
    nc = t // q
    di = SSD_HEADS * p
    gw = di // SSD_GROUPS

    def body(dya_ref, yssd_ref, z_ref, xs_ref, b_ref, c_ref, dtr_ref, st_ref, bias_ref, alog_ref, d_ref, g_ref,
             dz_ref, dxbc_ref, ddtr_ref, dng_ref, dd_ref, dalog_ref, dbias_ref, dstate, dyssd):
        ci = pl.program_id(0)

        @pl.when(ci == 0)
        def _():
            dstate[...] = jnp.zeros_like(dstate)
            dng_ref[...] = jnp.zeros_like(dng_ref)
            dd_ref[...] = jnp.zeros_like(dd_ref)
            dalog_ref[...] = jnp.zeros_like(dalog_ref)
            dbias_ref[...] = jnp.zeros_like(dbias_ref)

        zv = z_ref[...]
        sz = _sigmoid(zv)
        silu_z = zv * sz
        yv = yssd_ref[...]
        ya1 = yv * silu_z
        dyav = dya_ref[...]
        for g in range(SSD_GROUPS):
            gsl = slice(g * gw, (g + 1) * gw)
            sl = ya1[:, gsl]
            rg = lax.rsqrt(jnp.mean(sl * sl, axis=-1, keepdims=True) + EPS)
            ya2 = sl * rg
            dy_g = dyav[:, gsl]
            dng_ref[:, gsl] += jnp.sum(dy_g * ya2, axis=0, keepdims=True)
            dya2 = dy_g * g_ref[:, gsl]
            dya1 = rg * (dya2 - ya2 * jnp.mean(dya2 * ya2, axis=-1, keepdims=True))
            dyssd[:, gsl] = dya1 * silu_z[:, gsl]
            dz_ref[:, gsl] = (dya1 * yv[:, gsl] * (sz[:, gsl] * (1.0 + zv[:, gsl] * (1.0 - sz[:, gsl])))).astype(BF16)

        dtr_v = dtr_ref[...]
        dt, a_row, cum, cum_t, ltri, row, col = _ssd_chunk_terms(dtr_v, bias_ref[...], alog_ref[...])
        causal = row >= col
        expand, collapse = _head_maps()
        lane = lax.broadcasted_iota(jnp.int32, (1, LANES), 1)
        sub = lax.broadcasted_iota(jnp.int32, (LANES, 1), 0)
        is_last = lax.broadcasted_iota(jnp.int32, (q, 1), 0) == q - 1
        low = lax.broadcasted_iota(jnp.int32, (q, 2 * p), 1) < p
        dt_full = _dot_hi(dt, expand)
        cum_full = _dot_hi(cum, expand)
        last_full = cum_full[q - 1:q, :]
        e_full = jnp.exp(cum_full)
        eend_full = jnp.exp(last_full)
        dend_full = jnp.exp(last_full - cum_full)
        xs = xs_ref[...]
        xdt = xs * dt_full
        xdt_b = xdt.astype(BF16)
        xd_b = (xdt * dend_full).astype(BF16)
        dy_all = dyssd[...]
        dy_b = dy_all.astype(BF16)
        dg_b = (dy_all * e_full).astype(BF16)
        dcum_mat = jnp.zeros((q, LANES), F32)
        rmat = jnp.zeros((LANES, q), F32)
        yoff_prod, bds_list, dx_list, dlast_parts = [], [], [], []
        for g in range(SSD_GROUPS):
            slab = slice(g * gw, (g + 1) * gw)
            bg = b_ref[:, g * n:(g + 1) * n].astype(BF16)
            cg = c_ref[:, g * n:(g + 1) * n].astype(BF16)
            cb = _dot_nt(cg, bg)
            st_g = st_ref[0, :, slab]
            st_b = st_g.astype(BF16)
            dsn = dstate[:, slab]
            dsn_b = dsn.astype(BF16)
            yoff_prod.append(dy_all[:, slab] * e_full[:, slab] * _dot(cg, st_b))
            dcg = _dot_nt(dg_b[:, slab], st_b)
            dstate[:, slab] = _dot_tn(cg, dg_b[:, slab]) + eend_full[:, slab] * dsn
            bds_list.append(_dot(bg, dsn_b))
            dbg = _dot_nt(xd_b[:, slab], dsn_b)
            dlast_parts.append(jnp.sum(dsn * st_g, axis=0, keepdims=True))
            dcb = jnp.zeros((q, q), F32)
            dx_pairs = []
            for pr in range(SSD_HPG // 2):
                h0 = g * SSD_HPG + 2 * pr
                ps = slice(h0 * p, (h0 + 2) * p)
                l0 = _ssd_decay(cum, cum_t, h0, causal)
                l1 = _ssd_decay(cum, cum_t, h0 + 1, causal)
                m0, m1 = cb * l0, cb * l1
                dyp = dy_b[:, ps]
                zero = jnp.zeros_like(dyp)
                dy0, dy1 = jnp.where(low, dyp, zero), jnp.where(low, zero, dyp)
                dm0 = _dot_nt(dy0, xdt_b[:, ps])
                dm1 = _dot_nt(dy1, xdt_b[:, ps])
                w0, w1 = dm0 * m0, dm1 * m1
                dcum_mat = (dcum_mat + jnp.sum(w0, axis=1, keepdims=True) * (lane == h0).astype(F32)
                            + jnp.sum(w1, axis=1, keepdims=True) * (lane == h0 + 1).astype(F32))
                rmat = (rmat + jnp.where(sub == h0, jnp.sum(w0, axis=0, keepdims=True), 0.0)
                        + jnp.where(sub == h0 + 1, jnp.sum(w1, axis=0, keepdims=True), 0.0))
                dcb = dcb + dm0 * l0 + dm1 * l1
                dx_pairs.append(_dot_tn(jnp.concatenate([m0.astype(BF16), m1.astype(BF16)], axis=0),
                                        jnp.concatenate([dy0, dy1], axis=0)))
            dx_list.append(jnp.concatenate(dx_pairs, axis=1))
            dcb_b = dcb.astype(BF16)
            dxbc_ref[:, di + g * n:di + (g + 1) * n] = dbg + _dot_tn(dcb_b, cg)
            dxbc_ref[:, di + SSD_GROUPS * n + g * n:di + SSD_GROUPS * n + (g + 1) * n] = dcg + _dot(dcb_b, bg)
        bds_all = jnp.concatenate(bds_list, axis=1)
        dx_all = jnp.concatenate(dx_list, axis=1) + dend_full * bds_all
        last_row = cum[q - 1:q, :]
        qv = _dot_hi(xdt * bds_all, collapse) * jnp.exp(last_row - cum)
        dcum_mat = dcum_mat + _dot_hi(jnp.concatenate(yoff_prod, axis=1), collapse) - qv
        dlast_full = jnp.broadcast_to(jnp.concatenate(dlast_parts, axis=1), (SUBLANES, di))
        dlast_row = _dot_hi(dlast_full, collapse)[0:1, :] * jnp.exp(last_row) + jnp.sum(qv, axis=0, keepdims=True)
        dcum_mat = dcum_mat + jnp.where(is_last, dlast_row, 0.0)
        dla = _dot_tn_hi(ltri, dcum_mat) - _dot_nt_hi((row <= col).astype(F32), rmat)
        ddt_mat = _dot_hi(dx_all * xs, collapse) + dla * a_row
        dxbc_ref[:, 0:di] = d_ref[...] * dy_all + dx_all * dt_full
        ddtr = ddt_mat * _sigmoid(dtr_v + bias_ref[...])
        ddtr_ref[...] = ddtr.astype(BF16)
        dd_full = jnp.broadcast_to(jnp.sum(dy_all * xs, axis=0, keepdims=True), (SUBLANES, di))
        dd_ref[...] += _dot_hi(dd_full, collapse)[0:1, :]
        dalog_ref[...] += jnp.sum(dla * dt, axis=0, keepdims=True) * a_row
        dbias_ref[...] += jnp.sum(ddtr, axis=0, keepdims=True)

    blk = lambda w, j: pl.BlockSpec((q, w), lambda c: (nc - 1 - c, j))
    row_out = lambda w: jax.ShapeDtypeStruct((1, w), F32)
    return _tiled_call(
        body, name=name, grid=(nc,),
        in_specs=[blk(di, 0), blk(di, 0), blk(di, 0), blk(di, 0), blk(512, 2), blk(512, 3), blk(LANES, 0),
                  pl.BlockSpec((1, n, di), lambda c: (nc - 1 - c, 0, 0)),
                  _full((1, LANES)), _full((1, LANES)), _full((1, di)), _full((1, di))],
        out_specs=[blk(di, 0), blk(CONV_DIM, 0), blk(LANES, 0),
                   _full((1, di)), _full((1, LANES)), _full((1, LANES)), _full((1, LANES))],
        out_shape=[jax.ShapeDtypeStruct((t, di), BF16), jax.ShapeDtypeStruct((t, CONV_DIM), F32),
                   jax.ShapeDtypeStruct((t, LANES), BF16), row_out(di), row_out(LANES), row_out(LANES), row_out(LANES)],
        scratch_shapes=[pltpu.VMEM((n, di), F32), pltpu.VMEM((q, di), F32)],
        operands=(dya, yssd, z, xbc, xbc, xbc, dtr, states, bias_row, alog_row, d_row, norm_g.reshape(1, di)),
        comm=comm)


S5_BLOCKS = 4
S5_BC = S5_WIDTH // S5_BLOCKS
S5_BS = S5_N // S5_BLOCKS


def _s5_tables(ar, ai, reverse):
    if reverse:
        ai = -ai
    pows = [(ar, ai)]
    for _ in range(SUBLANES - 1):
        pr, pi = pows[-1]
        pows.append((pr * ar - pi * ai, pr * ai + pi * ar))
    row = lax.broadcasted_iota(jnp.int32, (SUBLANES, ar.shape[1]), 0)
    out = []
    for k in (1, 2, 4):
        pr, pi = pows[k - 1]
        mask = (row < SUBLANES - k) if reverse else (row >= k)
        out += [jnp.where(mask, pr, 0.0), jnp.where(mask, pi, 0.0)]
    order = list(range(SUBLANES))
    if reverse:
        order = order[::-1]
    out.append(jnp.concatenate([pows[e][0] for e in order], axis=0))
    out.append(jnp.concatenate([pows[e][1] for e in order], axis=0))
    return out


def _s5_scan(s_scr, tab, carry, tm, reverse):
    ngrp = tm // SUBLANES
    for lc in range(S5_BLOCKS):
        lr = slice(lc * S5_BS, (lc + 1) * S5_BS)
        li = slice(S5_N + lc * S5_BS, S5_N + (lc + 1) * S5_BS)
        tb = [tab[i, :, lr] for i in range(8)]

        def step(i, c, lr=lr, li=li, tb=tb):
            grp = (ngrp - 1 - i) if reverse else i
            r0 = pl.multiple_of(grp * SUBLANES, SUBLANES)
            xr = s_scr[pl.ds(r0, SUBLANES), lr]
            xi = s_scr[pl.ds(r0, SUBLANES), li]
            for lvl, k in enumerate((1, 2, 4)):
                sh = (SUBLANES - k) if reverse else k
                pr, pi = tb[2 * lvl], tb[2 * lvl + 1]
                rr = pltpu.roll(xr, sh, 0)
                ri = pltpu.roll(xi, sh, 0)
                xr, xi = xr + pr * rr - pi * ri, xi + pr * ri + pi * rr
            cr, ci = c
            xr, xi = xr + tb[6] * cr - tb[7] * ci, xi + tb[6] * ci + tb[7] * cr
            s_scr[pl.ds(r0, SUBLANES), lr] = xr
            s_scr[pl.ds(r0, SUBLANES), li] = xi
            e = 0 if reverse else SUBLANES - 1
            return xr[e:e + 1, :], xi[e:e + 1, :]

        cf = lax.fori_loop(0, ngrp, step, (carry[0:1, lr], carry[0:1, li]))
        carry[0:1, lr] = cf[0]
        carry[0:1, li] = cf[1]


def s5_fwd(u5, bbc, ccc, ab8, d_row, wglu, bglu_row, *, tm, name):
    t, w = u5.shape

    def body(u_ref, bb_ref, cc_ref, ab_ref, d_ref, wg_ref, bg_ref, s_ref, ys_ref, yb_ref, s_scr, tab, carry):
        i = pl.program_id(0)

        @pl.when(i == 0)
        def _():
            carry[...] = jnp.zeros_like(carry)
            for k, tv in enumerate(_s5_tables(ab_ref[0:1, :], ab_ref[1:2, :], False)):
                tab[k] = tv

        u = u_ref[...]
        ub = u.astype(BF16)
        for j in range(S5_BLOCKS):
            uj = ub[:, j * S5_BC:(j + 1) * S5_BC]
            bj = bb_ref[j * S5_BC:(j + 1) * S5_BC, :]
            s_scr[:, j * S5_BS:(j + 1) * S5_BS] = _dot(uj, bj[:, :S5_BS])
            s_scr[:, S5_N + j * S5_BS:S5_N + (j + 1) * S5_BS] = _dot(uj, bj[:, S5_BS:])
        _s5_scan(s_scr, tab, carry, tm, False)
        s = s_scr[...]
        s_ref[...] = s
        sb = s.astype(BF16)
        ys = []
        for j in range(S5_BLOCKS):
            cj = cc_ref[:, j * S5_BC:(j + 1) * S5_BC]
            ys.append(_dot(sb[:, j * S5_BS:(j + 1) * S5_BS], cj[:S5_BS, :])
                      + _dot(sb[:, S5_N + j * S5_BS:S5_N + (j + 1) * S5_BS], cj[S5_BS:, :]))
        y = jnp.concatenate(ys, axis=1) + d_ref[...] * u
        ys_ref[...] = y
        yb1 = _gelu(y)
        tt = _dot(yb1.astype(BF16), wg_ref[...]) + bg_ref[...]
        yb_ref[...] = (yb1 * _sigmoid(tt)).astype(BF16)

    tile = lambda c: pl.BlockSpec((tm, c), lambda i: (i, 0))
    return pl.pallas_call(
        body, name=name, grid=(t // tm,),
        in_specs=[tile(w), _full((w, 2 * S5_BS)), _full((2 * S5_BS, w)), _full((SUBLANES, S5_N)), _full((1, w)),
                  _full((w, w)), _full((1, w))],
        out_specs=[tile(2 * S5_N), tile(w), tile(w)],
        out_shape=[jax.ShapeDtypeStruct((t, 2 * S5_N), F32), jax.ShapeDtypeStruct((t, w), F32),
                   jax.ShapeDtypeStruct((t, w), BF16)],
        scratch_shapes=[pltpu.VMEM((tm, 2 * S5_N), F32), pltpu.VMEM((8, SUBLANES, S5_N), F32),
                        pltpu.VMEM((SUBLANES, 2 * S5_N), F32)],
        compiler_params=_params("arbitrary"),
    )(u5, bbc, ccc, ab8, d_row, wglu, bglu_row)


def s5_bwd(dyb, ys5, u5, s, bbc, ccc, ab8, d_row, wglu, bglu_row, *, tm, name, comm=None):
    t, w = u5.shape
    nt = t // tm
    per = tm // SUBLANES

    def body(dyb_ref, ys_ref, u_ref, s_ref, sprev_ref, bb_ref, cc_ref, ab_ref, d_ref, wg_ref, bg_ref,
             du_ref, dbb_ref, dcc_ref, dab_ref, dd_ref, dwg_ref, dbg_ref, g_scr, tab, carry):
        i = pl.program_id(0)

        @pl.when(i == 0)
        def _():
            carry[...] = jnp.zeros_like(carry)
            for k, tv in enumerate(_s5_tables(ab_ref[0:1, :], ab_ref[1:2, :], True)):
                tab[k] = tv
            for r in (dbb_ref, dcc_ref, dab_ref, dd_ref, dwg_ref, dbg_ref):
                r[...] = jnp.zeros_like(r)

        y = ys_ref[...]
        u = u_ref[...]
        yb1 = _gelu(y)
        yb1b = yb1.astype(BF16)
        sg = _sigmoid(_dot(yb1b, wg_ref[...]) + bg_ref[...])
        dybv = dyb_ref[...]
        dtt = dybv * yb1 * sg * (1.0 - sg)
        dttb = dtt.astype(BF16)
        dyb1 = dybv * sg + _dot_nt(dttb, wg_ref[...])
        dwg_ref[...] += _dot_tn(yb1b, dttb)
        dbg_ref[...] += jnp.sum(dtt, axis=0, keepdims=True)
        dy = dyb1 * _gelu_grad(y)
        dd_ref[...] += jnp.sum(dy * u, axis=0, keepdims=True)
        dyh = dy.astype(BF16)
        for j in range(S5_BLOCKS):
            cj = cc_ref[:, j * S5_BC:(j + 1) * S5_BC]
            dyj = dyh[:, j * S5_BC:(j + 1) * S5_BC]
            g_scr[:, j * S5_BS:(j + 1) * S5_BS] = _dot_nt(dyj, cj[:S5_BS, :])
            g_scr[:, S5_N + j * S5_BS:S5_N + (j + 1) * S5_BS] = _dot_nt(dyj, cj[S5_BS:, :])
        _s5_scan(g_scr, tab, carry, tm, True)
        gs = g_scr[...]
        gb = gs.astype(BF16)
        sv = s_ref[...]
        sb = sv.astype(BF16)
        ub = u.astype(BF16)
        dus = []
        for j in range(S5_BLOCKS):
            rs = slice(j * S5_BS, (j + 1) * S5_BS)
            ims = slice(S5_N + j * S5_BS, S5_N + (j + 1) * S5_BS)
            cs = slice(j * S5_BC, (j + 1) * S5_BC)
            bj = bb_ref[cs, :]
            dus.append(_dot_nt(gb[:, rs], bj[:, :S5_BS]) + _dot_nt(gb[:, ims], bj[:, S5_BS:]))
            dbb_ref[cs, :S5_BS] += _dot_tn(ub[:, cs], gb[:, rs])
            dbb_ref[cs, S5_BS:] += _dot_tn(ub[:, cs], gb[:, ims])
            dcc_ref[:S5_BS, cs] += _dot_tn(sb[:, rs], dyh[:, cs])
            dcc_ref[S5_BS:, cs] += _dot_tn(sb[:, ims], dyh[:, cs])
        du_ref[...] = (jnp.concatenate(dus, axis=1) + d_ref[...] * dy).astype(BF16)
        first = lax.broadcasted_iota(jnp.int32, (tm, 1), 0) == 0
        before = jnp.where(i == nt - 1, 0.0, sprev_ref[SUBLANES - 1:SUBLANES, :])
        sp = jnp.where(first, before, pltpu.roll(sv, 1, 0))
        g_re, g_im = gs[:, :S5_N], gs[:, S5_N:]
        sp_re, sp_im = sp[:, :S5_N], sp[:, S5_N:]
        dab_ref[0:1, :] += jnp.sum(g_re * sp_re + g_im * sp_im, axis=0, keepdims=True)
        dab_ref[1:2, :] += jnp.sum(g_im * sp_re - g_re * sp_im, axis=0, keepdims=True)

    tile = lambda c: pl.BlockSpec((tm, c), lambda i: (nt - 1 - i, 0))
    acc = lambda r, c: jax.ShapeDtypeStruct((r, c), F32)
    return _tiled_call(
        body, name=name, grid=(nt,),
        in_specs=[tile(w), tile(w), tile(w), tile(2 * S5_N),
                  pl.BlockSpec((SUBLANES, 2 * S5_N), lambda i: (jnp.maximum((nt - 1 - i) * per - 1, 0), 0)),
                  _full((w, 2 * S5_BS)), _full((2 * S5_BS, w)), _full((SUBLANES, S5_N)), _full((1, w)),
                  _full((w, w)), _full((1, w))],
        out_specs=[tile(w), _full((w, 2 * S5_BS)), _full((2 * S5_BS, w)), _full((SUBLANES, S5_N)), _full((1, w)),
                   _full((w, w)), _full((1, w))],
        out_shape=[jax.ShapeDtypeStruct((t, w), BF16), acc(w, 2 * S5_BS), acc(2 * S5_BS, w), acc(SUBLANES, S5_N),
                   acc(1, w), acc(w, w), acc(1, w)],
        scratch_shapes=[pltpu.VMEM((tm, 2 * S5_N), F32), pltpu.VMEM((8, SUBLANES, S5_N), F32),
                        pltpu.VMEM((SUBLANES, 2 * S5_N), F32)],
        operands=(dyb, ys5, u5, s, s, bbc, ccc, ab8, d_row, wglu, bglu_row), comm=comm)


def s5_discretize(a_re, a_im, log_dt, b_re, b_im, *, name):
    n = a_re.shape[0]

    def body(ar_ref, ai_ref, dt_ref, br_ref, bi_ref, ab_ref, bbr_ref, bbi_ref):
        ar, ai, dtv = ar_ref[...], ai_ref[...], jnp.exp(dt_ref[...])
        mag = jnp.exp(ar * dtv)
        abr = mag * jnp.cos(ai * dtv)
        abi = mag * jnp.sin(ai * dtv)
        den = ar * ar + ai * ai
        nr = abr - 1.0
        cr = (nr * ar + abi * ai) / den
        ci = (abi * ar - nr * ai) / den
        ab_ref[:, 0:1] = abr
        ab_ref[:, 1:2] = abi
        br, bi = br_ref[...], bi_ref[...]
        bbr_ref[...] = cr * br - ci * bi
        bbi_ref[...] = cr * bi + ci * br

    col = jax.ShapeDtypeStruct((n, 2), F32)
    mat = jax.ShapeDtypeStruct((n, S5_GROUP), F32)
    return pl.pallas_call(body, name=name, out_shape=[col, mat, mat])(a_re, a_im, log_dt, b_re, b_im)


def s5_discretize_bwd(a_re, a_im, log_dt, b_re, b_im, dab, dbb_re, dbb_im, *, name):
    n = a_re.shape[0]

    def body(ar_ref, ai_ref, dt_ref, br_ref, bi_ref, dab_ref, dbr_ref, dbi_ref, da_ref, ddt_ref, gbr_ref, gbi_ref):
        ar, ai, dtv = ar_ref[...], ai_ref[...], jnp.exp(dt_ref[...])
        mag = jnp.exp(ar * dtv)
        abr = mag * jnp.cos(ai * dtv)
        abi = mag * jnp.sin(ai * dtv)
        den = ar * ar + ai * ai
        nr = abr - 1.0
        cr = (nr * ar + abi * ai) / den
        ci = (abi * ar - nr * ai) / den
        br, bi = br_ref[...], bi_ref[...]
        dbr, dbi = dbr_ref[...], dbi_ref[...]
        gbr_ref[...] = cr * dbr + ci * dbi
        gbi_ref[...] = cr * dbi - ci * dbr
        gcr = jnp.sum(dbr * br + dbi * bi, axis=1, keepdims=True)
        gci = jnp.sum(dbi * br - dbr * bi, axis=1, keepdims=True)
        ilr, ili = ar / den, -ai / den
        gabr = dab_ref[:, 0:1] + (ilr * gcr + ili * gci)
        gabi = dab_ref[:, 1:2] + (ilr * gci - ili * gcr)
        t1r, t1i = dtv * abr, dtv * abi
        t2r, t2i = -(cr * ilr - ci * ili), -(cr * ili + ci * ilr)
        da_ref[:, 0:1] = (t1r * gabr + t1i * gabi) + (t2r * gcr + t2i * gci)
        da_ref[:, 1:2] = (t1r * gabi - t1i * gabr) + (t2r * gci - t2i * gcr)
        lr, li = ar * abr - ai * abi, ar * abi + ai * abr
        dlog = (lr * gabr + li * gabi) * dtv
        ddt_ref[...] = jnp.sum(dlog.reshape(S5_GROUPS, S5_STATE, 1), axis=1)

    col2 = jax.ShapeDtypeStruct((n, 2), F32)
    col1 = jax.ShapeDtypeStruct((S5_GROUPS, 1), F32)
    mat = jax.ShapeDtypeStruct((n, S5_GROUP), F32)
    return pl.pallas_call(body, name=name, out_shape=[col2, col1, mat, mat])(
        a_re, a_im, log_dt, b_re, b_im, dab, dbb_re, dbb_im)


def merge_fwd(ya, yb, graw, x, wba, wbb, wout, *, tm, name):
    t, d = x.shape

    def body(ya_ref, yb_ref, g_ref, x_ref, wba_ref, wbb_ref, wo_ref, x1_ref, pa_ref, pb_ref):
        pa = _dot(ya_ref[...], wba_ref[...])
        pb = _dot(yb_ref[...], wbb_ref[...])
        gate = _sigmoid(g_ref[...])
        merged = gate[:, :d] * pa + gate[:, d:] * pb
        x1_ref[...] = x_ref[...] + _dot(merged.astype(BF16), wo_ref[...])
        pa_ref[...] = pa
        pb_ref[...] = pb

    tile = lambda c: pl.BlockSpec((tm, c), lambda i: (i, 0))
    out = jax.ShapeDtypeStruct((t, d), F32)
    return pl.pallas_call(
        body, name=name, grid=(t // tm,),
        in_specs=[tile(d), tile(S5_WIDTH), tile(2 * d), tile(d), _full((d, d)), _full((S5_WIDTH, d)), _full((d, d))],
        out_specs=[tile(d), tile(d), tile(d)], out_shape=[out, out, out],
        compiler_params=_params("parallel"),
    )(ya, yb, graw, x, wba, wbb, wout)


def merge_bwd(dx1b, graw, pa, pb, wba, wbb, wout, *, tm, name):
    t, d = pa.shape

    def body(dx_ref, g_ref, pa_ref, pb_ref, wba_ref, wbb_ref, wo_ref,
             mg_ref, dg_ref, dpa_ref, dpb_ref, dya_ref, dyb_ref):
        dm = _dot_nt(dx_ref[...], wo_ref[...])
        gate = _sigmoid(g_ref[...])
        g0, g1 = gate[:, :d], gate[:, d:]
        pa, pb = pa_ref[...], pb_ref[...]
        mg_ref[...] = (g0 * pa + g1 * pb).astype(BF16)
        dg_ref[:, :d] = (dm * pa * g0 * (1.0 - g0)).astype(BF16)
        dg_ref[:, d:] = (dm * pb * g1 * (1.0 - g1)).astype(BF16)
        dpa = (dm * g0).astype(BF16)
        dpb = (dm * g1).astype(BF16)
        dpa_ref[...] = dpa
        dpb_ref[...] = dpb
        dya_ref[...] = _dot_nt(dpa, wba_ref[...])
        dyb_ref[...] = _dot_nt(dpb, wbb_ref[...])

    tile = lambda c: pl.BlockSpec((tm, c), lambda i: (i, 0))
    sds = jax.ShapeDtypeStruct
    return pl.pallas_call(
        body, name=name, grid=(t // tm,),
        in_specs=[tile(d), tile(2 * d), tile(d), tile(d), _full((d, d)), _full((S5_WIDTH, d)), _full((d, d))],
        out_specs=[tile(d), tile(2 * d), tile(d), tile(d), tile(d), tile(S5_WIDTH)],
        out_shape=[sds((t, d), BF16), sds((t, 2 * d), BF16), sds((t, d), BF16), sds((t, d), BF16),
                   sds((t, d), F32), sds((t, S5_WIDTH), F32)],
        compiler_params=_params("parallel"),
    )(dx1b, graw, pa, pb, wba, wbb, wout)


def mlp_out_loss(a1, x1, w2, gf, target, *, tm, name):
    t, d = x1.shape
    f = a1.shape[1]

    def body(a_ref, x_ref, w_ref, g_ref, tg_ref, act_ref, dx_ref, dxb_ref, loss_ref, dg_ref):
        i = pl.program_id(0)
        av = jnp.maximum(a_ref[...], 0.0)
        act = (av * av).astype(BF16)
        act_ref[...] = act
        x2 = x_ref[...] + _dot(act, w_ref[...])
        r = lax.rsqrt(jnp.mean(x2 * x2, axis=-1, keepdims=True) + EPS)
        xn = x2 * r
        err = xn * g_ref[...] - tg_ref[...]
        dyf = err * (1.0 / d)
        dxn = dyf * g_ref[...]
        dx = r * (dxn - xn * jnp.mean(dxn * xn, axis=-1, keepdims=True))
        dx_ref[...] = dx
        dxb_ref[...] = dx.astype(BF16)

        @pl.when(i == 0)
        def _():
            loss_ref[...] = jnp.zeros_like(loss_ref)
            dg_ref[...] = jnp.zeros_like(dg_ref)

        loss_ref[...] += jnp.sum(err * err) * (0.5 / d)
        dg_ref[...] += jnp.sum(dyf * xn, axis=0, keepdims=True)

    tile = lambda c: pl.BlockSpec((tm, c), lambda i: (i, 0))
    sds = jax.ShapeDtypeStruct
    return pl.pallas_call(
        body, name=name, grid=(t // tm,),
        in_specs=[tile(f), tile(d), _full((f, d)), _full((1, d)), tile(d)],
        out_specs=[tile(f), tile(d), tile(d), _full((1, LANES)), _full((1, d))],
        out_shape=[sds((t, f), BF16), sds((t, d), F32), sds((t, d), BF16), sds((1, LANES), F32), sds((1, d), F32)],
        compiler_params=_params("arbitrary"),
    )(a1, x1, w2, gf.reshape(1, d), target)


def mlp_bwd_act(dx2b, a1, w2, *, tm, name):
    t, f = a1.shape
    d = dx2b.shape[1]

    def body(dx_ref, a_ref, w_ref, o_ref):
        dact = _dot_nt(dx_ref[...], w_ref[...])
        o_ref[...] = (dact * 2.0 * jnp.maximum(a_ref[...], 0.0)).astype(BF16)

    tile = lambda c: pl.BlockSpec((tm, c), lambda i: (i, 0))
    return pl.pallas_call(
        body, name=name, grid=(t // tm,),
        in_specs=[tile(d), tile(f), _full((f, d))], out_specs=tile(f),
        out_shape=jax.ShapeDtypeStruct((t, f), BF16),
        compiler_params=_params("parallel"),
    )(dx2b, a1, w2)


ADAM_TILE_ELEMS = 128 * 1024


def _row_tile(rows, cap):
    if rows <= cap:
        return rows
    best = None
    for c in range(16, cap + 1, 16):
        if rows % c == 0:
            best = c
    assert best is not None, rows
    return best


def adamw(w, m, v, gparts, *, name):
    rows, cols = w.shape
    tr = _row_tile(rows, max(16, ADAM_TILE_ELEMS // cols))

    def body(w_ref, m_ref, v_ref, g_ref, go_ref, d_ref, mo_ref, vo_ref):
        g = g_ref[0].astype(F32)
        for k in range(1, N_DEV):
            g = g + g_ref[k].astype(F32)
        mn = ADAM_B1 * m_ref[...] + (1.0 - ADAM_B1) * g
        vn = ADAM_B2 * v_ref[...] + (1.0 - ADAM_B2) * (g * g)
        m_hat = mn / (1.0 - ADAM_B1 ** ADAM_STEP)
        v_hat = vn / (1.0 - ADAM_B2 ** ADAM_STEP)
        go_ref[...] = g
        d_ref[...] = -ADAM_LR * (m_hat / (jnp.sqrt(v_hat) + ADAM_EPS) + ADAM_WD * w_ref[...])
        mo_ref[...] = mn
        vo_ref[...] = vn

    tile = pl.BlockSpec((tr, cols), lambda i: (i, 0))
    out = jax.ShapeDtypeStruct((rows, cols), F32)
    return pl.pallas_call(
        body, name=name, grid=(rows // tr,),
        in_specs=[tile, tile, tile, pl.BlockSpec((N_DEV, tr, cols), lambda i: (0, i, 0))],
        out_specs=[tile, tile, tile, tile], out_shape=[out, out, out, out],
        compiler_params=_params("parallel"),
    )(w, m, v, gparts)


def all_gather_arrays(blocks, *, name):
    na = len(blocks)

    def body(*refs):
        x_refs, out_refs = refs[:na], refs[na:2 * na]
        send_sems, recv_sems, local_sems = refs[2 * na:]
        x, y, c, _ = _place()
        me, sibling = (x, y, c), (x, y, 1 - c)
        chips = [(1 - x, y), (x, 1 - y), (1 - x, 1 - y)]

        def copy(a, k, blk, to, own=False):
            px, py, pc = blk
            dst = out_refs[a].at[4 * px + 2 * py + pc]
            return pltpu.make_async_remote_copy(
                src_ref=x_refs[a] if own else dst, dst_ref=dst,
                send_sem=send_sems.at[7 * a + k], recv_sem=recv_sems.at[7 * a + k], device_id=to, device_id_type=_MESH)

        mine = [pltpu.make_async_copy(x_refs[a], out_refs[a].at[4 * x + 2 * y + c], local_sems.at[a]) for a in range(na)]
        for cp in mine:
            cp.start()
        sent = []
        for a in range(na):
            first = [copy(a, 0, me, sibling, own=True)]
            first += [copy(a, 1 + j, me, (*chip, c), own=True) for j, chip in enumerate(chips)]
            for cp in first:
                cp.start()
            sent += first
        for a in range(na):
            for j, chip in enumerate(chips):
                copy(a, 1 + j, (*chip, c), me).wait_recv()
                fwd = copy(a, 4 + j, (*chip, c), sibling)
                fwd.start()
                sent.append(fwd)
        for a in range(na):
            copy(a, 0, sibling, me).wait_recv()
            for j, chip in enumerate(chips):
                copy(a, 4 + j, (*chip, 1 - c), me).wait_recv()
        for cp in sent:
            cp.wait_send()
        for cp in mine:
            cp.wait()

    any_spec = pl.BlockSpec(memory_space=pl.ANY)
    return pl.pallas_call(
        body, name=name, in_specs=[any_spec] * na, out_specs=[any_spec] * na,
        out_shape=[jax.ShapeDtypeStruct((N_DEV,) + b.shape, b.dtype) for b in blocks],
        scratch_shapes=[pltpu.SemaphoreType.DMA((7 * na,)), pltpu.SemaphoreType.DMA((7 * na,)),
                        pltpu.SemaphoreType.DMA((na,))],
        compiler_params=pltpu.CompilerParams(has_side_effects=True),
    )(*blocks)


def exchange_partials(parts, *, name):
    na = len(parts)

    def body(*refs):
        local, remote = _comm_copies("exchange", refs[:na], refs[na:2 * na], *refs[2 * na:])
        for cp in local + remote:
            cp.start()
        for cp in remote:
            cp.wait_recv()
        for cp in remote:
            cp.wait_send()
        for cp in local:
            cp.wait()

    any_spec = pl.BlockSpec(memory_space=pl.ANY)
    return pl.pallas_call(
        body, name=name, in_specs=[any_spec] * na, out_specs=[any_spec] * na,
        out_shape=[jax.ShapeDtypeStruct(p.shape, p.dtype) for p in parts],
        scratch_shapes=[pltpu.SemaphoreType.DMA((na * N_REL,)), pltpu.SemaphoreType.DMA((na * N_REL,)),
                        pltpu.SemaphoreType.DMA((na,))],
        compiler_params=pltpu.CompilerParams(has_side_effects=True),
    )(*parts)


def _lane_row(v):
    return jnp.pad(v.astype(F32), (0, LANES - v.shape[0])).reshape(1, LANES)


def _block_diag_b(bb):
    eye = jnp.eye(SUBLANES, dtype=F32)
    bt = bb.reshape(S5_BLOCKS, 8, S5_STATE, S5_GROUP).transpose(0, 1, 3, 2)
    return (bt[:, :, :, None, :] * eye[None, :, None, :, None]).reshape(S5_WIDTH, S5_BS)


def _block_diag_c(cc):
    eye = jnp.eye(SUBLANES, dtype=F32)
    ct = cc.reshape(S5_BLOCKS, 8, S5_GROUP, S5_STATE).transpose(3, 0, 1, 2)
    return (ct[None, :, :, :, :] * eye[:, None, None, :, None]).reshape(S5_BS, S5_WIDTH)


def _diag_blocks_b(m):
    eye = jnp.eye(SUBLANES, dtype=F32)
    m5 = m.reshape(S5_BLOCKS, 8, S5_GROUP, 8, S5_STATE)
    return jnp.sum(m5 * eye[None, :, None, :, None], axis=3).transpose(0, 1, 3, 2).reshape(S5_GROUPS, S5_STATE, S5_GROUP)


def _diag_blocks_c(m):
    eye = jnp.eye(SUBLANES, dtype=F32)
    m5 = m.reshape(8, S5_STATE, S5_BLOCKS, 8, S5_GROUP)
    return jnp.sum(m5 * eye[:, None, None, :, None], axis=0).transpose(1, 2, 3, 0).reshape(S5_GROUPS, S5_GROUP, S5_STATE)


def train_step(x, target, p, m, v):
    t = x.shape[0]
    tm = min(256, t)
    bf = lambda n: p[n].astype(BF16)
    slots = lambda n, a: a.reshape((N_DEV,) + _shard_shape(n))
    updated = {}

    def update(names, got):
        for n, parts in zip(names, got):
            updated[n] = adamw(p[n], m[n], v[n], parts, name=f"adamw_{n}")

    st_in, st_conv = all_gather_arrays([bf("w_in"), p["conv_w"]], name="gather_first")
    wi = _join_shards("w_in", st_in)
    win_p = jnp.concatenate([wi[:, 1024:3072], wi[:, 3600:5648], wi[:, 0:1024], wi[:, 3088:3600], wi[:, 3072:3088],
                             jnp.zeros((D_MODEL, LANES - SSD_HEADS), BF16)], axis=1)
    conv_w8 = jnp.pad(_join_shards("conv_w", st_conv), ((0, SUBLANES - SSD_CONV), (0, 0)))
    (h, xbc_raw, graw, z, u5, dtr), (st_branch, st_out, st_glu) = rms_matmul(
        x, p["norm_mix_g"], win_p, IN_SPLITS, tm=tm, name="in_proj",
        comm=("gather", [bf("w_branch"), bf("w_out"), bf("s5_glu_w")]))
    w_branch = st_branch.reshape(D_MODEL + S5_WIDTH, D_MODEL)
    wba, wbb = w_branch[:D_MODEL], w_branch[D_MODEL:]
    wout, wglu = st_out.reshape(D_MODEL, D_MODEL), st_glu.reshape(S5_WIDTH, S5_WIDTH)
    xbc = conv_silu_fwd(xbc_raw, conv_w8, p["conv_b"], tm=tm, name="conv_fwd")
    bias_row, alog_row = _lane_row(p["dt_bias"]), _lane_row(p["a_log"])
    d_row = jnp.repeat(p["d_ssd"], SSD_HEADDIM).reshape(1, SSD_HEADS * SSD_HEADDIM)
    (ya, yssd, states), (w1, st_w2) = ssd_fwd(xbc, dtr, z, bias_row, alog_row, d_row, p["ssd_norm_g"], name="ssd_fwd",
                                              comm=("gather", [bf("w_mlp_in"), bf("w_mlp_out")]))
    w2 = st_w2.reshape(D_FF, D_MODEL)

    n = S5_N
    a_re_c, a_im_c = p["s5_a_re"].reshape(n, 1), p["s5_a_im"].reshape(n, 1)
    dt_c = jnp.repeat(p["s5_log_dt"], S5_STATE).reshape(n, 1)
    b_re_m, b_im_m = p["s5_b_re"].reshape(n, S5_GROUP), p["s5_b_im"].reshape(n, S5_GROUP)
    ab, bb_re, bb_im = s5_discretize(a_re_c, a_im_c, dt_c, b_re_m, b_im_m, name="s5_disc")
    ab8 = jnp.pad(ab.T, ((0, SUBLANES - 2), (0, 0)))
    bbc = jnp.concatenate([_block_diag_b(bb_re.reshape(S5_GROUPS, S5_STATE, S5_GROUP)),
                           _block_diag_b(bb_im.reshape(S5_GROUPS, S5_STATE, S5_GROUP))], axis=1).astype(BF16)
    ccc = jnp.concatenate([_block_diag_c(p["s5_c_re"]), -_block_diag_c(p["s5_c_im"])], axis=0).astype(BF16)
    s5d_row = p["s5_d"].reshape(1, S5_WIDTH)
    bglu_row = p["s5_glu_b"].reshape(1, S5_WIDTH)
    s, ys5, yb = s5_fwd(u5, bbc, ccc, ab8, s5d_row, wglu, bglu_row, tm=tm, name="s5_fwd")

    x1, pa, pb = merge_fwd(ya, yb, graw, x, wba, wbb, wout, tm=tm, name="merge_fwd")
    (h2, a1), _ = rms_matmul(x1, p["norm_mlp_g"], w1, (D_FF,), tm=tm, name="mlp_in")
    act, dx2, dx2b, loss_row, dgf = mlp_out_loss(a1, x1, w2, p["norm_final_g"], target, tm=tm, name="mlp_out_loss")

    g = {}
    g["norm_final_g"] = dgf[0]
    da1 = mlp_bwd_act(dx2b, a1, w2, tm=tm, name="mlp_bwd_act")
    dw2 = slots("w_mlp_out", matmul_tn(act, dx2b, out_dtype=BF16, name="dw_mlp_out"))
    dw1 = matmul_tn(h2, da1, out_dtype=BF16, col_shards=N_DEV, name="dw_mlp_in")
    (dx1, dx1b, dgm), _ = matmul_nt_rms_bwd([da1], w1, x1, p["norm_mlp_g"], dx2, tm=tm, name="mlp_in_bwd")
    g["norm_mlp_g"] = dgm[0]
    merged, dgraw, dpa, dpb, dya, dyb = merge_bwd(dx1b, graw, pa, pb, wba, wbb, wout, tm=tm, name="merge_bwd")
    dwo = slots("w_out", matmul_tn(merged, dx1b, out_dtype=BF16, name="dw_out"))
    dwb = slots("w_branch", jnp.concatenate([matmul_tn(ya, dpa, out_dtype=BF16, name="dw_branch_a"),
                                             matmul_tn(yb, dpb, out_dtype=BF16, name="dw_branch_b")], axis=0))

    (du5, dbbc, dccc, dab, dd5, dwglu, dbglu), got = s5_bwd(
        dyb, ys5, u5, s, bbc, ccc, ab8, s5d_row, wglu, bglu_row, tm=min(128, t), name="s5_bwd",
        comm=("exchange", [dw2, dw1]))
    update(["w_mlp_out", "w_mlp_in"], got)
    g["s5_glu_b"], g["s5_d"] = dbglu[0], dd5[0]
    g["s5_c_re"] = _diag_blocks_c(dccc[:S5_BS])
    g["s5_c_im"] = -_diag_blocks_c(dccc[S5_BS:])
    dbb_re = _diag_blocks_b(dbbc[:, :S5_BS]).reshape(n, S5_GROUP)
    dbb_im = _diag_blocks_b(dbbc[:, S5_BS:]).reshape(n, S5_GROUP)
    da, dlogdt, gb_re, gb_im = s5_discretize_bwd(a_re_c, a_im_c, dt_c, b_re_m, b_im_m, dab[:2].T, dbb_re, dbb_im,
                                                  name="s5_disc_bwd")
    g["s5_a_re"] = da[:, 0].reshape(S5_GROUPS, S5_STATE)
    g["s5_a_im"] = da[:, 1].reshape(S5_GROUPS, S5_STATE)
    g["s5_log_dt"] = dlogdt[:, 0]
    g["s5_b_re"] = gb_re.reshape(S5_GROUPS, S5_STATE, S5_GROUP)
    g["s5_b_im"] = gb_im.reshape(S5_GROUPS, S5_STATE, S5_GROUP)

    (dz, dxbc, ddtr, dng, dd_row, dalog_row, dbias_row), got = ssd_bwd(
        dya, yssd, z, xbc, dtr, states, bias_row, alog_row, d_row, p["ssd_norm_g"], name="ssd_bwd",
        comm=("exchange", [dwo, dwb, slots("s5_glu_w", dwglu.astype(BF16))]))
    update(["w_out", "w_branch", "s5_glu_w"], got)
    g["ssd_norm_g"] = dng[0]
    g["d_ssd"], g["a_log"], g["dt_bias"] = dd_row[0, :SSD_HEADS], dalog_row[0, :SSD_HEADS], dbias_row[0, :SSD_HEADS]
    dxc, dconv_b, dconv_w8 = conv_silu_bwd_a(xbc_raw, dxbc, conv_w8, p["conv_b"], tm=tm, name="conv_bwd_a")
    g["conv_b"] = dconv_b[0]
    dconv = dconv_w8[:SSD_CONV].reshape(SSD_CONV, N_DEV, CONV_DIM // N_DEV).transpose(1, 0, 2)
    dxbc_raw = conv_bwd_b(dxc, conv_w8, tm=tm, name="conv_bwd_b")
    dproj = [dxbc_raw, dgraw, dz, du5, ddtr]
    dxbc_w, dgate_w, dz_w, du5_w, ddt_w = [matmul_tn(h, piece, out_dtype=BF16, name=f"dw_in_{k}")
                                           for k, piece in enumerate(dproj)]
    dw_in = jnp.concatenate([dz_w, dxbc_w, ddt_w[:, :SSD_HEADS], du5_w, dgate_w], axis=1)
    dw_in = dw_in.reshape(D_MODEL, N_DEV, D_IN_PROJ // N_DEV).transpose(1, 0, 2)
    (grad_x, _, dgx), got = matmul_nt_rms_bwd(dproj, win_p, x, p["norm_mix_g"], dx1, tm=tm, name="in_proj_bwd",
                                              comm=("exchange", [dw_in, dconv]))
    update(["w_in", "conv_w"], got)
    g["norm_mix_g"] = dgx[0]

    small_rows = _pack_small(g, loss_row[0, 0])
    got_small, = exchange_partials([jnp.broadcast_to(small_rows, (N_DEV,) + small_rows.shape)], name="exchange_small")
    zero = jnp.zeros((), F32)
    packed = adamw(_pack_small(p, zero), _pack_small(m, zero), _pack_small(v, zero), got_small, name="adamw_small")
    shapes = {n: p[n].shape for n in SMALL_ORDER}
    loss = None
    for j, rows in enumerate(packed):
        vals, extra = _unpack_small(rows, shapes)
        loss = extra if loss is None else loss
        for n in SMALL_ORDER:
            updated.setdefault(n, [None] * 4)[j] = vals[n]
    return loss, grad_x, updated


WEIGHT_ORDER = ["norm_mix_g", "w_in", "conv_w", "conv_b", "dt_bias", "a_log", "d_ssd", "ssd_norm_g", "s5_a_re",
                "s5_a_im", "s5_log_dt", "s5_b_re", "s5_b_im", "s5_c_re", "s5_c_im", "s5_d", "s5_glu_w", "s5_glu_b",
                "w_branch", "w_out", "norm_mlp_g", "w_mlp_in", "w_mlp_out", "norm_final_g"]
SHARDED = {"w_in": ((D_MODEL, D_IN_PROJ), 1), "conv_w": ((SSD_CONV, CONV_DIM), 1), "s5_glu_w": ((S5_WIDTH, S5_WIDTH), 0),
           "w_branch": ((D_MODEL + S5_WIDTH, D_MODEL), 0), "w_out": ((D_MODEL, D_MODEL), 0),
           "w_mlp_in": ((D_MODEL, D_FF), 1), "w_mlp_out": ((D_FF, D_MODEL), 0)}
SHARDED_ORDER = ["w_in", "conv_w", "s5_glu_w", "w_branch", "w_out", "w_mlp_in", "w_mlp_out"]
SMALL_ORDER = [n for n in WEIGHT_ORDER if n not in SHARDED]


def _shard_shape(name):
    (r, c), ax = SHARDED[name]
    return (r, c // N_DEV) if ax == 1 else (r // N_DEV, c)


def _join_shards(name, stacked):
    (r, c), ax = SHARDED[name]
    if ax == 1:
        return stacked.transpose(1, 0, 2).reshape(r, c)
    return stacked.reshape(r, c)


def _pack_small(values, loss):
    flat = jnp.concatenate([values[n].reshape(-1).astype(F32) for n in SMALL_ORDER] + [loss.reshape(1).astype(F32)])
    rows = -(-flat.shape[0] // (LANES * SUBLANES)) * SUBLANES
    return jnp.pad(flat, (0, rows * LANES - flat.shape[0])).reshape(rows, LANES)


def _unpack_small(rows, shapes):
    flat = rows.reshape(-1)
    out, off = {}, 0
    for n in SMALL_ORDER:
        size = math.prod(shapes[n])
        out[n] = flat[off:off + size].reshape(shapes[n])
        off += size
    return out, flat[off]


def kernel(x, norm_mix_g, w_in, conv_w, conv_b, dt_bias, a_log, d_ssd, ssd_norm_g, s5_a_re, s5_a_im, s5_log_dt, s5_b_re, s5_b_im, s5_c_re, s5_c_im, s5_d, s5_glu_w, s5_glu_b, w_branch, w_out, norm_mlp_g, w_mlp_in, w_mlp_out, norm_final_g, loss_target, m_norm_mix_g, m_w_in, m_conv_w, m_conv_b, m_dt_bias, m_a_log, m_d_ssd, m_ssd_norm_g, m_s5_a_re, m_s5_a_im, m_s5_log_dt, m_s5_b_re, m_s5_b_im, m_s5_c_re, m_s5_c_im, m_s5_d, m_s5_glu_w, m_s5_glu_b, m_w_branch, m_w_out, m_norm_mlp_g, m_w_mlp_in, m_w_mlp_out, m_norm_final_g, v_norm_mix_g, v_w_in, v_conv_w, v_conv_b, v_dt_bias, v_a_log, v_d_ssd, v_ssd_norm_g, v_s5_a_re, v_s5_a_im, v_s5_log_dt, v_s5_b_re, v_s5_b_im, v_s5_c_re, v_s5_c_im, v_s5_d, v_s5_glu_w, v_s5_glu_b, v_w_branch, v_w_out, v_norm_mlp_g, v_w_mlp_in, v_w_mlp_out, v_norm_final_g):
    w = dict(norm_mix_g=norm_mix_g, w_in=w_in, conv_w=conv_w, conv_b=conv_b, dt_bias=dt_bias, a_log=a_log, d_ssd=d_ssd,
             ssd_norm_g=ssd_norm_g, s5_a_re=s5_a_re, s5_a_im=s5_a_im, s5_log_dt=s5_log_dt, s5_b_re=s5_b_re,
             s5_b_im=s5_b_im, s5_c_re=s5_c_re, s5_c_im=s5_c_im, s5_d=s5_d, s5_glu_w=s5_glu_w, s5_glu_b=s5_glu_b,
             w_branch=w_branch, w_out=w_out, norm_mlp_g=norm_mlp_g, w_mlp_in=w_mlp_in, w_mlp_out=w_mlp_out,
             norm_final_g=norm_final_g)
    m = dict(norm_mix_g=m_norm_mix_g, w_in=m_w_in, conv_w=m_conv_w, conv_b=m_conv_b, dt_bias=m_dt_bias, a_log=m_a_log,
             d_ssd=m_d_ssd, ssd_norm_g=m_ssd_norm_g, s5_a_re=m_s5_a_re, s5_a_im=m_s5_a_im, s5_log_dt=m_s5_log_dt,
             s5_b_re=m_s5_b_re, s5_b_im=m_s5_b_im, s5_c_re=m_s5_c_re, s5_c_im=m_s5_c_im, s5_d=m_s5_d,
             s5_glu_w=m_s5_glu_w, s5_glu_b=m_s5_glu_b, w_branch=m_w_branch, w_out=m_w_out, norm_mlp_g=m_norm_mlp_g,
             w_mlp_in=m_w_mlp_in, w_mlp_out=m_w_mlp_out, norm_final_g=m_norm_final_g)
    v = dict(norm_mix_g=v_norm_mix_g, w_in=v_w_in, conv_w=v_conv_w, conv_b=v_conv_b, dt_bias=v_dt_bias, a_log=v_a_log,
             d_ssd=v_d_ssd, ssd_norm_g=v_ssd_norm_g, s5_a_re=v_s5_a_re, s5_a_im=v_s5_a_im, s5_log_dt=v_s5_log_dt,
             s5_b_re=v_s5_b_re, s5_b_im=v_s5_b_im, s5_c_re=v_s5_c_re, s5_c_im=v_s5_c_im, s5_d=v_s5_d,
             s5_glu_w=v_s5_glu_w, s5_glu_b=v_s5_glu_b, w_branch=v_w_branch, w_out=v_w_out, norm_mlp_g=v_norm_mlp_g,
             w_mlp_in=v_w_mlp_in, w_mlp_out=v_w_mlp_out, norm_final_g=v_norm_final_g)

    loss, grad_x, upd = train_step(x[0], loss_target[0], w, m, v)
    return (loss, grad_x.reshape(x.shape), *[upd[n][j] for j in range(4) for n in WEIGHT_ORDER])
```

```python
import functools
import math

import jax
import jax.numpy as jnp
from jax import lax
from jax.experimental import pallas as pl
from jax.experimental.pallas import tpu as pltpu

F32 = jnp.float32
BF16 = jnp.bfloat16
HI = lax.Precision.HIGHEST

N_DEV = 8
D_MODEL = 1024
SSD_HEADS = 16
SSD_HEADDIM = 64
SSD_GROUPS = 4
SSD_HPG = 4
SSD_STATE = 128
SSD_CHUNK = 128
SSD_CONV = 4
CONV_DIM = 2048
S5_WIDTH = 512
S5_GROUP = 16
S5_GROUPS = 32
S5_STATE = 64
S5_N = S5_GROUPS * S5_STATE
D_FF = 4096
D_IN_PROJ = 5648
IN_SPLITS = (2048, 2048, 1024, 512, 128)
D_IN_PAD = sum(IN_SPLITS)
EPS = 1e-6
LANES = 128
SUBLANES = 8
VMEM_LIMIT = 56 * 1024 * 1024

ADAM_LR = 0.001
ADAM_B1 = 0.9
ADAM_B2 = 0.999
ADAM_EPS = 1e-08
ADAM_WD = 0.01
ADAM_STEP = 10

GELU_C = math.sqrt(2.0 / math.pi)
GELU_K = 0.044715


def _params(*sem):
    return pltpu.CompilerParams(dimension_semantics=sem, vmem_limit_bytes=VMEM_LIMIT)


def _full(shape):
    nd = len(shape)
    return pl.BlockSpec(shape, lambda *_: (0,) * nd)


_MESH = pl.DeviceIdType.MESH
_RELATIONS = [(dx, dy, dc) for dx in (0, 1) for dy in (0, 1) for dc in (0, 1)][1:]
N_REL = len(_RELATIONS)


def _place():
    x, y, c = lax.axis_index("x"), lax.axis_index("y"), lax.axis_index("c")
    return x, y, c, 4 * x + 2 * y + c


def _comm_out_shapes(comm):
    kind, arrays = comm
    lead = (N_DEV,) if kind == "gather" else ()
    return [jax.ShapeDtypeStruct(lead + a.shape, a.dtype) for a in arrays]


def _comm_copies(kind, src_refs, dst_refs, send_sems, recv_sems, local_sems):
    x, y, c, idx = _place()
    local, remote = [], []
    for a, (src, dst) in enumerate(zip(src_refs, dst_refs)):
        local.append(pltpu.make_async_copy(src if kind == "gather" else src.at[idx], dst.at[idx], local_sems.at[a]))
        for k, (dx, dy, dc) in enumerate(_RELATIONS):
            px, py, pc = x ^ dx, y ^ dy, c ^ dc
            remote.append(pltpu.make_async_remote_copy(
                src_ref=src if kind == "gather" else src.at[4 * px + 2 * py + pc], dst_ref=dst.at[idx],
                send_sem=send_sems.at[N_REL * a + k], recv_sem=recv_sems.at[N_REL * a + k],
                device_id=(px, py, pc), device_id_type=_MESH))
    return local, remote


def _tiled_call(body, *, name, grid, in_specs, out_specs, out_shape, operands, scratch_shapes=(), comm=None):
    params = pltpu.CompilerParams(dimension_semantics=("arbitrary",), vmem_limit_bytes=VMEM_LIMIT,
                                  has_side_effects=comm is not None)
    if comm is None:
        outs = pl.pallas_call(body, name=name, grid=grid, in_specs=in_specs, out_specs=out_specs, out_shape=out_shape,
                              scratch_shapes=list(scratch_shapes), compiler_params=params)(*operands)
        return outs, []
    kind, arrays = comm
    na, n_in, n_out, n_scr = len(arrays), len(in_specs), len(out_specs), len(scratch_shapes)
    last = grid[0] - 1

    def hosted(*refs):
        ins, refs = refs[:n_in], refs[n_in:]
        cin, refs = refs[:na], refs[na:]
        outs, refs = refs[:n_out], refs[n_out:]
        cout, refs = refs[:na], refs[na:]
        scr, sems = refs[:n_scr], refs[n_scr:]
        i = pl.program_id(0)

        @pl.when(i == 0)
        def _():
            local, remote = _comm_copies(kind, cin, cout, *sems)
            for cp in local + remote:
                cp.start()

        body(*ins, *outs, *scr)

        @pl.when(i == last)
        def _():
            local, remote = _comm_copies(kind, cin, cout, *sems)
            for cp in remote:
                cp.wait_recv()
            for cp in remote:
                cp.wait_send()
            for cp in local:
                cp.wait()

    any_spec = pl.BlockSpec(memory_space=pl.ANY)
    res = pl.pallas_call(
        hosted, name=name, grid=grid,
        in_specs=list(in_specs) + [any_spec] * na, out_specs=list(out_specs) + [any_spec] * na,
        out_shape=list(out_shape) + _comm_out_shapes(comm),
        scratch_shapes=list(scratch_shapes) + [pltpu.SemaphoreType.DMA((N_REL * na,)), pltpu.SemaphoreType.DMA((N_REL * na,)),
                                               pltpu.SemaphoreType.DMA((na,))],
        compiler_params=params)(*operands, *arrays)
    return res[:n_out], res[n_out:]


def _dot(a, b):
    return jnp.dot(a, b, preferred_element_type=F32)


def _dot_nt(a, b):
    return lax.dot_general(a, b, (((1,), (1,)), ((), ())), preferred_element_type=F32)


def _dot_tn(a, b):
    return lax.dot_general(a, b, (((0,), (0,)), ((), ())), preferred_element_type=F32)


def _dot_hi(a, b):
    return jnp.dot(a, b, preferred_element_type=F32, precision=HI)


def _dot_nt_hi(a, b):
    return lax.dot_general(a, b, (((1,), (1,)), ((), ())), preferred_element_type=F32, precision=HI)


def _dot_tn_hi(a, b):
    return lax.dot_general(a, b, (((0,), (0,)), ((), ())), preferred_element_type=F32, precision=HI)


def _sigmoid(x):
    return 1.0 / (1.0 + jnp.exp(-x))


def _softplus(x):
    return jnp.maximum(x, 0.0) + jnp.log(1.0 + jnp.exp(-jnp.abs(x)))


def _gelu(x):
    return 0.5 * x * (1.0 + jnp.tanh(GELU_C * (x + GELU_K * x * x * x)))


def _gelu_grad(x):
    th = jnp.tanh(GELU_C * (x + GELU_K * x * x * x))
    return 0.5 * (1.0 + th) + 0.5 * x * (1.0 - th * th) * GELU_C * (1.0 + 3.0 * GELU_K * x * x)


def rms_matmul(x, g, w, splits, *, tm, name, comm=None):
    t, d = x.shape
    stacked = w.ndim == 3
    n = w.shape[0] * w.shape[2] if stacked else w.shape[1]
    assert sum(splits) == n and (len(splits) == 1 or not stacked)

    def body(x_ref, g_ref, w_ref, h_ref, *o_refs):
        xv = x_ref[...]
        r = lax.rsqrt(jnp.mean(xv * xv, axis=-1, keepdims=True) + EPS)
        h = (xv * r * g_ref[...]).astype(BF16)
        h_ref[...] = h
        if stacked:
            wk = w.shape[2]
            for k in range(w.shape[0]):
                o_refs[0][:, k * wk:(k + 1) * wk] = _dot(h, w_ref[k])
            return
        off = 0
        for o_ref, width in zip(o_refs, splits):
            o_ref[...] = _dot(h, w_ref[:, off:off + width])
            off += width

    tile = lambda c: pl.BlockSpec((tm, c), lambda i: (i, 0))
    return _tiled_call(
        body, name=name, grid=(t // tm,),
        in_specs=[tile(d), _full((1, d)), _full(w.shape)],
        out_specs=[tile(d)] + [tile(c) for c in splits],
        out_shape=[jax.ShapeDtypeStruct((t, d), BF16)] + [jax.ShapeDtypeStruct((t, c), F32) for c in splits],
        operands=(x, g.reshape(1, d), w), comm=comm)


def _pick(n, cap):
    best = LANES
    for c in range(LANES, cap + 1, LANES):
        if n % c == 0:
            best = c
    return best


def matmul_tn(a, b, *, name, out_dtype=F32, col_shards=None, tk=1024):
    t, m = a.shape
    n = b.shape[1]
    tk = min(tk, t)
    tm = _pick(m, 1024)
    tn = n // col_shards if col_shards else _pick(n, 1280)
    nk = t // tk

    def body(a_ref, b_ref, o_ref, acc):
        k = pl.program_id(2)

        @pl.when(k == 0)
        def _():
            acc[...] = jnp.zeros_like(acc)

        acc[...] += _dot_tn(a_ref[...], b_ref[...])

        @pl.when(k == nk - 1)
        def _():
            o_ref[...] = acc[...].reshape(o_ref.shape).astype(out_dtype)

    if col_shards:
        out_spec = pl.BlockSpec((1, tm, tn), lambda i, j, k: (j, i, 0))
        out_shape = jax.ShapeDtypeStruct((col_shards, m, tn), out_dtype)
    else:
        out_spec = pl.BlockSpec((tm, tn), lambda i, j, k: (i, j))
        out_shape = jax.ShapeDtypeStruct((m, n), out_dtype)
    return pl.pallas_call(
        body, name=name, grid=(m // tm, n // tn, nk),
        in_specs=[pl.BlockSpec((tk, tm), lambda i, j, k: (k, i)), pl.BlockSpec((tk, tn), lambda i, j, k: (k, j))],
        out_specs=out_spec, out_shape=out_shape,
        scratch_shapes=[pltpu.VMEM((tm, tn), F32)],
        compiler_params=_params("parallel", "parallel", "arbitrary"),
    )(a, b)


def matmul_nt_rms_bwd(dys, w, x, g, dres, *, tm, name, comm=None):
    t = x.shape[0]
    stacked = w.ndim == 3
    d = w.shape[1] if stacked else w.shape[0]
    widths = [dy.shape[1] for dy in dys]
    n_dy = len(dys)

    def body(*refs):
        dy_refs = refs[:n_dy]
        w_ref, x_ref, g_ref, dres_ref, dx_ref, dxb_ref, dg_ref = refs[n_dy:]
        i = pl.program_id(0)
        if stacked:
            wk = w.shape[2]
            dh = _dot_nt(dy_refs[0][:, 0:wk], w_ref[0])
            for k in range(1, w.shape[0]):
                dh = dh + _dot_nt(dy_refs[0][:, k * wk:(k + 1) * wk], w_ref[k])
        else:
            dh, off = None, 0
            for dy_ref, width in zip(dy_refs, widths):
                part = _dot_nt(dy_ref[...], w_ref[:, off:off + width])
                dh = part if dh is None else dh + part
                off += width
        xv = x_ref[...]
        r = lax.rsqrt(jnp.mean(xv * xv, axis=-1, keepdims=True) + EPS)
        xn = xv * r
        dxn = dh * g_ref[...]
        dx = dres_ref[...] + r * (dxn - xn * jnp.mean(dxn * xn, axis=-1, keepdims=True))
        dx_ref[...] = dx
        dxb_ref[...] = dx.astype(BF16)

        @pl.when(i == 0)
        def _():
            dg_ref[...] = jnp.zeros_like(dg_ref)

        dg_ref[...] += jnp.sum(dh * xn, axis=0, keepdims=True)

    tile = lambda c: pl.BlockSpec((tm, c), lambda i: (i, 0))
    return _tiled_call(
        body, name=name, grid=(t // tm,),
        in_specs=[tile(c) for c in widths] + [_full(w.shape), tile(d), _full((1, d)), tile(d)],
        out_specs=[tile(d), tile(d), _full((1, d))],
        out_shape=[jax.ShapeDtypeStruct((t, d), F32), jax.ShapeDtypeStruct((t, d), BF16),
                   jax.ShapeDtypeStruct((1, d), F32)],
        operands=(*dys, w, x, g.reshape(1, d), dres), comm=comm)


def _conv_taps(prev8, cur, w_ref, tm):
    ext = jnp.concatenate([prev8, cur], axis=0)
    acc = cur * w_ref[3:4, :]
    for k in range(3):
        off = SUBLANES - 3 + k
        acc = acc + ext[off:off + tm, :] * w_ref[k:k + 1, :]
    return acc


def conv_silu_fwd(u, w8, b, *, tm, name):
    t, c = u.shape
    per = tm // SUBLANES

    def body(prev_ref, cur_ref, w_ref, b_ref, o_ref):
        i = pl.program_id(0)
        prev8 = jnp.where(i == 0, 0.0, prev_ref[...])
        xc = _conv_taps(prev8, cur_ref[...], w_ref, tm) + b_ref[...]
        o_ref[...] = xc * _sigmoid(xc)

    return pl.pallas_call(
        body, name=name, grid=(t // tm,),
        in_specs=[pl.BlockSpec((SUBLANES, c), lambda i: (jnp.maximum(i * per - 1, 0), 0)),
                  pl.BlockSpec((tm, c), lambda i: (i, 0)), _full((SUBLANES, c)), _full((1, c))],
        out_specs=pl.BlockSpec((tm, c), lambda i: (i, 0)),
        out_shape=jax.ShapeDtypeStruct((t, c), F32),
        compiler_params=_params("parallel"),
    )(u, u, w8, b.reshape(1, c))


def conv_silu_bwd_a(u, dy, w8, b, *, tm, name):
    t, c = u.shape
    per = tm // SUBLANES

    def body(prev_ref, cur_ref, dy_ref, w_ref, b_ref, dxc_ref, db_ref, dw_ref):
        i = pl.program_id(0)
        prev8 = jnp.where(i == 0, 0.0, prev_ref[...])
        cur = cur_ref[...]
        xc = _conv_taps(prev8, cur, w_ref, tm) + b_ref[...]
        sg = _sigmoid(xc)
        dxc = dy_ref[...] * (sg * (1.0 + xc * (1.0 - sg)))
        dxc_ref[...] = dxc

        @pl.when(i == 0)
        def _():
            db_ref[...] = jnp.zeros_like(db_ref)
            dw_ref[...] = jnp.zeros_like(dw_ref)

        db_ref[...] += jnp.sum(dxc, axis=0, keepdims=True)
        ext = jnp.concatenate([prev8, cur], axis=0)
        rows = []
        for k in range(SSD_CONV):
            off = SUBLANES - 3 + k
            rows.append(jnp.sum(dxc * ext[off:off + tm, :], axis=0, keepdims=True))
        rows.append(jnp.zeros((SUBLANES - SSD_CONV, c), F32))
        dw_ref[...] += jnp.concatenate(rows, axis=0)

    return pl.pallas_call(
        body, name=name, grid=(t // tm,),
        in_specs=[pl.BlockSpec((SUBLANES, c), lambda i: (jnp.maximum(i * per - 1, 0), 0)),
                  pl.BlockSpec((tm, c), lambda i: (i, 0)), pl.BlockSpec((tm, c), lambda i: (i, 0)),
                  _full((SUBLANES, c)), _full((1, c))],
        out_specs=[pl.BlockSpec((tm, c), lambda i: (i, 0)), _full((1, c)), _full((SUBLANES, c))],
        out_shape=[jax.ShapeDtypeStruct((t, c), F32), jax.ShapeDtypeStruct((1, c), F32),
                   jax.ShapeDtypeStruct((SUBLANES, c), F32)],
        compiler_params=_params("arbitrary"),
    )(u, u, dy, w8, b.reshape(1, c))


def conv_bwd_b(dxc, w8, *, tm, name):
    t, c = dxc.shape
    per = tm // SUBLANES
    last = t // SUBLANES - 1
    nt = t // tm

    def body(cur_ref, next_ref, w_ref, o_ref):
        i = pl.program_id(0)
        nxt = jnp.where(i == nt - 1, 0.0, next_ref[...])
        cur = cur_ref[...]
        ext = jnp.concatenate([cur, nxt], axis=0)
        acc = cur * w_ref[3:4, :]
        for j in range(1, SSD_CONV):
            acc = acc + ext[j:j + tm, :] * w_ref[3 - j:4 - j, :]
        o_ref[...] = acc.astype(BF16)

    return pl.pallas_call(
        body, name=name, grid=(nt,),
        in_specs=[pl.BlockSpec((tm, c), lambda i: (i, 0)),
                  pl.BlockSpec((SUBLANES, c), lambda i: (jnp.minimum((i + 1) * per, last), 0)), _full((SUBLANES, c))],
        out_specs=pl.BlockSpec((tm, c), lambda i: (i, 0)),
        out_shape=jax.ShapeDtypeStruct((t, c), BF16),
        compiler_params=_params("parallel"),
    )(dxc, dxc, w8)


def _ssd_chunk_terms(dtr, bias_row, alog_row):
    q = SSD_CHUNK
    row = lax.broadcasted_iota(jnp.int32, (q, q), 0)
    col = lax.broadcasted_iota(jnp.int32, (q, q), 1)
    ltri = (col <= row).astype(F32)
    dt = _softplus(dtr + bias_row)
    a_row = -jnp.exp(alog_row)
    la = dt * a_row
    cum = _dot_hi(ltri, la)
    cum_t = _dot_tn_hi(la, (row <= col).astype(F32))
    return dt, a_row, cum, cum_t, ltri, row, col


def _head_maps():
    di = SSD_HEADS * SSD_HEADDIM
    expand = (lax.broadcasted_iota(jnp.int32, (LANES, di), 1) // SSD_HEADDIM
              == lax.broadcasted_iota(jnp.int32, (LANES, di), 0)).astype(F32)
    collapse = (lax.broadcasted_iota(jnp.int32, (di, LANES), 0) // SSD_HEADDIM
                == lax.broadcasted_iota(jnp.int32, (di, LANES), 1)).astype(F32)
    return expand, collapse


def _ssd_decay(cum, cum_t, h, causal):
    seg = cum[:, h:h + 1] - cum_t[h:h + 1, :]
    return jnp.where(causal, jnp.exp(jnp.where(causal, seg, 0.0)), 0.0)


def _pair_block_diag(x2):
    low = lax.broadcasted_iota(jnp.int32, x2.shape, 1) < SSD_HEADDIM
    zero = jnp.zeros_like(x2)
    return jnp.concatenate([jnp.where(low, x2, zero), jnp.where(low, zero, x2)], axis=0)


def ssd_fwd(xbc, dtr, z, bias_row, alog_row, d_row, norm_g, *, name, comm=None):
    t = xbc.shape[0]
    q, p, n = SSD_CHUNK, SSD_HEADDIM, SSD_STATE
    nc = t // q
    di = SSD_HEADS * p
    gw = di // SSD_GROUPS

    def body(xs_ref, b_ref, c_ref, dtr_ref, z_ref, bias_ref, alog_ref, d_ref, g_ref,
             ya_ref, yssd_ref, st_ref, state):
        ci = pl.program_id(0)

        @pl.when(ci == 0)
        def _():
            state[...] = jnp.zeros_like(state)

        st_ref[0] = state[...]
        dt, _, cum, cum_t, _, row, col = _ssd_chunk_terms(dtr_ref[...], bias_ref[...], alog_ref[...])
        causal = row >= col
        expand, _ = _head_maps()
        dt_full = _dot_hi(dt, expand)
        cum_full = _dot_hi(cum, expand)
        last_full = cum_full[q - 1:q, :]
        xs = xs_ref[...]
        xdt = xs * dt_full
        xd = (xdt * jnp.exp(last_full - cum_full)).astype(BF16)
        xdt_b = xdt.astype(BF16)
        e_full = jnp.exp(cum_full)
        eend_full = jnp.exp(last_full)
        for g in range(SSD_GROUPS):
            slab = slice(g * gw, (g + 1) * gw)
            bg = b_ref[:, g * n:(g + 1) * n].astype(BF16)
            cg = c_ref[:, g * n:(g + 1) * n].astype(BF16)
            cb = _dot_nt(cg, bg)
            st_g = state[:, slab]
            pairs = []
            for pr in range(SSD_HPG // 2):
                h0 = g * SSD_HPG + 2 * pr
                m0 = (cb * _ssd_decay(cum, cum_t, h0, causal)).astype(BF16)
                m1 = (cb * _ssd_decay(cum, cum_t, h0 + 1, causal)).astype(BF16)
                pairs.append(_dot(jnp.concatenate([m0, m1], axis=1), _pair_block_diag(xdt_b[:, h0 * p:(h0 + 2) * p])))
            y = jnp.concatenate(pairs, axis=1) + e_full[:, slab] * _dot(cg, st_g.astype(BF16))
            yssd_ref[:, slab] = y + d_ref[:, slab] * xs[:, slab]
            state[:, slab] = eend_full[:, slab] * st_g + _dot_tn(bg, xd[:, slab])
        zv = z_ref[...]
        ya1 = yssd_ref[...] * (zv * _sigmoid(zv))
        for g in range(SSD_GROUPS):
            gsl = slice(g * gw, (g + 1) * gw)
            sl = ya1[:, gsl]
            rg = lax.rsqrt(jnp.mean(sl * sl, axis=-1, keepdims=True) + EPS)
            ya_ref[:, gsl] = (sl * rg * g_ref[:, gsl]).astype(BF16)

    blk = lambda w, j: pl.BlockSpec((q, w), lambda c: (c, j))
    return _tiled_call(
        body, name=name, grid=(nc,),
        in_specs=[blk(di, 0), blk(512, 2), blk(512, 3), blk(LANES, 0), blk(di, 0),
                  _full((1, LANES)), _full((1, LANES)), _full((1, di)), _full((1, di))],
        out_specs=[blk(di, 0), blk(di, 0), pl.BlockSpec((1, n, di), lambda c: (c, 0, 0))],
        out_shape=[jax.ShapeDtypeStruct((t, di), BF16), jax.ShapeDtypeStruct((t, di), F32),
                   jax.ShapeDtypeStruct((nc, n, di), F32)],
        scratch_shapes=[pltpu.VMEM((n, di), F32)],
        operands=(xbc, xbc, xbc, dtr, z, bias_row, alog_row, d_row, norm_g.reshape(1, di)), comm=comm)


def ssd_bwd(dya, yssd, z, xbc, dtr, states, bias_row, alog_row, d_row, norm_g, *, name, comm=None):
    t = xbc.shape[0]
    q, p, n = SSD_CHUNK, SSD_HEADDIM, SSD_STATE
    nc = t // q
    di = SSD_HEADS * p
    gw = di // SSD_GROUPS

    def body(dya_ref, yssd_ref, z_ref, xs_ref, b_ref, c_ref, dtr_ref, st_ref, bias_ref, alog_ref, d_ref, g_ref,
             dz_ref, dxbc_ref, ddtr_ref, dng_ref, dd_ref, dalog_ref, dbias_ref, dstate, dyssd):
        ci = pl.program_id(0)

        @pl.when(ci == 0)
        def _():
            dstate[...] = jnp.zeros_like(dstate)
            dng_ref[...] = jnp.zeros_like(dng_ref)
            dd_ref[...] = jnp.zeros_like(dd_ref)
            dalog_ref[...] = jnp.zeros_like(dalog_ref)
            dbias_ref[...] = jnp.zeros_like(dbias_ref)

        zv = z_ref[...]
        sz = _sigmoid(zv)
        silu_z = zv * sz
        yv = yssd_ref[...]
        ya1 = yv * silu_z
        dyav = dya_ref[...]
        for g in range(SSD_GROUPS):
            gsl = slice(g * gw, (g + 1) * gw)
            sl = ya1[:, gsl]
            rg = lax.rsqrt(jnp.mean(sl * sl, axis=-1, keepdims=True) + EPS)
            ya2 = sl * rg
            dy_g = dyav[:, gsl]
            dng_ref[:, gsl] += jnp.sum(dy_g * ya2, axis=0, keepdims=True)
            dya2 = dy_g * g_ref[:, gsl]
            dya1 = rg * (dya2 - ya2 * jnp.mean(dya2 * ya2, axis=-1, keepdims=True))
            dyssd[:, gsl] = dya1 * silu_z[:, gsl]
            dz_ref[:, gsl] = (dya1 * yv[:, gsl] * (sz[:, gsl] * (1.0 + zv[:, gsl] * (1.0 - sz[:, gsl])))).astype(BF16)

        dtr_v = dtr_ref[...]
        dt, a_row, cum, cum_t, ltri, row, col = _ssd_chunk_terms(dtr_v, bias_ref[...], alog_ref[...])
        causal = row >= col
        expand, collapse = _head_maps()
        lane = lax.broadcasted_iota(jnp.int32, (1, LANES), 1)
        sub = lax.broadcasted_iota(jnp.int32, (LANES, 1), 0)
        is_last = lax.broadcasted_iota(jnp.int32, (q, 1), 0) == q - 1
        low = lax.broadcasted_iota(jnp.int32, (q, 2 * p), 1) < p
        dt_full = _dot_hi(dt, expand)
        cum_full = _dot_hi(cum, expand)
        last_full = cum_full[q - 1:q, :]
        e_full = jnp.exp(cum_full)
        eend_full = jnp.exp(last_full)
        dend_full = jnp.exp(last_full - cum_full)
        xs = xs_ref[...]
        xdt = xs * dt_full
        xdt_b = xdt.astype(BF16)
        xd_b = (xdt * dend_full).astype(BF16)
        dy_all = dyssd[...]
        dy_b = dy_all.astype(BF16)
        dg_b = (dy_all * e_full).astype(BF16)
        dcum_mat = jnp.zeros((q, LANES), F32)
        rmat = jnp.zeros((LANES, q), F32)
        yoff_prod, bds_list, dx_list, dlast_parts = [], [], [], []
        for g in range(SSD_GROUPS):
            slab = slice(g * gw, (g + 1) * gw)
            bg = b_ref[:, g * n:(g + 1) * n].astype(BF16)
            cg = c_ref[:, g * n:(g + 1) * n].astype(BF16)
            cb = _dot_nt(cg, bg)
            st_g = st_ref[0, :, slab]
            st_b = st_g.astype(BF16)
            dsn = dstate[:, slab]
            dsn_b = dsn.astype(BF16)
            yoff_prod.append(dy_all[:, slab] * e_full[:, slab] * _dot(cg, st_b))
            dcg = _dot_nt(dg_b[:, slab], st_b)
            dstate[:, slab] = _dot_tn(cg, dg_b[:, slab]) + eend_full[:, slab] * dsn
            bds_list.append(_dot(bg, dsn_b))
            dbg = _dot_nt(xd_b[:, slab], dsn_b)
            dlast_parts.append(jnp.sum(dsn * st_g, axis=0, keepdims=True))
            dcb = jnp.zeros((q, q), F32)
            dx_pairs = []
            for pr in range(SSD_HPG // 2):
                h0 = g * SSD_HPG + 2 * pr
                ps = slice(h0 * p, (h0 + 2) * p)
                l0 = _ssd_decay(cum, cum_t, h0, causal)
                l1 = _ssd_decay(cum, cum_t, h0 + 1, causal)
                m0, m1 = cb * l0, cb * l1
                dyp = dy_b[:, ps]
                zero = jnp.zeros_like(dyp)
                dy0, dy1 = jnp.where(low, dyp, zero), jnp.where(low, zero, dyp)
                dm0 = _dot_nt(dy0, xdt_b[:, ps])
                dm1 = _dot_nt(dy1, xdt_b[:, ps])
                w0, w1 = dm0 * m0, dm1 * m1
                dcum_mat = (dcum_mat + jnp.sum(w0, axis=1, keepdims=True) * (lane == h0).astype(F32)
                            + jnp.sum(w1, axis=1, keepdims=True) * (lane == h0 + 1).astype(F32))
                rmat = (rmat + jnp.where(sub == h0, jnp.sum(w0, axis=0, keepdims=True), 0.0)
                        + jnp.where(sub == h0 + 1, jnp.sum(w1, axis=0, keepdims=True), 0.0))
                dcb = dcb + dm0 * l0 + dm1 * l1
                dx_pairs.append(_dot_tn(jnp.concatenate([m0.astype(BF16), m1.astype(BF16)], axis=0),
                                        jnp.concatenate([dy0, dy1], axis=0)))
            dx_list.append(jnp.concatenate(dx_pairs, axis=1))
            dcb_b = dcb.astype(BF16)
            dxbc_ref[:, di + g * n:di + (g + 1) * n] = dbg + _dot_tn(dcb_b, cg)
            dxbc_ref[:, di + SSD_GROUPS * n + g * n:di + SSD_GROUPS * n + (g + 1) * n] = dcg + _dot(dcb_b, bg)
        bds_all = jnp.concatenate(bds_list, axis=1)
        dx_all = jnp.concatenate(dx_list, axis=1) + dend_full * bds_all
        last_row = cum[q - 1:q, :]
        qv = _dot_hi(xdt * bds_all, collapse) * jnp.exp(last_row - cum)
        dcum_mat = dcum_mat + _dot_hi(jnp.concatenate(yoff_prod, axis=1), collapse) - qv
        dlast_full = jnp.broadcast_to(jnp.concatenate(dlast_parts, axis=1), (SUBLANES, di))
        dlast_row = _dot_hi(dlast_full, collapse)[0:1, :] * jnp.exp(last_row) + jnp.sum(qv, axis=0, keepdims=True)
        dcum_mat = dcum_mat + jnp.where(is_last, dlast_row, 0.0)
        dla = _dot_tn_hi(ltri, dcum_mat) - _dot_nt_hi((row <= col).astype(F32), rmat)
        ddt_mat = _dot_hi(dx_all * xs, collapse) + dla * a_row
        dxbc_ref[:, 0:di] = d_ref[...] * dy_all + dx_all * dt_full
        ddtr = ddt_mat * _sigmoid(dtr_v + bias_ref[...])
        ddtr_ref[...] = ddtr.astype(BF16)
        dd_full = jnp.broadcast_to(jnp.sum(dy_all * xs, axis=0, keepdims=True), (SUBLANES, di))
        dd_ref[...] += _dot_hi(dd_full, collapse)[0:1, :]
        dalog_ref[...] += jnp.sum(dla * dt, axis=0, keepdims=True) * a_row
        dbias_ref[...] += jnp.sum(ddtr, axis=0, keepdims=True)

    blk = lambda w, j: pl.BlockSpec((q, w), lambda c: (nc - 1 - c, j))
    row_out = lambda w: jax.ShapeDtypeStruct((1, w), F32)
    return _tiled_call(
        body, name=name, grid=(nc,),
        in_specs=[blk(di, 0), blk(di, 0), blk(di, 0), blk(di, 0), blk(512, 2), blk(512, 3), blk(LANES, 0),
                  pl.BlockSpec((1, n, di), lambda c: (nc - 1 - c, 0, 0)),
                  _full((1, LANES)), _full((1, LANES)), _full((1, di)), _full((1, di))],
        out_specs=[blk(di, 0), blk(CONV_DIM, 0), blk(LANES, 0),
                   _full((1, di)), _full((1, LANES)), _full((1, LANES)), _full((1, LANES))],
        out_shape=[jax.ShapeDtypeStruct((t, di), BF16), jax.ShapeDtypeStruct((t, CONV_DIM), F32),
                   jax.ShapeDtypeStruct((t, LANES), BF16), row_out(di), row_out(LANES), row_out(LANES), row_out(LANES)],
        scratch_shapes=[pltpu.VMEM((n, di), F32), pltpu.VMEM((q, di), F32)],
        operands=(dya, yssd, z, xbc, xbc, xbc, dtr, states, bias_row, alog_row, d_row, norm_g.reshape(1, di)),
        comm=comm)


S5_BLOCKS = 4
S5_BC = S5_WIDTH // S5_BLOCKS
S5_BS = S5_N // S5_BLOCKS


def _s5_tables(ar, ai, reverse):
    pows = [(ar, ai)]
    for _ in range(SUBLANES - 1):
        pr, pi = pows[-1]
        pows.append((pr * ar - pi * ai, pr * ai + pi * ar))
    row = lax.broadcasted_iota(jnp.int32, (SUBLANES, ar.shape[1]), 0)
    out = []
    for k in (1, 2, 4):
        pr, pi = pows[k - 1]
        mask = (row < SUBLANES - k) if reverse else (row >= k)
        out += [jnp.where(mask, pr, 0.0), jnp.where(mask, pi, 0.0)]
    order = list(range(SUBLANES))
    if reverse:
        order = order[::-1]
    out.append(jnp.concatenate([pows[e][0] for e in order], axis=0))
    out.append(jnp.concatenate([pows[e][1] for e in order], axis=0))
    return out


S5_CHUNKS = S5_N // LANES
S5_SCAN_GROUP = 4


def _s5_scan_setup(ab_ref, pwb, tab, seg, reverse):
    shape = (SUBLANES, S5_N)
    ar = jnp.broadcast_to(ab_ref[0:1, :], shape)
    ai = jnp.broadcast_to(ab_ref[1:2, :], shape)
    if reverse:
        ai = -ai
    cr, ci = ar, ai
    for i in range(seg):
        if i:
            cr, ci = cr * ar - ci * ai, cr * ai + ci * ar
        for c in range(S5_CHUNKS):
            pwb[c, i * SUBLANES:(i + 1) * SUBLANES, :] = cr[:, c * LANES:(c + 1) * LANES]
            pwb[S5_CHUNKS + c, i * SUBLANES:(i + 1) * SUBLANES, :] = ci[:, c * LANES:(c + 1) * LANES]
    for k, tv in enumerate(_s5_tables(cr[0:1, :], ci[0:1, :], reverse)):
        tab[k] = tv


def _s5_scan(s_scr, pwb, tab, carry, tm, reverse):
    seg = tm // SUBLANES
    first = lax.broadcasted_iota(jnp.int32, (SUBLANES, LANES), 0) == (SUBLANES - 1 if reverse else 0)
    rows = lambda i: pl.ds((seg - 1 - i) if reverse else i, SUBLANES, stride=seg)
    for c0 in range(0, S5_CHUNKS, S5_SCAN_GROUP):
        chunks = range(c0, c0 + S5_SCAN_GROUP)
        base = [(pwb[c, 0:SUBLANES, :], pwb[S5_CHUNKS + c, 0:SUBLANES, :]) for c in chunks]

        def phase1(i, xs, chunks=chunks, base=base):
            out = []
            for (xr, xi), (ar, ai), c in zip(xs, base, chunks):
                br = s_scr[c, rows(i), :]
                bi = s_scr[S5_CHUNKS + c, rows(i), :]
                xr, xi = ar * xr - ai * xi + br, ar * xi + ai * xr + bi
                s_scr[c, rows(i), :] = xr
                s_scr[S5_CHUNKS + c, rows(i), :] = xi
                out.append((xr, xi))
            return tuple(out)

        zero = jnp.zeros((SUBLANES, LANES), F32)
        ends = lax.fori_loop(0, seg, phase1, tuple((zero, zero) for _ in chunks))

        entering = []
        for (fr, fi), c in zip(ends, chunks):
            lanes = slice(c * LANES, (c + 1) * LANES)
            tb = [tab[k, :, lanes] for k in range(8)]
            for lvl, k in enumerate((1, 2, 4)):
                sh = (SUBLANES - k) if reverse else k
                pr, pi = tb[2 * lvl], tb[2 * lvl + 1]
                rr, ri = pltpu.roll(fr, sh, 0), pltpu.roll(fi, sh, 0)
                fr, fi = fr + pr * rr - pi * ri, fi + pr * ri + pi * rr
            cr, ci = carry[0:1, lanes], carry[0:1, S5_N + c * LANES:S5_N + (c + 1) * LANES]
            fr, fi = fr + tb[6] * cr - tb[7] * ci, fi + tb[6] * ci + tb[7] * cr
            last = 0 if reverse else SUBLANES - 1
            carry[0:1, lanes] = fr[last:last + 1, :]
            carry[0:1, S5_N + c * LANES:S5_N + (c + 1) * LANES] = fi[last:last + 1, :]
            sh = (SUBLANES - 1) if reverse else 1
            entering.append((jnp.where(first, cr, pltpu.roll(fr, sh, 0)), jnp.where(first, ci, pltpu.roll(fi, sh, 0))))

        def phase3(i, _, chunks=chunks, entering=entering):
            for (gr, gi), c in zip(entering, chunks):
                pr = pwb[c, pl.ds(pl.multiple_of(i * SUBLANES, SUBLANES), SUBLANES), :]
                pi = pwb[S5_CHUNKS + c, pl.ds(pl.multiple_of(i * SUBLANES, SUBLANES), SUBLANES), :]
                s_scr[c, rows(i), :] += pr * gr - pi * gi
                s_scr[S5_CHUNKS + c, rows(i), :] += pr * gi + pi * gr
            return 0

        lax.fori_loop(0, seg, phase3, 0)


def _s5_scratch(tm):
    return [pltpu.VMEM((2 * S5_CHUNKS, tm, LANES), F32), pltpu.VMEM((2 * S5_CHUNKS, tm, LANES), F32),
            pltpu.VMEM((8, SUBLANES, S5_N), F32), pltpu.VMEM((SUBLANES, 2 * S5_N), F32)]


def _s5_put(scr, j, val):
    per = S5_BS // LANES
    for q in range(per):
        scr[per * j + q] = val[:, q * LANES:(q + 1) * LANES]
        scr[S5_CHUNKS + per * j + q] = val[:, S5_BS + q * LANES:S5_BS + (q + 1) * LANES]


def _s5_get(scr, j):
    per = S5_BS // LANES
    return jnp.concatenate([scr[per * j + q] for q in range(per)] + [scr[S5_CHUNKS + per * j + q] for q in range(per)],
                           axis=1)


def s5_fwd(u5, bbc, ccc, ab8, d_row, wglu, bglu_row, *, tm, name, comm=None):
    t, w = u5.shape

    def body(u_ref, bb_ref, cc_ref, ab_ref, d_ref, wg_ref, bg_ref, s_ref, ys_ref, yb_ref, s_scr, pwb, tab, carry):
        i = pl.program_id(0)

        @pl.when(i == 0)
        def _():
            carry[...] = jnp.zeros_like(carry)
            _s5_scan_setup(ab_ref, pwb, tab, tm // SUBLANES, False)

        u = u_ref[...]
        ub = u.astype(BF16)
        for j in range(S5_BLOCKS):
            uj = ub[:, j * S5_BC:(j + 1) * S5_BC]
            _s5_put(s_scr, j, _dot(uj, bb_ref[j * S5_BC:(j + 1) * S5_BC, :]))
        _s5_scan(s_scr, pwb, tab, carry, tm, False)
        for c in range(2 * S5_CHUNKS):
            s_ref[:, c * LANES:(c + 1) * LANES] = s_scr[c]
        ys = []
        for j in range(S5_BLOCKS):
            ys.append(_dot(_s5_get(s_scr, j).astype(BF16), cc_ref[:, j * S5_BC:(j + 1) * S5_BC]))
        y = jnp.concatenate(ys, axis=1) + d_ref[...] * u
        ys_ref[...] = y
        yb1 = _gelu(y)
        tt = _dot(yb1.astype(BF16), wg_ref[...]) + bg_ref[...]
        yb_ref[...] = (yb1 * _sigmoid(tt)).astype(BF16)

    tile = lambda c: pl.BlockSpec((tm, c), lambda i: (i, 0))
    return _tiled_call(
        body, name=name, grid=(t // tm,),
        in_specs=[tile(w), _full((w, 2 * S5_BS)), _full((2 * S5_BS, w)), _full((SUBLANES, S5_N)), _full((1, w)),
                  _full((w, w)), _full((1, w))],
        out_specs=[tile(2 * S5_N), tile(w), tile(w)],
        out_shape=[jax.ShapeDtypeStruct((t, 2 * S5_N), F32), jax.ShapeDtypeStruct((t, w), F32),
                   jax.ShapeDtypeStruct((t, w), BF16)],
        scratch_shapes=_s5_scratch(tm),
        operands=(u5, bbc, ccc, ab8, d_row, wglu, bglu_row), comm=comm)


def s5_bwd(dyb, ys5, u5, s, bbc, ccc, ab8, d_row, wglu, bglu_row, *, tm, name, comm=None):
    t, w = u5.shape
    nt = t // tm
    per = tm // SUBLANES

    def body(dyb_ref, ys_ref, u_ref, s_ref, sprev_ref, bb_ref, cc_ref, ab_ref, d_ref, wg_ref, bg_ref,
             du_ref, dbb_ref, dcc_ref, dab_ref, dd_ref, dwg_ref, dbg_ref, g_scr, pwb, tab, carry):
        i = pl.program_id(0)

        @pl.when(i == 0)
        def _():
            carry[...] = jnp.zeros_like(carry)
            _s5_scan_setup(ab_ref, pwb, tab, tm // SUBLANES, True)
            for r in (dbb_ref, dcc_ref, dab_ref, dd_ref, dwg_ref, dbg_ref):
                r[...] = jnp.zeros_like(r)

        y = ys_ref[...]
        u = u_ref[...]
        yb1 = _gelu(y)
        yb1b = yb1.astype(BF16)
        sg = _sigmoid(_dot(yb1b, wg_ref[...]) + bg_ref[...])
        dybv = dyb_ref[...]
        dtt = dybv * yb1 * sg * (1.0 - sg)
        dttb = dtt.astype(BF16)
        dyb1 = dybv * sg + _dot_nt(dttb, wg_ref[...])
        dwg_ref[...] += _dot_tn(yb1b, dttb)
        dbg_ref[...] += jnp.sum(dtt, axis=0, keepdims=True)
        dy = dyb1 * _gelu_grad(y)
        dd_ref[...] += jnp.sum(dy * u, axis=0, keepdims=True)
        dyh = dy.astype(BF16)
        for j in range(S5_BLOCKS):
            cs = slice(j * S5_BC, (j + 1) * S5_BC)
            _s5_put(g_scr, j, _dot_nt(dyh[:, cs], cc_ref[:, cs]))
        _s5_scan(g_scr, pwb, tab, carry, tm, True)
        ub = u.astype(BF16)
        per_blk = S5_BS // LANES
        dus = []
        for j in range(S5_BLOCKS):
            cs = slice(j * S5_BC, (j + 1) * S5_BC)
            gb = _s5_get(g_scr, j).astype(BF16)
            dus.append(_dot_nt(gb, bb_ref[cs, :]))
            dbb_ref[cs, :] += _dot_tn(ub[:, cs], gb)
            sb = jnp.concatenate(
                [s_ref[:, (per_blk * j + q) * LANES:(per_blk * j + q + 1) * LANES] for q in range(per_blk)]
                + [s_ref[:, S5_N + (per_blk * j + q) * LANES:S5_N + (per_blk * j + q + 1) * LANES] for q in range(per_blk)],
                axis=1).astype(BF16)
            dcc_ref[:, cs] += _dot_tn(sb, dyh[:, cs])
        du_ref[...] = (jnp.concatenate(dus, axis=1) + d_ref[...] * dy).astype(BF16)
        first = lax.broadcasted_iota(jnp.int32, (tm, 1), 0) == 0
        for c in range(S5_CHUNKS):
            lr = slice(c * LANES, (c + 1) * LANES)
            li = slice(S5_N + c * LANES, S5_N + (c + 1) * LANES)
            g_re, g_im = g_scr[c], g_scr[S5_CHUNKS + c]
            shifted = []
            for sl in (lr, li):
                before = jnp.where(i == nt - 1, 0.0, sprev_ref[SUBLANES - 1:SUBLANES, sl])
                shifted.append(jnp.where(first, before, pltpu.roll(s_ref[:, sl], 1, 0)))
            sp_re, sp_im = shifted
            dab_ref[0:1, lr] += jnp.sum(g_re * sp_re + g_im * sp_im, axis=0, keepdims=True)
            dab_ref[1:2, lr] += jnp.sum(g_im * sp_re - g_re * sp_im, axis=0, keepdims=True)

    tile = lambda c: pl.BlockSpec((tm, c), lambda i: (nt - 1 - i, 0))
    acc = lambda r, c: jax.ShapeDtypeStruct((r, c), F32)
    return _tiled_call(
        body, name=name, grid=(nt,),
        in_specs=[tile(w), tile(w), tile(w), tile(2 * S5_N),
                  pl.BlockSpec((SUBLANES, 2 * S5_N), lambda i: (jnp.maximum((nt - 1 - i) * per - 1, 0), 0)),
                  _full((w, 2 * S5_BS)), _full((2 * S5_BS, w)), _full((SUBLANES, S5_N)), _full((1, w)),
                  _full((w, w)), _full((1, w))],
        out_specs=[tile(w), _full((w, 2 * S5_BS)), _full((2 * S5_BS, w)), _full((SUBLANES, S5_N)), _full((1, w)),
                   _full((w, w)), _full((1, w))],
        out_shape=[jax.ShapeDtypeStruct((t, w), BF16), acc(w, 2 * S5_BS), acc(2 * S5_BS, w), acc(SUBLANES, S5_N),
                   acc(1, w), acc(w, w), acc(1, w)],
        scratch_shapes=_s5_scratch(tm),
        operands=(dyb, ys5, u5, s, s, bbc, ccc, ab8, d_row, wglu, bglu_row), comm=comm)


def s5_discretize(a_re, a_im, log_dt, b_re, b_im, *, name):
    n = a_re.shape[0]

    def body(ar_ref, ai_ref, dt_ref, br_ref, bi_ref, ab_ref, bbr_ref, bbi_ref):
        ar, ai, dtv = ar_ref[...], ai_ref[...], jnp.exp(dt_ref[...])
        mag = jnp.exp(ar * dtv)
        abr = mag * jnp.cos(ai * dtv)
        abi = mag * jnp.sin(ai * dtv)
        den = ar * ar + ai * ai
        nr = abr - 1.0
        cr = (nr * ar + abi * ai) / den
        ci = (abi * ar - nr * ai) / den
        ab_ref[:, 0:1] = abr
        ab_ref[:, 1:2] = abi
        br, bi = br_ref[...], bi_ref[...]
        bbr_ref[...] = cr * br - ci * bi
        bbi_ref[...] = cr * bi + ci * br

    col = jax.ShapeDtypeStruct((n, 2), F32)
    mat = jax.ShapeDtypeStruct((n, S5_GROUP), F32)
    return pl.pallas_call(body, name=name, out_shape=[col, mat, mat])(a_re, a_im, log_dt, b_re, b_im)


def s5_discretize_bwd(a_re, a_im, log_dt, b_re, b_im, dab, dbb_re, dbb_im, *, name):
    n = a_re.shape[0]

    def body(ar_ref, ai_ref, dt_ref, br_ref, bi_ref, dab_ref, dbr_ref, dbi_ref, da_ref, ddt_ref, gbr_ref, gbi_ref):
        ar, ai, dtv = ar_ref[...], ai_ref[...], jnp.exp(dt_ref[...])
        mag = jnp.exp(ar * dtv)
        abr = mag * jnp.cos(ai * dtv)
        abi = mag * jnp.sin(ai * dtv)
        den = ar * ar + ai * ai
        nr = abr - 1.0
        cr = (nr * ar + abi * ai) / den
        ci = (abi * ar - nr * ai) / den
        br, bi = br_ref[...], bi_ref[...]
        dbr, dbi = dbr_ref[...], dbi_ref[...]
        gbr_ref[...] = cr * dbr + ci * dbi
        gbi_ref[...] = cr * dbi - ci * dbr
        gcr = jnp.sum(dbr * br + dbi * bi, axis=1, keepdims=True)
        gci = jnp.sum(dbi * br - dbr * bi, axis=1, keepdims=True)
        ilr, ili = ar / den, -ai / den
        gabr = dab_ref[:, 0:1] + (ilr * gcr + ili * gci)
        gabi = dab_ref[:, 1:2] + (ilr * gci - ili * gcr)
        t1r, t1i = dtv * abr, dtv * abi
        t2r, t2i = -(cr * ilr - ci * ili), -(cr * ili + ci * ilr)
        da_ref[:, 0:1] = (t1r * gabr + t1i * gabi) + (t2r * gcr + t2i * gci)
        da_ref[:, 1:2] = (t1r * gabi - t1i * gabr) + (t2r * gci - t2i * gcr)
        lr, li = ar * abr - ai * abi, ar * abi + ai * abr
        dlog = (lr * gabr + li * gabi) * dtv
        ddt_ref[...] = jnp.sum(dlog.reshape(S5_GROUPS, S5_STATE, 1), axis=1)

    col2 = jax.ShapeDtypeStruct((n, 2), F32)
    col1 = jax.ShapeDtypeStruct((S5_GROUPS, 1), F32)
    mat = jax.ShapeDtypeStruct((n, S5_GROUP), F32)
    return pl.pallas_call(body, name=name, out_shape=[col2, col1, mat, mat])(
        a_re, a_im, log_dt, b_re, b_im, dab, dbb_re, dbb_im)


def merge_fwd(ya, yb, graw, x, wba, wbb, wout, *, tm, name):
    t, d = x.shape

    def body(ya_ref, yb_ref, g_ref, x_ref, wba_ref, wbb_ref, wo_ref, x1_ref, pa_ref, pb_ref):
        pa = _dot(ya_ref[...], wba_ref[...])
        pb = _dot(yb_ref[...], wbb_ref[...])
        gate = _sigmoid(g_ref[...])
        merged = gate[:, :d] * pa + gate[:, d:] * pb
        x1_ref[...] = x_ref[...] + _dot(merged.astype(BF16), wo_ref[...])
        pa_ref[...] = pa
        pb_ref[...] = pb

    tile = lambda c: pl.BlockSpec((tm, c), lambda i: (i, 0))
    out = jax.ShapeDtypeStruct((t, d), F32)
    return pl.pallas_call(
        body, name=name, grid=(t // tm,),
        in_specs=[tile(d), tile(S5_WIDTH), tile(2 * d), tile(d), _full((d, d)), _full((S5_WIDTH, d)), _full((d, d))],
        out_specs=[tile(d), tile(d), tile(d)], out_shape=[out, out, out],
        compiler_params=_params("parallel"),
    )(ya, yb, graw, x, wba, wbb, wout)


def merge_bwd(dx1b, graw, pa, pb, wba, wbb, wout, *, tm, name):
    t, d = pa.shape

    def body(dx_ref, g_ref, pa_ref, pb_ref, wba_ref, wbb_ref, wo_ref,
             mg_ref, dg_ref, dpa_ref, dpb_ref, dya_ref, dyb_ref):
        dm = _dot_nt(dx_ref[...], wo_ref[...])
        gate = _sigmoid(g_ref[...])
        g0, g1 = gate[:, :d], gate[:, d:]
        pa, pb = pa_ref[...], pb_ref[...]
        mg_ref[...] = (g0 * pa + g1 * pb).astype(BF16)
        dg_ref[:, :d] = (dm * pa * g0 * (1.0 - g0)).astype(BF16)
        dg_ref[:, d:] = (dm * pb * g1 * (1.0 - g1)).astype(BF16)
        dpa = (dm * g0).astype(BF16)
        dpb = (dm * g1).astype(BF16)
        dpa_ref[...] = dpa
        dpb_ref[...] = dpb
        dya_ref[...] = _dot_nt(dpa, wba_ref[...])
        dyb_ref[...] = _dot_nt(dpb, wbb_ref[...])

    tile = lambda c: pl.BlockSpec((tm, c), lambda i: (i, 0))
    sds = jax.ShapeDtypeStruct
    return pl.pallas_call(
        body, name=name, grid=(t // tm,),
        in_specs=[tile(d), tile(2 * d), tile(d), tile(d), _full((d, d)), _full((S5_WIDTH, d)), _full((d, d))],
        out_specs=[tile(d), tile(2 * d), tile(d), tile(d), tile(d), tile(S5_WIDTH)],
        out_shape=[sds((t, d), BF16), sds((t, 2 * d), BF16), sds((t, d), BF16), sds((t, d), BF16),
                   sds((t, d), F32), sds((t, S5_WIDTH), F32)],
        compiler_params=_params("parallel"),
    )(dx1b, graw, pa, pb, wba, wbb, wout)


def mlp_out_loss(a1, x1, w2, gf, target, *, tm, name):
    t, d = x1.shape
    f = a1.shape[1]

    def body(a_ref, x_ref, w_ref, g_ref, tg_ref, act_ref, dx_ref, dxb_ref, loss_ref, dg_ref):
        i = pl.program_id(0)
        av = jnp.maximum(a_ref[...], 0.0)
        act = (av * av).astype(BF16)
        act_ref[...] = act
        x2 = x_ref[...] + _dot(act, w_ref[...])
        r = lax.rsqrt(jnp.mean(x2 * x2, axis=-1, keepdims=True) + EPS)
        xn = x2 * r
        err = xn * g_ref[...] - tg_ref[...]
        dyf = err * (1.0 / d)
        dxn = dyf * g_ref[...]
        dx = r * (dxn - xn * jnp.mean(dxn * xn, axis=-1, keepdims=True))
        dx_ref[...] = dx
        dxb_ref[...] = dx.astype(BF16)

        @pl.when(i == 0)
        def _():
            loss_ref[...] = jnp.zeros_like(loss_ref)
            dg_ref[...] = jnp.zeros_like(dg_ref)

        loss_ref[...] += jnp.sum(err * err) * (0.5 / d)
        dg_ref[...] += jnp.sum(dyf * xn, axis=0, keepdims=True)

    tile = lambda c: pl.BlockSpec((tm, c), lambda i: (i, 0))
    sds = jax.ShapeDtypeStruct
    return pl.pallas_call(
        body, name=name, grid=(t // tm,),
        in_specs=[tile(f), tile(d), _full((f, d)), _full((1, d)), tile(d)],
        out_specs=[tile(f), tile(d), tile(d), _full((1, LANES)), _full((1, d))],
        out_shape=[sds((t, f), BF16), sds((t, d), F32), sds((t, d), BF16), sds((1, LANES), F32), sds((1, d), F32)],
        compiler_params=_params("arbitrary"),
    )(a1, x1, w2, gf.reshape(1, d), target)


def mlp_bwd_act(dx2b, a1, w2, *, tm, name):
    t, f = a1.shape
    d = dx2b.shape[1]

    def body(dx_ref, a_ref, w_ref, o_ref):
        dact = _dot_nt(dx_ref[...], w_ref[...])
        o_ref[...] = (dact * 2.0 * jnp.maximum(a_ref[...], 0.0)).astype(BF16)

    tile = lambda c: pl.BlockSpec((tm, c), lambda i: (i, 0))
    return pl.pallas_call(
        body, name=name, grid=(t // tm,),
        in_specs=[tile(d), tile(f), _full((f, d))], out_specs=tile(f),
        out_shape=jax.ShapeDtypeStruct((t, f), BF16),
        compiler_params=_params("parallel"),
    )(dx2b, a1, w2)


ADAM_TILE_ELEMS = 128 * 1024


def _row_tile(rows, cap):
    if rows <= cap:
        return rows
    best = None
    for c in range(16, cap + 1, 16):
        if rows % c == 0:
            best = c
    assert best is not None, rows
    return best


def adamw(w, m, v, gparts, *, name):
    rows, cols = w.shape
    tr = _row_tile(rows, max(16, ADAM_TILE_ELEMS // cols))

    def body(w_ref, m_ref, v_ref, g_ref, go_ref, d_ref, mo_ref, vo_ref):
        g = g_ref[0].astype(F32)
        for k in range(1, N_DEV):
            g = g + g_ref[k].astype(F32)
        mn = ADAM_B1 * m_ref[...] + (1.0 - ADAM_B1) * g
        vn = ADAM_B2 * v_ref[...] + (1.0 - ADAM_B2) * (g * g)
        m_hat = mn / (1.0 - ADAM_B1 ** ADAM_STEP)
        v_hat = vn / (1.0 - ADAM_B2 ** ADAM_STEP)
        go_ref[...] = g
        d_ref[...] = -ADAM_LR * (m_hat / (jnp.sqrt(v_hat) + ADAM_EPS) + ADAM_WD * w_ref[...])
        mo_ref[...] = mn
        vo_ref[...] = vn

    tile = pl.BlockSpec((tr, cols), lambda i: (i, 0))
    out = jax.ShapeDtypeStruct((rows, cols), F32)
    return pl.pallas_call(
        body, name=name, grid=(rows // tr,),
        in_specs=[tile, tile, tile, pl.BlockSpec((N_DEV, tr, cols), lambda i: (0, i, 0))],
        out_specs=[tile, tile, tile, tile], out_shape=[out, out, out, out],
        compiler_params=_params("parallel"),
    )(w, m, v, gparts)


def all_gather_arrays(blocks, *, name):
    na = len(blocks)

    def body(*refs):
        x_refs, out_refs = refs[:na], refs[na:2 * na]
        send_sems, recv_sems, local_sems = refs[2 * na:]
        x, y, c, _ = _place()
        me, sibling = (x, y, c), (x, y, 1 - c)
        chips = [(1 - x, y), (x, 1 - y), (1 - x, 1 - y)]

        def copy(a, k, blk, to, own=False):
            px, py, pc = blk
            dst = out_refs[a].at[4 * px + 2 * py + pc]
            return pltpu.make_async_remote_copy(
                src_ref=x_refs[a] if own else dst, dst_ref=dst,
                send_sem=send_sems.at[7 * a + k], recv_sem=recv_sems.at[7 * a + k], device_id=to, device_id_type=_MESH)

        mine = [pltpu.make_async_copy(x_refs[a], out_refs[a].at[4 * x + 2 * y + c], local_sems.at[a]) for a in range(na)]
        for cp in mine:
            cp.start()
        sent = []
        for a in range(na):
            first = [copy(a, 0, me, sibling, own=True)]
            first += [copy(a, 1 + j, me, (*chip, c), own=True) for j, chip in enumerate(chips)]
            for cp in first:
                cp.start()
            sent += first
        for a in range(na):
            for j, chip in enumerate(chips):
                copy(a, 1 + j, (*chip, c), me).wait_recv()
                fwd = copy(a, 4 + j, (*chip, c), sibling)
                fwd.start()
                sent.append(fwd)
        for a in range(na):
            copy(a, 0, sibling, me).wait_recv()
            for j, chip in enumerate(chips):
                copy(a, 4 + j, (*chip, 1 - c), me).wait_recv()
        for cp in sent:
            cp.wait_send()
        for cp in mine:
            cp.wait()

    any_spec = pl.BlockSpec(memory_space=pl.ANY)
    return pl.pallas_call(
        body, name=name, in_specs=[any_spec] * na, out_specs=[any_spec] * na,
        out_shape=[jax.ShapeDtypeStruct((N_DEV,) + b.shape, b.dtype) for b in blocks],
        scratch_shapes=[pltpu.SemaphoreType.DMA((7 * na,)), pltpu.SemaphoreType.DMA((7 * na,)),
                        pltpu.SemaphoreType.DMA((na,))],
        compiler_params=pltpu.CompilerParams(has_side_effects=True),
    )(*blocks)


def exchange_partials(parts, *, name):
    na = len(parts)

    def body(*refs):
        local, remote = _comm_copies("exchange", refs[:na], refs[na:2 * na], *refs[2 * na:])
        for cp in local + remote:
            cp.start()
        for cp in remote:
            cp.wait_recv()
        for cp in remote:
            cp.wait_send()
        for cp in local:
            cp.wait()

    any_spec = pl.BlockSpec(memory_space=pl.ANY)
    return pl.pallas_call(
        body, name=name, in_specs=[any_spec] * na, out_specs=[any_spec] * na,
        out_shape=[jax.ShapeDtypeStruct(p.shape, p.dtype) for p in parts],
        scratch_shapes=[pltpu.SemaphoreType.DMA((na * N_REL,)), pltpu.SemaphoreType.DMA((na * N_REL,)),
                        pltpu.SemaphoreType.DMA((na,))],
        compiler_params=pltpu.CompilerParams(has_side_effects=True),
    )(*parts)


def _lane_row(v):
    return jnp.pad(v.astype(F32), (0, LANES - v.shape[0])).reshape(1, LANES)


def _block_diag_b(bb):
    eye = jnp.eye(SUBLANES, dtype=F32)
    bt = bb.reshape(S5_BLOCKS, 8, S5_STATE, S5_GROUP).transpose(0, 1, 3, 2)
    return (bt[:, :, :, None, :] * eye[None, :, None, :, None]).reshape(S5_WIDTH, S5_BS)


def _block_diag_c(cc):
    eye = jnp.eye(SUBLANES, dtype=F32)
    ct = cc.reshape(S5_BLOCKS, 8, S5_GROUP, S5_STATE).transpose(3, 0, 1, 2)
    return (ct[None, :, :, :, :] * eye[:, None, None, :, None]).reshape(S5_BS, S5_WIDTH)


def _diag_blocks_b(m):
    eye = jnp.eye(SUBLANES, dtype=F32)
    m5 = m.reshape(S5_BLOCKS, 8, S5_GROUP, 8, S5_STATE)
    return jnp.sum(m5 * eye[None, :, None, :, None], axis=3).transpose(0, 1, 3, 2).reshape(S5_GROUPS, S5_STATE, S5_GROUP)


def _diag_blocks_c(m):
    eye = jnp.eye(SUBLANES, dtype=F32)
    m5 = m.reshape(8, S5_STATE, S5_BLOCKS, 8, S5_GROUP)
    return jnp.sum(m5 * eye[:, None, None, :, None], axis=0).transpose(1, 2, 3, 0).reshape(S5_GROUPS, S5_GROUP, S5_STATE)


def train_step(x, target, p, m, v):
    t = x.shape[0]
    tm = min(256, t)
    bf = lambda n: p[n].astype(BF16)
    slots = lambda n, a: a.reshape((N_DEV,) + _shard_shape(n))
    updated = {}

    def update(names, got):
        for n, parts in zip(names, got):
            updated[n] = adamw(p[n], m[n], v[n], parts, name=f"adamw_{n}")

    st_in, st_conv = all_gather_arrays([bf("w_in"), p["conv_w"]], name="gather_first")
    win_p = jnp.concatenate([st_in[k][:, a:b] for seg in IN_PADDED_ORDER for k, a, b in _shard_pieces(*IN_SEGMENTS[seg])]
                            + [jnp.zeros((D_MODEL, LANES - SSD_HEADS), BF16)], axis=1)
    conv_w8 = jnp.pad(_join_shards("conv_w", st_conv), ((0, SUBLANES - SSD_CONV), (0, 0)))
    (h, xbc_raw, graw, z, u5, dtr), (st_branch, st_out, st_glu) = rms_matmul(
        x, p["norm_mix_g"], win_p, IN_SPLITS, tm=tm, name="in_proj",
        comm=("gather", [bf("w_branch"), bf("w_out"), bf("s5_glu_w")]))
    w_branch = st_branch.reshape(D_MODEL + S5_WIDTH, D_MODEL)
    wba, wbb = w_branch[:D_MODEL], w_branch[D_MODEL:]
    wout, wglu = st_out.reshape(D_MODEL, D_MODEL), st_glu.reshape(S5_WIDTH, S5_WIDTH)
    xbc = conv_silu_fwd(xbc_raw, conv_w8, p["conv_b"], tm=tm, name="conv_fwd")
    bias_row, alog_row = _lane_row(p["dt_bias"]), _lane_row(p["a_log"])
    d_row = jnp.repeat(p["d_ssd"], SSD_HEADDIM).reshape(1, SSD_HEADS * SSD_HEADDIM)
    (ya, yssd, states), (w1,) = ssd_fwd(xbc, dtr, z, bias_row, alog_row, d_row, p["ssd_norm_g"], name="ssd_fwd",
                                        comm=("gather", [bf("w_mlp_in")]))

    n = S5_N
    a_re_c, a_im_c = p["s5_a_re"].reshape(n, 1), p["s5_a_im"].reshape(n, 1)
    dt_c = jnp.repeat(p["s5_log_dt"], S5_STATE).reshape(n, 1)
    b_re_m, b_im_m = p["s5_b_re"].reshape(n, S5_GROUP), p["s5_b_im"].reshape(n, S5_GROUP)
    ab, bb_re, bb_im = s5_discretize(a_re_c, a_im_c, dt_c, b_re_m, b_im_m, name="s5_disc")
    ab8 = jnp.pad(ab.T, ((0, SUBLANES - 2), (0, 0)))
    bbc = jnp.concatenate([_block_diag_b(bb_re.reshape(S5_GROUPS, S5_STATE, S5_GROUP)),
                           _block_diag_b(bb_im.reshape(S5_GROUPS, S5_STATE, S5_GROUP))], axis=1).astype(BF16)
    ccc = jnp.concatenate([_block_diag_c(p["s5_c_re"]), -_block_diag_c(p["s5_c_im"])], axis=0).astype(BF16)
    s5d_row = p["s5_d"].reshape(1, S5_WIDTH)
    bglu_row = p["s5_glu_b"].reshape(1, S5_WIDTH)
    (s, ys5, yb), (st_w2,) = s5_fwd(u5, bbc, ccc, ab8, s5d_row, wglu, bglu_row, tm=tm, name="s5_fwd",
                                    comm=("gather", [bf("w_mlp_out")]))
    w2 = st_w2.reshape(D_FF, D_MODEL)

    x1, pa, pb = merge_fwd(ya, yb, graw, x, wba, wbb, wout, tm=tm, name="merge_fwd")
    (h2, a1), _ = rms_matmul(x1, p["norm_mlp_g"], w1, (D_FF,), tm=tm, name="mlp_in")
    act, dx2, dx2b, loss_row, dgf = mlp_out_loss(a1, x1, w2, p["norm_final_g"], target, tm=tm, name="mlp_out_loss")

    g = {}
    g["norm_final_g"] = dgf[0]
    da1 = mlp_bwd_act(dx2b, a1, w2, tm=tm, name="mlp_bwd_act")
    dw2 = slots("w_mlp_out", matmul_tn(act, dx2b, out_dtype=BF16, name="dw_mlp_out"))
    dw1 = matmul_tn(h2, da1, out_dtype=BF16, col_shards=N_DEV, name="dw_mlp_in")
    (dx1, dx1b, dgm), _ = matmul_nt_rms_bwd([da1], w1, x1, p["norm_mlp_g"], dx2, tm=tm, name="mlp_in_bwd")
    g["norm_mlp_g"] = dgm[0]
    merged, dgraw, dpa, dpb, dya, dyb = merge_bwd(dx1b, graw, pa, pb, wba, wbb, wout, tm=tm, name="merge_bwd")
    dwo = slots("w_out", matmul_tn(merged, dx1b, out_dtype=BF16, name="dw_out"))
    dwb = slots("w_branch", jnp.concatenate([matmul_tn(ya, dpa, out_dtype=BF16, name="dw_branch_a"),
                                             matmul_tn(yb, dpb, out_dtype=BF16, name="dw_branch_b")], axis=0))

    (du5, dbbc, dccc, dab, dd5, dwglu, dbglu), got = s5_bwd(
        dyb, ys5, u5, s, bbc, ccc, ab8, s5d_row, wglu, bglu_row, tm=tm, name="s5_bwd",
        comm=("exchange", [dw2, dw1]))
    update(["w_mlp_out", "w_mlp_in"], got)
    g["s5_glu_b"], g["s5_d"] = dbglu[0], dd5[0]
    g["s5_c_re"] = _diag_blocks_c(dccc[:S5_BS])
    g["s5_c_im"] = -_diag_blocks_c(dccc[S5_BS:])
    dbb_re = _diag_blocks_b(dbbc[:, :S5_BS]).reshape(n, S5_GROUP)
    dbb_im = _diag_blocks_b(dbbc[:, S5_BS:]).reshape(n, S5_GROUP)
    da, dlogdt, gb_re, gb_im = s5_discretize_bwd(a_re_c, a_im_c, dt_c, b_re_m, b_im_m, dab[:2].T, dbb_re, dbb_im,
                                                  name="s5_disc_bwd")
    g["s5_a_re"] = da[:, 0].reshape(S5_GROUPS, S5_STATE)
    g["s5_a_im"] = da[:, 1].reshape(S5_GROUPS, S5_STATE)
    g["s5_log_dt"] = dlogdt[:, 0]
    g["s5_b_re"] = gb_re.reshape(S5_GROUPS, S5_STATE, S5_GROUP)
    g["s5_b_im"] = gb_im.reshape(S5_GROUPS, S5_STATE, S5_GROUP)

    (dz, dxbc, ddtr, dng, dd_row, dalog_row, dbias_row), got = ssd_bwd(
        dya, yssd, z, xbc, dtr, states, bias_row, alog_row, d_row, p["ssd_norm_g"], name="ssd_bwd",
        comm=("exchange", [dwo, dwb, slots("s5_glu_w", dwglu.astype(BF16))]))
    update(["w_out", "w_branch", "s5_glu_w"], got)
    g["ssd_norm_g"] = dng[0]
    g["d_ssd"], g["a_log"], g["dt_bias"] = dd_row[0, :SSD_HEADS], dalog_row[0, :SSD_HEADS], dbias_row[0, :SSD_HEADS]
    dxc, dconv_b, dconv_w8 = conv_silu_bwd_a(xbc_raw, dxbc, conv_w8, p["conv_b"], tm=tm, name="conv_bwd_a")
    g["conv_b"] = dconv_b[0]
    dconv = dconv_w8[:SSD_CONV].reshape(SSD_CONV, N_DEV, CONV_DIM // N_DEV).transpose(1, 0, 2)
    dxbc_raw = conv_bwd_b(dxc, conv_w8, tm=tm, name="conv_bwd_b")
    dproj = [dxbc_raw, dgraw, dz, du5, ddtr]
    dxbc_w, dgate_w, dz_w, du5_w, ddt_w = [matmul_tn(h, piece, out_dtype=BF16, name=f"dw_in_{k}")
                                           for k, piece in enumerate(dproj)]
    by_segment = {"z": dz_w, "xbc": dxbc_w, "dt": ddt_w, "u5": du5_w, "gates": dgate_w}
    width = D_IN_PROJ // N_DEV
    dw_in = jnp.stack([jnp.concatenate(
        [by_segment[seg][:, max(k * width, lo) - lo:min((k + 1) * width, hi) - lo]
         for seg, (lo, hi) in IN_SEGMENTS.items() if max(k * width, lo) < min((k + 1) * width, hi)], axis=1)
        for k in range(N_DEV)])

    early = [n for n in SMALL_ORDER if n != "norm_mix_g"]
    bcast = lambda rows: jnp.broadcast_to(rows, (N_DEV,) + rows.shape)
    (grad_x, _, dgx), got = matmul_nt_rms_bwd(
        dproj, win_p, x, p["norm_mix_g"], dx1, tm=tm, name="in_proj_bwd",
        comm=("exchange", [dw_in, dconv, bcast(_pack_small(g, early, loss_row[0, 0]))]))
    update(["w_in", "conv_w"], got[:2])
    g["norm_mix_g"] = dgx[0]
    zero = jnp.zeros((), F32)
    got_late, = exchange_partials([bcast(_pack_small(g, ["norm_mix_g"], zero))], name="exchange_last")
    shapes = {n: p[n].shape for n in SMALL_ORDER}
    loss = None
    for names, parts, tag in ((early, got[2], "early"), (["norm_mix_g"], got_late, "last")):
        packed = adamw(_pack_small(p, names, zero), _pack_small(m, names, zero), _pack_small(v, names, zero), parts,
                       name=f"adamw_small_{tag}")
        for j, rows in enumerate(packed):
            vals, extra = _unpack_small(rows, names, shapes)
            if tag == "early" and j == 0:
                loss = extra
            for n in names:
                updated.setdefault(n, [None] * 4)[j] = vals[n]
    return loss, grad_x, updated


WEIGHT_ORDER = ["norm_mix_g", "w_in", "conv_w", "conv_b", "dt_bias", "a_log", "d_ssd", "ssd_norm_g", "s5_a_re",
                "s5_a_im", "s5_log_dt", "s5_b_re", "s5_b_im", "s5_c_re", "s5_c_im", "s5_d", "s5_glu_w", "s5_glu_b",
                "w_branch", "w_out", "norm_mlp_g", "w_mlp_in", "w_mlp_out", "norm_final_g"]
SHARDED = {"w_in": ((D_MODEL, D_IN_PROJ), 1), "conv_w": ((SSD_CONV, CONV_DIM), 1), "s5_glu_w": ((S5_WIDTH, S5_WIDTH), 0),
           "w_branch": ((D_MODEL + S5_WIDTH, D_MODEL), 0), "w_out": ((D_MODEL, D_MODEL), 0),
           "w_mlp_in": ((D_MODEL, D_FF), 1), "w_mlp_out": ((D_FF, D_MODEL), 0)}
SHARDED_ORDER = ["w_in", "conv_w", "s5_glu_w", "w_branch", "w_out", "w_mlp_in", "w_mlp_out"]
SMALL_ORDER = [n for n in WEIGHT_ORDER if n not in SHARDED]


def _shard_shape(name):
    (r, c), ax = SHARDED[name]
    return (r, c // N_DEV) if ax == 1 else (r // N_DEV, c)


def _join_shards(name, stacked):
    (r, c), ax = SHARDED[name]
    if ax == 1:
        return stacked.transpose(1, 0, 2).reshape(r, c)
    return stacked.reshape(r, c)


def _pack_small(values, names, extra):
    flat = jnp.concatenate([values[n].reshape(-1).astype(F32) for n in names] + [extra.reshape(1).astype(F32)])
    rows = -(-flat.shape[0] // (LANES * 16)) * 16
    return jnp.pad(flat, (0, rows * LANES - flat.shape[0])).reshape(rows, LANES)


def _unpack_small(rows, names, shapes):
    flat = rows.reshape(-1)
    out, off = {}, 0
    for n in names:
        size = math.prod(shapes[n])
        out[n] = flat[off:off + size].reshape(shapes[n])
        off += size
    return out, flat[off]


def _shard_pieces(lo, hi):
    width = D_IN_PROJ // N_DEV
    out = []
    while lo < hi:
        k = lo // width
        end = min(hi, (k + 1) * width)
        out.append((k, lo - k * width, end - k * width))
        lo = end
    return out


IN_SEGMENTS = {"z": (0, 1024), "xbc": (1024, 3072), "dt": (3072, 3088), "u5": (3088, 3600), "gates": (3600, 5648)}
IN_PADDED_ORDER = ("xbc", "gates", "z", "u5", "dt")


def kernel(x, norm_mix_g, w_in, conv_w, conv_b, dt_bias, a_log, d_ssd, ssd_norm_g, s5_a_re, s5_a_im, s5_log_dt, s5_b_re, s5_b_im, s5_c_re, s5_c_im, s5_d, s5_glu_w, s5_glu_b, w_branch, w_out, norm_mlp_g, w_mlp_in, w_mlp_out, norm_final_g, loss_target, m_norm_mix_g, m_w_in, m_conv_w, m_conv_b, m_dt_bias, m_a_log, m_d_ssd, m_ssd_norm_g, m_s5_a_re, m_s5_a_im, m_s5_log_dt, m_s5_b_re, m_s5_b_im, m_s5_c_re, m_s5_c_im, m_s5_d, m_s5_glu_w, m_s5_glu_b, m_w_branch, m_w_out, m_norm_mlp_g, m_w_mlp_in, m_w_mlp_out, m_norm_final_g, v_norm_mix_g, v_w_in, v_conv_w, v_conv_b, v_dt_bias, v_a_log, v_d_ssd, v_ssd_norm_g, v_s5_a_re, v_s5_a_im, v_s5_log_dt, v_s5_b_re, v_s5_b_im, v_s5_c_re, v_s5_c_im, v_s5_d, v_s5_glu_w, v_s5_glu_b, v_w_branch, v_w_out, v_norm_mlp_g, v_w_mlp_in, v_w_mlp_out, v_norm_final_g):
    w = dict(norm_mix_g=norm_mix_g, w_in=w_in, conv_w=conv_w, conv_b=conv_b, dt_bias=dt_bias, a_log=a_log, d_ssd=d_ssd,
             ssd_norm_g=ssd_norm_g, s5_a_re=s5_a_re, s5_a_im=s5_a_im, s5_log_dt=s5_log_dt, s5_b_re=s5_b_re,
             s5_b_im=s5_b_im, s5_c_re=s5_c_re, s5_c_im=s5_c_im, s5_d=s5_d, s5_glu_w=s5_glu_w, s5_glu_b=s5_glu_b,
             w_branch=w_branch, w_out=w_out, norm_mlp_g=norm_mlp_g, w_mlp_in=w_mlp_in, w_mlp_out=w_mlp_out,
             norm_final_g=norm_final_g)
    m = dict(norm_mix_g=m_norm_mix_g, w_in=m_w_in, conv_w=m_conv_w, conv_b=m_conv_b, dt_bias=m_dt_bias, a_log=m_a_log,
             d_ssd=m_d_ssd, ssd_norm_g=m_ssd_norm_g, s5_a_re=m_s5_a_re, s5_a_im=m_s5_a_im, s5_log_dt=m_s5_log_dt,
             s5_b_re=m_s5_b_re, s5_b_im=m_s5_b_im, s5_c_re=m_s5_c_re, s5_c_im=m_s5_c_im, s5_d=m_s5_d,
             s5_glu_w=m_s5_glu_w, s5_glu_b=m_s5_glu_b, w_branch=m_w_branch, w_out=m_w_out, norm_mlp_g=m_norm_mlp_g,
             w_mlp_in=m_w_mlp_in, w_mlp_out=m_w_mlp_out, norm_final_g=m_norm_final_g)
    v = dict(norm_mix_g=v_norm_mix_g, w_in=v_w_in, conv_w=v_conv_w, conv_b=v_conv_b, dt_bias=v_dt_bias, a_log=v_a_log,
             d_ssd=v_d_ssd, ssd_norm_g=v_ssd_norm_g, s5_a_re=v_s5_a_re, s5_a_im=v_s5_a_im, s5_log_dt=v_s5_log_dt,
             s5_b_re=v_s5_b_re, s5_b_im=v_s5_b_im, s5_c_re=v_s5_c_re, s5_c_im=v_s5_c_im, s5_d=v_s5_d,
             s5_glu_w=v_s5_glu_w, s5_glu_b=v_s5_glu_b, w_branch=v_w_branch, w_out=v_w_out, norm_mlp_g=v_norm_mlp_g,
             w_mlp_in=v_w_mlp_in, w_mlp_out=v_w_mlp_out, norm_final_g=v_norm_final_g)

    loss, grad_x, upd = train_step(x[0], loss_target[0], w, m, v)
    return (loss, grad_x.reshape(x.shape), *[upd[n][j] for j in range(4) for n in WEIGHT_ORDER])
```

```python
import functools
import math

import jax
import jax.numpy as jnp
from jax import lax
from jax.experimental import pallas as pl
from jax.experimental.pallas import tpu as pltpu

F32 = jnp.float32
BF16 = jnp.bfloat16
HI = lax.Precision.HIGHEST

N_DEV = 8
D_MODEL = 1024
SSD_HEADS = 16
SSD_HEADDIM = 64
SSD_GROUPS = 4
SSD_HPG = 4
SSD_STATE = 128
SSD_CHUNK = 128
SSD_CONV = 4
CONV_DIM = 2048
S5_WIDTH = 512
S5_GROUP = 16
S5_GROUPS = 32
S5_STATE = 64
S5_N = S5_GROUPS * S5_STATE
D_FF = 4096
D_IN_PROJ = 5648
IN_SPLITS = (2048, 2048, 1024, 512, 128)
D_IN_PAD = sum(IN_SPLITS)
EPS = 1e-6
LANES = 128
SUBLANES = 8
VMEM_LIMIT = 56 * 1024 * 1024

ADAM_LR = 0.001
ADAM_B1 = 0.9
ADAM_B2 = 0.999
ADAM_EPS = 1e-08
ADAM_WD = 0.01
ADAM_STEP = 10

GELU_C = math.sqrt(2.0 / math.pi)
GELU_K = 0.044715


def _params(*sem):
    return pltpu.CompilerParams(dimension_semantics=sem, vmem_limit_bytes=VMEM_LIMIT)


def _full(shape):
    nd = len(shape)
    return pl.BlockSpec(shape, lambda *_: (0,) * nd)


_MESH = pl.DeviceIdType.MESH
_RELATIONS = [(dx, dy, dc) for dx in (0, 1) for dy in (0, 1) for dc in (0, 1)][1:]
N_REL = len(_RELATIONS)


def _place():
    x, y, c = lax.axis_index("x"), lax.axis_index("y"), lax.axis_index("c")
    return x, y, c, 4 * x + 2 * y + c


def _comm_out_shapes(comm):
    kind, arrays = comm
    lead = (N_DEV,) if kind == "gather" else ()
    return [jax.ShapeDtypeStruct(lead + a.shape, a.dtype) for a in arrays]


def _comm_copies(kind, src_refs, dst_refs, send_sems, recv_sems, local_sems):
    x, y, c, idx = _place()
    local, remote = [], []
    for a, (src, dst) in enumerate(zip(src_refs, dst_refs)):
        local.append(pltpu.make_async_copy(src if kind == "gather" else src.at[idx], dst.at[idx], local_sems.at[a]))
        for k, (dx, dy, dc) in enumerate(_RELATIONS):
            px, py, pc = x ^ dx, y ^ dy, c ^ dc
            remote.append(pltpu.make_async_remote_copy(
                src_ref=src if kind == "gather" else src.at[4 * px + 2 * py + pc], dst_ref=dst.at[idx],
                send_sem=send_sems.at[N_REL * a + k], recv_sem=recv_sems.at[N_REL * a + k],
                device_id=(px, py, pc), device_id_type=_MESH))
    return local, remote


def _tiled_call(body, *, name, grid, in_specs, out_specs, out_shape, operands, scratch_shapes=(), comm=None):
    params = pltpu.CompilerParams(dimension_semantics=("arbitrary",), vmem_limit_bytes=VMEM_LIMIT,
                                  has_side_effects=comm is not None)
    if comm is None:
        outs = pl.pallas_call(body, name=name, grid=grid, in_specs=in_specs, out_specs=out_specs, out_shape=out_shape,
                              scratch_shapes=list(scratch_shapes), compiler_params=params)(*operands)
        return outs, []
    kind, arrays = comm
    na, n_in, n_out, n_scr = len(arrays), len(in_specs), len(out_specs), len(scratch_shapes)
    last = grid[0] - 1

    def hosted(*refs):
        ins, refs = refs[:n_in], refs[n_in:]
        cin, refs = refs[:na], refs[na:]
        outs, refs = refs[:n_out], refs[n_out:]
        cout, refs = refs[:na], refs[na:]
        scr, sems = refs[:n_scr], refs[n_scr:]
        i = pl.program_id(0)

        @pl.when(i == 0)
        def _():
            local, remote = _comm_copies(kind, cin, cout, *sems)
            for cp in local + remote:
                cp.start()

        body(*ins, *outs, *scr)

        @pl.when(i == last)
        def _():
            local, remote = _comm_copies(kind, cin, cout, *sems)
            for cp in remote:
                cp.wait_recv()
            for cp in remote:
                cp.wait_send()
            for cp in local:
                cp.wait()

    any_spec = pl.BlockSpec(memory_space=pl.ANY)
    res = pl.pallas_call(
        hosted, name=name, grid=grid,
        in_specs=list(in_specs) + [any_spec] * na, out_specs=list(out_specs) + [any_spec] * na,
        out_shape=list(out_shape) + _comm_out_shapes(comm),
        scratch_shapes=list(scratch_shapes) + [pltpu.SemaphoreType.DMA((N_REL * na,)), pltpu.SemaphoreType.DMA((N_REL * na,)),
                                               pltpu.SemaphoreType.DMA((na,))],
        compiler_params=params)(*operands, *arrays)
    return res[:n_out], res[n_out:]


def _dot(a, b):
    return jnp.dot(a, b, preferred_element_type=F32)


def _dot_nt(a, b):
    return lax.dot_general(a, b, (((1,), (1,)), ((), ())), preferred_element_type=F32)


def _dot_tn(a, b):
    return lax.dot_general(a, b, (((0,), (0,)), ((), ())), preferred_element_type=F32)


def _dot_hi(a, b):
    return jnp.dot(a, b, preferred_element_type=F32, precision=HI)


def _dot_nt_hi(a, b):
    return lax.dot_general(a, b, (((1,), (1,)), ((), ())), preferred_element_type=F32, precision=HI)


def _dot_tn_hi(a, b):
    return lax.dot_general(a, b, (((0,), (0,)), ((), ())), preferred_element_type=F32, precision=HI)


def _sigmoid(x):
    return 1.0 / (1.0 + jnp.exp(-x))


def _softplus(x):
    return jnp.maximum(x, 0.0) + jnp.log(1.0 + jnp.exp(-jnp.abs(x)))


def _gelu(x):
    return 0.5 * x * (1.0 + jnp.tanh(GELU_C * (x + GELU_K * x * x * x)))


def _gelu_grad(x):
    th = jnp.tanh(GELU_C * (x + GELU_K * x * x * x))
    return 0.5 * (1.0 + th) + 0.5 * x * (1.0 - th * th) * GELU_C * (1.0 + 3.0 * GELU_K * x * x)


def rms_matmul(x, g, w, splits, *, tm, name, comm=None):
    t, d = x.shape
    stacked = w.ndim == 3
    n = w.shape[0] * w.shape[2] if stacked else w.shape[1]
    assert sum(splits) == n and (len(splits) == 1 or not stacked)

    def body(x_ref, g_ref, w_ref, h_ref, *o_refs):
        xv = x_ref[...]
        r = lax.rsqrt(jnp.mean(xv * xv, axis=-1, keepdims=True) + EPS)
        h = (xv * r * g_ref[...]).astype(BF16)
        h_ref[...] = h
        if stacked:
            wk = w.shape[2]
            for k in range(w.shape[0]):
                o_refs[0][:, k * wk:(k + 1) * wk] = _dot(h, w_ref[k])
            return
        off = 0
        for o_ref, width in zip(o_refs, splits):
            o_ref[...] = _dot(h, w_ref[:, off:off + width])
            off += width

    tile = lambda c: pl.BlockSpec((tm, c), lambda i: (i, 0))
    return _tiled_call(
        body, name=name, grid=(t // tm,),
        in_specs=[tile(d), _full((1, d)), _full(w.shape)],
        out_specs=[tile(d)] + [tile(c) for c in splits],
        out_shape=[jax.ShapeDtypeStruct((t, d), BF16)] + [jax.ShapeDtypeStruct((t, c), F32) for c in splits],
        operands=(x, g.reshape(1, d), w), comm=comm)


def _pick(n, cap):
    best = LANES
    for c in range(LANES, cap + 1, LANES):
        if n % c == 0:
            best = c
    return best


def matmul_tn(a, b, *, name, out_dtype=F32, col_shards=None, tk=1024):
    t, m = a.shape
    n = b.shape[1]
    tk = min(tk, t)
    tm = _pick(m, 1024)
    tn = n // col_shards if col_shards else _pick(n, 1280)
    nk = t // tk

    def body(a_ref, b_ref, o_ref, acc):
        k = pl.program_id(2)

        @pl.when(k == 0)
        def _():
            acc[...] = jnp.zeros_like(acc)

        acc[...] += _dot_tn(a_ref[...], b_ref[...])

        @pl.when(k == nk - 1)
        def _():
            o_ref[...] = acc[...].reshape(o_ref.shape).astype(out_dtype)

    if col_shards:
        out_spec = pl.BlockSpec((1, tm, tn), lambda i, j, k: (j, i, 0))
        out_shape = jax.ShapeDtypeStruct((col_shards, m, tn), out_dtype)
    else:
        out_spec = pl.BlockSpec((tm, tn), lambda i, j, k: (i, j))
        out_shape = jax.ShapeDtypeStruct((m, n), out_dtype)
    return pl.pallas_call(
        body, name=name, grid=(m // tm, n // tn, nk),
        in_specs=[pl.BlockSpec((tk, tm), lambda i, j, k: (k, i)), pl.BlockSpec((tk, tn), lambda i, j, k: (k, j))],
        out_specs=out_spec, out_shape=out_shape,
        scratch_shapes=[pltpu.VMEM((tm, tn), F32)],
        compiler_params=_params("parallel", "parallel", "arbitrary"),
    )(a, b)


def matmul_nt_rms_bwd(dys, w, x, g, dres, *, tm, name, comm=None):
    t = x.shape[0]
    stacked = w.ndim == 3
    d = w.shape[1] if stacked else w.shape[0]
    widths = [dy.shape[1] for dy in dys]
    n_dy = len(dys)

    def body(*refs):
        dy_refs = refs[:n_dy]
        w_ref, x_ref, g_ref, dres_ref, dx_ref, dxb_ref, dg_ref = refs[n_dy:]
        i = pl.program_id(0)
        if stacked:
            wk = w.shape[2]
            dh = _dot_nt(dy_refs[0][:, 0:wk], w_ref[0])
            for k in range(1, w.shape[0]):
                dh = dh + _dot_nt(dy_refs[0][:, k * wk:(k + 1) * wk], w_ref[k])
        else:
            dh, off = None, 0
            for dy_ref, width in zip(dy_refs, widths):
                part = _dot_nt(dy_ref[...], w_ref[:, off:off + width])
                dh = part if dh is None else dh + part
                off += width
        xv = x_ref[...]
        r = lax.rsqrt(jnp.mean(xv * xv, axis=-1, keepdims=True) + EPS)
        xn = xv * r
        dxn = dh * g_ref[...]
        dx = dres_ref[...] + r * (dxn - xn * jnp.mean(dxn * xn, axis=-1, keepdims=True))
        dx_ref[...] = dx
        dxb_ref[...] = dx.astype(BF16)

        @pl.when(i == 0)
        def _():
            dg_ref[...] = jnp.zeros_like(dg_ref)

        dg_ref[...] += jnp.sum(dh * xn, axis=0, keepdims=True)

    tile = lambda c: pl.BlockSpec((tm, c), lambda i: (i, 0))
    return _tiled_call(
        body, name=name, grid=(t // tm,),
        in_specs=[tile(c) for c in widths] + [_full(w.shape), tile(d), _full((1, d)), tile(d)],
        out_specs=[tile(d), tile(d), _full((1, d))],
        out_shape=[jax.ShapeDtypeStruct((t, d), F32), jax.ShapeDtypeStruct((t, d), BF16),
                   jax.ShapeDtypeStruct((1, d), F32)],
        operands=(*dys, w, x, g.reshape(1, d), dres), comm=comm)


def _conv_taps(prev8, cur, w_ref, tm):
    ext = jnp.concatenate([prev8, cur], axis=0)
    acc = cur * w_ref[3:4, :]
    for k in range(3):
        off = SUBLANES - 3 + k
        acc = acc + ext[off:off + tm, :] * w_ref[k:k + 1, :]
    return acc


def conv_silu_fwd(u, w8, b, *, tm, name):
    t, c = u.shape
    per = tm // SUBLANES

    def body(prev_ref, cur_ref, w_ref, b_ref, o_ref):
        i = pl.program_id(0)
        prev8 = jnp.where(i == 0, 0.0, prev_ref[...])
        xc = _conv_taps(prev8, cur_ref[...], w_ref, tm) + b_ref[...]
        o_ref[...] = xc * _sigmoid(xc)

    return pl.pallas_call(
        body, name=name, grid=(t // tm,),
        in_specs=[pl.BlockSpec((SUBLANES, c), lambda i: (jnp.maximum(i * per - 1, 0), 0)),
                  pl.BlockSpec((tm, c), lambda i: (i, 0)), _full((SUBLANES, c)), _full((1, c))],
        out_specs=pl.BlockSpec((tm, c), lambda i: (i, 0)),
        out_shape=jax.ShapeDtypeStruct((t, c), F32),
        compiler_params=_params("parallel"),
    )(u, u, w8, b.reshape(1, c))


def conv_silu_bwd_a(u, dy, w8, b, *, tm, name):
    t, c = u.shape
    per = tm // SUBLANES

    def body(prev_ref, cur_ref, dy_ref, w_ref, b_ref, dxc_ref, db_ref, dw_ref):
        i = pl.program_id(0)
        prev8 = jnp.where(i == 0, 0.0, prev_ref[...])
        cur = cur_ref[...]
        xc = _conv_taps(prev8, cur, w_ref, tm) + b_ref[...]
        sg = _sigmoid(xc)
        dxc = dy_ref[...] * (sg * (1.0 + xc * (1.0 - sg)))
        dxc_ref[...] = dxc

        @pl.when(i == 0)
        def _():
            db_ref[...] = jnp.zeros_like(db_ref)
            dw_ref[...] = jnp.zeros_like(dw_ref)

        db_ref[...] += jnp.sum(dxc, axis=0, keepdims=True)
        ext = jnp.concatenate([prev8, cur], axis=0)
        rows = []
        for k in range(SSD_CONV):
            off = SUBLANES - 3 + k
            rows.append(jnp.sum(dxc * ext[off:off + tm, :], axis=0, keepdims=True))
        rows.append(jnp.zeros((SUBLANES - SSD_CONV, c), F32))
        dw_ref[...] += jnp.concatenate(rows, axis=0)

    return pl.pallas_call(
        body, name=name, grid=(t // tm,),
        in_specs=[pl.BlockSpec((SUBLANES, c), lambda i: (jnp.maximum(i * per - 1, 0), 0)),
                  pl.BlockSpec((tm, c), lambda i: (i, 0)), pl.BlockSpec((tm, c), lambda i: (i, 0)),
                  _full((SUBLANES, c)), _full((1, c))],
        out_specs=[pl.BlockSpec((tm, c), lambda i: (i, 0)), _full((1, c)), _full((SUBLANES, c))],
        out_shape=[jax.ShapeDtypeStruct((t, c), F32), jax.ShapeDtypeStruct((1, c), F32),
                   jax.ShapeDtypeStruct((SUBLANES, c), F32)],
        compiler_params=_params("arbitrary"),
    )(u, u, dy, w8, b.reshape(1, c))


def conv_bwd_b(dxc, w8, *, tm, name):
    t, c = dxc.shape
    per = tm // SUBLANES
    last = t // SUBLANES - 1
    nt = t // tm

    def body(cur_ref, next_ref, w_ref, o_ref):
        i = pl.program_id(0)
        nxt = jnp.where(i == nt - 1, 0.0, next_ref[...])
        cur = cur_ref[...]
        ext = jnp.concatenate([cur, nxt], axis=0)
        acc = cur * w_ref[3:4, :]
        for j in range(1, SSD_CONV):
            acc = acc + ext[j:j + tm, :] * w_ref[3 - j:4 - j, :]
        o_ref[...] = acc.astype(BF16)

    return pl.pallas_call(
        body, name=name, grid=(nt,),
        in_specs=[pl.BlockSpec((tm, c), lambda i: (i, 0)),
                  pl.BlockSpec((SUBLANES, c), lambda i: (jnp.minimum((i + 1) * per, last), 0)), _full((SUBLANES, c))],
        out_specs=pl.BlockSpec((tm, c), lambda i: (i, 0)),
        out_shape=jax.ShapeDtypeStruct((t, c), BF16),
        compiler_params=_params("parallel"),
    )(dxc, dxc, w8)


def _ssd_chunk_terms(dtr, bias_row, alog_row):
    q = SSD_CHUNK
    row = lax.broadcasted_iota(jnp.int32, (q, q), 0)
    col = lax.broadcasted_iota(jnp.int32, (q, q), 1)
    ltri = (col <= row).astype(F32)
    dt = _softplus(dtr + bias_row)
    a_row = -jnp.exp(alog_row)
    la = dt * a_row
    cum = _dot_hi(ltri, la)
    cum_t = _dot_tn_hi(la, (row <= col).astype(F32))
    return dt, a_row, cum, cum_t, ltri, row, col


def _head_maps():
    di = SSD_HEADS * SSD_HEADDIM
    expand = (lax.broadcasted_iota(jnp.int32, (LANES, di), 1) // SSD_HEADDIM
              == lax.broadcasted_iota(jnp.int32, (LANES, di), 0)).astype(F32)
    collapse = (lax.broadcasted_iota(jnp.int32, (di, LANES), 0) // SSD_HEADDIM
                == lax.broadcasted_iota(jnp.int32, (di, LANES), 1)).astype(F32)
    return expand, collapse


def _ssd_decay(cum, cum_t, h, causal):
    seg = cum[:, h:h + 1] - cum_t[h:h + 1, :]
    return jnp.where(causal, jnp.exp(jnp.where(causal, seg, 0.0)), 0.0)


def _pair_block_diag(x2):
    low = lax.broadcasted_iota(jnp.int32, x2.shape, 1) < SSD_HEADDIM
    zero = jnp.zeros_like(x2)
    return jnp.concatenate([jnp.where(low, x2, zero), jnp.where(low, zero, x2)], axis=0)


def ssd_fwd(xbc, dtr, z, bias_row, alog_row, d_row, norm_g, *, name, comm=None):
    t = xbc.shape[0]
    q, p, n = SSD_CHUNK, SSD_HEADDIM, SSD_STATE
    nc = t // q
    di = SSD_HEADS * p
    gw = di // SSD_GROUPS

    def body(xs_ref, b_ref, c_ref, dtr_ref, z_ref, bias_ref, alog_ref, d_ref, g_ref,
             ya_ref, yssd_ref, st_ref, state):
        ci = pl.program_id(0)

        @pl.when(ci == 0)
        def _():
            state[...] = jnp.zeros_like(state)

        st_ref[0] = state[...]
        dt, _, cum, cum_t, _, row, col = _ssd_chunk_terms(dtr_ref[...], bias_ref[...], alog_ref[...])
        causal = row >= col
        expand, _ = _head_maps()
        dt_full = _dot_hi(dt, expand)
        cum_full = _dot_hi(cum, expand)
        last_full = cum_full[q - 1:q, :]
        xs = xs_ref[...]
        xdt = xs * dt_full
        xd = (xdt * jnp.exp(last_full - cum_full)).astype(BF16)
        xdt_b = xdt.astype(BF16)
        e_full = jnp.exp(cum_full)
        eend_full = jnp.exp(last_full)
        for g in range(SSD_GROUPS):
            slab = slice(g * gw, (g + 1) * gw)
            bg = b_ref[:, g * n:(g + 1) * n].astype(BF16)
            cg = c_ref[:, g * n:(g + 1) * n].astype(BF16)
            cb = _dot_nt(cg, bg)
            st_g = state[:, slab]
            pairs = []
            for pr in range(SSD_HPG // 2):
                h0 = g * SSD_HPG + 2 * pr
                m0 = (cb * _ssd_decay(cum, cum_t, h0, causal)).astype(BF16)
                m1 = (cb * _ssd_decay(cum, cum_t, h0 + 1, causal)).astype(BF16)
                pairs.append(_dot(jnp.concatenate([m0, m1], axis=1), _pair_block_diag(xdt_b[:, h0 * p:(h0 + 2) * p])))
            y = jnp.concatenate(pairs, axis=1) + e_full[:, slab] * _dot(cg, st_g.astype(BF16))
            yssd_ref[:, slab] = y + d_ref[:, slab] * xs[:, slab]
            state[:, slab] = eend_full[:, slab] * st_g + _dot_tn(bg, xd[:, slab])
        zv = z_ref[...]
        ya1 = yssd_ref[...] * (zv * _sigmoid(zv))
        for g in range(SSD_GROUPS):
            gsl = slice(g * gw, (g + 1) * gw)
            sl = ya1[:, gsl]
            rg = lax.rsqrt(jnp.mean(sl * sl, axis=-1, keepdims=True) + EPS)
            ya_ref[:, gsl] = (sl * rg * g_ref[:, gsl]).astype(BF16)

    blk = lambda w, j: pl.BlockSpec((q, w), lambda c: (c, j))
    return _tiled_call(
        body, name=name, grid=(nc,),
        in_specs=[blk(di, 0), blk(512, 2), blk(512, 3), blk(LANES, 0), blk(di, 0),
                  _full((1, LANES)), _full((1, LANES)), _full((1, di)), _full((1, di))],
        out_specs=[blk(di, 0), blk(di, 0), pl.BlockSpec((1, n, di), lambda c: (c, 0, 0))],
        out_shape=[jax.ShapeDtypeStruct((t, di), BF16), jax.ShapeDtypeStruct((t, di), F32),
                   jax.ShapeDtypeStruct((nc, n, di), F32)],
        scratch_shapes=[pltpu.VMEM((n, di), F32)],
        operands=(xbc, xbc, xbc, dtr, z, bias_row, alog_row, d_row, norm_g.reshape(1, di)), comm=comm)


def ssd_bwd(dya, yssd, z, xbc, dtr, states, bias_row, alog_row, d_row, norm_g, *, name, comm=None):
    t = xbc.shape[0]
    q, p, n = SSD_CHUNK, SSD_HEADDIM, SSD_STATE
    nc = t // q
    di = SSD_HEADS * p
    gw = di // SSD_GROUPS

    def body(dya_ref, yssd_ref, z_ref, xs_ref, b_ref, c_ref, dtr_ref, st_ref, bias_ref, alog_ref, d_ref, g_ref,
             dz_ref, dxbc_ref, ddtr_ref, dng_ref, dd_ref, dalog_ref, dbias_ref, dstate, dyssd):
        ci = pl.program_id(0)

        @pl.when(ci == 0)
        def _():
            dstate[...] = jnp.zeros_like(dstate)
            dng_ref[...] = jnp.zeros_like(dng_ref)
            dd_ref[...] = jnp.zeros_like(dd_ref)
            dalog_ref[...] = jnp.zeros_like(dalog_ref)
            dbias_ref[...] = jnp.zeros_like(dbias_ref)

        zv = z_ref[...]
        sz = _sigmoid(zv)
        silu_z = zv * sz
        yv = yssd_ref[...]
        ya1 = yv * silu_z
        dyav = dya_ref[...]
        for g in range(SSD_GROUPS):
            gsl = slice(g * gw, (g + 1) * gw)
            sl = ya1[:, gsl]
            rg = lax.rsqrt(jnp.mean(sl * sl, axis=-1, keepdims=True) + EPS)
            ya2 = sl * rg
            dy_g = dyav[:, gsl]
            dng_ref[:, gsl] += jnp.sum(dy_g * ya2, axis=0, keepdims=True)
            dya2 = dy_g * g_ref[:, gsl]
            dya1 = rg * (dya2 - ya2 * jnp.mean(dya2 * ya2, axis=-1, keepdims=True))
            dyssd[:, gsl] = dya1 * silu_z[:, gsl]
            dz_ref[:, gsl] = (dya1 * yv[:, gsl] * (sz[:, gsl] * (1.0 + zv[:, gsl] * (1.0 - sz[:, gsl])))).astype(BF16)

        dtr_v = dtr_ref[...]
        dt, a_row, cum, cum_t, ltri, row, col = _ssd_chunk_terms(dtr_v, bias_ref[...], alog_ref[...])
        causal = row >= col
        expand, collapse = _head_maps()
        lane = lax.broadcasted_iota(jnp.int32, (1, LANES), 1)
        sub = lax.broadcasted_iota(jnp.int32, (LANES, 1), 0)
        is_last = lax.broadcasted_iota(jnp.int32, (q, 1), 0) == q - 1
        low = lax.broadcasted_iota(jnp.int32, (q, 2 * p), 1) < p
        dt_full = _dot_hi(dt, expand)
        cum_full = _dot_hi(cum, expand)
        last_full = cum_full[q - 1:q, :]
        e_full = jnp.exp(cum_full)
        eend_full = jnp.exp(last_full)
        dend_full = jnp.exp(last_full - cum_full)
        xs = xs_ref[...]
        xdt = xs * dt_full
        xdt_b = xdt.astype(BF16)
        xd_b = (xdt * dend_full).astype(BF16)
        dy_all = dyssd[...]
        dy_b = dy_all.astype(BF16)
        dg_b = (dy_all * e_full).astype(BF16)
        dcum_mat = jnp.zeros((q, LANES), F32)
        rmat = jnp.zeros((LANES, q), F32)
        yoff_prod, bds_list, dx_list, dlast_parts = [], [], [], []
        for g in range(SSD_GROUPS):
            slab = slice(g * gw, (g + 1) * gw)
            bg = b_ref[:, g * n:(g + 1) * n].astype(BF16)
            cg = c_ref[:, g * n:(g + 1) * n].astype(BF16)
            cb = _dot_nt(cg, bg)
            st_g = st_ref[0, :, slab]
            st_b = st_g.astype(BF16)
            dsn = dstate[:, slab]
            dsn_b = dsn.astype(BF16)
            yoff_prod.append(dy_all[:, slab] * e_full[:, slab] * _dot(cg, st_b))
            dcg = _dot_nt(dg_b[:, slab], st_b)
            dstate[:, slab] = _dot_tn(cg, dg_b[:, slab]) + eend_full[:, slab] * dsn
            bds_list.append(_dot(bg, dsn_b))
            dbg = _dot_nt(xd_b[:, slab], dsn_b)
            dlast_parts.append(jnp.sum(dsn * st_g, axis=0, keepdims=True))
            dcb = jnp.zeros((q, q), F32)
            dx_pairs = []
            for pr in range(SSD_HPG // 2):
                h0 = g * SSD_HPG + 2 * pr
                ps = slice(h0 * p, (h0 + 2) * p)
                l0 = _ssd_decay(cum, cum_t, h0, causal)
                l1 = _ssd_decay(cum, cum_t, h0 + 1, causal)
                m0, m1 = cb * l0, cb * l1
                dyp = dy_b[:, ps]
                zero = jnp.zeros_like(dyp)
                dy0, dy1 = jnp.where(low, dyp, zero), jnp.where(low, zero, dyp)
                dm0 = _dot_nt(dy0, xdt_b[:, ps])
                dm1 = _dot_nt(dy1, xdt_b[:, ps])
                w0, w1 = dm0 * m0, dm1 * m1
                dcum_mat = (dcum_mat + jnp.sum(w0, axis=1, keepdims=True) * (lane == h0).astype(F32)
                            + jnp.sum(w1, axis=1, keepdims=True) * (lane == h0 + 1).astype(F32))
                rmat = (rmat + jnp.where(sub == h0, jnp.sum(w0, axis=0, keepdims=True), 0.0)
                        + jnp.where(sub == h0 + 1, jnp.sum(w1, axis=0, keepdims=True), 0.0))
                dcb = dcb + dm0 * l0 + dm1 * l1
                dx_pairs.append(_dot_tn(jnp.concatenate([m0.astype(BF16), m1.astype(BF16)], axis=0),
                                        jnp.concatenate([dy0, dy1], axis=0)))
            dx_list.append(jnp.concatenate(dx_pairs, axis=1))
            dcb_b = dcb.astype(BF16)
            dxbc_ref[:, di + g * n:di + (g + 1) * n] = dbg + _dot_tn(dcb_b, cg)
            dxbc_ref[:, di + SSD_GROUPS * n + g * n:di + SSD_GROUPS * n + (g + 1) * n] = dcg + _dot(dcb_b, bg)
        bds_all = jnp.concatenate(bds_list, axis=1)
        dx_all = jnp.concatenate(dx_list, axis=1) + dend_full * bds_all
        last_row = cum[q - 1:q, :]
        qv = _dot_hi(xdt * bds_all, collapse) * jnp.exp(last_row - cum)
        dcum_mat = dcum_mat + _dot_hi(jnp.concatenate(yoff_prod, axis=1), collapse) - qv
        dlast_full = jnp.broadcast_to(jnp.concatenate(dlast_parts, axis=1), (SUBLANES, di))
        dlast_row = _dot_hi(dlast_full, collapse)[0:1, :] * jnp.exp(last_row) + jnp.sum(qv, axis=0, keepdims=True)
        dcum_mat = dcum_mat + jnp.where(is_last, dlast_row, 0.0)
        dla = _dot_tn_hi(ltri, dcum_mat) - _dot_nt_hi((row <= col).astype(F32), rmat)
        ddt_mat = _dot_hi(dx_all * xs, collapse) + dla * a_row
        dxbc_ref[:, 0:di] = d_ref[...] * dy_all + dx_all * dt_full
        ddtr = ddt_mat * _sigmoid(dtr_v + bias_ref[...])
        ddtr_ref[...] = ddtr.astype(BF16)
        dd_full = jnp.broadcast_to(jnp.sum(dy_all * xs, axis=0, keepdims=True), (SUBLANES, di))
        dd_ref[...] += _dot_hi(dd_full, collapse)[0:1, :]
        dalog_ref[...] += jnp.sum(dla * dt, axis=0, keepdims=True) * a_row
        dbias_ref[...] += jnp.sum(ddtr, axis=0, keepdims=True)

    blk = lambda w, j: pl.BlockSpec((q, w), lambda c: (nc - 1 - c, j))
    row_out = lambda w: jax.ShapeDtypeStruct((1, w), F32)
    return _tiled_call(
        body, name=name, grid=(nc,),
        in_specs=[blk(di, 0), blk(di, 0), blk(di, 0), blk(di, 0), blk(512, 2), blk(512, 3), blk(LANES, 0),
                  pl.BlockSpec((1, n, di), lambda c: (nc - 1 - c, 0, 0)),
                  _full((1, LANES)), _full((1, LANES)), _full((1, di)), _full((1, di))],
        out_specs=[blk(di, 0), blk(CONV_DIM, 0), blk(LANES, 0),
                   _full((1, di)), _full((1, LANES)), _full((1, LANES)), _full((1, LANES))],
        out_shape=[jax.ShapeDtypeStruct((t, di), BF16), jax.ShapeDtypeStruct((t, CONV_DIM), F32),
                   jax.ShapeDtypeStruct((t, LANES), BF16), row_out(di), row_out(LANES), row_out(LANES), row_out(LANES)],
        scratch_shapes=[pltpu.VMEM((n, di), F32), pltpu.VMEM((q, di), F32)],
        operands=(dya, yssd, z, xbc, xbc, xbc, dtr, states, bias_row, alog_row, d_row, norm_g.reshape(1, di)),
        comm=comm)


S5_BLOCKS = 4
S5_BC = S5_WIDTH // S5_BLOCKS
S5_BS = S5_N // S5_BLOCKS


def _s5_tables(ar, ai, reverse):
    pows = [(ar, ai)]
    for _ in range(SUBLANES - 1):
        pr, pi = pows[-1]
        pows.append((pr * ar - pi * ai, pr * ai + pi * ar))
    row = lax.broadcasted_iota(jnp.int32, (SUBLANES, ar.shape[1]), 0)
    out = []
    for k in (1, 2, 4):
        pr, pi = pows[k - 1]
        mask = (row < SUBLANES - k) if reverse else (row >= k)
        out += [jnp.where(mask, pr, 0.0), jnp.where(mask, pi, 0.0)]
    order = list(range(SUBLANES))
    if reverse:
        order = order[::-1]
    out.append(jnp.concatenate([pows[e][0] for e in order], axis=0))
    out.append(jnp.concatenate([pows[e][1] for e in order], axis=0))
    return out


def s5_interleave(a, tm):
    t, c = a.shape
    return a.reshape(t // tm, SUBLANES, tm // SUBLANES, c).transpose(0, 2, 1, 3).reshape(t, c)


def s5_deinterleave(a, tm):
    t, c = a.shape
    return a.reshape(t // tm, tm // SUBLANES, SUBLANES, c).transpose(0, 2, 1, 3).reshape(t, c)


def _s5_scan_setup(ab_ref, base, tab, seg, reverse):
    ar = ab_ref[0:1, :]
    ai = -ab_ref[1:2, :] if reverse else ab_ref[1:2, :]
    base[0:1, :] = ar
    base[1:2, :] = ai
    assert seg & (seg - 1) == 0
    pr, pi = ar, ai
    for _ in range(seg.bit_length() - 1):
        pr, pi = pr * pr - pi * pi, 2.0 * pr * pi
    for k, tv in enumerate(_s5_tables(pr, pi, reverse)):
        tab[k] = tv


def _s5_scan(s_scr, base, tab, carry, tm, reverse):
    seg = tm // SUBLANES
    width = S5_BS
    first = lax.broadcasted_iota(jnp.int32, (SUBLANES, width), 0) == (SUBLANES - 1 if reverse else 0)

    def rows(i):
        step = (seg - 1 - i) if reverse else i
        return pl.ds(pl.multiple_of(step * SUBLANES, SUBLANES), SUBLANES)

    for lc in range(S5_N // width):
        lr = slice(lc * width, (lc + 1) * width)
        li = slice(S5_N + lc * width, S5_N + (lc + 1) * width)
        ar = jnp.broadcast_to(base[0:1, lr], (SUBLANES, width))
        ai = jnp.broadcast_to(base[1:2, lr], (SUBLANES, width))

        def advance(i, x, lr=lr, li=li, ar=ar, ai=ai):
            xr, xi = x
            return ar * xr - ai * xi + s_scr[rows(i), lr], ar * xi + ai * xr + s_scr[rows(i), li]

        zero = jnp.zeros((SUBLANES, width), F32)
        fr, fi = lax.fori_loop(0, seg, advance, (zero, zero))

        tb = [tab[k, :, lr] for k in range(8)]
        for lvl, k in enumerate((1, 2, 4)):
            sh = (SUBLANES - k) if reverse else k
            pr, pi = tb[2 * lvl], tb[2 * lvl + 1]
            rr, ri = pltpu.roll(fr, sh, 0), pltpu.roll(fi, sh, 0)
            fr, fi = fr + pr * rr - pi * ri, fi + pr * ri + pi * rr
        cr, ci = carry[0:1, lr], carry[0:1, li]
        fr, fi = fr + tb[6] * cr - tb[7] * ci, fi + tb[6] * ci + tb[7] * cr
        last = 0 if reverse else SUBLANES - 1
        carry[0:1, lr] = fr[last:last + 1, :]
        carry[0:1, li] = fi[last:last + 1, :]
        sh = (SUBLANES - 1) if reverse else 1
        entering = (jnp.where(first, cr, pltpu.roll(fr, sh, 0)), jnp.where(first, ci, pltpu.roll(fi, sh, 0)))

        def rerun(i, x, lr=lr, li=li, advance=advance):
            xr, xi = advance(i, x)
            s_scr[rows(i), lr] = xr
            s_scr[rows(i), li] = xi
            return xr, xi

        lax.fori_loop(0, seg, rerun, entering)


def _s5_scratch(tm):
    return [pltpu.VMEM((tm, 2 * S5_N), F32), pltpu.VMEM((SUBLANES, S5_N), F32),
            pltpu.VMEM((8, SUBLANES, S5_N), F32), pltpu.VMEM((SUBLANES, 2 * S5_N), F32)]


def _s5_put(scr, j, val):
    scr[:, j * S5_BS:(j + 1) * S5_BS] = val[:, :S5_BS]
    scr[:, S5_N + j * S5_BS:S5_N + (j + 1) * S5_BS] = val[:, S5_BS:]


def _s5_get(scr, j):
    return jnp.concatenate([scr[:, j * S5_BS:(j + 1) * S5_BS], scr[:, S5_N + j * S5_BS:S5_N + (j + 1) * S5_BS]], axis=1)


def s5_fwd(u5, bbc, ccc, ab8, d_row, wglu, bglu_row, *, tm, name, comm=None):
    t, w = u5.shape

    def body(u_ref, bb_ref, cc_ref, ab_ref, d_ref, wg_ref, bg_ref, s_ref, ys_ref, yb_ref, s_scr, base, tab, carry):
        i = pl.program_id(0)

        @pl.when(i == 0)
        def _():
            carry[...] = jnp.zeros_like(carry)
            _s5_scan_setup(ab_ref, base, tab, tm // SUBLANES, False)

        u = u_ref[...]
        ub = u.astype(BF16)
        for j in range(S5_BLOCKS):
            uj = ub[:, j * S5_BC:(j + 1) * S5_BC]
            _s5_put(s_scr, j, _dot(uj, bb_ref[j * S5_BC:(j + 1) * S5_BC, :]))
        _s5_scan(s_scr, base, tab, carry, tm, False)
        s_ref[...] = s_scr[...]
        ys = []
        for j in range(S5_BLOCKS):
            ys.append(_dot(_s5_get(s_scr, j).astype(BF16), cc_ref[:, j * S5_BC:(j + 1) * S5_BC]))
        y = jnp.concatenate(ys, axis=1) + d_ref[...] * u
        ys_ref[...] = y
        yb1 = _gelu(y)
        tt = _dot(yb1.astype(BF16), wg_ref[...]) + bg_ref[...]
        yb_ref[...] = (yb1 * _sigmoid(tt)).astype(BF16)

    tile = lambda c: pl.BlockSpec((tm, c), lambda i: (i, 0))
    return _tiled_call(
        body, name=name, grid=(t // tm,),
        in_specs=[tile(w), _full((w, 2 * S5_BS)), _full((2 * S5_BS, w)), _full((SUBLANES, S5_N)), _full((1, w)),
                  _full((w, w)), _full((1, w))],
        out_specs=[tile(2 * S5_N), tile(w), tile(w)],
        out_shape=[jax.ShapeDtypeStruct((t, 2 * S5_N), F32), jax.ShapeDtypeStruct((t, w), F32),
                   jax.ShapeDtypeStruct((t, w), BF16)],
        scratch_shapes=_s5_scratch(tm),
        operands=(u5, bbc, ccc, ab8, d_row, wglu, bglu_row), comm=comm)


def s5_bwd(dyb, ys5, u5, s, bbc, ccc, ab8, d_row, wglu, bglu_row, *, tm, name, comm=None):
    t, w = u5.shape
    nt = t // tm
    per = tm // SUBLANES

    def body(dyb_ref, ys_ref, u_ref, s_ref, sprev_ref, bb_ref, cc_ref, ab_ref, d_ref, wg_ref, bg_ref,
             du_ref, dbb_ref, dcc_ref, dab_ref, dd_ref, dwg_ref, dbg_ref, g_scr, base, tab, carry):
        i = pl.program_id(0)

        @pl.when(i == 0)
        def _():
            carry[...] = jnp.zeros_like(carry)
            _s5_scan_setup(ab_ref, base, tab, tm // SUBLANES, True)
            for r in (dbb_ref, dcc_ref, dab_ref, dd_ref, dwg_ref, dbg_ref):
                r[...] = jnp.zeros_like(r)

        y = ys_ref[...]
        u = u_ref[...]
        yb1 = _gelu(y)
        yb1b = yb1.astype(BF16)
        sg = _sigmoid(_dot(yb1b, wg_ref[...]) + bg_ref[...])
        dybv = dyb_ref[...]
        dtt = dybv * yb1 * sg * (1.0 - sg)
        dttb = dtt.astype(BF16)
        dyb1 = dybv * sg + _dot_nt(dttb, wg_ref[...])
        dwg_ref[...] += _dot_tn(yb1b, dttb)
        dbg_ref[...] += jnp.sum(dtt, axis=0, keepdims=True)
        dy = dyb1 * _gelu_grad(y)
        dd_ref[...] += jnp.sum(dy * u, axis=0, keepdims=True)
        dyh = dy.astype(BF16)
        for j in range(S5_BLOCKS):
            cs = slice(j * S5_BC, (j + 1) * S5_BC)
            _s5_put(g_scr, j, _dot_nt(dyh[:, cs], cc_ref[:, cs]))
        _s5_scan(g_scr, base, tab, carry, tm, True)
        ub = u.astype(BF16)
        dus = []
        for j in range(S5_BLOCKS):
            cs = slice(j * S5_BC, (j + 1) * S5_BC)
            gb = _s5_get(g_scr, j).astype(BF16)
            dus.append(_dot_nt(gb, bb_ref[cs, :]))
            dbb_ref[cs, :] += _dot_tn(ub[:, cs], gb)
            dcc_ref[:, cs] += _dot_tn(_s5_get(s_ref, j).astype(BF16), dyh[:, cs])
        du_ref[...] = (jnp.concatenate(dus, axis=1) + d_ref[...] * dy).astype(BF16)
        top = lax.broadcasted_iota(jnp.int32, (SUBLANES, S5_BS), 0) == 0

        def corr(g_sl, s_sl):
            before = jnp.where(i == nt - 1, 0.0, sprev_ref[SUBLANES - 1:SUBLANES, s_sl])
            wrap = jnp.where(top, before, pltpu.roll(s_ref[tm - SUBLANES:tm, s_sl], 1, 0))
            return (jnp.sum(g_scr[SUBLANES:tm, g_sl] * s_ref[0:tm - SUBLANES, s_sl], axis=0, keepdims=True)
                    + jnp.sum(g_scr[0:SUBLANES, g_sl] * wrap, axis=0, keepdims=True))

        for j in range(S5_BLOCKS):
            lr = slice(j * S5_BS, (j + 1) * S5_BS)
            li = slice(S5_N + j * S5_BS, S5_N + (j + 1) * S5_BS)
            dab_ref[0:1, lr] += corr(lr, lr) + corr(li, li)
            dab_ref[1:2, lr] += corr(li, lr) - corr(lr, li)

    tile = lambda c: pl.BlockSpec((tm, c), lambda i: (nt - 1 - i, 0))
    acc = lambda r, c: jax.ShapeDtypeStruct((r, c), F32)
    return _tiled_call(
        body, name=name, grid=(nt,),
        in_specs=[tile(w), tile(w), tile(w), tile(2 * S5_N),
                  pl.BlockSpec((SUBLANES, 2 * S5_N), lambda i: (jnp.maximum((nt - 1 - i) * per - 1, 0), 0)),
                  _full((w, 2 * S5_BS)), _full((2 * S5_BS, w)), _full((SUBLANES, S5_N)), _full((1, w)),
                  _full((w, w)), _full((1, w))],
        out_specs=[tile(w), _full((w, 2 * S5_BS)), _full((2 * S5_BS, w)), _full((SUBLANES, S5_N)), _full((1, w)),
                   _full((w, w)), _full((1, w))],
        out_shape=[jax.ShapeDtypeStruct((t, w), BF16), acc(w, 2 * S5_BS), acc(2 * S5_BS, w), acc(SUBLANES, S5_N),
                   acc(1, w), acc(w, w), acc(1, w)],
        scratch_shapes=_s5_scratch(tm),
        operands=(dyb, ys5, u5, s, s, bbc, ccc, ab8, d_row, wglu, bglu_row), comm=comm)


def s5_discretize(a_re, a_im, log_dt, b_re, b_im, *, name):
    n = a_re.shape[0]

    def body(ar_ref, ai_ref, dt_ref, br_ref, bi_ref, ab_ref, bbr_ref, bbi_ref):
        ar, ai, dtv = ar_ref[...], ai_ref[...], jnp.exp(dt_ref[...])
        mag = jnp.exp(ar * dtv)
        abr = mag * jnp.cos(ai * dtv)
        abi = mag * jnp.sin(ai * dtv)
        den = ar * ar + ai * ai
        nr = abr - 1.0
        cr = (nr * ar + abi * ai) / den
        ci = (abi * ar - nr * ai) / den
        ab_ref[:, 0:1] = abr
        ab_ref[:, 1:2] = abi
        br, bi = br_ref[...], bi_ref[...]
        bbr_ref[...] = cr * br - ci * bi
        bbi_ref[...] = cr * bi + ci * br

    col = jax.ShapeDtypeStruct((n, 2), F32)
    mat = jax.ShapeDtypeStruct((n, S5_GROUP), F32)
    return pl.pallas_call(body, name=name, out_shape=[col, mat, mat])(a_re, a_im, log_dt, b_re, b_im)


def s5_discretize_bwd(a_re, a_im, log_dt, b_re, b_im, dab, dbb_re, dbb_im, *, name):
    n = a_re.shape[0]

    def body(ar_ref, ai_ref, dt_ref, br_ref, bi_ref, dab_ref, dbr_ref, dbi_ref, da_ref, ddt_ref, gbr_ref, gbi_ref):
        ar, ai, dtv = ar_ref[...], ai_ref[...], jnp.exp(dt_ref[...])
        mag = jnp.exp(ar * dtv)
        abr = mag * jnp.cos(ai * dtv)
        abi = mag * jnp.sin(ai * dtv)
        den = ar * ar + ai * ai
        nr = abr - 1.0
        cr = (nr * ar + abi * ai) / den
        ci = (abi * ar - nr * ai) / den
        br, bi = br_ref[...], bi_ref[...]
        dbr, dbi = dbr_ref[...], dbi_ref[...]
        gbr_ref[...] = cr * dbr + ci * dbi
        gbi_ref[...] = cr * dbi - ci * dbr
        gcr = jnp.sum(dbr * br + dbi * bi, axis=1, keepdims=True)
        gci = jnp.sum(dbi * br - dbr * bi, axis=1, keepdims=True)
        ilr, ili = ar / den, -ai / den
        gabr = dab_ref[:, 0:1] + (ilr * gcr + ili * gci)
        gabi = dab_ref[:, 1:2] + (ilr * gci - ili * gcr)
        t1r, t1i = dtv * abr, dtv * abi
        t2r, t2i = -(cr * ilr - ci * ili), -(cr * ili + ci * ilr)
        da_ref[:, 0:1] = (t1r * gabr + t1i * gabi) + (t2r * gcr + t2i * gci)
        da_ref[:, 1:2] = (t1r * gabi - t1i * gabr) + (t2r * gci - t2i * gcr)
        lr, li = ar * abr - ai * abi, ar * abi + ai * abr
        dlog = (lr * gabr + li * gabi) * dtv
        ddt_ref[...] = jnp.sum(dlog.reshape(S5_GROUPS, S5_STATE, 1), axis=1)

    col2 = jax.ShapeDtypeStruct((n, 2), F32)
    col1 = jax.ShapeDtypeStruct((S5_GROUPS, 1), F32)
    mat = jax.ShapeDtypeStruct((n, S5_GROUP), F32)
    return pl.pallas_call(body, name=name, out_shape=[col2, col1, mat, mat])(
        a_re, a_im, log_dt, b_re, b_im, dab, dbb_re, dbb_im)


def merge_fwd(ya, yb, graw, x, wba, wbb, wout, *, tm, name):
    t, d = x.shape

    def body(ya_ref, yb_ref, g_ref, x_ref, wba_ref, wbb_ref, wo_ref, x1_ref, pa_ref, pb_ref):
        pa = _dot(ya_ref[...], wba_ref[...])
        pb = _dot(yb_ref[...], wbb_ref[...])
        gate = _sigmoid(g_ref[...])
        merged = gate[:, :d] * pa + gate[:, d:] * pb
        x1_ref[...] = x_ref[...] + _dot(merged.astype(BF16), wo_ref[...])
        pa_ref[...] = pa
        pb_ref[...] = pb

    tile = lambda c: pl.BlockSpec((tm, c), lambda i: (i, 0))
    out = jax.ShapeDtypeStruct((t, d), F32)
    return pl.pallas_call(
        body, name=name, grid=(t // tm,),
        in_specs=[tile(d), tile(S5_WIDTH), tile(2 * d), tile(d), _full((d, d)), _full((S5_WIDTH, d)), _full((d, d))],
        out_specs=[tile(d), tile(d), tile(d)], out_shape=[out, out, out],
        compiler_params=_params("parallel"),
    )(ya, yb, graw, x, wba, wbb, wout)


def merge_bwd(dx1b, graw, pa, pb, wba, wbb, wout, *, tm, name):
    t, d = pa.shape

    def body(dx_ref, g_ref, pa_ref, pb_ref, wba_ref, wbb_ref, wo_ref,
             mg_ref, dg_ref, dpa_ref, dpb_ref, dya_ref, dyb_ref):
        dm = _dot_nt(dx_ref[...], wo_ref[...])
        gate = _sigmoid(g_ref[...])
        g0, g1 = gate[:, :d], gate[:, d:]
        pa, pb = pa_ref[...], pb_ref[...]
        mg_ref[...] = (g0 * pa + g1 * pb).astype(BF16)
        dg_ref[:, :d] = (dm * pa * g0 * (1.0 - g0)).astype(BF16)
        dg_ref[:, d:] = (dm * pb * g1 * (1.0 - g1)).astype(BF16)
        dpa = (dm * g0).astype(BF16)
        dpb = (dm * g1).astype(BF16)
        dpa_ref[...] = dpa
        dpb_ref[...] = dpb
        dya_ref[...] = _dot_nt(dpa, wba_ref[...])
        dyb_ref[...] = _dot_nt(dpb, wbb_ref[...])

    tile = lambda c: pl.BlockSpec((tm, c), lambda i: (i, 0))
    sds = jax.ShapeDtypeStruct
    return pl.pallas_call(
        body, name=name, grid=(t // tm,),
        in_specs=[tile(d), tile(2 * d), tile(d), tile(d), _full((d, d)), _full((S5_WIDTH, d)), _full((d, d))],
        out_specs=[tile(d), tile(2 * d), tile(d), tile(d), tile(d), tile(S5_WIDTH)],
        out_shape=[sds((t, d), BF16), sds((t, 2 * d), BF16), sds((t, d), BF16), sds((t, d), BF16),
                   sds((t, d), F32), sds((t, S5_WIDTH), F32)],
        compiler_params=_params("parallel"),
    )(dx1b, graw, pa, pb, wba, wbb, wout)


def mlp_out_loss(a1, x1, w2, gf, target, *, tm, name):
    t, d = x1.shape
    f = a1.shape[1]

    def body(a_ref, x_ref, w_ref, g_ref, tg_ref, act_ref, dx_ref, dxb_ref, loss_ref, dg_ref):
        i = pl.program_id(0)
        av = jnp.maximum(a_ref[...], 0.0)
        act = (av * av).astype(BF16)
        act_ref[...] = act
        x2 = x_ref[...] + _dot(act, w_ref[...])
        r = lax.rsqrt(jnp.mean(x2 * x2, axis=-1, keepdims=True) + EPS)
        xn = x2 * r
        err = xn * g_ref[...] - tg_ref[...]
        dyf = err * (1.0 / d)
        dxn = dyf * g_ref[...]
        dx = r * (dxn - xn * jnp.mean(dxn * xn, axis=-1, keepdims=True))
        dx_ref[...] = dx
        dxb_ref[...] = dx.astype(BF16)

        @pl.when(i == 0)
        def _():
            loss_ref[...] = jnp.zeros_like(loss_ref)
            dg_ref[...] = jnp.zeros_like(dg_ref)

        loss_ref[...] += jnp.sum(err * err) * (0.5 / d)
        dg_ref[...] += jnp.sum(dyf * xn, axis=0, keepdims=True)

    tile = lambda c: pl.BlockSpec((tm, c), lambda i: (i, 0))
    sds = jax.ShapeDtypeStruct
    return pl.pallas_call(
        body, name=name, grid=(t // tm,),
        in_specs=[tile(f), tile(d), _full((f, d)), _full((1, d)), tile(d)],
        out_specs=[tile(f), tile(d), tile(d), _full((1, LANES)), _full((1, d))],
        out_shape=[sds((t, f), BF16), sds((t, d), F32), sds((t, d), BF16), sds((1, LANES), F32), sds((1, d), F32)],
        compiler_params=_params("arbitrary"),
    )(a1, x1, w2, gf.reshape(1, d), target)


def mlp_bwd_act(dx2b, a1, w2, *, tm, name):
    t, f = a1.shape
    d = dx2b.shape[1]

    def body(dx_ref, a_ref, w_ref, o_ref):
        dact = _dot_nt(dx_ref[...], w_ref[...])
        o_ref[...] = (dact * 2.0 * jnp.maximum(a_ref[...], 0.0)).astype(BF16)

    tile = lambda c: pl.BlockSpec((tm, c), lambda i: (i, 0))
    return pl.pallas_call(
        body, name=name, grid=(t // tm,),
        in_specs=[tile(d), tile(f), _full((f, d))], out_specs=tile(f),
        out_shape=jax.ShapeDtypeStruct((t, f), BF16),
        compiler_params=_params("parallel"),
    )(dx2b, a1, w2)


ADAM_TILE_ELEMS = 128 * 1024


def _row_tile(rows, cap):
    if rows <= cap:
        return rows
    best = None
    for c in range(16, cap + 1, 16):
        if rows % c == 0:
            best = c
    assert best is not None, rows
    return best


def adamw(w, m, v, gparts, *, name):
    rows, cols = w.shape
    tr = _row_tile(rows, max(16, ADAM_TILE_ELEMS // cols))

    def body(w_ref, m_ref, v_ref, g_ref, go_ref, d_ref, mo_ref, vo_ref):
        g = g_ref[0].astype(F32)
        for k in range(1, N_DEV):
            g = g + g_ref[k].astype(F32)
        mn = ADAM_B1 * m_ref[...] + (1.0 - ADAM_B1) * g
        vn = ADAM_B2 * v_ref[...] + (1.0 - ADAM_B2) * (g * g)
        m_hat = mn / (1.0 - ADAM_B1 ** ADAM_STEP)
        v_hat = vn / (1.0 - ADAM_B2 ** ADAM_STEP)
        go_ref[...] = g
        d_ref[...] = -ADAM_LR * (m_hat / (jnp.sqrt(v_hat) + ADAM_EPS) + ADAM_WD * w_ref[...])
        mo_ref[...] = mn
        vo_ref[...] = vn

    tile = pl.BlockSpec((tr, cols), lambda i: (i, 0))
    out = jax.ShapeDtypeStruct((rows, cols), F32)
    return pl.pallas_call(
        body, name=name, grid=(rows // tr,),
        in_specs=[tile, tile, tile, pl.BlockSpec((N_DEV, tr, cols), lambda i: (0, i, 0))],
        out_specs=[tile, tile, tile, tile], out_shape=[out, out, out, out],
        compiler_params=_params("parallel"),
    )(w, m, v, gparts)


def all_gather_arrays(blocks, *, name):
    na = len(blocks)

    def body(*refs):
        x_refs, out_refs = refs[:na], refs[na:2 * na]
        send_sems, recv_sems, local_sems = refs[2 * na:]
        x, y, c, _ = _place()
        me, sibling = (x, y, c), (x, y, 1 - c)
        chips = [(1 - x, y), (x, 1 - y), (1 - x, 1 - y)]

        def copy(a, k, blk, to, own=False):
            px, py, pc = blk
            dst = out_refs[a].at[4 * px + 2 * py + pc]
            return pltpu.make_async_remote_copy(
                src_ref=x_refs[a] if own else dst, dst_ref=dst,
                send_sem=send_sems.at[7 * a + k], recv_sem=recv_sems.at[7 * a + k], device_id=to, device_id_type=_MESH)

        mine = [pltpu.make_async_copy(x_refs[a], out_refs[a].at[4 * x + 2 * y + c], local_sems.at[a]) for a in range(na)]
        for cp in mine:
            cp.start()
        sent = []
        for a in range(na):
            first = [copy(a, 0, me, sibling, own=True)]
            first += [copy(a, 1 + j, me, (*chip, c), own=True) for j, chip in enumerate(chips)]
            for cp in first:
                cp.start()
            sent += first
        for a in range(na):
            for j, chip in enumerate(chips):
                copy(a, 1 + j, (*chip, c), me).wait_recv()
                fwd = copy(a, 4 + j, (*chip, c), sibling)
                fwd.start()
                sent.append(fwd)
        for a in range(na):
            copy(a, 0, sibling, me).wait_recv()
            for j, chip in enumerate(chips):
                copy(a, 4 + j, (*chip, 1 - c), me).wait_recv()
        for cp in sent:
            cp.wait_send()
        for cp in mine:
            cp.wait()

    any_spec = pl.BlockSpec(memory_space=pl.ANY)
    return pl.pallas_call(
        body, name=name, in_specs=[any_spec] * na, out_specs=[any_spec] * na,
        out_shape=[jax.ShapeDtypeStruct((N_DEV,) + b.shape, b.dtype) for b in blocks],
        scratch_shapes=[pltpu.SemaphoreType.DMA((7 * na,)), pltpu.SemaphoreType.DMA((7 * na,)),
                        pltpu.SemaphoreType.DMA((na,))],
        compiler_params=pltpu.CompilerParams(has_side_effects=True),
    )(*blocks)


def exchange_partials(parts, *, name):
    na = len(parts)

    def body(*refs):
        local, remote = _comm_copies("exchange", refs[:na], refs[na:2 * na], *refs[2 * na:])
        for cp in local + remote:
            cp.start()
        for cp in remote:
            cp.wait_recv()
        for cp in remote:
            cp.wait_send()
        for cp in local:
            cp.wait()

    any_spec = pl.BlockSpec(memory_space=pl.ANY)
    return pl.pallas_call(
        body, name=name, in_specs=[any_spec] * na, out_specs=[any_spec] * na,
        out_shape=[jax.ShapeDtypeStruct(p.shape, p.dtype) for p in parts],
        scratch_shapes=[pltpu.SemaphoreType.DMA((na * N_REL,)), pltpu.SemaphoreType.DMA((na * N_REL,)),
                        pltpu.SemaphoreType.DMA((na,))],
        compiler_params=pltpu.CompilerParams(has_side_effects=True),
    )(*parts)


def _lane_row(v):
    return jnp.pad(v.astype(F32), (0, LANES - v.shape[0])).reshape(1, LANES)


def _block_diag_b(bb):
    eye = jnp.eye(SUBLANES, dtype=F32)
    bt = bb.reshape(S5_BLOCKS, 8, S5_STATE, S5_GROUP).transpose(0, 1, 3, 2)
    return (bt[:, :, :, None, :] * eye[None, :, None, :, None]).reshape(S5_WIDTH, S5_BS)


def _block_diag_c(cc):
    eye = jnp.eye(SUBLANES, dtype=F32)
    ct = cc.reshape(S5_BLOCKS, 8, S5_GROUP, S5_STATE).transpose(3, 0, 1, 2)
    return (ct[None, :, :, :, :] * eye[:, None, None, :, None]).reshape(S5_BS, S5_WIDTH)


def _diag_blocks_b(m):
    eye = jnp.eye(SUBLANES, dtype=F32)
    m5 = m.reshape(S5_BLOCKS, 8, S5_GROUP, 8, S5_STATE)
    return jnp.sum(m5 * eye[None, :, None, :, None], axis=3).transpose(0, 1, 3, 2).reshape(S5_GROUPS, S5_STATE, S5_GROUP)


def _diag_blocks_c(m):
    eye = jnp.eye(SUBLANES, dtype=F32)
    m5 = m.reshape(8, S5_STATE, S5_BLOCKS, 8, S5_GROUP)
    return jnp.sum(m5 * eye[:, None, None, :, None], axis=0).transpose(1, 2, 3, 0).reshape(S5_GROUPS, S5_GROUP, S5_STATE)


def train_step(x, target, p, m, v):
    t = x.shape[0]
    tm = min(256, t)
    bf = lambda n: p[n].astype(BF16)
    slots = lambda n, a: a.reshape((N_DEV,) + _shard_shape(n))
    updated = {}

    def update(names, got):
        for n, parts in zip(names, got):
            updated[n] = adamw(p[n], m[n], v[n], parts, name=f"adamw_{n}")

    st_in, st_conv = all_gather_arrays([bf("w_in"), p["conv_w"]], name="gather_first")
    win_p = jnp.concatenate([st_in[k][:, a:b] for seg in IN_PADDED_ORDER for k, a, b in _shard_pieces(*IN_SEGMENTS[seg])]
                            + [jnp.zeros((D_MODEL, LANES - SSD_HEADS), BF16)], axis=1)
    conv_w8 = jnp.pad(_join_shards("conv_w", st_conv), ((0, SUBLANES - SSD_CONV), (0, 0)))
    (h, xbc_raw, graw, z, u5, dtr), (st_branch, st_out, st_glu) = rms_matmul(
        x, p["norm_mix_g"], win_p, IN_SPLITS, tm=tm, name="in_proj",
        comm=("gather", [bf("w_branch"), bf("w_out"), bf("s5_glu_w")]))
    w_branch = st_branch.reshape(D_MODEL + S5_WIDTH, D_MODEL)
    wba, wbb = w_branch[:D_MODEL], w_branch[D_MODEL:]
    wout, wglu = st_out.reshape(D_MODEL, D_MODEL), st_glu.reshape(S5_WIDTH, S5_WIDTH)
    xbc = conv_silu_fwd(xbc_raw, conv_w8, p["conv_b"], tm=tm, name="conv_fwd")
    bias_row, alog_row = _lane_row(p["dt_bias"]), _lane_row(p["a_log"])
    d_row = jnp.repeat(p["d_ssd"], SSD_HEADDIM).reshape(1, SSD_HEADS * SSD_HEADDIM)
    (ya, yssd, states), (w1,) = ssd_fwd(xbc, dtr, z, bias_row, alog_row, d_row, p["ssd_norm_g"], name="ssd_fwd",
                                        comm=("gather", [bf("w_mlp_in")]))

    n = S5_N
    a_re_c, a_im_c = p["s5_a_re"].reshape(n, 1), p["s5_a_im"].reshape(n, 1)
    dt_c = jnp.repeat(p["s5_log_dt"], S5_STATE).reshape(n, 1)
    b_re_m, b_im_m = p["s5_b_re"].reshape(n, S5_GROUP), p["s5_b_im"].reshape(n, S5_GROUP)
    ab, bb_re, bb_im = s5_discretize(a_re_c, a_im_c, dt_c, b_re_m, b_im_m, name="s5_disc")
    ab8 = jnp.pad(ab.T, ((0, SUBLANES - 2), (0, 0)))
    bbc = jnp.concatenate([_block_diag_b(bb_re.reshape(S5_GROUPS, S5_STATE, S5_GROUP)),
                           _block_diag_b(bb_im.reshape(S5_GROUPS, S5_STATE, S5_GROUP))], axis=1).astype(BF16)
    ccc = jnp.concatenate([_block_diag_c(p["s5_c_re"]), -_block_diag_c(p["s5_c_im"])], axis=0).astype(BF16)
    s5d_row = p["s5_d"].reshape(1, S5_WIDTH)
    bglu_row = p["s5_glu_b"].reshape(1, S5_WIDTH)
    u5 = s5_interleave(u5, tm)
    (s, ys5, yb), (st_w2,) = s5_fwd(u5, bbc, ccc, ab8, s5d_row, wglu, bglu_row, tm=tm, name="s5_fwd",
                                    comm=("gather", [bf("w_mlp_out")]))
    yb = s5_deinterleave(yb, tm)
    w2 = st_w2.reshape(D_FF, D_MODEL)

    x1, pa, pb = merge_fwd(ya, yb, graw, x, wba, wbb, wout, tm=tm, name="merge_fwd")
    (h2, a1), _ = rms_matmul(x1, p["norm_mlp_g"], w1, (D_FF,), tm=tm, name="mlp_in")
    act, dx2, dx2b, loss_row, dgf = mlp_out_loss(a1, x1, w2, p["norm_final_g"], target, tm=tm, name="mlp_out_loss")

    g = {}
    g["norm_final_g"] = dgf[0]
    da1 = mlp_bwd_act(dx2b, a1, w2, tm=tm, name="mlp_bwd_act")
    dw2 = slots("w_mlp_out", matmul_tn(act, dx2b, out_dtype=BF16, name="dw_mlp_out"))
    dw1 = matmul_tn(h2, da1, out_dtype=BF16, col_shards=N_DEV, name="dw_mlp_in")
    (dx1, dx1b, dgm), _ = matmul_nt_rms_bwd([da1], w1, x1, p["norm_mlp_g"], dx2, tm=tm, name="mlp_in_bwd")
    g["norm_mlp_g"] = dgm[0]
    merged, dgraw, dpa, dpb, dya, dyb = merge_bwd(dx1b, graw, pa, pb, wba, wbb, wout, tm=tm, name="merge_bwd")
    dwo = slots("w_out", matmul_tn(merged, dx1b, out_dtype=BF16, name="dw_out"))
    dwb = slots("w_branch", jnp.concatenate([matmul_tn(ya, dpa, out_dtype=BF16, name="dw_branch_a"),
                                             matmul_tn(yb, dpb, out_dtype=BF16, name="dw_branch_b")], axis=0))

    (du5, dbbc, dccc, dab, dd5, dwglu, dbglu), got = s5_bwd(
        s5_interleave(dyb, tm), ys5, u5, s, bbc, ccc, ab8, s5d_row, wglu, bglu_row, tm=tm, name="s5_bwd",
        comm=("exchange", [dw2, dw1]))
    du5 = s5_deinterleave(du5, tm)
    update(["w_mlp_out", "w_mlp_in"], got)
    g["s5_glu_b"], g["s5_d"] = dbglu[0], dd5[0]
    g["s5_c_re"] = _diag_blocks_c(dccc[:S5_BS])
    g["s5_c_im"] = -_diag_blocks_c(dccc[S5_BS:])
    dbb_re = _diag_blocks_b(dbbc[:, :S5_BS]).reshape(n, S5_GROUP)
    dbb_im = _diag_blocks_b(dbbc[:, S5_BS:]).reshape(n, S5_GROUP)
    da, dlogdt, gb_re, gb_im = s5_discretize_bwd(a_re_c, a_im_c, dt_c, b_re_m, b_im_m, dab[:2].T, dbb_re, dbb_im,
                                                  name="s5_disc_bwd")
    g["s5_a_re"] = da[:, 0].reshape(S5_GROUPS, S5_STATE)
    g["s5_a_im"] = da[:, 1].reshape(S5_GROUPS, S5_STATE)
    g["s5_log_dt"] = dlogdt[:, 0]
    g["s5_b_re"] = gb_re.reshape(S5_GROUPS, S5_STATE, S5_GROUP)
    g["s5_b_im"] = gb_im.reshape(S5_GROUPS, S5_STATE, S5_GROUP)

    zero = jnp.zeros((), F32)
    bcast = lambda rows: jnp.broadcast_to(rows, (N_DEV,) + rows.shape)
    late = ["ssd_norm_g", "d_ssd", "a_log", "dt_bias", "conv_b"]
    early = [n for n in SMALL_ORDER if n not in late and n != "norm_mix_g"]
    (dz, dxbc, ddtr, dng, dd_row, dalog_row, dbias_row), got = ssd_bwd(
        dya, yssd, z, xbc, dtr, states, bias_row, alog_row, d_row, p["ssd_norm_g"], name="ssd_bwd",
        comm=("exchange", [dwo, dwb, slots("s5_glu_w", dwglu.astype(BF16)),
                           bcast(_pack_small(g, early, loss_row[0, 0]))]))
    update(["w_out", "w_branch", "s5_glu_w"], got[:3])
    got_small = {"early": got[3]}
    g["ssd_norm_g"] = dng[0]
    g["d_ssd"], g["a_log"], g["dt_bias"] = dd_row[0, :SSD_HEADS], dalog_row[0, :SSD_HEADS], dbias_row[0, :SSD_HEADS]
    dxc, dconv_b, dconv_w8 = conv_silu_bwd_a(xbc_raw, dxbc, conv_w8, p["conv_b"], tm=tm, name="conv_bwd_a")
    g["conv_b"] = dconv_b[0]
    dconv = dconv_w8[:SSD_CONV].reshape(SSD_CONV, N_DEV, CONV_DIM // N_DEV).transpose(1, 0, 2)
    dxbc_raw = conv_bwd_b(dxc, conv_w8, tm=tm, name="conv_bwd_b")
    dproj = [dxbc_raw, dgraw, dz, du5, ddtr]
    dxbc_w, dgate_w, dz_w, du5_w, ddt_w = [matmul_tn(h, piece, out_dtype=BF16, name=f"dw_in_{k}")
                                           for k, piece in enumerate(dproj)]
    by_segment = {"z": dz_w, "xbc": dxbc_w, "dt": ddt_w, "u5": du5_w, "gates": dgate_w}
    width = D_IN_PROJ // N_DEV
    dw_in = jnp.stack([jnp.concatenate(
        [by_segment[seg][:, max(k * width, lo) - lo:min((k + 1) * width, hi) - lo]
         for seg, (lo, hi) in IN_SEGMENTS.items() if max(k * width, lo) < min((k + 1) * width, hi)], axis=1)
        for k in range(N_DEV)])

    (grad_x, _, dgx), got = matmul_nt_rms_bwd(
        dproj, win_p, x, p["norm_mix_g"], dx1, tm=tm, name="in_proj_bwd",
        comm=("exchange", [dw_in, dconv, bcast(_pack_small(g, late, zero))]))
    update(["w_in", "conv_w"], got[:2])
    got_small["late"] = got[2]
    g["norm_mix_g"] = dgx[0]
    got_small["last"], = exchange_partials([bcast(_pack_small(g, ["norm_mix_g"], zero))], name="exchange_last")
    shapes = {n: p[n].shape for n in SMALL_ORDER}
    loss = None
    for names, tag in ((early, "early"), (late, "late"), (["norm_mix_g"], "last")):
        packed = adamw(_pack_small(p, names, zero), _pack_small(m, names, zero), _pack_small(v, names, zero),
                       got_small[tag], name=f"adamw_small_{tag}")
        for j, rows in enumerate(packed):
            vals, extra = _unpack_small(rows, names, shapes)
            if tag == "early" and j == 0:
                loss = extra
            for n in names:
                updated.setdefault(n, [None] * 4)[j] = vals[n]
    return loss, grad_x, updated


WEIGHT_ORDER = ["norm_mix_g", "w_in", "conv_w", "conv_b", "dt_bias", "a_log", "d_ssd", "ssd_norm_g", "s5_a_re",
                "s5_a_im", "s5_log_dt", "s5_b_re", "s5_b_im", "s5_c_re", "s5_c_im", "s5_d", "s5_glu_w", "s5_glu_b",
                "w_branch", "w_out", "norm_mlp_g", "w_mlp_in", "w_mlp_out", "norm_final_g"]
SHARDED = {"w_in": ((D_MODEL, D_IN_PROJ), 1), "conv_w": ((SSD_CONV, CONV_DIM), 1), "s5_glu_w": ((S5_WIDTH, S5_WIDTH), 0),
           "w_branch": ((D_MODEL + S5_WIDTH, D_MODEL), 0), "w_out": ((D_MODEL, D_MODEL), 0),
           "w_mlp_in": ((D_MODEL, D_FF), 1), "w_mlp_out": ((D_FF, D_MODEL), 0)}
SHARDED_ORDER = ["w_in", "conv_w", "s5_glu_w", "w_branch", "w_out", "w_mlp_in", "w_mlp_out"]
SMALL_ORDER = [n for n in WEIGHT_ORDER if n not in SHARDED]


def _shard_shape(name):
    (r, c), ax = SHARDED[name]
    return (r, c // N_DEV) if ax == 1 else (r // N_DEV, c)


def _join_shards(name, stacked):
    (r, c), ax = SHARDED[name]
    if ax == 1:
        return stacked.transpose(1, 0, 2).reshape(r, c)
    return stacked.reshape(r, c)


def _pack_small(values, names, extra):
    flat = jnp.concatenate([values[n].reshape(-1).astype(F32) for n in names] + [extra.reshape(1).astype(F32)])
    rows = -(-flat.shape[0] // (LANES * 16)) * 16
    return jnp.pad(flat, (0, rows * LANES - flat.shape[0])).reshape(rows, LANES)


def _unpack_small(rows, names, shapes):
    flat = rows.reshape(-1)
    out, off = {}, 0
    for n in names:
        size = math.prod(shapes[n])
        out[n] = flat[off:off + size].reshape(shapes[n])
        off += size
    return out, flat[off]


def _shard_pieces(lo, hi):
    width = D_IN_PROJ // N_DEV
    out = []
    while lo < hi:
        k = lo // width
        end = min(hi, (k + 1) * width)
        out.append((k, lo - k * width, end - k * width))
        lo = end
    return out


IN_SEGMENTS = {"z": (0, 1024), "xbc": (1024, 3072), "dt": (3072, 3088), "u5": (3088, 3600), "gates": (3600, 5648)}
IN_PADDED_ORDER = ("xbc", "gates", "z", "u5", "dt")


def kernel(x, norm_mix_g, w_in, conv_w, conv_b, dt_bias, a_log, d_ssd, ssd_norm_g, s5_a_re, s5_a_im, s5_log_dt, s5_b_re, s5_b_im, s5_c_re, s5_c_im, s5_d, s5_glu_w, s5_glu_b, w_branch, w_out, norm_mlp_g, w_mlp_in, w_mlp_out, norm_final_g, loss_target, m_norm_mix_g, m_w_in, m_conv_w, m_conv_b, m_dt_bias, m_a_log, m_d_ssd, m_ssd_norm_g, m_s5_a_re, m_s5_a_im, m_s5_log_dt, m_s5_b_re, m_s5_b_im, m_s5_c_re, m_s5_c_im, m_s5_d, m_s5_glu_w, m_s5_glu_b, m_w_branch, m_w_out, m_norm_mlp_g, m_w_mlp_in, m_w_mlp_out, m_norm_final_g, v_norm_mix_g, v_w_in, v_conv_w, v_conv_b, v_dt_bias, v_a_log, v_d_ssd, v_ssd_norm_g, v_s5_a_re, v_s5_a_im, v_s5_log_dt, v_s5_b_re, v_s5_b_im, v_s5_c_re, v_s5_c_im, v_s5_d, v_s5_glu_w, v_s5_glu_b, v_w_branch, v_w_out, v_norm_mlp_g, v_w_mlp_in, v_w_mlp_out, v_norm_final_g):
    w = dict(norm_mix_g=norm_mix_g, w_in=w_in, conv_w=conv_w, conv_b=conv_b, dt_bias=dt_bias, a_log=a_log, d_ssd=d_ssd,
             ssd_norm_g=ssd_norm_g, s5_a_re=s5_a_re, s5_a_im=s5_a_im, s5_log_dt=s5_log_dt, s5_b_re=s5_b_re,
             s5_b_im=s5_b_im, s5_c_re=s5_c_re, s5_c_im=s5_c_im, s5_d=s5_d, s5_glu_w=s5_glu_w, s5_glu_b=s5_glu_b,
             w_branch=w_branch, w_out=w_out, norm_mlp_g=norm_mlp_g, w_mlp_in=w_mlp_in, w_mlp_out=w_mlp_out,
             norm_final_g=norm_final_g)
    m = dict(norm_mix_g=m_norm_mix_g, w_in=m_w_in, conv_w=m_conv_w, conv_b=m_conv_b, dt_bias=m_dt_bias, a_log=m_a_log,
             d_ssd=m_d_ssd, ssd_norm_g=m_ssd_norm_g, s5_a_re=m_s5_a_re, s5_a_im=m_s5_a_im, s5_log_dt=m_s5_log_dt,
             s5_b_re=m_s5_b_re, s5_b_im=m_s5_b_im, s5_c_re=m_s5_c_re, s5_c_im=m_s5_c_im, s5_d=m_s5_d,
             s5_glu_w=m_s5_glu_w, s5_glu_b=m_s5_glu_b, w_branch=m_w_branch, w_out=m_w_out, norm_mlp_g=m_norm_mlp_g,
             w_mlp_in=m_w_mlp_in, w_mlp_out=m_w_mlp_out, norm_final_g=m_norm_final_g)
    v = dict(norm_mix_g=v_norm_mix_g, w_in=v_w_in, conv_w=v_conv_w, conv_b=v_conv_b, dt_bias=v_dt_bias, a_log=v_a_log,
             d_ssd=v_d_ssd, ssd_norm_g=v_ssd_norm_g, s5_a_re=v_s5_a_re, s5_a_im=v_s5_a_im, s5_log_dt=v_s5_log_dt,
             s5_b_re=v_s5_b_re, s5_b_im=v_s5_b_im, s5_c_re=v_s5_c_re, s5_c_im=v_s5_c_im, s5_d=v_s5_d,
             s5_glu_w=v_s5_glu_w, s5_glu_b=v_s5_glu_b, w_branch=v_w_branch, w_out=v_w_out, norm_mlp_g=v_norm_mlp_g,
             w_mlp_in=v_w_mlp_in, w_mlp_out=v_w_mlp_out, norm_final_g=v_norm_final_g)

    loss, grad_x, upd = train_step(x[0], loss_target[0], w, m, v)
    return (loss, grad_x.reshape(x.shape), *[upd[n][j] for j in range(4) for n in WEIGHT_ORDER])
```

```python
import functools
import math

import jax
import jax.numpy as jnp
from jax import lax
from jax.experimental import pallas as pl
from jax.experimental.pallas import tpu as pltpu

F32 = jnp.float32
BF16 = jnp.bfloat16
HI = lax.Precision.HIGHEST

N_DEV = 8
D_MODEL = 1024
SSD_HEADS = 16
SSD_HEADDIM = 64
SSD_GROUPS = 4
SSD_HPG = 4
SSD_STATE = 128
SSD_CHUNK = 128
SSD_CONV = 4
CONV_DIM = 2048
S5_WIDTH = 512
S5_GROUP = 16
S5_GROUPS = 32
S5_STATE = 64
S5_N = S5_GROUPS * S5_STATE
D_FF = 4096
D_IN_PROJ = 5648
IN_SPLITS = (2048, 2048, 1024, 512, 128)
D_IN_PAD = sum(IN_SPLITS)
EPS = 1e-6
LANES = 128
SUBLANES = 8
VMEM_LIMIT = 56 * 1024 * 1024

ADAM_LR = 0.001
ADAM_B1 = 0.9
ADAM_B2 = 0.999
ADAM_EPS = 1e-08
ADAM_WD = 0.01
ADAM_STEP = 10

GELU_C = math.sqrt(2.0 / math.pi)
GELU_K = 0.044715


def _params(*sem):
    return pltpu.CompilerParams(dimension_semantics=sem, vmem_limit_bytes=VMEM_LIMIT)


def _full(shape):
    nd = len(shape)
    return pl.BlockSpec(shape, lambda *_: (0,) * nd)


_MESH = pl.DeviceIdType.MESH
_RELATIONS = [(dx, dy, dc) for dx in (0, 1) for dy in (0, 1) for dc in (0, 1)][1:]
N_REL = len(_RELATIONS)


def _place():
    x, y, c = lax.axis_index("x"), lax.axis_index("y"), lax.axis_index("c")
    return x, y, c, 4 * x + 2 * y + c


def gathers(arrays):
    return [("gather", a) for a in arrays]


def exchanges(arrays):
    return [("exchange", a) for a in arrays]


def _comm_out_shapes(comm):
    return [jax.ShapeDtypeStruct(((N_DEV,) if kind == "gather" else ()) + a.shape, a.dtype) for kind, a in comm]


def _comm_copies(kinds, src_refs, dst_refs, send_sems, recv_sems, local_sems):
    x, y, c, idx = _place()
    local, remote = [], []
    for a, (kind, src, dst) in enumerate(zip(kinds, src_refs, dst_refs)):
        local.append(pltpu.make_async_copy(src if kind == "gather" else src.at[idx], dst.at[idx], local_sems.at[a]))
        for k, (dx, dy, dc) in enumerate(_RELATIONS):
            px, py, pc = x ^ dx, y ^ dy, c ^ dc
            remote.append(pltpu.make_async_remote_copy(
                src_ref=src if kind == "gather" else src.at[4 * px + 2 * py + pc], dst_ref=dst.at[idx],
                send_sem=send_sems.at[N_REL * a + k], recv_sem=recv_sems.at[N_REL * a + k],
                device_id=(px, py, pc), device_id_type=_MESH))
    return local, remote


def _tiled_call(body, *, name, grid, in_specs, out_specs, out_shape, operands, scratch_shapes=(), comm=None):
    params = pltpu.CompilerParams(dimension_semantics=("arbitrary",), vmem_limit_bytes=VMEM_LIMIT,
                                  has_side_effects=bool(comm))
    if not comm:
        outs = pl.pallas_call(body, name=name, grid=grid, in_specs=in_specs, out_specs=out_specs, out_shape=out_shape,
                              scratch_shapes=list(scratch_shapes), compiler_params=params)(*operands)
        return outs, []
    kind = [k for k, _ in comm]
    arrays = [a for _, a in comm]
    na, n_in, n_out, n_scr = len(arrays), len(in_specs), len(out_specs), len(scratch_shapes)
    last = grid[0] - 1

    def hosted(*refs):
        ins, refs = refs[:n_in], refs[n_in:]
        cin, refs = refs[:na], refs[na:]
        outs, refs = refs[:n_out], refs[n_out:]
        cout, refs = refs[:na], refs[na:]
        scr, sems = refs[:n_scr], refs[n_scr:]
        i = pl.program_id(0)

        @pl.when(i == 0)
        def _():
            local, remote = _comm_copies(kind, cin, cout, *sems)
            for cp in local + remote:
                cp.start()

        body(*ins, *outs, *scr)

        @pl.when(i == last)
        def _():
            local, remote = _comm_copies(kind, cin, cout, *sems)
            for cp in remote:
                cp.wait_recv()
            for cp in remote:
                cp.wait_send()
            for cp in local:
                cp.wait()

    any_spec = pl.BlockSpec(memory_space=pl.ANY)
    res = pl.pallas_call(
        hosted, name=name, grid=grid,
        in_specs=list(in_specs) + [any_spec] * na, out_specs=list(out_specs) + [any_spec] * na,
        out_shape=list(out_shape) + _comm_out_shapes(comm),
        scratch_shapes=list(scratch_shapes) + [pltpu.SemaphoreType.DMA((N_REL * na,)), pltpu.SemaphoreType.DMA((N_REL * na,)),
                                               pltpu.SemaphoreType.DMA((na,))],
        compiler_params=params)(*operands, *arrays)
    return res[:n_out], res[n_out:]


def _dot(a, b):
    return jnp.dot(a, b, preferred_element_type=F32)


def _dot_nt(a, b):
    return lax.dot_general(a, b, (((1,), (1,)), ((), ())), preferred_element_type=F32)


def _dot_tn(a, b):
    return lax.dot_general(a, b, (((0,), (0,)), ((), ())), preferred_element_type=F32)


def _dot_hi(a, b):
    return jnp.dot(a, b, preferred_element_type=F32, precision=HI)


def _dot_nt_hi(a, b):
    return lax.dot_general(a, b, (((1,), (1,)), ((), ())), preferred_element_type=F32, precision=HI)


def _dot_tn_hi(a, b):
    return lax.dot_general(a, b, (((0,), (0,)), ((), ())), preferred_element_type=F32, precision=HI)


def _sigmoid(x):
    return 1.0 / (1.0 + jnp.exp(-x))


def _softplus(x):
    return jnp.maximum(x, 0.0) + jnp.log(1.0 + jnp.exp(-jnp.abs(x)))


def _gelu(x):
    return 0.5 * x * (1.0 + jnp.tanh(GELU_C * (x + GELU_K * x * x * x)))


def _gelu_grad(x):
    th = jnp.tanh(GELU_C * (x + GELU_K * x * x * x))
    return 0.5 * (1.0 + th) + 0.5 * x * (1.0 - th * th) * GELU_C * (1.0 + 3.0 * GELU_K * x * x)


def rms_matmul(x, g, w, splits, *, tm, name, comm=None):
    t, d = x.shape
    stacked = w.ndim == 3
    n = w.shape[0] * w.shape[2] if stacked else w.shape[1]
    assert sum(splits) == n and (len(splits) == 1 or not stacked)

    def body(x_ref, g_ref, w_ref, h_ref, *o_refs):
        xv = x_ref[...]
        r = lax.rsqrt(jnp.mean(xv * xv, axis=-1, keepdims=True) + EPS)
        h = (xv * r * g_ref[...]).astype(BF16)
        h_ref[...] = h
        if stacked:
            wk = w.shape[2]
            for k in range(w.shape[0]):
                o_refs[0][:, k * wk:(k + 1) * wk] = _dot(h, w_ref[k])
            return
        off = 0
        for o_ref, width in zip(o_refs, splits):
            o_ref[...] = _dot(h, w_ref[:, off:off + width])
            off += width

    tile = lambda c: pl.BlockSpec((tm, c), lambda i: (i, 0))
    return _tiled_call(
        body, name=name, grid=(t // tm,),
        in_specs=[tile(d), _full((1, d)), _full(w.shape)],
        out_specs=[tile(d)] + [tile(c) for c in splits],
        out_shape=[jax.ShapeDtypeStruct((t, d), BF16)] + [jax.ShapeDtypeStruct((t, c), F32) for c in splits],
        operands=(x, g.reshape(1, d), w), comm=comm)


def _pick(n, cap):
    best = LANES
    for c in range(LANES, cap + 1, LANES):
        if n % c == 0:
            best = c
    return best


def matmul_tn(a, b, *, name, out_dtype=F32, col_shards=None, tk=1024):
    t, m = a.shape
    n = b.shape[1]
    tk = min(tk, t)
    tm = _pick(m, 1024)
    tn = n // col_shards if col_shards else _pick(n, 1280)
    nk = t // tk

    def body(a_ref, b_ref, o_ref, acc):
        k = pl.program_id(2)

        @pl.when(k == 0)
        def _():
            acc[...] = jnp.zeros_like(acc)

        acc[...] += _dot_tn(a_ref[...], b_ref[...])

        @pl.when(k == nk - 1)
        def _():
            o_ref[...] = acc[...].reshape(o_ref.shape).astype(out_dtype)

    if col_shards:
        out_spec = pl.BlockSpec((1, tm, tn), lambda i, j, k: (j, i, 0))
        out_shape = jax.ShapeDtypeStruct((col_shards, m, tn), out_dtype)
    else:
        out_spec = pl.BlockSpec((tm, tn), lambda i, j, k: (i, j))
        out_shape = jax.ShapeDtypeStruct((m, n), out_dtype)
    return pl.pallas_call(
        body, name=name, grid=(m // tm, n // tn, nk),
        in_specs=[pl.BlockSpec((tk, tm), lambda i, j, k: (k, i)), pl.BlockSpec((tk, tn), lambda i, j, k: (k, j))],
        out_specs=out_spec, out_shape=out_shape,
        scratch_shapes=[pltpu.VMEM((tm, tn), F32)],
        compiler_params=_params("parallel", "parallel", "arbitrary"),
    )(a, b)


def matmul_nt_rms_bwd(dys, w, x, g, dres, *, tm, name, comm=None):
    t = x.shape[0]
    stacked = w.ndim == 3
    d = w.shape[1] if stacked else w.shape[0]
    widths = [dy.shape[1] for dy in dys]
    n_dy = len(dys)

    def body(*refs):
        dy_refs = refs[:n_dy]
        w_ref, x_ref, g_ref, dres_ref, dx_ref, dxb_ref, dg_ref = refs[n_dy:]
        i = pl.program_id(0)
        if stacked:
            wk = w.shape[2]
            dh = _dot_nt(dy_refs[0][:, 0:wk], w_ref[0])
            for k in range(1, w.shape[0]):
                dh = dh + _dot_nt(dy_refs[0][:, k * wk:(k + 1) * wk], w_ref[k])
        else:
            dh, off = None, 0
            for dy_ref, width in zip(dy_refs, widths):
                part = _dot_nt(dy_ref[...], w_ref[:, off:off + width])
                dh = part if dh is None else dh + part
                off += width
        xv = x_ref[...]
        r = lax.rsqrt(jnp.mean(xv * xv, axis=-1, keepdims=True) + EPS)
        xn = xv * r
        dxn = dh * g_ref[...]
        dx = dres_ref[...] + r * (dxn - xn * jnp.mean(dxn * xn, axis=-1, keepdims=True))
        dx_ref[...] = dx
        dxb_ref[...] = dx.astype(BF16)

        @pl.when(i == 0)
        def _():
            dg_ref[...] = jnp.zeros_like(dg_ref)

        dg_ref[...] += jnp.sum(dh * xn, axis=0, keepdims=True)

    tile = lambda c: pl.BlockSpec((tm, c), lambda i: (i, 0))
    return _tiled_call(
        body, name=name, grid=(t // tm,),
        in_specs=[tile(c) for c in widths] + [_full(w.shape), tile(d), _full((1, d)), tile(d)],
        out_specs=[tile(d), tile(d), _full((1, d))],
        out_shape=[jax.ShapeDtypeStruct((t, d), F32), jax.ShapeDtypeStruct((t, d), BF16),
                   jax.ShapeDtypeStruct((1, d), F32)],
        operands=(*dys, w, x, g.reshape(1, d), dres), comm=comm)


def _conv_taps(prev8, cur, w_ref, tm):
    ext = jnp.concatenate([prev8, cur], axis=0)
    acc = cur * w_ref[3:4, :]
    for k in range(3):
        off = SUBLANES - 3 + k
        acc = acc + ext[off:off + tm, :] * w_ref[k:k + 1, :]
    return acc


def conv_silu_fwd(u, w8, b, *, tm, name):
    t, c = u.shape
    per = tm // SUBLANES

    def body(prev_ref, cur_ref, w_ref, b_ref, o_ref):
        i = pl.program_id(0)
        prev8 = jnp.where(i == 0, 0.0, prev_ref[...])
        xc = _conv_taps(prev8, cur_ref[...], w_ref, tm) + b_ref[...]
        o_ref[...] = xc * _sigmoid(xc)

    return pl.pallas_call(
        body, name=name, grid=(t // tm,),
        in_specs=[pl.BlockSpec((SUBLANES, c), lambda i: (jnp.maximum(i * per - 1, 0), 0)),
                  pl.BlockSpec((tm, c), lambda i: (i, 0)), _full((SUBLANES, c)), _full((1, c))],
        out_specs=pl.BlockSpec((tm, c), lambda i: (i, 0)),
        out_shape=jax.ShapeDtypeStruct((t, c), F32),
        compiler_params=_params("parallel"),
    )(u, u, w8, b.reshape(1, c))


def conv_silu_bwd_a(u, dy, w8, b, *, tm, name):
    t, c = u.shape
    per = tm // SUBLANES

    def body(prev_ref, cur_ref, dy_ref, w_ref, b_ref, dxc_ref, db_ref, dw_ref):
        i = pl.program_id(0)
        prev8 = jnp.where(i == 0, 0.0, prev_ref[...])
        cur = cur_ref[...]
        xc = _conv_taps(prev8, cur, w_ref, tm) + b_ref[...]
        sg = _sigmoid(xc)
        dxc = dy_ref[...] * (sg * (1.0 + xc * (1.0 - sg)))
        dxc_ref[...] = dxc

        @pl.when(i == 0)
        def _():
            db_ref[...] = jnp.zeros_like(db_ref)
            dw_ref[...] = jnp.zeros_like(dw_ref)

        db_ref[...] += jnp.sum(dxc, axis=0, keepdims=True)
        ext = jnp.concatenate([prev8, cur], axis=0)
        rows = []
        for k in range(SSD_CONV):
            off = SUBLANES - 3 + k
            rows.append(jnp.sum(dxc * ext[off:off + tm, :], axis=0, keepdims=True))
        rows.append(jnp.zeros((SUBLANES - SSD_CONV, c), F32))
        dw_ref[...] += jnp.concatenate(rows, axis=0)

    return pl.pallas_call(
        body, name=name, grid=(t // tm,),
        in_specs=[pl.BlockSpec((SUBLANES, c), lambda i: (jnp.maximum(i * per - 1, 0), 0)),
                  pl.BlockSpec((tm, c), lambda i: (i, 0)), pl.BlockSpec((tm, c), lambda i: (i, 0)),
                  _full((SUBLANES, c)), _full((1, c))],
        out_specs=[pl.BlockSpec((tm, c), lambda i: (i, 0)), _full((1, c)), _full((SUBLANES, c))],
        out_shape=[jax.ShapeDtypeStruct((t, c), F32), jax.ShapeDtypeStruct((1, c), F32),
                   jax.ShapeDtypeStruct((SUBLANES, c), F32)],
        compiler_params=_params("arbitrary"),
    )(u, u, dy, w8, b.reshape(1, c))


def conv_bwd_b(dxc, w8, *, tm, name):
    t, c = dxc.shape
    per = tm // SUBLANES
    last = t // SUBLANES - 1
    nt = t // tm

    def body(cur_ref, next_ref, w_ref, o_ref):
        i = pl.program_id(0)
        nxt = jnp.where(i == nt - 1, 0.0, next_ref[...])
        cur = cur_ref[...]
        ext = jnp.concatenate([cur, nxt], axis=0)
        acc = cur * w_ref[3:4, :]
        for j in range(1, SSD_CONV):
            acc = acc + ext[j:j + tm, :] * w_ref[3 - j:4 - j, :]
        o_ref[...] = acc.astype(BF16)

    return pl.pallas_call(
        body, name=name, grid=(nt,),
        in_specs=[pl.BlockSpec((tm, c), lambda i: (i, 0)),
                  pl.BlockSpec((SUBLANES, c), lambda i: (jnp.minimum((i + 1) * per, last), 0)), _full((SUBLANES, c))],
        out_specs=pl.BlockSpec((tm, c), lambda i: (i, 0)),
        out_shape=jax.ShapeDtypeStruct((t, c), BF16),
        compiler_params=_params("parallel"),
    )(dxc, dxc, w8)


def _ssd_chunk_terms(dtr, bias_row, alog_row):
    q = SSD_CHUNK
    row = lax.broadcasted_iota(jnp.int32, (q, q), 0)
    col = lax.broadcasted_iota(jnp.int32, (q, q), 1)
    ltri = (col <= row).astype(F32)
    dt = _softplus(dtr + bias_row)
    a_row = -jnp.exp(alog_row)
    la = dt * a_row
    cum = _dot_hi(ltri, la)
    cum_t = _dot_tn_hi(la, (row <= col).astype(F32))
    return dt, a_row, cum, cum_t, ltri, row, col


def _head_maps():
    di = SSD_HEADS * SSD_HEADDIM
    expand = (lax.broadcasted_iota(jnp.int32, (LANES, di), 1) // SSD_HEADDIM
              == lax.broadcasted_iota(jnp.int32, (LANES, di), 0)).astype(F32)
    collapse = (lax.broadcasted_iota(jnp.int32, (di, LANES), 0) // SSD_HEADDIM
                == lax.broadcasted_iota(jnp.int32, (di, LANES), 1)).astype(F32)
    return expand, collapse


def _ssd_decay(cum, cum_t, h, causal):
    seg = cum[:, h:h + 1] - cum_t[h:h + 1, :]
    return jnp.where(causal, jnp.exp(jnp.where(causal, seg, 0.0)), 0.0)


def _pair_block_diag(x2):
    low = lax.broadcasted_iota(jnp.int32, x2.shape, 1) < SSD_HEADDIM
    zero = jnp.zeros_like(x2)
    return jnp.concatenate([jnp.where(low, x2, zero), jnp.where(low, zero, x2)], axis=0)


def ssd_fwd(xbc, dtr, z, bias_row, alog_row, d_row, norm_g, *, name, comm=None):
    t = xbc.shape[0]
    q, p, n = SSD_CHUNK, SSD_HEADDIM, SSD_STATE
    nc = t // q
    di = SSD_HEADS * p
    gw = di // SSD_GROUPS

    def body(xs_ref, b_ref, c_ref, dtr_ref, z_ref, bias_ref, alog_ref, d_ref, g_ref,
             ya_ref, yssd_ref, st_ref, state):
        ci = pl.program_id(0)

        @pl.when(ci == 0)
        def _():
            state[...] = jnp.zeros_like(state)

        st_ref[0] = state[...]
        dt, _, cum, cum_t, _, row, col = _ssd_chunk_terms(dtr_ref[...], bias_ref[...], alog_ref[...])
        causal = row >= col
        expand, _ = _head_maps()
        dt_full = _dot_hi(dt, expand)
        cum_full = _dot_hi(cum, expand)
        last_full = cum_full[q - 1:q, :]
        xs = xs_ref[...]
        xdt = xs * dt_full
        xd = (xdt * jnp.exp(last_full - cum_full)).astype(BF16)
        xdt_b = xdt.astype(BF16)
        e_full = jnp.exp(cum_full)
        eend_full = jnp.exp(last_full)
        for g in range(SSD_GROUPS):
            slab = slice(g * gw, (g + 1) * gw)
            bg = b_ref[:, g * n:(g + 1) * n].astype(BF16)
            cg = c_ref[:, g * n:(g + 1) * n].astype(BF16)
            cb = _dot_nt(cg, bg)
            st_g = state[:, slab]
            pairs = []
            for pr in range(SSD_HPG // 2):
                h0 = g * SSD_HPG + 2 * pr
                m0 = (cb * _ssd_decay(cum, cum_t, h0, causal)).astype(BF16)
                m1 = (cb * _ssd_decay(cum, cum_t, h0 + 1, causal)).astype(BF16)
                pairs.append(_dot(jnp.concatenate([m0, m1], axis=1), _pair_block_diag(xdt_b[:, h0 * p:(h0 + 2) * p])))
            y = jnp.concatenate(pairs, axis=1) + e_full[:, slab] * _dot(cg, st_g.astype(BF16))
            yssd_ref[:, slab] = y + d_ref[:, slab] * xs[:, slab]
            state[:, slab] = eend_full[:, slab] * st_g + _dot_tn(bg, xd[:, slab])
        zv = z_ref[...]
        ya1 = yssd_ref[...] * (zv * _sigmoid(zv))
        for g in range(SSD_GROUPS):
            gsl = slice(g * gw, (g + 1) * gw)
            sl = ya1[:, gsl]
            rg = lax.rsqrt(jnp.mean(sl * sl, axis=-1, keepdims=True) + EPS)
            ya_ref[:, gsl] = (sl * rg * g_ref[:, gsl]).astype(BF16)

    blk = lambda w, j: pl.BlockSpec((q, w), lambda c: (c, j))
    return _tiled_call(
        body, name=name, grid=(nc,),
        in_specs=[blk(di, 0), blk(512, 2), blk(512, 3), blk(LANES, 0), blk(di, 0),
                  _full((1, LANES)), _full((1, LANES)), _full((1, di)), _full((1, di))],
        out_specs=[blk(di, 0), blk(di, 0), pl.BlockSpec((1, n, di), lambda c: (c, 0, 0))],
        out_shape=[jax.ShapeDtypeStruct((t, di), BF16), jax.ShapeDtypeStruct((t, di), F32),
                   jax.ShapeDtypeStruct((nc, n, di), F32)],
        scratch_shapes=[pltpu.VMEM((n, di), F32)],
        operands=(xbc, xbc, xbc, dtr, z, bias_row, alog_row, d_row, norm_g.reshape(1, di)), comm=comm)


def ssd_bwd(dya, yssd, z, xbc, dtr, states, bias_row, alog_row, d_row, norm_g, *, name, comm=None):
    t = xbc.shape[0]
    q, p, n = SSD_CHUNK, SSD_HEADDIM, SSD_STATE
    nc = t // q
    di = SSD_HEADS * p
    gw = di // SSD_GROUPS

    def body(dya_ref, yssd_ref, z_ref, xs_ref, b_ref, c_ref, dtr_ref, st_ref, bias_ref, alog_ref, d_ref, g_ref,
             dz_ref, dxbc_ref, ddtr_ref, dng_ref, dd_ref, dalog_ref, dbias_ref, dstate, dyssd):
        ci = pl.program_id(0)

        @pl.when(ci == 0)
        def _():
            dstate[...] = jnp.zeros_like(dstate)
            dng_ref[...] = jnp.zeros_like(dng_ref)
            dd_ref[...] = jnp.zeros_like(dd_ref)
            dalog_ref[...] = jnp.zeros_like(dalog_ref)
            dbias_ref[...] = jnp.zeros_like(dbias_ref)

        zv = z_ref[...]
        sz = _sigmoid(zv)
        silu_z = zv * sz
        yv = yssd_ref[...]
        ya1 = yv * silu_z
        dyav = dya_ref[...]
        for g in range(SSD_GROUPS):
            gsl = slice(g * gw, (g + 1) * gw)
            sl = ya1[:, gsl]
            rg = lax.rsqrt(jnp.mean(sl * sl, axis=-1, keepdims=True) + EPS)
            ya2 = sl * rg
            dy_g = dyav[:, gsl]
            dng_ref[:, gsl] += jnp.sum(dy_g * ya2, axis=0, keepdims=True)
            dya2 = dy_g * g_ref[:, gsl]
            dya1 = rg * (dya2 - ya2 * jnp.mean(dya2 * ya2, axis=-1, keepdims=True))
            dyssd[:, gsl] = dya1 * silu_z[:, gsl]
            dz_ref[:, gsl] = (dya1 * yv[:, gsl] * (sz[:, gsl] * (1.0 + zv[:, gsl] * (1.0 - sz[:, gsl])))).astype(BF16)

        dtr_v = dtr_ref[...]
        dt, a_row, cum, cum_t, ltri, row, col = _ssd_chunk_terms(dtr_v, bias_ref[...], alog_ref[...])
        causal = row >= col
        expand, collapse = _head_maps()
        lane = lax.broadcasted_iota(jnp.int32, (1, LANES), 1)
        sub = lax.broadcasted_iota(jnp.int32, (LANES, 1), 0)
        is_last = lax.broadcasted_iota(jnp.int32, (q, 1), 0) == q - 1
        low = lax.broadcasted_iota(jnp.int32, (q, 2 * p), 1) < p
        dt_full = _dot_hi(dt, expand)
        cum_full = _dot_hi(cum, expand)
        last_full = cum_full[q - 1:q, :]
        e_full = jnp.exp(cum_full)
        eend_full = jnp.exp(last_full)
        dend_full = jnp.exp(last_full - cum_full)
        xs = xs_ref[...]
        xdt = xs * dt_full
        xdt_b = xdt.astype(BF16)
        xd_b = (xdt * dend_full).astype(BF16)
        dy_all = dyssd[...]
        dy_b = dy_all.astype(BF16)
        dg_b = (dy_all * e_full).astype(BF16)
        dcum_mat = jnp.zeros((q, LANES), F32)
        rmat = jnp.zeros((LANES, q), F32)
        yoff_prod, bds_list, dx_list, dlast_parts = [], [], [], []
        for g in range(SSD_GROUPS):
            slab = slice(g * gw, (g + 1) * gw)
            bg = b_ref[:, g * n:(g + 1) * n].astype(BF16)
            cg = c_ref[:, g * n:(g + 1) * n].astype(BF16)
            cb = _dot_nt(cg, bg)
            st_g = st_ref[0, :, slab]
            st_b = st_g.astype(BF16)
            dsn = dstate[:, slab]
            dsn_b = dsn.astype(BF16)
            yoff_prod.append(dy_all[:, slab] * e_full[:, slab] * _dot(cg, st_b))
            dcg = _dot_nt(dg_b[:, slab], st_b)
            dstate[:, slab] = _dot_tn(cg, dg_b[:, slab]) + eend_full[:, slab] * dsn
            bds_list.append(_dot(bg, dsn_b))
            dbg = _dot_nt(xd_b[:, slab], dsn_b)
            dlast_parts.append(jnp.sum(dsn * st_g, axis=0, keepdims=True))
            dcb = jnp.zeros((q, q), F32)
            dx_pairs = []
            for pr in range(SSD_HPG // 2):
                h0 = g * SSD_HPG + 2 * pr
                ps = slice(h0 * p, (h0 + 2) * p)
                l0 = _ssd_decay(cum, cum_t, h0, causal)
                l1 = _ssd_decay(cum, cum_t, h0 + 1, causal)
                m0, m1 = cb * l0, cb * l1
                dyp = dy_b[:, ps]
                zero = jnp.zeros_like(dyp)
                dy0, dy1 = jnp.where(low, dyp, zero), jnp.where(low, zero, dyp)
                dm0 = _dot_nt(dy0, xdt_b[:, ps])
                dm1 = _dot_nt(dy1, xdt_b[:, ps])
                w0, w1 = dm0 * m0, dm1 * m1
                dcum_mat = (dcum_mat + jnp.sum(w0, axis=1, keepdims=True) * (lane == h0).astype(F32)
                            + jnp.sum(w1, axis=1, keepdims=True) * (lane == h0 + 1).astype(F32))
                rmat = (rmat + jnp.where(sub == h0, jnp.sum(w0, axis=0, keepdims=True), 0.0)
                        + jnp.where(sub == h0 + 1, jnp.sum(w1, axis=0, keepdims=True), 0.0))
                dcb = dcb + dm0 * l0 + dm1 * l1
                dx_pairs.append(_dot_tn(jnp.concatenate([m0.astype(BF16), m1.astype(BF16)], axis=0),
                                        jnp.concatenate([dy0, dy1], axis=0)))
            dx_list.append(jnp.concatenate(dx_pairs, axis=1))
            dcb_b = dcb.astype(BF16)
            dxbc_ref[:, di + g * n:di + (g + 1) * n] = dbg + _dot_tn(dcb_b, cg)
            dxbc_ref[:, di + SSD_GROUPS * n + g * n:di + SSD_GROUPS * n + (g + 1) * n] = dcg + _dot(dcb_b, bg)
        bds_all = jnp.concatenate(bds_list, axis=1)
        dx_all = jnp.concatenate(dx_list, axis=1) + dend_full * bds_all
        last_row = cum[q - 1:q, :]
        qv = _dot_hi(xdt * bds_all, collapse) * jnp.exp(last_row - cum)
        dcum_mat = dcum_mat + _dot_hi(jnp.concatenate(yoff_prod, axis=1), collapse) - qv
        dlast_full = jnp.broadcast_to(jnp.concatenate(dlast_parts, axis=1), (SUBLANES, di))
        dlast_row = _dot_hi(dlast_full, collapse)[0:1, :] * jnp.exp(last_row) + jnp.sum(qv, axis=0, keepdims=True)
        dcum_mat = dcum_mat + jnp.where(is_last, dlast_row, 0.0)
        dla = _dot_tn_hi(ltri, dcum_mat) - _dot_nt_hi((row <= col).astype(F32), rmat)
        ddt_mat = _dot_hi(dx_all * xs, collapse) + dla * a_row
        dxbc_ref[:, 0:di] = d_ref[...] * dy_all + dx_all * dt_full
        ddtr = ddt_mat * _sigmoid(dtr_v + bias_ref[...])
        ddtr_ref[...] = ddtr.astype(BF16)
        dd_full = jnp.broadcast_to(jnp.sum(dy_all * xs, axis=0, keepdims=True), (SUBLANES, di))
        dd_ref[...] += _dot_hi(dd_full, collapse)[0:1, :]
        dalog_ref[...] += jnp.sum(dla * dt, axis=0, keepdims=True) * a_row
        dbias_ref[...] += jnp.sum(ddtr, axis=0, keepdims=True)

    blk = lambda w, j: pl.BlockSpec((q, w), lambda c: (nc - 1 - c, j))
    row_out = lambda w: jax.ShapeDtypeStruct((1, w), F32)
    return _tiled_call(
        body, name=name, grid=(nc,),
        in_specs=[blk(di, 0), blk(di, 0), blk(di, 0), blk(di, 0), blk(512, 2), blk(512, 3), blk(LANES, 0),
                  pl.BlockSpec((1, n, di), lambda c: (nc - 1 - c, 0, 0)),
                  _full((1, LANES)), _full((1, LANES)), _full((1, di)), _full((1, di))],
        out_specs=[blk(di, 0), blk(CONV_DIM, 0), blk(LANES, 0),
                   _full((1, di)), _full((1, LANES)), _full((1, LANES)), _full((1, LANES))],
        out_shape=[jax.ShapeDtypeStruct((t, di), BF16), jax.ShapeDtypeStruct((t, CONV_DIM), F32),
                   jax.ShapeDtypeStruct((t, LANES), BF16), row_out(di), row_out(LANES), row_out(LANES), row_out(LANES)],
        scratch_shapes=[pltpu.VMEM((n, di), F32), pltpu.VMEM((q, di), F32)],
        operands=(dya, yssd, z, xbc, xbc, xbc, dtr, states, bias_row, alog_row, d_row, norm_g.reshape(1, di)),
        comm=comm)


S5_BLOCKS = 4
S5_BC = S5_WIDTH // S5_BLOCKS
S5_BS = S5_N // S5_BLOCKS


def _s5_tables(ar, ai, reverse):
    pows = [(ar, ai)]
    for _ in range(SUBLANES - 1):
        pr, pi = pows[-1]
        pows.append((pr * ar - pi * ai, pr * ai + pi * ar))
    row = lax.broadcasted_iota(jnp.int32, (SUBLANES, ar.shape[1]), 0)
    out = []
    for k in (1, 2, 4):
        pr, pi = pows[k - 1]
        mask = (row < SUBLANES - k) if reverse else (row >= k)
        out += [jnp.where(mask, pr, 0.0), jnp.where(mask, pi, 0.0)]
    order = list(range(SUBLANES))
    if reverse:
        order = order[::-1]
    out.append(jnp.concatenate([pows[e][0] for e in order], axis=0))
    out.append(jnp.concatenate([pows[e][1] for e in order], axis=0))
    return out


def s5_interleave(a, tm):
    t, c = a.shape
    return a.reshape(t // tm, SUBLANES, tm // SUBLANES, c).transpose(0, 2, 1, 3).reshape(t, c)


def s5_deinterleave(a, tm):
    t, c = a.shape
    return a.reshape(t // tm, tm // SUBLANES, SUBLANES, c).transpose(0, 2, 1, 3).reshape(t, c)


def _s5_scan_setup(ab_ref, base, tab, seg, reverse):
    ar = ab_ref[0:1, :]
    ai = -ab_ref[1:2, :] if reverse else ab_ref[1:2, :]
    base[0:1, :] = ar
    base[1:2, :] = ai
    assert seg & (seg - 1) == 0
    pr, pi = ar, ai
    for _ in range(seg.bit_length() - 1):
        pr, pi = pr * pr - pi * pi, 2.0 * pr * pi
    for k, tv in enumerate(_s5_tables(pr, pi, reverse)):
        tab[k] = tv


def _s5_scan(s_scr, base, tab, carry, tm, reverse):
    seg = tm // SUBLANES
    width = S5_BS
    first = lax.broadcasted_iota(jnp.int32, (SUBLANES, width), 0) == (SUBLANES - 1 if reverse else 0)

    def rows(i):
        step = (seg - 1 - i) if reverse else i
        return pl.ds(pl.multiple_of(step * SUBLANES, SUBLANES), SUBLANES)

    for lc in range(S5_N // width):
        lr = slice(lc * width, (lc + 1) * width)
        li = slice(S5_N + lc * width, S5_N + (lc + 1) * width)
        ar = jnp.broadcast_to(base[0:1, lr], (SUBLANES, width))
        ai = jnp.broadcast_to(base[1:2, lr], (SUBLANES, width))

        def advance(i, x, lr=lr, li=li, ar=ar, ai=ai):
            xr, xi = x
            return ar * xr - ai * xi + s_scr[rows(i), lr], ar * xi + ai * xr + s_scr[rows(i), li]

        zero = jnp.zeros((SUBLANES, width), F32)
        fr, fi = lax.fori_loop(0, seg, advance, (zero, zero))

        tb = [tab[k, :, lr] for k in range(8)]
        for lvl, k in enumerate((1, 2, 4)):
            sh = (SUBLANES - k) if reverse else k
            pr, pi = tb[2 * lvl], tb[2 * lvl + 1]
            rr, ri = pltpu.roll(fr, sh, 0), pltpu.roll(fi, sh, 0)
            fr, fi = fr + pr * rr - pi * ri, fi + pr * ri + pi * rr
        cr, ci = carry[0:1, lr], carry[0:1, li]
        fr, fi = fr + tb[6] * cr - tb[7] * ci, fi + tb[6] * ci + tb[7] * cr
        last = 0 if reverse else SUBLANES - 1
        carry[0:1, lr] = fr[last:last + 1, :]
        carry[0:1, li] = fi[last:last + 1, :]
        sh = (SUBLANES - 1) if reverse else 1
        entering = (jnp.where(first, cr, pltpu.roll(fr, sh, 0)), jnp.where(first, ci, pltpu.roll(fi, sh, 0)))

        def rerun(i, x, lr=lr, li=li, advance=advance):
            xr, xi = advance(i, x)
            s_scr[rows(i), lr] = xr
            s_scr[rows(i), li] = xi
            return xr, xi

        lax.fori_loop(0, seg, rerun, entering)


def _s5_scratch(tm):
    return [pltpu.VMEM((tm, 2 * S5_N), F32), pltpu.VMEM((SUBLANES, S5_N), F32),
            pltpu.VMEM((8, SUBLANES, S5_N), F32), pltpu.VMEM((SUBLANES, 2 * S5_N), F32)]


def _s5_put(scr, j, val):
    scr[:, j * S5_BS:(j + 1) * S5_BS] = val[:, :S5_BS]
    scr[:, S5_N + j * S5_BS:S5_N + (j + 1) * S5_BS] = val[:, S5_BS:]


def _s5_get(scr, j):
    return jnp.concatenate([scr[:, j * S5_BS:(j + 1) * S5_BS], scr[:, S5_N + j * S5_BS:S5_N + (j + 1) * S5_BS]], axis=1)


def s5_fwd(u5, bbc, ccc, ab8, d_row, wglu, bglu_row, *, tm, name, comm=None):
    t, w = u5.shape

    def body(u_ref, bb_ref, cc_ref, ab_ref, d_ref, wg_ref, bg_ref, s_ref, ys_ref, yb_ref, s_scr, base, tab, carry):
        i = pl.program_id(0)

        @pl.when(i == 0)
        def _():
            carry[...] = jnp.zeros_like(carry)
            _s5_scan_setup(ab_ref, base, tab, tm // SUBLANES, False)

        u = u_ref[...]
        ub = u.astype(BF16)
        for j in range(S5_BLOCKS):
            uj = ub[:, j * S5_BC:(j + 1) * S5_BC]
            _s5_put(s_scr, j, _dot(uj, bb_ref[j * S5_BC:(j + 1) * S5_BC, :]))
        _s5_scan(s_scr, base, tab, carry, tm, False)
        s_ref[...] = s_scr[...]
        ys = []
        for j in range(S5_BLOCKS):
            ys.append(_dot(_s5_get(s_scr, j).astype(BF16), cc_ref[:, j * S5_BC:(j + 1) * S5_BC]))
        y = jnp.concatenate(ys, axis=1) + d_ref[...] * u
        ys_ref[...] = y
        yb1 = _gelu(y)
        tt = _dot(yb1.astype(BF16), wg_ref[...]) + bg_ref[...]
        yb_ref[...] = (yb1 * _sigmoid(tt)).astype(BF16)

    tile = lambda c: pl.BlockSpec((tm, c), lambda i: (i, 0))
    return _tiled_call(
        body, name=name, grid=(t // tm,),
        in_specs=[tile(w), _full((w, 2 * S5_BS)), _full((2 * S5_BS, w)), _full((SUBLANES, S5_N)), _full((1, w)),
                  _full((w, w)), _full((1, w))],
        out_specs=[tile(2 * S5_N), tile(w), tile(w)],
        out_shape=[jax.ShapeDtypeStruct((t, 2 * S5_N), F32), jax.ShapeDtypeStruct((t, w), F32),
                   jax.ShapeDtypeStruct((t, w), BF16)],
        scratch_shapes=_s5_scratch(tm),
        operands=(u5, bbc, ccc, ab8, d_row, wglu, bglu_row), comm=comm)


def s5_bwd(dyb, ys5, u5, s, bbc, ccc, ab8, d_row, wglu, bglu_row, *, tm, name, comm=None):
    t, w = u5.shape
    nt = t // tm
    per = tm // SUBLANES

    def body(dyb_ref, ys_ref, u_ref, s_ref, sprev_ref, bb_ref, cc_ref, ab_ref, d_ref, wg_ref, bg_ref,
             du_ref, dbb_ref, dcc_ref, dab_ref, dd_ref, dwg_ref, dbg_ref, g_scr, base, tab, carry):
        i = pl.program_id(0)

        @pl.when(i == 0)
        def _():
            carry[...] = jnp.zeros_like(carry)
            _s5_scan_setup(ab_ref, base, tab, tm // SUBLANES, True)
            for r in (dbb_ref, dcc_ref, dab_ref, dd_ref, dwg_ref, dbg_ref):
                r[...] = jnp.zeros_like(r)

        y = ys_ref[...]
        u = u_ref[...]
        yb1 = _gelu(y)
        yb1b = yb1.astype(BF16)
        sg = _sigmoid(_dot(yb1b, wg_ref[...]) + bg_ref[...])
        dybv = dyb_ref[...]
        dtt = dybv * yb1 * sg * (1.0 - sg)
        dttb = dtt.astype(BF16)
        dyb1 = dybv * sg + _dot_nt(dttb, wg_ref[...])
        dwg_ref[...] += _dot_tn(yb1b, dttb)
        dbg_ref[...] += jnp.sum(dtt, axis=0, keepdims=True)
        dy = dyb1 * _gelu_grad(y)
        dd_ref[...] += jnp.sum(dy * u, axis=0, keepdims=True)
        dyh = dy.astype(BF16)
        for j in range(S5_BLOCKS):
            cs = slice(j * S5_BC, (j + 1) * S5_BC)
            _s5_put(g_scr, j, _dot_nt(dyh[:, cs], cc_ref[:, cs]))
        _s5_scan(g_scr, base, tab, carry, tm, True)
        ub = u.astype(BF16)
        dus = []
        for j in range(S5_BLOCKS):
            cs = slice(j * S5_BC, (j + 1) * S5_BC)
            gb = _s5_get(g_scr, j).astype(BF16)
            dus.append(_dot_nt(gb, bb_ref[cs, :]))
            dbb_ref[cs, :] += _dot_tn(ub[:, cs], gb)
            dcc_ref[:, cs] += _dot_tn(_s5_get(s_ref, j).astype(BF16), dyh[:, cs])
        du_ref[...] = (jnp.concatenate(dus, axis=1) + d_ref[...] * dy).astype(BF16)
        top = lax.broadcasted_iota(jnp.int32, (SUBLANES, S5_BS), 0) == 0

        def corr(g_sl, s_sl):
            before = jnp.where(i == nt - 1, 0.0, sprev_ref[SUBLANES - 1:SUBLANES, s_sl])
            wrap = jnp.where(top, before, pltpu.roll(s_ref[tm - SUBLANES:tm, s_sl], 1, 0))
            return (jnp.sum(g_scr[SUBLANES:tm, g_sl] * s_ref[0:tm - SUBLANES, s_sl], axis=0, keepdims=True)
                    + jnp.sum(g_scr[0:SUBLANES, g_sl] * wrap, axis=0, keepdims=True))

        for j in range(S5_BLOCKS):
            lr = slice(j * S5_BS, (j + 1) * S5_BS)
            li = slice(S5_N + j * S5_BS, S5_N + (j + 1) * S5_BS)
            dab_ref[0:1, lr] += corr(lr, lr) + corr(li, li)
            dab_ref[1:2, lr] += corr(li, lr) - corr(lr, li)

    tile = lambda c: pl.BlockSpec((tm, c), lambda i: (nt - 1 - i, 0))
    acc = lambda r, c: jax.ShapeDtypeStruct((r, c), F32)
    return _tiled_call(
        body, name=name, grid=(nt,),
        in_specs=[tile(w), tile(w), tile(w), tile(2 * S5_N),
                  pl.BlockSpec((SUBLANES, 2 * S5_N), lambda i: (jnp.maximum((nt - 1 - i) * per - 1, 0), 0)),
                  _full((w, 2 * S5_BS)), _full((2 * S5_BS, w)), _full((SUBLANES, S5_N)), _full((1, w)),
                  _full((w, w)), _full((1, w))],
        out_specs=[tile(w), _full((w, 2 * S5_BS)), _full((2 * S5_BS, w)), _full((SUBLANES, S5_N)), _full((1, w)),
                   _full((w, w)), _full((1, w))],
        out_shape=[jax.ShapeDtypeStruct((t, w), BF16), acc(w, 2 * S5_BS), acc(2 * S5_BS, w), acc(SUBLANES, S5_N),
                   acc(1, w), acc(w, w), acc(1, w)],
        scratch_shapes=_s5_scratch(tm),
        operands=(dyb, ys5, u5, s, s, bbc, ccc, ab8, d_row, wglu, bglu_row), comm=comm)


def s5_discretize(a_re, a_im, log_dt, b_re, b_im, *, name):
    n = a_re.shape[0]

    def body(ar_ref, ai_ref, dt_ref, br_ref, bi_ref, ab_ref, bbr_ref, bbi_ref):
        ar, ai, dtv = ar_ref[...], ai_ref[...], jnp.exp(dt_ref[...])
        mag = jnp.exp(ar * dtv)
        abr = mag * jnp.cos(ai * dtv)
        abi = mag * jnp.sin(ai * dtv)
        den = ar * ar + ai * ai
        nr = abr - 1.0
        cr = (nr * ar + abi * ai) / den
        ci = (abi * ar - nr * ai) / den
        ab_ref[:, 0:1] = abr
        ab_ref[:, 1:2] = abi
        br, bi = br_ref[...], bi_ref[...]
        bbr_ref[...] = cr * br - ci * bi
        bbi_ref[...] = cr * bi + ci * br

    col = jax.ShapeDtypeStruct((n, 2), F32)
    mat = jax.ShapeDtypeStruct((n, S5_GROUP), F32)
    return pl.pallas_call(body, name=name, out_shape=[col, mat, mat])(a_re, a_im, log_dt, b_re, b_im)


def s5_discretize_bwd(a_re, a_im, log_dt, b_re, b_im, dab, dbb_re, dbb_im, *, name):
    n = a_re.shape[0]

    def body(ar_ref, ai_ref, dt_ref, br_ref, bi_ref, dab_ref, dbr_ref, dbi_ref, da_ref, ddt_ref, gbr_ref, gbi_ref):
        ar, ai, dtv = ar_ref[...], ai_ref[...], jnp.exp(dt_ref[...])
        mag = jnp.exp(ar * dtv)
        abr = mag * jnp.cos(ai * dtv)
        abi = mag * jnp.sin(ai * dtv)
        den = ar * ar + ai * ai
        nr = abr - 1.0
        cr = (nr * ar + abi * ai) / den
        ci = (abi * ar - nr * ai) / den
        br, bi = br_ref[...], bi_ref[...]
        dbr, dbi = dbr_ref[...], dbi_ref[...]
        gbr_ref[...] = cr * dbr + ci * dbi
        gbi_ref[...] = cr * dbi - ci * dbr
        gcr = jnp.sum(dbr * br + dbi * bi, axis=1, keepdims=True)
        gci = jnp.sum(dbi * br - dbr * bi, axis=1, keepdims=True)
        ilr, ili = ar / den, -ai / den
        gabr = dab_ref[:, 0:1] + (ilr * gcr + ili * gci)
        gabi = dab_ref[:, 1:2] + (ilr * gci - ili * gcr)
        t1r, t1i = dtv * abr, dtv * abi
        t2r, t2i = -(cr * ilr - ci * ili), -(cr * ili + ci * ilr)
        da_ref[:, 0:1] = (t1r * gabr + t1i * gabi) + (t2r * gcr + t2i * gci)
        da_ref[:, 1:2] = (t1r * gabi - t1i * gabr) + (t2r * gci - t2i * gcr)
        lr, li = ar * abr - ai * abi, ar * abi + ai * abr
        dlog = (lr * gabr + li * gabi) * dtv
        ddt_ref[...] = jnp.sum(dlog.reshape(S5_GROUPS, S5_STATE, 1), axis=1)

    col2 = jax.ShapeDtypeStruct((n, 2), F32)
    col1 = jax.ShapeDtypeStruct((S5_GROUPS, 1), F32)
    mat = jax.ShapeDtypeStruct((n, S5_GROUP), F32)
    return pl.pallas_call(body, name=name, out_shape=[col2, col1, mat, mat])(
        a_re, a_im, log_dt, b_re, b_im, dab, dbb_re, dbb_im)


def merge_fwd(ya, yb, graw, x, wba, wbb, wout, *, tm, name):
    t, d = x.shape

    def body(ya_ref, yb_ref, g_ref, x_ref, wba_ref, wbb_ref, wo_ref, x1_ref, pa_ref, pb_ref):
        pa = _dot(ya_ref[...], wba_ref[...])
        pb = _dot(yb_ref[...], wbb_ref[...])
        gate = _sigmoid(g_ref[...])
        merged = gate[:, :d] * pa + gate[:, d:] * pb
        x1_ref[...] = x_ref[...] + _dot(merged.astype(BF16), wo_ref[...])
        pa_ref[...] = pa
        pb_ref[...] = pb

    tile = lambda c: pl.BlockSpec((tm, c), lambda i: (i, 0))
    out = jax.ShapeDtypeStruct((t, d), F32)
    return pl.pallas_call(
        body, name=name, grid=(t // tm,),
        in_specs=[tile(d), tile(S5_WIDTH), tile(2 * d), tile(d), _full((d, d)), _full((S5_WIDTH, d)), _full((d, d))],
        out_specs=[tile(d), tile(d), tile(d)], out_shape=[out, out, out],
        compiler_params=_params("parallel"),
    )(ya, yb, graw, x, wba, wbb, wout)


def merge_bwd(dx1b, graw, pa, pb, wba, wbb, wout, *, tm, name):
    t, d = pa.shape

    def body(dx_ref, g_ref, pa_ref, pb_ref, wba_ref, wbb_ref, wo_ref,
             mg_ref, dg_ref, dpa_ref, dpb_ref, dya_ref, dyb_ref):
        dm = _dot_nt(dx_ref[...], wo_ref[...])
        gate = _sigmoid(g_ref[...])
        g0, g1 = gate[:, :d], gate[:, d:]
        pa, pb = pa_ref[...], pb_ref[...]
        mg_ref[...] = (g0 * pa + g1 * pb).astype(BF16)
        dg_ref[:, :d] = (dm * pa * g0 * (1.0 - g0)).astype(BF16)
        dg_ref[:, d:] = (dm * pb * g1 * (1.0 - g1)).astype(BF16)
        dpa = (dm * g0).astype(BF16)
        dpb = (dm * g1).astype(BF16)
        dpa_ref[...] = dpa
        dpb_ref[...] = dpb
        dya_ref[...] = _dot_nt(dpa, wba_ref[...])
        dyb_ref[...] = _dot_nt(dpb, wbb_ref[...])

    tile = lambda c: pl.BlockSpec((tm, c), lambda i: (i, 0))
    sds = jax.ShapeDtypeStruct
    return pl.pallas_call(
        body, name=name, grid=(t // tm,),
        in_specs=[tile(d), tile(2 * d), tile(d), tile(d), _full((d, d)), _full((S5_WIDTH, d)), _full((d, d))],
        out_specs=[tile(d), tile(2 * d), tile(d), tile(d), tile(d), tile(S5_WIDTH)],
        out_shape=[sds((t, d), BF16), sds((t, 2 * d), BF16), sds((t, d), BF16), sds((t, d), BF16),
                   sds((t, d), F32), sds((t, S5_WIDTH), F32)],
        compiler_params=_params("parallel"),
    )(dx1b, graw, pa, pb, wba, wbb, wout)


def mlp_out_loss(a1, x1, w2, gf, target, *, tm, name):
    t, d = x1.shape
    f = a1.shape[1]

    def body(a_ref, x_ref, w_ref, g_ref, tg_ref, act_ref, dx_ref, dxb_ref, loss_ref, dg_ref):
        i = pl.program_id(0)
        av = jnp.maximum(a_ref[...], 0.0)
        act = (av * av).astype(BF16)
        act_ref[...] = act
        x2 = x_ref[...] + _dot(act, w_ref[...])
        r = lax.rsqrt(jnp.mean(x2 * x2, axis=-1, keepdims=True) + EPS)
        xn = x2 * r
        err = xn * g_ref[...] - tg_ref[...]
        dyf = err * (1.0 / d)
        dxn = dyf * g_ref[...]
        dx = r * (dxn - xn * jnp.mean(dxn * xn, axis=-1, keepdims=True))
        dx_ref[...] = dx
        dxb_ref[...] = dx.astype(BF16)

        @pl.when(i == 0)
        def _():
            loss_ref[...] = jnp.zeros_like(loss_ref)
            dg_ref[...] = jnp.zeros_like(dg_ref)

        loss_ref[...] += jnp.sum(err * err) * (0.5 / d)
        dg_ref[...] += jnp.sum(dyf * xn, axis=0, keepdims=True)

    tile = lambda c: pl.BlockSpec((tm, c), lambda i: (i, 0))
    sds = jax.ShapeDtypeStruct
    return pl.pallas_call(
        body, name=name, grid=(t // tm,),
        in_specs=[tile(f), tile(d), _full((f, d)), _full((1, d)), tile(d)],
        out_specs=[tile(f), tile(d), tile(d), _full((1, LANES)), _full((1, d))],
        out_shape=[sds((t, f), BF16), sds((t, d), F32), sds((t, d), BF16), sds((1, LANES), F32), sds((1, d), F32)],
        compiler_params=_params("arbitrary"),
    )(a1, x1, w2, gf.reshape(1, d), target)


def mlp_bwd_act(dx2b, a1, w2, *, tm, name):
    t, f = a1.shape
    d = dx2b.shape[1]

    def body(dx_ref, a_ref, w_ref, o_ref):
        dact = _dot_nt(dx_ref[...], w_ref[...])
        o_ref[...] = (dact * 2.0 * jnp.maximum(a_ref[...], 0.0)).astype(BF16)

    tile = lambda c: pl.BlockSpec((tm, c), lambda i: (i, 0))
    return pl.pallas_call(
        body, name=name, grid=(t // tm,),
        in_specs=[tile(d), tile(f), _full((f, d))], out_specs=tile(f),
        out_shape=jax.ShapeDtypeStruct((t, f), BF16),
        compiler_params=_params("parallel"),
    )(dx2b, a1, w2)


ADAM_TILE_ELEMS = 128 * 1024


def _row_tile(rows, cap):
    if rows <= cap:
        return rows
    best = None
    for c in range(16, cap + 1, 16):
        if rows % c == 0:
            best = c
    assert best is not None, rows
    return best


def _device_order_sum(parts_ref):
    total = parts_ref[0].astype(F32)
    for k in range(1, N_DEV):
        total = total + parts_ref[k].astype(F32)
    return total


def _adamw_math(w, m, v, gparts_ref):
    g = _device_order_sum(gparts_ref)
    mn = ADAM_B1 * m + (1.0 - ADAM_B1) * g
    vn = ADAM_B2 * v + (1.0 - ADAM_B2) * (g * g)
    m_hat = mn / (1.0 - ADAM_B1 ** ADAM_STEP)
    v_hat = vn / (1.0 - ADAM_B2 ** ADAM_STEP)
    return g, -ADAM_LR * (m_hat / (jnp.sqrt(v_hat) + ADAM_EPS) + ADAM_WD * w), mn, vn


def adamw(w, m, v, gparts, *, name):
    rows, cols = w.shape
    tr = _row_tile(rows, max(16, ADAM_TILE_ELEMS // cols))

    def body(w_ref, m_ref, v_ref, g_ref, go_ref, d_ref, mo_ref, vo_ref):
        go_ref[...], d_ref[...], mo_ref[...], vo_ref[...] = _adamw_math(w_ref[...], m_ref[...], v_ref[...], g_ref)

    tile = pl.BlockSpec((tr, cols), lambda i: (i, 0))
    out = jax.ShapeDtypeStruct((rows, cols), F32)
    return pl.pallas_call(
        body, name=name, grid=(rows // tr,),
        in_specs=[tile, tile, tile, pl.BlockSpec((N_DEV, tr, cols), lambda i: (0, i, 0))],
        out_specs=[tile, tile, tile, tile], out_shape=[out, out, out, out],
        compiler_params=_params("parallel"),
    )(w, m, v, gparts)


def all_gather_arrays(blocks, *, name):
    na = len(blocks)

    def body(*refs):
        x_refs, out_refs = refs[:na], refs[na:2 * na]
        send_sems, recv_sems, local_sems = refs[2 * na:]
        x, y, c, _ = _place()
        me, sibling = (x, y, c), (x, y, 1 - c)
        chips = [(1 - x, y), (x, 1 - y), (1 - x, 1 - y)]

        def copy(a, k, blk, to, own=False):
            px, py, pc = blk
            dst = out_refs[a].at[4 * px + 2 * py + pc]
            return pltpu.make_async_remote_copy(
                src_ref=x_refs[a] if own else dst, dst_ref=dst,
                send_sem=send_sems.at[7 * a + k], recv_sem=recv_sems.at[7 * a + k], device_id=to, device_id_type=_MESH)

        mine = [pltpu.make_async_copy(x_refs[a], out_refs[a].at[4 * x + 2 * y + c], local_sems.at[a]) for a in range(na)]
        for cp in mine:
            cp.start()
        sent = []
        for a in range(na):
            first = [copy(a, 0, me, sibling, own=True)]
            first += [copy(a, 1 + j, me, (*chip, c), own=True) for j, chip in enumerate(chips)]
            for cp in first:
                cp.start()
            sent += first
        for a in range(na):
            for j, chip in enumerate(chips):
                copy(a, 1 + j, (*chip, c), me).wait_recv()
                fwd = copy(a, 4 + j, (*chip, c), sibling)
                fwd.start()
                sent.append(fwd)
        for a in range(na):
            copy(a, 0, sibling, me).wait_recv()
            for j, chip in enumerate(chips):
                copy(a, 4 + j, (*chip, 1 - c), me).wait_recv()
        for cp in sent:
            cp.wait_send()
        for cp in mine:
            cp.wait()

    any_spec = pl.BlockSpec(memory_space=pl.ANY)
    return pl.pallas_call(
        body, name=name, in_specs=[any_spec] * na, out_specs=[any_spec] * na,
        out_shape=[jax.ShapeDtypeStruct((N_DEV,) + b.shape, b.dtype) for b in blocks],
        scratch_shapes=[pltpu.SemaphoreType.DMA((7 * na,)), pltpu.SemaphoreType.DMA((7 * na,)),
                        pltpu.SemaphoreType.DMA((na,))],
        compiler_params=pltpu.CompilerParams(has_side_effects=True),
    )(*blocks)


def all_to_all(comm, *, name):
    na = len(comm)
    kinds = [k for k, _ in comm]

    def body(*refs):
        local, remote = _comm_copies(kinds, refs[:na], refs[na:2 * na], *refs[2 * na:])
        for cp in local + remote:
            cp.start()
        for cp in remote:
            cp.wait_recv()
        for cp in remote:
            cp.wait_send()
        for cp in local:
            cp.wait()

    any_spec = pl.BlockSpec(memory_space=pl.ANY)
    return pl.pallas_call(
        body, name=name, in_specs=[any_spec] * na, out_specs=[any_spec] * na,
        out_shape=_comm_out_shapes(comm),
        scratch_shapes=[pltpu.SemaphoreType.DMA((na * N_REL,)), pltpu.SemaphoreType.DMA((na * N_REL,)),
                        pltpu.SemaphoreType.DMA((na,))],
        compiler_params=pltpu.CompilerParams(has_side_effects=True),
    )(*[a for _, a in comm])


def adamw_whole(ws, ms, vs, gparts, *, name, totals=()):
    n, nt = len(ws), len(totals)

    def body(*refs):
        w_refs, m_refs, v_refs, g_refs = refs[:n], refs[n:2 * n], refs[2 * n:3 * n], refs[3 * n:4 * n]
        t_refs, refs = refs[4 * n:4 * n + nt], refs[4 * n + nt:]
        outs, t_outs = refs[:4 * n], refs[4 * n:]
        for k in range(n):
            g, delta, mn, vn = _adamw_math(w_refs[k][...], m_refs[k][...], v_refs[k][...], g_refs[k])
            outs[k][...] = g
            outs[n + k][...] = delta
            outs[2 * n + k][...] = mn
            outs[3 * n + k][...] = vn
        for t_ref, o_ref in zip(t_refs, t_outs):
            o_ref[...] = _device_order_sum(t_ref)

    shapes = [jax.ShapeDtypeStruct(w.shape, F32) for w in ws]
    res = pl.pallas_call(
        body, name=name, out_shape=shapes * 4 + [jax.ShapeDtypeStruct(t.shape[1:], F32) for t in totals],
        compiler_params=pltpu.CompilerParams(vmem_limit_bytes=VMEM_LIMIT),
    )(*ws, *ms, *vs, *gparts, *totals)
    return [res[j * n:(j + 1) * n] for j in range(4)], res[4 * n:]


def _lane_row(v):
    return jnp.pad(v.astype(F32), (0, LANES - v.shape[0])).reshape(1, LANES)


def _block_diag_b(bb):
    eye = jnp.eye(SUBLANES, dtype=F32)
    bt = bb.reshape(S5_BLOCKS, 8, S5_STATE, S5_GROUP).transpose(0, 1, 3, 2)
    return (bt[:, :, :, None, :] * eye[None, :, None, :, None]).reshape(S5_WIDTH, S5_BS)


def _block_diag_c(cc):
    eye = jnp.eye(SUBLANES, dtype=F32)
    ct = cc.reshape(S5_BLOCKS, 8, S5_GROUP, S5_STATE).transpose(3, 0, 1, 2)
    return (ct[None, :, :, :, :] * eye[:, None, None, :, None]).reshape(S5_BS, S5_WIDTH)


def _diag_blocks_b(m):
    eye = jnp.eye(SUBLANES, dtype=F32)
    m5 = m.reshape(S5_BLOCKS, 8, S5_GROUP, 8, S5_STATE)
    return jnp.sum(m5 * eye[None, :, None, :, None], axis=3).transpose(0, 1, 3, 2).reshape(S5_GROUPS, S5_STATE, S5_GROUP)


def _diag_blocks_c(m):
    eye = jnp.eye(SUBLANES, dtype=F32)
    m5 = m.reshape(8, S5_STATE, S5_BLOCKS, 8, S5_GROUP)
    return jnp.sum(m5 * eye[:, None, None, :, None], axis=0).transpose(1, 2, 3, 0).reshape(S5_GROUPS, S5_GROUP, S5_STATE)


def train_step(x, target, p, m, v):
    t = x.shape[0]
    tm = min(256, t)
    bf = lambda n: p[n].astype(BF16)
    slots = lambda n, a: a.reshape((N_DEV,) + _shard_shape(n))
    updated = {}

    def update(names, got):
        for n, parts in zip(names, got):
            updated[n] = adamw(p[n], m[n], v[n], parts, name=f"adamw_{n}")

    st_in, st_conv = all_gather_arrays([bf("w_in"), p["conv_w"]], name="gather_first")
    win_p = jnp.concatenate([st_in[k][:, a:b] for seg in IN_PADDED_ORDER for k, a, b in _shard_pieces(*IN_SEGMENTS[seg])]
                            + [jnp.zeros((D_MODEL, LANES - SSD_HEADS), BF16)], axis=1)
    conv_w8 = jnp.pad(_join_shards("conv_w", st_conv), ((0, SUBLANES - SSD_CONV), (0, 0)))
    (h, xbc_raw, graw, z, u5, dtr), (st_branch, st_out, st_glu) = rms_matmul(
        x, p["norm_mix_g"], win_p, IN_SPLITS, tm=tm, name="in_proj",
        comm=gathers([bf("w_branch"), bf("w_out"), bf("s5_glu_w")]))
    w_branch = st_branch.reshape(D_MODEL + S5_WIDTH, D_MODEL)
    wba, wbb = w_branch[:D_MODEL], w_branch[D_MODEL:]
    wout, wglu = st_out.reshape(D_MODEL, D_MODEL), st_glu.reshape(S5_WIDTH, S5_WIDTH)
    xbc = conv_silu_fwd(xbc_raw, conv_w8, p["conv_b"], tm=tm, name="conv_fwd")
    bias_row, alog_row = _lane_row(p["dt_bias"]), _lane_row(p["a_log"])
    d_row = jnp.repeat(p["d_ssd"], SSD_HEADDIM).reshape(1, SSD_HEADS * SSD_HEADDIM)
    (ya, yssd, states), (w1,) = ssd_fwd(xbc, dtr, z, bias_row, alog_row, d_row, p["ssd_norm_g"], name="ssd_fwd",
                                        comm=gathers([bf("w_mlp_in")]))

    n = S5_N
    a_re_c, a_im_c = p["s5_a_re"].reshape(n, 1), p["s5_a_im"].reshape(n, 1)
    dt_c = jnp.repeat(p["s5_log_dt"], S5_STATE).reshape(n, 1)
    b_re_m, b_im_m = p["s5_b_re"].reshape(n, S5_GROUP), p["s5_b_im"].reshape(n, S5_GROUP)
    ab, bb_re, bb_im = s5_discretize(a_re_c, a_im_c, dt_c, b_re_m, b_im_m, name="s5_disc")
    ab8 = jnp.pad(ab.T, ((0, SUBLANES - 2), (0, 0)))
    bbc = jnp.concatenate([_block_diag_b(bb_re.reshape(S5_GROUPS, S5_STATE, S5_GROUP)),
                           _block_diag_b(bb_im.reshape(S5_GROUPS, S5_STATE, S5_GROUP))], axis=1).astype(BF16)
    ccc = jnp.concatenate([_block_diag_c(p["s5_c_re"]), -_block_diag_c(p["s5_c_im"])], axis=0).astype(BF16)
    s5d_row = p["s5_d"].reshape(1, S5_WIDTH)
    bglu_row = p["s5_glu_b"].reshape(1, S5_WIDTH)
    u5 = s5_interleave(u5, tm)
    (s, ys5, yb), (st_w2,) = s5_fwd(u5, bbc, ccc, ab8, s5d_row, wglu, bglu_row, tm=tm, name="s5_fwd",
                                    comm=gathers([bf("w_mlp_out")]))
    yb = s5_deinterleave(yb, tm)
    w2 = st_w2.reshape(D_FF, D_MODEL)

    x1, pa, pb = merge_fwd(ya, yb, graw, x, wba, wbb, wout, tm=tm, name="merge_fwd")
    (h2, a1), _ = rms_matmul(x1, p["norm_mlp_g"], w1, (D_FF,), tm=tm, name="mlp_in")
    act, dx2, dx2b, loss_row, dgf = mlp_out_loss(a1, x1, w2, p["norm_final_g"], target, tm=tm, name="mlp_out_loss")

    g = {}
    g["norm_final_g"] = dgf[0]
    da1 = mlp_bwd_act(dx2b, a1, w2, tm=tm, name="mlp_bwd_act")
    dw2 = slots("w_mlp_out", matmul_tn(act, dx2b, out_dtype=BF16, name="dw_mlp_out"))
    dw1 = matmul_tn(h2, da1, out_dtype=BF16, col_shards=N_DEV, name="dw_mlp_in")
    (dx1, dx1b, dgm), _ = matmul_nt_rms_bwd([da1], w1, x1, p["norm_mlp_g"], dx2, tm=tm, name="mlp_in_bwd")
    g["norm_mlp_g"] = dgm[0]
    merged, dgraw, dpa, dpb, dya, dyb = merge_bwd(dx1b, graw, pa, pb, wba, wbb, wout, tm=tm, name="merge_bwd")
    dwo = slots("w_out", matmul_tn(merged, dx1b, out_dtype=BF16, name="dw_out"))
    dwb = slots("w_branch", jnp.concatenate([matmul_tn(ya, dpa, out_dtype=BF16, name="dw_branch_a"),
                                             matmul_tn(yb, dpb, out_dtype=BF16, name="dw_branch_b")], axis=0))

    (du5, dbbc, dccc, dab, dd5, dwglu, dbglu), got = s5_bwd(
        s5_interleave(dyb, tm), ys5, u5, s, bbc, ccc, ab8, s5d_row, wglu, bglu_row, tm=tm, name="s5_bwd",
        comm=exchanges([dw2, dw1]))
    du5 = s5_deinterleave(du5, tm)
    update(["w_mlp_out", "w_mlp_in"], got)
    g["s5_glu_b"], g["s5_d"] = dbglu[0], dd5[0]
    g["s5_c_re"] = _diag_blocks_c(dccc[:S5_BS])
    g["s5_c_im"] = -_diag_blocks_c(dccc[S5_BS:])
    dbb_re = _diag_blocks_b(dbbc[:, :S5_BS]).reshape(n, S5_GROUP)
    dbb_im = _diag_blocks_b(dbbc[:, S5_BS:]).reshape(n, S5_GROUP)
    da, dlogdt, gb_re, gb_im = s5_discretize_bwd(a_re_c, a_im_c, dt_c, b_re_m, b_im_m, dab[:2].T, dbb_re, dbb_im,
                                                  name="s5_disc_bwd")
    g["s5_a_re"] = da[:, 0].reshape(S5_GROUPS, S5_STATE)
    g["s5_a_im"] = da[:, 1].reshape(S5_GROUPS, S5_STATE)
    g["s5_log_dt"] = dlogdt[:, 0]
    g["s5_b_re"] = gb_re.reshape(S5_GROUPS, S5_STATE, S5_GROUP)
    g["s5_b_im"] = gb_im.reshape(S5_GROUPS, S5_STATE, S5_GROUP)

    def update_small(names, got, name, totals=()):
        (gs, ds, nm, nv), sums = adamw_whole([p[n] for n in names], [m[n] for n in names], [v[n] for n in names], got,
                                             name=name, totals=totals)
        for k, n in enumerate(names):
            updated[n] = (gs[k], ds[k], nm[k], nv[k])
        return sums

    wide = ["s5_b_re", "s5_b_im"]
    late = ["ssd_norm_g", "d_ssd", "a_log", "dt_bias", "conv_b"]
    early = [n for n in SMALL_ORDER if n not in wide + late and n != "norm_mix_g"]
    (dz, dxbc, ddtr, dng, dd_row, dalog_row, dbias_row), got = ssd_bwd(
        dya, yssd, z, xbc, dtr, states, bias_row, alog_row, d_row, p["ssd_norm_g"], name="ssd_bwd",
        comm=exchanges([dwo, dwb, slots("s5_glu_w", dwglu.astype(BF16))])
        + gathers([g[n] for n in wide + early] + [loss_row]))
    update(["w_out", "w_branch", "s5_glu_w"], got[:3])
    update_small(wide, got[3:3 + len(wide)], "adamw_s5_b")
    loss_sum, = update_small(early, got[3 + len(wide):-1], "adamw_small_early", totals=[got[-1]])
    g["ssd_norm_g"] = dng[0]
    g["d_ssd"], g["a_log"], g["dt_bias"] = dd_row[0, :SSD_HEADS], dalog_row[0, :SSD_HEADS], dbias_row[0, :SSD_HEADS]
    dxc, dconv_b, dconv_w8 = conv_silu_bwd_a(xbc_raw, dxbc, conv_w8, p["conv_b"], tm=tm, name="conv_bwd_a")
    g["conv_b"] = dconv_b[0]
    dconv = dconv_w8[:SSD_CONV].reshape(SSD_CONV, N_DEV, CONV_DIM // N_DEV).transpose(1, 0, 2)
    dxbc_raw = conv_bwd_b(dxc, conv_w8, tm=tm, name="conv_bwd_b")
    dproj = [dxbc_raw, dgraw, dz, du5, ddtr]
    dxbc_w, dgate_w, dz_w, du5_w, ddt_w = [matmul_tn(h, piece, out_dtype=BF16, name=f"dw_in_{k}")
                                           for k, piece in enumerate(dproj)]
    by_segment = {"z": dz_w, "xbc": dxbc_w, "dt": ddt_w, "u5": du5_w, "gates": dgate_w}
    width = D_IN_PROJ // N_DEV
    dw_in = jnp.stack([jnp.concatenate(
        [by_segment[seg][:, max(k * width, lo) - lo:min((k + 1) * width, hi) - lo]
         for seg, (lo, hi) in IN_SEGMENTS.items() if max(k * width, lo) < min((k + 1) * width, hi)], axis=1)
        for k in range(N_DEV)])

    (grad_x, _, dgx), got = matmul_nt_rms_bwd(
        dproj, win_p, x, p["norm_mix_g"], dx1, tm=tm, name="in_proj_bwd",
        comm=exchanges([dw_in, dconv]) + gathers([g[n] for n in late]))
    update(["w_in", "conv_w"], got[:2])
    update_small(late, got[2:], "adamw_small_late")
    update_small(["norm_mix_g"], all_to_all(gathers([dgx[0]]), name="gather_last"), "adamw_small_last")
    return loss_sum[0, 0], grad_x, updated


WEIGHT_ORDER = ["norm_mix_g", "w_in", "conv_w", "conv_b", "dt_bias", "a_log", "d_ssd", "ssd_norm_g", "s5_a_re",
                "s5_a_im", "s5_log_dt", "s5_b_re", "s5_b_im", "s5_c_re", "s5_c_im", "s5_d", "s5_glu_w", "s5_glu_b",
                "w_branch", "w_out", "norm_mlp_g", "w_mlp_in", "w_mlp_out", "norm_final_g"]
SHARDED = {"w_in": ((D_MODEL, D_IN_PROJ), 1), "conv_w": ((SSD_CONV, CONV_DIM), 1), "s5_glu_w": ((S5_WIDTH, S5_WIDTH), 0),
           "w_branch": ((D_MODEL + S5_WIDTH, D_MODEL), 0), "w_out": ((D_MODEL, D_MODEL), 0),
           "w_mlp_in": ((D_MODEL, D_FF), 1), "w_mlp_out": ((D_FF, D_MODEL), 0)}
SHARDED_ORDER = ["w_in", "conv_w", "s5_glu_w", "w_branch", "w_out", "w_mlp_in", "w_mlp_out"]
SMALL_ORDER = [n for n in WEIGHT_ORDER if n not in SHARDED]


def _shard_shape(name):
    (r, c), ax = SHARDED[name]
    return (r, c // N_DEV) if ax == 1 else (r // N_DEV, c)


def _join_shards(name, stacked):
    (r, c), ax = SHARDED[name]
    if ax == 1:
        return stacked.transpose(1, 0, 2).reshape(r, c)
    return stacked.reshape(r, c)


def _shard_pieces(lo, hi):
    width = D_IN_PROJ // N_DEV
    out = []
    while lo < hi:
        k = lo // width
        end = min(hi, (k + 1) * width)
        out.append((k, lo - k * width, end - k * width))
        lo = end
    return out


IN_SEGMENTS = {"z": (0, 1024), "xbc": (1024, 3072), "dt": (3072, 3088), "u5": (3088, 3600), "gates": (3600, 5648)}
IN_PADDED_ORDER = ("xbc", "gates", "z", "u5", "dt")


def kernel(x, norm_mix_g, w_in, conv_w, conv_b, dt_bias, a_log, d_ssd, ssd_norm_g, s5_a_re, s5_a_im, s5_log_dt, s5_b_re, s5_b_im, s5_c_re, s5_c_im, s5_d, s5_glu_w, s5_glu_b, w_branch, w_out, norm_mlp_g, w_mlp_in, w_mlp_out, norm_final_g, loss_target, m_norm_mix_g, m_w_in, m_conv_w, m_conv_b, m_dt_bias, m_a_log, m_d_ssd, m_ssd_norm_g, m_s5_a_re, m_s5_a_im, m_s5_log_dt, m_s5_b_re, m_s5_b_im, m_s5_c_re, m_s5_c_im, m_s5_d, m_s5_glu_w, m_s5_glu_b, m_w_branch, m_w_out, m_norm_mlp_g, m_w_mlp_in, m_w_mlp_out, m_norm_final_g, v_norm_mix_g, v_w_in, v_conv_w, v_conv_b, v_dt_bias, v_a_log, v_d_ssd, v_ssd_norm_g, v_s5_a_re, v_s5_a_im, v_s5_log_dt, v_s5_b_re, v_s5_b_im, v_s5_c_re, v_s5_c_im, v_s5_d, v_s5_glu_w, v_s5_glu_b, v_w_branch, v_w_out, v_norm_mlp_g, v_w_mlp_in, v_w_mlp_out, v_norm_final_g):
    w = dict(norm_mix_g=norm_mix_g, w_in=w_in, conv_w=conv_w, conv_b=conv_b, dt_bias=dt_bias, a_log=a_log, d_ssd=d_ssd,
             ssd_norm_g=ssd_norm_g, s5_a_re=s5_a_re, s5_a_im=s5_a_im, s5_log_dt=s5_log_dt, s5_b_re=s5_b_re,
             s5_b_im=s5_b_im, s5_c_re=s5_c_re, s5_c_im=s5_c_im, s5_d=s5_d, s5_glu_w=s5_glu_w, s5_glu_b=s5_glu_b,
             w_branch=w_branch, w_out=w_out, norm_mlp_g=norm_mlp_g, w_mlp_in=w_mlp_in, w_mlp_out=w_mlp_out,
             norm_final_g=norm_final_g)
    m = dict(norm_mix_g=m_norm_mix_g, w_in=m_w_in, conv_w=m_conv_w, conv_b=m_conv_b, dt_bias=m_dt_bias, a_log=m_a_log,
             d_ssd=m_d_ssd, ssd_norm_g=m_ssd_norm_g, s5_a_re=m_s5_a_re, s5_a_im=m_s5_a_im, s5_log_dt=m_s5_log_dt,
             s5_b_re=m_s5_b_re, s5_b_im=m_s5_b_im, s5_c_re=m_s5_c_re, s5_c_im=m_s5_c_im, s5_d=m_s5_d,
             s5_glu_w=m_s5_glu_w, s5_glu_b=m_s5_glu_b, w_branch=m_w_branch, w_out=m_w_out, norm_mlp_g=m_norm_mlp_g,
             w_mlp_in=m_w_mlp_in, w_mlp_out=m_w_mlp_out, norm_final_g=m_norm_final_g)
    v = dict(norm_mix_g=v_norm_mix_g, w_in=v_w_in, conv_w=v_conv_w, conv_b=v_conv_b, dt_bias=v_dt_bias, a_log=v_a_log,
             d_ssd=v_d_ssd, ssd_norm_g=v_ssd_norm_g, s5_a_re=v_s5_a_re, s5_a_im=v_s5_a_im, s5_log_dt=v_s5_log_dt,
             s5_b_re=v_s5_b_re, s5_b_im=v_s5_b_im, s5_c_re=v_s5_c_re, s5_c_im=v_s5_c_im, s5_d=v_s5_d,
             s5_glu_w=v_s5_glu_w, s5_glu_b=v_s5_glu_b, w_branch=v_w_branch, w_out=v_w_out, norm_mlp_g=v_norm_mlp_g,
             w_mlp_in=v_w_mlp_in, w_mlp_out=v_w_mlp_out, norm_final_g=v_norm_final_g)

    loss, grad_x, upd = train_step(x[0], loss_target[0], w, m, v)
    return (loss, grad_x.reshape(x.shape), *[upd[n][j] for j in range(4) for n in WEIGHT_ORDER])
```

```python
import functools
import math

import jax
import jax.numpy as jnp
from jax import lax
from jax.experimental import pallas as pl
from jax.experimental.pallas import tpu as pltpu

F32 = jnp.float32
BF16 = jnp.bfloat16
HI = lax.Precision.HIGHEST

N_DEV = 8
D_MODEL = 1024
SSD_HEADS = 16
SSD_HEADDIM = 64
SSD_GROUPS = 4
SSD_HPG = 4
SSD_STATE = 128
SSD_CHUNK = 128
SSD_CONV = 4
CONV_DIM = 2048
S5_WIDTH = 512
S5_GROUP = 16
S5_GROUPS = 32
S5_STATE = 64
S5_N = S5_GROUPS * S5_STATE
D_FF = 4096
D_IN_PROJ = 5648
IN_SPLITS = (2048, 2048, 1024, 512, 128)
D_IN_PAD = sum(IN_SPLITS)
EPS = 1e-6
LANES = 128
SUBLANES = 8
VMEM_LIMIT = 56 * 1024 * 1024

ADAM_LR = 0.001
ADAM_B1 = 0.9
ADAM_B2 = 0.999
ADAM_EPS = 1e-08
ADAM_WD = 0.01
ADAM_STEP = 10

GELU_C = math.sqrt(2.0 / math.pi)
GELU_K = 0.044715


def _params(*sem):
    return pltpu.CompilerParams(dimension_semantics=sem, vmem_limit_bytes=VMEM_LIMIT)


def _full(shape):
    nd = len(shape)
    return pl.BlockSpec(shape, lambda *_: (0,) * nd)


_MESH = pl.DeviceIdType.MESH
_RELATIONS = [(dx, dy, dc) for dx in (0, 1) for dy in (0, 1) for dc in (0, 1)][1:]
N_REL = len(_RELATIONS)


def _place():
    x, y, c = lax.axis_index("x"), lax.axis_index("y"), lax.axis_index("c")
    return x, y, c, 4 * x + 2 * y + c


def gathers(arrays):
    return [("gather", a) for a in arrays]


def exchanges(arrays):
    return [("exchange", a) for a in arrays]


def _comm_out_shapes(comm):
    return [jax.ShapeDtypeStruct(((N_DEV,) if kind == "gather" else ()) + a.shape, a.dtype) for kind, a in comm]


def _comm_copies(kinds, src_refs, dst_refs, send_sems, recv_sems, local_sems):
    x, y, c, idx = _place()
    local, remote = [], []
    for a, (kind, src, dst) in enumerate(zip(kinds, src_refs, dst_refs)):
        local.append(pltpu.make_async_copy(src if kind == "gather" else src.at[idx], dst.at[idx], local_sems.at[a]))
        for k, (dx, dy, dc) in enumerate(_RELATIONS):
            px, py, pc = x ^ dx, y ^ dy, c ^ dc
            remote.append(pltpu.make_async_remote_copy(
                src_ref=src if kind == "gather" else src.at[4 * px + 2 * py + pc], dst_ref=dst.at[idx],
                send_sem=send_sems.at[N_REL * a + k], recv_sem=recv_sems.at[N_REL * a + k],
                device_id=(px, py, pc), device_id_type=_MESH))
    return local, remote


def _tiled_call(body, *, name, grid, in_specs, out_specs, out_shape, operands, scratch_shapes=(), comm=None):
    params = pltpu.CompilerParams(dimension_semantics=("arbitrary",), vmem_limit_bytes=VMEM_LIMIT,
                                  has_side_effects=bool(comm))
    if not comm:
        outs = pl.pallas_call(body, name=name, grid=grid, in_specs=in_specs, out_specs=out_specs, out_shape=out_shape,
                              scratch_shapes=list(scratch_shapes), compiler_params=params)(*operands)
        return outs, []
    kind = [k for k, _ in comm]
    arrays = [a for _, a in comm]
    na, n_in, n_out, n_scr = len(arrays), len(in_specs), len(out_specs), len(scratch_shapes)
    last = grid[0] - 1

    def hosted(*refs):
        ins, refs = refs[:n_in], refs[n_in:]
        cin, refs = refs[:na], refs[na:]
        outs, refs = refs[:n_out], refs[n_out:]
        cout, refs = refs[:na], refs[na:]
        scr, sems = refs[:n_scr], refs[n_scr:]
        i = pl.program_id(0)

        @pl.when(i == 0)
        def _():
            local, remote = _comm_copies(kind, cin, cout, *sems)
            for cp in local + remote:
                cp.start()

        body(*ins, *outs, *scr)

        @pl.when(i == last)
        def _():
            local, remote = _comm_copies(kind, cin, cout, *sems)
            for cp in remote:
                cp.wait_recv()
            for cp in remote:
                cp.wait_send()
            for cp in local:
                cp.wait()

    any_spec = pl.BlockSpec(memory_space=pl.ANY)
    res = pl.pallas_call(
        hosted, name=name, grid=grid,
        in_specs=list(in_specs) + [any_spec] * na, out_specs=list(out_specs) + [any_spec] * na,
        out_shape=list(out_shape) + _comm_out_shapes(comm),
        scratch_shapes=list(scratch_shapes) + [pltpu.SemaphoreType.DMA((N_REL * na,)), pltpu.SemaphoreType.DMA((N_REL * na,)),
                                               pltpu.SemaphoreType.DMA((na,))],
        compiler_params=params)(*operands, *arrays)
    return res[:n_out], res[n_out:]


def _dot(a, b):
    return jnp.dot(a, b, preferred_element_type=F32)


def _dot_nt(a, b):
    return lax.dot_general(a, b, (((1,), (1,)), ((), ())), preferred_element_type=F32)


def _dot_tn(a, b):
    return lax.dot_general(a, b, (((0,), (0,)), ((), ())), preferred_element_type=F32)


def _dot_hi(a, b):
    return jnp.dot(a, b, preferred_element_type=F32, precision=HI)


def _dot_mid(a, b):
    return jnp.dot(a, b, preferred_element_type=F32, precision=lax.Precision.HIGH)


def _dot_nt_hi(a, b):
    return lax.dot_general(a, b, (((1,), (1,)), ((), ())), preferred_element_type=F32, precision=HI)


def _dot_tn_hi(a, b):
    return lax.dot_general(a, b, (((0,), (0,)), ((), ())), preferred_element_type=F32, precision=HI)


def _sigmoid(x):
    return 1.0 / (1.0 + jnp.exp(-x))


def _softplus(x):
    return jnp.maximum(x, 0.0) + jnp.log(1.0 + jnp.exp(-jnp.abs(x)))


def _gelu(x):
    return 0.5 * x * (1.0 + jnp.tanh(GELU_C * (x + GELU_K * x * x * x)))


def _gelu_grad(x):
    th = jnp.tanh(GELU_C * (x + GELU_K * x * x * x))
    return 0.5 * (1.0 + th) + 0.5 * x * (1.0 - th * th) * GELU_C * (1.0 + 3.0 * GELU_K * x * x)


def rms_matmul(x, g, w, splits, *, tm, name, comm=None):
    t, d = x.shape
    stacked = w.ndim == 3
    n = w.shape[0] * w.shape[2] if stacked else w.shape[1]
    assert sum(splits) == n and (len(splits) == 1 or not stacked)

    def body(x_ref, g_ref, w_ref, h_ref, *o_refs):
        xv = x_ref[...]
        r = lax.rsqrt(jnp.mean(xv * xv, axis=-1, keepdims=True) + EPS)
        h = (xv * r * g_ref[...]).astype(BF16)
        h_ref[...] = h
        if stacked:
            wk = w.shape[2]
            for k in range(w.shape[0]):
                o_refs[0][:, k * wk:(k + 1) * wk] = _dot(h, w_ref[k])
            return
        off = 0
        for o_ref, width in zip(o_refs, splits):
            o_ref[...] = _dot(h, w_ref[:, off:off + width])
            off += width

    tile = lambda c: pl.BlockSpec((tm, c), lambda i: (i, 0))
    return _tiled_call(
        body, name=name, grid=(t // tm,),
        in_specs=[tile(d), _full((1, d)), _full(w.shape)],
        out_specs=[tile(d)] + [tile(c) for c in splits],
        out_shape=[jax.ShapeDtypeStruct((t, d), BF16)] + [jax.ShapeDtypeStruct((t, c), F32) for c in splits],
        operands=(x, g.reshape(1, d), w), comm=comm)


def _pick(n, cap):
    best = LANES
    for c in range(LANES, cap + 1, LANES):
        if n % c == 0:
            best = c
    return best


def matmul_tn(a, b, *, name, out_dtype=F32, col_shards=None, tk=1024):
    t, m = a.shape
    n = b.shape[1]
    tk = min(tk, t)
    tm = _pick(m, 1024)
    tn = n // col_shards if col_shards else _pick(n, 1280)
    nk = t // tk

    def body(a_ref, b_ref, o_ref, acc):
        k = pl.program_id(2)

        @pl.when(k == 0)
        def _():
            acc[...] = jnp.zeros_like(acc)

        acc[...] += _dot_tn(a_ref[...], b_ref[...])

        @pl.when(k == nk - 1)
        def _():
            o_ref[...] = acc[...].reshape(o_ref.shape).astype(out_dtype)

    if col_shards:
        out_spec = pl.BlockSpec((1, tm, tn), lambda i, j, k: (j, i, 0))
        out_shape = jax.ShapeDtypeStruct((col_shards, m, tn), out_dtype)
    else:
        out_spec = pl.BlockSpec((tm, tn), lambda i, j, k: (i, j))
        out_shape = jax.ShapeDtypeStruct((m, n), out_dtype)
    return pl.pallas_call(
        body, name=name, grid=(m // tm, n // tn, nk),
        in_specs=[pl.BlockSpec((tk, tm), lambda i, j, k: (k, i)), pl.BlockSpec((tk, tn), lambda i, j, k: (k, j))],
        out_specs=out_spec, out_shape=out_shape,
        scratch_shapes=[pltpu.VMEM((tm, tn), F32)],
        compiler_params=_params("parallel", "parallel", "arbitrary"),
    )(a, b)


def matmul_nt_rms_bwd(dys, w, x, g, dres, *, tm, name, comm=None):
    t = x.shape[0]
    stacked = w.ndim == 3
    d = w.shape[1] if stacked else w.shape[0]
    widths = [dy.shape[1] for dy in dys]
    n_dy = len(dys)

    def body(*refs):
        dy_refs = refs[:n_dy]
        w_ref, x_ref, g_ref, dres_ref, dx_ref, dxb_ref, dg_ref = refs[n_dy:]
        i = pl.program_id(0)
        if stacked:
            wk = w.shape[2]
            dh = _dot_nt(dy_refs[0][:, 0:wk], w_ref[0])
            for k in range(1, w.shape[0]):
                dh = dh + _dot_nt(dy_refs[0][:, k * wk:(k + 1) * wk], w_ref[k])
        else:
            dh, off = None, 0
            for dy_ref, width in zip(dy_refs, widths):
                part = _dot_nt(dy_ref[...], w_ref[:, off:off + width])
                dh = part if dh is None else dh + part
                off += width
        xv = x_ref[...]
        r = lax.rsqrt(jnp.mean(xv * xv, axis=-1, keepdims=True) + EPS)
        xn = xv * r
        dxn = dh * g_ref[...]
        dx = dres_ref[...] + r * (dxn - xn * jnp.mean(dxn * xn, axis=-1, keepdims=True))
        dx_ref[...] = dx
        dxb_ref[...] = dx.astype(BF16)

        @pl.when(i == 0)
        def _():
            dg_ref[...] = jnp.zeros_like(dg_ref)

        dg_ref[...] += jnp.sum(dh * xn, axis=0, keepdims=True)

    tile = lambda c: pl.BlockSpec((tm, c), lambda i: (i, 0))
    return _tiled_call(
        body, name=name, grid=(t // tm,),
        in_specs=[tile(c) for c in widths] + [_full(w.shape), tile(d), _full((1, d)), tile(d)],
        out_specs=[tile(d), tile(d), _full((1, d))],
        out_shape=[jax.ShapeDtypeStruct((t, d), F32), jax.ShapeDtypeStruct((t, d), BF16),
                   jax.ShapeDtypeStruct((1, d), F32)],
        operands=(*dys, w, x, g.reshape(1, d), dres), comm=comm)


def _conv_taps(prev8, cur, w_ref, tm):
    ext = jnp.concatenate([prev8, cur], axis=0)
    acc = cur * w_ref[3:4, :]
    for k in range(3):
        off = SUBLANES - 3 + k
        acc = acc + ext[off:off + tm, :] * w_ref[k:k + 1, :]
    return acc


def _ssd_chunk_terms(dtr, bias_row, alog_row):
    q = SSD_CHUNK
    row = lax.broadcasted_iota(jnp.int32, (q, q), 0)
    col = lax.broadcasted_iota(jnp.int32, (q, q), 1)
    ltri = (col <= row).astype(F32)
    dt = _softplus(dtr + bias_row)
    a_row = -jnp.exp(alog_row)
    la = dt * a_row
    cum = _dot_hi(ltri, la)
    cum_t = _dot_tn_hi(la, (row <= col).astype(F32))
    return dt, a_row, cum, cum_t, ltri, row, col


def _head_maps():
    di = SSD_HEADS * SSD_HEADDIM
    expand = (lax.broadcasted_iota(jnp.int32, (LANES, di), 1) // SSD_HEADDIM
              == lax.broadcasted_iota(jnp.int32, (LANES, di), 0)).astype(F32)
    collapse = (lax.broadcasted_iota(jnp.int32, (di, LANES), 0) // SSD_HEADDIM
                == lax.broadcasted_iota(jnp.int32, (di, LANES), 1)).astype(F32)
    return expand, collapse


def _ssd_decay(cum, cum_t, h, causal):
    seg = cum[:, h:h + 1] - cum_t[h:h + 1, :]
    return jnp.where(causal, jnp.exp(jnp.where(causal, seg, 0.0)), 0.0)


def _pair_block_diag(x2):
    low = lax.broadcasted_iota(jnp.int32, x2.shape, 1) < SSD_HEADDIM
    zero = jnp.zeros_like(x2)
    return jnp.concatenate([jnp.where(low, x2, zero), jnp.where(low, zero, x2)], axis=0)


def ssd_fwd(xbc_raw, conv_w8, conv_b, dtr, z, bias_row, alog_row, d_row, norm_g, *, name, comm=None):
    t = xbc_raw.shape[0]
    q, p, n = SSD_CHUNK, SSD_HEADDIM, SSD_STATE
    nc = t // q
    di = SSD_HEADS * p
    gw = di // SSD_GROUPS

    def body(raw_ref, cw_ref, cb_ref, dtr_ref, z_ref, bias_ref, alog_ref, d_ref, g_ref,
             ya_ref, yssd_ref, st_ref, state, tail, xbc_s):
        ci = pl.program_id(0)

        @pl.when(ci == 0)
        def _():
            state[...] = jnp.zeros_like(state)
            tail[...] = jnp.zeros_like(tail)

        raw = raw_ref[...]
        xc = _conv_taps(tail[...], raw, cw_ref, q) + cb_ref[...]
        tail[...] = raw[q - SUBLANES:q, :]
        xbc_s[...] = xc * _sigmoid(xc)
        st_ref[0] = state[...]
        dt, _, cum, cum_t, _, row, col = _ssd_chunk_terms(dtr_ref[...], bias_ref[...], alog_ref[...])
        causal = row >= col
        expand, _ = _head_maps()
        dt_full = _dot_mid(dt, expand)
        cum_full = _dot_mid(cum, expand)
        last_full = cum_full[q - 1:q, :]
        xs = xbc_s[:, 0:di]
        xdt = xs * dt_full
        xd = (xdt * jnp.exp(last_full - cum_full)).astype(BF16)
        xdt_b = xdt.astype(BF16)
        e_full = jnp.exp(cum_full)
        eend_full = jnp.exp(last_full)
        for g in range(SSD_GROUPS):
            slab = slice(g * gw, (g + 1) * gw)
            bg = xbc_s[:, di + g * n:di + (g + 1) * n].astype(BF16)
            cg = xbc_s[:, di + (SSD_GROUPS + g) * n:di + (SSD_GROUPS + g + 1) * n].astype(BF16)
            cb = _dot_nt(cg, bg)
            st_g = state[:, slab]
            pairs = []
            for pr in range(SSD_HPG // 2):
                h0 = g * SSD_HPG + 2 * pr
                m0 = (cb * _ssd_decay(cum, cum_t, h0, causal)).astype(BF16)
                m1 = (cb * _ssd_decay(cum, cum_t, h0 + 1, causal)).astype(BF16)
                pairs.append(_dot(jnp.concatenate([m0, m1], axis=1), _pair_block_diag(xdt_b[:, h0 * p:(h0 + 2) * p])))
            y = jnp.concatenate(pairs, axis=1) + e_full[:, slab] * _dot(cg, st_g.astype(BF16))
            yssd_ref[:, slab] = y + d_ref[:, slab] * xs[:, slab]
            state[:, slab] = eend_full[:, slab] * st_g + _dot_tn(bg, xd[:, slab])
        zv = z_ref[...]
        ya1 = yssd_ref[...] * (zv * _sigmoid(zv))
        for g in range(SSD_GROUPS):
            gsl = slice(g * gw, (g + 1) * gw)
            sl = ya1[:, gsl]
            rg = lax.rsqrt(jnp.mean(sl * sl, axis=-1, keepdims=True) + EPS)
            ya_ref[:, gsl] = (sl * rg * g_ref[:, gsl]).astype(BF16)

    blk = lambda w, j: pl.BlockSpec((q, w), lambda c: (c, j))
    return _tiled_call(
        body, name=name, grid=(nc,),
        in_specs=[blk(CONV_DIM, 0), _full((SUBLANES, CONV_DIM)), _full((1, CONV_DIM)), blk(LANES, 0), blk(di, 0),
                  _full((1, LANES)), _full((1, LANES)), _full((1, di)), _full((1, di))],
        out_specs=[blk(di, 0), blk(di, 0), pl.BlockSpec((1, n, di), lambda c: (c, 0, 0))],
        out_shape=[jax.ShapeDtypeStruct((t, di), BF16), jax.ShapeDtypeStruct((t, di), F32),
                   jax.ShapeDtypeStruct((nc, n, di), F32)],
        scratch_shapes=[pltpu.VMEM((n, di), F32), pltpu.VMEM((SUBLANES, CONV_DIM), F32), pltpu.VMEM((q, CONV_DIM), F32)],
        operands=(xbc_raw, conv_w8, conv_b.reshape(1, CONV_DIM), dtr, z, bias_row, alog_row, d_row,
                  norm_g.reshape(1, di)), comm=comm)


def ssd_bwd(dya, yssd, z, xbc_raw, conv_w8, conv_b, dtr, states, bias_row, alog_row, d_row, norm_g, *, name, comm=None):
    t = xbc_raw.shape[0]
    q, p, n = SSD_CHUNK, SSD_HEADDIM, SSD_STATE
    nc = t // q
    di = SSD_HEADS * p
    gw = di // SSD_GROUPS
    per = q // SUBLANES

    def body(dya_ref, yssd_ref, z_ref, raw_ref, prev_ref, cw_ref, cb_ref, dtr_ref, st_ref, bias_ref, alog_ref, d_ref,
             g_ref, dz_ref, draw_ref, ddtr_ref, dng_ref, dd_ref, dalog_ref, dbias_ref, dcb_ref, dcw_ref,
             dstate, dyssd, xbc_s, dxbc_ref, head):
        ci = pl.program_id(0)

        @pl.when(ci == 0)
        def _():
            dstate[...] = jnp.zeros_like(dstate)
            head[...] = jnp.zeros_like(head)
            for r in (dng_ref, dd_ref, dalog_ref, dbias_ref, dcb_ref, dcw_ref):
                r[...] = jnp.zeros_like(r)

        raw = raw_ref[...]
        prev8 = jnp.where(ci == nc - 1, 0.0, prev_ref[...])
        xc = _conv_taps(prev8, raw, cw_ref, q) + cb_ref[...]
        sig_c = _sigmoid(xc)
        xbc_s[...] = xc * sig_c

        zv = z_ref[...]
        sz = _sigmoid(zv)
        silu_z = zv * sz
        yv = yssd_ref[...]
        ya1 = yv * silu_z
        dyav = dya_ref[...]
        for g in range(SSD_GROUPS):
            gsl = slice(g * gw, (g + 1) * gw)
            sl = ya1[:, gsl]
            rg = lax.rsqrt(jnp.mean(sl * sl, axis=-1, keepdims=True) + EPS)
            ya2 = sl * rg
            dy_g = dyav[:, gsl]
            dng_ref[:, gsl] += jnp.sum(dy_g * ya2, axis=0, keepdims=True)
            dya2 = dy_g * g_ref[:, gsl]
            dya1 = rg * (dya2 - ya2 * jnp.mean(dya2 * ya2, axis=-1, keepdims=True))
            dyssd[:, gsl] = dya1 * silu_z[:, gsl]
            dz_ref[:, gsl] = (dya1 * yv[:, gsl] * (sz[:, gsl] * (1.0 + zv[:, gsl] * (1.0 - sz[:, gsl])))).astype(BF16)

        dtr_v = dtr_ref[...]
        dt, a_row, cum, cum_t, ltri, row, col = _ssd_chunk_terms(dtr_v, bias_ref[...], alog_ref[...])
        causal = row >= col
        expand, collapse = _head_maps()
        lane = lax.broadcasted_iota(jnp.int32, (1, LANES), 1)
        sub = lax.broadcasted_iota(jnp.int32, (LANES, 1), 0)
        is_last = lax.broadcasted_iota(jnp.int32, (q, 1), 0) == q - 1
        low = lax.broadcasted_iota(jnp.int32, (q, 2 * p), 1) < p
        dt_full = _dot_mid(dt, expand)
        cum_full = _dot_mid(cum, expand)
        last_full = cum_full[q - 1:q, :]
        e_full = jnp.exp(cum_full)
        eend_full = jnp.exp(last_full)
        dend_full = jnp.exp(last_full - cum_full)
        xs = xbc_s[:, 0:di]
        xdt = xs * dt_full
        xdt_b = xdt.astype(BF16)
        xd_b = (xdt * dend_full).astype(BF16)
        dy_all = dyssd[...]
        dy_b = dy_all.astype(BF16)
        dg_b = (dy_all * e_full).astype(BF16)
        dcum_mat = jnp.zeros((q, LANES), F32)
        rmat = jnp.zeros((LANES, q), F32)
        yoff_prod, bds_list, dx_list, dlast_parts = [], [], [], []
        for g in range(SSD_GROUPS):
            slab = slice(g * gw, (g + 1) * gw)
            bg = xbc_s[:, di + g * n:di + (g + 1) * n].astype(BF16)
            cg = xbc_s[:, di + (SSD_GROUPS + g) * n:di + (SSD_GROUPS + g + 1) * n].astype(BF16)
            cb = _dot_nt(cg, bg)
            st_g = st_ref[0, :, slab]
            st_b = st_g.astype(BF16)
            dsn = dstate[:, slab]
            dsn_b = dsn.astype(BF16)
            yoff_prod.append(dy_all[:, slab] * e_full[:, slab] * _dot(cg, st_b))
            dcg = _dot_nt(dg_b[:, slab], st_b)
            dstate[:, slab] = _dot_tn(cg, dg_b[:, slab]) + eend_full[:, slab] * dsn
            bds_list.append(_dot(bg, dsn_b))
            dbg = _dot_nt(xd_b[:, slab], dsn_b)
            dlast_parts.append(jnp.sum(dsn * st_g, axis=0, keepdims=True))
            dcb = jnp.zeros((q, q), F32)
            dx_pairs = []
            for pr in range(SSD_HPG // 2):
                h0 = g * SSD_HPG + 2 * pr
                ps = slice(h0 * p, (h0 + 2) * p)
                l0 = _ssd_decay(cum, cum_t, h0, causal)
                l1 = _ssd_decay(cum, cum_t, h0 + 1, causal)
                m0, m1 = cb * l0, cb * l1
                dyp = dy_b[:, ps]
                zero = jnp.zeros_like(dyp)
                dy0, dy1 = jnp.where(low, dyp, zero), jnp.where(low, zero, dyp)
                dm0 = _dot_nt(dy0, xdt_b[:, ps])
                dm1 = _dot_nt(dy1, xdt_b[:, ps])
                w0, w1 = dm0 * m0, dm1 * m1
                dcum_mat = (dcum_mat + jnp.sum(w0, axis=1, keepdims=True) * (lane == h0).astype(F32)
                            + jnp.sum(w1, axis=1, keepdims=True) * (lane == h0 + 1).astype(F32))
                rmat = (rmat + jnp.where(sub == h0, jnp.sum(w0, axis=0, keepdims=True), 0.0)
                        + jnp.where(sub == h0 + 1, jnp.sum(w1, axis=0, keepdims=True), 0.0))
                dcb = dcb + dm0 * l0 + dm1 * l1
                dx_pairs.append(_dot_tn(jnp.concatenate([m0.astype(BF16), m1.astype(BF16)], axis=0),
                                        jnp.concatenate([dy0, dy1], axis=0)))
            dx_list.append(jnp.concatenate(dx_pairs, axis=1))
            dcb_b = dcb.astype(BF16)
            dxbc_ref[:, di + g * n:di + (g + 1) * n] = dbg + _dot_tn(dcb_b, cg)
            dxbc_ref[:, di + SSD_GROUPS * n + g * n:di + SSD_GROUPS * n + (g + 1) * n] = dcg + _dot(dcb_b, bg)
        bds_all = jnp.concatenate(bds_list, axis=1)
        dx_all = jnp.concatenate(dx_list, axis=1) + dend_full * bds_all
        last_row = cum[q - 1:q, :]
        qv = _dot_mid(xdt * bds_all, collapse) * jnp.exp(last_row - cum)
        dcum_mat = dcum_mat + _dot_mid(jnp.concatenate(yoff_prod, axis=1), collapse) - qv
        dlast_full = jnp.broadcast_to(jnp.concatenate(dlast_parts, axis=1), (SUBLANES, di))
        dlast_row = _dot_mid(dlast_full, collapse)[0:1, :] * jnp.exp(last_row) + jnp.sum(qv, axis=0, keepdims=True)
        dcum_mat = dcum_mat + jnp.where(is_last, dlast_row, 0.0)
        dla = _dot_tn_hi(ltri, dcum_mat) - _dot_nt_hi((row <= col).astype(F32), rmat)
        ddt_mat = _dot_mid(dx_all * xs, collapse) + dla * a_row
        dxbc_ref[:, 0:di] = d_ref[...] * dy_all + dx_all * dt_full
        ddtr = ddt_mat * _sigmoid(dtr_v + bias_ref[...])
        ddtr_ref[...] = ddtr.astype(BF16)
        dd_full = jnp.broadcast_to(jnp.sum(dy_all * xs, axis=0, keepdims=True), (SUBLANES, di))
        dd_ref[...] += _dot_mid(dd_full, collapse)[0:1, :]
        dalog_ref[...] += jnp.sum(dla * dt, axis=0, keepdims=True) * a_row
        dbias_ref[...] += jnp.sum(ddtr, axis=0, keepdims=True)

        dxc = dxbc_ref[...] * (sig_c * (1.0 + xc * (1.0 - sig_c)))
        dcb_ref[...] += jnp.sum(dxc, axis=0, keepdims=True)
        ext = jnp.concatenate([prev8, raw], axis=0)
        taps = [jnp.sum(dxc * ext[SUBLANES - 3 + k:SUBLANES - 3 + k + q, :], axis=0, keepdims=True)
                for k in range(SSD_CONV)]
        dcw_ref[...] += jnp.concatenate(taps + [jnp.zeros((SUBLANES - SSD_CONV, CONV_DIM), F32)], axis=0)
        after = jnp.concatenate([dxc, head[...]], axis=0)
        draw = dxc * cw_ref[3:4, :]
        for j in range(1, SSD_CONV):
            draw = draw + after[j:j + q, :] * cw_ref[3 - j:4 - j, :]
        draw_ref[...] = draw.astype(BF16)
        head[...] = dxc[0:SUBLANES, :]

    blk = lambda w, j: pl.BlockSpec((q, w), lambda c: (nc - 1 - c, j))
    row_out = lambda w: jax.ShapeDtypeStruct((1, w), F32)
    return _tiled_call(
        body, name=name, grid=(nc,),
        in_specs=[blk(di, 0), blk(di, 0), blk(di, 0), blk(CONV_DIM, 0),
                  pl.BlockSpec((SUBLANES, CONV_DIM), lambda c: (jnp.maximum((nc - 1 - c) * per - 1, 0), 0)),
                  _full((SUBLANES, CONV_DIM)), _full((1, CONV_DIM)), blk(LANES, 0),
                  pl.BlockSpec((1, n, di), lambda c: (nc - 1 - c, 0, 0)),
                  _full((1, LANES)), _full((1, LANES)), _full((1, di)), _full((1, di))],
        out_specs=[blk(di, 0), blk(CONV_DIM, 0), blk(LANES, 0),
                   _full((1, di)), _full((1, LANES)), _full((1, LANES)), _full((1, LANES)),
                   _full((1, CONV_DIM)), _full((SUBLANES, CONV_DIM))],
        out_shape=[jax.ShapeDtypeStruct((t, di), BF16), jax.ShapeDtypeStruct((t, CONV_DIM), BF16),
                   jax.ShapeDtypeStruct((t, LANES), BF16), row_out(di), row_out(LANES), row_out(LANES), row_out(LANES),
                   row_out(CONV_DIM), jax.ShapeDtypeStruct((SUBLANES, CONV_DIM), F32)],
        scratch_shapes=[pltpu.VMEM((n, di), F32), pltpu.VMEM((q, di), F32), pltpu.VMEM((q, CONV_DIM), F32),
                        pltpu.VMEM((q, CONV_DIM), F32), pltpu.VMEM((SUBLANES, CONV_DIM), F32)],
        operands=(dya, yssd, z, xbc_raw, xbc_raw, conv_w8, conv_b.reshape(1, CONV_DIM), dtr, states, bias_row,
                  alog_row, d_row, norm_g.reshape(1, di)),
        comm=comm)


S5_BLOCKS = 4
S5_BC = S5_WIDTH // S5_BLOCKS
S5_BS = S5_N // S5_BLOCKS


def _s5_tables(ar, ai, reverse):
    pows = [(ar, ai)]
    for _ in range(SUBLANES - 1):
        pr, pi = pows[-1]
        pows.append((pr * ar - pi * ai, pr * ai + pi * ar))
    row = lax.broadcasted_iota(jnp.int32, (SUBLANES, ar.shape[1]), 0)
    out = []
    for k in (1, 2, 4):
        pr, pi = pows[k - 1]
        mask = (row < SUBLANES - k) if reverse else (row >= k)
        out += [jnp.where(mask, pr, 0.0), jnp.where(mask, pi, 0.0)]
    order = list(range(SUBLANES))
    if reverse:
        order = order[::-1]
    out.append(jnp.concatenate([pows[e][0] for e in order], axis=0))
    out.append(jnp.concatenate([pows[e][1] for e in order], axis=0))
    return out


def s5_interleave(a, tm):
    t, c = a.shape
    return a.reshape(t // tm, SUBLANES, tm // SUBLANES, c).transpose(0, 2, 1, 3).reshape(t, c)


def s5_deinterleave(a, tm):
    t, c = a.shape
    return a.reshape(t // tm, tm // SUBLANES, SUBLANES, c).transpose(0, 2, 1, 3).reshape(t, c)


def _s5_scan_setup(ab_ref, base, tab, seg, reverse):
    ar = ab_ref[0:1, :]
    ai = -ab_ref[1:2, :] if reverse else ab_ref[1:2, :]
    base[0:1, :] = ar
    base[1:2, :] = ai
    assert seg & (seg - 1) == 0
    pr, pi = ar, ai
    for _ in range(seg.bit_length() - 1):
        pr, pi = pr * pr - pi * pi, 2.0 * pr * pi
    for k, tv in enumerate(_s5_tables(pr, pi, reverse)):
        tab[k] = tv


def _s5_scan(s_scr, base, tab, carry, tm, reverse):
    seg = tm // SUBLANES
    width = S5_BS
    first = lax.broadcasted_iota(jnp.int32, (SUBLANES, width), 0) == (SUBLANES - 1 if reverse else 0)

    def rows(i):
        step = (seg - 1 - i) if reverse else i
        return pl.ds(pl.multiple_of(step * SUBLANES, SUBLANES), SUBLANES)

    for lc in range(S5_N // width):
        lr = slice(lc * width, (lc + 1) * width)
        li = slice(S5_N + lc * width, S5_N + (lc + 1) * width)
        ar = jnp.broadcast_to(base[0:1, lr], (SUBLANES, width))
        ai = jnp.broadcast_to(base[1:2, lr], (SUBLANES, width))

        def advance(i, x, lr=lr, li=li, ar=ar, ai=ai):
            xr, xi = x
            return ar * xr - ai * xi + s_scr[rows(i), lr], ar * xi + ai * xr + s_scr[rows(i), li]

        zero = jnp.zeros((SUBLANES, width), F32)
        fr, fi = lax.fori_loop(0, seg, advance, (zero, zero))

        tb = [tab[k, :, lr] for k in range(8)]
        for lvl, k in enumerate((1, 2, 4)):
            sh = (SUBLANES - k) if reverse else k
            pr, pi = tb[2 * lvl], tb[2 * lvl + 1]
            rr, ri = pltpu.roll(fr, sh, 0), pltpu.roll(fi, sh, 0)
            fr, fi = fr + pr * rr - pi * ri, fi + pr * ri + pi * rr
        cr, ci = carry[0:1, lr], carry[0:1, li]
        fr, fi = fr + tb[6] * cr - tb[7] * ci, fi + tb[6] * ci + tb[7] * cr
        last = 0 if reverse else SUBLANES - 1
        carry[0:1, lr] = fr[last:last + 1, :]
        carry[0:1, li] = fi[last:last + 1, :]
        sh = (SUBLANES - 1) if reverse else 1
        entering = (jnp.where(first, cr, pltpu.roll(fr, sh, 0)), jnp.where(first, ci, pltpu.roll(fi, sh, 0)))

        def rerun(i, x, lr=lr, li=li, advance=advance):
            xr, xi = advance(i, x)
            s_scr[rows(i), lr] = xr
            s_scr[rows(i), li] = xi
            return xr, xi

        lax.fori_loop(0, seg, rerun, entering)


def _s5_scratch(tm):
    return [pltpu.VMEM((tm, 2 * S5_N), F32), pltpu.VMEM((SUBLANES, S5_N), F32),
            pltpu.VMEM((8, SUBLANES, S5_N), F32), pltpu.VMEM((SUBLANES, 2 * S5_N), F32)]


def _s5_put(scr, j, val):
    scr[:, j * S5_BS:(j + 1) * S5_BS] = val[:, :S5_BS]
    scr[:, S5_N + j * S5_BS:S5_N + (j + 1) * S5_BS] = val[:, S5_BS:]


def _s5_get(scr, j):
    return jnp.concatenate([scr[:, j * S5_BS:(j + 1) * S5_BS], scr[:, S5_N + j * S5_BS:S5_N + (j + 1) * S5_BS]], axis=1)


def s5_fwd(u5, bbc, ccc, ab8, d_row, wglu, bglu_row, *, tm, name, comm=None):
    t, w = u5.shape

    def body(u_ref, bb_ref, cc_ref, ab_ref, d_ref, wg_ref, bg_ref, s_ref, ys_ref, yb_ref, s_scr, base, tab, carry):
        i = pl.program_id(0)

        @pl.when(i == 0)
        def _():
            carry[...] = jnp.zeros_like(carry)
            _s5_scan_setup(ab_ref, base, tab, tm // SUBLANES, False)

        u = u_ref[...]
        ub = u.astype(BF16)
        for j in range(S5_BLOCKS):
            uj = ub[:, j * S5_BC:(j + 1) * S5_BC]
            _s5_put(s_scr, j, _dot(uj, bb_ref[j * S5_BC:(j + 1) * S5_BC, :]))
        _s5_scan(s_scr, base, tab, carry, tm, False)
        s_ref[...] = s_scr[...]
        ys = []
        for j in range(S5_BLOCKS):
            ys.append(_dot(_s5_get(s_scr, j).astype(BF16), cc_ref[:, j * S5_BC:(j + 1) * S5_BC]))
        y = jnp.concatenate(ys, axis=1) + d_ref[...] * u
        ys_ref[...] = y
        yb1 = _gelu(y)
        tt = _dot(yb1.astype(BF16), wg_ref[...]) + bg_ref[...]
        yb_ref[...] = (yb1 * _sigmoid(tt)).astype(BF16)

    tile = lambda c: pl.BlockSpec((tm, c), lambda i: (i, 0))
    return _tiled_call(
        body, name=name, grid=(t // tm,),
        in_specs=[tile(w), _full((w, 2 * S5_BS)), _full((2 * S5_BS, w)), _full((SUBLANES, S5_N)), _full((1, w)),
                  _full((w, w)), _full((1, w))],
        out_specs=[tile(2 * S5_N), tile(w), tile(w)],
        out_shape=[jax.ShapeDtypeStruct((t, 2 * S5_N), F32), jax.ShapeDtypeStruct((t, w), F32),
                   jax.ShapeDtypeStruct((t, w), BF16)],
        scratch_shapes=_s5_scratch(tm),
        operands=(u5, bbc, ccc, ab8, d_row, wglu, bglu_row), comm=comm)


def s5_bwd(dyb, ys5, u5, s, bbc, ccc, ab8, d_row, wglu, bglu_row, *, tm, name, comm=None):
    t, w = u5.shape
    nt = t // tm
    per = tm // SUBLANES

    def body(dyb_ref, ys_ref, u_ref, s_ref, sprev_ref, bb_ref, cc_ref, ab_ref, d_ref, wg_ref, bg_ref,
             du_ref, dbb_ref, dcc_ref, dab_ref, dd_ref, dwg_ref, dbg_ref, g_scr, base, tab, carry):
        i = pl.program_id(0)

        @pl.when(i == 0)
        def _():
            carry[...] = jnp.zeros_like(carry)
            _s5_scan_setup(ab_ref, base, tab, tm // SUBLANES, True)
            for r in (dbb_ref, dcc_ref, dab_ref, dd_ref, dwg_ref, dbg_ref):
                r[...] = jnp.zeros_like(r)

        y = ys_ref[...]
        u = u_ref[...]
        yb1 = _gelu(y)
        yb1b = yb1.astype(BF16)
        sg = _sigmoid(_dot(yb1b, wg_ref[...]) + bg_ref[...])
        dybv = dyb_ref[...]
        dtt = dybv * yb1 * sg * (1.0 - sg)
        dttb = dtt.astype(BF16)
        dyb1 = dybv * sg + _dot_nt(dttb, wg_ref[...])
        dwg_ref[...] += _dot_tn(yb1b, dttb)
        dbg_ref[...] += jnp.sum(dtt, axis=0, keepdims=True)
        dy = dyb1 * _gelu_grad(y)
        dd_ref[...] += jnp.sum(dy * u, axis=0, keepdims=True)
        dyh = dy.astype(BF16)
        for j in range(S5_BLOCKS):
            cs = slice(j * S5_BC, (j + 1) * S5_BC)
            _s5_put(g_scr, j, _dot_nt(dyh[:, cs], cc_ref[:, cs]))
        _s5_scan(g_scr, base, tab, carry, tm, True)
        ub = u.astype(BF16)
        dus = []
        for j in range(S5_BLOCKS):
            cs = slice(j * S5_BC, (j + 1) * S5_BC)
            gb = _s5_get(g_scr, j).astype(BF16)
            dus.append(_dot_nt(gb, bb_ref[cs, :]))
            dbb_ref[cs, :] += _dot_tn(ub[:, cs], gb)
            dcc_ref[:, cs] += _dot_tn(_s5_get(s_ref, j).astype(BF16), dyh[:, cs])
        du_ref[...] = (jnp.concatenate(dus, axis=1) + d_ref[...] * dy).astype(BF16)
        top = lax.broadcasted_iota(jnp.int32, (SUBLANES, S5_BS), 0) == 0

        def corr(g_sl, s_sl):
            before = jnp.where(i == nt - 1, 0.0, sprev_ref[SUBLANES - 1:SUBLANES, s_sl])
            wrap = jnp.where(top, before, pltpu.roll(s_ref[tm - SUBLANES:tm, s_sl], 1, 0))
            return (jnp.sum(g_scr[SUBLANES:tm, g_sl] * s_ref[0:tm - SUBLANES, s_sl], axis=0, keepdims=True)
                    + jnp.sum(g_scr[0:SUBLANES, g_sl] * wrap, axis=0, keepdims=True))

        for j in range(S5_BLOCKS):
            lr = slice(j * S5_BS, (j + 1) * S5_BS)
            li = slice(S5_N + j * S5_BS, S5_N + (j + 1) * S5_BS)
            dab_ref[0:1, lr] += corr(lr, lr) + corr(li, li)
            dab_ref[1:2, lr] += corr(li, lr) - corr(lr, li)

    tile = lambda c: pl.BlockSpec((tm, c), lambda i: (nt - 1 - i, 0))
    acc = lambda r, c: jax.ShapeDtypeStruct((r, c), F32)
    return _tiled_call(
        body, name=name, grid=(nt,),
        in_specs=[tile(w), tile(w), tile(w), tile(2 * S5_N),
                  pl.BlockSpec((SUBLANES, 2 * S5_N), lambda i: (jnp.maximum((nt - 1 - i) * per - 1, 0), 0)),
                  _full((w, 2 * S5_BS)), _full((2 * S5_BS, w)), _full((SUBLANES, S5_N)), _full((1, w)),
                  _full((w, w)), _full((1, w))],
        out_specs=[tile(w), _full((w, 2 * S5_BS)), _full((2 * S5_BS, w)), _full((SUBLANES, S5_N)), _full((1, w)),
                   _full((w, w)), _full((1, w))],
        out_shape=[jax.ShapeDtypeStruct((t, w), BF16), acc(w, 2 * S5_BS), acc(2 * S5_BS, w), acc(SUBLANES, S5_N),
                   acc(1, w), acc(w, w), acc(1, w)],
        scratch_shapes=_s5_scratch(tm),
        operands=(dyb, ys5, u5, s, s, bbc, ccc, ab8, d_row, wglu, bglu_row), comm=comm)


def s5_discretize(a_re, a_im, log_dt, b_re, b_im, *, name):
    n = a_re.shape[0]

    def body(ar_ref, ai_ref, dt_ref, br_ref, bi_ref, ab_ref, bbr_ref, bbi_ref):
        ar, ai, dtv = ar_ref[...], ai_ref[...], jnp.exp(dt_ref[...])
        mag = jnp.exp(ar * dtv)
        abr = mag * jnp.cos(ai * dtv)
        abi = mag * jnp.sin(ai * dtv)
        den = ar * ar + ai * ai
        nr = abr - 1.0
        cr = (nr * ar + abi * ai) / den
        ci = (abi * ar - nr * ai) / den
        ab_ref[:, 0:1] = abr
        ab_ref[:, 1:2] = abi
        br, bi = br_ref[...], bi_ref[...]
        bbr_ref[...] = cr * br - ci * bi
        bbi_ref[...] = cr * bi + ci * br

    col = jax.ShapeDtypeStruct((n, 2), F32)
    mat = jax.ShapeDtypeStruct((n, S5_GROUP), F32)
    return pl.pallas_call(body, name=name, out_shape=[col, mat, mat])(a_re, a_im, log_dt, b_re, b_im)


def s5_discretize_bwd(a_re, a_im, log_dt, b_re, b_im, dab, dbb_re, dbb_im, *, name):
    n = a_re.shape[0]

    def body(ar_ref, ai_ref, dt_ref, br_ref, bi_ref, dab_ref, dbr_ref, dbi_ref, da_ref, ddt_ref, gbr_ref, gbi_ref):
        ar, ai, dtv = ar_ref[...], ai_ref[...], jnp.exp(dt_ref[...])
        mag = jnp.exp(ar * dtv)
        abr = mag * jnp.cos(ai * dtv)
        abi = mag * jnp.sin(ai * dtv)
        den = ar * ar + ai * ai
        nr = abr - 1.0
        cr = (nr * ar + abi * ai) / den
        ci = (abi * ar - nr * ai) / den
        br, bi = br_ref[...], bi_ref[...]
        dbr, dbi = dbr_ref[...], dbi_ref[...]
        gbr_ref[...] = cr * dbr + ci * dbi
        gbi_ref[...] = cr * dbi - ci * dbr
        gcr = jnp.sum(dbr * br + dbi * bi, axis=1, keepdims=True)
        gci = jnp.sum(dbi * br - dbr * bi, axis=1, keepdims=True)
        ilr, ili = ar / den, -ai / den
        gabr = dab_ref[:, 0:1] + (ilr * gcr + ili * gci)
        gabi = dab_ref[:, 1:2] + (ilr * gci - ili * gcr)
        t1r, t1i = dtv * abr, dtv * abi
        t2r, t2i = -(cr * ilr - ci * ili), -(cr * ili + ci * ilr)
        da_ref[:, 0:1] = (t1r * gabr + t1i * gabi) + (t2r * gcr + t2i * gci)
        da_ref[:, 1:2] = (t1r * gabi - t1i * gabr) + (t2r * gci - t2i * gcr)
        lr, li = ar * abr - ai * abi, ar * abi + ai * abr
        dlog = (lr * gabr + li * gabi) * dtv
        ddt_ref[...] = jnp.sum(dlog.reshape(S5_GROUPS, S5_STATE, 1), axis=1)

    col2 = jax.ShapeDtypeStruct((n, 2), F32)
    col1 = jax.ShapeDtypeStruct((S5_GROUPS, 1), F32)
    mat = jax.ShapeDtypeStruct((n, S5_GROUP), F32)
    return pl.pallas_call(body, name=name, out_shape=[col2, col1, mat, mat])(
        a_re, a_im, log_dt, b_re, b_im, dab, dbb_re, dbb_im)


def merge_fwd(ya, yb, graw, x, wba, wbb, wout, *, tm, name):
    t, d = x.shape

    def body(ya_ref, yb_ref, g_ref, x_ref, wba_ref, wbb_ref, wo_ref, x1_ref, pa_ref, pb_ref):
        pa = _dot(ya_ref[...], wba_ref[...])
        pb = _dot(yb_ref[...], wbb_ref[...])
        gate = _sigmoid(g_ref[...])
        merged = gate[:, :d] * pa + gate[:, d:] * pb
        x1_ref[...] = x_ref[...] + _dot(merged.astype(BF16), wo_ref[...])
        pa_ref[...] = pa
        pb_ref[...] = pb

    tile = lambda c: pl.BlockSpec((tm, c), lambda i: (i, 0))
    out = jax.ShapeDtypeStruct((t, d), F32)
    return pl.pallas_call(
        body, name=name, grid=(t // tm,),
        in_specs=[tile(d), tile(S5_WIDTH), tile(2 * d), tile(d), _full((d, d)), _full((S5_WIDTH, d)), _full((d, d))],
        out_specs=[tile(d), tile(d), tile(d)], out_shape=[out, out, out],
        compiler_params=_params("parallel"),
    )(ya, yb, graw, x, wba, wbb, wout)


def merge_bwd(dx1b, graw, pa, pb, wba, wbb, wout, *, tm, name):
    t, d = pa.shape

    def body(dx_ref, g_ref, pa_ref, pb_ref, wba_ref, wbb_ref, wo_ref,
             mg_ref, dg_ref, dpa_ref, dpb_ref, dya_ref, dyb_ref):
        dm = _dot_nt(dx_ref[...], wo_ref[...])
        gate = _sigmoid(g_ref[...])
        g0, g1 = gate[:, :d], gate[:, d:]
        pa, pb = pa_ref[...], pb_ref[...]
        mg_ref[...] = (g0 * pa + g1 * pb).astype(BF16)
        dg_ref[:, :d] = (dm * pa * g0 * (1.0 - g0)).astype(BF16)
        dg_ref[:, d:] = (dm * pb * g1 * (1.0 - g1)).astype(BF16)
        dpa = (dm * g0).astype(BF16)
        dpb = (dm * g1).astype(BF16)
        dpa_ref[...] = dpa
        dpb_ref[...] = dpb
        dya_ref[...] = _dot_nt(dpa, wba_ref[...])
        dyb_ref[...] = _dot_nt(dpb, wbb_ref[...])

    tile = lambda c: pl.BlockSpec((tm, c), lambda i: (i, 0))
    sds = jax.ShapeDtypeStruct
    return pl.pallas_call(
        body, name=name, grid=(t // tm,),
        in_specs=[tile(d), tile(2 * d), tile(d), tile(d), _full((d, d)), _full((S5_WIDTH, d)), _full((d, d))],
        out_specs=[tile(d), tile(2 * d), tile(d), tile(d), tile(d), tile(S5_WIDTH)],
        out_shape=[sds((t, d), BF16), sds((t, 2 * d), BF16), sds((t, d), BF16), sds((t, d), BF16),
                   sds((t, d), F32), sds((t, S5_WIDTH), F32)],
        compiler_params=_params("parallel"),
    )(dx1b, graw, pa, pb, wba, wbb, wout)


def mlp_out_loss(a1, x1, w2, gf, target, *, tm, name):
    t, d = x1.shape
    f = a1.shape[1]

    def body(a_ref, x_ref, w_ref, g_ref, tg_ref, act_ref, dx_ref, dxb_ref, loss_ref, dg_ref):
        i = pl.program_id(0)
        av = jnp.maximum(a_ref[...], 0.0)
        act = (av * av).astype(BF16)
        act_ref[...] = act
        x2 = x_ref[...] + _dot(act, w_ref[...])
        r = lax.rsqrt(jnp.mean(x2 * x2, axis=-1, keepdims=True) + EPS)
        xn = x2 * r
        err = xn * g_ref[...] - tg_ref[...]
        dyf = err * (1.0 / d)
        dxn = dyf * g_ref[...]
        dx = r * (dxn - xn * jnp.mean(dxn * xn, axis=-1, keepdims=True))
        dx_ref[...] = dx
        dxb_ref[...] = dx.astype(BF16)

        @pl.when(i == 0)
        def _():
            loss_ref[...] = jnp.zeros_like(loss_ref)
            dg_ref[...] = jnp.zeros_like(dg_ref)

        loss_ref[...] += jnp.sum(err * err) * (0.5 / d)
        dg_ref[...] += jnp.sum(dyf * xn, axis=0, keepdims=True)

    tile = lambda c: pl.BlockSpec((tm, c), lambda i: (i, 0))
    sds = jax.ShapeDtypeStruct
    return pl.pallas_call(
        body, name=name, grid=(t // tm,),
        in_specs=[tile(f), tile(d), _full((f, d)), _full((1, d)), tile(d)],
        out_specs=[tile(f), tile(d), tile(d), _full((1, LANES)), _full((1, d))],
        out_shape=[sds((t, f), BF16), sds((t, d), F32), sds((t, d), BF16), sds((1, LANES), F32), sds((1, d), F32)],
        compiler_params=_params("arbitrary"),
    )(a1, x1, w2, gf.reshape(1, d), target)


def mlp_bwd_act(dx2b, a1, w2, *, tm, name):
    t, f = a1.shape
    d = dx2b.shape[1]

    def body(dx_ref, a_ref, w_ref, o_ref):
        dact = _dot_nt(dx_ref[...], w_ref[...])
        o_ref[...] = (dact * 2.0 * jnp.maximum(a_ref[...], 0.0)).astype(BF16)

    tile = lambda c: pl.BlockSpec((tm, c), lambda i: (i, 0))
    return pl.pallas_call(
        body, name=name, grid=(t // tm,),
        in_specs=[tile(d), tile(f), _full((f, d))], out_specs=tile(f),
        out_shape=jax.ShapeDtypeStruct((t, f), BF16),
        compiler_params=_params("parallel"),
    )(dx2b, a1, w2)


ADAM_TILE_ELEMS = 128 * 1024


def _row_tile(rows, cap):
    if rows <= cap:
        return rows
    best = None
    for c in range(16, cap + 1, 16):
        if rows % c == 0:
            best = c
    assert best is not None, rows
    return best


def _device_order_sum(parts_ref):
    total = parts_ref[0].astype(F32)
    for k in range(1, N_DEV):
        total = total + parts_ref[k].astype(F32)
    return total


def _adamw_math(w, m, v, gparts_ref):
    g = _device_order_sum(gparts_ref)
    mn = ADAM_B1 * m + (1.0 - ADAM_B1) * g
    vn = ADAM_B2 * v + (1.0 - ADAM_B2) * (g * g)
    m_hat = mn / (1.0 - ADAM_B1 ** ADAM_STEP)
    v_hat = vn / (1.0 - ADAM_B2 ** ADAM_STEP)
    return g, -ADAM_LR * (m_hat / (jnp.sqrt(v_hat) + ADAM_EPS) + ADAM_WD * w), mn, vn


def adamw(w, m, v, gparts, *, name):
    rows, cols = w.shape
    tr = _row_tile(rows, max(16, ADAM_TILE_ELEMS // cols))

    def body(w_ref, m_ref, v_ref, g_ref, go_ref, d_ref, mo_ref, vo_ref):
        go_ref[...], d_ref[...], mo_ref[...], vo_ref[...] = _adamw_math(w_ref[...], m_ref[...], v_ref[...], g_ref)

    tile = pl.BlockSpec((tr, cols), lambda i: (i, 0))
    out = jax.ShapeDtypeStruct((rows, cols), F32)
    return pl.pallas_call(
        body, name=name, grid=(rows // tr,),
        in_specs=[tile, tile, tile, pl.BlockSpec((N_DEV, tr, cols), lambda i: (0, i, 0))],
        out_specs=[tile, tile, tile, tile], out_shape=[out, out, out, out],
        compiler_params=_params("parallel"),
    )(w, m, v, gparts)


def all_gather_arrays(blocks, *, name):
    na = len(blocks)

    def body(*refs):
        x_refs, out_refs = refs[:na], refs[na:2 * na]
        send_sems, recv_sems, local_sems = refs[2 * na:]
        x, y, c, _ = _place()
        me, sibling = (x, y, c), (x, y, 1 - c)
        chips = [(1 - x, y), (x, 1 - y), (1 - x, 1 - y)]

        def copy(a, k, blk, to, own=False):
            px, py, pc = blk
            dst = out_refs[a].at[4 * px + 2 * py + pc]
            return pltpu.make_async_remote_copy(
                src_ref=x_refs[a] if own else dst, dst_ref=dst,
                send_sem=send_sems.at[7 * a + k], recv_sem=recv_sems.at[7 * a + k], device_id=to, device_id_type=_MESH)

        mine = [pltpu.make_async_copy(x_refs[a], out_refs[a].at[4 * x + 2 * y + c], local_sems.at[a]) for a in range(na)]
        for cp in mine:
            cp.start()
        sent = []
        for a in range(na):
            first = [copy(a, 0, me, sibling, own=True)]
            first += [copy(a, 1 + j, me, (*chip, c), own=True) for j, chip in enumerate(chips)]
            for cp in first:
                cp.start()
            sent += first
        for a in range(na):
            for j, chip in enumerate(chips):
                copy(a, 1 + j, (*chip, c), me).wait_recv()
                fwd = copy(a, 4 + j, (*chip, c), sibling)
                fwd.start()
                sent.append(fwd)
        for a in range(na):
            copy(a, 0, sibling, me).wait_recv()
            for j, chip in enumerate(chips):
                copy(a, 4 + j, (*chip, 1 - c), me).wait_recv()
        for cp in sent:
            cp.wait_send()
        for cp in mine:
            cp.wait()

    any_spec = pl.BlockSpec(memory_space=pl.ANY)
    return pl.pallas_call(
        body, name=name, in_specs=[any_spec] * na, out_specs=[any_spec] * na,
        out_shape=[jax.ShapeDtypeStruct((N_DEV,) + b.shape, b.dtype) for b in blocks],
        scratch_shapes=[pltpu.SemaphoreType.DMA((7 * na,)), pltpu.SemaphoreType.DMA((7 * na,)),
                        pltpu.SemaphoreType.DMA((na,))],
        compiler_params=pltpu.CompilerParams(has_side_effects=True),
    )(*blocks)


def all_to_all(comm, *, name):
    na = len(comm)
    kinds = [k for k, _ in comm]

    def body(*refs):
        local, remote = _comm_copies(kinds, refs[:na], refs[na:2 * na], *refs[2 * na:])
        for cp in local + remote:
            cp.start()
        for cp in remote:
            cp.wait_recv()
        for cp in remote:
            cp.wait_send()
        for cp in local:
            cp.wait()

    any_spec = pl.BlockSpec(memory_space=pl.ANY)
    return pl.pallas_call(
        body, name=name, in_specs=[any_spec] * na, out_specs=[any_spec] * na,
        out_shape=_comm_out_shapes(comm),
        scratch_shapes=[pltpu.SemaphoreType.DMA((na * N_REL,)), pltpu.SemaphoreType.DMA((na * N_REL,)),
                        pltpu.SemaphoreType.DMA((na,))],
        compiler_params=pltpu.CompilerParams(has_side_effects=True),
    )(*[a for _, a in comm])


def adamw_whole(ws, ms, vs, gparts, *, name, totals=()):
    n, nt = len(ws), len(totals)

    def body(*refs):
        w_refs, m_refs, v_refs, g_refs = refs[:n], refs[n:2 * n], refs[2 * n:3 * n], refs[3 * n:4 * n]
        t_refs, refs = refs[4 * n:4 * n + nt], refs[4 * n + nt:]
        outs, t_outs = refs[:4 * n], refs[4 * n:]
        for k in range(n):
            g, delta, mn, vn = _adamw_math(w_refs[k][...], m_refs[k][...], v_refs[k][...], g_refs[k])
            outs[k][...] = g
            outs[n + k][...] = delta
            outs[2 * n + k][...] = mn
            outs[3 * n + k][...] = vn
        for t_ref, o_ref in zip(t_refs, t_outs):
            o_ref[...] = _device_order_sum(t_ref)

    shapes = [jax.ShapeDtypeStruct(w.shape, F32) for w in ws]
    res = pl.pallas_call(
        body, name=name, out_shape=shapes * 4 + [jax.ShapeDtypeStruct(t.shape[1:], F32) for t in totals],
        compiler_params=pltpu.CompilerParams(vmem_limit_bytes=VMEM_LIMIT),
    )(*ws, *ms, *vs, *gparts, *totals)
    return [res[j * n:(j + 1) * n] for j in range(4)], res[4 * n:]


def _lane_row(v):
    return jnp.pad(v.astype(F32), (0, LANES - v.shape[0])).reshape(1, LANES)


def _block_diag_b(bb):
    eye = jnp.eye(SUBLANES, dtype=F32)
    bt = bb.reshape(S5_BLOCKS, 8, S5_STATE, S5_GROUP).transpose(0, 1, 3, 2)
    return (bt[:, :, :, None, :] * eye[None, :, None, :, None]).reshape(S5_WIDTH, S5_BS)


def _block_diag_c(cc):
    eye = jnp.eye(SUBLANES, dtype=F32)
    ct = cc.reshape(S5_BLOCKS, 8, S5_GROUP, S5_STATE).transpose(3, 0, 1, 2)
    return (ct[None, :, :, :, :] * eye[:, None, None, :, None]).reshape(S5_BS, S5_WIDTH)


def _diag_blocks_b(m):
    eye = jnp.eye(SUBLANES, dtype=F32)
    m5 = m.reshape(S5_BLOCKS, 8, S5_GROUP, 8, S5_STATE)
    return jnp.sum(m5 * eye[None, :, None, :, None], axis=3).transpose(0, 1, 3, 2).reshape(S5_GROUPS, S5_STATE, S5_GROUP)


def _diag_blocks_c(m):
    eye = jnp.eye(SUBLANES, dtype=F32)
    m5 = m.reshape(8, S5_STATE, S5_BLOCKS, 8, S5_GROUP)
    return jnp.sum(m5 * eye[:, None, None, :, None], axis=0).transpose(1, 2, 3, 0).reshape(S5_GROUPS, S5_GROUP, S5_STATE)


def train_step(x, target, p, m, v):
    t = x.shape[0]
    tm = min(256, t)
    bf = lambda n: p[n].astype(BF16)
    slots = lambda n, a: a.reshape((N_DEV,) + _shard_shape(n))
    updated = {}

    def update(names, got):
        for n, parts in zip(names, got):
            updated[n] = adamw(p[n], m[n], v[n], parts, name=f"adamw_{n}")

    st_in, st_conv = all_gather_arrays([bf("w_in"), p["conv_w"]], name="gather_first")
    win_p = jnp.concatenate([st_in[k][:, a:b] for seg in IN_PADDED_ORDER for k, a, b in _shard_pieces(*IN_SEGMENTS[seg])]
                            + [jnp.zeros((D_MODEL, LANES - SSD_HEADS), BF16)], axis=1)
    conv_w8 = jnp.pad(_join_shards("conv_w", st_conv), ((0, SUBLANES - SSD_CONV), (0, 0)))
    (h, xbc_raw, graw, z, u5, dtr), (st_branch, st_out, st_glu) = rms_matmul(
        x, p["norm_mix_g"], win_p, IN_SPLITS, tm=tm, name="in_proj",
        comm=gathers([bf("w_branch"), bf("w_out"), bf("s5_glu_w")]))
    w_branch = st_branch.reshape(D_MODEL + S5_WIDTH, D_MODEL)
    wba, wbb = w_branch[:D_MODEL], w_branch[D_MODEL:]
    wout, wglu = st_out.reshape(D_MODEL, D_MODEL), st_glu.reshape(S5_WIDTH, S5_WIDTH)
    bias_row, alog_row = _lane_row(p["dt_bias"]), _lane_row(p["a_log"])
    d_row = jnp.repeat(p["d_ssd"], SSD_HEADDIM).reshape(1, SSD_HEADS * SSD_HEADDIM)
    (ya, yssd, states), (w1,) = ssd_fwd(xbc_raw, conv_w8, p["conv_b"], dtr, z, bias_row, alog_row, d_row,
                                        p["ssd_norm_g"], name="ssd_fwd", comm=gathers([bf("w_mlp_in")]))

    n = S5_N
    a_re_c, a_im_c = p["s5_a_re"].reshape(n, 1), p["s5_a_im"].reshape(n, 1)
    dt_c = jnp.repeat(p["s5_log_dt"], S5_STATE).reshape(n, 1)
    b_re_m, b_im_m = p["s5_b_re"].reshape(n, S5_GROUP), p["s5_b_im"].reshape(n, S5_GROUP)
    ab, bb_re, bb_im = s5_discretize(a_re_c, a_im_c, dt_c, b_re_m, b_im_m, name="s5_disc")
    ab8 = jnp.pad(ab.T, ((0, SUBLANES - 2), (0, 0)))
    bbc = jnp.concatenate([_block_diag_b(bb_re.reshape(S5_GROUPS, S5_STATE, S5_GROUP)),
                           _block_diag_b(bb_im.reshape(S5_GROUPS, S5_STATE, S5_GROUP))], axis=1).astype(BF16)
    ccc = jnp.concatenate([_block_diag_c(p["s5_c_re"]), -_block_diag_c(p["s5_c_im"])], axis=0).astype(BF16)
    s5d_row = p["s5_d"].reshape(1, S5_WIDTH)
    bglu_row = p["s5_glu_b"].reshape(1, S5_WIDTH)
    u5 = s5_interleave(u5, tm)
    (s, ys5, yb), (st_w2,) = s5_fwd(u5, bbc, ccc, ab8, s5d_row, wglu, bglu_row, tm=tm, name="s5_fwd",
                                    comm=gathers([bf("w_mlp_out")]))
    yb = s5_deinterleave(yb, tm)
    w2 = st_w2.reshape(D_FF, D_MODEL)

    x1, pa, pb = merge_fwd(ya, yb, graw, x, wba, wbb, wout, tm=tm, name="merge_fwd")
    (h2, a1), _ = rms_matmul(x1, p["norm_mlp_g"], w1, (D_FF,), tm=tm, name="mlp_in")
    act, dx2, dx2b, loss_row, dgf = mlp_out_loss(a1, x1, w2, p["norm_final_g"], target, tm=tm, name="mlp_out_loss")

    g = {}
    g["norm_final_g"] = dgf[0]
    da1 = mlp_bwd_act(dx2b, a1, w2, tm=tm, name="mlp_bwd_act")
    dw2 = slots("w_mlp_out", matmul_tn(act, dx2b, out_dtype=BF16, name="dw_mlp_out"))
    dw1 = matmul_tn(h2, da1, out_dtype=BF16, col_shards=N_DEV, name="dw_mlp_in")
    (dx1, dx1b, dgm), _ = matmul_nt_rms_bwd([da1], w1, x1, p["norm_mlp_g"], dx2, tm=tm, name="mlp_in_bwd")
    g["norm_mlp_g"] = dgm[0]
    merged, dgraw, dpa, dpb, dya, dyb = merge_bwd(dx1b, graw, pa, pb, wba, wbb, wout, tm=tm, name="merge_bwd")
    dwo = slots("w_out", matmul_tn(merged, dx1b, out_dtype=BF16, name="dw_out"))
    dwb = slots("w_branch", jnp.concatenate([matmul_tn(ya, dpa, out_dtype=BF16, name="dw_branch_a"),
                                             matmul_tn(yb, dpb, out_dtype=BF16, name="dw_branch_b")], axis=0))

    (du5, dbbc, dccc, dab, dd5, dwglu, dbglu), got = s5_bwd(
        s5_interleave(dyb, tm), ys5, u5, s, bbc, ccc, ab8, s5d_row, wglu, bglu_row, tm=tm, name="s5_bwd",
        comm=exchanges([dw2, dw1]))
    du5 = s5_deinterleave(du5, tm)
    update(["w_mlp_out", "w_mlp_in"], got)
    g["s5_glu_b"], g["s5_d"] = dbglu[0], dd5[0]
    g["s5_c_re"] = _diag_blocks_c(dccc[:S5_BS])
    g["s5_c_im"] = -_diag_blocks_c(dccc[S5_BS:])
    dbb_re = _diag_blocks_b(dbbc[:, :S5_BS]).reshape(n, S5_GROUP)
    dbb_im = _diag_blocks_b(dbbc[:, S5_BS:]).reshape(n, S5_GROUP)
    da, dlogdt, gb_re, gb_im = s5_discretize_bwd(a_re_c, a_im_c, dt_c, b_re_m, b_im_m, dab[:2].T, dbb_re, dbb_im,
                                                  name="s5_disc_bwd")
    g["s5_a_re"] = da[:, 0].reshape(S5_GROUPS, S5_STATE)
    g["s5_a_im"] = da[:, 1].reshape(S5_GROUPS, S5_STATE)
    g["s5_log_dt"] = dlogdt[:, 0]
    g["s5_b_re"] = gb_re.reshape(S5_GROUPS, S5_STATE, S5_GROUP)
    g["s5_b_im"] = gb_im.reshape(S5_GROUPS, S5_STATE, S5_GROUP)

    def update_small(names, got, name, totals=()):
        (gs, ds, nm, nv), sums = adamw_whole([p[n] for n in names], [m[n] for n in names], [v[n] for n in names], got,
                                             name=name, totals=totals)
        for k, n in enumerate(names):
            updated[n] = (gs[k], ds[k], nm[k], nv[k])
        return sums

    wide = ["s5_b_re", "s5_b_im"]
    late = ["ssd_norm_g", "d_ssd", "a_log", "dt_bias", "conv_b"]
    early = [n for n in SMALL_ORDER if n not in wide + late and n != "norm_mix_g"]
    (dz, dxbc_raw, ddtr, dng, dd_row, dalog_row, dbias_row, dconv_b, dconv_w8), got = ssd_bwd(
        dya, yssd, z, xbc_raw, conv_w8, p["conv_b"], dtr, states, bias_row, alog_row, d_row, p["ssd_norm_g"],
        name="ssd_bwd",
        comm=exchanges([dwo, dwb, slots("s5_glu_w", dwglu.astype(BF16))])
        + gathers([g[n] for n in wide + early] + [loss_row]))
    update(["w_out", "w_branch", "s5_glu_w"], got[:3])
    update_small(wide, got[3:3 + len(wide)], "adamw_s5_b")
    loss_sum, = update_small(early, got[3 + len(wide):-1], "adamw_small_early", totals=[got[-1]])
    g["ssd_norm_g"] = dng[0]
    g["d_ssd"], g["a_log"], g["dt_bias"] = dd_row[0, :SSD_HEADS], dalog_row[0, :SSD_HEADS], dbias_row[0, :SSD_HEADS]
    g["conv_b"] = dconv_b[0]
    dconv = dconv_w8[:SSD_CONV].reshape(SSD_CONV, N_DEV, CONV_DIM // N_DEV).transpose(1, 0, 2)
    dproj =[dxbc_raw, dgraw, dz, du5, ddtr]
    dxbc_w, dgate_w, dz_w, du5_w, ddt_w = [matmul_tn(h, piece, out_dtype=BF16, name=f"dw_in_{k}")
                                           for k, piece in enumerate(dproj)]
    by_segment = {"z": dz_w, "xbc": dxbc_w, "dt": ddt_w, "u5": du5_w, "gates": dgate_w}
    width = D_IN_PROJ // N_DEV
    dw_in = jnp.stack([jnp.concatenate(
        [by_segment[seg][:, max(k * width, lo) - lo:min((k + 1) * width, hi) - lo]
         for seg, (lo, hi) in IN_SEGMENTS.items() if max(k * width, lo) < min((k + 1) * width, hi)], axis=1)
        for k in range(N_DEV)])

    (grad_x, _, dgx), got = matmul_nt_rms_bwd(
        dproj, win_p, x, p["norm_mix_g"], dx1, tm=tm, name="in_proj_bwd",
        comm=exchanges([dw_in, dconv]) + gathers([g[n] for n in late]))
    update(["w_in", "conv_w"], got[:2])
    update_small(late, got[2:], "adamw_small_late")
    update_small(["norm_mix_g"], all_to_all(gathers([dgx[0]]), name="gather_last"), "adamw_small_last")
    return loss_sum[0, 0], grad_x, updated


WEIGHT_ORDER = ["norm_mix_g", "w_in", "conv_w", "conv_b", "dt_bias", "a_log", "d_ssd", "ssd_norm_g", "s5_a_re",
                "s5_a_im", "s5_log_dt", "s5_b_re", "s5_b_im", "s5_c_re", "s5_c_im", "s5_d", "s5_glu_w", "s5_glu_b",
                "w_branch", "w_out", "norm_mlp_g", "w_mlp_in", "w_mlp_out", "norm_final_g"]
SHARDED = {"w_in": ((D_MODEL, D_IN_PROJ), 1), "conv_w": ((SSD_CONV, CONV_DIM), 1), "s5_glu_w": ((S5_WIDTH, S5_WIDTH), 0),
           "w_branch": ((D_MODEL + S5_WIDTH, D_MODEL), 0), "w_out": ((D_MODEL, D_MODEL), 0),
           "w_mlp_in": ((D_MODEL, D_FF), 1), "w_mlp_out": ((D_FF, D_MODEL), 0)}
SHARDED_ORDER = ["w_in", "conv_w", "s5_glu_w", "w_branch", "w_out", "w_mlp_in", "w_mlp_out"]
SMALL_ORDER = [n for n in WEIGHT_ORDER if n not in SHARDED]


def _shard_shape(name):
    (r, c), ax = SHARDED[name]
    return (r, c // N_DEV) if ax == 1 else (r // N_DEV, c)


def _join_shards(name, stacked):
    (r, c), ax = SHARDED[name]
    if ax == 1:
        return stacked.transpose(1, 0, 2).reshape(r, c)
    return stacked.reshape(r, c)


def _shard_pieces(lo, hi):
    width = D_IN_PROJ // N_DEV
    out = []
    while lo < hi:
        k = lo // width
        end = min(hi, (k + 1) * width)
        out.append((k, lo - k * width, end - k * width))
        lo = end
    return out


IN_SEGMENTS = {"z": (0, 1024), "xbc": (1024, 3072), "dt": (3072, 3088), "u5": (3088, 3600), "gates": (3600, 5648)}
IN_PADDED_ORDER = ("xbc", "gates", "z", "u5", "dt")


def kernel(x, norm_mix_g, w_in, conv_w, conv_b, dt_bias, a_log, d_ssd, ssd_norm_g, s5_a_re, s5_a_im, s5_log_dt, s5_b_re, s5_b_im, s5_c_re, s5_c_im, s5_d, s5_glu_w, s5_glu_b, w_branch, w_out, norm_mlp_g, w_mlp_in, w_mlp_out, norm_final_g, loss_target, m_norm_mix_g, m_w_in, m_conv_w, m_conv_b, m_dt_bias, m_a_log, m_d_ssd, m_ssd_norm_g, m_s5_a_re, m_s5_a_im, m_s5_log_dt, m_s5_b_re, m_s5_b_im, m_s5_c_re, m_s5_c_im, m_s5_d, m_s5_glu_w, m_s5_glu_b, m_w_branch, m_w_out, m_norm_mlp_g, m_w_mlp_in, m_w_mlp_out, m_norm_final_g, v_norm_mix_g, v_w_in, v_conv_w, v_conv_b, v_dt_bias, v_a_log, v_d_ssd, v_ssd_norm_g, v_s5_a_re, v_s5_a_im, v_s5_log_dt, v_s5_b_re, v_s5_b_im, v_s5_c_re, v_s5_c_im, v_s5_d, v_s5_glu_w, v_s5_glu_b, v_w_branch, v_w_out, v_norm_mlp_g, v_w_mlp_in, v_w_mlp_out, v_norm_final_g):
    w = dict(norm_mix_g=norm_mix_g, w_in=w_in, conv_w=conv_w, conv_b=conv_b, dt_bias=dt_bias, a_log=a_log, d_ssd=d_ssd,
             ssd_norm_g=ssd_norm_g, s5_a_re=s5_a_re, s5_a_im=s5_a_im, s5_log_dt=s5_log_dt, s5_b_re=s5_b_re,
             s5_b_im=s5_b_im, s5_c_re=s5_c_re, s5_c_im=s5_c_im, s5_d=s5_d, s5_glu_w=s5_glu_w, s5_glu_b=s5_glu_b,
             w_branch=w_branch, w_out=w_out, norm_mlp_g=norm_mlp_g, w_mlp_in=w_mlp_in, w_mlp_out=w_mlp_out,
             norm_final_g=norm_final_g)
    m = dict(norm_mix_g=m_norm_mix_g, w_in=m_w_in, conv_w=m_conv_w, conv_b=m_conv_b, dt_bias=m_dt_bias, a_log=m_a_log,
             d_ssd=m_d_ssd, ssd_norm_g=m_ssd_norm_g, s5_a_re=m_s5_a_re, s5_a_im=m_s5_a_im, s5_log_dt=m_s5_log_dt,
             s5_b_re=m_s5_b_re, s5_b_im=m_s5_b_im, s5_c_re=m_s5_c_re, s5_c_im=m_s5_c_im, s5_d=m_s5_d,
             s5_glu_w=m_s5_glu_w, s5_glu_b=m_s5_glu_b, w_branch=m_w_branch, w_out=m_w_out, norm_mlp_g=m_norm_mlp_g,
             w_mlp_in=m_w_mlp_in, w_mlp_out=m_w_mlp_out, norm_final_g=m_norm_final_g)
    v = dict(norm_mix_g=v_norm_mix_g, w_in=v_w_in, conv_w=v_conv_w, conv_b=v_conv_b, dt_bias=v_dt_bias, a_log=v_a_log,
             d_ssd=v_d_ssd, ssd_norm_g=v_ssd_norm_g, s5_a_re=v_s5_a_re, s5_a_im=v_s5_a_im, s5_log_dt=v_s5_log_dt,
             s5_b_re=v_s5_b_re, s5_b_im=v_s5_b_im, s5_c_re=v_s5_c_re, s5_c_im=v_s5_c_im, s5_d=v_s5_d,
             s5_glu_w=v_s5_glu_w, s5_glu_b=v_s5_glu_b, w_branch=v_w_branch, w_out=v_w_out, norm_mlp_g=v_norm_mlp_g,
             w_mlp_in=v_w_mlp_in, w_mlp_out=v_w_mlp_out, norm_final_g=v_norm_final_g)

    loss, grad_x, upd = train_step(x[0], loss_target[0], w, m, v)
    return (loss, grad_x.reshape(x.shape), *[upd[n][j] for j in range(4) for n in WEIGHT_ORDER])
```

```python
import functools
import math

import jax
import jax.numpy as jnp
from jax import lax
from jax.experimental import pallas as pl
from jax.experimental.pallas import tpu as pltpu

F32 = jnp.float32
BF16 = jnp.bfloat16
HI = lax.Precision.HIGHEST

N_DEV = 8
D_MODEL = 1024
SSD_HEADS = 16
SSD_HEADDIM = 64
SSD_GROUPS = 4
SSD_HPG = 4
SSD_STATE = 128
SSD_CHUNK = 128
SSD_CONV = 4
CONV_DIM = 2048
S5_WIDTH = 512
S5_GROUP = 16
S5_GROUPS = 32
S5_STATE = 64
S5_N = S5_GROUPS * S5_STATE
D_FF = 4096
D_IN_PROJ = 5648
IN_SPLITS = (2048, 2048, 1024, 512, 128)
D_IN_PAD = sum(IN_SPLITS)
EPS = 1e-6
LANES = 128
SUBLANES = 8
VMEM_LIMIT = 56 * 1024 * 1024

ADAM_LR = 0.001
ADAM_B1 = 0.9
ADAM_B2 = 0.999
ADAM_EPS = 1e-08
ADAM_WD = 0.01
ADAM_STEP = 10

GELU_C = math.sqrt(2.0 / math.pi)
GELU_K = 0.044715


def _params(*sem):
    return pltpu.CompilerParams(dimension_semantics=sem, vmem_limit_bytes=VMEM_LIMIT)


def _full(shape):
    nd = len(shape)
    return pl.BlockSpec(shape, lambda *_: (0,) * nd)


_MESH = pl.DeviceIdType.MESH
_RELATIONS = [(dx, dy, dc) for dx in (0, 1) for dy in (0, 1) for dc in (0, 1)][1:]
N_REL = len(_RELATIONS)


def _place():
    x, y, c = lax.axis_index("x"), lax.axis_index("y"), lax.axis_index("c")
    return x, y, c, 4 * x + 2 * y + c


def gathers(arrays):
    return [("gather", a) for a in arrays]


def exchanges(arrays):
    return [("exchange", a) for a in arrays]


def _comm_out_shapes(comm):
    return [jax.ShapeDtypeStruct(((N_DEV,) if kind == "gather" else ()) + a.shape, a.dtype) for kind, a in comm]


def _comm_copies(kinds, src_refs, dst_refs, send_sems, recv_sems, local_sems):
    x, y, c, idx = _place()
    local, remote = [], []
    for a, (kind, src, dst) in enumerate(zip(kinds, src_refs, dst_refs)):
        local.append(pltpu.make_async_copy(src if kind == "gather" else src.at[idx], dst.at[idx], local_sems.at[a]))
        for k, (dx, dy, dc) in enumerate(_RELATIONS):
            px, py, pc = x ^ dx, y ^ dy, c ^ dc
            remote.append(pltpu.make_async_remote_copy(
                src_ref=src if kind == "gather" else src.at[4 * px + 2 * py + pc], dst_ref=dst.at[idx],
                send_sem=send_sems.at[N_REL * a + k], recv_sem=recv_sems.at[N_REL * a + k],
                device_id=(px, py, pc), device_id_type=_MESH))
    return local, remote


def _tiled_call(body, *, name, grid, in_specs, out_specs, out_shape, operands, scratch_shapes=(), comm=None):
    params = pltpu.CompilerParams(dimension_semantics=("arbitrary",), vmem_limit_bytes=VMEM_LIMIT,
                                  has_side_effects=bool(comm))
    if not comm:
        outs = pl.pallas_call(body, name=name, grid=grid, in_specs=in_specs, out_specs=out_specs, out_shape=out_shape,
                              scratch_shapes=list(scratch_shapes), compiler_params=params)(*operands)
        return outs, []
    kind = [k for k, _ in comm]
    arrays = [a for _, a in comm]
    na, n_in, n_out, n_scr = len(arrays), len(in_specs), len(out_specs), len(scratch_shapes)
    last = grid[0] - 1

    def hosted(*refs):
        ins, refs = refs[:n_in], refs[n_in:]
        cin, refs = refs[:na], refs[na:]
        outs, refs = refs[:n_out], refs[n_out:]
        cout, refs = refs[:na], refs[na:]
        scr, sems = refs[:n_scr], refs[n_scr:]
        i = pl.program_id(0)

        @pl.when(i == 0)
        def _():
            local, remote = _comm_copies(kind, cin, cout, *sems)
            for cp in local + remote:
                cp.start()

        body(*ins, *outs, *scr)

        @pl.when(i == last)
        def _():
            local, remote = _comm_copies(kind, cin, cout, *sems)
            for cp in remote:
                cp.wait_recv()
            for cp in remote:
                cp.wait_send()
            for cp in local:
                cp.wait()

    any_spec = pl.BlockSpec(memory_space=pl.ANY)
    res = pl.pallas_call(
        hosted, name=name, grid=grid,
        in_specs=list(in_specs) + [any_spec] * na, out_specs=list(out_specs) + [any_spec] * na,
        out_shape=list(out_shape) + _comm_out_shapes(comm),
        scratch_shapes=list(scratch_shapes) + [pltpu.SemaphoreType.DMA((N_REL * na,)), pltpu.SemaphoreType.DMA((N_REL * na,)),
                                               pltpu.SemaphoreType.DMA((na,))],
        compiler_params=params)(*operands, *arrays)
    return res[:n_out], res[n_out:]


def _dot(a, b):
    return jnp.dot(a, b, preferred_element_type=F32)


def _dot_nt(a, b):
    return lax.dot_general(a, b, (((1,), (1,)), ((), ())), preferred_element_type=F32)


def _dot_tn(a, b):
    return lax.dot_general(a, b, (((0,), (0,)), ((), ())), preferred_element_type=F32)


def _dot_hi(a, b):
    return jnp.dot(a, b, preferred_element_type=F32, precision=HI)


def _dot_mid(a, b):
    return jnp.dot(a, b, preferred_element_type=F32, precision=lax.Precision.HIGH)


def _dot_nt_hi(a, b):
    return lax.dot_general(a, b, (((1,), (1,)), ((), ())), preferred_element_type=F32, precision=HI)


def _dot_tn_hi(a, b):
    return lax.dot_general(a, b, (((0,), (0,)), ((), ())), preferred_element_type=F32, precision=HI)


def _sigmoid(x):
    return 1.0 / (1.0 + jnp.exp(-x))


def _softplus(x):
    return jnp.maximum(x, 0.0) + jnp.log(1.0 + jnp.exp(-jnp.abs(x)))


def _gelu(x):
    return 0.5 * x * (1.0 + jnp.tanh(GELU_C * (x + GELU_K * x * x * x)))


def _gelu_grad(x):
    th = jnp.tanh(GELU_C * (x + GELU_K * x * x * x))
    return 0.5 * (1.0 + th) + 0.5 * x * (1.0 - th * th) * GELU_C * (1.0 + 3.0 * GELU_K * x * x)


def rms_matmul(x, g, w, splits, *, tm, name, comm=None, out_dtypes=None):
    t, d = x.shape
    stacked = w.ndim == 3
    n = w.shape[0] * w.shape[2] if stacked else w.shape[1]
    assert sum(splits) == n and (len(splits) == 1 or not stacked)

    def body(x_ref, g_ref, w_ref, h_ref, *o_refs):
        xv = x_ref[...]
        r = lax.rsqrt(jnp.mean(xv * xv, axis=-1, keepdims=True) + EPS)
        h = (xv * r * g_ref[...]).astype(BF16)
        h_ref[...] = h
        if stacked:
            wk = w.shape[2]
            for k in range(w.shape[0]):
                o_refs[0][:, k * wk:(k + 1) * wk] = _dot(h, w_ref[k]).astype(o_refs[0].dtype)
            return
        off = 0
        for o_ref, width in zip(o_refs, splits):
            o_ref[...] = _dot(h, w_ref[:, off:off + width]).astype(o_ref.dtype)
            off += width

    tile = lambda c: pl.BlockSpec((tm, c), lambda i: (i, 0))
    dtypes = out_dtypes or (F32,) * len(splits)
    return _tiled_call(
        body, name=name, grid=(t // tm,),
        in_specs=[tile(d), _full((1, d)), _full(w.shape)],
        out_specs=[tile(d)] + [tile(c) for c in splits],
        out_shape=[jax.ShapeDtypeStruct((t, d), BF16)] + [jax.ShapeDtypeStruct((t, c), dt) for c, dt in zip(splits, dtypes)],
        operands=(x, g.reshape(1, d), w), comm=comm)


def _pick(n, cap):
    best = LANES
    for c in range(LANES, cap + 1, LANES):
        if n % c == 0:
            best = c
    return best


def matmul_tn(a, b, *, name, out_dtype=F32, col_shards=None, tk=1024, a_relu2=False):
    t, m = a.shape
    n = b.shape[1]
    tk = min(tk, t)
    tm = _pick(m, 1024)
    tn = n // col_shards if col_shards else _pick(n, 1280)
    nk = t // tk

    def body(a_ref, b_ref, o_ref, acc):
        k = pl.program_id(2)

        @pl.when(k == 0)
        def _():
            acc[...] = jnp.zeros_like(acc)

        av = a_ref[...]
        if a_relu2:
            pos = jnp.maximum(av.astype(F32), 0.0)
            av = (pos * pos).astype(BF16)
        acc[...] += _dot_tn(av, b_ref[...])

        @pl.when(k == nk - 1)
        def _():
            o_ref[...] = acc[...].reshape(o_ref.shape).astype(out_dtype)

    if col_shards:
        out_spec = pl.BlockSpec((1, tm, tn), lambda i, j, k: (j, i, 0))
        out_shape = jax.ShapeDtypeStruct((col_shards, m, tn), out_dtype)
    else:
        out_spec = pl.BlockSpec((tm, tn), lambda i, j, k: (i, j))
        out_shape = jax.ShapeDtypeStruct((m, n), out_dtype)
    return pl.pallas_call(
        body, name=name, grid=(m // tm, n // tn, nk),
        in_specs=[pl.BlockSpec((tk, tm), lambda i, j, k: (k, i)), pl.BlockSpec((tk, tn), lambda i, j, k: (k, j))],
        out_specs=out_spec, out_shape=out_shape,
        scratch_shapes=[pltpu.VMEM((tm, tn), F32)],
        compiler_params=_params("parallel", "parallel", "arbitrary"),
    )(a, b)


def matmul_nt_rms_bwd(dys, w, x, g, dres, *, tm, name, comm=None):
    t = x.shape[0]
    stacked = w.ndim == 3
    d = w.shape[1] if stacked else w.shape[0]
    widths = [dy.shape[1] for dy in dys]
    n_dy = len(dys)

    def body(*refs):
        dy_refs = refs[:n_dy]
        w_ref, x_ref, g_ref, dres_ref, dx_ref, dxb_ref, dg_ref = refs[n_dy:]
        i = pl.program_id(0)
        if stacked:
            wk = w.shape[2]
            dh = _dot_nt(dy_refs[0][:, 0:wk], w_ref[0])
            for k in range(1, w.shape[0]):
                dh = dh + _dot_nt(dy_refs[0][:, k * wk:(k + 1) * wk], w_ref[k])
        else:
            dh, off = None, 0
            for dy_ref, width in zip(dy_refs, widths):
                part = _dot_nt(dy_ref[...], w_ref[:, off:off + width])
                dh = part if dh is None else dh + part
                off += width
        xv = x_ref[...]
        r = lax.rsqrt(jnp.mean(xv * xv, axis=-1, keepdims=True) + EPS)
        xn = xv * r
        dxn = dh * g_ref[...]
        dx = dres_ref[...] + r * (dxn - xn * jnp.mean(dxn * xn, axis=-1, keepdims=True))
        dx_ref[...] = dx
        dxb_ref[...] = dx.astype(BF16)

        @pl.when(i == 0)
        def _():
            dg_ref[...] = jnp.zeros_like(dg_ref)

        dg_ref[...] += jnp.sum(dh * xn, axis=0, keepdims=True)

    tile = lambda c: pl.BlockSpec((tm, c), lambda i: (i, 0))
    return _tiled_call(
        body, name=name, grid=(t // tm,),
        in_specs=[tile(c) for c in widths] + [_full(w.shape), tile(d), _full((1, d)), tile(d)],
        out_specs=[tile(d), tile(d), _full((1, d))],
        out_shape=[jax.ShapeDtypeStruct((t, d), F32), jax.ShapeDtypeStruct((t, d), BF16),
                   jax.ShapeDtypeStruct((1, d), F32)],
        operands=(*dys, w, x, g.reshape(1, d), dres), comm=comm)


def _conv_taps(ext_ref, w_ref, tm):
    acc = ext_ref[SUBLANES:SUBLANES + tm, :] * w_ref[3:4, :]
    for k in range(3):
        off = SUBLANES - 3 + k
        acc = acc + ext_ref[off:off + tm, :] * w_ref[k:k + 1, :]
    return acc


def _ssd_chunk_terms(dtr, bias_row, alog_row):
    q = SSD_CHUNK
    row = lax.broadcasted_iota(jnp.int32, (q, q), 0)
    col = lax.broadcasted_iota(jnp.int32, (q, q), 1)
    ltri = (col <= row).astype(F32)
    dt = _softplus(dtr + bias_row)
    a_row = -jnp.exp(alog_row)
    la = dt * a_row
    cum = _dot_hi(ltri, la)
    cum_t = _dot_tn_hi(la, (row <= col).astype(F32))
    return dt, a_row, cum, cum_t, ltri, row, col


def _head_maps():
    di = SSD_HEADS * SSD_HEADDIM
    expand = (lax.broadcasted_iota(jnp.int32, (LANES, di), 1) // SSD_HEADDIM
              == lax.broadcasted_iota(jnp.int32, (LANES, di), 0)).astype(F32)
    collapse = (lax.broadcasted_iota(jnp.int32, (di, LANES), 0) // SSD_HEADDIM
                == lax.broadcasted_iota(jnp.int32, (di, LANES), 1)).astype(F32)
    return expand, collapse


def _ssd_decay(cum, cum_t, h, causal):
    seg = cum[:, h:h + 1] - cum_t[h:h + 1, :]
    return jnp.where(causal, jnp.exp(jnp.where(causal, seg, 0.0)), 0.0)


def _pair_block_diag(x2):
    low = lax.broadcasted_iota(jnp.int32, x2.shape, 1) < SSD_HEADDIM
    zero = jnp.zeros_like(x2)
    return jnp.concatenate([jnp.where(low, x2, zero), jnp.where(low, zero, x2)], axis=0)


def ssd_fwd(xbc_raw, conv_w8, conv_b, dtr, z, bias_row, alog_row, d_row, norm_g, *, name, comm=None):
    t = xbc_raw.shape[0]
    q, p, n = SSD_CHUNK, SSD_HEADDIM, SSD_STATE
    nc = t // q
    di = SSD_HEADS * p
    gw = di // SSD_GROUPS

    def body(raw_ref, cw_ref, cb_ref, dtr_ref, z_ref, bias_ref, alog_ref, d_ref, g_ref,
             ya_ref, yssd_ref, st_ref, state, ext, xbc_s):
        ci = pl.program_id(0)

        @pl.when(ci == 0)
        def _():
            state[...] = jnp.zeros_like(state)
            ext[0:SUBLANES, :] = jnp.zeros((SUBLANES, CONV_DIM), F32)

        ext[SUBLANES:SUBLANES + q, :] = raw_ref[...].astype(F32)
        xc = _conv_taps(ext, cw_ref, q) + cb_ref[...]
        ext[0:SUBLANES, :] = ext[q:q + SUBLANES, :]
        xbc_s[...] = xc * _sigmoid(xc)
        st_ref[0] = state[...]
        dt, _, cum, cum_t, _, row, col = _ssd_chunk_terms(dtr_ref[...], bias_ref[...], alog_ref[...])
        causal = row >= col
        expand, _ = _head_maps()
        dt_full = _dot_mid(dt, expand)
        cum_full = _dot_mid(cum, expand)
        last_full = cum_full[q - 1:q, :]
        xs = xbc_s[:, 0:di]
        xdt = xs * dt_full
        xd = (xdt * jnp.exp(last_full - cum_full)).astype(BF16)
        xdt_b = xdt.astype(BF16)
        e_full = jnp.exp(cum_full)
        eend_full = jnp.exp(last_full)
        for g in range(SSD_GROUPS):
            slab = slice(g * gw, (g + 1) * gw)
            bg = xbc_s[:, di + g * n:di + (g + 1) * n].astype(BF16)
            cg = xbc_s[:, di + (SSD_GROUPS + g) * n:di + (SSD_GROUPS + g + 1) * n].astype(BF16)
            cb = _dot_nt(cg, bg)
            st_g = state[:, slab]
            pairs = []
            for pr in range(SSD_HPG // 2):
                h0 = g * SSD_HPG + 2 * pr
                m0 = (cb * _ssd_decay(cum, cum_t, h0, causal)).astype(BF16)
                m1 = (cb * _ssd_decay(cum, cum_t, h0 + 1, causal)).astype(BF16)
                pairs.append(_dot(jnp.concatenate([m0, m1], axis=1), _pair_block_diag(xdt_b[:, h0 * p:(h0 + 2) * p])))
            y = jnp.concatenate(pairs, axis=1) + e_full[:, slab] * _dot(cg, st_g.astype(BF16))
            yssd_ref[:, slab] = y + d_ref[:, slab] * xs[:, slab]
            state[:, slab] = eend_full[:, slab] * st_g + _dot_tn(bg, xd[:, slab])
        zv = z_ref[...].astype(F32)
        ya1 = yssd_ref[...] * (zv * _sigmoid(zv))
        for g in range(SSD_GROUPS):
            gsl = slice(g * gw, (g + 1) * gw)
            sl = ya1[:, gsl]
            rg = lax.rsqrt(jnp.mean(sl * sl, axis=-1, keepdims=True) + EPS)
            ya_ref[:, gsl] = (sl * rg * g_ref[:, gsl]).astype(BF16)

    blk = lambda w, j: pl.BlockSpec((q, w), lambda c: (c, j))
    return _tiled_call(
        body, name=name, grid=(nc,),
        in_specs=[blk(CONV_DIM, 0), _full((SUBLANES, CONV_DIM)), _full((1, CONV_DIM)), blk(LANES, 0), blk(di, 0),
                  _full((1, LANES)), _full((1, LANES)), _full((1, di)), _full((1, di))],
        out_specs=[blk(di, 0), blk(di, 0), pl.BlockSpec((1, n, di), lambda c: (c, 0, 0))],
        out_shape=[jax.ShapeDtypeStruct((t, di), BF16), jax.ShapeDtypeStruct((t, di), F32),
                   jax.ShapeDtypeStruct((nc, n, di), F32)],
        scratch_shapes=[pltpu.VMEM((n, di), F32), pltpu.VMEM((SUBLANES + q, CONV_DIM), F32), pltpu.VMEM((q, CONV_DIM), F32)],
        operands=(xbc_raw, conv_w8, conv_b.reshape(1, CONV_DIM), dtr, z, bias_row, alog_row, d_row,
                  norm_g.reshape(1, di)), comm=comm)


def ssd_bwd(dya, yssd, z, xbc_raw, conv_w8, conv_b, dtr, states, bias_row, alog_row, d_row, norm_g, *, name, comm=None):
    t = xbc_raw.shape[0]
    q, p, n = SSD_CHUNK, SSD_HEADDIM, SSD_STATE
    nc = t // q
    di = SSD_HEADS * p
    gw = di // SSD_GROUPS
    per = q // (2 * SUBLANES)

    def body(dya_ref, yssd_ref, z_ref, raw_ref, prev_ref, cw_ref, cb_ref, dtr_ref, st_ref, bias_ref, alog_ref, d_ref,
             g_ref, dz_ref, draw_ref, ddtr_ref, dng_ref, dd_ref, dalog_ref, dbias_ref, dcb_ref, dcw_ref,
             dstate, dyssd, xbc_s, dxbc_ref, ext, aft):
        ci = pl.program_id(0)

        @pl.when(ci == 0)
        def _():
            dstate[...] = jnp.zeros_like(dstate)
            aft[q:q + SUBLANES, :] = jnp.zeros((SUBLANES, CONV_DIM), F32)
            for r in (dng_ref, dd_ref, dalog_ref, dbias_ref, dcb_ref, dcw_ref):
                r[...] = jnp.zeros_like(r)

        ext[0:SUBLANES, :] = jnp.where(ci == nc - 1, 0.0, prev_ref[SUBLANES:2 * SUBLANES, :].astype(F32))
        ext[SUBLANES:SUBLANES + q, :] = raw_ref[...].astype(F32)
        xc = _conv_taps(ext, cw_ref, q) + cb_ref[...]
        sig_c = _sigmoid(xc)
        xbc_s[...] = xc * sig_c

        zv = z_ref[...].astype(F32)
        sz = _sigmoid(zv)
        silu_z = zv * sz
        yv = yssd_ref[...]
        ya1 = yv * silu_z
        dyav = dya_ref[...]
        for g in range(SSD_GROUPS):
            gsl = slice(g * gw, (g + 1) * gw)
            sl = ya1[:, gsl]
            rg = lax.rsqrt(jnp.mean(sl * sl, axis=-1, keepdims=True) + EPS)
            ya2 = sl * rg
            dy_g = dyav[:, gsl]
            dng_ref[:, gsl] += jnp.sum(dy_g * ya2, axis=0, keepdims=True)
            dya2 = dy_g * g_ref[:, gsl]
            dya1 = rg * (dya2 - ya2 * jnp.mean(dya2 * ya2, axis=-1, keepdims=True))
            dyssd[:, gsl] = dya1 * silu_z[:, gsl]
            dz_ref[:, gsl] = (dya1 * yv[:, gsl] * (sz[:, gsl] * (1.0 + zv[:, gsl] * (1.0 - sz[:, gsl])))).astype(BF16)

        dtr_v = dtr_ref[...]
        dt, a_row, cum, cum_t, ltri, row, col = _ssd_chunk_terms(dtr_v, bias_ref[...], alog_ref[...])
        causal = row >= col
        expand, collapse = _head_maps()
        lane = lax.broadcasted_iota(jnp.int32, (1, LANES), 1)
        sub = lax.broadcasted_iota(jnp.int32, (LANES, 1), 0)
        is_last = lax.broadcasted_iota(jnp.int32, (q, 1), 0) == q - 1
        low = lax.broadcasted_iota(jnp.int32, (q, 2 * p), 1) < p
        dt_full = _dot_mid(dt, expand)
        cum_full = _dot_mid(cum, expand)
        last_full = cum_full[q - 1:q, :]
        e_full = jnp.exp(cum_full)
        eend_full = jnp.exp(last_full)
        dend_full = jnp.exp(last_full - cum_full)
        xs = xbc_s[:, 0:di]
        xdt = xs * dt_full
        xdt_b = xdt.astype(BF16)
        xd_b = (xdt * dend_full).astype(BF16)
        dy_all = dyssd[...]
        dy_b = dy_all.astype(BF16)
        dg_b = (dy_all * e_full).astype(BF16)
        dcum_mat = jnp.zeros((q, LANES), F32)
        rmat = jnp.zeros((LANES, q), F32)
        yoff_prod, bds_list, dx_list, dlast_parts = [], [], [], []
        for g in range(SSD_GROUPS):
            slab = slice(g * gw, (g + 1) * gw)
            bg = xbc_s[:, di + g * n:di + (g + 1) * n].astype(BF16)
            cg = xbc_s[:, di + (SSD_GROUPS + g) * n:di + (SSD_GROUPS + g + 1) * n].astype(BF16)
            cb = _dot_nt(cg, bg)
            st_g = st_ref[0, :, slab]
            st_b = st_g.astype(BF16)
            dsn = dstate[:, slab]
            dsn_b = dsn.astype(BF16)
            yoff_prod.append(dy_all[:, slab] * e_full[:, slab] * _dot(cg, st_b))
            dcg = _dot_nt(dg_b[:, slab], st_b)
            dstate[:, slab] = _dot_tn(cg, dg_b[:, slab]) + eend_full[:, slab] * dsn
            bds_list.append(_dot(bg, dsn_b))
            dbg = _dot_nt(xd_b[:, slab], dsn_b)
            dlast_parts.append(jnp.sum(dsn * st_g, axis=0, keepdims=True))
            dcb = jnp.zeros((q, q), F32)
            dx_pairs = []
            for pr in range(SSD_HPG // 2):
                h0 = g * SSD_HPG + 2 * pr
                ps = slice(h0 * p, (h0 + 2) * p)
                l0 = _ssd_decay(cum, cum_t, h0, causal)
                l1 = _ssd_decay(cum, cum_t, h0 + 1, causal)
                m0, m1 = cb * l0, cb * l1
                dyp = dy_b[:, ps]
                zero = jnp.zeros_like(dyp)
                dy0, dy1 = jnp.where(low, dyp, zero), jnp.where(low, zero, dyp)
                dm0 = _dot_nt(dy0, xdt_b[:, ps])
                dm1 = _dot_nt(dy1, xdt_b[:, ps])
                w0, w1 = dm0 * m0, dm1 * m1
                dcum_mat = (dcum_mat + jnp.sum(w0, axis=1, keepdims=True) * (lane == h0).astype(F32)
                            + jnp.sum(w1, axis=1, keepdims=True) * (lane == h0 + 1).astype(F32))
                rmat = (rmat + jnp.where(sub == h0, jnp.sum(w0, axis=0, keepdims=True), 0.0)
                        + jnp.where(sub == h0 + 1, jnp.sum(w1, axis=0, keepdims=True), 0.0))
                dcb = dcb + dm0 * l0 + dm1 * l1
                dx_pairs.append(_dot_tn(jnp.concatenate([m0.astype(BF16), m1.astype(BF16)], axis=0),
                                        jnp.concatenate([dy0, dy1], axis=0)))
            dx_list.append(jnp.concatenate(dx_pairs, axis=1))
            dcb_b = dcb.astype(BF16)
            dxbc_ref[:, di + g * n:di + (g + 1) * n] = dbg + _dot_tn(dcb_b, cg)
            dxbc_ref[:, di + SSD_GROUPS * n + g * n:di + SSD_GROUPS * n + (g + 1) * n] = dcg + _dot(dcb_b, bg)
        bds_all = jnp.concatenate(bds_list, axis=1)
        dx_all = jnp.concatenate(dx_list, axis=1) + dend_full * bds_all
        last_row = cum[q - 1:q, :]
        qv = _dot_mid(xdt * bds_all, collapse) * jnp.exp(last_row - cum)
        dcum_mat = dcum_mat + _dot_mid(jnp.concatenate(yoff_prod, axis=1), collapse) - qv
        dlast_full = jnp.broadcast_to(jnp.concatenate(dlast_parts, axis=1), (SUBLANES, di))
        dlast_row = _dot_mid(dlast_full, collapse)[0:1, :] * jnp.exp(last_row) + jnp.sum(qv, axis=0, keepdims=True)
        dcum_mat = dcum_mat + jnp.where(is_last, dlast_row, 0.0)
        dla = _dot_tn_hi(ltri, dcum_mat) - _dot_nt_hi((row <= col).astype(F32), rmat)
        ddt_mat = _dot_mid(dx_all * xs, collapse) + dla * a_row
        dxbc_ref[:, 0:di] = d_ref[...] * dy_all + dx_all * dt_full
        ddtr = ddt_mat * _sigmoid(dtr_v + bias_ref[...])
        ddtr_ref[...] = ddtr.astype(BF16)
        dd_full = jnp.broadcast_to(jnp.sum(dy_all * xs, axis=0, keepdims=True), (SUBLANES, di))
        dd_ref[...] += _dot_mid(dd_full, collapse)[0:1, :]
        dalog_ref[...] += jnp.sum(dla * dt, axis=0, keepdims=True) * a_row
        dbias_ref[...] += jnp.sum(ddtr, axis=0, keepdims=True)

        dxc = dxbc_ref[...] * (sig_c * (1.0 + xc * (1.0 - sig_c)))
        dcb_ref[...] += jnp.sum(dxc, axis=0, keepdims=True)
        taps = [jnp.sum(dxc * ext[SUBLANES - 3 + k:SUBLANES - 3 + k + q, :], axis=0, keepdims=True)
                for k in range(SSD_CONV)]
        dcw_ref[...] += jnp.concatenate(taps + [jnp.zeros((SUBLANES - SSD_CONV, CONV_DIM), F32)], axis=0)
        aft[0:q, :] = dxc
        draw = dxc * cw_ref[3:4, :]
        for j in range(1, SSD_CONV):
            draw = draw + aft[j:j + q, :] * cw_ref[3 - j:4 - j, :]
        draw_ref[...] = draw.astype(BF16)
        aft[q:q + SUBLANES, :] = dxc[0:SUBLANES, :]

    blk = lambda w, j: pl.BlockSpec((q, w), lambda c: (nc - 1 - c, j))
    row_out = lambda w: jax.ShapeDtypeStruct((1, w), F32)
    return _tiled_call(
        body, name=name, grid=(nc,),
        in_specs=[blk(di, 0), blk(di, 0), blk(di, 0), blk(CONV_DIM, 0),
                  pl.BlockSpec((2 * SUBLANES, CONV_DIM), lambda c: (jnp.maximum((nc - 1 - c) * per - 1, 0), 0)),
                  _full((SUBLANES, CONV_DIM)), _full((1, CONV_DIM)), blk(LANES, 0),
                  pl.BlockSpec((1, n, di), lambda c: (nc - 1 - c, 0, 0)),
                  _full((1, LANES)), _full((1, LANES)), _full((1, di)), _full((1, di))],
        out_specs=[blk(di, 0), blk(CONV_DIM, 0), blk(LANES, 0),
                   _full((1, di)), _full((1, LANES)), _full((1, LANES)), _full((1, LANES)),
                   _full((1, CONV_DIM)), _full((SUBLANES, CONV_DIM))],
        out_shape=[jax.ShapeDtypeStruct((t, di), BF16), jax.ShapeDtypeStruct((t, CONV_DIM), BF16),
                   jax.ShapeDtypeStruct((t, LANES), BF16), row_out(di), row_out(LANES), row_out(LANES), row_out(LANES),
                   row_out(CONV_DIM), jax.ShapeDtypeStruct((SUBLANES, CONV_DIM), F32)],
        scratch_shapes=[pltpu.VMEM((n, di), F32), pltpu.VMEM((q, di), F32), pltpu.VMEM((q, CONV_DIM), F32),
                        pltpu.VMEM((q, CONV_DIM), F32), pltpu.VMEM((SUBLANES + q, CONV_DIM), F32),
                        pltpu.VMEM((q + SUBLANES, CONV_DIM), F32)],
        operands=(dya, yssd, z, xbc_raw, xbc_raw, conv_w8, conv_b.reshape(1, CONV_DIM), dtr, states, bias_row,
                  alog_row, d_row, norm_g.reshape(1, di)),
        comm=comm)


S5_BLOCKS = 4
S5_BC = S5_WIDTH // S5_BLOCKS
S5_BS = S5_N // S5_BLOCKS


def _s5_tables(ar, ai, reverse):
    pows = [(ar, ai)]
    for _ in range(SUBLANES - 1):
        pr, pi = pows[-1]
        pows.append((pr * ar - pi * ai, pr * ai + pi * ar))
    row = lax.broadcasted_iota(jnp.int32, (SUBLANES, ar.shape[1]), 0)
    out = []
    for k in (1, 2, 4):
        pr, pi = pows[k - 1]
        mask = (row < SUBLANES - k) if reverse else (row >= k)
        out += [jnp.where(mask, pr, 0.0), jnp.where(mask, pi, 0.0)]
    order = list(range(SUBLANES))
    if reverse:
        order = order[::-1]
    out.append(jnp.concatenate([pows[e][0] for e in order], axis=0))
    out.append(jnp.concatenate([pows[e][1] for e in order], axis=0))
    return out


def s5_interleave(a, tm):
    t, c = a.shape
    return a.reshape(t // tm, SUBLANES, tm // SUBLANES, c).transpose(0, 2, 1, 3).reshape(t, c)


def s5_deinterleave(a, tm):
    t, c = a.shape
    return a.reshape(t // tm, tm // SUBLANES, SUBLANES, c).transpose(0, 2, 1, 3).reshape(t, c)


def _s5_scan_setup(ab_ref, base, tab, seg, reverse):
    ar = ab_ref[0:1, :]
    ai = -ab_ref[1:2, :] if reverse else ab_ref[1:2, :]
    base[0:1, :] = ar
    base[1:2, :] = ai
    assert seg & (seg - 1) == 0
    pr, pi = ar, ai
    for _ in range(seg.bit_length() - 1):
        pr, pi = pr * pr - pi * pi, 2.0 * pr * pi
    for k, tv in enumerate(_s5_tables(pr, pi, reverse)):
        tab[k] = tv


def _s5_scan(s_scr, base, tab, carry, tm, reverse):
    seg = tm // SUBLANES
    width = S5_BS
    first = lax.broadcasted_iota(jnp.int32, (SUBLANES, width), 0) == (SUBLANES - 1 if reverse else 0)

    def rows(i):
        step = (seg - 1 - i) if reverse else i
        return pl.ds(pl.multiple_of(step * SUBLANES, SUBLANES), SUBLANES)

    for lc in range(S5_N // width):
        lr = slice(lc * width, (lc + 1) * width)
        li = slice(S5_N + lc * width, S5_N + (lc + 1) * width)
        ar = jnp.broadcast_to(base[0:1, lr], (SUBLANES, width))
        ai = jnp.broadcast_to(base[1:2, lr], (SUBLANES, width))

        def advance(i, x, lr=lr, li=li, ar=ar, ai=ai):
            xr, xi = x
            return ar * xr - ai * xi + s_scr[rows(i), lr], ar * xi + ai * xr + s_scr[rows(i), li]

        zero = jnp.zeros((SUBLANES, width), F32)
        fr, fi = lax.fori_loop(0, seg, advance, (zero, zero))

        tb = [tab[k, :, lr] for k in range(8)]
        for lvl, k in enumerate((1, 2, 4)):
            sh = (SUBLANES - k) if reverse else k
            pr, pi = tb[2 * lvl], tb[2 * lvl + 1]
            rr, ri = pltpu.roll(fr, sh, 0), pltpu.roll(fi, sh, 0)
            fr, fi = fr + pr * rr - pi * ri, fi + pr * ri + pi * rr
        cr, ci = carry[0:1, lr], carry[0:1, li]
        fr, fi = fr + tb[6] * cr - tb[7] * ci, fi + tb[6] * ci + tb[7] * cr
        last = 0 if reverse else SUBLANES - 1
        carry[0:1, lr] = fr[last:last + 1, :]
        carry[0:1, li] = fi[last:last + 1, :]
        sh = (SUBLANES - 1) if reverse else 1
        entering = (jnp.where(first, cr, pltpu.roll(fr, sh, 0)), jnp.where(first, ci, pltpu.roll(fi, sh, 0)))

        def rerun(i, x, lr=lr, li=li, advance=advance):
            xr, xi = advance(i, x)
            s_scr[rows(i), lr] = xr
            s_scr[rows(i), li] = xi
            return xr, xi

        lax.fori_loop(0, seg, rerun, entering)


def _s5_scratch(tm):
    return [pltpu.VMEM((tm, 2 * S5_N), F32), pltpu.VMEM((SUBLANES, S5_N), F32),
            pltpu.VMEM((8, SUBLANES, S5_N), F32), pltpu.VMEM((SUBLANES, 2 * S5_N), F32)]


def _s5_put(scr, j, val):
    scr[:, j * S5_BS:(j + 1) * S5_BS] = val[:, :S5_BS]
    scr[:, S5_N + j * S5_BS:S5_N + (j + 1) * S5_BS] = val[:, S5_BS:]


def _s5_get(scr, j):
    return jnp.concatenate([scr[:, j * S5_BS:(j + 1) * S5_BS], scr[:, S5_N + j * S5_BS:S5_N + (j + 1) * S5_BS]], axis=1)


def s5_fwd(u5, bbc, ccc, ab8, d_row, wglu, bglu_row, *, tm, name, comm=None):
    t, w = u5.shape

    def body(u_ref, bb_ref, cc_ref, ab_ref, d_ref, wg_ref, bg_ref, s_ref, ys_ref, yb_ref, s_scr, base, tab, carry):
        i = pl.program_id(0)

        @pl.when(i == 0)
        def _():
            carry[...] = jnp.zeros_like(carry)
            _s5_scan_setup(ab_ref, base, tab, tm // SUBLANES, False)

        ub = u_ref[...]
        u = ub.astype(F32)
        for j in range(S5_BLOCKS):
            uj = ub[:, j * S5_BC:(j + 1) * S5_BC]
            _s5_put(s_scr, j, _dot(uj, bb_ref[j * S5_BC:(j + 1) * S5_BC, :]))
        _s5_scan(s_scr, base, tab, carry, tm, False)
        s_ref[...] = s_scr[...]
        ys = []
        for j in range(S5_BLOCKS):
            ys.append(_dot(_s5_get(s_scr, j).astype(BF16), cc_ref[:, j * S5_BC:(j + 1) * S5_BC]))
        y = jnp.concatenate(ys, axis=1) + d_ref[...] * u
        ys_ref[...] = y
        yb1 = _gelu(y)
        tt = _dot(yb1.astype(BF16), wg_ref[...]) + bg_ref[...]
        yb_ref[...] = (yb1 * _sigmoid(tt)).astype(BF16)

    tile = lambda c: pl.BlockSpec((tm, c), lambda i: (i, 0))
    return _tiled_call(
        body, name=name, grid=(t // tm,),
        in_specs=[tile(w), _full((w, 2 * S5_BS)), _full((2 * S5_BS, w)), _full((SUBLANES, S5_N)), _full((1, w)),
                  _full((w, w)), _full((1, w))],
        out_specs=[tile(2 * S5_N), tile(w), tile(w)],
        out_shape=[jax.ShapeDtypeStruct((t, 2 * S5_N), F32), jax.ShapeDtypeStruct((t, w), F32),
                   jax.ShapeDtypeStruct((t, w), BF16)],
        scratch_shapes=_s5_scratch(tm),
        operands=(u5, bbc, ccc, ab8, d_row, wglu, bglu_row), comm=comm)


def s5_bwd(dyb, ys5, u5, s, bbc, ccc, ab8, d_row, wglu, bglu_row, *, tm, name, comm=None):
    t, w = u5.shape
    nt = t // tm
    per = tm // SUBLANES

    def body(dyb_ref, ys_ref, u_ref, s_ref, sprev_ref, bb_ref, cc_ref, ab_ref, d_ref, wg_ref, bg_ref,
             du_ref, dbb_ref, dcc_ref, dab_ref, dd_ref, dwg_ref, dbg_ref, g_scr, base, tab, carry):
        i = pl.program_id(0)

        @pl.when(i == 0)
        def _():
            carry[...] = jnp.zeros_like(carry)
            _s5_scan_setup(ab_ref, base, tab, tm // SUBLANES, True)
            for r in (dbb_ref, dcc_ref, dab_ref, dd_ref, dwg_ref, dbg_ref):
                r[...] = jnp.zeros_like(r)

        y = ys_ref[...]
        ub = u_ref[...]
        u = ub.astype(F32)
        yb1 = _gelu(y)
        yb1b = yb1.astype(BF16)
        sg = _sigmoid(_dot(yb1b, wg_ref[...]) + bg_ref[...])
        dybv = dyb_ref[...]
        dtt = dybv * yb1 * sg * (1.0 - sg)
        dttb = dtt.astype(BF16)
        dyb1 = dybv * sg + _dot_nt(dttb, wg_ref[...])
        dwg_ref[...] += _dot_tn(yb1b, dttb)
        dbg_ref[...] += jnp.sum(dtt, axis=0, keepdims=True)
        dy = dyb1 * _gelu_grad(y)
        dd_ref[...] += jnp.sum(dy * u, axis=0, keepdims=True)
        dyh = dy.astype(BF16)
        for j in range(S5_BLOCKS):
            cs = slice(j * S5_BC, (j + 1) * S5_BC)
            _s5_put(g_scr, j, _dot_nt(dyh[:, cs], cc_ref[:, cs]))
        _s5_scan(g_scr, base, tab, carry, tm, True)
        dus = []
        for j in range(S5_BLOCKS):
            cs = slice(j * S5_BC, (j + 1) * S5_BC)
            gb = _s5_get(g_scr, j).astype(BF16)
            dus.append(_dot_nt(gb, bb_ref[cs, :]))
            dbb_ref[cs, :] += _dot_tn(ub[:, cs], gb)
            dcc_ref[:, cs] += _dot_tn(_s5_get(s_ref, j).astype(BF16), dyh[:, cs])
        du_ref[...] = (jnp.concatenate(dus, axis=1) + d_ref[...] * dy).astype(BF16)
        top = lax.broadcasted_iota(jnp.int32, (SUBLANES, S5_BS), 0) == 0

        def corr(g_sl, s_sl):
            before = jnp.where(i == nt - 1, 0.0, sprev_ref[SUBLANES - 1:SUBLANES, s_sl])
            wrap = jnp.where(top, before, pltpu.roll(s_ref[tm - SUBLANES:tm, s_sl], 1, 0))
            return (jnp.sum(g_scr[SUBLANES:tm, g_sl] * s_ref[0:tm - SUBLANES, s_sl], axis=0, keepdims=True)
                    + jnp.sum(g_scr[0:SUBLANES, g_sl] * wrap, axis=0, keepdims=True))

        for j in range(S5_BLOCKS):
            lr = slice(j * S5_BS, (j + 1) * S5_BS)
            li = slice(S5_N + j * S5_BS, S5_N + (j + 1) * S5_BS)
            dab_ref[0:1, lr] += corr(lr, lr) + corr(li, li)
            dab_ref[1:2, lr] += corr(li, lr) - corr(lr, li)

    tile = lambda c: pl.BlockSpec((tm, c), lambda i: (nt - 1 - i, 0))
    acc = lambda r, c: jax.ShapeDtypeStruct((r, c), F32)
    return _tiled_call(
        body, name=name, grid=(nt,),
        in_specs=[tile(w), tile(w), tile(w), tile(2 * S5_N),
                  pl.BlockSpec((SUBLANES, 2 * S5_N), lambda i: (jnp.maximum((nt - 1 - i) * per - 1, 0), 0)),
                  _full((w, 2 * S5_BS)), _full((2 * S5_BS, w)), _full((SUBLANES, S5_N)), _full((1, w)),
                  _full((w, w)), _full((1, w))],
        out_specs=[tile(w), _full((w, 2 * S5_BS)), _full((2 * S5_BS, w)), _full((SUBLANES, S5_N)), _full((1, w)),
                   _full((w, w)), _full((1, w))],
        out_shape=[jax.ShapeDtypeStruct((t, w), BF16), acc(w, 2 * S5_BS), acc(2 * S5_BS, w), acc(SUBLANES, S5_N),
                   acc(1, w), acc(w, w), acc(1, w)],
        scratch_shapes=_s5_scratch(tm),
        operands=(dyb, ys5, u5, s, s, bbc, ccc, ab8, d_row, wglu, bglu_row), comm=comm)


def s5_discretize(a_re, a_im, log_dt, b_re, b_im, *, name):
    n = a_re.shape[0]

    def body(ar_ref, ai_ref, dt_ref, br_ref, bi_ref, ab_ref, bbr_ref, bbi_ref):
        ar, ai, dtv = ar_ref[...], ai_ref[...], jnp.exp(dt_ref[...])
        mag = jnp.exp(ar * dtv)
        abr = mag * jnp.cos(ai * dtv)
        abi = mag * jnp.sin(ai * dtv)
        den = ar * ar + ai * ai
        nr = abr - 1.0
        cr = (nr * ar + abi * ai) / den
        ci = (abi * ar - nr * ai) / den
        ab_ref[:, 0:1] = abr
        ab_ref[:, 1:2] = abi
        br, bi = br_ref[...], bi_ref[...]
        bbr_ref[...] = cr * br - ci * bi
        bbi_ref[...] = cr * bi + ci * br

    col = jax.ShapeDtypeStruct((n, 2), F32)
    mat = jax.ShapeDtypeStruct((n, S5_GROUP), F32)
    return pl.pallas_call(body, name=name, out_shape=[col, mat, mat])(a_re, a_im, log_dt, b_re, b_im)


def s5_discretize_bwd(a_re, a_im, log_dt, b_re, b_im, dab, dbb_re, dbb_im, *, name):
    n = a_re.shape[0]

    def body(ar_ref, ai_ref, dt_ref, br_ref, bi_ref, dab_ref, dbr_ref, dbi_ref, da_ref, ddt_ref, gbr_ref, gbi_ref):
        ar, ai, dtv = ar_ref[...], ai_ref[...], jnp.exp(dt_ref[...])
        mag = jnp.exp(ar * dtv)
        abr = mag * jnp.cos(ai * dtv)
        abi = mag * jnp.sin(ai * dtv)
        den = ar * ar + ai * ai
        nr = abr - 1.0
        cr = (nr * ar + abi * ai) / den
        ci = (abi * ar - nr * ai) / den
        br, bi = br_ref[...], bi_ref[...]
        dbr, dbi = dbr_ref[...], dbi_ref[...]
        gbr_ref[...] = cr * dbr + ci * dbi
        gbi_ref[...] = cr * dbi - ci * dbr
        gcr = jnp.sum(dbr * br + dbi * bi, axis=1, keepdims=True)
        gci = jnp.sum(dbi * br - dbr * bi, axis=1, keepdims=True)
        ilr, ili = ar / den, -ai / den
        gabr = dab_ref[:, 0:1] + (ilr * gcr + ili * gci)
        gabi = dab_ref[:, 1:2] + (ilr * gci - ili * gcr)
        t1r, t1i = dtv * abr, dtv * abi
        t2r, t2i = -(cr * ilr - ci * ili), -(cr * ili + ci * ilr)
        da_ref[:, 0:1] = (t1r * gabr + t1i * gabi) + (t2r * gcr + t2i * gci)
        da_ref[:, 1:2] = (t1r * gabi - t1i * gabr) + (t2r * gci - t2i * gcr)
        lr, li = ar * abr - ai * abi, ar * abi + ai * abr
        dlog = (lr * gabr + li * gabi) * dtv
        ddt_ref[...] = jnp.sum(dlog.reshape(S5_GROUPS, S5_STATE, 1), axis=1)

    col2 = jax.ShapeDtypeStruct((n, 2), F32)
    col1 = jax.ShapeDtypeStruct((S5_GROUPS, 1), F32)
    mat = jax.ShapeDtypeStruct((n, S5_GROUP), F32)
    return pl.pallas_call(body, name=name, out_shape=[col2, col1, mat, mat])(
        a_re, a_im, log_dt, b_re, b_im, dab, dbb_re, dbb_im)


def merge_fwd(ya, yb, graw, x, wb, wout, *, tm, name):
    t, d = x.shape

    def body(ya_ref, yb_ref, g_ref, x_ref, wb_ref, wo_ref, x1_ref, pa_ref, pb_ref):
        pa = _dot(ya_ref[...], wb_ref[0:d, :])
        pb = _dot(yb_ref[...], wb_ref[d:, :])
        gate = _sigmoid(g_ref[...].astype(F32))
        merged = gate[:, :d] * pa + gate[:, d:] * pb
        x1_ref[...] = x_ref[...] + _dot(merged.astype(BF16), wo_ref[...])
        pa_ref[...] = pa.astype(BF16)
        pb_ref[...] = pb.astype(BF16)

    tile = lambda c: pl.BlockSpec((tm, c), lambda i: (i, 0))
    sds = jax.ShapeDtypeStruct
    return pl.pallas_call(
        body, name=name, grid=(t // tm,),
        in_specs=[tile(d), tile(S5_WIDTH), tile(2 * d), tile(d), _full((d + S5_WIDTH, d)), _full((d, d))],
        out_specs=[tile(d), tile(d), tile(d)], out_shape=[sds((t, d), F32), sds((t, d), BF16), sds((t, d), BF16)],
        compiler_params=_params("parallel"),
    )(ya, yb, graw, x, wb, wout)


def merge_bwd(dx1b, graw, pa, pb, wb, wout, *, tm, name):
    t, d = pa.shape

    def body(dx_ref, g_ref, pa_ref, pb_ref, wb_ref, wo_ref,
             mg_ref, dg_ref, dpa_ref, dpb_ref, dya_ref, dyb_ref):
        dm = _dot_nt(dx_ref[...], wo_ref[...])
        gate = _sigmoid(g_ref[...].astype(F32))
        g0, g1 = gate[:, :d], gate[:, d:]
        pa, pb = pa_ref[...].astype(F32), pb_ref[...].astype(F32)
        mg_ref[...] = (g0 * pa + g1 * pb).astype(BF16)
        dg_ref[:, :d] = (dm * pa * g0 * (1.0 - g0)).astype(BF16)
        dg_ref[:, d:] = (dm * pb * g1 * (1.0 - g1)).astype(BF16)
        dpa = (dm * g0).astype(BF16)
        dpb = (dm * g1).astype(BF16)
        dpa_ref[...] = dpa
        dpb_ref[...] = dpb
        dya_ref[...] = _dot_nt(dpa, wb_ref[0:d, :])
        dyb_ref[...] = _dot_nt(dpb, wb_ref[d:, :])

    tile = lambda c: pl.BlockSpec((tm, c), lambda i: (i, 0))
    sds = jax.ShapeDtypeStruct
    return pl.pallas_call(
        body, name=name, grid=(t // tm,),
        in_specs=[tile(d), tile(2 * d), tile(d), tile(d), _full((d + S5_WIDTH, d)), _full((d, d))],
        out_specs=[tile(d), tile(2 * d), tile(d), tile(d), tile(d), tile(S5_WIDTH)],
        out_shape=[sds((t, d), BF16), sds((t, 2 * d), BF16), sds((t, d), BF16), sds((t, d), BF16),
                   sds((t, d), F32), sds((t, S5_WIDTH), F32)],
        compiler_params=_params("parallel"),
    )(dx1b, graw, pa, pb, wb, wout)


def mlp_out_loss(a1, x1, w2, gf, target, *, tm, name):
    t, d = x1.shape
    f = a1.shape[1]

    def body(a_ref, x_ref, w_ref, g_ref, tg_ref, dx_ref, dxb_ref, loss_ref, dg_ref):
        i = pl.program_id(0)
        av = jnp.maximum(a_ref[...].astype(F32), 0.0)
        x2 = x_ref[...] + _dot((av * av).astype(BF16), w_ref[...])
        r = lax.rsqrt(jnp.mean(x2 * x2, axis=-1, keepdims=True) + EPS)
        xn = x2 * r
        err = xn * g_ref[...] - tg_ref[...]
        dyf = err * (1.0 / d)
        dxn = dyf * g_ref[...]
        dx = r * (dxn - xn * jnp.mean(dxn * xn, axis=-1, keepdims=True))
        dx_ref[...] = dx
        dxb_ref[...] = dx.astype(BF16)

        @pl.when(i == 0)
        def _():
            loss_ref[...] = jnp.zeros_like(loss_ref)
            dg_ref[...] = jnp.zeros_like(dg_ref)

        loss_ref[...] += jnp.sum(err * err) * (0.5 / d)
        dg_ref[...] += jnp.sum(dyf * xn, axis=0, keepdims=True)

    tile = lambda c: pl.BlockSpec((tm, c), lambda i: (i, 0))
    sds = jax.ShapeDtypeStruct
    return pl.pallas_call(
        body, name=name, grid=(t // tm,),
        in_specs=[tile(f), tile(d), _full((f, d)), _full((1, d)), tile(d)],
        out_specs=[tile(d), tile(d), _full((1, LANES)), _full((1, d))],
        out_shape=[sds((t, d), F32), sds((t, d), BF16), sds((1, LANES), F32), sds((1, d), F32)],
        compiler_params=_params("arbitrary"),
    )(a1, x1, w2, gf.reshape(1, d), target)


def mlp_bwd_act(dx2b, a1, w2, *, tm, name):
    t, f = a1.shape
    d = dx2b.shape[1]

    def body(dx_ref, a_ref, w_ref, o_ref):
        dact = _dot_nt(dx_ref[...], w_ref[...])
        o_ref[...] = (dact * 2.0 * jnp.maximum(a_ref[...].astype(F32), 0.0)).astype(BF16)

    tile = lambda c: pl.BlockSpec((tm, c), lambda i: (i, 0))
    return pl.pallas_call(
        body, name=name, grid=(t // tm,),
        in_specs=[tile(d), tile(f), _full((f, d))], out_specs=tile(f),
        out_shape=jax.ShapeDtypeStruct((t, f), BF16),
        compiler_params=_params("parallel"),
    )(dx2b, a1, w2)


ADAM_TILE_ELEMS = 128 * 1024


def _row_tile(rows, cap):
    if rows <= cap:
        return rows
    best = None
    for c in range(16, cap + 1, 16):
        if rows % c == 0:
            best = c
    assert best is not None, rows
    return best


def _device_order_sum(parts_ref):
    total = parts_ref[0].astype(F32)
    for k in range(1, N_DEV):
        total = total + parts_ref[k].astype(F32)
    return total


def _adamw_math(w, m, v, gparts_ref):
    g = _device_order_sum(gparts_ref)
    mn = ADAM_B1 * m + (1.0 - ADAM_B1) * g
    vn = ADAM_B2 * v + (1.0 - ADAM_B2) * (g * g)
    m_hat = mn / (1.0 - ADAM_B1 ** ADAM_STEP)
    v_hat = vn / (1.0 - ADAM_B2 ** ADAM_STEP)
    return g, -ADAM_LR * (m_hat / (jnp.sqrt(v_hat) + ADAM_EPS) + ADAM_WD * w), mn, vn


def adamw(w, m, v, gparts, *, name):
    rows, cols = w.shape
    tr = _row_tile(rows, max(16, ADAM_TILE_ELEMS // cols))

    def body(w_ref, m_ref, v_ref, g_ref, go_ref, d_ref, mo_ref, vo_ref):
        go_ref[...], d_ref[...], mo_ref[...], vo_ref[...] = _adamw_math(w_ref[...], m_ref[...], v_ref[...], g_ref)

    tile = pl.BlockSpec((tr, cols), lambda i: (i, 0))
    out = jax.ShapeDtypeStruct((rows, cols), F32)
    return pl.pallas_call(
        body, name=name, grid=(rows // tr,),
        in_specs=[tile, tile, tile, pl.BlockSpec((N_DEV, tr, cols), lambda i: (0, i, 0))],
        out_specs=[tile, tile, tile, tile], out_shape=[out, out, out, out],
        compiler_params=_params("parallel"),
    )(w, m, v, gparts)


def all_gather_arrays(blocks, *, name):
    na = len(blocks)

    def body(*refs):
        x_refs, out_refs = refs[:na], refs[na:2 * na]
        send_sems, recv_sems, local_sems = refs[2 * na:]
        x, y, c, _ = _place()
        me, sibling = (x, y, c), (x, y, 1 - c)
        chips = [(1 - x, y), (x, 1 - y), (1 - x, 1 - y)]

        def copy(a, k, blk, to, own=False):
            px, py, pc = blk
            dst = out_refs[a].at[4 * px + 2 * py + pc]
            return pltpu.make_async_remote_copy(
                src_ref=x_refs[a] if own else dst, dst_ref=dst,
                send_sem=send_sems.at[7 * a + k], recv_sem=recv_sems.at[7 * a + k], device_id=to, device_id_type=_MESH)

        mine = [pltpu.make_async_copy(x_refs[a], out_refs[a].at[4 * x + 2 * y + c], local_sems.at[a]) for a in range(na)]
        for cp in mine:
            cp.start()
        sent = []
        for a in range(na):
            first = [copy(a, 0, me, sibling, own=True)]
            first += [copy(a, 1 + j, me, (*chip, c), own=True) for j, chip in enumerate(chips)]
            for cp in first:
                cp.start()
            sent += first
        for a in range(na):
            for j, chip in enumerate(chips):
                copy(a, 1 + j, (*chip, c), me).wait_recv()
                fwd = copy(a, 4 + j, (*chip, c), sibling)
                fwd.start()
                sent.append(fwd)
        for a in range(na):
            copy(a, 0, sibling, me).wait_recv()
            for j, chip in enumerate(chips):
                copy(a, 4 + j, (*chip, 1 - c), me).wait_recv()
        for cp in sent:
            cp.wait_send()
        for cp in mine:
            cp.wait()

    any_spec = pl.BlockSpec(memory_space=pl.ANY)
    return pl.pallas_call(
        body, name=name, in_specs=[any_spec] * na, out_specs=[any_spec] * na,
        out_shape=[jax.ShapeDtypeStruct((N_DEV,) + b.shape, b.dtype) for b in blocks],
        scratch_shapes=[pltpu.SemaphoreType.DMA((7 * na,)), pltpu.SemaphoreType.DMA((7 * na,)),
                        pltpu.SemaphoreType.DMA((na,))],
        compiler_params=pltpu.CompilerParams(has_side_effects=True),
    )(*blocks)


def all_to_all(comm, *, name):
    na = len(comm)
    kinds = [k for k, _ in comm]

    def body(*refs):
        local, remote = _comm_copies(kinds, refs[:na], refs[na:2 * na], *refs[2 * na:])
        for cp in local + remote:
            cp.start()
        for cp in remote:
            cp.wait_recv()
        for cp in remote:
            cp.wait_send()
        for cp in local:
            cp.wait()

    any_spec = pl.BlockSpec(memory_space=pl.ANY)
    return pl.pallas_call(
        body, name=name, in_specs=[any_spec] * na, out_specs=[any_spec] * na,
        out_shape=_comm_out_shapes(comm),
        scratch_shapes=[pltpu.SemaphoreType.DMA((na * N_REL,)), pltpu.SemaphoreType.DMA((na * N_REL,)),
                        pltpu.SemaphoreType.DMA((na,))],
        compiler_params=pltpu.CompilerParams(has_side_effects=True),
    )(*[a for _, a in comm])


def adamw_whole(ws, ms, vs, gparts, *, name, totals=()):
    n, nt = len(ws), len(totals)

    def body(*refs):
        w_refs, m_refs, v_refs, g_refs = refs[:n], refs[n:2 * n], refs[2 * n:3 * n], refs[3 * n:4 * n]
        t_refs, refs = refs[4 * n:4 * n + nt], refs[4 * n + nt:]
        outs, t_outs = refs[:4 * n], refs[4 * n:]
        for k in range(n):
            g, delta, mn, vn = _adamw_math(w_refs[k][...], m_refs[k][...], v_refs[k][...], g_refs[k])
            outs[k][...] = g
            outs[n + k][...] = delta
            outs[2 * n + k][...] = mn
            outs[3 * n + k][...] = vn
        for t_ref, o_ref in zip(t_refs, t_outs):
            o_ref[...] = _device_order_sum(t_ref)

    shapes = [jax.ShapeDtypeStruct(w.shape, F32) for w in ws]
    res = pl.pallas_call(
        body, name=name, out_shape=shapes * 4 + [jax.ShapeDtypeStruct(t.shape[1:], F32) for t in totals],
        compiler_params=pltpu.CompilerParams(vmem_limit_bytes=VMEM_LIMIT),
    )(*ws, *ms, *vs, *gparts, *totals)
    return [res[j * n:(j + 1) * n] for j in range(4)], res[4 * n:]


def _lane_row(v):
    return jnp.pad(v.astype(F32), (0, LANES - v.shape[0])).reshape(1, LANES)


def _block_diag_b(bb):
    eye = jnp.eye(SUBLANES, dtype=F32)
    bt = bb.reshape(S5_BLOCKS, 8, S5_STATE, S5_GROUP).transpose(0, 1, 3, 2)
    return (bt[:, :, :, None, :] * eye[None, :, None, :, None]).reshape(S5_WIDTH, S5_BS)


def _block_diag_c(cc):
    eye = jnp.eye(SUBLANES, dtype=F32)
    ct = cc.reshape(S5_BLOCKS, 8, S5_GROUP, S5_STATE).transpose(3, 0, 1, 2)
    return (ct[None, :, :, :, :] * eye[:, None, None, :, None]).reshape(S5_BS, S5_WIDTH)


def _diag_blocks_b(m):
    eye = jnp.eye(SUBLANES, dtype=F32)
    m5 = m.reshape(S5_BLOCKS, 8, S5_GROUP, 8, S5_STATE)
    return jnp.sum(m5 * eye[None, :, None, :, None], axis=3).transpose(0, 1, 3, 2).reshape(S5_GROUPS, S5_STATE, S5_GROUP)


def _diag_blocks_c(m):
    eye = jnp.eye(SUBLANES, dtype=F32)
    m5 = m.reshape(8, S5_STATE, S5_BLOCKS, 8, S5_GROUP)
    return jnp.sum(m5 * eye[:, None, None, :, None], axis=0).transpose(1, 2, 3, 0).reshape(S5_GROUPS, S5_GROUP, S5_STATE)


def train_step(x, target, p, m, v):
    t = x.shape[0]
    tm = min(256, t)
    bf = lambda n: p[n].astype(BF16)
    slots = lambda n, a: a.reshape((N_DEV,) + _shard_shape(n))
    updated = {}

    def update(names, got):
        for n, parts in zip(names, got):
            updated[n] = adamw(p[n], m[n], v[n], parts, name=f"adamw_{n}")

    st_in, st_conv = all_gather_arrays([bf("w_in"), p["conv_w"]], name="gather_first")
    win_p = jnp.concatenate([st_in[k][:, a:b] for seg in IN_PADDED_ORDER for k, a, b in _shard_pieces(*IN_SEGMENTS[seg])]
                            + [jnp.zeros((D_MODEL, LANES - SSD_HEADS), BF16)], axis=1)
    conv_w8 = jnp.pad(_join_shards("conv_w", st_conv), ((0, SUBLANES - SSD_CONV), (0, 0)))
    (h, xbc_raw, graw, z, u5, dtr), (st_branch, st_out, st_glu) = rms_matmul(
        x, p["norm_mix_g"], win_p, IN_SPLITS, tm=tm, name="in_proj", out_dtypes=(BF16, BF16, BF16, BF16, F32),
        comm=gathers([bf("w_branch"), bf("w_out"), bf("s5_glu_w")]))
    w_branch = st_branch.reshape(D_MODEL + S5_WIDTH, D_MODEL)
    wout, wglu = st_out.reshape(D_MODEL, D_MODEL), st_glu.reshape(S5_WIDTH, S5_WIDTH)
    bias_row, alog_row = _lane_row(p["dt_bias"]), _lane_row(p["a_log"])
    d_row = jnp.repeat(p["d_ssd"], SSD_HEADDIM).reshape(1, SSD_HEADS * SSD_HEADDIM)
    (ya, yssd, states), (w1,) = ssd_fwd(xbc_raw, conv_w8, p["conv_b"], dtr, z, bias_row, alog_row, d_row,
                                        p["ssd_norm_g"], name="ssd_fwd", comm=gathers([bf("w_mlp_in")]))

    n = S5_N
    a_re_c, a_im_c = p["s5_a_re"].reshape(n, 1), p["s5_a_im"].reshape(n, 1)
    dt_c = jnp.repeat(p["s5_log_dt"], S5_STATE).reshape(n, 1)
    b_re_m, b_im_m = p["s5_b_re"].reshape(n, S5_GROUP), p["s5_b_im"].reshape(n, S5_GROUP)
    ab, bb_re, bb_im = s5_discretize(a_re_c, a_im_c, dt_c, b_re_m, b_im_m, name="s5_disc")
    ab8 = jnp.pad(ab.T, ((0, SUBLANES - 2), (0, 0)))
    bbc = jnp.concatenate([_block_diag_b(bb_re.reshape(S5_GROUPS, S5_STATE, S5_GROUP)),
                           _block_diag_b(bb_im.reshape(S5_GROUPS, S5_STATE, S5_GROUP))], axis=1).astype(BF16)
    ccc = jnp.concatenate([_block_diag_c(p["s5_c_re"]), -_block_diag_c(p["s5_c_im"])], axis=0).astype(BF16)
    s5d_row = p["s5_d"].reshape(1, S5_WIDTH)
    bglu_row = p["s5_glu_b"].reshape(1, S5_WIDTH)
    u5 = s5_interleave(u5, tm)
    (s, ys5, yb), (st_w2,) = s5_fwd(u5, bbc, ccc, ab8, s5d_row, wglu, bglu_row, tm=tm, name="s5_fwd",
                                    comm=gathers([bf("w_mlp_out")]))
    yb = s5_deinterleave(yb, tm)
    w2 = st_w2.reshape(D_FF, D_MODEL)

    x1, pa, pb = merge_fwd(ya, yb, graw, x, w_branch, wout, tm=tm, name="merge_fwd")
    (h2, a1), _ = rms_matmul(x1, p["norm_mlp_g"], w1, (D_FF,), tm=tm, name="mlp_in", out_dtypes=(BF16,))
    dx2, dx2b, loss_row, dgf = mlp_out_loss(a1, x1, w2, p["norm_final_g"], target, tm=tm, name="mlp_out_loss")

    g = {}
    g["norm_final_g"] = dgf[0]
    da1 = mlp_bwd_act(dx2b, a1, w2, tm=tm, name="mlp_bwd_act")
    dw2 = slots("w_mlp_out", matmul_tn(a1, dx2b, out_dtype=BF16, a_relu2=True, name="dw_mlp_out"))
    dw1 = matmul_tn(h2, da1, out_dtype=BF16, col_shards=N_DEV, name="dw_mlp_in")
    (dx1, dx1b, dgm), _ = matmul_nt_rms_bwd([da1], w1, x1, p["norm_mlp_g"], dx2, tm=tm, name="mlp_in_bwd")
    g["norm_mlp_g"] = dgm[0]
    merged, dgraw, dpa, dpb, dya, dyb = merge_bwd(dx1b, graw, pa, pb, w_branch, wout, tm=tm, name="merge_bwd")
    dwo = slots("w_out", matmul_tn(merged, dx1b, out_dtype=BF16, name="dw_out"))
    dwb = slots("w_branch", jnp.concatenate([matmul_tn(ya, dpa, out_dtype=BF16, name="dw_branch_a"),
                                             matmul_tn(yb, dpb, out_dtype=BF16, name="dw_branch_b")], axis=0))

    (du5, dbbc, dccc, dab, dd5, dwglu, dbglu), got = s5_bwd(
        s5_interleave(dyb, tm), ys5, u5, s, bbc, ccc, ab8, s5d_row, wglu, bglu_row, tm=tm, name="s5_bwd",
        comm=exchanges([dw2, dw1]))
    du5 = s5_deinterleave(du5, tm)
    update(["w_mlp_out", "w_mlp_in"], got)
    g["s5_glu_b"], g["s5_d"] = dbglu[0], dd5[0]
    g["s5_c_re"] = _diag_blocks_c(dccc[:S5_BS])
    g["s5_c_im"] = -_diag_blocks_c(dccc[S5_BS:])
    dbb_re = _diag_blocks_b(dbbc[:, :S5_BS]).reshape(n, S5_GROUP)
    dbb_im = _diag_blocks_b(dbbc[:, S5_BS:]).reshape(n, S5_GROUP)
    da, dlogdt, gb_re, gb_im = s5_discretize_bwd(a_re_c, a_im_c, dt_c, b_re_m, b_im_m, dab[:2].T, dbb_re, dbb_im,
                                                  name="s5_disc_bwd")
    g["s5_a_re"] = da[:, 0].reshape(S5_GROUPS, S5_STATE)
    g["s5_a_im"] = da[:, 1].reshape(S5_GROUPS, S5_STATE)
    g["s5_log_dt"] = dlogdt[:, 0]
    g["s5_b_re"] = gb_re.reshape(S5_GROUPS, S5_STATE, S5_GROUP)
    g["s5_b_im"] = gb_im.reshape(S5_GROUPS, S5_STATE, S5_GROUP)

    def update_small(names, got, name, totals=()):
        (gs, ds, nm, nv), sums = adamw_whole([p[n] for n in names], [m[n] for n in names], [v[n] for n in names], got,
                                             name=name, totals=totals)
        for k, n in enumerate(names):
            updated[n] = (gs[k], ds[k], nm[k], nv[k])
        return sums

    wide = ["s5_b_re", "s5_b_im"]
    late = ["ssd_norm_g", "d_ssd", "a_log", "dt_bias", "conv_b"]
    early = [n for n in SMALL_ORDER if n not in wide + late and n != "norm_mix_g"]
    (dz, dxbc_raw, ddtr, dng, dd_row, dalog_row, dbias_row, dconv_b, dconv_w8), got = ssd_bwd(
        dya, yssd, z, xbc_raw, conv_w8, p["conv_b"], dtr, states, bias_row, alog_row, d_row, p["ssd_norm_g"],
        name="ssd_bwd",
        comm=exchanges([dwo, dwb, slots("s5_glu_w", dwglu.astype(BF16))])
        + gathers([g[n] for n in wide + early] + [loss_row]))
    update(["w_out", "w_branch", "s5_glu_w"], got[:3])
    update_small(wide, got[3:3 + len(wide)], "adamw_s5_b")
    loss_sum, = update_small(early, got[3 + len(wide):-1], "adamw_small_early", totals=[got[-1]])
    g["ssd_norm_g"] = dng[0]
    g["d_ssd"], g["a_log"], g["dt_bias"] = dd_row[0, :SSD_HEADS], dalog_row[0, :SSD_HEADS], dbias_row[0, :SSD_HEADS]
    g["conv_b"] = dconv_b[0]
    dconv = dconv_w8[:SSD_CONV].reshape(SSD_CONV, N_DEV, CONV_DIM // N_DEV).transpose(1, 0, 2)
    dproj =[dxbc_raw, dgraw, dz, du5, ddtr]
    dxbc_w, dgate_w, dz_w, du5_w, ddt_w = [matmul_tn(h, piece, out_dtype=BF16, name=f"dw_in_{k}")
                                           for k, piece in enumerate(dproj)]
    by_segment = {"z": dz_w, "xbc": dxbc_w, "dt": ddt_w, "u5": du5_w, "gates": dgate_w}
    width = D_IN_PROJ // N_DEV
    dw_in = jnp.stack([jnp.concatenate(
        [by_segment[seg][:, max(k * width, lo) - lo:min((k + 1) * width, hi) - lo]
         for seg, (lo, hi) in IN_SEGMENTS.items() if max(k * width, lo) < min((k + 1) * width, hi)], axis=1)
        for k in range(N_DEV)])

    (grad_x, _, dgx), got = matmul_nt_rms_bwd(
        dproj, win_p, x, p["norm_mix_g"], dx1, tm=tm, name="in_proj_bwd",
        comm=exchanges([dw_in, dconv]) + gathers([g[n] for n in late]))
    update(["w_in", "conv_w"], got[:2])
    update_small(late, got[2:], "adamw_small_late")
    update_small(["norm_mix_g"], all_to_all(gathers([dgx[0]]), name="gather_last"), "adamw_small_last")
    return loss_sum[0, 0], grad_x, updated


WEIGHT_ORDER = ["norm_mix_g", "w_in", "conv_w", "conv_b", "dt_bias", "a_log", "d_ssd", "ssd_norm_g", "s5_a_re",
                "s5_a_im", "s5_log_dt", "s5_b_re", "s5_b_im", "s5_c_re", "s5_c_im", "s5_d", "s5_glu_w", "s5_glu_b",
                "w_branch", "w_out", "norm_mlp_g", "w_mlp_in", "w_mlp_out", "norm_final_g"]
SHARDED = {"w_in": ((D_MODEL, D_IN_PROJ), 1), "conv_w": ((SSD_CONV, CONV_DIM), 1), "s5_glu_w": ((S5_WIDTH, S5_WIDTH), 0),
           "w_branch": ((D_MODEL + S5_WIDTH, D_MODEL), 0), "w_out": ((D_MODEL, D_MODEL), 0),
           "w_mlp_in": ((D_MODEL, D_FF), 1), "w_mlp_out": ((D_FF, D_MODEL), 0)}
SHARDED_ORDER = ["w_in", "conv_w", "s5_glu_w", "w_branch", "w_out", "w_mlp_in", "w_mlp_out"]
SMALL_ORDER = [n for n in WEIGHT_ORDER if n not in SHARDED]


def _shard_shape(name):
    (r, c), ax = SHARDED[name]
    return (r, c // N_DEV) if ax == 1 else (r // N_DEV, c)


def _join_shards(name, stacked):
    (r, c), ax = SHARDED[name]
    if ax == 1:
        return stacked.transpose(1, 0, 2).reshape(r, c)
    return stacked.reshape(r, c)


def _shard_pieces(lo, hi):
    width = D_IN_PROJ // N_DEV
    out = []
    while lo < hi:
        k = lo // width
        end = min(hi, (k + 1) * width)
        out.append((k, lo - k * width, end - k * width))
        lo = end
    return out


IN_SEGMENTS = {"z": (0, 1024), "xbc": (1024, 3072), "dt": (3072, 3088), "u5": (3088, 3600), "gates": (3600, 5648)}
IN_PADDED_ORDER = ("xbc", "gates", "z", "u5", "dt")


def kernel(x, norm_mix_g, w_in, conv_w, conv_b, dt_bias, a_log, d_ssd, ssd_norm_g, s5_a_re, s5_a_im, s5_log_dt, s5_b_re, s5_b_im, s5_c_re, s5_c_im, s5_d, s5_glu_w, s5_glu_b, w_branch, w_out, norm_mlp_g, w_mlp_in, w_mlp_out, norm_final_g, loss_target, m_norm_mix_g, m_w_in, m_conv_w, m_conv_b, m_dt_bias, m_a_log, m_d_ssd, m_ssd_norm_g, m_s5_a_re, m_s5_a_im, m_s5_log_dt, m_s5_b_re, m_s5_b_im, m_s5_c_re, m_s5_c_im, m_s5_d, m_s5_glu_w, m_s5_glu_b, m_w_branch, m_w_out, m_norm_mlp_g, m_w_mlp_in, m_w_mlp_out, m_norm_final_g, v_norm_mix_g, v_w_in, v_conv_w, v_conv_b, v_dt_bias, v_a_log, v_d_ssd, v_ssd_norm_g, v_s5_a_re, v_s5_a_im, v_s5_log_dt, v_s5_b_re, v_s5_b_im, v_s5_c_re, v_s5_c_im, v_s5_d, v_s5_glu_w, v_s5_glu_b, v_w_branch, v_w_out, v_norm_mlp_g, v_w_mlp_in, v_w_mlp_out, v_norm_final_g):
    w = dict(norm_mix_g=norm_mix_g, w_in=w_in, conv_w=conv_w, conv_b=conv_b, dt_bias=dt_bias, a_log=a_log, d_ssd=d_ssd,
             ssd_norm_g=ssd_norm_g, s5_a_re=s5_a_re, s5_a_im=s5_a_im, s5_log_dt=s5_log_dt, s5_b_re=s5_b_re,
             s5_b_im=s5_b_im, s5_c_re=s5_c_re, s5_c_im=s5_c_im, s5_d=s5_d, s5_glu_w=s5_glu_w, s5_glu_b=s5_glu_b,
             w_branch=w_branch, w_out=w_out, norm_mlp_g=norm_mlp_g, w_mlp_in=w_mlp_in, w_mlp_out=w_mlp_out,
             norm_final_g=norm_final_g)
    m = dict(norm_mix_g=m_norm_mix_g, w_in=m_w_in, conv_w=m_conv_w, conv_b=m_conv_b, dt_bias=m_dt_bias, a_log=m_a_log,
             d_ssd=m_d_ssd, ssd_norm_g=m_ssd_norm_g, s5_a_re=m_s5_a_re, s5_a_im=m_s5_a_im, s5_log_dt=m_s5_log_dt,
             s5_b_re=m_s5_b_re, s5_b_im=m_s5_b_im, s5_c_re=m_s5_c_re, s5_c_im=m_s5_c_im, s5_d=m_s5_d,
             s5_glu_w=m_s5_glu_w, s5_glu_b=m_s5_glu_b, w_branch=m_w_branch, w_out=m_w_out, norm_mlp_g=m_norm_mlp_g,
             w_mlp_in=m_w_mlp_in, w_mlp_out=m_w_mlp_out, norm_final_g=m_norm_final_g)
    v = dict(norm_mix_g=v_norm_mix_g, w_in=v_w_in, conv_w=v_conv_w, conv_b=v_conv_b, dt_bias=v_dt_bias, a_log=v_a_log,
             d_ssd=v_d_ssd, ssd_norm_g=v_ssd_norm_g, s5_a_re=v_s5_a_re, s5_a_im=v_s5_a_im, s5_log_dt=v_s5_log_dt,
             s5_b_re=v_s5_b_re, s5_b_im=v_s5_b_im, s5_c_re=v_s5_c_re, s5_c_im=v_s5_c_im, s5_d=v_s5_d,
             s5_glu_w=v_s5_glu_w, s5_glu_b=v_s5_glu_b, w_branch=v_w_branch, w_out=v_w_out, norm_mlp_g=v_norm_mlp_g,
             w_mlp_in=v_w_mlp_in, w_mlp_out=v_w_mlp_out, norm_final_g=v_norm_final_g)

    loss, grad_x, upd = train_step(x[0], loss_target[0], w, m, v)
    return (loss, grad_x.reshape(x.shape), *[upd[n][j] for j in range(4) for n in WEIGHT_ORDER])
```

```python
import functools
import math

import jax
import jax.numpy as jnp
from jax import lax
from jax.experimental import pallas as pl
from jax.experimental.pallas import tpu as pltpu

F32 = jnp.float32
BF16 = jnp.bfloat16
HI = lax.Precision.HIGHEST

N_DEV = 8
D_MODEL = 1024
SSD_HEADS = 16
SSD_HEADDIM = 64
SSD_GROUPS = 4
SSD_HPG = 4
SSD_STATE = 128
SSD_CHUNK = 128
SSD_CONV = 4
CONV_DIM = 2048
S5_WIDTH = 512
S5_GROUP = 16
S5_GROUPS = 32
S5_STATE = 64
S5_N = S5_GROUPS * S5_STATE
D_FF = 4096
D_IN_PROJ = 5648
IN_SPLITS = (2048, 2048, 1024, 512, 128)
D_IN_PAD = sum(IN_SPLITS)
EPS = 1e-6
LANES = 128
SUBLANES = 8
VMEM_LIMIT = 56 * 1024 * 1024

ADAM_LR = 0.001
ADAM_B1 = 0.9
ADAM_B2 = 0.999
ADAM_EPS = 1e-08
ADAM_WD = 0.01
ADAM_STEP = 10

GELU_C = math.sqrt(2.0 / math.pi)
GELU_K = 0.044715


def _params(*sem):
    return pltpu.CompilerParams(dimension_semantics=sem, vmem_limit_bytes=VMEM_LIMIT)


def _full(shape):
    nd = len(shape)
    return pl.BlockSpec(shape, lambda *_: (0,) * nd)


_MESH = pl.DeviceIdType.MESH
_RELATIONS = [(dx, dy, dc) for dx in (0, 1) for dy in (0, 1) for dc in (0, 1)][1:]
N_REL = len(_RELATIONS)


def _place():
    x, y, c = lax.axis_index("x"), lax.axis_index("y"), lax.axis_index("c")
    return x, y, c, 4 * x + 2 * y + c


def gathers(arrays):
    return [("gather", a) for a in arrays]


def exchanges(arrays):
    return [("exchange", a) for a in arrays]


def _comm_out_shapes(comm):
    return [jax.ShapeDtypeStruct(((N_DEV,) if kind == "gather" else ()) + a.shape, a.dtype) for kind, a in comm]


def _comm_copies(kinds, src_refs, dst_refs, send_sems, recv_sems, local_sems):
    x, y, c, idx = _place()
    local, remote = [], []
    for a, (kind, src, dst) in enumerate(zip(kinds, src_refs, dst_refs)):
        local.append(pltpu.make_async_copy(src if kind == "gather" else src.at[idx], dst.at[idx], local_sems.at[a]))
        for k, (dx, dy, dc) in enumerate(_RELATIONS):
            px, py, pc = x ^ dx, y ^ dy, c ^ dc
            remote.append(pltpu.make_async_remote_copy(
                src_ref=src if kind == "gather" else src.at[4 * px + 2 * py + pc], dst_ref=dst.at[idx],
                send_sem=send_sems.at[N_REL * a + k], recv_sem=recv_sems.at[N_REL * a + k],
                device_id=(px, py, pc), device_id_type=_MESH))
    return local, remote


def _tiled_call(body, *, name, grid, in_specs, out_specs, out_shape, operands, scratch_shapes=(), comm=None):
    params = pltpu.CompilerParams(dimension_semantics=("arbitrary",), vmem_limit_bytes=VMEM_LIMIT,
                                  has_side_effects=bool(comm))
    if not comm:
        outs = pl.pallas_call(body, name=name, grid=grid, in_specs=in_specs, out_specs=out_specs, out_shape=out_shape,
                              scratch_shapes=list(scratch_shapes), compiler_params=params)(*operands)
        return outs, []
    kind = [k for k, _ in comm]
    arrays = [a for _, a in comm]
    na, n_in, n_out, n_scr = len(arrays), len(in_specs), len(out_specs), len(scratch_shapes)
    last = grid[0] - 1

    def hosted(*refs):
        ins, refs = refs[:n_in], refs[n_in:]
        cin, refs = refs[:na], refs[na:]
        outs, refs = refs[:n_out], refs[n_out:]
        cout, refs = refs[:na], refs[na:]
        scr, sems = refs[:n_scr], refs[n_scr:]
        i = pl.program_id(0)

        @pl.when(i == 0)
        def _():
            local, remote = _comm_copies(kind, cin, cout, *sems)
            for cp in local + remote:
                cp.start()

        body(*ins, *outs, *scr)

        @pl.when(i == last)
        def _():
            local, remote = _comm_copies(kind, cin, cout, *sems)
            for cp in remote:
                cp.wait_recv()
            for cp in remote:
                cp.wait_send()
            for cp in local:
                cp.wait()

    any_spec = pl.BlockSpec(memory_space=pl.ANY)
    res = pl.pallas_call(
        hosted, name=name, grid=grid,
        in_specs=list(in_specs) + [any_spec] * na, out_specs=list(out_specs) + [any_spec] * na,
        out_shape=list(out_shape) + _comm_out_shapes(comm),
        scratch_shapes=list(scratch_shapes) + [pltpu.SemaphoreType.DMA((N_REL * na,)), pltpu.SemaphoreType.DMA((N_REL * na,)),
                                               pltpu.SemaphoreType.DMA((na,))],
        compiler_params=params)(*operands, *arrays)
    return res[:n_out], res[n_out:]


def _dot(a, b):
    return jnp.dot(a, b, preferred_element_type=F32)


def _dot_nt(a, b):
    return lax.dot_general(a, b, (((1,), (1,)), ((), ())), preferred_element_type=F32)


def _dot_tn(a, b):
    return lax.dot_general(a, b, (((0,), (0,)), ((), ())), preferred_element_type=F32)


def _dot_hi(a, b):
    return jnp.dot(a, b, preferred_element_type=F32, precision=HI)


def _dot_mid(a, b):
    return jnp.dot(a, b, preferred_element_type=F32, precision=lax.Precision.HIGH)


def _dot_nt_hi(a, b):
    return lax.dot_general(a, b, (((1,), (1,)), ((), ())), preferred_element_type=F32, precision=HI)


def _dot_tn_hi(a, b):
    return lax.dot_general(a, b, (((0,), (0,)), ((), ())), preferred_element_type=F32, precision=HI)


def _sigmoid(x):
    return 1.0 / (1.0 + jnp.exp(-x))


def _softplus(x):
    return jnp.maximum(x, 0.0) + jnp.log(1.0 + jnp.exp(-jnp.abs(x)))


def _gelu(x):
    return 0.5 * x * (1.0 + jnp.tanh(GELU_C * (x + GELU_K * x * x * x)))


def _gelu_grad(x):
    th = jnp.tanh(GELU_C * (x + GELU_K * x * x * x))
    return 0.5 * (1.0 + th) + 0.5 * x * (1.0 - th * th) * GELU_C * (1.0 + 3.0 * GELU_K * x * x)


def rms_matmul(x, g, w, splits, *, tm, name, comm=None, out_dtypes=None):
    t, d = x.shape
    stacked = w.ndim == 3
    n = w.shape[0] * w.shape[2] if stacked else w.shape[1]
    assert sum(splits) == n and (len(splits) == 1 or not stacked)

    def body(x_ref, g_ref, w_ref, h_ref, *o_refs):
        xv = x_ref[...]
        r = lax.rsqrt(jnp.mean(xv * xv, axis=-1, keepdims=True) + EPS)
        h = (xv * r * g_ref[...]).astype(BF16)
        h_ref[...] = h
        if stacked:
            wk = w.shape[2]
            for k in range(w.shape[0]):
                o_refs[0][:, k * wk:(k + 1) * wk] = _dot(h, w_ref[k]).astype(o_refs[0].dtype)
            return
        off = 0
        for o_ref, width in zip(o_refs, splits):
            o_ref[...] = _dot(h, w_ref[:, off:off + width]).astype(o_ref.dtype)
            off += width

    tile = lambda c: pl.BlockSpec((tm, c), lambda i: (i, 0))
    dtypes = out_dtypes or (F32,) * len(splits)
    return _tiled_call(
        body, name=name, grid=(t // tm,),
        in_specs=[tile(d), _full((1, d)), _full(w.shape)],
        out_specs=[tile(d)] + [tile(c) for c in splits],
        out_shape=[jax.ShapeDtypeStruct((t, d), BF16)] + [jax.ShapeDtypeStruct((t, c), dt) for c, dt in zip(splits, dtypes)],
        operands=(x, g.reshape(1, d), w), comm=comm)


def _pick(n, cap):
    best = LANES
    for c in range(LANES, cap + 1, LANES):
        if n % c == 0:
            best = c
    return best


def matmul_tn(a, b, *, name, out_dtype=F32, col_shards=None, tk=1024, a_relu2=False):
    t, m = a.shape
    n = b.shape[1]
    tk = min(tk, t)
    tm = _pick(m, 1024)
    tn = n // col_shards if col_shards else _pick(n, 1280)
    nk = t // tk

    def body(a_ref, b_ref, o_ref, acc):
        k = pl.program_id(2)

        @pl.when(k == 0)
        def _():
            acc[...] = jnp.zeros_like(acc)

        av = a_ref[...]
        if a_relu2:
            pos = jnp.maximum(av.astype(F32), 0.0)
            av = (pos * pos).astype(BF16)
        acc[...] += _dot_tn(av, b_ref[...])

        @pl.when(k == nk - 1)
        def _():
            o_ref[...] = acc[...].reshape(o_ref.shape).astype(out_dtype)

    if col_shards:
        out_spec = pl.BlockSpec((1, tm, tn), lambda i, j, k: (j, i, 0))
        out_shape = jax.ShapeDtypeStruct((col_shards, m, tn), out_dtype)
    else:
        out_spec = pl.BlockSpec((tm, tn), lambda i, j, k: (i, j))
        out_shape = jax.ShapeDtypeStruct((m, n), out_dtype)
    return pl.pallas_call(
        body, name=name, grid=(m // tm, n // tn, nk),
        in_specs=[pl.BlockSpec((tk, tm), lambda i, j, k: (k, i)), pl.BlockSpec((tk, tn), lambda i, j, k: (k, j))],
        out_specs=out_spec, out_shape=out_shape,
        scratch_shapes=[pltpu.VMEM((tm, tn), F32)],
        compiler_params=_params("parallel", "parallel", "arbitrary"),
    )(a, b)


def matmul_nt_rms_bwd(dys, w, x, g, dres, *, tm, name, comm=None):
    t = x.shape[0]
    stacked = w.ndim == 3
    d = w.shape[1] if stacked else w.shape[0]
    widths = [dy.shape[1] for dy in dys]
    n_dy = len(dys)

    def body(*refs):
        dy_refs = refs[:n_dy]
        w_ref, x_ref, g_ref, dres_ref, dx_ref, dxb_ref, dg_ref = refs[n_dy:]
        i = pl.program_id(0)
        if stacked:
            wk = w.shape[2]
            dh = _dot_nt(dy_refs[0][:, 0:wk], w_ref[0])
            for k in range(1, w.shape[0]):
                dh = dh + _dot_nt(dy_refs[0][:, k * wk:(k + 1) * wk], w_ref[k])
        else:
            dh, off = None, 0
            for dy_ref, width in zip(dy_refs, widths):
                part = _dot_nt(dy_ref[...], w_ref[:, off:off + width])
                dh = part if dh is None else dh + part
                off += width
        xv = x_ref[...]
        r = lax.rsqrt(jnp.mean(xv * xv, axis=-1, keepdims=True) + EPS)
        xn = xv * r
        dxn = dh * g_ref[...]
        dx = dres_ref[...] + r * (dxn - xn * jnp.mean(dxn * xn, axis=-1, keepdims=True))
        dx_ref[...] = dx
        dxb_ref[...] = dx.astype(BF16)

        @pl.when(i == 0)
        def _():
            dg_ref[...] = jnp.zeros_like(dg_ref)

        dg_ref[...] += jnp.sum(dh * xn, axis=0, keepdims=True)

    tile = lambda c: pl.BlockSpec((tm, c), lambda i: (i, 0))
    return _tiled_call(
        body, name=name, grid=(t // tm,),
        in_specs=[tile(c) for c in widths] + [_full(w.shape), tile(d), _full((1, d)), tile(d)],
        out_specs=[tile(d), tile(d), _full((1, d))],
        out_shape=[jax.ShapeDtypeStruct((t, d), F32), jax.ShapeDtypeStruct((t, d), BF16),
                   jax.ShapeDtypeStruct((1, d), F32)],
        operands=(*dys, w, x, g.reshape(1, d), dres), comm=comm)


def _conv_windows(ext_ref, tm):
    return [ext_ref[SUBLANES - 3 + k:SUBLANES - 3 + k + tm, :] for k in range(SSD_CONV)]


def _conv_taps(windows, w_ref):
    acc = windows[3] * w_ref[3:4, :]
    for k in range(3):
        acc = acc + windows[k] * w_ref[k:k + 1, :]
    return acc


def _ssd_chunk_terms(dtr, bias_row, alog_row):
    q = SSD_CHUNK
    row = lax.broadcasted_iota(jnp.int32, (q, q), 0)
    col = lax.broadcasted_iota(jnp.int32, (q, q), 1)
    ltri = (col <= row).astype(F32)
    dt = _softplus(dtr + bias_row)
    a_row = -jnp.exp(alog_row)
    la = dt * a_row
    cum = _dot_hi(ltri, la)
    cum_t = _dot_tn_hi(la, (row <= col).astype(F32))
    return dt, a_row, cum, cum_t, ltri, row, col


def _head_maps():
    di = SSD_HEADS * SSD_HEADDIM
    expand = (lax.broadcasted_iota(jnp.int32, (LANES, di), 1) // SSD_HEADDIM
              == lax.broadcasted_iota(jnp.int32, (LANES, di), 0)).astype(F32)
    collapse = (lax.broadcasted_iota(jnp.int32, (di, LANES), 0) // SSD_HEADDIM
                == lax.broadcasted_iota(jnp.int32, (di, LANES), 1)).astype(F32)
    return expand, collapse


def _ssd_decay(cum, cum_t, h, causal):
    seg = cum[:, h:h + 1] - cum_t[h:h + 1, :]
    return jnp.where(causal, jnp.exp(jnp.where(causal, seg, 0.0)), 0.0)


def _pair_block_diag(x2):
    low = lax.broadcasted_iota(jnp.int32, x2.shape, 1) < SSD_HEADDIM
    zero = jnp.zeros_like(x2)
    return jnp.concatenate([jnp.where(low, x2, zero), jnp.where(low, zero, x2)], axis=0)


def ssd_fwd(xbc_raw, conv_w8, conv_b, dtr, z, bias_row, alog_row, d_row, norm_g, *, name, comm=None):
    t = xbc_raw.shape[0]
    q, p, n = SSD_CHUNK, SSD_HEADDIM, SSD_STATE
    nc = t // q
    di = SSD_HEADS * p
    gw = di // SSD_GROUPS

    def body(raw_ref, cw_ref, cb_ref, dtr_ref, z_ref, bias_ref, alog_ref, d_ref, g_ref,
             ya_ref, yssd_ref, st_ref, state, ext, xbc_s):
        ci = pl.program_id(0)

        @pl.when(ci == 0)
        def _():
            state[...] = jnp.zeros_like(state)
            ext[0:SUBLANES, :] = jnp.zeros((SUBLANES, CONV_DIM), F32)

        ext[SUBLANES:SUBLANES + q, :] = raw_ref[...].astype(F32)
        xc = _conv_taps(_conv_windows(ext, q), cw_ref) + cb_ref[...]
        ext[0:SUBLANES, :] = ext[q:q + SUBLANES, :]
        xbc_s[...] = xc * _sigmoid(xc)
        st_ref[0] = state[...]
        dt, _, cum, cum_t, _, row, col = _ssd_chunk_terms(dtr_ref[...], bias_ref[...], alog_ref[...])
        causal = row >= col
        expand, _ = _head_maps()
        dt_full = _dot_mid(dt, expand)
        cum_full = _dot_mid(cum, expand)
        last_full = cum_full[q - 1:q, :]
        xs = xbc_s[:, 0:di]
        xdt = xs * dt_full
        xd = (xdt * jnp.exp(last_full - cum_full)).astype(BF16)
        xdt_b = xdt.astype(BF16)
        e_full = jnp.exp(cum_full)
        eend_full = jnp.exp(last_full)
        for g in range(SSD_GROUPS):
            slab = slice(g * gw, (g + 1) * gw)
            bg = xbc_s[:, di + g * n:di + (g + 1) * n].astype(BF16)
            cg = xbc_s[:, di + (SSD_GROUPS + g) * n:di + (SSD_GROUPS + g + 1) * n].astype(BF16)
            cb = _dot_nt(cg, bg)
            st_g = state[:, slab]
            pairs = []
            for pr in range(SSD_HPG // 2):
                h0 = g * SSD_HPG + 2 * pr
                m0 = (cb * _ssd_decay(cum, cum_t, h0, causal)).astype(BF16)
                m1 = (cb * _ssd_decay(cum, cum_t, h0 + 1, causal)).astype(BF16)
                pairs.append(_dot(jnp.concatenate([m0, m1], axis=1), _pair_block_diag(xdt_b[:, h0 * p:(h0 + 2) * p])))
            y = jnp.concatenate(pairs, axis=1) + e_full[:, slab] * _dot(cg, st_g.astype(BF16))
            yssd_ref[:, slab] = y + d_ref[:, slab] * xs[:, slab]
            state[:, slab] = eend_full[:, slab] * st_g + _dot_tn(bg, xd[:, slab])
        zv = z_ref[...].astype(F32)
        ya1 = yssd_ref[...] * (zv * _sigmoid(zv))
        for g in range(SSD_GROUPS):
            gsl = slice(g * gw, (g + 1) * gw)
            sl = ya1[:, gsl]
            rg = lax.rsqrt(jnp.mean(sl * sl, axis=-1, keepdims=True) + EPS)
            ya_ref[:, gsl] = (sl * rg * g_ref[:, gsl]).astype(BF16)

    blk = lambda w, j: pl.BlockSpec((q, w), lambda c: (c, j))
    return _tiled_call(
        body, name=name, grid=(nc,),
        in_specs=[blk(CONV_DIM, 0), _full((SUBLANES, CONV_DIM)), _full((1, CONV_DIM)), blk(LANES, 0), blk(di, 0),
                  _full((1, LANES)), _full((1, LANES)), _full((1, di)), _full((1, di))],
        out_specs=[blk(di, 0), blk(di, 0), pl.BlockSpec((1, n, di), lambda c: (c, 0, 0))],
        out_shape=[jax.ShapeDtypeStruct((t, di), BF16), jax.ShapeDtypeStruct((t, di), F32),
                   jax.ShapeDtypeStruct((nc, n, di), F32)],
        scratch_shapes=[pltpu.VMEM((n, di), F32), pltpu.VMEM((SUBLANES + q, CONV_DIM), F32), pltpu.VMEM((q, CONV_DIM), F32)],
        operands=(xbc_raw, conv_w8, conv_b.reshape(1, CONV_DIM), dtr, z, bias_row, alog_row, d_row,
                  norm_g.reshape(1, di)), comm=comm)


def ssd_bwd(dya, yssd, z, xbc_raw, conv_w8, conv_b, dtr, states, bias_row, alog_row, d_row, norm_g, *, name, comm=None):
    t = xbc_raw.shape[0]
    q, p, n = SSD_CHUNK, SSD_HEADDIM, SSD_STATE
    nc = t // q
    di = SSD_HEADS * p
    gw = di // SSD_GROUPS
    per = q // (2 * SUBLANES)

    def body(dya_ref, yssd_ref, z_ref, raw_ref, prev_ref, cw_ref, cb_ref, dtr_ref, st_ref, bias_ref, alog_ref, d_ref,
             g_ref, dz_ref, draw_ref, ddtr_ref, dng_ref, dd_ref, dalog_ref, dbias_ref, dcb_ref, dcw_ref,
             dstate, dyssd, xbc_s, dxbc_ref, ext, aft):
        ci = pl.program_id(0)

        @pl.when(ci == 0)
        def _():
            dstate[...] = jnp.zeros_like(dstate)
            aft[q:q + SUBLANES, :] = jnp.zeros((SUBLANES, CONV_DIM), F32)
            for r in (dng_ref, dd_ref, dalog_ref, dbias_ref, dcb_ref, dcw_ref):
                r[...] = jnp.zeros_like(r)

        ext[0:SUBLANES, :] = jnp.where(ci == nc - 1, 0.0, prev_ref[SUBLANES:2 * SUBLANES, :].astype(F32))
        ext[SUBLANES:SUBLANES + q, :] = raw_ref[...].astype(F32)
        windows = _conv_windows(ext, q)
        xc = _conv_taps(windows, cw_ref) + cb_ref[...]
        sig_c = _sigmoid(xc)
        xbc_s[...] = xc * sig_c

        zv = z_ref[...].astype(F32)
        sz = _sigmoid(zv)
        silu_z = zv * sz
        yv = yssd_ref[...]
        ya1 = yv * silu_z
        dyav = dya_ref[...]
        for g in range(SSD_GROUPS):
            gsl = slice(g * gw, (g + 1) * gw)
            sl = ya1[:, gsl]
            rg = lax.rsqrt(jnp.mean(sl * sl, axis=-1, keepdims=True) + EPS)
            ya2 = sl * rg
            dy_g = dyav[:, gsl]
            dng_ref[:, gsl] += jnp.sum(dy_g * ya2, axis=0, keepdims=True)
            dya2 = dy_g * g_ref[:, gsl]
            dya1 = rg * (dya2 - ya2 * jnp.mean(dya2 * ya2, axis=-1, keepdims=True))
            dyssd[:, gsl] = dya1 * silu_z[:, gsl]
            dz_ref[:, gsl] = (dya1 * yv[:, gsl] * (sz[:, gsl] * (1.0 + zv[:, gsl] * (1.0 - sz[:, gsl])))).astype(BF16)

        dtr_v = dtr_ref[...]
        dt, a_row, cum, cum_t, ltri, row, col = _ssd_chunk_terms(dtr_v, bias_ref[...], alog_ref[...])
        causal = row >= col
        expand, collapse = _head_maps()
        lane = lax.broadcasted_iota(jnp.int32, (1, LANES), 1)
        sub = lax.broadcasted_iota(jnp.int32, (LANES, 1), 0)
        is_last = lax.broadcasted_iota(jnp.int32, (q, 1), 0) == q - 1
        low = lax.broadcasted_iota(jnp.int32, (q, 2 * p), 1) < p
        dt_full = _dot_mid(dt, expand)
        cum_full = _dot_mid(cum, expand)
        last_full = cum_full[q - 1:q, :]
        e_full = jnp.exp(cum_full)
        eend_full = jnp.exp(last_full)
        dend_full = jnp.exp(last_full - cum_full)
        xs = xbc_s[:, 0:di]
        xdt = xs * dt_full
        xdt_b = xdt.astype(BF16)
        xd_b = (xdt * dend_full).astype(BF16)
        dy_all = dyssd[...]
        dy_b = dy_all.astype(BF16)
        dg_b = (dy_all * e_full).astype(BF16)
        dcum_mat = jnp.zeros((q, LANES), F32)
        rmat = jnp.zeros((LANES, q), F32)
        yoff_prod, bds_list, dx_list, dlast_parts = [], [], [], []
        for g in range(SSD_GROUPS):
            slab = slice(g * gw, (g + 1) * gw)
            bg = xbc_s[:, di + g * n:di + (g + 1) * n].astype(BF16)
            cg = xbc_s[:, di + (SSD_GROUPS + g) * n:di + (SSD_GROUPS + g + 1) * n].astype(BF16)
            cb = _dot_nt(cg, bg)
            st_g = st_ref[0, :, slab]
            st_b = st_g.astype(BF16)
            dsn = dstate[:, slab]
            dsn_b = dsn.astype(BF16)
            yoff_prod.append(dy_all[:, slab] * e_full[:, slab] * _dot(cg, st_b))
            dcg = _dot_nt(dg_b[:, slab], st_b)
            dstate[:, slab] = _dot_tn(cg, dg_b[:, slab]) + eend_full[:, slab] * dsn
            bds_list.append(_dot(bg, dsn_b))
            dbg = _dot_nt(xd_b[:, slab], dsn_b)
            dlast_parts.append(jnp.sum(dsn * st_g, axis=0, keepdims=True))
            dcb = jnp.zeros((q, q), F32)
            dx_pairs = []
            for pr in range(SSD_HPG // 2):
                h0 = g * SSD_HPG + 2 * pr
                ps = slice(h0 * p, (h0 + 2) * p)
                l0 = _ssd_decay(cum, cum_t, h0, causal)
                l1 = _ssd_decay(cum, cum_t, h0 + 1, causal)
                m0, m1 = cb * l0, cb * l1
                dyp = dy_b[:, ps]
                zero = jnp.zeros_like(dyp)
                dy0, dy1 = jnp.where(low, dyp, zero), jnp.where(low, zero, dyp)
                dm0 = _dot_nt(dy0, xdt_b[:, ps])
                dm1 = _dot_nt(dy1, xdt_b[:, ps])
                w0, w1 = dm0 * m0, dm1 * m1
                dcum_mat = (dcum_mat + jnp.sum(w0, axis=1, keepdims=True) * (lane == h0).astype(F32)
                            + jnp.sum(w1, axis=1, keepdims=True) * (lane == h0 + 1).astype(F32))
                rmat = (rmat + jnp.where(sub == h0, jnp.sum(w0, axis=0, keepdims=True), 0.0)
                        + jnp.where(sub == h0 + 1, jnp.sum(w1, axis=0, keepdims=True), 0.0))
                dcb = dcb + dm0 * l0 + dm1 * l1
                dx_pairs.append(_dot_tn(jnp.concatenate([m0.astype(BF16), m1.astype(BF16)], axis=0),
                                        jnp.concatenate([dy0, dy1], axis=0)))
            dx_list.append(jnp.concatenate(dx_pairs, axis=1))
            dcb_b = dcb.astype(BF16)
            dxbc_ref[:, di + g * n:di + (g + 1) * n] = dbg + _dot_tn(dcb_b, cg)
            dxbc_ref[:, di + SSD_GROUPS * n + g * n:di + SSD_GROUPS * n + (g + 1) * n] = dcg + _dot(dcb_b, bg)
        bds_all = jnp.concatenate(bds_list, axis=1)
        dx_all = jnp.concatenate(dx_list, axis=1) + dend_full * bds_all
        last_row = cum[q - 1:q, :]
        qv = _dot_mid(xdt * bds_all, collapse) * jnp.exp(last_row - cum)
        dcum_mat = dcum_mat + _dot_mid(jnp.concatenate(yoff_prod, axis=1), collapse) - qv
        dlast_full = jnp.broadcast_to(jnp.concatenate(dlast_parts, axis=1), (SUBLANES, di))
        dlast_row = _dot_mid(dlast_full, collapse)[0:1, :] * jnp.exp(last_row) + jnp.sum(qv, axis=0, keepdims=True)
        dcum_mat = dcum_mat + jnp.where(is_last, dlast_row, 0.0)
        dla = _dot_tn_hi(ltri, dcum_mat) - _dot_nt_hi((row <= col).astype(F32), rmat)
        ddt_mat = _dot_mid(dx_all * xs, collapse) + dla * a_row
        dxbc_ref[:, 0:di] = d_ref[...] * dy_all + dx_all * dt_full
        ddtr = ddt_mat * _sigmoid(dtr_v + bias_ref[...])
        ddtr_ref[...] = ddtr.astype(BF16)
        dd_full = jnp.broadcast_to(jnp.sum(dy_all * xs, axis=0, keepdims=True), (SUBLANES, di))
        dd_ref[...] += _dot_mid(dd_full, collapse)[0:1, :]
        dalog_ref[...] += jnp.sum(dla * dt, axis=0, keepdims=True) * a_row
        dbias_ref[...] += jnp.sum(ddtr, axis=0, keepdims=True)

        dxc = dxbc_ref[...] * (sig_c * (1.0 + xc * (1.0 - sig_c)))
        dcb_ref[...] += jnp.sum(dxc, axis=0, keepdims=True)
        taps = [jnp.sum(dxc * windows[k], axis=0, keepdims=True) for k in range(SSD_CONV)]
        dcw_ref[...] += jnp.concatenate(taps + [jnp.zeros((SUBLANES - SSD_CONV, CONV_DIM), F32)], axis=0)
        aft[0:q, :] = dxc
        draw = dxc * cw_ref[3:4, :]
        for j in range(1, SSD_CONV):
            draw = draw + aft[j:j + q, :] * cw_ref[3 - j:4 - j, :]
        draw_ref[...] = draw.astype(BF16)
        aft[q:q + SUBLANES, :] = dxc[0:SUBLANES, :]

    blk = lambda w, j: pl.BlockSpec((q, w), lambda c: (nc - 1 - c, j))
    row_out = lambda w: jax.ShapeDtypeStruct((1, w), F32)
    return _tiled_call(
        body, name=name, grid=(nc,),
        in_specs=[blk(di, 0), blk(di, 0), blk(di, 0), blk(CONV_DIM, 0),
                  pl.BlockSpec((2 * SUBLANES, CONV_DIM), lambda c: (jnp.maximum((nc - 1 - c) * per - 1, 0), 0)),
                  _full((SUBLANES, CONV_DIM)), _full((1, CONV_DIM)), blk(LANES, 0),
                  pl.BlockSpec((1, n, di), lambda c: (nc - 1 - c, 0, 0)),
                  _full((1, LANES)), _full((1, LANES)), _full((1, di)), _full((1, di))],
        out_specs=[blk(di, 0), blk(CONV_DIM, 0), blk(LANES, 0),
                   _full((1, di)), _full((1, LANES)), _full((1, LANES)), _full((1, LANES)),
                   _full((1, CONV_DIM)), _full((SUBLANES, CONV_DIM))],
        out_shape=[jax.ShapeDtypeStruct((t, di), BF16), jax.ShapeDtypeStruct((t, CONV_DIM), BF16),
                   jax.ShapeDtypeStruct((t, LANES), BF16), row_out(di), row_out(LANES), row_out(LANES), row_out(LANES),
                   row_out(CONV_DIM), jax.ShapeDtypeStruct((SUBLANES, CONV_DIM), F32)],
        scratch_shapes=[pltpu.VMEM((n, di), F32), pltpu.VMEM((q, di), F32), pltpu.VMEM((q, CONV_DIM), F32),
                        pltpu.VMEM((q, CONV_DIM), F32), pltpu.VMEM((SUBLANES + q, CONV_DIM), F32),
                        pltpu.VMEM((q + SUBLANES, CONV_DIM), F32)],
        operands=(dya, yssd, z, xbc_raw, xbc_raw, conv_w8, conv_b.reshape(1, CONV_DIM), dtr, states, bias_row,
                  alog_row, d_row, norm_g.reshape(1, di)),
        comm=comm)


S5_BLOCKS = 4
S5_BC = S5_WIDTH // S5_BLOCKS
S5_BS = S5_N // S5_BLOCKS


def _s5_tables(ar, ai, reverse):
    pows = [(ar, ai)]
    for _ in range(SUBLANES - 1):
        pr, pi = pows[-1]
        pows.append((pr * ar - pi * ai, pr * ai + pi * ar))
    row = lax.broadcasted_iota(jnp.int32, (SUBLANES, ar.shape[1]), 0)
    out = []
    for k in (1, 2, 4):
        pr, pi = pows[k - 1]
        mask = (row < SUBLANES - k) if reverse else (row >= k)
        out += [jnp.where(mask, pr, 0.0), jnp.where(mask, pi, 0.0)]
    order = list(range(SUBLANES))
    if reverse:
        order = order[::-1]
    out.append(jnp.concatenate([pows[e][0] for e in order], axis=0))
    out.append(jnp.concatenate([pows[e][1] for e in order], axis=0))
    return out


def s5_interleave(a, tm):
    t, c = a.shape
    return a.reshape(t // tm, SUBLANES, tm // SUBLANES, c).transpose(0, 2, 1, 3).reshape(t, c)


def s5_deinterleave(a, tm):
    t, c = a.shape
    return a.reshape(t // tm, tm // SUBLANES, SUBLANES, c).transpose(0, 2, 1, 3).reshape(t, c)


def _s5_scan_setup(ab_ref, base, tab, seg, reverse):
    ar = ab_ref[0:1, :]
    ai = -ab_ref[1:2, :] if reverse else ab_ref[1:2, :]
    base[0:1, :] = ar
    base[1:2, :] = ai
    assert seg & (seg - 1) == 0
    pr, pi = ar, ai
    for _ in range(seg.bit_length() - 1):
        pr, pi = pr * pr - pi * pi, 2.0 * pr * pi
    for k, tv in enumerate(_s5_tables(pr, pi, reverse)):
        tab[k] = tv


def _s5_scan(s_scr, base, tab, carry, tm, reverse):
    seg = tm // SUBLANES
    width = S5_BS
    first = lax.broadcasted_iota(jnp.int32, (SUBLANES, width), 0) == (SUBLANES - 1 if reverse else 0)

    def rows(i):
        step = (seg - 1 - i) if reverse else i
        return pl.ds(pl.multiple_of(step * SUBLANES, SUBLANES), SUBLANES)

    for lc in range(S5_N // width):
        lr = slice(lc * width, (lc + 1) * width)
        li = slice(S5_N + lc * width, S5_N + (lc + 1) * width)
        ar = jnp.broadcast_to(base[0:1, lr], (SUBLANES, width))
        ai = jnp.broadcast_to(base[1:2, lr], (SUBLANES, width))

        def advance(i, x, lr=lr, li=li, ar=ar, ai=ai):
            xr, xi = x
            return ar * xr - ai * xi + s_scr[rows(i), lr], ar * xi + ai * xr + s_scr[rows(i), li]

        zero = jnp.zeros((SUBLANES, width), F32)
        fr, fi = lax.fori_loop(0, seg, advance, (zero, zero))

        tb = [tab[k, :, lr] for k in range(8)]
        for lvl, k in enumerate((1, 2, 4)):
            sh = (SUBLANES - k) if reverse else k
            pr, pi = tb[2 * lvl], tb[2 * lvl + 1]
            rr, ri = pltpu.roll(fr, sh, 0), pltpu.roll(fi, sh, 0)
            fr, fi = fr + pr * rr - pi * ri, fi + pr * ri + pi * rr
        cr, ci = carry[0:1, lr], carry[0:1, li]
        fr, fi = fr + tb[6] * cr - tb[7] * ci, fi + tb[6] * ci + tb[7] * cr
        last = 0 if reverse else SUBLANES - 1
        carry[0:1, lr] = fr[last:last + 1, :]
        carry[0:1, li] = fi[last:last + 1, :]
        sh = (SUBLANES - 1) if reverse else 1
        entering = (jnp.where(first, cr, pltpu.roll(fr, sh, 0)), jnp.where(first, ci, pltpu.roll(fi, sh, 0)))

        def rerun(i, x, lr=lr, li=li, advance=advance):
            xr, xi = advance(i, x)
            s_scr[rows(i), lr] = xr
            s_scr[rows(i), li] = xi
            return xr, xi

        lax.fori_loop(0, seg, rerun, entering)


def _s5_scratch(tm):
    return [pltpu.VMEM((tm, 2 * S5_N), F32), pltpu.VMEM((SUBLANES, S5_N), F32),
            pltpu.VMEM((8, SUBLANES, S5_N), F32), pltpu.VMEM((SUBLANES, 2 * S5_N), F32)]


def _s5_put(scr, j, val):
    scr[:, j * S5_BS:(j + 1) * S5_BS] = val[:, :S5_BS]
    scr[:, S5_N + j * S5_BS:S5_N + (j + 1) * S5_BS] = val[:, S5_BS:]


def _s5_get(scr, j):
    return jnp.concatenate([scr[:, j * S5_BS:(j + 1) * S5_BS], scr[:, S5_N + j * S5_BS:S5_N + (j + 1) * S5_BS]], axis=1)


def s5_fwd(u5, bbc, ccc, ab8, d_row, wglu, bglu_row, *, tm, name, comm=None):
    t, w = u5.shape

    def body(u_ref, bb_ref, cc_ref, ab_ref, d_ref, wg_ref, bg_ref, s_ref, ys_ref, yb_ref, s_scr, base, tab, carry):
        i = pl.program_id(0)

        @pl.when(i == 0)
        def _():
            carry[...] = jnp.zeros_like(carry)
            _s5_scan_setup(ab_ref, base, tab, tm // SUBLANES, False)

        ub = u_ref[...]
        u = ub.astype(F32)
        for j in range(S5_BLOCKS):
            uj = ub[:, j * S5_BC:(j + 1) * S5_BC]
            _s5_put(s_scr, j, _dot(uj, bb_ref[j * S5_BC:(j + 1) * S5_BC, :]))
        _s5_scan(s_scr, base, tab, carry, tm, False)
        s_ref[...] = s_scr[...]
        ys = []
        for j in range(S5_BLOCKS):
            ys.append(_dot(_s5_get(s_scr, j).astype(BF16), cc_ref[:, j * S5_BC:(j + 1) * S5_BC]))
        y = jnp.concatenate(ys, axis=1) + d_ref[...] * u
        ys_ref[...] = y
        yb1 = _gelu(y)
        tt = _dot(yb1.astype(BF16), wg_ref[...]) + bg_ref[...]
        yb_ref[...] = (yb1 * _sigmoid(tt)).astype(BF16)

    tile = lambda c: pl.BlockSpec((tm, c), lambda i: (i, 0))
    return _tiled_call(
        body, name=name, grid=(t // tm,),
        in_specs=[tile(w), _full((w, 2 * S5_BS)), _full((2 * S5_BS, w)), _full((SUBLANES, S5_N)), _full((1, w)),
                  _full((w, w)), _full((1, w))],
        out_specs=[tile(2 * S5_N), tile(w), tile(w)],
        out_shape=[jax.ShapeDtypeStruct((t, 2 * S5_N), F32), jax.ShapeDtypeStruct((t, w), F32),
                   jax.ShapeDtypeStruct((t, w), BF16)],
        scratch_shapes=_s5_scratch(tm),
        operands=(u5, bbc, ccc, ab8, d_row, wglu, bglu_row), comm=comm)


def s5_bwd(dyb, ys5, u5, s, bbc, ccc, ab8, d_row, wglu, bglu_row, *, tm, name, comm=None):
    t, w = u5.shape
    nt = t // tm
    per = tm // SUBLANES

    def body(dyb_ref, ys_ref, u_ref, s_ref, sprev_ref, bb_ref, cc_ref, ab_ref, d_ref, wg_ref, bg_ref,
             du_ref, dbb_ref, dcc_ref, dab_ref, dd_ref, dwg_ref, dbg_ref, g_scr, base, tab, carry):
        i = pl.program_id(0)

        @pl.when(i == 0)
        def _():
            carry[...] = jnp.zeros_like(carry)
            _s5_scan_setup(ab_ref, base, tab, tm // SUBLANES, True)
            for r in (dbb_ref, dcc_ref, dab_ref, dd_ref, dwg_ref, dbg_ref):
                r[...] = jnp.zeros_like(r)

        y = ys_ref[...]
        ub = u_ref[...]
        u = ub.astype(F32)
        yb1 = _gelu(y)
        yb1b = yb1.astype(BF16)
        sg = _sigmoid(_dot(yb1b, wg_ref[...]) + bg_ref[...])
        dybv = dyb_ref[...]
        dtt = dybv * yb1 * sg * (1.0 - sg)
        dttb = dtt.astype(BF16)
        dyb1 = dybv * sg + _dot_nt(dttb, wg_ref[...])
        dwg_ref[...] += _dot_tn(yb1b, dttb)
        dbg_ref[...] += jnp.sum(dtt, axis=0, keepdims=True)
        dy = dyb1 * _gelu_grad(y)
        dd_ref[...] += jnp.sum(dy * u, axis=0, keepdims=True)
        dyh = dy.astype(BF16)
        for j in range(S5_BLOCKS):
            cs = slice(j * S5_BC, (j + 1) * S5_BC)
            _s5_put(g_scr, j, _dot_nt(dyh[:, cs], cc_ref[:, cs]))
        _s5_scan(g_scr, base, tab, carry, tm, True)
        dus = []
        for j in range(S5_BLOCKS):
            cs = slice(j * S5_BC, (j + 1) * S5_BC)
            gb = _s5_get(g_scr, j).astype(BF16)
            dus.append(_dot_nt(gb, bb_ref[cs, :]))
            dbb_ref[cs, :] += _dot_tn(ub[:, cs], gb)
            dcc_ref[:, cs] += _dot_tn(_s5_get(s_ref, j).astype(BF16), dyh[:, cs])
        du_ref[...] = (jnp.concatenate(dus, axis=1) + d_ref[...] * dy).astype(BF16)
        top = lax.broadcasted_iota(jnp.int32, (SUBLANES, S5_BS), 0) == 0

        def corr(g_sl, s_sl):
            before = jnp.where(i == nt - 1, 0.0, sprev_ref[SUBLANES - 1:SUBLANES, s_sl])
            wrap = jnp.where(top, before, pltpu.roll(s_ref[tm - SUBLANES:tm, s_sl], 1, 0))
            return (jnp.sum(g_scr[SUBLANES:tm, g_sl] * s_ref[0:tm - SUBLANES, s_sl], axis=0, keepdims=True)
                    + jnp.sum(g_scr[0:SUBLANES, g_sl] * wrap, axis=0, keepdims=True))

        for j in range(S5_BLOCKS):
            lr = slice(j * S5_BS, (j + 1) * S5_BS)
            li = slice(S5_N + j * S5_BS, S5_N + (j + 1) * S5_BS)
            dab_ref[0:1, lr] += corr(lr, lr) + corr(li, li)
            dab_ref[1:2, lr] += corr(li, lr) - corr(lr, li)

    tile = lambda c: pl.BlockSpec((tm, c), lambda i: (nt - 1 - i, 0))
    acc = lambda r, c: jax.ShapeDtypeStruct((r, c), F32)
    return _tiled_call(
        body, name=name, grid=(nt,),
        in_specs=[tile(w), tile(w), tile(w), tile(2 * S5_N),
                  pl.BlockSpec((SUBLANES, 2 * S5_N), lambda i: (jnp.maximum((nt - 1 - i) * per - 1, 0), 0)),
                  _full((w, 2 * S5_BS)), _full((2 * S5_BS, w)), _full((SUBLANES, S5_N)), _full((1, w)),
                  _full((w, w)), _full((1, w))],
        out_specs=[tile(w), _full((w, 2 * S5_BS)), _full((2 * S5_BS, w)), _full((SUBLANES, S5_N)), _full((1, w)),
                   _full((w, w)), _full((1, w))],
        out_shape=[jax.ShapeDtypeStruct((t, w), BF16), acc(w, 2 * S5_BS), acc(2 * S5_BS, w), acc(SUBLANES, S5_N),
                   acc(1, w), acc(w, w), acc(1, w)],
        scratch_shapes=_s5_scratch(tm),
        operands=(dyb, ys5, u5, s, s, bbc, ccc, ab8, d_row, wglu, bglu_row), comm=comm)


def s5_discretize(a_re, a_im, log_dt, b_re, b_im, *, name):
    n = a_re.shape[0]

    def body(ar_ref, ai_ref, dt_ref, br_ref, bi_ref, ab_ref, bbr_ref, bbi_ref):
        ar, ai, dtv = ar_ref[...], ai_ref[...], jnp.exp(dt_ref[...])
        mag = jnp.exp(ar * dtv)
        abr = mag * jnp.cos(ai * dtv)
        abi = mag * jnp.sin(ai * dtv)
        den = ar * ar + ai * ai
        nr = abr - 1.0
        cr = (nr * ar + abi * ai) / den
        ci = (abi * ar - nr * ai) / den
        ab_ref[:, 0:1] = abr
        ab_ref[:, 1:2] = abi
        br, bi = br_ref[...], bi_ref[...]
        bbr_ref[...] = cr * br - ci * bi
        bbi_ref[...] = cr * bi + ci * br

    col = jax.ShapeDtypeStruct((n, 2), F32)
    mat = jax.ShapeDtypeStruct((n, S5_GROUP), F32)
    return pl.pallas_call(body, name=name, out_shape=[col, mat, mat])(a_re, a_im, log_dt, b_re, b_im)


def s5_discretize_bwd(a_re, a_im, log_dt, b_re, b_im, dab, dbb_re, dbb_im, *, name):
    n = a_re.shape[0]

    def body(ar_ref, ai_ref, dt_ref, br_ref, bi_ref, dab_ref, dbr_ref, dbi_ref, da_ref, ddt_ref, gbr_ref, gbi_ref):
        ar, ai, dtv = ar_ref[...], ai_ref[...], jnp.exp(dt_ref[...])
        mag = jnp.exp(ar * dtv)
        abr = mag * jnp.cos(ai * dtv)
        abi = mag * jnp.sin(ai * dtv)
        den = ar * ar + ai * ai
        nr = abr - 1.0
        cr = (nr * ar + abi * ai) / den
        ci = (abi * ar - nr * ai) / den
        br, bi = br_ref[...], bi_ref[...]
        dbr, dbi = dbr_ref[...], dbi_ref[...]
        gbr_ref[...] = cr * dbr + ci * dbi
        gbi_ref[...] = cr * dbi - ci * dbr
        gcr = jnp.sum(dbr * br + dbi * bi, axis=1, keepdims=True)
        gci = jnp.sum(dbi * br - dbr * bi, axis=1, keepdims=True)
        ilr, ili = ar / den, -ai / den
        gabr = dab_ref[:, 0:1] + (ilr * gcr + ili * gci)
        gabi = dab_ref[:, 1:2] + (ilr * gci - ili * gcr)
        t1r, t1i = dtv * abr, dtv * abi
        t2r, t2i = -(cr * ilr - ci * ili), -(cr * ili + ci * ilr)
        da_ref[:, 0:1] = (t1r * gabr + t1i * gabi) + (t2r * gcr + t2i * gci)
        da_ref[:, 1:2] = (t1r * gabi - t1i * gabr) + (t2r * gci - t2i * gcr)
        lr, li = ar * abr - ai * abi, ar * abi + ai * abr
        dlog = (lr * gabr + li * gabi) * dtv
        ddt_ref[...] = jnp.sum(dlog.reshape(S5_GROUPS, S5_STATE, 1), axis=1)

    col2 = jax.ShapeDtypeStruct((n, 2), F32)
    col1 = jax.ShapeDtypeStruct((S5_GROUPS, 1), F32)
    mat = jax.ShapeDtypeStruct((n, S5_GROUP), F32)
    return pl.pallas_call(body, name=name, out_shape=[col2, col1, mat, mat])(
        a_re, a_im, log_dt, b_re, b_im, dab, dbb_re, dbb_im)


def merge_fwd(ya, yb, graw, x, wb, wout, *, tm, name):
    t, d = x.shape

    def body(ya_ref, yb_ref, g_ref, x_ref, wb_ref, wo_ref, x1_ref, pa_ref, pb_ref):
        pa = _dot(ya_ref[...], wb_ref[0:d, :])
        pb = _dot(yb_ref[...], wb_ref[d:, :])
        gate = _sigmoid(g_ref[...].astype(F32))
        merged = gate[:, :d] * pa + gate[:, d:] * pb
        x1_ref[...] = x_ref[...] + _dot(merged.astype(BF16), wo_ref[...])
        pa_ref[...] = pa.astype(BF16)
        pb_ref[...] = pb.astype(BF16)

    tile = lambda c: pl.BlockSpec((tm, c), lambda i: (i, 0))
    sds = jax.ShapeDtypeStruct
    return pl.pallas_call(
        body, name=name, grid=(t // tm,),
        in_specs=[tile(d), tile(S5_WIDTH), tile(2 * d), tile(d), _full((d + S5_WIDTH, d)), _full((d, d))],
        out_specs=[tile(d), tile(d), tile(d)], out_shape=[sds((t, d), F32), sds((t, d), BF16), sds((t, d), BF16)],
        compiler_params=_params("parallel"),
    )(ya, yb, graw, x, wb, wout)


def merge_bwd(dx1b, graw, pa, pb, wb, wout, *, tm, name):
    t, d = pa.shape

    def body(dx_ref, g_ref, pa_ref, pb_ref, wb_ref, wo_ref,
             mg_ref, dg_ref, dpa_ref, dpb_ref, dya_ref, dyb_ref):
        dm = _dot_nt(dx_ref[...], wo_ref[...])
        gate = _sigmoid(g_ref[...].astype(F32))
        g0, g1 = gate[:, :d], gate[:, d:]
        pa, pb = pa_ref[...].astype(F32), pb_ref[...].astype(F32)
        mg_ref[...] = (g0 * pa + g1 * pb).astype(BF16)
        dg_ref[:, :d] = (dm * pa * g0 * (1.0 - g0)).astype(BF16)
        dg_ref[:, d:] = (dm * pb * g1 * (1.0 - g1)).astype(BF16)
        dpa = (dm * g0).astype(BF16)
        dpb = (dm * g1).astype(BF16)
        dpa_ref[...] = dpa
        dpb_ref[...] = dpb
        dya_ref[...] = _dot_nt(dpa, wb_ref[0:d, :])
        dyb_ref[...] = _dot_nt(dpb, wb_ref[d:, :])

    tile = lambda c: pl.BlockSpec((tm, c), lambda i: (i, 0))
    sds = jax.ShapeDtypeStruct
    return pl.pallas_call(
        body, name=name, grid=(t // tm,),
        in_specs=[tile(d), tile(2 * d), tile(d), tile(d), _full((d + S5_WIDTH, d)), _full((d, d))],
        out_specs=[tile(d), tile(2 * d), tile(d), tile(d), tile(d), tile(S5_WIDTH)],
        out_shape=[sds((t, d), BF16), sds((t, 2 * d), BF16), sds((t, d), BF16), sds((t, d), BF16),
                   sds((t, d), F32), sds((t, S5_WIDTH), F32)],
        compiler_params=_params("parallel"),
    )(dx1b, graw, pa, pb, wb, wout)


def mlp_out_loss(a1, x1, w2, gf, target, *, tm, name):
    t, d = x1.shape
    f = a1.shape[1]

    def body(a_ref, x_ref, w_ref, g_ref, tg_ref, dx_ref, dxb_ref, loss_ref, dg_ref):
        i = pl.program_id(0)
        av = jnp.maximum(a_ref[...].astype(F32), 0.0)
        x2 = x_ref[...] + _dot((av * av).astype(BF16), w_ref[...])
        r = lax.rsqrt(jnp.mean(x2 * x2, axis=-1, keepdims=True) + EPS)
        xn = x2 * r
        err = xn * g_ref[...] - tg_ref[...]
        dyf = err * (1.0 / d)
        dxn = dyf * g_ref[...]
        dx = r * (dxn - xn * jnp.mean(dxn * xn, axis=-1, keepdims=True))
        dx_ref[...] = dx
        dxb_ref[...] = dx.astype(BF16)

        @pl.when(i == 0)
        def _():
            loss_ref[...] = jnp.zeros_like(loss_ref)
            dg_ref[...] = jnp.zeros_like(dg_ref)

        loss_ref[...] += jnp.sum(err * err) * (0.5 / d)
        dg_ref[...] += jnp.sum(dyf * xn, axis=0, keepdims=True)

    tile = lambda c: pl.BlockSpec((tm, c), lambda i: (i, 0))
    sds = jax.ShapeDtypeStruct
    return pl.pallas_call(
        body, name=name, grid=(t // tm,),
        in_specs=[tile(f), tile(d), _full((f, d)), _full((1, d)), tile(d)],
        out_specs=[tile(d), tile(d), _full((1, LANES)), _full((1, d))],
        out_shape=[sds((t, d), F32), sds((t, d), BF16), sds((1, LANES), F32), sds((1, d), F32)],
        compiler_params=_params("arbitrary"),
    )(a1, x1, w2, gf.reshape(1, d), target)


def mlp_bwd_act(dx2b, a1, w2, *, tm, name):
    t, f = a1.shape
    d = dx2b.shape[1]

    def body(dx_ref, a_ref, w_ref, o_ref):
        dact = _dot_nt(dx_ref[...], w_ref[...])
        o_ref[...] = (dact * 2.0 * jnp.maximum(a_ref[...].astype(F32), 0.0)).astype(BF16)

    tile = lambda c: pl.BlockSpec((tm, c), lambda i: (i, 0))
    return pl.pallas_call(
        body, name=name, grid=(t // tm,),
        in_specs=[tile(d), tile(f), _full((f, d))], out_specs=tile(f),
        out_shape=jax.ShapeDtypeStruct((t, f), BF16),
        compiler_params=_params("parallel"),
    )(dx2b, a1, w2)


ADAM_TILE_ELEMS = 128 * 1024


def _row_tile(rows, cap):
    if rows <= cap:
        return rows
    best = None
    for c in range(16, cap + 1, 16):
        if rows % c == 0:
            best = c
    assert best is not None, rows
    return best


def _device_order_sum(parts_ref):
    total = parts_ref[0].astype(F32)
    for k in range(1, N_DEV):
        total = total + parts_ref[k].astype(F32)
    return total


def _adamw_math(w, m, v, gparts_ref):
    g = _device_order_sum(gparts_ref)
    mn = ADAM_B1 * m + (1.0 - ADAM_B1) * g
    vn = ADAM_B2 * v + (1.0 - ADAM_B2) * (g * g)
    m_hat = mn / (1.0 - ADAM_B1 ** ADAM_STEP)
    v_hat = vn / (1.0 - ADAM_B2 ** ADAM_STEP)
    return g, -ADAM_LR * (m_hat / (jnp.sqrt(v_hat) + ADAM_EPS) + ADAM_WD * w), mn, vn


def adamw(w, m, v, gparts, *, name):
    rows, cols = w.shape
    tr = _row_tile(rows, max(16, ADAM_TILE_ELEMS // cols))

    def body(w_ref, m_ref, v_ref, g_ref, go_ref, d_ref, mo_ref, vo_ref):
        go_ref[...], d_ref[...], mo_ref[...], vo_ref[...] = _adamw_math(w_ref[...], m_ref[...], v_ref[...], g_ref)

    tile = pl.BlockSpec((tr, cols), lambda i: (i, 0))
    out = jax.ShapeDtypeStruct((rows, cols), F32)
    return pl.pallas_call(
        body, name=name, grid=(rows // tr,),
        in_specs=[tile, tile, tile, pl.BlockSpec((N_DEV, tr, cols), lambda i: (0, i, 0))],
        out_specs=[tile, tile, tile, tile], out_shape=[out, out, out, out],
        compiler_params=_params("parallel"),
    )(w, m, v, gparts)


def all_gather_arrays(blocks, *, name):
    na = len(blocks)

    def body(*refs):
        x_refs, out_refs = refs[:na], refs[na:2 * na]
        send_sems, recv_sems, local_sems = refs[2 * na:]
        x, y, c, _ = _place()
        me, sibling = (x, y, c), (x, y, 1 - c)
        chips = [(1 - x, y), (x, 1 - y), (1 - x, 1 - y)]

        def copy(a, k, blk, to, own=False):
            px, py, pc = blk
            dst = out_refs[a].at[4 * px + 2 * py + pc]
            return pltpu.make_async_remote_copy(
                src_ref=x_refs[a] if own else dst, dst_ref=dst,
                send_sem=send_sems.at[7 * a + k], recv_sem=recv_sems.at[7 * a + k], device_id=to, device_id_type=_MESH)

        mine = [pltpu.make_async_copy(x_refs[a], out_refs[a].at[4 * x + 2 * y + c], local_sems.at[a]) for a in range(na)]
        for cp in mine:
            cp.start()
        sent = []
        for a in range(na):
            first = [copy(a, 0, me, sibling, own=True)]
            first += [copy(a, 1 + j, me, (*chip, c), own=True) for j, chip in enumerate(chips)]
            for cp in first:
                cp.start()
            sent += first
        for a in range(na):
            for j, chip in enumerate(chips):
                copy(a, 1 + j, (*chip, c), me).wait_recv()
                fwd = copy(a, 4 + j, (*chip, c), sibling)
                fwd.start()
                sent.append(fwd)
        for a in range(na):
            copy(a, 0, sibling, me).wait_recv()
            for j, chip in enumerate(chips):
                copy(a, 4 + j, (*chip, 1 - c), me).wait_recv()
        for cp in sent:
            cp.wait_send()
        for cp in mine:
            cp.wait()

    any_spec = pl.BlockSpec(memory_space=pl.ANY)
    return pl.pallas_call(
        body, name=name, in_specs=[any_spec] * na, out_specs=[any_spec] * na,
        out_shape=[jax.ShapeDtypeStruct((N_DEV,) + b.shape, b.dtype) for b in blocks],
        scratch_shapes=[pltpu.SemaphoreType.DMA((7 * na,)), pltpu.SemaphoreType.DMA((7 * na,)),
                        pltpu.SemaphoreType.DMA((na,))],
        compiler_params=pltpu.CompilerParams(has_side_effects=True),
    )(*blocks)


def all_to_all(comm, *, name):
    na = len(comm)
    kinds = [k for k, _ in comm]

    def body(*refs):
        local, remote = _comm_copies(kinds, refs[:na], refs[na:2 * na], *refs[2 * na:])
        for cp in local + remote:
            cp.start()
        for cp in remote:
            cp.wait_recv()
        for cp in remote:
            cp.wait_send()
        for cp in local:
            cp.wait()

    any_spec = pl.BlockSpec(memory_space=pl.ANY)
    return pl.pallas_call(
        body, name=name, in_specs=[any_spec] * na, out_specs=[any_spec] * na,
        out_shape=_comm_out_shapes(comm),
        scratch_shapes=[pltpu.SemaphoreType.DMA((na * N_REL,)), pltpu.SemaphoreType.DMA((na * N_REL,)),
                        pltpu.SemaphoreType.DMA((na,))],
        compiler_params=pltpu.CompilerParams(has_side_effects=True),
    )(*[a for _, a in comm])


def adamw_whole(ws, ms, vs, gparts, *, name, totals=()):
    n, nt = len(ws), len(totals)

    def body(*refs):
        w_refs, m_refs, v_refs, g_refs = refs[:n], refs[n:2 * n], refs[2 * n:3 * n], refs[3 * n:4 * n]
        t_refs, refs = refs[4 * n:4 * n + nt], refs[4 * n + nt:]
        outs, t_outs = refs[:4 * n], refs[4 * n:]
        for k in range(n):
            g, delta, mn, vn = _adamw_math(w_refs[k][...], m_refs[k][...], v_refs[k][...], g_refs[k])
            outs[k][...] = g
            outs[n + k][...] = delta
            outs[2 * n + k][...] = mn
            outs[3 * n + k][...] = vn
        for t_ref, o_ref in zip(t_refs, t_outs):
            o_ref[...] = _device_order_sum(t_ref)

    shapes = [jax.ShapeDtypeStruct(w.shape, F32) for w in ws]
    res = pl.pallas_call(
        body, name=name, out_shape=shapes * 4 + [jax.ShapeDtypeStruct(t.shape[1:], F32) for t in totals],
        compiler_params=pltpu.CompilerParams(vmem_limit_bytes=VMEM_LIMIT),
    )(*ws, *ms, *vs, *gparts, *totals)
    return [res[j * n:(j + 1) * n] for j in range(4)], res[4 * n:]


def _lane_row(v):
    return jnp.pad(v.astype(F32), (0, LANES - v.shape[0])).reshape(1, LANES)


def _block_diag_b(bb):
    eye = jnp.eye(SUBLANES, dtype=F32)
    bt = bb.reshape(S5_BLOCKS, 8, S5_STATE, S5_GROUP).transpose(0, 1, 3, 2)
    return (bt[:, :, :, None, :] * eye[None, :, None, :, None]).reshape(S5_WIDTH, S5_BS)


def _block_diag_c(cc):
    eye = jnp.eye(SUBLANES, dtype=F32)
    ct = cc.reshape(S5_BLOCKS, 8, S5_GROUP, S5_STATE).transpose(3, 0, 1, 2)
    return (ct[None, :, :, :, :] * eye[:, None, None, :, None]).reshape(S5_BS, S5_WIDTH)


def _diag_blocks_b(m):
    eye = jnp.eye(SUBLANES, dtype=F32)
    m5 = m.reshape(S5_BLOCKS, 8, S5_GROUP, 8, S5_STATE)
    return jnp.sum(m5 * eye[None, :, None, :, None], axis=3).transpose(0, 1, 3, 2).reshape(S5_GROUPS, S5_STATE, S5_GROUP)


def _diag_blocks_c(m):
    eye = jnp.eye(SUBLANES, dtype=F32)
    m5 = m.reshape(8, S5_STATE, S5_BLOCKS, 8, S5_GROUP)
    return jnp.sum(m5 * eye[:, None, None, :, None], axis=0).transpose(1, 2, 3, 0).reshape(S5_GROUPS, S5_GROUP, S5_STATE)


def train_step(x, target, p, m, v):
    t = x.shape[0]
    tm = min(256, t)
    tb = min(512, t)
    bf = lambda n: p[n].astype(BF16)
    slots = lambda n, a: a.reshape((N_DEV,) + _shard_shape(n))
    updated = {}

    def update(names, got):
        for n, parts in zip(names, got):
            updated[n] = adamw(p[n], m[n], v[n], parts, name=f"adamw_{n}")

    st_in, st_conv = all_gather_arrays([bf("w_in"), p["conv_w"]], name="gather_first")
    win_p = jnp.concatenate([st_in[k][:, a:b] for seg in IN_PADDED_ORDER for k, a, b in _shard_pieces(*IN_SEGMENTS[seg])]
                            + [jnp.zeros((D_MODEL, LANES - SSD_HEADS), BF16)], axis=1)
    conv_w8 = jnp.pad(_join_shards("conv_w", st_conv), ((0, SUBLANES - SSD_CONV), (0, 0)))
    (h, xbc_raw, graw, z, u5, dtr), (st_branch, st_out, st_glu) = rms_matmul(
        x, p["norm_mix_g"], win_p, IN_SPLITS, tm=tb, name="in_proj", out_dtypes=(BF16, BF16, BF16, BF16, F32),
        comm=gathers([bf("w_branch"), bf("w_out"), bf("s5_glu_w")]))
    w_branch = st_branch.reshape(D_MODEL + S5_WIDTH, D_MODEL)
    wout, wglu = st_out.reshape(D_MODEL, D_MODEL), st_glu.reshape(S5_WIDTH, S5_WIDTH)
    bias_row, alog_row = _lane_row(p["dt_bias"]), _lane_row(p["a_log"])
    d_row = jnp.repeat(p["d_ssd"], SSD_HEADDIM).reshape(1, SSD_HEADS * SSD_HEADDIM)
    (ya, yssd, states), (w1,) = ssd_fwd(xbc_raw, conv_w8, p["conv_b"], dtr, z, bias_row, alog_row, d_row,
                                        p["ssd_norm_g"], name="ssd_fwd", comm=gathers([bf("w_mlp_in")]))

    n = S5_N
    a_re_c, a_im_c = p["s5_a_re"].reshape(n, 1), p["s5_a_im"].reshape(n, 1)
    dt_c = jnp.repeat(p["s5_log_dt"], S5_STATE).reshape(n, 1)
    b_re_m, b_im_m = p["s5_b_re"].reshape(n, S5_GROUP), p["s5_b_im"].reshape(n, S5_GROUP)
    ab, bb_re, bb_im = s5_discretize(a_re_c, a_im_c, dt_c, b_re_m, b_im_m, name="s5_disc")
    ab8 = jnp.pad(ab.T, ((0, SUBLANES - 2), (0, 0)))
    bbc = jnp.concatenate([_block_diag_b(bb_re.reshape(S5_GROUPS, S5_STATE, S5_GROUP)),
                           _block_diag_b(bb_im.reshape(S5_GROUPS, S5_STATE, S5_GROUP))], axis=1).astype(BF16)
    ccc = jnp.concatenate([_block_diag_c(p["s5_c_re"]), -_block_diag_c(p["s5_c_im"])], axis=0).astype(BF16)
    s5d_row = p["s5_d"].reshape(1, S5_WIDTH)
    bglu_row = p["s5_glu_b"].reshape(1, S5_WIDTH)
    u5 = s5_interleave(u5, tm)
    (s, ys5, yb), (st_w2,) = s5_fwd(u5, bbc, ccc, ab8, s5d_row, wglu, bglu_row, tm=tm, name="s5_fwd",
                                    comm=gathers([bf("w_mlp_out")]))
    yb = s5_deinterleave(yb, tm)
    w2 = st_w2.reshape(D_FF, D_MODEL)

    x1, pa, pb = merge_fwd(ya, yb, graw, x, w_branch, wout, tm=tb, name="merge_fwd")
    (h2, a1), _ = rms_matmul(x1, p["norm_mlp_g"], w1, (D_FF,), tm=tb, name="mlp_in", out_dtypes=(BF16,))
    dx2, dx2b, loss_row, dgf = mlp_out_loss(a1, x1, w2, p["norm_final_g"], target, tm=tb, name="mlp_out_loss")

    g = {}
    g["norm_final_g"] = dgf[0]
    da1 = mlp_bwd_act(dx2b, a1, w2, tm=tb, name="mlp_bwd_act")
    dw2 = slots("w_mlp_out", matmul_tn(a1, dx2b, out_dtype=BF16, a_relu2=True, name="dw_mlp_out"))
    dw1 = matmul_tn(h2, da1, out_dtype=BF16, col_shards=N_DEV, name="dw_mlp_in")
    (dx1, dx1b, dgm), _ = matmul_nt_rms_bwd([da1], w1, x1, p["norm_mlp_g"], dx2, tm=tb, name="mlp_in_bwd")
    g["norm_mlp_g"] = dgm[0]
    merged, dgraw, dpa, dpb, dya, dyb = merge_bwd(dx1b, graw, pa, pb, w_branch, wout, tm=tm, name="merge_bwd")
    dwo = slots("w_out", matmul_tn(merged, dx1b, out_dtype=BF16, name="dw_out"))
    dwb = slots("w_branch", jnp.concatenate([matmul_tn(ya, dpa, out_dtype=BF16, name="dw_branch_a"),
                                             matmul_tn(yb, dpb, out_dtype=BF16, name="dw_branch_b")], axis=0))

    (du5, dbbc, dccc, dab, dd5, dwglu, dbglu), got = s5_bwd(
        s5_interleave(dyb, tm), ys5, u5, s, bbc, ccc, ab8, s5d_row, wglu, bglu_row, tm=tm, name="s5_bwd",
        comm=exchanges([dw2, dw1]))
    du5 = s5_deinterleave(du5, tm)
    update(["w_mlp_out", "w_mlp_in"], got)
    g["s5_glu_b"], g["s5_d"] = dbglu[0], dd5[0]
    g["s5_c_re"] = _diag_blocks_c(dccc[:S5_BS])
    g["s5_c_im"] = -_diag_blocks_c(dccc[S5_BS:])
    dbb_re = _diag_blocks_b(dbbc[:, :S5_BS]).reshape(n, S5_GROUP)
    dbb_im = _diag_blocks_b(dbbc[:, S5_BS:]).reshape(n, S5_GROUP)
    da, dlogdt, gb_re, gb_im = s5_discretize_bwd(a_re_c, a_im_c, dt_c, b_re_m, b_im_m, dab[:2].T, dbb_re, dbb_im,
                                                  name="s5_disc_bwd")
    g["s5_a_re"] = da[:, 0].reshape(S5_GROUPS, S5_STATE)
    g["s5_a_im"] = da[:, 1].reshape(S5_GROUPS, S5_STATE)
    g["s5_log_dt"] = dlogdt[:, 0]
    g["s5_b_re"] = gb_re.reshape(S5_GROUPS, S5_STATE, S5_GROUP)
    g["s5_b_im"] = gb_im.reshape(S5_GROUPS, S5_STATE, S5_GROUP)

    def update_small(names, got, name, totals=()):
        (gs, ds, nm, nv), sums = adamw_whole([p[n] for n in names], [m[n] for n in names], [v[n] for n in names], got,
                                             name=name, totals=totals)
        for k, n in enumerate(names):
            updated[n] = (gs[k], ds[k], nm[k], nv[k])
        return sums

    wide = ["s5_b_re", "s5_b_im"]
    late = ["ssd_norm_g", "d_ssd", "a_log", "dt_bias", "conv_b"]
    early = [n for n in SMALL_ORDER if n not in wide + late and n != "norm_mix_g"]
    (dz, dxbc_raw, ddtr, dng, dd_row, dalog_row, dbias_row, dconv_b, dconv_w8), got = ssd_bwd(
        dya, yssd, z, xbc_raw, conv_w8, p["conv_b"], dtr, states, bias_row, alog_row, d_row, p["ssd_norm_g"],
        name="ssd_bwd",
        comm=exchanges([dwo, dwb, slots("s5_glu_w", dwglu.astype(BF16))])
        + gathers([g[n] for n in wide + early] + [loss_row]))
    update(["w_out", "w_branch", "s5_glu_w"], got[:3])
    update_small(wide, got[3:3 + len(wide)], "adamw_s5_b")
    loss_sum, = update_small(early, got[3 + len(wide):-1], "adamw_small_early", totals=[got[-1]])
    g["ssd_norm_g"] = dng[0]
    g["d_ssd"], g["a_log"], g["dt_bias"] = dd_row[0, :SSD_HEADS], dalog_row[0, :SSD_HEADS], dbias_row[0, :SSD_HEADS]
    g["conv_b"] = dconv_b[0]
    dconv = dconv_w8[:SSD_CONV].reshape(SSD_CONV, N_DEV, CONV_DIM // N_DEV).transpose(1, 0, 2)
    dproj =[dxbc_raw, dgraw, dz, du5, ddtr]
    dxbc_w, dgate_w, dz_w, du5_w, ddt_w = [matmul_tn(h, piece, out_dtype=BF16, name=f"dw_in_{k}")
                                           for k, piece in enumerate(dproj)]
    by_segment = {"z": dz_w, "xbc": dxbc_w, "dt": ddt_w, "u5": du5_w, "gates": dgate_w}
    width = D_IN_PROJ // N_DEV
    dw_in = jnp.stack([jnp.concatenate(
        [by_segment[seg][:, max(k * width, lo) - lo:min((k + 1) * width, hi) - lo]
         for seg, (lo, hi) in IN_SEGMENTS.items() if max(k * width, lo) < min((k + 1) * width, hi)], axis=1)
        for k in range(N_DEV)])

    (grad_x, _, dgx), got = matmul_nt_rms_bwd(
        dproj, win_p, x, p["norm_mix_g"], dx1, tm=tm, name="in_proj_bwd",
        comm=exchanges([dw_in, dconv]) + gathers([g[n] for n in late]))
    update(["w_in", "conv_w"], got[:2])
    update_small(late, got[2:], "adamw_small_late")
    update_small(["norm_mix_g"], all_to_all(gathers([dgx[0]]), name="gather_last"), "adamw_small_last")
    return loss_sum[0, 0], grad_x, updated


WEIGHT_ORDER = ["norm_mix_g", "w_in", "conv_w", "conv_b", "dt_bias", "a_log", "d_ssd", "ssd_norm_g", "s5_a_re",
                "s5_a_im", "s5_log_dt", "s5_b_re", "s5_b_im", "s5_c_re", "s5_c_im", "s5_d", "s5_glu_w", "s5_glu_b",
                "w_branch", "w_out", "norm_mlp_g", "w_mlp_in", "w_mlp_out", "norm_final_g"]
SHARDED = {"w_in": ((D_MODEL, D_IN_PROJ), 1), "conv_w": ((SSD_CONV, CONV_DIM), 1), "s5_glu_w": ((S5_WIDTH, S5_WIDTH), 0),
           "w_branch": ((D_MODEL + S5_WIDTH, D_MODEL), 0), "w_out": ((D_MODEL, D_MODEL), 0),
           "w_mlp_in": ((D_MODEL, D_FF), 1), "w_mlp_out": ((D_FF, D_MODEL), 0)}
SHARDED_ORDER = ["w_in", "conv_w", "s5_glu_w", "w_branch", "w_out", "w_mlp_in", "w_mlp_out"]
SMALL_ORDER = [n for n in WEIGHT_ORDER if n not in SHARDED]


def _shard_shape(name):
    (r, c), ax = SHARDED[name]
    return (r, c // N_DEV) if ax == 1 else (r // N_DEV, c)


def _join_shards(name, stacked):
    (r, c), ax = SHARDED[name]
    if ax == 1:
        return stacked.transpose(1, 0, 2).reshape(r, c)
    return stacked.reshape(r, c)


def _shard_pieces(lo, hi):
    width = D_IN_PROJ // N_DEV
    out = []
    while lo < hi:
        k = lo // width
        end = min(hi, (k + 1) * width)
        out.append((k, lo - k * width, end - k * width))
        lo = end
    return out


IN_SEGMENTS = {"z": (0, 1024), "xbc": (1024, 3072), "dt": (3072, 3088), "u5": (3088, 3600), "gates": (3600, 5648)}
IN_PADDED_ORDER = ("xbc", "gates", "z", "u5", "dt")


def kernel(x, norm_mix_g, w_in, conv_w, conv_b, dt_bias, a_log, d_ssd, ssd_norm_g, s5_a_re, s5_a_im, s5_log_dt, s5_b_re, s5_b_im, s5_c_re, s5_c_im, s5_d, s5_glu_w, s5_glu_b, w_branch, w_out, norm_mlp_g, w_mlp_in, w_mlp_out, norm_final_g, loss_target, m_norm_mix_g, m_w_in, m_conv_w, m_conv_b, m_dt_bias, m_a_log, m_d_ssd, m_ssd_norm_g, m_s5_a_re, m_s5_a_im, m_s5_log_dt, m_s5_b_re, m_s5_b_im, m_s5_c_re, m_s5_c_im, m_s5_d, m_s5_glu_w, m_s5_glu_b, m_w_branch, m_w_out, m_norm_mlp_g, m_w_mlp_in, m_w_mlp_out, m_norm_final_g, v_norm_mix_g, v_w_in, v_conv_w, v_conv_b, v_dt_bias, v_a_log, v_d_ssd, v_ssd_norm_g, v_s5_a_re, v_s5_a_im, v_s5_log_dt, v_s5_b_re, v_s5_b_im, v_s5_c_re, v_s5_c_im, v_s5_d, v_s5_glu_w, v_s5_glu_b, v_w_branch, v_w_out, v_norm_mlp_g, v_w_mlp_in, v_w_mlp_out, v_norm_final_g):
    w = dict(norm_mix_g=norm_mix_g, w_in=w_in, conv_w=conv_w, conv_b=conv_b, dt_bias=dt_bias, a_log=a_log, d_ssd=d_ssd,
             ssd_norm_g=ssd_norm_g, s5_a_re=s5_a_re, s5_a_im=s5_a_im, s5_log_dt=s5_log_dt, s5_b_re=s5_b_re,
             s5_b_im=s5_b_im, s5_c_re=s5_c_re, s5_c_im=s5_c_im, s5_d=s5_d, s5_glu_w=s5_glu_w, s5_glu_b=s5_glu_b,
             w_branch=w_branch, w_out=w_out, norm_mlp_g=norm_mlp_g, w_mlp_in=w_mlp_in, w_mlp_out=w_mlp_out,
             norm_final_g=norm_final_g)
    m = dict(norm_mix_g=m_norm_mix_g, w_in=m_w_in, conv_w=m_conv_w, conv_b=m_conv_b, dt_bias=m_dt_bias, a_log=m_a_log,
             d_ssd=m_d_ssd, ssd_norm_g=m_ssd_norm_g, s5_a_re=m_s5_a_re, s5_a_im=m_s5_a_im, s5_log_dt=m_s5_log_dt,
             s5_b_re=m_s5_b_re, s5_b_im=m_s5_b_im, s5_c_re=m_s5_c_re, s5_c_im=m_s5_c_im, s5_d=m_s5_d,
             s5_glu_w=m_s5_glu_w, s5_glu_b=m_s5_glu_b, w_branch=m_w_branch, w_out=m_w_out, norm_mlp_g=m_norm_mlp_g,
             w_mlp_in=m_w_mlp_in, w_mlp_out=m_w_mlp_out, norm_final_g=m_norm_final_g)
    v = dict(norm_mix_g=v_norm_mix_g, w_in=v_w_in, conv_w=v_conv_w, conv_b=v_conv_b, dt_bias=v_dt_bias, a_log=v_a_log,
             d_ssd=v_d_ssd, ssd_norm_g=v_ssd_norm_g, s5_a_re=v_s5_a_re, s5_a_im=v_s5_a_im, s5_log_dt=v_s5_log_dt,
             s5_b_re=v_s5_b_re, s5_b_im=v_s5_b_im, s5_c_re=v_s5_c_re, s5_c_im=v_s5_c_im, s5_d=v_s5_d,
             s5_glu_w=v_s5_glu_w, s5_glu_b=v_s5_glu_b, w_branch=v_w_branch, w_out=v_w_out, norm_mlp_g=v_norm_mlp_g,
             w_mlp_in=v_w_mlp_in, w_mlp_out=v_w_mlp_out, norm_final_g=v_norm_final_g)

    loss, grad_x, upd = train_step(x[0], loss_target[0], w, m, v)
    return (loss, grad_x.reshape(x.shape), *[upd[n][j] for j in range(4) for n in WEIGHT_ORDER])
```

```python
import functools
import math

import jax
import jax.numpy as jnp
from jax import lax
from jax.experimental import pallas as pl
from jax.experimental.pallas import tpu as pltpu

F32 = jnp.float32
BF16 = jnp.bfloat16
HI = lax.Precision.HIGHEST

N_DEV = 8
D_MODEL = 1024
SSD_HEADS = 16
SSD_HEADDIM = 64
SSD_GROUPS = 4
SSD_HPG = 4
SSD_STATE = 128
SSD_CHUNK = 128
SSD_CONV = 4
CONV_DIM = 2048
S5_WIDTH = 512
S5_GROUP = 16
S5_GROUPS = 32
S5_STATE = 64
S5_N = S5_GROUPS * S5_STATE
D_FF = 4096
D_IN_PROJ = 5648
IN_SPLITS = (2048, 2048, 1024, 512, 128)
D_IN_PAD = sum(IN_SPLITS)
EPS = 1e-6
LANES = 128
SUBLANES = 8
VMEM_LIMIT = 56 * 1024 * 1024

ADAM_LR = 0.001
ADAM_B1 = 0.9
ADAM_B2 = 0.999
ADAM_EPS = 1e-08
ADAM_WD = 0.01
ADAM_STEP = 10

GELU_C = math.sqrt(2.0 / math.pi)
GELU_K = 0.044715


def _params(*sem):
    return pltpu.CompilerParams(dimension_semantics=sem, vmem_limit_bytes=VMEM_LIMIT)


def _full(shape):
    nd = len(shape)
    return pl.BlockSpec(shape, lambda *_: (0,) * nd)


def _resident(shape):
    nd = len(shape)
    return pl.BlockSpec(shape, lambda *_: (0,) * nd, pipeline_mode=pl.Buffered(1))


_MESH = pl.DeviceIdType.MESH
_RELATIONS = [(dx, dy, dc) for dx in (0, 1) for dy in (0, 1) for dc in (0, 1)][1:]
N_REL = len(_RELATIONS)


def _place():
    x, y, c = lax.axis_index("x"), lax.axis_index("y"), lax.axis_index("c")
    return x, y, c, 4 * x + 2 * y + c


def gathers(arrays):
    return [("gather", a) for a in arrays]


def exchanges(arrays):
    return [("exchange", a) for a in arrays]


def _comm_out_shapes(comm):
    return [jax.ShapeDtypeStruct(((N_DEV,) if kind == "gather" else ()) + a.shape, a.dtype) for kind, a in comm]


def _comm_copies(kinds, src_refs, dst_refs, send_sems, recv_sems, local_sems):
    x, y, c, idx = _place()
    local, remote = [], []
    for a, (kind, src, dst) in enumerate(zip(kinds, src_refs, dst_refs)):
        local.append(pltpu.make_async_copy(src if kind == "gather" else src.at[idx], dst.at[idx], local_sems.at[a]))
        for k, (dx, dy, dc) in enumerate(_RELATIONS):
            px, py, pc = x ^ dx, y ^ dy, c ^ dc
            remote.append(pltpu.make_async_remote_copy(
                src_ref=src if kind == "gather" else src.at[4 * px + 2 * py + pc], dst_ref=dst.at[idx],
                send_sem=send_sems.at[N_REL * a + k], recv_sem=recv_sems.at[N_REL * a + k],
                device_id=(px, py, pc), device_id_type=_MESH))
    return local, remote


def _tiled_call(body, *, name, grid, in_specs, out_specs, out_shape, operands, scratch_shapes=(), comm=None):
    params = pltpu.CompilerParams(dimension_semantics=("arbitrary",), vmem_limit_bytes=VMEM_LIMIT,
                                  has_side_effects=bool(comm))
    if not comm:
        outs = pl.pallas_call(body, name=name, grid=grid, in_specs=in_specs, out_specs=out_specs, out_shape=out_shape,
                              scratch_shapes=list(scratch_shapes), compiler_params=params)(*operands)
        return outs, []
    kind = [k for k, _ in comm]
    arrays = [a for _, a in comm]
    na, n_in, n_out, n_scr = len(arrays), len(in_specs), len(out_specs), len(scratch_shapes)
    last = grid[0] - 1

    def hosted(*refs):
        ins, refs = refs[:n_in], refs[n_in:]
        cin, refs = refs[:na], refs[na:]
        outs, refs = refs[:n_out], refs[n_out:]
        cout, refs = refs[:na], refs[na:]
        scr, sems = refs[:n_scr], refs[n_scr:]
        i = pl.program_id(0)

        @pl.when(i == 0)
        def _():
            local, remote = _comm_copies(kind, cin, cout, *sems)
            for cp in local + remote:
                cp.start()

        body(*ins, *outs, *scr)

        @pl.when(i == last)
        def _():
            local, remote = _comm_copies(kind, cin, cout, *sems)
            for cp in remote:
                cp.wait_recv()
            for cp in remote:
                cp.wait_send()
            for cp in local:
                cp.wait()

    any_spec = pl.BlockSpec(memory_space=pl.ANY)
    res = pl.pallas_call(
        hosted, name=name, grid=grid,
        in_specs=list(in_specs) + [any_spec] * na, out_specs=list(out_specs) + [any_spec] * na,
        out_shape=list(out_shape) + _comm_out_shapes(comm),
        scratch_shapes=list(scratch_shapes) + [pltpu.SemaphoreType.DMA((N_REL * na,)), pltpu.SemaphoreType.DMA((N_REL * na,)),
                                               pltpu.SemaphoreType.DMA((na,))],
        compiler_params=params)(*operands, *arrays)
    return res[:n_out], res[n_out:]


def _dot(a, b):
    return jnp.dot(a, b, preferred_element_type=F32)


def _dot_nt(a, b):
    return lax.dot_general(a, b, (((1,), (1,)), ((), ())), preferred_element_type=F32)


def _dot_tn(a, b):
    return lax.dot_general(a, b, (((0,), (0,)), ((), ())), preferred_element_type=F32)


def _dot_hi(a, b):
    return jnp.dot(a, b, preferred_element_type=F32, precision=HI)


def _dot_mid(a, b):
    return jnp.dot(a, b, preferred_element_type=F32, precision=lax.Precision.HIGH)


def _dot_nt_hi(a, b):
    return lax.dot_general(a, b, (((1,), (1,)), ((), ())), preferred_element_type=F32, precision=HI)


def _dot_tn_hi(a, b):
    return lax.dot_general(a, b, (((0,), (0,)), ((), ())), preferred_element_type=F32, precision=HI)


def _sigmoid(x):
    return 1.0 / (1.0 + jnp.exp(-x))


def _softplus(x):
    return jnp.maximum(x, 0.0) + jnp.log(1.0 + jnp.exp(-jnp.abs(x)))


def _gelu(x):
    return 0.5 * x * (1.0 + jnp.tanh(GELU_C * (x + GELU_K * x * x * x)))


def _gelu_grad(x):
    th = jnp.tanh(GELU_C * (x + GELU_K * x * x * x))
    return 0.5 * (1.0 + th) + 0.5 * x * (1.0 - th * th) * GELU_C * (1.0 + 3.0 * GELU_K * x * x)


def rms_matmul(x, g, w, splits, *, tm, name, comm=None, out_dtypes=None):
    t, d = x.shape
    stacked = w.ndim == 3
    n = w.shape[0] * w.shape[2] if stacked else w.shape[1]
    assert sum(splits) == n and (len(splits) == 1 or not stacked)

    def body(x_ref, g_ref, w_ref, h_ref, *o_refs):
        xv = x_ref[...]
        r = lax.rsqrt(jnp.mean(xv * xv, axis=-1, keepdims=True) + EPS)
        h = (xv * r * g_ref[...]).astype(BF16)
        h_ref[...] = h
        if stacked:
            wk = w.shape[2]
            for k in range(w.shape[0]):
                o_refs[0][:, k * wk:(k + 1) * wk] = _dot(h, w_ref[k]).astype(o_refs[0].dtype)
            return
        off = 0
        for o_ref, width in zip(o_refs, splits):
            o_ref[...] = _dot(h, w_ref[:, off:off + width]).astype(o_ref.dtype)
            off += width

    tile = lambda c: pl.BlockSpec((tm, c), lambda i: (i, 0))
    dtypes = out_dtypes or (F32,) * len(splits)
    return _tiled_call(
        body, name=name, grid=(t // tm,),
        in_specs=[tile(d), _full((1, d)), _resident(w.shape)],
        out_specs=[tile(d)] + [tile(c) for c in splits],
        out_shape=[jax.ShapeDtypeStruct((t, d), BF16)] + [jax.ShapeDtypeStruct((t, c), dt) for c, dt in zip(splits, dtypes)],
        operands=(x, g.reshape(1, d), w), comm=comm)


MATMUL_TN_ACC_BYTES = 16 * 1024 * 1024


def _pick(n, cap):
    best = LANES
    for c in range(LANES, cap + 1, LANES):
        if n % c == 0:
            best = c
    return best


def matmul_tn(a, b, *, name, out_dtype=F32, col_shards=None, tk=1024, a_relu2=False):
    t, m = a.shape
    n = b.shape[1]
    tm = _pick(m, 1024)
    whole_n = tm * n * 4 <= MATMUL_TN_ACC_BYTES
    tn = n if whole_n else _pick(n, 1280)
    assert whole_n or not col_shards
    tk = min(tk if tn <= 2048 else tk // 2, t)
    nk = t // tk

    def body(a_ref, b_ref, o_ref, acc):
        k = pl.program_id(2)

        @pl.when(k == 0)
        def _():
            acc[...] = jnp.zeros_like(acc)

        av = a_ref[...]
        if a_relu2:
            pos = jnp.maximum(av.astype(F32), 0.0)
            av = (pos * pos).astype(BF16)
        acc[...] += _dot_tn(av, b_ref[...])

        @pl.when(k == nk - 1)
        def _():
            if col_shards:
                w = n // col_shards
                for s in range(col_shards):
                    o_ref[s] = acc[:, s * w:(s + 1) * w].astype(out_dtype)
            else:
                o_ref[...] = acc[...].astype(out_dtype)

    if col_shards:
        out_spec = pl.BlockSpec((col_shards, tm, n // col_shards), lambda i, j, k: (0, i, 0))
        out_shape = jax.ShapeDtypeStruct((col_shards, m, n // col_shards), out_dtype)
    else:
        out_spec = pl.BlockSpec((tm, tn), lambda i, j, k: (i, j))
        out_shape = jax.ShapeDtypeStruct((m, n), out_dtype)
    return pl.pallas_call(
        body, name=name, grid=(m // tm, n // tn, nk),
        in_specs=[pl.BlockSpec((tk, tm), lambda i, j, k: (k, i)), pl.BlockSpec((tk, tn), lambda i, j, k: (k, j))],
        out_specs=out_spec, out_shape=out_shape,
        scratch_shapes=[pltpu.VMEM((tm, tn), F32)],
        compiler_params=_params("parallel", "parallel", "arbitrary"),
    )(a, b)


def matmul_nt_rms_bwd(dys, w, x, g, dres, *, tm, name, comm=None):
    t = x.shape[0]
    stacked = w.ndim == 3
    d = w.shape[1] if stacked else w.shape[0]
    widths = [dy.shape[1] for dy in dys]
    n_dy = len(dys)

    def body(*refs):
        dy_refs = refs[:n_dy]
        w_ref, x_ref, g_ref, dres_ref, dx_ref, dxb_ref, dg_ref = refs[n_dy:]
        i = pl.program_id(0)
        if stacked:
            wk = w.shape[2]
            dh = _dot_nt(dy_refs[0][:, 0:wk], w_ref[0])
            for k in range(1, w.shape[0]):
                dh = dh + _dot_nt(dy_refs[0][:, k * wk:(k + 1) * wk], w_ref[k])
        else:
            dh, off = None, 0
            for dy_ref, width in zip(dy_refs, widths):
                part = _dot_nt(dy_ref[...], w_ref[:, off:off + width])
                dh = part if dh is None else dh + part
                off += width
        xv = x_ref[...]
        r = lax.rsqrt(jnp.mean(xv * xv, axis=-1, keepdims=True) + EPS)
        xn = xv * r
        dxn = dh * g_ref[...]
        dx = dres_ref[...] + r * (dxn - xn * jnp.mean(dxn * xn, axis=-1, keepdims=True))
        dx_ref[...] = dx
        dxb_ref[...] = dx.astype(BF16)

        @pl.when(i == 0)
        def _():
            dg_ref[...] = jnp.zeros_like(dg_ref)

        dg_ref[...] += jnp.sum(dh * xn, axis=0, keepdims=True)

    tile = lambda c: pl.BlockSpec((tm, c), lambda i: (i, 0))
    return _tiled_call(
        body, name=name, grid=(t // tm,),
        in_specs=[tile(c) for c in widths] + [_resident(w.shape), tile(d), _full((1, d)), tile(d)],
        out_specs=[tile(d), tile(d), _full((1, d))],
        out_shape=[jax.ShapeDtypeStruct((t, d), F32), jax.ShapeDtypeStruct((t, d), BF16),
                   jax.ShapeDtypeStruct((1, d), F32)],
        operands=(*dys, w, x, g.reshape(1, d), dres), comm=comm)


def _conv_windows(ext_ref, tm):
    return [ext_ref[SUBLANES - 3 + k:SUBLANES - 3 + k + tm, :] for k in range(SSD_CONV)]


def _conv_taps(windows, w_ref):
    acc = windows[3] * w_ref[3:4, :]
    for k in range(3):
        acc = acc + windows[k] * w_ref[k:k + 1, :]
    return acc


def _ssd_chunk_terms(dtr, bias_row, alog_row):
    q = SSD_CHUNK
    row = lax.broadcasted_iota(jnp.int32, (q, q), 0)
    col = lax.broadcasted_iota(jnp.int32, (q, q), 1)
    ltri = (col <= row).astype(F32)
    dt = _softplus(dtr + bias_row)
    a_row = -jnp.exp(alog_row)
    la = dt * a_row
    cum = _dot_hi(ltri, la)
    cum_t = _dot_tn_hi(la, (row <= col).astype(F32))
    return dt, a_row, cum, cum_t, ltri, row, col


def _head_maps():
    di = SSD_HEADS * SSD_HEADDIM
    expand = (lax.broadcasted_iota(jnp.int32, (LANES, di), 1) // SSD_HEADDIM
              == lax.broadcasted_iota(jnp.int32, (LANES, di), 0)).astype(F32)
    collapse = (lax.broadcasted_iota(jnp.int32, (di, LANES), 0) // SSD_HEADDIM
                == lax.broadcasted_iota(jnp.int32, (di, LANES), 1)).astype(F32)
    return expand, collapse


def _ssd_decay(cum, cum_t, h, causal):
    seg = cum[:, h:h + 1] - cum_t[h:h + 1, :]
    return jnp.where(causal, jnp.exp(jnp.where(causal, seg, 0.0)), 0.0)


def _pair_block_diag(x2):
    low = lax.broadcasted_iota(jnp.int32, x2.shape, 1) < SSD_HEADDIM
    zero = jnp.zeros_like(x2)
    return jnp.concatenate([jnp.where(low, x2, zero), jnp.where(low, zero, x2)], axis=0)


def ssd_fwd(xbc_raw, conv_w8, conv_b, dtr, z, bias_row, alog_row, d_row, norm_g, *, name, comm=None):
    t = xbc_raw.shape[0]
    q, p, n = SSD_CHUNK, SSD_HEADDIM, SSD_STATE
    nc = t // q
    di = SSD_HEADS * p
    gw = di // SSD_GROUPS

    def body(raw_ref, cw_ref, cb_ref, dtr_ref, z_ref, bias_ref, alog_ref, d_ref, g_ref,
             ya_ref, yssd_ref, st_ref, state, ext, xbc_s):
        ci = pl.program_id(0)

        @pl.when(ci == 0)
        def _():
            state[...] = jnp.zeros_like(state)
            ext[0:SUBLANES, :] = jnp.zeros((SUBLANES, CONV_DIM), F32)

        ext[SUBLANES:SUBLANES + q, :] = raw_ref[...].astype(F32)
        xc = _conv_taps(_conv_windows(ext, q), cw_ref) + cb_ref[...]
        ext[0:SUBLANES, :] = ext[q:q + SUBLANES, :]
        xbc_s[...] = xc * _sigmoid(xc)
        st_ref[0] = state[...]
        dt, _, cum, cum_t, _, row, col = _ssd_chunk_terms(dtr_ref[...], bias_ref[...], alog_ref[...])
        causal = row >= col
        expand, _ = _head_maps()
        dt_full = _dot_mid(dt, expand)
        cum_full = _dot_mid(cum, expand)
        last_full = cum_full[q - 1:q, :]
        xs = xbc_s[:, 0:di]
        xdt = xs * dt_full
        xd = (xdt * jnp.exp(last_full - cum_full)).astype(BF16)
        xdt_b = xdt.astype(BF16)
        e_full = jnp.exp(cum_full)
        eend_full = jnp.exp(last_full)
        for g in range(SSD_GROUPS):
            slab = slice(g * gw, (g + 1) * gw)
            bg = xbc_s[:, di + g * n:di + (g + 1) * n].astype(BF16)
            cg = xbc_s[:, di + (SSD_GROUPS + g) * n:di + (SSD_GROUPS + g + 1) * n].astype(BF16)
            cb = _dot_nt(cg, bg)
            st_g = state[:, slab]
            pairs = []
            for pr in range(SSD_HPG // 2):
                h0 = g * SSD_HPG + 2 * pr
                m0 = (cb * _ssd_decay(cum, cum_t, h0, causal)).astype(BF16)
                m1 = (cb * _ssd_decay(cum, cum_t, h0 + 1, causal)).astype(BF16)
                pairs.append(_dot(jnp.concatenate([m0, m1], axis=1), _pair_block_diag(xdt_b[:, h0 * p:(h0 + 2) * p])))
            y = jnp.concatenate(pairs, axis=1) + e_full[:, slab] * _dot(cg, st_g.astype(BF16))
            yssd_ref[:, slab] = y + d_ref[:, slab] * xs[:, slab]
            state[:, slab] = eend_full[:, slab] * st_g + _dot_tn(bg, xd[:, slab])
        zv = z_ref[...].astype(F32)
        ya1 = yssd_ref[...] * (zv * _sigmoid(zv))
        for g in range(SSD_GROUPS):
            gsl = slice(g * gw, (g + 1) * gw)
            sl = ya1[:, gsl]
            rg = lax.rsqrt(jnp.mean(sl * sl, axis=-1, keepdims=True) + EPS)
            ya_ref[:, gsl] = (sl * rg * g_ref[:, gsl]).astype(BF16)

    blk = lambda w, j: pl.BlockSpec((q, w), lambda c: (c, j))
    return _tiled_call(
        body, name=name, grid=(nc,),
        in_specs=[blk(CONV_DIM, 0), _full((SUBLANES, CONV_DIM)), _full((1, CONV_DIM)), blk(LANES, 0), blk(di, 0),
                  _full((1, LANES)), _full((1, LANES)), _full((1, di)), _full((1, di))],
        out_specs=[blk(di, 0), blk(di, 0), pl.BlockSpec((1, n, di), lambda c: (c, 0, 0))],
        out_shape=[jax.ShapeDtypeStruct((t, di), BF16), jax.ShapeDtypeStruct((t, di), F32),
                   jax.ShapeDtypeStruct((nc, n, di), F32)],
        scratch_shapes=[pltpu.VMEM((n, di), F32), pltpu.VMEM((SUBLANES + q, CONV_DIM), F32), pltpu.VMEM((q, CONV_DIM), F32)],
        operands=(xbc_raw, conv_w8, conv_b.reshape(1, CONV_DIM), dtr, z, bias_row, alog_row, d_row,
                  norm_g.reshape(1, di)), comm=comm)


def ssd_bwd(dya, yssd, z, xbc_raw, conv_w8, conv_b, dtr, states, bias_row, alog_row, d_row, norm_g, *, name, comm=None):
    t = xbc_raw.shape[0]
    q, p, n = SSD_CHUNK, SSD_HEADDIM, SSD_STATE
    nc = t // q
    di = SSD_HEADS * p
    gw = di // SSD_GROUPS
    per = q // (2 * SUBLANES)

    def body(dya_ref, yssd_ref, z_ref, raw_ref, prev_ref, cw_ref, cb_ref, dtr_ref, st_ref, bias_ref, alog_ref, d_ref,
             g_ref, dz_ref, draw_ref, ddtr_ref, dng_ref, dd_ref, dalog_ref, dbias_ref, dcb_ref, dcw_ref,
             dstate, dyssd, xbc_s, dxbc_ref, ext, aft):
        ci = pl.program_id(0)

        @pl.when(ci == 0)
        def _():
            dstate[...] = jnp.zeros_like(dstate)
            aft[q:q + SUBLANES, :] = jnp.zeros((SUBLANES, CONV_DIM), F32)
            for r in (dng_ref, dd_ref, dalog_ref, dbias_ref, dcb_ref, dcw_ref):
                r[...] = jnp.zeros_like(r)

        ext[0:SUBLANES, :] = jnp.where(ci == nc - 1, 0.0, prev_ref[SUBLANES:2 * SUBLANES, :].astype(F32))
        ext[SUBLANES:SUBLANES + q, :] = raw_ref[...].astype(F32)
        windows = _conv_windows(ext, q)
        xc = _conv_taps(windows, cw_ref) + cb_ref[...]
        sig_c = _sigmoid(xc)
        xbc_s[...] = xc * sig_c

        zv = z_ref[...].astype(F32)
        sz = _sigmoid(zv)
        silu_z = zv * sz
        yv = yssd_ref[...]
        ya1 = yv * silu_z
        dyav = dya_ref[...]
        for g in range(SSD_GROUPS):
            gsl = slice(g * gw, (g + 1) * gw)
            sl = ya1[:, gsl]
            rg = lax.rsqrt(jnp.mean(sl * sl, axis=-1, keepdims=True) + EPS)
            ya2 = sl * rg
            dy_g = dyav[:, gsl]
            dng_ref[:, gsl] += jnp.sum(dy_g * ya2, axis=0, keepdims=True)
            dya2 = dy_g * g_ref[:, gsl]
            dya1 = rg * (dya2 - ya2 * jnp.mean(dya2 * ya2, axis=-1, keepdims=True))
            dyssd[:, gsl] = dya1 * silu_z[:, gsl]
            dz_ref[:, gsl] = (dya1 * yv[:, gsl] * (sz[:, gsl] * (1.0 + zv[:, gsl] * (1.0 - sz[:, gsl])))).astype(BF16)

        dtr_v = dtr_ref[...]
        dt, a_row, cum, cum_t, ltri, row, col = _ssd_chunk_terms(dtr_v, bias_ref[...], alog_ref[...])
        causal = row >= col
        expand, collapse = _head_maps()
        lane = lax.broadcasted_iota(jnp.int32, (1, LANES), 1)
        sub = lax.broadcasted_iota(jnp.int32, (LANES, 1), 0)
        is_last = lax.broadcasted_iota(jnp.int32, (q, 1), 0) == q - 1
        low = lax.broadcasted_iota(jnp.int32, (q, 2 * p), 1) < p
        dt_full = _dot_mid(dt, expand)
        cum_full = _dot_mid(cum, expand)
        last_full = cum_full[q - 1:q, :]
        e_full = jnp.exp(cum_full)
        eend_full = jnp.exp(last_full)
        dend_full = jnp.exp(last_full - cum_full)
        xs = xbc_s[:, 0:di]
        xdt = xs * dt_full
        xdt_b = xdt.astype(BF16)
        xd_b = (xdt * dend_full).astype(BF16)
        dy_all = dyssd[...]
        dy_b = dy_all.astype(BF16)
        dg_b = (dy_all * e_full).astype(BF16)
        dcum_mat = jnp.zeros((q, LANES), F32)
        rmat = jnp.zeros((LANES, q), F32)
        yoff_prod, bds_list, dx_list, dlast_parts = [], [], [], []
        for g in range(SSD_GROUPS):
            slab = slice(g * gw, (g + 1) * gw)
            bg = xbc_s[:, di + g * n:di + (g + 1) * n].astype(BF16)
            cg = xbc_s[:, di + (SSD_GROUPS + g) * n:di + (SSD_GROUPS + g + 1) * n].astype(BF16)
            cb = _dot_nt(cg, bg)
            st_g = st_ref[0, :, slab]
            st_b = st_g.astype(BF16)
            dsn = dstate[:, slab]
            dsn_b = dsn.astype(BF16)
            yoff_prod.append(dy_all[:, slab] * e_full[:, slab] * _dot(cg, st_b))
            dcg = _dot_nt(dg_b[:, slab], st_b)
            dstate[:, slab] = _dot_tn(cg, dg_b[:, slab]) + eend_full[:, slab] * dsn
            bds_list.append(_dot(bg, dsn_b))
            dbg = _dot_nt(xd_b[:, slab], dsn_b)
            dlast_parts.append(jnp.sum(dsn * st_g, axis=0, keepdims=True))
            dcb = jnp.zeros((q, q), F32)
            dx_pairs = []
            for pr in range(SSD_HPG // 2):
                h0 = g * SSD_HPG + 2 * pr
                ps = slice(h0 * p, (h0 + 2) * p)
                l0 = _ssd_decay(cum, cum_t, h0, causal)
                l1 = _ssd_decay(cum, cum_t, h0 + 1, causal)
                m0, m1 = cb * l0, cb * l1
                dyp = dy_b[:, ps]
                zero = jnp.zeros_like(dyp)
                dy0, dy1 = jnp.where(low, dyp, zero), jnp.where(low, zero, dyp)
                dm0 = _dot_nt(dy0, xdt_b[:, ps])
                dm1 = _dot_nt(dy1, xdt_b[:, ps])
                w0, w1 = dm0 * m0, dm1 * m1
                dcum_mat = (dcum_mat + jnp.sum(w0, axis=1, keepdims=True) * (lane == h0).astype(F32)
                            + jnp.sum(w1, axis=1, keepdims=True) * (lane == h0 + 1).astype(F32))
                rmat = (rmat + jnp.where(sub == h0, jnp.sum(w0, axis=0, keepdims=True), 0.0)
                        + jnp.where(sub == h0 + 1, jnp.sum(w1, axis=0, keepdims=True), 0.0))
                dcb = dcb + dm0 * l0 + dm1 * l1
                dx_pairs.append(_dot_tn(jnp.concatenate([m0.astype(BF16), m1.astype(BF16)], axis=0),
                                        jnp.concatenate([dy0, dy1], axis=0)))
            dx_list.append(jnp.concatenate(dx_pairs, axis=1))
            dcb_b = dcb.astype(BF16)
            dxbc_ref[:, di + g * n:di + (g + 1) * n] = dbg + _dot_tn(dcb_b, cg)
            dxbc_ref[:, di + SSD_GROUPS * n + g * n:di + SSD_GROUPS * n + (g + 1) * n] = dcg + _dot(dcb_b, bg)
        bds_all = jnp.concatenate(bds_list, axis=1)
        dx_all = jnp.concatenate(dx_list, axis=1) + dend_full * bds_all
        last_row = cum[q - 1:q, :]
        qv = _dot_mid(xdt * bds_all, collapse) * jnp.exp(last_row - cum)
        dcum_mat = dcum_mat + _dot_mid(jnp.concatenate(yoff_prod, axis=1), collapse) - qv
        dlast_full = jnp.broadcast_to(jnp.concatenate(dlast_parts, axis=1), (SUBLANES, di))
        dlast_row = _dot_mid(dlast_full, collapse)[0:1, :] * jnp.exp(last_row) + jnp.sum(qv, axis=0, keepdims=True)
        dcum_mat = dcum_mat + jnp.where(is_last, dlast_row, 0.0)
        dla = _dot_tn_hi(ltri, dcum_mat) - _dot_nt_hi((row <= col).astype(F32), rmat)
        ddt_mat = _dot_mid(dx_all * xs, collapse) + dla * a_row
        dxbc_ref[:, 0:di] = d_ref[...] * dy_all + dx_all * dt_full
        ddtr = ddt_mat * _sigmoid(dtr_v + bias_ref[...])
        ddtr_ref[...] = ddtr.astype(BF16)
        dd_full = jnp.broadcast_to(jnp.sum(dy_all * xs, axis=0, keepdims=True), (SUBLANES, di))
        dd_ref[...] += _dot_mid(dd_full, collapse)[0:1, :]
        dalog_ref[...] += jnp.sum(dla * dt, axis=0, keepdims=True) * a_row
        dbias_ref[...] += jnp.sum(ddtr, axis=0, keepdims=True)

        dxc = dxbc_ref[...] * (sig_c * (1.0 + xc * (1.0 - sig_c)))
        dcb_ref[...] += jnp.sum(dxc, axis=0, keepdims=True)
        taps = [jnp.sum(dxc * windows[k], axis=0, keepdims=True) for k in range(SSD_CONV)]
        dcw_ref[...] += jnp.concatenate(taps + [jnp.zeros((SUBLANES - SSD_CONV, CONV_DIM), F32)], axis=0)
        aft[0:q, :] = dxc
        draw = dxc * cw_ref[3:4, :]
        for j in range(1, SSD_CONV):
            draw = draw + aft[j:j + q, :] * cw_ref[3 - j:4 - j, :]
        draw_ref[...] = draw.astype(BF16)
        aft[q:q + SUBLANES, :] = dxc[0:SUBLANES, :]

    blk = lambda w, j: pl.BlockSpec((q, w), lambda c: (nc - 1 - c, j))
    row_out = lambda w: jax.ShapeDtypeStruct((1, w), F32)
    return _tiled_call(
        body, name=name, grid=(nc,),
        in_specs=[blk(di, 0), blk(di, 0), blk(di, 0), blk(CONV_DIM, 0),
                  pl.BlockSpec((2 * SUBLANES, CONV_DIM), lambda c: (jnp.maximum((nc - 1 - c) * per - 1, 0), 0)),
                  _full((SUBLANES, CONV_DIM)), _full((1, CONV_DIM)), blk(LANES, 0),
                  pl.BlockSpec((1, n, di), lambda c: (nc - 1 - c, 0, 0)),
                  _full((1, LANES)), _full((1, LANES)), _full((1, di)), _full((1, di))],
        out_specs=[blk(di, 0), blk(CONV_DIM, 0), blk(LANES, 0),
                   _full((1, di)), _full((1, LANES)), _full((1, LANES)), _full((1, LANES)),
                   _full((1, CONV_DIM)), _full((SUBLANES, CONV_DIM))],
        out_shape=[jax.ShapeDtypeStruct((t, di), BF16), jax.ShapeDtypeStruct((t, CONV_DIM), BF16),
                   jax.ShapeDtypeStruct((t, LANES), BF16), row_out(di), row_out(LANES), row_out(LANES), row_out(LANES),
                   row_out(CONV_DIM), jax.ShapeDtypeStruct((SUBLANES, CONV_DIM), F32)],
        scratch_shapes=[pltpu.VMEM((n, di), F32), pltpu.VMEM((q, di), F32), pltpu.VMEM((q, CONV_DIM), F32),
                        pltpu.VMEM((q, CONV_DIM), F32), pltpu.VMEM((SUBLANES + q, CONV_DIM), F32),
                        pltpu.VMEM((q + SUBLANES, CONV_DIM), F32)],
        operands=(dya, yssd, z, xbc_raw, xbc_raw, conv_w8, conv_b.reshape(1, CONV_DIM), dtr, states, bias_row,
                  alog_row, d_row, norm_g.reshape(1, di)),
        comm=comm)


S5_BLOCKS = 4
S5_BC = S5_WIDTH // S5_BLOCKS
S5_BS = S5_N // S5_BLOCKS


def _s5_tables(ar, ai, reverse):
    pows = [(ar, ai)]
    for _ in range(SUBLANES - 1):
        pr, pi = pows[-1]
        pows.append((pr * ar - pi * ai, pr * ai + pi * ar))
    row = lax.broadcasted_iota(jnp.int32, (SUBLANES, ar.shape[1]), 0)
    out = []
    for k in (1, 2, 4):
        pr, pi = pows[k - 1]
        mask = (row < SUBLANES - k) if reverse else (row >= k)
        out += [jnp.where(mask, pr, 0.0), jnp.where(mask, pi, 0.0)]
    order = list(range(SUBLANES))
    if reverse:
        order = order[::-1]
    out.append(jnp.concatenate([pows[e][0] for e in order], axis=0))
    out.append(jnp.concatenate([pows[e][1] for e in order], axis=0))
    return out


def s5_interleave(a, tm):
    t, c = a.shape
    return a.reshape(t // tm, SUBLANES, tm // SUBLANES, c).transpose(0, 2, 1, 3).reshape(t, c)


def s5_deinterleave(a, tm):
    t, c = a.shape
    return a.reshape(t // tm, tm // SUBLANES, SUBLANES, c).transpose(0, 2, 1, 3).reshape(t, c)


def _s5_scan_setup(ab_ref, base, tab, seg, reverse):
    ar = ab_ref[0:1, :]
    ai = -ab_ref[1:2, :] if reverse else ab_ref[1:2, :]
    base[0:1, :] = ar
    base[1:2, :] = ai
    assert seg & (seg - 1) == 0
    pr, pi = ar, ai
    for _ in range(seg.bit_length() - 1):
        pr, pi = pr * pr - pi * pi, 2.0 * pr * pi
    for k, tv in enumerate(_s5_tables(pr, pi, reverse)):
        tab[k] = tv


def _s5_scan(s_scr, base, tab, carry, tm, reverse):
    seg = tm // SUBLANES
    width = S5_BS
    first = lax.broadcasted_iota(jnp.int32, (SUBLANES, width), 0) == (SUBLANES - 1 if reverse else 0)

    def rows(i):
        step = (seg - 1 - i) if reverse else i
        return pl.ds(pl.multiple_of(step * SUBLANES, SUBLANES), SUBLANES)

    for lc in range(S5_N // width):
        lr = slice(lc * width, (lc + 1) * width)
        li = slice(S5_N + lc * width, S5_N + (lc + 1) * width)
        ar = jnp.broadcast_to(base[0:1, lr], (SUBLANES, width))
        ai = jnp.broadcast_to(base[1:2, lr], (SUBLANES, width))

        def advance(i, x, lr=lr, li=li, ar=ar, ai=ai):
            xr, xi = x
            return ar * xr - ai * xi + s_scr[rows(i), lr], ar * xi + ai * xr + s_scr[rows(i), li]

        zero = jnp.zeros((SUBLANES, width), F32)
        fr, fi = lax.fori_loop(0, seg, advance, (zero, zero))

        tb = [tab[k, :, lr] for k in range(8)]
        for lvl, k in enumerate((1, 2, 4)):
            sh = (SUBLANES - k) if reverse else k
            pr, pi = tb[2 * lvl], tb[2 * lvl + 1]
            rr, ri = pltpu.roll(fr, sh, 0), pltpu.roll(fi, sh, 0)
            fr, fi = fr + pr * rr - pi * ri, fi + pr * ri + pi * rr
        cr, ci = carry[0:1, lr], carry[0:1, li]
        fr, fi = fr + tb[6] * cr - tb[7] * ci, fi + tb[6] * ci + tb[7] * cr
        last = 0 if reverse else SUBLANES - 1
        carry[0:1, lr] = fr[last:last + 1, :]
        carry[0:1, li] = fi[last:last + 1, :]
        sh = (SUBLANES - 1) if reverse else 1
        entering = (jnp.where(first, cr, pltpu.roll(fr, sh, 0)), jnp.where(first, ci, pltpu.roll(fi, sh, 0)))

        def rerun(i, x, lr=lr, li=li, advance=advance):
            xr, xi = advance(i, x)
            s_scr[rows(i), lr] = xr
            s_scr[rows(i), li] = xi
            return xr, xi

        lax.fori_loop(0, seg, rerun, entering)


def _s5_scratch(tm):
    return [pltpu.VMEM((tm, 2 * S5_N), F32), pltpu.VMEM((SUBLANES, S5_N), F32),
            pltpu.VMEM((8, SUBLANES, S5_N), F32), pltpu.VMEM((SUBLANES, 2 * S5_N), F32)]


def _s5_put(scr, j, val):
    scr[:, j * S5_BS:(j + 1) * S5_BS] = val[:, :S5_BS]
    scr[:, S5_N + j * S5_BS:S5_N + (j + 1) * S5_BS] = val[:, S5_BS:]


def _s5_get(scr, j):
    return jnp.concatenate([scr[:, j * S5_BS:(j + 1) * S5_BS], scr[:, S5_N + j * S5_BS:S5_N + (j + 1) * S5_BS]], axis=1)


def s5_fwd(u5, bbc, ccc, ab8, d_row, wglu, bglu_row, *, tm, name, comm=None):
    t, w = u5.shape

    def body(u_ref, bb_ref, cc_ref, ab_ref, d_ref, wg_ref, bg_ref, s_ref, ys_ref, yb_ref, s_scr, base, tab, carry):
        i = pl.program_id(0)

        @pl.when(i == 0)
        def _():
            carry[...] = jnp.zeros_like(carry)
            _s5_scan_setup(ab_ref, base, tab, tm // SUBLANES, False)

        ub = u_ref[...]
        u = ub.astype(F32)
        for j in range(S5_BLOCKS):
            uj = ub[:, j * S5_BC:(j + 1) * S5_BC]
            _s5_put(s_scr, j, _dot(uj, bb_ref[j * S5_BC:(j + 1) * S5_BC, :]))
        _s5_scan(s_scr, base, tab, carry, tm, False)
        s_ref[...] = s_scr[...]
        ys = []
        for j in range(S5_BLOCKS):
            ys.append(_dot(_s5_get(s_scr, j).astype(BF16), cc_ref[:, j * S5_BC:(j + 1) * S5_BC]))
        y = jnp.concatenate(ys, axis=1) + d_ref[...] * u
        ys_ref[...] = y
        yb1 = _gelu(y)
        tt = _dot(yb1.astype(BF16), wg_ref[...]) + bg_ref[...]
        yb_ref[...] = (yb1 * _sigmoid(tt)).astype(BF16)

    tile = lambda c: pl.BlockSpec((tm, c), lambda i: (i, 0))
    return _tiled_call(
        body, name=name, grid=(t // tm,),
        in_specs=[tile(w), _full((w, 2 * S5_BS)), _full((2 * S5_BS, w)), _full((SUBLANES, S5_N)), _full((1, w)),
                  _full((w, w)), _full((1, w))],
        out_specs=[tile(2 * S5_N), tile(w), tile(w)],
        out_shape=[jax.ShapeDtypeStruct((t, 2 * S5_N), F32), jax.ShapeDtypeStruct((t, w), F32),
                   jax.ShapeDtypeStruct((t, w), BF16)],
        scratch_shapes=_s5_scratch(tm),
        operands=(u5, bbc, ccc, ab8, d_row, wglu, bglu_row), comm=comm)


def s5_bwd(dyb, ys5, u5, s, bbc, ccc, ab8, d_row, wglu, bglu_row, *, tm, name, comm=None):
    t, w = u5.shape
    nt = t // tm
    per = tm // SUBLANES

    def body(dyb_ref, ys_ref, u_ref, s_ref, sprev_ref, bb_ref, cc_ref, ab_ref, d_ref, wg_ref, bg_ref,
             du_ref, dbb_ref, dcc_ref, dab_ref, dd_ref, dwg_ref, dbg_ref, g_scr, base, tab, carry):
        i = pl.program_id(0)

        @pl.when(i == 0)
        def _():
            carry[...] = jnp.zeros_like(carry)
            _s5_scan_setup(ab_ref, base, tab, tm // SUBLANES, True)
            for r in (dbb_ref, dcc_ref, dab_ref, dd_ref, dwg_ref, dbg_ref):
                r[...] = jnp.zeros_like(r)

        y = ys_ref[...]
        ub = u_ref[...]
        u = ub.astype(F32)
        yb1 = _gelu(y)
        yb1b = yb1.astype(BF16)
        sg = _sigmoid(_dot(yb1b, wg_ref[...]) + bg_ref[...])
        dybv = dyb_ref[...]
        dtt = dybv * yb1 * sg * (1.0 - sg)
        dttb = dtt.astype(BF16)
        dyb1 = dybv * sg + _dot_nt(dttb, wg_ref[...])
        dwg_ref[...] += _dot_tn(yb1b, dttb)
        dbg_ref[...] += jnp.sum(dtt, axis=0, keepdims=True)
        dy = dyb1 * _gelu_grad(y)
        dd_ref[...] += jnp.sum(dy * u, axis=0, keepdims=True)
        dyh = dy.astype(BF16)
        for j in range(S5_BLOCKS):
            cs = slice(j * S5_BC, (j + 1) * S5_BC)
            _s5_put(g_scr, j, _dot_nt(dyh[:, cs], cc_ref[:, cs]))
        _s5_scan(g_scr, base, tab, carry, tm, True)
        dus = []
        for j in range(S5_BLOCKS):
            cs = slice(j * S5_BC, (j + 1) * S5_BC)
            gb = _s5_get(g_scr, j).astype(BF16)
            dus.append(_dot_nt(gb, bb_ref[cs, :]))
            dbb_ref[cs, :] += _dot_tn(ub[:, cs], gb)
            dcc_ref[:, cs] += _dot_tn(_s5_get(s_ref, j).astype(BF16), dyh[:, cs])
        du_ref[...] = (jnp.concatenate(dus, axis=1) + d_ref[...] * dy).astype(BF16)
        top = lax.broadcasted_iota(jnp.int32, (SUBLANES, S5_BS), 0) == 0

        def corr(g_sl, s_sl):
            before = jnp.where(i == nt - 1, 0.0, sprev_ref[SUBLANES - 1:SUBLANES, s_sl])
            wrap = jnp.where(top, before, pltpu.roll(s_ref[tm - SUBLANES:tm, s_sl], 1, 0))
            return (jnp.sum(g_scr[SUBLANES:tm, g_sl] * s_ref[0:tm - SUBLANES, s_sl], axis=0, keepdims=True)
                    + jnp.sum(g_scr[0:SUBLANES, g_sl] * wrap, axis=0, keepdims=True))

        for j in range(S5_BLOCKS):
            lr = slice(j * S5_BS, (j + 1) * S5_BS)
            li = slice(S5_N + j * S5_BS, S5_N + (j + 1) * S5_BS)
            dab_ref[0:1, lr] += corr(lr, lr) + corr(li, li)
            dab_ref[1:2, lr] += corr(li, lr) - corr(lr, li)

    tile = lambda c: pl.BlockSpec((tm, c), lambda i: (nt - 1 - i, 0))
    acc = lambda r, c: jax.ShapeDtypeStruct((r, c), F32)
    return _tiled_call(
        body, name=name, grid=(nt,),
        in_specs=[tile(w), tile(w), tile(w), tile(2 * S5_N),
                  pl.BlockSpec((SUBLANES, 2 * S5_N), lambda i: (jnp.maximum((nt - 1 - i) * per - 1, 0), 0)),
                  _full((w, 2 * S5_BS)), _full((2 * S5_BS, w)), _full((SUBLANES, S5_N)), _full((1, w)),
                  _full((w, w)), _full((1, w))],
        out_specs=[tile(w), _full((w, 2 * S5_BS)), _full((2 * S5_BS, w)), _full((SUBLANES, S5_N)), _full((1, w)),
                   _full((w, w)), _full((1, w))],
        out_shape=[jax.ShapeDtypeStruct((t, w), BF16), acc(w, 2 * S5_BS), acc(2 * S5_BS, w), acc(SUBLANES, S5_N),
                   acc(1, w), acc(w, w), acc(1, w)],
        scratch_shapes=_s5_scratch(tm),
        operands=(dyb, ys5, u5, s, s, bbc, ccc, ab8, d_row, wglu, bglu_row), comm=comm)


def s5_discretize(a_re, a_im, log_dt, b_re, b_im, *, name):
    n = a_re.shape[0]

    def body(ar_ref, ai_ref, dt_ref, br_ref, bi_ref, ab_ref, bbr_ref, bbi_ref):
        ar, ai, dtv = ar_ref[...], ai_ref[...], jnp.exp(dt_ref[...])
        mag = jnp.exp(ar * dtv)
        abr = mag * jnp.cos(ai * dtv)
        abi = mag * jnp.sin(ai * dtv)
        den = ar * ar + ai * ai
        nr = abr - 1.0
        cr = (nr * ar + abi * ai) / den
        ci = (abi * ar - nr * ai) / den
        ab_ref[:, 0:1] = abr
        ab_ref[:, 1:2] = abi
        br, bi = br_ref[...], bi_ref[...]
        bbr_ref[...] = cr * br - ci * bi
        bbi_ref[...] = cr * bi + ci * br

    col = jax.ShapeDtypeStruct((n, 2), F32)
    mat = jax.ShapeDtypeStruct((n, S5_GROUP), F32)
    return pl.pallas_call(body, name=name, out_shape=[col, mat, mat])(a_re, a_im, log_dt, b_re, b_im)


def s5_discretize_bwd(a_re, a_im, log_dt, b_re, b_im, dab, dbb_re, dbb_im, *, name):
    n = a_re.shape[0]

    def body(ar_ref, ai_ref, dt_ref, br_ref, bi_ref, dab_ref, dbr_ref, dbi_ref, da_ref, ddt_ref, gbr_ref, gbi_ref):
        ar, ai, dtv = ar_ref[...], ai_ref[...], jnp.exp(dt_ref[...])
        mag = jnp.exp(ar * dtv)
        abr = mag * jnp.cos(ai * dtv)
        abi = mag * jnp.sin(ai * dtv)
        den = ar * ar + ai * ai
        nr = abr - 1.0
        cr = (nr * ar + abi * ai) / den
        ci = (abi * ar - nr * ai) / den
        br, bi = br_ref[...], bi_ref[...]
        dbr, dbi = dbr_ref[...], dbi_ref[...]
        gbr_ref[...] = cr * dbr + ci * dbi
        gbi_ref[...] = cr * dbi - ci * dbr
        gcr = jnp.sum(dbr * br + dbi * bi, axis=1, keepdims=True)
        gci = jnp.sum(dbi * br - dbr * bi, axis=1, keepdims=True)
        ilr, ili = ar / den, -ai / den
        gabr = dab_ref[:, 0:1] + (ilr * gcr + ili * gci)
        gabi = dab_ref[:, 1:2] + (ilr * gci - ili * gcr)
        t1r, t1i = dtv * abr, dtv * abi
        t2r, t2i = -(cr * ilr - ci * ili), -(cr * ili + ci * ilr)
        da_ref[:, 0:1] = (t1r * gabr + t1i * gabi) + (t2r * gcr + t2i * gci)
        da_ref[:, 1:2] = (t1r * gabi - t1i * gabr) + (t2r * gci - t2i * gcr)
        lr, li = ar * abr - ai * abi, ar * abi + ai * abr
        dlog = (lr * gabr + li * gabi) * dtv
        ddt_ref[...] = jnp.sum(dlog.reshape(S5_GROUPS, S5_STATE, 1), axis=1)

    col2 = jax.ShapeDtypeStruct((n, 2), F32)
    col1 = jax.ShapeDtypeStruct((S5_GROUPS, 1), F32)
    mat = jax.ShapeDtypeStruct((n, S5_GROUP), F32)
    return pl.pallas_call(body, name=name, out_shape=[col2, col1, mat, mat])(
        a_re, a_im, log_dt, b_re, b_im, dab, dbb_re, dbb_im)


def merge_fwd(ya, yb, graw, x, wb, wout, *, tm, name):
    t, d = x.shape

    def body(ya_ref, yb_ref, g_ref, x_ref, wb_ref, wo_ref, x1_ref, pa_ref, pb_ref):
        pa = _dot(ya_ref[...], wb_ref[0:d, :])
        pb = _dot(yb_ref[...], wb_ref[d:, :])
        gate = _sigmoid(g_ref[...].astype(F32))
        merged = gate[:, :d] * pa + gate[:, d:] * pb
        x1_ref[...] = x_ref[...] + _dot(merged.astype(BF16), wo_ref[...])
        pa_ref[...] = pa.astype(BF16)
        pb_ref[...] = pb.astype(BF16)

    tile = lambda c: pl.BlockSpec((tm, c), lambda i: (i, 0))
    sds = jax.ShapeDtypeStruct
    return pl.pallas_call(
        body, name=name, grid=(t // tm,),
        in_specs=[tile(d), tile(S5_WIDTH), tile(2 * d), tile(d), _resident((d + S5_WIDTH, d)), _resident((d, d))],
        out_specs=[tile(d), tile(d), tile(d)], out_shape=[sds((t, d), F32), sds((t, d), BF16), sds((t, d), BF16)],
        compiler_params=_params("parallel"),
    )(ya, yb, graw, x, wb, wout)


def merge_bwd(dx1b, graw, pa, pb, wb, wout, *, tm, name):
    t, d = pa.shape

    def body(dx_ref, g_ref, pa_ref, pb_ref, wb_ref, wo_ref,
             mg_ref, dg_ref, dpa_ref, dpb_ref, dya_ref, dyb_ref):
        dm = _dot_nt(dx_ref[...], wo_ref[...])
        gate = _sigmoid(g_ref[...].astype(F32))
        g0, g1 = gate[:, :d], gate[:, d:]
        pa, pb = pa_ref[...].astype(F32), pb_ref[...].astype(F32)
        mg_ref[...] = (g0 * pa + g1 * pb).astype(BF16)
        dg_ref[:, :d] = (dm * pa * g0 * (1.0 - g0)).astype(BF16)
        dg_ref[:, d:] = (dm * pb * g1 * (1.0 - g1)).astype(BF16)
        dpa = (dm * g0).astype(BF16)
        dpb = (dm * g1).astype(BF16)
        dpa_ref[...] = dpa
        dpb_ref[...] = dpb
        dya_ref[...] = _dot_nt(dpa, wb_ref[0:d, :])
        dyb_ref[...] = _dot_nt(dpb, wb_ref[d:, :])

    tile = lambda c: pl.BlockSpec((tm, c), lambda i: (i, 0))
    sds = jax.ShapeDtypeStruct
    return pl.pallas_call(
        body, name=name, grid=(t // tm,),
        in_specs=[tile(d), tile(2 * d), tile(d), tile(d), _resident((d + S5_WIDTH, d)), _resident((d, d))],
        out_specs=[tile(d), tile(2 * d), tile(d), tile(d), tile(d), tile(S5_WIDTH)],
        out_shape=[sds((t, d), BF16), sds((t, 2 * d), BF16), sds((t, d), BF16), sds((t, d), BF16),
                   sds((t, d), F32), sds((t, S5_WIDTH), F32)],
        compiler_params=_params("parallel"),
    )(dx1b, graw, pa, pb, wb, wout)


def mlp_out_loss(a1, x1, w2, gf, target, *, tm, name):
    t, d = x1.shape
    f = a1.shape[1]

    def body(a_ref, x_ref, w_ref, g_ref, tg_ref, dx_ref, dxb_ref, loss_ref, dg_ref):
        i = pl.program_id(0)
        av = jnp.maximum(a_ref[...].astype(F32), 0.0)
        x2 = x_ref[...] + _dot((av * av).astype(BF16), w_ref[...])
        r = lax.rsqrt(jnp.mean(x2 * x2, axis=-1, keepdims=True) + EPS)
        xn = x2 * r
        err = xn * g_ref[...] - tg_ref[...]
        dyf = err * (1.0 / d)
        dxn = dyf * g_ref[...]
        dx = r * (dxn - xn * jnp.mean(dxn * xn, axis=-1, keepdims=True))
        dx_ref[...] = dx
        dxb_ref[...] = dx.astype(BF16)

        @pl.when(i == 0)
        def _():
            loss_ref[...] = jnp.zeros_like(loss_ref)
            dg_ref[...] = jnp.zeros_like(dg_ref)

        loss_ref[...] += jnp.sum(err * err) * (0.5 / d)
        dg_ref[...] += jnp.sum(dyf * xn, axis=0, keepdims=True)

    tile = lambda c: pl.BlockSpec((tm, c), lambda i: (i, 0))
    sds = jax.ShapeDtypeStruct
    return pl.pallas_call(
        body, name=name, grid=(t // tm,),
        in_specs=[tile(f), tile(d), _resident((f, d)), _full((1, d)), tile(d)],
        out_specs=[tile(d), tile(d), _full((1, LANES)), _full((1, d))],
        out_shape=[sds((t, d), F32), sds((t, d), BF16), sds((1, LANES), F32), sds((1, d), F32)],
        compiler_params=_params("arbitrary"),
    )(a1, x1, w2, gf.reshape(1, d), target)


def mlp_bwd_act(dx2b, a1, w2, *, tm, name):
    t, f = a1.shape
    d = dx2b.shape[1]

    def body(dx_ref, a_ref, w_ref, o_ref):
        dact = _dot_nt(dx_ref[...], w_ref[...])
        o_ref[...] = (dact * 2.0 * jnp.maximum(a_ref[...].astype(F32), 0.0)).astype(BF16)

    tile = lambda c: pl.BlockSpec((tm, c), lambda i: (i, 0))
    return pl.pallas_call(
        body, name=name, grid=(t // tm,),
        in_specs=[tile(d), tile(f), _resident((f, d))], out_specs=tile(f),
        out_shape=jax.ShapeDtypeStruct((t, f), BF16),
        compiler_params=_params("parallel"),
    )(dx2b, a1, w2)


ADAM_TILE_ELEMS = 128 * 1024


def _row_tile(rows, cap):
    if rows <= cap:
        return rows
    best = None
    for c in range(16, cap + 1, 16):
        if rows % c == 0:
            best = c
    assert best is not None, rows
    return best


def _device_order_sum(parts_ref):
    total = parts_ref[0].astype(F32)
    for k in range(1, N_DEV):
        total = total + parts_ref[k].astype(F32)
    return total


def _adamw_math(w, m, v, gparts_ref):
    g = _device_order_sum(gparts_ref)
    mn = ADAM_B1 * m + (1.0 - ADAM_B1) * g
    vn = ADAM_B2 * v + (1.0 - ADAM_B2) * (g * g)
    m_hat = mn / (1.0 - ADAM_B1 ** ADAM_STEP)
    v_hat = vn / (1.0 - ADAM_B2 ** ADAM_STEP)
    return g, -ADAM_LR * (m_hat / (jnp.sqrt(v_hat) + ADAM_EPS) + ADAM_WD * w), mn, vn


def adamw(w, m, v, gparts, *, name):
    rows, cols = w.shape
    tr = _row_tile(rows, max(16, ADAM_TILE_ELEMS // cols))

    def body(w_ref, m_ref, v_ref, g_ref, go_ref, d_ref, mo_ref, vo_ref):
        go_ref[...], d_ref[...], mo_ref[...], vo_ref[...] = _adamw_math(w_ref[...], m_ref[...], v_ref[...], g_ref)

    tile = pl.BlockSpec((tr, cols), lambda i: (i, 0))
    out = jax.ShapeDtypeStruct((rows, cols), F32)
    return pl.pallas_call(
        body, name=name, grid=(rows // tr,),
        in_specs=[tile, tile, tile, pl.BlockSpec((N_DEV, tr, cols), lambda i: (0, i, 0))],
        out_specs=[tile, tile, tile, tile], out_shape=[out, out, out, out],
        compiler_params=_params("parallel"),
    )(w, m, v, gparts)


def all_gather_arrays(blocks, *, name):
    na = len(blocks)

    def body(*refs):
        x_refs, out_refs = refs[:na], refs[na:2 * na]
        send_sems, recv_sems, local_sems = refs[2 * na:]
        x, y, c, _ = _place()
        me, sibling = (x, y, c), (x, y, 1 - c)
        chips = [(1 - x, y), (x, 1 - y), (1 - x, 1 - y)]

        def copy(a, k, blk, to, own=False):
            px, py, pc = blk
            dst = out_refs[a].at[4 * px + 2 * py + pc]
            return pltpu.make_async_remote_copy(
                src_ref=x_refs[a] if own else dst, dst_ref=dst,
                send_sem=send_sems.at[7 * a + k], recv_sem=recv_sems.at[7 * a + k], device_id=to, device_id_type=_MESH)

        mine = [pltpu.make_async_copy(x_refs[a], out_refs[a].at[4 * x + 2 * y + c], local_sems.at[a]) for a in range(na)]
        for cp in mine:
            cp.start()
        sent = []
        for a in range(na):
            first = [copy(a, 0, me, sibling, own=True)]
            first += [copy(a, 1 + j, me, (*chip, c), own=True) for j, chip in enumerate(chips)]
            for cp in first:
                cp.start()
            sent += first
        for a in range(na):
            for j, chip in enumerate(chips):
                copy(a, 1 + j, (*chip, c), me).wait_recv()
                fwd = copy(a, 4 + j, (*chip, c), sibling)
                fwd.start()
                sent.append(fwd)
        for a in range(na):
            copy(a, 0, sibling, me).wait_recv()
            for j, chip in enumerate(chips):
                copy(a, 4 + j, (*chip, 1 - c), me).wait_recv()
        for cp in sent:
            cp.wait_send()
        for cp in mine:
            cp.wait()

    any_spec = pl.BlockSpec(memory_space=pl.ANY)
    return pl.pallas_call(
        body, name=name, in_specs=[any_spec] * na, out_specs=[any_spec] * na,
        out_shape=[jax.ShapeDtypeStruct((N_DEV,) + b.shape, b.dtype) for b in blocks],
        scratch_shapes=[pltpu.SemaphoreType.DMA((7 * na,)), pltpu.SemaphoreType.DMA((7 * na,)),
                        pltpu.SemaphoreType.DMA((na,))],
        compiler_params=pltpu.CompilerParams(has_side_effects=True),
    )(*blocks)


def all_to_all(comm, *, name):
    na = len(comm)
    kinds = [k for k, _ in comm]

    def body(*refs):
        local, remote = _comm_copies(kinds, refs[:na], refs[na:2 * na], *refs[2 * na:])
        for cp in local + remote:
            cp.start()
        for cp in remote:
            cp.wait_recv()
        for cp in remote:
            cp.wait_send()
        for cp in local:
            cp.wait()

    any_spec = pl.BlockSpec(memory_space=pl.ANY)
    return pl.pallas_call(
        body, name=name, in_specs=[any_spec] * na, out_specs=[any_spec] * na,
        out_shape=_comm_out_shapes(comm),
        scratch_shapes=[pltpu.SemaphoreType.DMA((na * N_REL,)), pltpu.SemaphoreType.DMA((na * N_REL,)),
                        pltpu.SemaphoreType.DMA((na,))],
        compiler_params=pltpu.CompilerParams(has_side_effects=True),
    )(*[a for _, a in comm])


def adamw_whole(ws, ms, vs, gparts, *, name, totals=()):
    n, nt = len(ws), len(totals)

    def body(*refs):
        w_refs, m_refs, v_refs, g_refs = refs[:n], refs[n:2 * n], refs[2 * n:3 * n], refs[3 * n:4 * n]
        t_refs, refs = refs[4 * n:4 * n + nt], refs[4 * n + nt:]
        outs, t_outs = refs[:4 * n], refs[4 * n:]
        for k in range(n):
            g, delta, mn, vn = _adamw_math(w_refs[k][...], m_refs[k][...], v_refs[k][...], g_refs[k])
            outs[k][...] = g
            outs[n + k][...] = delta
            outs[2 * n + k][...] = mn
            outs[3 * n + k][...] = vn
        for t_ref, o_ref in zip(t_refs, t_outs):
            o_ref[...] = _device_order_sum(t_ref)

    shapes = [jax.ShapeDtypeStruct(w.shape, F32) for w in ws]
    res = pl.pallas_call(
        body, name=name, out_shape=shapes * 4 + [jax.ShapeDtypeStruct(t.shape[1:], F32) for t in totals],
        compiler_params=pltpu.CompilerParams(vmem_limit_bytes=VMEM_LIMIT),
    )(*ws, *ms, *vs, *gparts, *totals)
    return [res[j * n:(j + 1) * n] for j in range(4)], res[4 * n:]


def _lane_row(v):
    return jnp.pad(v.astype(F32), (0, LANES - v.shape[0])).reshape(1, LANES)


def _block_diag_b(bb):
    eye = jnp.eye(SUBLANES, dtype=F32)
    bt = bb.reshape(S5_BLOCKS, 8, S5_STATE, S5_GROUP).transpose(0, 1, 3, 2)
    return (bt[:, :, :, None, :] * eye[None, :, None, :, None]).reshape(S5_WIDTH, S5_BS)


def _block_diag_c(cc):
    eye = jnp.eye(SUBLANES, dtype=F32)
    ct = cc.reshape(S5_BLOCKS, 8, S5_GROUP, S5_STATE).transpose(3, 0, 1, 2)
    return (ct[None, :, :, :, :] * eye[:, None, None, :, None]).reshape(S5_BS, S5_WIDTH)


def _diag_blocks_b(m):
    eye = jnp.eye(SUBLANES, dtype=F32)
    m5 = m.reshape(S5_BLOCKS, 8, S5_GROUP, 8, S5_STATE)
    return jnp.sum(m5 * eye[None, :, None, :, None], axis=3).transpose(0, 1, 3, 2).reshape(S5_GROUPS, S5_STATE, S5_GROUP)


def _diag_blocks_c(m):
    eye = jnp.eye(SUBLANES, dtype=F32)
    m5 = m.reshape(8, S5_STATE, S5_BLOCKS, 8, S5_GROUP)
    return jnp.sum(m5 * eye[:, None, None, :, None], axis=0).transpose(1, 2, 3, 0).reshape(S5_GROUPS, S5_GROUP, S5_STATE)


def train_step(x, target, p, m, v):
    t = x.shape[0]
    tm = min(256, t)
    tb = min(512, t)
    bf = lambda n: p[n].astype(BF16)
    slots = lambda n, a: a.reshape((N_DEV,) + _shard_shape(n))
    updated = {}

    def update(names, got):
        for n, parts in zip(names, got):
            updated[n] = adamw(p[n], m[n], v[n], parts, name=f"adamw_{n}")

    st_in, st_conv = all_gather_arrays([bf("w_in"), p["conv_w"]], name="gather_first")
    win_p = jnp.concatenate([st_in[k][:, a:b] for seg in IN_PADDED_ORDER for k, a, b in _shard_pieces(*IN_SEGMENTS[seg])]
                            + [jnp.zeros((D_MODEL, LANES - SSD_HEADS), BF16)], axis=1)
    conv_w8 = jnp.pad(_join_shards("conv_w", st_conv), ((0, SUBLANES - SSD_CONV), (0, 0)))
    (h, xbc_raw, graw, z, u5, dtr), (st_branch, st_out, st_glu) = rms_matmul(
        x, p["norm_mix_g"], win_p, IN_SPLITS, tm=tb, name="in_proj", out_dtypes=(BF16, BF16, BF16, BF16, F32),
        comm=gathers([bf("w_branch"), bf("w_out"), bf("s5_glu_w")]))
    w_branch = st_branch.reshape(D_MODEL + S5_WIDTH, D_MODEL)
    wout, wglu = st_out.reshape(D_MODEL, D_MODEL), st_glu.reshape(S5_WIDTH, S5_WIDTH)
    bias_row, alog_row = _lane_row(p["dt_bias"]), _lane_row(p["a_log"])
    d_row = jnp.repeat(p["d_ssd"], SSD_HEADDIM).reshape(1, SSD_HEADS * SSD_HEADDIM)
    (ya, yssd, states), (w1,) = ssd_fwd(xbc_raw, conv_w8, p["conv_b"], dtr, z, bias_row, alog_row, d_row,
                                        p["ssd_norm_g"], name="ssd_fwd", comm=gathers([bf("w_mlp_in")]))

    n = S5_N
    a_re_c, a_im_c = p["s5_a_re"].reshape(n, 1), p["s5_a_im"].reshape(n, 1)
    dt_c = jnp.repeat(p["s5_log_dt"], S5_STATE).reshape(n, 1)
    b_re_m, b_im_m = p["s5_b_re"].reshape(n, S5_GROUP), p["s5_b_im"].reshape(n, S5_GROUP)
    ab, bb_re, bb_im = s5_discretize(a_re_c, a_im_c, dt_c, b_re_m, b_im_m, name="s5_disc")
    ab8 = jnp.pad(ab.T, ((0, SUBLANES - 2), (0, 0)))
    bbc = jnp.concatenate([_block_diag_b(bb_re.reshape(S5_GROUPS, S5_STATE, S5_GROUP)),
                           _block_diag_b(bb_im.reshape(S5_GROUPS, S5_STATE, S5_GROUP))], axis=1).astype(BF16)
    ccc = jnp.concatenate([_block_diag_c(p["s5_c_re"]), -_block_diag_c(p["s5_c_im"])], axis=0).astype(BF16)
    s5d_row = p["s5_d"].reshape(1, S5_WIDTH)
    bglu_row = p["s5_glu_b"].reshape(1, S5_WIDTH)
    u5 = s5_interleave(u5, tm)
    (s, ys5, yb), (st_w2,) = s5_fwd(u5, bbc, ccc, ab8, s5d_row, wglu, bglu_row, tm=tm, name="s5_fwd",
                                    comm=gathers([bf("w_mlp_out")]))
    yb = s5_deinterleave(yb, tm)
    w2 = st_w2.reshape(D_FF, D_MODEL)

    x1, pa, pb = merge_fwd(ya, yb, graw, x, w_branch, wout, tm=tb, name="merge_fwd")
    (h2, a1), _ = rms_matmul(x1, p["norm_mlp_g"], w1, (D_FF,), tm=tb, name="mlp_in", out_dtypes=(BF16,))
    dx2, dx2b, loss_row, dgf = mlp_out_loss(a1, x1, w2, p["norm_final_g"], target, tm=tb, name="mlp_out_loss")

    g = {}
    g["norm_final_g"] = dgf[0]
    da1 = mlp_bwd_act(dx2b, a1, w2, tm=tb, name="mlp_bwd_act")
    dw2 = slots("w_mlp_out", matmul_tn(a1, dx2b, out_dtype=BF16, a_relu2=True, name="dw_mlp_out"))
    dw1 = matmul_tn(h2, da1, out_dtype=BF16, col_shards=N_DEV, name="dw_mlp_in")
    (dx1, dx1b, dgm), _ = matmul_nt_rms_bwd([da1], w1, x1, p["norm_mlp_g"], dx2, tm=tb, name="mlp_in_bwd")
    g["norm_mlp_g"] = dgm[0]
    merged, dgraw, dpa, dpb, dya, dyb = merge_bwd(dx1b, graw, pa, pb, w_branch, wout, tm=tm, name="merge_bwd")
    dwo = slots("w_out", matmul_tn(merged, dx1b, out_dtype=BF16, name="dw_out"))
    dwb = slots("w_branch", jnp.concatenate([matmul_tn(ya, dpa, out_dtype=BF16, name="dw_branch_a"),
                                             matmul_tn(yb, dpb, out_dtype=BF16, name="dw_branch_b")], axis=0))

    (du5, dbbc, dccc, dab, dd5, dwglu, dbglu), got = s5_bwd(
        s5_interleave(dyb, tm), ys5, u5, s, bbc, ccc, ab8, s5d_row, wglu, bglu_row, tm=tm, name="s5_bwd",
        comm=exchanges([dw2, dw1]))
    du5 = s5_deinterleave(du5, tm)
    update(["w_mlp_out", "w_mlp_in"], got)
    g["s5_glu_b"], g["s5_d"] = dbglu[0], dd5[0]
    g["s5_c_re"] = _diag_blocks_c(dccc[:S5_BS])
    g["s5_c_im"] = -_diag_blocks_c(dccc[S5_BS:])
    dbb_re = _diag_blocks_b(dbbc[:, :S5_BS]).reshape(n, S5_GROUP)
    dbb_im = _diag_blocks_b(dbbc[:, S5_BS:]).reshape(n, S5_GROUP)
    da, dlogdt, gb_re, gb_im = s5_discretize_bwd(a_re_c, a_im_c, dt_c, b_re_m, b_im_m, dab[:2].T, dbb_re, dbb_im,
                                                  name="s5_disc_bwd")
    g["s5_a_re"] = da[:, 0].reshape(S5_GROUPS, S5_STATE)
    g["s5_a_im"] = da[:, 1].reshape(S5_GROUPS, S5_STATE)
    g["s5_log_dt"] = dlogdt[:, 0]
    g["s5_b_re"] = gb_re.reshape(S5_GROUPS, S5_STATE, S5_GROUP)
    g["s5_b_im"] = gb_im.reshape(S5_GROUPS, S5_STATE, S5_GROUP)

    def update_small(names, got, name, totals=()):
        (gs, ds, nm, nv), sums = adamw_whole([p[n] for n in names], [m[n] for n in names], [v[n] for n in names], got,
                                             name=name, totals=totals)
        for k, n in enumerate(names):
            updated[n] = (gs[k], ds[k], nm[k], nv[k])
        return sums

    wide = ["s5_b_re", "s5_b_im"]
    late = ["ssd_norm_g", "d_ssd", "a_log", "dt_bias", "conv_b"]
    early = [n for n in SMALL_ORDER if n not in wide + late and n != "norm_mix_g"]
    (dz, dxbc_raw, ddtr, dng, dd_row, dalog_row, dbias_row, dconv_b, dconv_w8), got = ssd_bwd(
        dya, yssd, z, xbc_raw, conv_w8, p["conv_b"], dtr, states, bias_row, alog_row, d_row, p["ssd_norm_g"],
        name="ssd_bwd",
        comm=exchanges([dwo, dwb, slots("s5_glu_w", dwglu.astype(BF16))])
        + gathers([g[n] for n in wide + early] + [loss_row]))
    update(["w_out", "w_branch", "s5_glu_w"], got[:3])
    update_small(wide, got[3:3 + len(wide)], "adamw_s5_b")
    loss_sum, = update_small(early, got[3 + len(wide):-1], "adamw_small_early", totals=[got[-1]])
    g["ssd_norm_g"] = dng[0]
    g["d_ssd"], g["a_log"], g["dt_bias"] = dd_row[0, :SSD_HEADS], dalog_row[0, :SSD_HEADS], dbias_row[0, :SSD_HEADS]
    g["conv_b"] = dconv_b[0]
    dconv = dconv_w8[:SSD_CONV].reshape(SSD_CONV, N_DEV, CONV_DIM // N_DEV).transpose(1, 0, 2)
    dproj =[dxbc_raw, dgraw, dz, du5, ddtr]
    dxbc_w, dgate_w, dz_w, du5_w, ddt_w = [matmul_tn(h, piece, out_dtype=BF16, name=f"dw_in_{k}")
                                           for k, piece in enumerate(dproj)]
    by_segment = {"z": dz_w, "xbc": dxbc_w, "dt": ddt_w, "u5": du5_w, "gates": dgate_w}
    width = D_IN_PROJ // N_DEV
    dw_in = jnp.stack([jnp.concatenate(
        [by_segment[seg][:, max(k * width, lo) - lo:min((k + 1) * width, hi) - lo]
         for seg, (lo, hi) in IN_SEGMENTS.items() if max(k * width, lo) < min((k + 1) * width, hi)], axis=1)
        for k in range(N_DEV)])

    (grad_x, _, dgx), got = matmul_nt_rms_bwd(
        dproj, win_p, x, p["norm_mix_g"], dx1, tm=tb, name="in_proj_bwd",
        comm=exchanges([dw_in, dconv]) + gathers([g[n] for n in late]))
    update(["w_in", "conv_w"], got[:2])
    update_small(late, got[2:], "adamw_small_late")
    update_small(["norm_mix_g"], all_to_all(gathers([dgx[0]]), name="gather_last"), "adamw_small_last")
    return loss_sum[0, 0], grad_x, updated


WEIGHT_ORDER = ["norm_mix_g", "w_in", "conv_w", "conv_b", "dt_bias", "a_log", "d_ssd", "ssd_norm_g", "s5_a_re",
                "s5_a_im", "s5_log_dt", "s5_b_re", "s5_b_im", "s5_c_re", "s5_c_im", "s5_d", "s5_glu_w", "s5_glu_b",
                "w_branch", "w_out", "norm_mlp_g", "w_mlp_in", "w_mlp_out", "norm_final_g"]
SHARDED = {"w_in": ((D_MODEL, D_IN_PROJ), 1), "conv_w": ((SSD_CONV, CONV_DIM), 1), "s5_glu_w": ((S5_WIDTH, S5_WIDTH), 0),
           "w_branch": ((D_MODEL + S5_WIDTH, D_MODEL), 0), "w_out": ((D_MODEL, D_MODEL), 0),
           "w_mlp_in": ((D_MODEL, D_FF), 1), "w_mlp_out": ((D_FF, D_MODEL), 0)}
SHARDED_ORDER = ["w_in", "conv_w", "s5_glu_w", "w_branch", "w_out", "w_mlp_in", "w_mlp_out"]
SMALL_ORDER = [n for n in WEIGHT_ORDER if n not in SHARDED]


def _shard_shape(name):
    (r, c), ax = SHARDED[name]
    return (r, c // N_DEV) if ax == 1 else (r // N_DEV, c)


def _join_shards(name, stacked):
    (r, c), ax = SHARDED[name]
    if ax == 1:
        return stacked.transpose(1, 0, 2).reshape(r, c)
    return stacked.reshape(r, c)


def _shard_pieces(lo, hi):
    width = D_IN_PROJ // N_DEV
    out = []
    while lo < hi:
        k = lo // width
        end = min(hi, (k + 1) * width)
        out.append((k, lo - k * width, end - k * width))
        lo = end
    return out


IN_SEGMENTS = {"z": (0, 1024), "xbc": (1024, 3072), "dt": (3072, 3088), "u5": (3088, 3600), "gates": (3600, 5648)}
IN_PADDED_ORDER = ("xbc", "gates", "z", "u5", "dt")


def kernel(x, norm_mix_g, w_in, conv_w, conv_b, dt_bias, a_log, d_ssd, ssd_norm_g, s5_a_re, s5_a_im, s5_log_dt, s5_b_re, s5_b_im, s5_c_re, s5_c_im, s5_d, s5_glu_w, s5_glu_b, w_branch, w_out, norm_mlp_g, w_mlp_in, w_mlp_out, norm_final_g, loss_target, m_norm_mix_g, m_w_in, m_conv_w, m_conv_b, m_dt_bias, m_a_log, m_d_ssd, m_ssd_norm_g, m_s5_a_re, m_s5_a_im, m_s5_log_dt, m_s5_b_re, m_s5_b_im, m_s5_c_re, m_s5_c_im, m_s5_d, m_s5_glu_w, m_s5_glu_b, m_w_branch, m_w_out, m_norm_mlp_g, m_w_mlp_in, m_w_mlp_out, m_norm_final_g, v_norm_mix_g, v_w_in, v_conv_w, v_conv_b, v_dt_bias, v_a_log, v_d_ssd, v_ssd_norm_g, v_s5_a_re, v_s5_a_im, v_s5_log_dt, v_s5_b_re, v_s5_b_im, v_s5_c_re, v_s5_c_im, v_s5_d, v_s5_glu_w, v_s5_glu_b, v_w_branch, v_w_out, v_norm_mlp_g, v_w_mlp_in, v_w_mlp_out, v_norm_final_g):
    w = dict(norm_mix_g=norm_mix_g, w_in=w_in, conv_w=conv_w, conv_b=conv_b, dt_bias=dt_bias, a_log=a_log, d_ssd=d_ssd,
             ssd_norm_g=ssd_norm_g, s5_a_re=s5_a_re, s5_a_im=s5_a_im, s5_log_dt=s5_log_dt, s5_b_re=s5_b_re,
             s5_b_im=s5_b_im, s5_c_re=s5_c_re, s5_c_im=s5_c_im, s5_d=s5_d, s5_glu_w=s5_glu_w, s5_glu_b=s5_glu_b,
             w_branch=w_branch, w_out=w_out, norm_mlp_g=norm_mlp_g, w_mlp_in=w_mlp_in, w_mlp_out=w_mlp_out,
             norm_final_g=norm_final_g)
    m = dict(norm_mix_g=m_norm_mix_g, w_in=m_w_in, conv_w=m_conv_w, conv_b=m_conv_b, dt_bias=m_dt_bias, a_log=m_a_log,
             d_ssd=m_d_ssd, ssd_norm_g=m_ssd_norm_g, s5_a_re=m_s5_a_re, s5_a_im=m_s5_a_im, s5_log_dt=m_s5_log_dt,
             s5_b_re=m_s5_b_re, s5_b_im=m_s5_b_im, s5_c_re=m_s5_c_re, s5_c_im=m_s5_c_im, s5_d=m_s5_d,
             s5_glu_w=m_s5_glu_w, s5_glu_b=m_s5_glu_b, w_branch=m_w_branch, w_out=m_w_out, norm_mlp_g=m_norm_mlp_g,
             w_mlp_in=m_w_mlp_in, w_mlp_out=m_w_mlp_out, norm_final_g=m_norm_final_g)
    v = dict(norm_mix_g=v_norm_mix_g, w_in=v_w_in, conv_w=v_conv_w, conv_b=v_conv_b, dt_bias=v_dt_bias, a_log=v_a_log,
             d_ssd=v_d_ssd, ssd_norm_g=v_ssd_norm_g, s5_a_re=v_s5_a_re, s5_a_im=v_s5_a_im, s5_log_dt=v_s5_log_dt,
             s5_b_re=v_s5_b_re, s5_b_im=v_s5_b_im, s5_c_re=v_s5_c_re, s5_c_im=v_s5_c_im, s5_d=v_s5_d,
             s5_glu_w=v_s5_glu_w, s5_glu_b=v_s5_glu_b, w_branch=v_w_branch, w_out=v_w_out, norm_mlp_g=v_norm_mlp_g,
             w_mlp_in=v_w_mlp_in, w_mlp_out=v_w_mlp_out, norm_final_g=v_norm_final_g)

    loss, grad_x, upd = train_step(x[0], loss_target[0], w, m, v)
    return (loss, grad_x.reshape(x.shape), *[upd[n][j] for j in range(4) for n in WEIGHT_ORDER])
```

```python
import functools
import math

import jax
import jax.numpy as jnp
from jax import lax
from jax.experimental import pallas as pl
from jax.experimental.pallas import tpu as pltpu

F32 = jnp.float32
BF16 = jnp.bfloat16
HI = lax.Precision.HIGHEST

N_DEV = 8
D_MODEL = 1024
SSD_HEADS = 16
SSD_HEADDIM = 64
SSD_GROUPS = 4
SSD_HPG = 4
SSD_STATE = 128
SSD_CHUNK = 128
SSD_CONV = 4
CONV_DIM = 2048
S5_WIDTH = 512
S5_GROUP = 16
S5_GROUPS = 32
S5_STATE = 64
S5_N = S5_GROUPS * S5_STATE
D_FF = 4096
D_IN_PROJ = 5648
IN_SPLITS = (2048, 2048, 1024, 512, 128)
D_IN_PAD = sum(IN_SPLITS)
EPS = 1e-6
LANES = 128
SUBLANES = 8
VMEM_LIMIT = 56 * 1024 * 1024

ADAM_LR = 0.001
ADAM_B1 = 0.9
ADAM_B2 = 0.999
ADAM_EPS = 1e-08
ADAM_WD = 0.01
ADAM_STEP = 10

GELU_C = math.sqrt(2.0 / math.pi)
GELU_K = 0.044715


def _params(*sem):
    return pltpu.CompilerParams(dimension_semantics=sem, vmem_limit_bytes=VMEM_LIMIT)


def _full(shape):
    nd = len(shape)
    return pl.BlockSpec(shape, lambda *_: (0,) * nd)


def _resident(shape):
    nd = len(shape)
    return pl.BlockSpec(shape, lambda *_: (0,) * nd, pipeline_mode=pl.Buffered(1))


_MESH = pl.DeviceIdType.MESH
_RELATIONS = [(dx, dy, dc) for dx in (0, 1) for dy in (0, 1) for dc in (0, 1)][1:]
N_REL = len(_RELATIONS)


def _place():
    x, y, c = lax.axis_index("x"), lax.axis_index("y"), lax.axis_index("c")
    return x, y, c, 4 * x + 2 * y + c


def gathers(arrays):
    return [("gather", a) for a in arrays]


def exchanges(arrays):
    return [("exchange", a) for a in arrays]


def _comm_out_shapes(comm):
    return [jax.ShapeDtypeStruct(((N_DEV,) if kind == "gather" else ()) + a.shape, a.dtype) for kind, a in comm]


def _comm_copies(kinds, src_refs, dst_refs, send_sems, recv_sems, local_sems):
    x, y, c, idx = _place()
    local, remote = [], []
    for a, (kind, src, dst) in enumerate(zip(kinds, src_refs, dst_refs)):
        local.append(pltpu.make_async_copy(src if kind == "gather" else src.at[idx], dst.at[idx], local_sems.at[a]))
        for k, (dx, dy, dc) in enumerate(_RELATIONS):
            px, py, pc = x ^ dx, y ^ dy, c ^ dc
            remote.append(pltpu.make_async_remote_copy(
                src_ref=src if kind == "gather" else src.at[4 * px + 2 * py + pc], dst_ref=dst.at[idx],
                send_sem=send_sems.at[N_REL * a + k], recv_sem=recv_sems.at[N_REL * a + k],
                device_id=(px, py, pc), device_id_type=_MESH))
    return local, remote


def _tiled_call(body, *, name, grid, in_specs, out_specs, out_shape, operands, scratch_shapes=(), comm=None):
    params = pltpu.CompilerParams(dimension_semantics=("arbitrary",), vmem_limit_bytes=VMEM_LIMIT,
                                  has_side_effects=bool(comm))
    if not comm:
        outs = pl.pallas_call(body, name=name, grid=grid, in_specs=in_specs, out_specs=out_specs, out_shape=out_shape,
                              scratch_shapes=list(scratch_shapes), compiler_params=params)(*operands)
        return outs, []
    kind = [k for k, _ in comm]
    arrays = [a for _, a in comm]
    na, n_in, n_out, n_scr = len(arrays), len(in_specs), len(out_specs), len(scratch_shapes)
    last = grid[0] - 1

    def hosted(*refs):
        ins, refs = refs[:n_in], refs[n_in:]
        cin, refs = refs[:na], refs[na:]
        outs, refs = refs[:n_out], refs[n_out:]
        cout, refs = refs[:na], refs[na:]
        scr, sems = refs[:n_scr], refs[n_scr:]
        i = pl.program_id(0)

        @pl.when(i == 0)
        def _():
            local, remote = _comm_copies(kind, cin, cout, *sems)
            for cp in local + remote:
                cp.start()

        body(*ins, *outs, *scr)

        @pl.when(i == last)
        def _():
            local, remote = _comm_copies(kind, cin, cout, *sems)
            for cp in remote:
                cp.wait_recv()
            for cp in remote:
                cp.wait_send()
            for cp in local:
                cp.wait()

    any_spec = pl.BlockSpec(memory_space=pl.ANY)
    res = pl.pallas_call(
        hosted, name=name, grid=grid,
        in_specs=list(in_specs) + [any_spec] * na, out_specs=list(out_specs) + [any_spec] * na,
        out_shape=list(out_shape) + _comm_out_shapes(comm),
        scratch_shapes=list(scratch_shapes) + [pltpu.SemaphoreType.DMA((N_REL * na,)), pltpu.SemaphoreType.DMA((N_REL * na,)),
                                               pltpu.SemaphoreType.DMA((na,))],
        compiler_params=params)(*operands, *arrays)
    return res[:n_out], res[n_out:]


def _dot(a, b):
    return jnp.dot(a, b, preferred_element_type=F32)


def _dot_nt(a, b):
    return lax.dot_general(a, b, (((1,), (1,)), ((), ())), preferred_element_type=F32)


def _dot_tn(a, b):
    return lax.dot_general(a, b, (((0,), (0,)), ((), ())), preferred_element_type=F32)


def _dot_hi(a, b):
    return jnp.dot(a, b, preferred_element_type=F32, precision=HI)


def _dot_mid(a, b):
    return jnp.dot(a, b, preferred_element_type=F32, precision=lax.Precision.HIGH)


def _dot_nt_hi(a, b):
    return lax.dot_general(a, b, (((1,), (1,)), ((), ())), preferred_element_type=F32, precision=HI)


def _dot_tn_hi(a, b):
    return lax.dot_general(a, b, (((0,), (0,)), ((), ())), preferred_element_type=F32, precision=HI)


def _sigmoid(x):
    return 1.0 / (1.0 + jnp.exp(-x))


def _softplus(x):
    return jnp.maximum(x, 0.0) + jnp.log(1.0 + jnp.exp(-jnp.abs(x)))


def _gelu(x):
    return 0.5 * x * (1.0 + jnp.tanh(GELU_C * (x + GELU_K * x * x * x)))


def _gelu_grad(x):
    th = jnp.tanh(GELU_C * (x + GELU_K * x * x * x))
    return 0.5 * (1.0 + th) + 0.5 * x * (1.0 - th * th) * GELU_C * (1.0 + 3.0 * GELU_K * x * x)


def rms_matmul(x, g, w, splits, *, tm, name, comm=None, out_dtypes=None):
    t, d = x.shape
    stacked = w.ndim == 3
    n = w.shape[0] * w.shape[2] if stacked else w.shape[1]
    assert sum(splits) == n and (len(splits) == 1 or not stacked)

    def body(x_ref, g_ref, w_ref, h_ref, *o_refs):
        xv = x_ref[...]
        r = lax.rsqrt(jnp.mean(xv * xv, axis=-1, keepdims=True) + EPS)
        h = (xv * r * g_ref[...]).astype(BF16)
        h_ref[...] = h
        if stacked:
            wk = w.shape[2]
            for k in range(w.shape[0]):
                o_refs[0][:, k * wk:(k + 1) * wk] = _dot(h, w_ref[k]).astype(o_refs[0].dtype)
            return
        off = 0
        for o_ref, width in zip(o_refs, splits):
            o_ref[...] = _dot(h, w_ref[:, off:off + width]).astype(o_ref.dtype)
            off += width

    tile = lambda c: pl.BlockSpec((tm, c), lambda i: (i, 0))
    dtypes = out_dtypes or (F32,) * len(splits)
    return _tiled_call(
        body, name=name, grid=(t // tm,),
        in_specs=[tile(d), _full((1, d)), _resident(w.shape)],
        out_specs=[tile(d)] + [tile(c) for c in splits],
        out_shape=[jax.ShapeDtypeStruct((t, d), BF16)] + [jax.ShapeDtypeStruct((t, c), dt) for c, dt in zip(splits, dtypes)],
        operands=(x, g.reshape(1, d), w), comm=comm)


MATMUL_TN_ACC_BYTES = 16 * 1024 * 1024


def _pick(n, cap):
    best = LANES
    for c in range(LANES, cap + 1, LANES):
        if n % c == 0:
            best = c
    return best


def matmul_tn(a, b, *, name, out_dtype=F32, col_shards=None, tk=1024, a_relu2=False):
    t, m = a.shape
    n = b.shape[1]
    tm = _pick(m, 1024)
    whole_n = tm * n * 4 <= MATMUL_TN_ACC_BYTES
    tn = n if whole_n else _pick(n, 1280)
    assert whole_n or not col_shards
    tk = min(tk if tn <= 2048 else tk // 2, t)
    nk = t // tk

    def body(a_ref, b_ref, o_ref, acc):
        k = pl.program_id(2)

        @pl.when(k == 0)
        def _():
            acc[...] = jnp.zeros_like(acc)

        av = a_ref[...]
        if a_relu2:
            pos = jnp.maximum(av.astype(F32), 0.0)
            av = (pos * pos).astype(BF16)
        acc[...] += _dot_tn(av, b_ref[...])

        @pl.when(k == nk - 1)
        def _():
            if col_shards:
                w = n // col_shards
                for s in range(col_shards):
                    o_ref[s] = acc[:, s * w:(s + 1) * w].astype(out_dtype)
            else:
                o_ref[...] = acc[...].astype(out_dtype)

    if col_shards:
        out_spec = pl.BlockSpec((col_shards, tm, n // col_shards), lambda i, j, k: (0, i, 0))
        out_shape = jax.ShapeDtypeStruct((col_shards, m, n // col_shards), out_dtype)
    else:
        out_spec = pl.BlockSpec((tm, tn), lambda i, j, k: (i, j))
        out_shape = jax.ShapeDtypeStruct((m, n), out_dtype)
    return pl.pallas_call(
        body, name=name, grid=(m // tm, n // tn, nk),
        in_specs=[pl.BlockSpec((tk, tm), lambda i, j, k: (k, i)), pl.BlockSpec((tk, tn), lambda i, j, k: (k, j))],
        out_specs=out_spec, out_shape=out_shape,
        scratch_shapes=[pltpu.VMEM((tm, tn), F32)],
        compiler_params=_params("parallel", "parallel", "arbitrary"),
    )(a, b)


def matmul_nt_rms_bwd(dys, w, x, g, dres, *, tm, name, comm=None):
    t = x.shape[0]
    stacked = w.ndim == 3
    d = w.shape[1] if stacked else w.shape[0]
    widths = [dy.shape[1] for dy in dys]
    n_dy = len(dys)

    def body(*refs):
        dy_refs = refs[:n_dy]
        w_ref, x_ref, g_ref, dres_ref, dx_ref, dxb_ref, dg_ref = refs[n_dy:]
        i = pl.program_id(0)
        if stacked:
            wk = w.shape[2]
            dh = _dot_nt(dy_refs[0][:, 0:wk], w_ref[0])
            for k in range(1, w.shape[0]):
                dh = dh + _dot_nt(dy_refs[0][:, k * wk:(k + 1) * wk], w_ref[k])
        else:
            dh, off = None, 0
            for dy_ref, width in zip(dy_refs, widths):
                part = _dot_nt(dy_ref[...], w_ref[:, off:off + width])
                dh = part if dh is None else dh + part
                off += width
        xv = x_ref[...]
        r = lax.rsqrt(jnp.mean(xv * xv, axis=-1, keepdims=True) + EPS)
        xn = xv * r
        dxn = dh * g_ref[...]
        dx = dres_ref[...] + r * (dxn - xn * jnp.mean(dxn * xn, axis=-1, keepdims=True))
        dx_ref[...] = dx
        dxb_ref[...] = dx.astype(BF16)

        @pl.when(i == 0)
        def _():
            dg_ref[...] = jnp.zeros_like(dg_ref)

        dg_ref[...] += jnp.sum(dh * xn, axis=0, keepdims=True)

    tile = lambda c: pl.BlockSpec((tm, c), lambda i: (i, 0))
    return _tiled_call(
        body, name=name, grid=(t // tm,),
        in_specs=[tile(c) for c in widths] + [_resident(w.shape), tile(d), _full((1, d)), tile(d)],
        out_specs=[tile(d), tile(d), _full((1, d))],
        out_shape=[jax.ShapeDtypeStruct((t, d), F32), jax.ShapeDtypeStruct((t, d), BF16),
                   jax.ShapeDtypeStruct((1, d), F32)],
        operands=(*dys, w, x, g.reshape(1, d), dres), comm=comm)


def _conv_windows(ext_ref, tm):
    return [ext_ref[SUBLANES - 3 + k:SUBLANES - 3 + k + tm, :] for k in range(SSD_CONV)]


def _conv_taps(windows, w_ref):
    acc = windows[3] * w_ref[3:4, :]
    for k in range(3):
        acc = acc + windows[k] * w_ref[k:k + 1, :]
    return acc


def _ssd_chunk_terms(dtr, bias_row, alog_row):
    q = SSD_CHUNK
    row = lax.broadcasted_iota(jnp.int32, (q, q), 0)
    col = lax.broadcasted_iota(jnp.int32, (q, q), 1)
    ltri = (col <= row).astype(F32)
    dt = _softplus(dtr + bias_row)
    a_row = -jnp.exp(alog_row)
    la = dt * a_row
    cum = _dot_hi(ltri, la)
    cum_t = _dot_tn_hi(la, (row <= col).astype(F32))
    return dt, a_row, cum, cum_t, ltri, row, col


def _head_maps():
    di = SSD_HEADS * SSD_HEADDIM
    expand = (lax.broadcasted_iota(jnp.int32, (LANES, di), 1) // SSD_HEADDIM
              == lax.broadcasted_iota(jnp.int32, (LANES, di), 0)).astype(F32)
    collapse = (lax.broadcasted_iota(jnp.int32, (di, LANES), 0) // SSD_HEADDIM
                == lax.broadcasted_iota(jnp.int32, (di, LANES), 1)).astype(F32)
    return expand, collapse


def _ssd_decay(cum, cum_t, h, causal):
    seg = cum[:, h:h + 1] - cum_t[h:h + 1, :]
    return jnp.where(causal, jnp.exp(jnp.where(causal, seg, 0.0)), 0.0)


def _pair_block_diag(x2):
    low = lax.broadcasted_iota(jnp.int32, x2.shape, 1) < SSD_HEADDIM
    zero = jnp.zeros_like(x2)
    return jnp.concatenate([jnp.where(low, x2, zero), jnp.where(low, zero, x2)], axis=0)


def ssd_fwd(xbc_raw, conv_w8, conv_b, dtr, z, bias_row, alog_row, d_row, norm_g, *, name, comm=None):
    t = xbc_raw.shape[0]
    q, p, n = SSD_CHUNK, SSD_HEADDIM, SSD_STATE
    nc = t // q
    di = SSD_HEADS * p
    gw = di // SSD_GROUPS

    def body(raw_ref, cw_ref, cb_ref, dtr_ref, z_ref, bias_ref, alog_ref, d_ref, g_ref,
             ya_ref, yssd_ref, st_ref, state, ext, xbc_s):
        ci = pl.program_id(0)

        @pl.when(ci == 0)
        def _():
            state[...] = jnp.zeros_like(state)
            ext[0:SUBLANES, :] = jnp.zeros((SUBLANES, CONV_DIM), F32)

        ext[SUBLANES:SUBLANES + q, :] = raw_ref[...].astype(F32)
        xc = _conv_taps(_conv_windows(ext, q), cw_ref) + cb_ref[...]
        ext[0:SUBLANES, :] = ext[q:q + SUBLANES, :]
        xbc_s[...] = xc * _sigmoid(xc)
        st_ref[0] = state[...]
        dt, _, cum, cum_t, _, row, col = _ssd_chunk_terms(dtr_ref[...], bias_ref[...], alog_ref[...])
        causal = row >= col
        expand, _ = _head_maps()
        dt_full = _dot_mid(dt, expand)
        cum_full = _dot_mid(cum, expand)
        last_full = cum_full[q - 1:q, :]
        xs = xbc_s[:, 0:di]
        xdt = xs * dt_full
        xd = (xdt * jnp.exp(last_full - cum_full)).astype(BF16)
        xdt_b = xdt.astype(BF16)
        e_full = jnp.exp(cum_full)
        eend_full = jnp.exp(last_full)
        for g in range(SSD_GROUPS):
            slab = slice(g * gw, (g + 1) * gw)
            bg = xbc_s[:, di + g * n:di + (g + 1) * n].astype(BF16)
            cg = xbc_s[:, di + (SSD_GROUPS + g) * n:di + (SSD_GROUPS + g + 1) * n].astype(BF16)
            cb = _dot_nt(cg, bg)
            st_g = state[:, slab]
            pairs = []
            for pr in range(SSD_HPG // 2):
                h0 = g * SSD_HPG + 2 * pr
                m0 = (cb * _ssd_decay(cum, cum_t, h0, causal)).astype(BF16)
                m1 = (cb * _ssd_decay(cum, cum_t, h0 + 1, causal)).astype(BF16)
                pairs.append(_dot(jnp.concatenate([m0, m1], axis=1), _pair_block_diag(xdt_b[:, h0 * p:(h0 + 2) * p])))
            y = jnp.concatenate(pairs, axis=1) + e_full[:, slab] * _dot(cg, st_g.astype(BF16))
            yssd_ref[:, slab] = y + d_ref[:, slab] * xs[:, slab]
            state[:, slab] = eend_full[:, slab] * st_g + _dot_tn(bg, xd[:, slab])
        zv = z_ref[...].astype(F32)
        ya1 = yssd_ref[...] * (zv * _sigmoid(zv))
        for g in range(SSD_GROUPS):
            gsl = slice(g * gw, (g + 1) * gw)
            sl = ya1[:, gsl]
            rg = lax.rsqrt(jnp.mean(sl * sl, axis=-1, keepdims=True) + EPS)
            ya_ref[:, gsl] = (sl * rg * g_ref[:, gsl]).astype(BF16)

    blk = lambda w, j: pl.BlockSpec((q, w), lambda c: (c, j))
    return _tiled_call(
        body, name=name, grid=(nc,),
        in_specs=[blk(CONV_DIM, 0), _full((SUBLANES, CONV_DIM)), _full((1, CONV_DIM)), blk(LANES, 0), blk(di, 0),
                  _full((1, LANES)), _full((1, LANES)), _full((1, di)), _full((1, di))],
        out_specs=[blk(di, 0), blk(di, 0), pl.BlockSpec((1, n, di), lambda c: (c, 0, 0))],
        out_shape=[jax.ShapeDtypeStruct((t, di), BF16), jax.ShapeDtypeStruct((t, di), F32),
                   jax.ShapeDtypeStruct((nc, n, di), F32)],
        scratch_shapes=[pltpu.VMEM((n, di), F32), pltpu.VMEM((SUBLANES + q, CONV_DIM), F32), pltpu.VMEM((q, CONV_DIM), F32)],
        operands=(xbc_raw, conv_w8, conv_b.reshape(1, CONV_DIM), dtr, z, bias_row, alog_row, d_row,
                  norm_g.reshape(1, di)), comm=comm)


def ssd_bwd(dya, yssd, z, xbc_raw, conv_w8, conv_b, dtr, states, bias_row, alog_row, d_row, norm_g, *, name, comm=None):
    t = xbc_raw.shape[0]
    q, p, n = SSD_CHUNK, SSD_HEADDIM, SSD_STATE
    nc = t // q
    di = SSD_HEADS * p
    gw = di // SSD_GROUPS
    per = q // (2 * SUBLANES)

    def body(dya_ref, yssd_ref, z_ref, raw_ref, prev_ref, cw_ref, cb_ref, dtr_ref, st_ref, bias_ref, alog_ref, d_ref,
             g_ref, dz_ref, draw_ref, ddtr_ref, dng_ref, dd_ref, dalog_ref, dbias_ref, dcb_ref, dcw_ref,
             dstate, dyssd, xbc_s, dxbc_ref, ext, aft):
        ci = pl.program_id(0)

        @pl.when(ci == 0)
        def _():
            dstate[...] = jnp.zeros_like(dstate)
            aft[q:q + SUBLANES, :] = jnp.zeros((SUBLANES, CONV_DIM), F32)
            for r in (dng_ref, dd_ref, dalog_ref, dbias_ref, dcb_ref, dcw_ref):
                r[...] = jnp.zeros_like(r)

        ext[0:SUBLANES, :] = jnp.where(ci == nc - 1, 0.0, prev_ref[SUBLANES:2 * SUBLANES, :].astype(F32))
        ext[SUBLANES:SUBLANES + q, :] = raw_ref[...].astype(F32)
        windows = _conv_windows(ext, q)
        xc = _conv_taps(windows, cw_ref) + cb_ref[...]
        sig_c = _sigmoid(xc)
        xbc_s[...] = xc * sig_c

        zv = z_ref[...].astype(F32)
        sz = _sigmoid(zv)
        silu_z = zv * sz
        yv = yssd_ref[...]
        ya1 = yv * silu_z
        dyav = dya_ref[...]
        for g in range(SSD_GROUPS):
            gsl = slice(g * gw, (g + 1) * gw)
            sl = ya1[:, gsl]
            rg = lax.rsqrt(jnp.mean(sl * sl, axis=-1, keepdims=True) + EPS)
            ya2 = sl * rg
            dy_g = dyav[:, gsl]
            dng_ref[:, gsl] += jnp.sum(dy_g * ya2, axis=0, keepdims=True)
            dya2 = dy_g * g_ref[:, gsl]
            dya1 = rg * (dya2 - ya2 * jnp.mean(dya2 * ya2, axis=-1, keepdims=True))
            dyssd[:, gsl] = dya1 * silu_z[:, gsl]
            dz_ref[:, gsl] = (dya1 * yv[:, gsl] * (sz[:, gsl] * (1.0 + zv[:, gsl] * (1.0 - sz[:, gsl])))).astype(BF16)

        dtr_v = dtr_ref[...]
        dt, a_row, cum, cum_t, ltri, row, col = _ssd_chunk_terms(dtr_v, bias_ref[...], alog_ref[...])
        causal = row >= col
        expand, collapse = _head_maps()
        lane = lax.broadcasted_iota(jnp.int32, (1, LANES), 1)
        sub = lax.broadcasted_iota(jnp.int32, (LANES, 1), 0)
        is_last = lax.broadcasted_iota(jnp.int32, (q, 1), 0) == q - 1
        low = lax.broadcasted_iota(jnp.int32, (q, 2 * p), 1) < p
        dt_full = _dot_mid(dt, expand)
        cum_full = _dot_mid(cum, expand)
        last_full = cum_full[q - 1:q, :]
        e_full = jnp.exp(cum_full)
        eend_full = jnp.exp(last_full)
        dend_full = jnp.exp(last_full - cum_full)
        xs = xbc_s[:, 0:di]
        xdt = xs * dt_full
        xdt_b = xdt.astype(BF16)
        xd_b = (xdt * dend_full).astype(BF16)
        dy_all = dyssd[...]
        dy_b = dy_all.astype(BF16)
        dg_b = (dy_all * e_full).astype(BF16)
        dcum_mat = jnp.zeros((q, LANES), F32)
        rmat = jnp.zeros((LANES, q), F32)
        yoff_prod, bds_list, dx_list, dlast_parts = [], [], [], []
        for g in range(SSD_GROUPS):
            slab = slice(g * gw, (g + 1) * gw)
            bg = xbc_s[:, di + g * n:di + (g + 1) * n].astype(BF16)
            cg = xbc_s[:, di + (SSD_GROUPS + g) * n:di + (SSD_GROUPS + g + 1) * n].astype(BF16)
            cb = _dot_nt(cg, bg)
            st_g = st_ref[0, :, slab]
            st_b = st_g.astype(BF16)
            dsn = dstate[:, slab]
            dsn_b = dsn.astype(BF16)
            yoff_prod.append(dy_all[:, slab] * e_full[:, slab] * _dot(cg, st_b))
            dcg = _dot_nt(dg_b[:, slab], st_b)
            dstate[:, slab] = _dot_tn(cg, dg_b[:, slab]) + eend_full[:, slab] * dsn
            bds_list.append(_dot(bg, dsn_b))
            dbg = _dot_nt(xd_b[:, slab], dsn_b)
            dlast_parts.append(jnp.sum(dsn * st_g, axis=0, keepdims=True))
            dcb = jnp.zeros((q, q), F32)
            dx_pairs = []
            for pr in range(SSD_HPG // 2):
                h0 = g * SSD_HPG + 2 * pr
                ps = slice(h0 * p, (h0 + 2) * p)
                l0 = _ssd_decay(cum, cum_t, h0, causal)
                l1 = _ssd_decay(cum, cum_t, h0 + 1, causal)
                m0, m1 = cb * l0, cb * l1
                dyp = dy_b[:, ps]
                zero = jnp.zeros_like(dyp)
                dy0, dy1 = jnp.where(low, dyp, zero), jnp.where(low, zero, dyp)
                dm0 = _dot_nt(dy0, xdt_b[:, ps])
                dm1 = _dot_nt(dy1, xdt_b[:, ps])
                w0, w1 = dm0 * m0, dm1 * m1
                dcum_mat = (dcum_mat + jnp.sum(w0, axis=1, keepdims=True) * (lane == h0).astype(F32)
                            + jnp.sum(w1, axis=1, keepdims=True) * (lane == h0 + 1).astype(F32))
                rmat = (rmat + jnp.where(sub == h0, jnp.sum(w0, axis=0, keepdims=True), 0.0)
                        + jnp.where(sub == h0 + 1, jnp.sum(w1, axis=0, keepdims=True), 0.0))
                dcb = dcb + dm0 * l0 + dm1 * l1
                dx_pairs.append(_dot_tn(jnp.concatenate([m0.astype(BF16), m1.astype(BF16)], axis=0),
                                        jnp.concatenate([dy0, dy1], axis=0)))
            dx_list.append(jnp.concatenate(dx_pairs, axis=1))
            dcb_b = dcb.astype(BF16)
            dxbc_ref[:, di + g * n:di + (g + 1) * n] = dbg + _dot_tn(dcb_b, cg)
            dxbc_ref[:, di + SSD_GROUPS * n + g * n:di + SSD_GROUPS * n + (g + 1) * n] = dcg + _dot(dcb_b, bg)
        bds_all = jnp.concatenate(bds_list, axis=1)
        dx_all = jnp.concatenate(dx_list, axis=1) + dend_full * bds_all
        last_row = cum[q - 1:q, :]
        qv = _dot_mid(xdt * bds_all, collapse) * jnp.exp(last_row - cum)
        dcum_mat = dcum_mat + _dot_mid(jnp.concatenate(yoff_prod, axis=1), collapse) - qv
        dlast_full = jnp.broadcast_to(jnp.concatenate(dlast_parts, axis=1), (SUBLANES, di))
        dlast_row = _dot_mid(dlast_full, collapse)[0:1, :] * jnp.exp(last_row) + jnp.sum(qv, axis=0, keepdims=True)
        dcum_mat = dcum_mat + jnp.where(is_last, dlast_row, 0.0)
        dla = _dot_tn_hi(ltri, dcum_mat) - _dot_nt_hi((row <= col).astype(F32), rmat)
        ddt_mat = _dot_mid(dx_all * xs, collapse) + dla * a_row
        dxbc_ref[:, 0:di] = d_ref[...] * dy_all + dx_all * dt_full
        ddtr = ddt_mat * _sigmoid(dtr_v + bias_ref[...])
        ddtr_ref[...] = ddtr.astype(BF16)
        dd_full = jnp.broadcast_to(jnp.sum(dy_all * xs, axis=0, keepdims=True), (SUBLANES, di))
        dd_ref[...] += _dot_mid(dd_full, collapse)[0:1, :]
        dalog_ref[...] += jnp.sum(dla * dt, axis=0, keepdims=True) * a_row
        dbias_ref[...] += jnp.sum(ddtr, axis=0, keepdims=True)

        dxc = dxbc_ref[...] * (sig_c * (1.0 + xc * (1.0 - sig_c)))
        dcb_ref[...] += jnp.sum(dxc, axis=0, keepdims=True)
        taps = [jnp.sum(dxc * windows[k], axis=0, keepdims=True) for k in range(SSD_CONV)]
        dcw_ref[...] += jnp.concatenate(taps + [jnp.zeros((SUBLANES - SSD_CONV, CONV_DIM), F32)], axis=0)
        aft[0:q, :] = dxc
        draw = dxc * cw_ref[3:4, :]
        for j in range(1, SSD_CONV):
            draw = draw + aft[j:j + q, :] * cw_ref[3 - j:4 - j, :]
        draw_ref[...] = draw.astype(BF16)
        aft[q:q + SUBLANES, :] = dxc[0:SUBLANES, :]

    blk = lambda w, j: pl.BlockSpec((q, w), lambda c: (nc - 1 - c, j))
    row_out = lambda w: jax.ShapeDtypeStruct((1, w), F32)
    return _tiled_call(
        body, name=name, grid=(nc,),
        in_specs=[blk(di, 0), blk(di, 0), blk(di, 0), blk(CONV_DIM, 0),
                  pl.BlockSpec((2 * SUBLANES, CONV_DIM), lambda c: (jnp.maximum((nc - 1 - c) * per - 1, 0), 0)),
                  _full((SUBLANES, CONV_DIM)), _full((1, CONV_DIM)), blk(LANES, 0),
                  pl.BlockSpec((1, n, di), lambda c: (nc - 1 - c, 0, 0)),
                  _full((1, LANES)), _full((1, LANES)), _full((1, di)), _full((1, di))],
        out_specs=[blk(di, 0), blk(CONV_DIM, 0), blk(LANES, 0),
                   _full((1, di)), _full((1, LANES)), _full((1, LANES)), _full((1, LANES)),
                   _full((1, CONV_DIM)), _full((SUBLANES, CONV_DIM))],
        out_shape=[jax.ShapeDtypeStruct((t, di), BF16), jax.ShapeDtypeStruct((t, CONV_DIM), BF16),
                   jax.ShapeDtypeStruct((t, LANES), BF16), row_out(di), row_out(LANES), row_out(LANES), row_out(LANES),
                   row_out(CONV_DIM), jax.ShapeDtypeStruct((SUBLANES, CONV_DIM), F32)],
        scratch_shapes=[pltpu.VMEM((n, di), F32), pltpu.VMEM((q, di), F32), pltpu.VMEM((q, CONV_DIM), F32),
                        pltpu.VMEM((q, CONV_DIM), F32), pltpu.VMEM((SUBLANES + q, CONV_DIM), F32),
                        pltpu.VMEM((q + SUBLANES, CONV_DIM), F32)],
        operands=(dya, yssd, z, xbc_raw, xbc_raw, conv_w8, conv_b.reshape(1, CONV_DIM), dtr, states, bias_row,
                  alog_row, d_row, norm_g.reshape(1, di)),
        comm=comm)


S5_BLOCKS = 4
S5_BC = S5_WIDTH // S5_BLOCKS
S5_BS = S5_N // S5_BLOCKS


def _s5_tables(ar, ai, reverse):
    pows = [(ar, ai)]
    for _ in range(SUBLANES - 1):
        pr, pi = pows[-1]
        pows.append((pr * ar - pi * ai, pr * ai + pi * ar))
    row = lax.broadcasted_iota(jnp.int32, (SUBLANES, ar.shape[1]), 0)
    out = []
    for k in (1, 2, 4):
        pr, pi = pows[k - 1]
        mask = (row < SUBLANES - k) if reverse else (row >= k)
        out += [jnp.where(mask, pr, 0.0), jnp.where(mask, pi, 0.0)]
    order = list(range(SUBLANES))
    if reverse:
        order = order[::-1]
    out.append(jnp.concatenate([pows[e][0] for e in order], axis=0))
    out.append(jnp.concatenate([pows[e][1] for e in order], axis=0))
    return out


def s5_interleave(a, tm):
    t, c = a.shape
    return a.reshape(t // tm, SUBLANES, tm // SUBLANES, c).transpose(0, 2, 1, 3).reshape(t, c)


def s5_deinterleave(a, tm):
    t, c = a.shape
    return a.reshape(t // tm, tm // SUBLANES, SUBLANES, c).transpose(0, 2, 1, 3).reshape(t, c)


def _s5_scan_setup(ab_ref, base, tab, seg, reverse):
    ar = ab_ref[0:1, :]
    ai = -ab_ref[1:2, :] if reverse else ab_ref[1:2, :]
    base[0:1, :] = ar
    base[1:2, :] = ai
    assert seg & (seg - 1) == 0
    pr, pi = ar, ai
    for _ in range(seg.bit_length() - 1):
        pr, pi = pr * pr - pi * pi, 2.0 * pr * pi
    for k, tv in enumerate(_s5_tables(pr, pi, reverse)):
        tab[k] = tv


def _s5_scan(s_scr, base, tab, carry, tm, reverse):
    seg = tm // SUBLANES
    width = S5_BS
    first = lax.broadcasted_iota(jnp.int32, (SUBLANES, width), 0) == (SUBLANES - 1 if reverse else 0)

    def rows(i):
        step = (seg - 1 - i) if reverse else i
        return pl.ds(pl.multiple_of(step * SUBLANES, SUBLANES), SUBLANES)

    for lc in range(S5_N // width):
        lr = slice(lc * width, (lc + 1) * width)
        li = slice(S5_N + lc * width, S5_N + (lc + 1) * width)
        ar = jnp.broadcast_to(base[0:1, lr], (SUBLANES, width))
        ai = jnp.broadcast_to(base[1:2, lr], (SUBLANES, width))

        def advance(i, x, lr=lr, li=li, ar=ar, ai=ai):
            xr, xi = x
            return ar * xr - ai * xi + s_scr[rows(i), lr], ar * xi + ai * xr + s_scr[rows(i), li]

        zero = jnp.zeros((SUBLANES, width), F32)
        fr, fi = lax.fori_loop(0, seg, advance, (zero, zero))

        tb = [tab[k, :, lr] for k in range(8)]
        for lvl, k in enumerate((1, 2, 4)):
            sh = (SUBLANES - k) if reverse else k
            pr, pi = tb[2 * lvl], tb[2 * lvl + 1]
            rr, ri = pltpu.roll(fr, sh, 0), pltpu.roll(fi, sh, 0)
            fr, fi = fr + pr * rr - pi * ri, fi + pr * ri + pi * rr
        cr, ci = carry[0:1, lr], carry[0:1, li]
        fr, fi = fr + tb[6] * cr - tb[7] * ci, fi + tb[6] * ci + tb[7] * cr
        last = 0 if reverse else SUBLANES - 1
        carry[0:1, lr] = fr[last:last + 1, :]
        carry[0:1, li] = fi[last:last + 1, :]
        sh = (SUBLANES - 1) if reverse else 1
        entering = (jnp.where(first, cr, pltpu.roll(fr, sh, 0)), jnp.where(first, ci, pltpu.roll(fi, sh, 0)))

        def rerun(i, x, lr=lr, li=li, advance=advance):
            xr, xi = advance(i, x)
            s_scr[rows(i), lr] = xr
            s_scr[rows(i), li] = xi
            return xr, xi

        lax.fori_loop(0, seg, rerun, entering)


def _s5_scratch(tm):
    return [pltpu.VMEM((tm, 2 * S5_N), F32), pltpu.VMEM((SUBLANES, S5_N), F32),
            pltpu.VMEM((8, SUBLANES, S5_N), F32), pltpu.VMEM((SUBLANES, 2 * S5_N), F32)]


def _s5_put(scr, j, val):
    scr[:, j * S5_BS:(j + 1) * S5_BS] = val[:, :S5_BS]
    scr[:, S5_N + j * S5_BS:S5_N + (j + 1) * S5_BS] = val[:, S5_BS:]


def _s5_get(scr, j):
    return jnp.concatenate([scr[:, j * S5_BS:(j + 1) * S5_BS], scr[:, S5_N + j * S5_BS:S5_N + (j + 1) * S5_BS]], axis=1)


def s5_fwd(u5, bbc, ccc, ab8, d_row, wglu, bglu_row, *, tm, name, comm=None):
    t, w = u5.shape

    def body(u_ref, bb_ref, cc_ref, ab_ref, d_ref, wg_ref, bg_ref, s_ref, ys_ref, yb_ref, s_scr, base, tab, carry):
        i = pl.program_id(0)

        @pl.when(i == 0)
        def _():
            carry[...] = jnp.zeros_like(carry)
            _s5_scan_setup(ab_ref, base, tab, tm // SUBLANES, False)

        ub = u_ref[...]
        u = ub.astype(F32)
        for j in range(S5_BLOCKS):
            uj = ub[:, j * S5_BC:(j + 1) * S5_BC]
            _s5_put(s_scr, j, _dot(uj, bb_ref[j * S5_BC:(j + 1) * S5_BC, :]))
        _s5_scan(s_scr, base, tab, carry, tm, False)
        s_ref[...] = s_scr[...]
        ys = []
        for j in range(S5_BLOCKS):
            ys.append(_dot(_s5_get(s_scr, j).astype(BF16), cc_ref[:, j * S5_BC:(j + 1) * S5_BC]))
        y = jnp.concatenate(ys, axis=1) + d_ref[...] * u
        ys_ref[...] = y
        yb1 = _gelu(y)
        tt = _dot(yb1.astype(BF16), wg_ref[...]) + bg_ref[...]
        yb_ref[...] = (yb1 * _sigmoid(tt)).astype(BF16)

    tile = lambda c: pl.BlockSpec((tm, c), lambda i: (i, 0))
    return _tiled_call(
        body, name=name, grid=(t // tm,),
        in_specs=[tile(w), _full((w, 2 * S5_BS)), _full((2 * S5_BS, w)), _full((SUBLANES, S5_N)), _full((1, w)),
                  _full((w, w)), _full((1, w))],
        out_specs=[tile(2 * S5_N), tile(w), tile(w)],
        out_shape=[jax.ShapeDtypeStruct((t, 2 * S5_N), F32), jax.ShapeDtypeStruct((t, w), F32),
                   jax.ShapeDtypeStruct((t, w), BF16)],
        scratch_shapes=_s5_scratch(tm),
        operands=(u5, bbc, ccc, ab8, d_row, wglu, bglu_row), comm=comm)


def s5_bwd(dyb, ys5, u5, s, bbc, ccc, ab8, d_row, wglu, bglu_row, *, tm, name, comm=None):
    t, w = u5.shape
    nt = t // tm
    per = tm // SUBLANES

    def body(dyb_ref, ys_ref, u_ref, s_ref, sprev_ref, bb_ref, cc_ref, ab_ref, d_ref, wg_ref, bg_ref,
             du_ref, dbb_ref, dcc_ref, dab_ref, dd_ref, dwg_ref, dbg_ref, g_scr, base, tab, carry):
        i = pl.program_id(0)

        @pl.when(i == 0)
        def _():
            carry[...] = jnp.zeros_like(carry)
            _s5_scan_setup(ab_ref, base, tab, tm // SUBLANES, True)
            for r in (dbb_ref, dcc_ref, dab_ref, dd_ref, dwg_ref, dbg_ref):
                r[...] = jnp.zeros_like(r)

        y = ys_ref[...]
        ub = u_ref[...]
        u = ub.astype(F32)
        yb1 = _gelu(y)
        yb1b = yb1.astype(BF16)
        sg = _sigmoid(_dot(yb1b, wg_ref[...]) + bg_ref[...])
        dybv = dyb_ref[...]
        dtt = dybv * yb1 * sg * (1.0 - sg)
        dttb = dtt.astype(BF16)
        dyb1 = dybv * sg + _dot_nt(dttb, wg_ref[...])
        dwg_ref[...] += _dot_tn(yb1b, dttb)
        dbg_ref[...] += jnp.sum(dtt, axis=0, keepdims=True)
        dy = dyb1 * _gelu_grad(y)
        dd_ref[...] += jnp.sum(dy * u, axis=0, keepdims=True)
        dyh = dy.astype(BF16)
        for j in range(S5_BLOCKS):
            cs = slice(j * S5_BC, (j + 1) * S5_BC)
            _s5_put(g_scr, j, _dot_nt(dyh[:, cs], cc_ref[:, cs]))
        _s5_scan(g_scr, base, tab, carry, tm, True)
        dus = []
        for j in range(S5_BLOCKS):
            cs = slice(j * S5_BC, (j + 1) * S5_BC)
            gb = _s5_get(g_scr, j).astype(BF16)
            dus.append(_dot_nt(gb, bb_ref[cs, :]))
            dbb_ref[cs, :] += _dot_tn(ub[:, cs], gb)
            dcc_ref[:, cs] += _dot_tn(_s5_get(s_ref, j).astype(BF16), dyh[:, cs])
        du_ref[...] = (jnp.concatenate(dus, axis=1) + d_ref[...] * dy).astype(BF16)
        top = lax.broadcasted_iota(jnp.int32, (SUBLANES, S5_BS), 0) == 0

        def corr(g_sl, s_sl):
            before = jnp.where(i == nt - 1, 0.0, sprev_ref[SUBLANES - 1:SUBLANES, s_sl])
            wrap = jnp.where(top, before, pltpu.roll(s_ref[tm - SUBLANES:tm, s_sl], 1, 0))
            return (jnp.sum(g_scr[SUBLANES:tm, g_sl] * s_ref[0:tm - SUBLANES, s_sl], axis=0, keepdims=True)
                    + jnp.sum(g_scr[0:SUBLANES, g_sl] * wrap, axis=0, keepdims=True))

        for j in range(S5_BLOCKS):
            lr = slice(j * S5_BS, (j + 1) * S5_BS)
            li = slice(S5_N + j * S5_BS, S5_N + (j + 1) * S5_BS)
            dab_ref[0:1, lr] += corr(lr, lr) + corr(li, li)
            dab_ref[1:2, lr] += corr(li, lr) - corr(lr, li)

    tile = lambda c: pl.BlockSpec((tm, c), lambda i: (nt - 1 - i, 0))
    acc = lambda r, c: jax.ShapeDtypeStruct((r, c), F32)
    return _tiled_call(
        body, name=name, grid=(nt,),
        in_specs=[tile(w), tile(w), tile(w), tile(2 * S5_N),
                  pl.BlockSpec((SUBLANES, 2 * S5_N), lambda i: (jnp.maximum((nt - 1 - i) * per - 1, 0), 0)),
                  _full((w, 2 * S5_BS)), _full((2 * S5_BS, w)), _full((SUBLANES, S5_N)), _full((1, w)),
                  _full((w, w)), _full((1, w))],
        out_specs=[tile(w), _full((w, 2 * S5_BS)), _full((2 * S5_BS, w)), _full((SUBLANES, S5_N)), _full((1, w)),
                   _full((w, w)), _full((1, w))],
        out_shape=[jax.ShapeDtypeStruct((t, w), BF16), acc(w, 2 * S5_BS), acc(2 * S5_BS, w), acc(SUBLANES, S5_N),
                   acc(1, w), acc(w, w), acc(1, w)],
        scratch_shapes=_s5_scratch(tm),
        operands=(dyb, ys5, u5, s, s, bbc, ccc, ab8, d_row, wglu, bglu_row), comm=comm)


def s5_discretize(a_re, a_im, log_dt, b_re, b_im, *, name):
    n = a_re.shape[0]

    def body(ar_ref, ai_ref, dt_ref, br_ref, bi_ref, ab_ref, bbr_ref, bbi_ref):
        ar, ai, dtv = ar_ref[...], ai_ref[...], jnp.exp(dt_ref[...])
        mag = jnp.exp(ar * dtv)
        abr = mag * jnp.cos(ai * dtv)
        abi = mag * jnp.sin(ai * dtv)
        den = ar * ar + ai * ai
        nr = abr - 1.0
        cr = (nr * ar + abi * ai) / den
        ci = (abi * ar - nr * ai) / den
        ab_ref[:, 0:1] = abr
        ab_ref[:, 1:2] = abi
        br, bi = br_ref[...], bi_ref[...]
        bbr_ref[...] = cr * br - ci * bi
        bbi_ref[...] = cr * bi + ci * br

    col = jax.ShapeDtypeStruct((n, 2), F32)
    mat = jax.ShapeDtypeStruct((n, S5_GROUP), F32)
    return pl.pallas_call(body, name=name, out_shape=[col, mat, mat])(a_re, a_im, log_dt, b_re, b_im)


def s5_discretize_bwd(a_re, a_im, log_dt, b_re, b_im, dab, dbb_re, dbb_im, *, name):
    n = a_re.shape[0]

    def body(ar_ref, ai_ref, dt_ref, br_ref, bi_ref, dab_ref, dbr_ref, dbi_ref, da_ref, ddt_ref, gbr_ref, gbi_ref):
        ar, ai, dtv = ar_ref[...], ai_ref[...], jnp.exp(dt_ref[...])
        mag = jnp.exp(ar * dtv)
        abr = mag * jnp.cos(ai * dtv)
        abi = mag * jnp.sin(ai * dtv)
        den = ar * ar + ai * ai
        nr = abr - 1.0
        cr = (nr * ar + abi * ai) / den
        ci = (abi * ar - nr * ai) / den
        br, bi = br_ref[...], bi_ref[...]
        dbr, dbi = dbr_ref[...], dbi_ref[...]
        gbr_ref[...] = cr * dbr + ci * dbi
        gbi_ref[...] = cr * dbi - ci * dbr
        gcr = jnp.sum(dbr * br + dbi * bi, axis=1, keepdims=True)
        gci = jnp.sum(dbi * br - dbr * bi, axis=1, keepdims=True)
        ilr, ili = ar / den, -ai / den
        gabr = dab_ref[:, 0:1] + (ilr * gcr + ili * gci)
        gabi = dab_ref[:, 1:2] + (ilr * gci - ili * gcr)
        t1r, t1i = dtv * abr, dtv * abi
        t2r, t2i = -(cr * ilr - ci * ili), -(cr * ili + ci * ilr)
        da_ref[:, 0:1] = (t1r * gabr + t1i * gabi) + (t2r * gcr + t2i * gci)
        da_ref[:, 1:2] = (t1r * gabi - t1i * gabr) + (t2r * gci - t2i * gcr)
        lr, li = ar * abr - ai * abi, ar * abi + ai * abr
        dlog = (lr * gabr + li * gabi) * dtv
        ddt_ref[...] = jnp.sum(dlog.reshape(S5_GROUPS, S5_STATE, 1), axis=1)

    col2 = jax.ShapeDtypeStruct((n, 2), F32)
    col1 = jax.ShapeDtypeStruct((S5_GROUPS, 1), F32)
    mat = jax.ShapeDtypeStruct((n, S5_GROUP), F32)
    return pl.pallas_call(body, name=name, out_shape=[col2, col1, mat, mat])(
        a_re, a_im, log_dt, b_re, b_im, dab, dbb_re, dbb_im)


def merge_fwd(ya, yb, graw, x, wb, wout, *, tm, name):
    t, d = x.shape

    def body(ya_ref, yb_ref, g_ref, x_ref, wb_ref, wo_ref, x1_ref, pa_ref, pb_ref):
        pa = _dot(ya_ref[...], wb_ref[0:d, :])
        pb = _dot(yb_ref[...], wb_ref[d:, :])
        gate = _sigmoid(g_ref[...].astype(F32))
        merged = gate[:, :d] * pa + gate[:, d:] * pb
        x1_ref[...] = x_ref[...] + _dot(merged.astype(BF16), wo_ref[...])
        pa_ref[...] = pa.astype(BF16)
        pb_ref[...] = pb.astype(BF16)

    tile = lambda c: pl.BlockSpec((tm, c), lambda i: (i, 0))
    sds = jax.ShapeDtypeStruct
    return pl.pallas_call(
        body, name=name, grid=(t // tm,),
        in_specs=[tile(d), tile(S5_WIDTH), tile(2 * d), tile(d), _resident((d + S5_WIDTH, d)), _resident((d, d))],
        out_specs=[tile(d), tile(d), tile(d)], out_shape=[sds((t, d), F32), sds((t, d), BF16), sds((t, d), BF16)],
        compiler_params=_params("parallel"),
    )(ya, yb, graw, x, wb, wout)


def merge_bwd(dx1b, graw, pa, pb, wb, wout, *, tm, name):
    t, d = pa.shape

    def body(dx_ref, g_ref, pa_ref, pb_ref, wb_ref, wo_ref,
             mg_ref, dg_ref, dpa_ref, dpb_ref, dya_ref, dyb_ref):
        dm = _dot_nt(dx_ref[...], wo_ref[...])
        gate = _sigmoid(g_ref[...].astype(F32))
        g0, g1 = gate[:, :d], gate[:, d:]
        pa, pb = pa_ref[...].astype(F32), pb_ref[...].astype(F32)
        mg_ref[...] = (g0 * pa + g1 * pb).astype(BF16)
        dg_ref[:, :d] = (dm * pa * g0 * (1.0 - g0)).astype(BF16)
        dg_ref[:, d:] = (dm * pb * g1 * (1.0 - g1)).astype(BF16)
        dpa = (dm * g0).astype(BF16)
        dpb = (dm * g1).astype(BF16)
        dpa_ref[...] = dpa
        dpb_ref[...] = dpb
        dya_ref[...] = _dot_nt(dpa, wb_ref[0:d, :])
        dyb_ref[...] = _dot_nt(dpb, wb_ref[d:, :])

    tile = lambda c: pl.BlockSpec((tm, c), lambda i: (i, 0))
    sds = jax.ShapeDtypeStruct
    return pl.pallas_call(
        body, name=name, grid=(t // tm,),
        in_specs=[tile(d), tile(2 * d), tile(d), tile(d), _resident((d + S5_WIDTH, d)), _resident((d, d))],
        out_specs=[tile(d), tile(2 * d), tile(d), tile(d), tile(d), tile(S5_WIDTH)],
        out_shape=[sds((t, d), BF16), sds((t, 2 * d), BF16), sds((t, d), BF16), sds((t, d), BF16),
                   sds((t, d), F32), sds((t, S5_WIDTH), F32)],
        compiler_params=_params("parallel"),
    )(dx1b, graw, pa, pb, wb, wout)


def mlp_out_loss(a1, x1, w2, gf, target, *, tm, name):
    t, d = x1.shape
    f = a1.shape[1]

    def body(a_ref, x_ref, w_ref, g_ref, tg_ref, dx_ref, dxb_ref, loss_ref, dg_ref):
        i = pl.program_id(0)
        av = jnp.maximum(a_ref[...].astype(F32), 0.0)
        x2 = x_ref[...] + _dot((av * av).astype(BF16), w_ref[...])
        r = lax.rsqrt(jnp.mean(x2 * x2, axis=-1, keepdims=True) + EPS)
        xn = x2 * r
        err = xn * g_ref[...] - tg_ref[...]
        dyf = err * (1.0 / d)
        dxn = dyf * g_ref[...]
        dx = r * (dxn - xn * jnp.mean(dxn * xn, axis=-1, keepdims=True))
        dx_ref[...] = dx
        dxb_ref[...] = dx.astype(BF16)

        @pl.when(i == 0)
        def _():
            loss_ref[...] = jnp.zeros_like(loss_ref)
            dg_ref[...] = jnp.zeros_like(dg_ref)

        loss_ref[...] += jnp.sum(err * err) * (0.5 / d)
        dg_ref[...] += jnp.sum(dyf * xn, axis=0, keepdims=True)

    tile = lambda c: pl.BlockSpec((tm, c), lambda i: (i, 0))
    sds = jax.ShapeDtypeStruct
    return pl.pallas_call(
        body, name=name, grid=(t // tm,),
        in_specs=[tile(f), tile(d), _resident((f, d)), _full((1, d)), tile(d)],
        out_specs=[tile(d), tile(d), _full((1, LANES)), _full((1, d))],
        out_shape=[sds((t, d), F32), sds((t, d), BF16), sds((1, LANES), F32), sds((1, d), F32)],
        compiler_params=_params("arbitrary"),
    )(a1, x1, w2, gf.reshape(1, d), target)


def mlp_bwd_act(dx2b, a1, w2, *, tm, name):
    t, f = a1.shape
    d = dx2b.shape[1]

    def body(dx_ref, a_ref, w_ref, o_ref):
        dact = _dot_nt(dx_ref[...], w_ref[...])
        o_ref[...] = (dact * 2.0 * jnp.maximum(a_ref[...].astype(F32), 0.0)).astype(BF16)

    tile = lambda c: pl.BlockSpec((tm, c), lambda i: (i, 0))
    return pl.pallas_call(
        body, name=name, grid=(t // tm,),
        in_specs=[tile(d), tile(f), _resident((f, d))], out_specs=tile(f),
        out_shape=jax.ShapeDtypeStruct((t, f), BF16),
        compiler_params=_params("parallel"),
    )(dx2b, a1, w2)


ADAM_TILE_ELEMS = 128 * 1024


def _row_tile(rows, cap):
    if rows <= cap:
        return rows
    best = None
    for c in range(16, cap + 1, 16):
        if rows % c == 0:
            best = c
    assert best is not None, rows
    return best


def _device_order_sum(parts_ref):
    total = parts_ref[0].astype(F32)
    for k in range(1, N_DEV):
        total = total + parts_ref[k].astype(F32)
    return total


def _adamw_math(w, m, v, gparts_ref):
    g = _device_order_sum(gparts_ref)
    mn = ADAM_B1 * m + (1.0 - ADAM_B1) * g
    vn = ADAM_B2 * v + (1.0 - ADAM_B2) * (g * g)
    m_hat = mn / (1.0 - ADAM_B1 ** ADAM_STEP)
    v_hat = vn / (1.0 - ADAM_B2 ** ADAM_STEP)
    return g, -ADAM_LR * (m_hat / (jnp.sqrt(v_hat) + ADAM_EPS) + ADAM_WD * w), mn, vn


def adamw(w, m, v, gparts, *, name):
    rows, cols = w.shape
    tr = _row_tile(rows, max(16, ADAM_TILE_ELEMS // cols))

    def body(w_ref, m_ref, v_ref, g_ref, go_ref, d_ref, mo_ref, vo_ref):
        go_ref[...], d_ref[...], mo_ref[...], vo_ref[...] = _adamw_math(w_ref[...], m_ref[...], v_ref[...], g_ref)

    tile = pl.BlockSpec((tr, cols), lambda i: (i, 0))
    out = jax.ShapeDtypeStruct((rows, cols), F32)
    return pl.pallas_call(
        body, name=name, grid=(rows // tr,),
        in_specs=[tile, tile, tile, pl.BlockSpec((N_DEV, tr, cols), lambda i: (0, i, 0))],
        out_specs=[tile, tile, tile, tile], out_shape=[out, out, out, out],
        compiler_params=_params("parallel"),
    )(w, m, v, gparts)


def all_gather_arrays(blocks, *, name):
    na = len(blocks)

    def body(*refs):
        x_refs, out_refs = refs[:na], refs[na:2 * na]
        send_sems, recv_sems, local_sems = refs[2 * na:]
        x, y, c, _ = _place()
        me, sibling = (x, y, c), (x, y, 1 - c)
        chips = [(1 - x, y), (x, 1 - y), (1 - x, 1 - y)]

        def copy(a, k, blk, to, own=False):
            px, py, pc = blk
            dst = out_refs[a].at[4 * px + 2 * py + pc]
            return pltpu.make_async_remote_copy(
                src_ref=x_refs[a] if own else dst, dst_ref=dst,
                send_sem=send_sems.at[7 * a + k], recv_sem=recv_sems.at[7 * a + k], device_id=to, device_id_type=_MESH)

        mine = [pltpu.make_async_copy(x_refs[a], out_refs[a].at[4 * x + 2 * y + c], local_sems.at[a]) for a in range(na)]
        for cp in mine:
            cp.start()
        sent = []
        for a in range(na):
            first = [copy(a, 0, me, sibling, own=True)]
            first += [copy(a, 1 + j, me, (*chip, c), own=True) for j, chip in enumerate(chips)]
            for cp in first:
                cp.start()
            sent += first
        for a in range(na):
            for j, chip in enumerate(chips):
                copy(a, 1 + j, (*chip, c), me).wait_recv()
                fwd = copy(a, 4 + j, (*chip, c), sibling)
                fwd.start()
                sent.append(fwd)
        for a in range(na):
            copy(a, 0, sibling, me).wait_recv()
            for j, chip in enumerate(chips):
                copy(a, 4 + j, (*chip, 1 - c), me).wait_recv()
        for cp in sent:
            cp.wait_send()
        for cp in mine:
            cp.wait()

    any_spec = pl.BlockSpec(memory_space=pl.ANY)
    return pl.pallas_call(
        body, name=name, in_specs=[any_spec] * na, out_specs=[any_spec] * na,
        out_shape=[jax.ShapeDtypeStruct((N_DEV,) + b.shape, b.dtype) for b in blocks],
        scratch_shapes=[pltpu.SemaphoreType.DMA((7 * na,)), pltpu.SemaphoreType.DMA((7 * na,)),
                        pltpu.SemaphoreType.DMA((na,))],
        compiler_params=pltpu.CompilerParams(has_side_effects=True),
    )(*blocks)


def all_to_all(comm, *, name):
    na = len(comm)
    kinds = [k for k, _ in comm]

    def body(*refs):
        local, remote = _comm_copies(kinds, refs[:na], refs[na:2 * na], *refs[2 * na:])
        for cp in local + remote:
            cp.start()
        for cp in remote:
            cp.wait_recv()
        for cp in remote:
            cp.wait_send()
        for cp in local:
            cp.wait()

    any_spec = pl.BlockSpec(memory_space=pl.ANY)
    return pl.pallas_call(
        body, name=name, in_specs=[any_spec] * na, out_specs=[any_spec] * na,
        out_shape=_comm_out_shapes(comm),
        scratch_shapes=[pltpu.SemaphoreType.DMA((na * N_REL,)), pltpu.SemaphoreType.DMA((na * N_REL,)),
                        pltpu.SemaphoreType.DMA((na,))],
        compiler_params=pltpu.CompilerParams(has_side_effects=True),
    )(*[a for _, a in comm])


def adamw_whole(ws, ms, vs, gparts, *, name, totals=()):
    n, nt = len(ws), len(totals)

    def body(*refs):
        w_refs, m_refs, v_refs, g_refs = refs[:n], refs[n:2 * n], refs[2 * n:3 * n], refs[3 * n:4 * n]
        t_refs, refs = refs[4 * n:4 * n + nt], refs[4 * n + nt:]
        outs, t_outs = refs[:4 * n], refs[4 * n:]
        for k in range(n):
            g, delta, mn, vn = _adamw_math(w_refs[k][...], m_refs[k][...], v_refs[k][...], g_refs[k])
            outs[k][...] = g
            outs[n + k][...] = delta
            outs[2 * n + k][...] = mn
            outs[3 * n + k][...] = vn
        for t_ref, o_ref in zip(t_refs, t_outs):
            o_ref[...] = _device_order_sum(t_ref)

    shapes = [jax.ShapeDtypeStruct(w.shape, F32) for w in ws]
    res = pl.pallas_call(
        body, name=name, out_shape=shapes * 4 + [jax.ShapeDtypeStruct(t.shape[1:], F32) for t in totals],
        compiler_params=pltpu.CompilerParams(vmem_limit_bytes=VMEM_LIMIT),
    )(*ws, *ms, *vs, *gparts, *totals)
    return [res[j * n:(j + 1) * n] for j in range(4)], res[4 * n:]


def _lane_row(v):
    return jnp.pad(v.astype(F32), (0, LANES - v.shape[0])).reshape(1, LANES)


def _block_diag_b(bb):
    eye = jnp.eye(SUBLANES, dtype=F32)
    bt = bb.reshape(S5_BLOCKS, 8, S5_STATE, S5_GROUP).transpose(0, 1, 3, 2)
    return (bt[:, :, :, None, :] * eye[None, :, None, :, None]).reshape(S5_WIDTH, S5_BS)


def _block_diag_c(cc):
    eye = jnp.eye(SUBLANES, dtype=F32)
    ct = cc.reshape(S5_BLOCKS, 8, S5_GROUP, S5_STATE).transpose(3, 0, 1, 2)
    return (ct[None, :, :, :, :] * eye[:, None, None, :, None]).reshape(S5_BS, S5_WIDTH)


def _diag_blocks_b(m):
    eye = jnp.eye(SUBLANES, dtype=F32)
    m5 = m.reshape(S5_BLOCKS, 8, S5_GROUP, 8, S5_STATE)
    return jnp.sum(m5 * eye[None, :, None, :, None], axis=3).transpose(0, 1, 3, 2).reshape(S5_GROUPS, S5_STATE, S5_GROUP)


def _diag_blocks_c(m):
    eye = jnp.eye(SUBLANES, dtype=F32)
    m5 = m.reshape(8, S5_STATE, S5_BLOCKS, 8, S5_GROUP)
    return jnp.sum(m5 * eye[:, None, None, :, None], axis=0).transpose(1, 2, 3, 0).reshape(S5_GROUPS, S5_GROUP, S5_STATE)


def train_step(x, target, p, m, v):
    t = x.shape[0]
    tm = min(256, t)
    tb = min(512, t)
    ts = min(512, t)
    bf = lambda n: p[n].astype(BF16)
    slots = lambda n, a: a.reshape((N_DEV,) + _shard_shape(n))
    updated = {}

    def update(names, got):
        for n, parts in zip(names, got):
            updated[n] = adamw(p[n], m[n], v[n], parts, name=f"adamw_{n}")

    st_in, st_conv = all_gather_arrays([bf("w_in"), p["conv_w"]], name="gather_first")
    win_p = jnp.concatenate([st_in[k][:, a:b] for seg in IN_PADDED_ORDER for k, a, b in _shard_pieces(*IN_SEGMENTS[seg])]
                            + [jnp.zeros((D_MODEL, LANES - SSD_HEADS), BF16)], axis=1)
    conv_w8 = jnp.pad(_join_shards("conv_w", st_conv), ((0, SUBLANES - SSD_CONV), (0, 0)))
    (h, xbc_raw, graw, z, u5, dtr), (st_branch, st_out, st_glu) = rms_matmul(
        x, p["norm_mix_g"], win_p, IN_SPLITS, tm=tb, name="in_proj", out_dtypes=(BF16, BF16, BF16, BF16, F32),
        comm=gathers([bf("w_branch"), bf("w_out"), bf("s5_glu_w")]))
    w_branch = st_branch.reshape(D_MODEL + S5_WIDTH, D_MODEL)
    wout, wglu = st_out.reshape(D_MODEL, D_MODEL), st_glu.reshape(S5_WIDTH, S5_WIDTH)
    bias_row, alog_row = _lane_row(p["dt_bias"]), _lane_row(p["a_log"])
    d_row = jnp.repeat(p["d_ssd"], SSD_HEADDIM).reshape(1, SSD_HEADS * SSD_HEADDIM)
    (ya, yssd, states), (w1,) = ssd_fwd(xbc_raw, conv_w8, p["conv_b"], dtr, z, bias_row, alog_row, d_row,
                                        p["ssd_norm_g"], name="ssd_fwd", comm=gathers([bf("w_mlp_in")]))

    n = S5_N
    a_re_c, a_im_c = p["s5_a_re"].reshape(n, 1), p["s5_a_im"].reshape(n, 1)
    dt_c = jnp.repeat(p["s5_log_dt"], S5_STATE).reshape(n, 1)
    b_re_m, b_im_m = p["s5_b_re"].reshape(n, S5_GROUP), p["s5_b_im"].reshape(n, S5_GROUP)
    ab, bb_re, bb_im = s5_discretize(a_re_c, a_im_c, dt_c, b_re_m, b_im_m, name="s5_disc")
    ab8 = jnp.pad(ab.T, ((0, SUBLANES - 2), (0, 0)))
    bbc = jnp.concatenate([_block_diag_b(bb_re.reshape(S5_GROUPS, S5_STATE, S5_GROUP)),
                           _block_diag_b(bb_im.reshape(S5_GROUPS, S5_STATE, S5_GROUP))], axis=1).astype(BF16)
    ccc = jnp.concatenate([_block_diag_c(p["s5_c_re"]), -_block_diag_c(p["s5_c_im"])], axis=0).astype(BF16)
    s5d_row = p["s5_d"].reshape(1, S5_WIDTH)
    bglu_row = p["s5_glu_b"].reshape(1, S5_WIDTH)
    u5 = s5_interleave(u5, ts)
    (s, ys5, yb), (st_w2,) = s5_fwd(u5, bbc, ccc, ab8, s5d_row, wglu, bglu_row, tm=ts, name="s5_fwd",
                                    comm=gathers([bf("w_mlp_out")]))
    yb = s5_deinterleave(yb, ts)
    w2 = st_w2.reshape(D_FF, D_MODEL)

    x1, pa, pb = merge_fwd(ya, yb, graw, x, w_branch, wout, tm=tb, name="merge_fwd")
    (h2, a1), _ = rms_matmul(x1, p["norm_mlp_g"], w1, (D_FF,), tm=tb, name="mlp_in", out_dtypes=(BF16,))
    dx2, dx2b, loss_row, dgf = mlp_out_loss(a1, x1, w2, p["norm_final_g"], target, tm=tb, name="mlp_out_loss")

    g = {}
    g["norm_final_g"] = dgf[0]
    da1 = mlp_bwd_act(dx2b, a1, w2, tm=tb, name="mlp_bwd_act")
    dw2 = slots("w_mlp_out", matmul_tn(a1, dx2b, out_dtype=BF16, a_relu2=True, name="dw_mlp_out"))
    dw1 = matmul_tn(h2, da1, out_dtype=BF16, col_shards=N_DEV, name="dw_mlp_in")
    (dx1, dx1b, dgm), _ = matmul_nt_rms_bwd([da1], w1, x1, p["norm_mlp_g"], dx2, tm=tb, name="mlp_in_bwd")
    g["norm_mlp_g"] = dgm[0]
    merged, dgraw, dpa, dpb, dya, dyb = merge_bwd(dx1b, graw, pa, pb, w_branch, wout, tm=tm, name="merge_bwd")
    dwo = slots("w_out", matmul_tn(merged, dx1b, out_dtype=BF16, name="dw_out"))
    dwb = slots("w_branch", jnp.concatenate([matmul_tn(ya, dpa, out_dtype=BF16, name="dw_branch_a"),
                                             matmul_tn(yb, dpb, out_dtype=BF16, name="dw_branch_b")], axis=0))

    (du5, dbbc, dccc, dab, dd5, dwglu, dbglu), got = s5_bwd(
        s5_interleave(dyb, ts), ys5, u5, s, bbc, ccc, ab8, s5d_row, wglu, bglu_row, tm=ts, name="s5_bwd",
        comm=exchanges([dw2, dw1]))
    du5 = s5_deinterleave(du5, ts)
    update(["w_mlp_out", "w_mlp_in"], got)
    g["s5_glu_b"], g["s5_d"] = dbglu[0], dd5[0]
    g["s5_c_re"] = _diag_blocks_c(dccc[:S5_BS])
    g["s5_c_im"] = -_diag_blocks_c(dccc[S5_BS:])
    dbb_re = _diag_blocks_b(dbbc[:, :S5_BS]).reshape(n, S5_GROUP)
    dbb_im = _diag_blocks_b(dbbc[:, S5_BS:]).reshape(n, S5_GROUP)
    da, dlogdt, gb_re, gb_im = s5_discretize_bwd(a_re_c, a_im_c, dt_c, b_re_m, b_im_m, dab[:2].T, dbb_re, dbb_im,
                                                  name="s5_disc_bwd")
    g["s5_a_re"] = da[:, 0].reshape(S5_GROUPS, S5_STATE)
    g["s5_a_im"] = da[:, 1].reshape(S5_GROUPS, S5_STATE)
    g["s5_log_dt"] = dlogdt[:, 0]
    g["s5_b_re"] = gb_re.reshape(S5_GROUPS, S5_STATE, S5_GROUP)
    g["s5_b_im"] = gb_im.reshape(S5_GROUPS, S5_STATE, S5_GROUP)

    def update_small(names, got, name, totals=()):
        (gs, ds, nm, nv), sums = adamw_whole([p[n] for n in names], [m[n] for n in names], [v[n] for n in names], got,
                                             name=name, totals=totals)
        for k, n in enumerate(names):
            updated[n] = (gs[k], ds[k], nm[k], nv[k])
        return sums

    wide = ["s5_b_re", "s5_b_im"]
    late = ["ssd_norm_g", "d_ssd", "a_log", "dt_bias", "conv_b"]
    early = [n for n in SMALL_ORDER if n not in wide + late and n != "norm_mix_g"]
    (dz, dxbc_raw, ddtr, dng, dd_row, dalog_row, dbias_row, dconv_b, dconv_w8), got = ssd_bwd(
        dya, yssd, z, xbc_raw, conv_w8, p["conv_b"], dtr, states, bias_row, alog_row, d_row, p["ssd_norm_g"],
        name="ssd_bwd",
        comm=exchanges([dwo, dwb, slots("s5_glu_w", dwglu.astype(BF16))])
        + gathers([g[n] for n in wide + early] + [loss_row]))
    update(["w_out", "w_branch", "s5_glu_w"], got[:3])
    update_small(wide, got[3:3 + len(wide)], "adamw_s5_b")
    loss_sum, = update_small(early, got[3 + len(wide):-1], "adamw_small_early", totals=[got[-1]])
    g["ssd_norm_g"] = dng[0]
    g["d_ssd"], g["a_log"], g["dt_bias"] = dd_row[0, :SSD_HEADS], dalog_row[0, :SSD_HEADS], dbias_row[0, :SSD_HEADS]
    g["conv_b"] = dconv_b[0]
    dconv = dconv_w8[:SSD_CONV].reshape(SSD_CONV, N_DEV, CONV_DIM // N_DEV).transpose(1, 0, 2)
    dproj =[dxbc_raw, dgraw, dz, du5, ddtr]
    dxbc_w, dgate_w, dz_w, du5_w, ddt_w = [matmul_tn(h, piece, out_dtype=BF16, name=f"dw_in_{k}")
                                           for k, piece in enumerate(dproj)]
    by_segment = {"z": dz_w, "xbc": dxbc_w, "dt": ddt_w, "u5": du5_w, "gates": dgate_w}
    width = D_IN_PROJ // N_DEV
    dw_in = jnp.stack([jnp.concatenate(
        [by_segment[seg][:, max(k * width, lo) - lo:min((k + 1) * width, hi) - lo]
         for seg, (lo, hi) in IN_SEGMENTS.items() if max(k * width, lo) < min((k + 1) * width, hi)], axis=1)
        for k in range(N_DEV)])

    (grad_x, _, dgx), got = matmul_nt_rms_bwd(
        dproj, win_p, x, p["norm_mix_g"], dx1, tm=tm, name="in_proj_bwd",
        comm=exchanges([dw_in, dconv]) + gathers([g[n] for n in late]))
    update(["w_in", "conv_w"], got[:2])
    update_small(late, got[2:], "adamw_small_late")
    update_small(["norm_mix_g"], all_to_all(gathers([dgx[0]]), name="gather_last"), "adamw_small_last")
    return loss_sum[0, 0], grad_x, updated


WEIGHT_ORDER = ["norm_mix_g", "w_in", "conv_w", "conv_b", "dt_bias", "a_log", "d_ssd", "ssd_norm_g", "s5_a_re",
                "s5_a_im", "s5_log_dt", "s5_b_re", "s5_b_im", "s5_c_re", "s5_c_im", "s5_d", "s5_glu_w", "s5_glu_b",
                "w_branch", "w_out", "norm_mlp_g", "w_mlp_in", "w_mlp_out", "norm_final_g"]
SHARDED = {"w_in": ((D_MODEL, D_IN_PROJ), 1), "conv_w": ((SSD_CONV, CONV_DIM), 1), "s5_glu_w": ((S5_WIDTH, S5_WIDTH), 0),
           "w_branch": ((D_MODEL + S5_WIDTH, D_MODEL), 0), "w_out": ((D_MODEL, D_MODEL), 0),
           "w_mlp_in": ((D_MODEL, D_FF), 1), "w_mlp_out": ((D_FF, D_MODEL), 0)}
SHARDED_ORDER = ["w_in", "conv_w", "s5_glu_w", "w_branch", "w_out", "w_mlp_in", "w_mlp_out"]
SMALL_ORDER = [n for n in WEIGHT_ORDER if n not in SHARDED]


def _shard_shape(name):
    (r, c), ax = SHARDED[name]
    return (r, c // N_DEV) if ax == 1 else (r // N_DEV, c)


def _join_shards(name, stacked):
    (r, c), ax = SHARDED[name]
    if ax == 1:
        return stacked.transpose(1, 0, 2).reshape(r, c)
    return stacked.reshape(r, c)


def _shard_pieces(lo, hi):
    width = D_IN_PROJ // N_DEV
    out = []
    while lo < hi:
        k = lo // width
        end = min(hi, (k + 1) * width)
        out.append((k, lo - k * width, end - k * width))
        lo = end
    return out


IN_SEGMENTS = {"z": (0, 1024), "xbc": (1024, 3072), "dt": (3072, 3088), "u5": (3088, 3600), "gates": (3600, 5648)}
IN_PADDED_ORDER = ("xbc", "gates", "z", "u5", "dt")


def kernel(x, norm_mix_g, w_in, conv_w, conv_b, dt_bias, a_log, d_ssd, ssd_norm_g, s5_a_re, s5_a_im, s5_log_dt, s5_b_re, s5_b_im, s5_c_re, s5_c_im, s5_d, s5_glu_w, s5_glu_b, w_branch, w_out, norm_mlp_g, w_mlp_in, w_mlp_out, norm_final_g, loss_target, m_norm_mix_g, m_w_in, m_conv_w, m_conv_b, m_dt_bias, m_a_log, m_d_ssd, m_ssd_norm_g, m_s5_a_re, m_s5_a_im, m_s5_log_dt, m_s5_b_re, m_s5_b_im, m_s5_c_re, m_s5_c_im, m_s5_d, m_s5_glu_w, m_s5_glu_b, m_w_branch, m_w_out, m_norm_mlp_g, m_w_mlp_in, m_w_mlp_out, m_norm_final_g, v_norm_mix_g, v_w_in, v_conv_w, v_conv_b, v_dt_bias, v_a_log, v_d_ssd, v_ssd_norm_g, v_s5_a_re, v_s5_a_im, v_s5_log_dt, v_s5_b_re, v_s5_b_im, v_s5_c_re, v_s5_c_im, v_s5_d, v_s5_glu_w, v_s5_glu_b, v_w_branch, v_w_out, v_norm_mlp_g, v_w_mlp_in, v_w_mlp_out, v_norm_final_g):
    w = dict(norm_mix_g=norm_mix_g, w_in=w_in, conv_w=conv_w, conv_b=conv_b, dt_bias=dt_bias, a_log=a_log, d_ssd=d_ssd,
             ssd_norm_g=ssd_norm_g, s5_a_re=s5_a_re, s5_a_im=s5_a_im, s5_log_dt=s5_log_dt, s5_b_re=s5_b_re,
             s5_b_im=s5_b_im, s5_c_re=s5_c_re, s5_c_im=s5_c_im, s5_d=s5_d, s5_glu_w=s5_glu_w, s5_glu_b=s5_glu_b,
             w_branch=w_branch, w_out=w_out, norm_mlp_g=norm_mlp_g, w_mlp_in=w_mlp_in, w_mlp_out=w_mlp_out,
             norm_final_g=norm_final_g)
    m = dict(norm_mix_g=m_norm_mix_g, w_in=m_w_in, conv_w=m_conv_w, conv_b=m_conv_b, dt_bias=m_dt_bias, a_log=m_a_log,
             d_ssd=m_d_ssd, ssd_norm_g=m_ssd_norm_g, s5_a_re=m_s5_a_re, s5_a_im=m_s5_a_im, s5_log_dt=m_s5_log_dt,
             s5_b_re=m_s5_b_re, s5_b_im=m_s5_b_im, s5_c_re=m_s5_c_re, s5_c_im=m_s5_c_im, s5_d=m_s5_d,
             s5_glu_w=m_s5_glu_w, s5_glu_b=m_s5_glu_b, w_branch=m_w_branch, w_out=m_w_out, norm_mlp_g=m_norm_mlp_g,
             w_mlp_in=m_w_mlp_in, w_mlp_out=m_w_mlp_out, norm_final_g=m_norm_final_g)
    v = dict(norm_mix_g=v_norm_mix_g, w_in=v_w_in, conv_w=v_conv_w, conv_b=v_conv_b, dt_bias=v_dt_bias, a_log=v_a_log,
             d_ssd=v_d_ssd, ssd_norm_g=v_ssd_norm_g, s5_a_re=v_s5_a_re, s5_a_im=v_s5_a_im, s5_log_dt=v_s5_log_dt,
             s5_b_re=v_s5_b_re, s5_b_im=v_s5_b_im, s5_c_re=v_s5_c_re, s5_c_im=v_s5_c_im, s5_d=v_s5_d,
             s5_glu_w=v_s5_glu_w, s5_glu_b=v_s5_glu_b, w_branch=v_w_branch, w_out=v_w_out, norm_mlp_g=v_norm_mlp_g,
             w_mlp_in=v_w_mlp_in, w_mlp_out=v_w_mlp_out, norm_final_g=v_norm_final_g)

    loss, grad_x, upd = train_step(x[0], loss_target[0], w, m, v)
    return (loss, grad_x.reshape(x.shape), *[upd[n][j] for j in range(4) for n in WEIGHT_ORDER])
```

```python
import functools
import math

import jax
import jax.numpy as jnp
from jax import lax
from jax.experimental import pallas as pl
from jax.experimental.pallas import tpu as pltpu

F32 = jnp.float32
BF16 = jnp.bfloat16
HI = lax.Precision.HIGHEST

N_DEV = 8
D_MODEL = 1024
SSD_HEADS = 16
SSD_HEADDIM = 64
SSD_GROUPS = 4
SSD_HPG = 4
SSD_STATE = 128
SSD_CHUNK = 128
SSD_CONV = 4
CONV_DIM = 2048
S5_WIDTH = 512
S5_GROUP = 16
S5_GROUPS = 32
S5_STATE = 64
S5_N = S5_GROUPS * S5_STATE
D_FF = 4096
D_IN_PROJ = 5648
IN_SPLITS = (2048, 2048, 1024, 512, 128)
D_IN_PAD = sum(IN_SPLITS)
EPS = 1e-6
LANES = 128
SUBLANES = 8
VMEM_LIMIT = 56 * 1024 * 1024

ADAM_LR = 0.001
ADAM_B1 = 0.9
ADAM_B2 = 0.999
ADAM_EPS = 1e-08
ADAM_WD = 0.01
ADAM_STEP = 10

GELU_C = math.sqrt(2.0 / math.pi)
GELU_K = 0.044715


def _params(*sem):
    return pltpu.CompilerParams(dimension_semantics=sem, vmem_limit_bytes=VMEM_LIMIT)


def _full(shape):
    nd = len(shape)
    return pl.BlockSpec(shape, lambda *_: (0,) * nd)


def _resident(shape):
    nd = len(shape)
    return pl.BlockSpec(shape, lambda *_: (0,) * nd, pipeline_mode=pl.Buffered(1))


_MESH = pl.DeviceIdType.MESH
_RELATIONS = [(dx, dy, dc) for dx in (0, 1) for dy in (0, 1) for dc in (0, 1)][1:]
N_REL = len(_RELATIONS)


def _place():
    x, y, c = lax.axis_index("x"), lax.axis_index("y"), lax.axis_index("c")
    return x, y, c, 4 * x + 2 * y + c


def gathers(arrays):
    return [("gather", a) for a in arrays]


def exchanges(arrays):
    return [("exchange", a) for a in arrays]


def _comm_out_shapes(comm):
    return [jax.ShapeDtypeStruct(((N_DEV,) if kind == "gather" else ()) + a.shape, a.dtype) for kind, a in comm]


def _comm_copies(kinds, src_refs, dst_refs, send_sems, recv_sems, local_sems):
    x, y, c, idx = _place()
    local, remote = [], []
    for a, (kind, src, dst) in enumerate(zip(kinds, src_refs, dst_refs)):
        local.append(pltpu.make_async_copy(src if kind == "gather" else src.at[idx], dst.at[idx], local_sems.at[a]))
        for k, (dx, dy, dc) in enumerate(_RELATIONS):
            px, py, pc = x ^ dx, y ^ dy, c ^ dc
            remote.append(pltpu.make_async_remote_copy(
                src_ref=src if kind == "gather" else src.at[4 * px + 2 * py + pc], dst_ref=dst.at[idx],
                send_sem=send_sems.at[N_REL * a + k], recv_sem=recv_sems.at[N_REL * a + k],
                device_id=(px, py, pc), device_id_type=_MESH))
    return local, remote


def _tiled_call(body, *, name, grid, in_specs, out_specs, out_shape, operands, scratch_shapes=(), comm=None):
    params = pltpu.CompilerParams(dimension_semantics=("arbitrary",), vmem_limit_bytes=VMEM_LIMIT,
                                  has_side_effects=bool(comm))
    if not comm:
        outs = pl.pallas_call(body, name=name, grid=grid, in_specs=in_specs, out_specs=out_specs, out_shape=out_shape,
                              scratch_shapes=list(scratch_shapes), compiler_params=params)(*operands)
        return outs, []
    kind = [k for k, _ in comm]
    arrays = [a for _, a in comm]
    na, n_in, n_out, n_scr = len(arrays), len(in_specs), len(out_specs), len(scratch_shapes)
    last = grid[0] - 1

    def hosted(*refs):
        ins, refs = refs[:n_in], refs[n_in:]
        cin, refs = refs[:na], refs[na:]
        outs, refs = refs[:n_out], refs[n_out:]
        cout, refs = refs[:na], refs[na:]
        scr, sems = refs[:n_scr], refs[n_scr:]
        i = pl.program_id(0)

        @pl.when(i == 0)
        def _():
            local, remote = _comm_copies(kind, cin, cout, *sems)
            for cp in local + remote:
                cp.start()

        body(*ins, *outs, *scr)

        @pl.when(i == last)
        def _():
            local, remote = _comm_copies(kind, cin, cout, *sems)
            for cp in remote:
                cp.wait_recv()
            for cp in remote:
                cp.wait_send()
            for cp in local:
                cp.wait()

    any_spec = pl.BlockSpec(memory_space=pl.ANY)
    res = pl.pallas_call(
        hosted, name=name, grid=grid,
        in_specs=list(in_specs) + [any_spec] * na, out_specs=list(out_specs) + [any_spec] * na,
        out_shape=list(out_shape) + _comm_out_shapes(comm),
        scratch_shapes=list(scratch_shapes) + [pltpu.SemaphoreType.DMA((N_REL * na,)), pltpu.SemaphoreType.DMA((N_REL * na,)),
                                               pltpu.SemaphoreType.DMA((na,))],
        compiler_params=params)(*operands, *arrays)
    return res[:n_out], res[n_out:]


def _dot(a, b):
    return jnp.dot(a, b, preferred_element_type=F32)


def _dot_nt(a, b):
    return lax.dot_general(a, b, (((1,), (1,)), ((), ())), preferred_element_type=F32)


def _dot_tn(a, b):
    return lax.dot_general(a, b, (((0,), (0,)), ((), ())), preferred_element_type=F32)


def _dot_hi(a, b):
    return jnp.dot(a, b, preferred_element_type=F32, precision=HI)


def _dot_mid(a, b):
    return jnp.dot(a, b, preferred_element_type=F32, precision=lax.Precision.HIGH)


def _dot_nt_hi(a, b):
    return lax.dot_general(a, b, (((1,), (1,)), ((), ())), preferred_element_type=F32, precision=HI)


def _dot_tn_hi(a, b):
    return lax.dot_general(a, b, (((0,), (0,)), ((), ())), preferred_element_type=F32, precision=HI)


def _sigmoid(x):
    return 1.0 / (1.0 + jnp.exp(-x))


def _softplus(x):
    return jnp.maximum(x, 0.0) + jnp.log(1.0 + jnp.exp(-jnp.abs(x)))


def _gelu(x):
    return 0.5 * x * (1.0 + jnp.tanh(GELU_C * (x + GELU_K * x * x * x)))


def _gelu_grad(x):
    th = jnp.tanh(GELU_C * (x + GELU_K * x * x * x))
    return 0.5 * (1.0 + th) + 0.5 * x * (1.0 - th * th) * GELU_C * (1.0 + 3.0 * GELU_K * x * x)


def rms_matmul(x, g, w, splits, *, tm, name, comm=None, out_dtypes=None):
    t, d = x.shape
    stacked = w.ndim == 3
    n = w.shape[0] * w.shape[2] if stacked else w.shape[1]
    assert sum(splits) == n and (len(splits) == 1 or not stacked)

    def body(x_ref, g_ref, w_ref, h_ref, *o_refs):
        xv = x_ref[...]
        r = lax.rsqrt(jnp.mean(xv * xv, axis=-1, keepdims=True) + EPS)
        h = (xv * r * g_ref[...]).astype(BF16)
        h_ref[...] = h
        if stacked:
            wk = w.shape[2]
            for k in range(w.shape[0]):
                o_refs[0][:, k * wk:(k + 1) * wk] = _dot(h, w_ref[k]).astype(o_refs[0].dtype)
            return
        off = 0
        for o_ref, width in zip(o_refs, splits):
            o_ref[...] = _dot(h, w_ref[:, off:off + width]).astype(o_ref.dtype)
            off += width

    tile = lambda c: pl.BlockSpec((tm, c), lambda i: (i, 0))
    dtypes = out_dtypes or (F32,) * len(splits)
    return _tiled_call(
        body, name=name, grid=(t // tm,),
        in_specs=[tile(d), _full((1, d)), _resident(w.shape)],
        out_specs=[tile(d)] + [tile(c) for c in splits],
        out_shape=[jax.ShapeDtypeStruct((t, d), BF16)] + [jax.ShapeDtypeStruct((t, c), dt) for c, dt in zip(splits, dtypes)],
        operands=(x, g.reshape(1, d), w), comm=comm)


MATMUL_TN_ACC_BYTES = 16 * 1024 * 1024


def _pick(n, cap):
    best = LANES
    for c in range(LANES, cap + 1, LANES):
        if n % c == 0:
            best = c
    return best


def matmul_tn(a, b, *, name, out_dtype=F32, col_shards=None, tk=1024, a_relu2=False):
    t, m = a.shape
    n = b.shape[1]
    tm = _pick(m, 1024)
    whole_n = tm * n * 4 <= MATMUL_TN_ACC_BYTES
    tn = n if whole_n else _pick(n, 1280)
    assert whole_n or not col_shards
    tk = min(tk if tn <= 2048 else tk // 2, t)
    nk = t // tk

    def body(a_ref, b_ref, o_ref, acc):
        k = pl.program_id(2)

        @pl.when(k == 0)
        def _():
            acc[...] = jnp.zeros_like(acc)

        av = a_ref[...]
        if a_relu2:
            pos = jnp.maximum(av.astype(F32), 0.0)
            av = (pos * pos).astype(BF16)
        acc[...] += _dot_tn(av, b_ref[...])

        @pl.when(k == nk - 1)
        def _():
            if col_shards:
                w = n // col_shards
                for s in range(col_shards):
                    o_ref[s] = acc[:, s * w:(s + 1) * w].astype(out_dtype)
            else:
                o_ref[...] = acc[...].astype(out_dtype)

    if col_shards:
        out_spec = pl.BlockSpec((col_shards, tm, n // col_shards), lambda i, j, k: (0, i, 0))
        out_shape = jax.ShapeDtypeStruct((col_shards, m, n // col_shards), out_dtype)
    else:
        out_spec = pl.BlockSpec((tm, tn), lambda i, j, k: (i, j))
        out_shape = jax.ShapeDtypeStruct((m, n), out_dtype)
    return pl.pallas_call(
        body, name=name, grid=(m // tm, n // tn, nk),
        in_specs=[pl.BlockSpec((tk, tm), lambda i, j, k: (k, i)), pl.BlockSpec((tk, tn), lambda i, j, k: (k, j))],
        out_specs=out_spec, out_shape=out_shape,
        scratch_shapes=[pltpu.VMEM((tm, tn), F32)],
        compiler_params=_params("parallel", "parallel", "arbitrary"),
    )(a, b)


def matmul_nt_rms_bwd(dys, w, x, g, dres, *, tm, name, comm=None):
    t = x.shape[0]
    stacked = w.ndim == 3
    d = w.shape[1] if stacked else w.shape[0]
    widths = [dy.shape[1] for dy in dys]
    n_dy = len(dys)

    def body(*refs):
        dy_refs = refs[:n_dy]
        w_ref, x_ref, g_ref, dres_ref, dx_ref, dxb_ref, dg_ref = refs[n_dy:]
        i = pl.program_id(0)
        if stacked:
            wk = w.shape[2]
            dh = _dot_nt(dy_refs[0][:, 0:wk], w_ref[0])
            for k in range(1, w.shape[0]):
                dh = dh + _dot_nt(dy_refs[0][:, k * wk:(k + 1) * wk], w_ref[k])
        else:
            dh, off = None, 0
            for dy_ref, width in zip(dy_refs, widths):
                part = _dot_nt(dy_ref[...], w_ref[:, off:off + width])
                dh = part if dh is None else dh + part
                off += width
        xv = x_ref[...]
        r = lax.rsqrt(jnp.mean(xv * xv, axis=-1, keepdims=True) + EPS)
        xn = xv * r
        dxn = dh * g_ref[...]
        dx = dres_ref[...] + r * (dxn - xn * jnp.mean(dxn * xn, axis=-1, keepdims=True))
        dx_ref[...] = dx
        dxb_ref[...] = dx.astype(BF16)

        @pl.when(i == 0)
        def _():
            dg_ref[...] = jnp.zeros_like(dg_ref)

        dg_ref[...] += jnp.sum(dh * xn, axis=0, keepdims=True)

    tile = lambda c: pl.BlockSpec((tm, c), lambda i: (i, 0))
    return _tiled_call(
        body, name=name, grid=(t // tm,),
        in_specs=[tile(c) for c in widths] + [_resident(w.shape), tile(d), _full((1, d)), tile(d)],
        out_specs=[tile(d), tile(d), _full((1, d))],
        out_shape=[jax.ShapeDtypeStruct((t, d), F32), jax.ShapeDtypeStruct((t, d), BF16),
                   jax.ShapeDtypeStruct((1, d), F32)],
        operands=(*dys, w, x, g.reshape(1, d), dres), comm=comm)


def _conv_windows(ext_ref, tm):
    ext = ext_ref[...]
    return [pltpu.roll(ext, 3 - k, 0)[SUBLANES:SUBLANES + tm, :] for k in range(3)] + [ext[SUBLANES:SUBLANES + tm, :]]


def _conv_taps(windows, w_ref):
    acc = windows[3] * w_ref[3:4, :]
    for k in range(3):
        acc = acc + windows[k] * w_ref[k:k + 1, :]
    return acc


def _ssd_chunk_terms(dtr, bias_row, alog_row):
    q = SSD_CHUNK
    row = lax.broadcasted_iota(jnp.int32, (q, q), 0)
    col = lax.broadcasted_iota(jnp.int32, (q, q), 1)
    ltri = (col <= row).astype(F32)
    dt = _softplus(dtr + bias_row)
    a_row = -jnp.exp(alog_row)
    la = dt * a_row
    cum = _dot_hi(ltri, la)
    cum_t = _dot_tn_hi(la, (row <= col).astype(F32))
    return dt, a_row, cum, cum_t, ltri, row, col


def _head_maps():
    di = SSD_HEADS * SSD_HEADDIM
    expand = (lax.broadcasted_iota(jnp.int32, (LANES, di), 1) // SSD_HEADDIM
              == lax.broadcasted_iota(jnp.int32, (LANES, di), 0)).astype(F32)
    collapse = (lax.broadcasted_iota(jnp.int32, (di, LANES), 0) // SSD_HEADDIM
                == lax.broadcasted_iota(jnp.int32, (di, LANES), 1)).astype(F32)
    return expand, collapse


def _ssd_decay(cum, cum_t, h, causal):
    seg = cum[:, h:h + 1] - cum_t[h:h + 1, :]
    return jnp.where(causal, jnp.exp(jnp.where(causal, seg, 0.0)), 0.0)


def _pair_block_diag(x2):
    low = lax.broadcasted_iota(jnp.int32, x2.shape, 1) < SSD_HEADDIM
    zero = jnp.zeros_like(x2)
    return jnp.concatenate([jnp.where(low, x2, zero), jnp.where(low, zero, x2)], axis=0)


def ssd_fwd(xbc_raw, conv_w8, conv_b, dtr, z, bias_row, alog_row, d_row, norm_g, *, name, comm=None):
    t = xbc_raw.shape[0]
    q, p, n = SSD_CHUNK, SSD_HEADDIM, SSD_STATE
    nc = t // q
    di = SSD_HEADS * p
    gw = di // SSD_GROUPS

    def body(raw_ref, cw_ref, cb_ref, dtr_ref, z_ref, bias_ref, alog_ref, d_ref, g_ref,
             ya_ref, yssd_ref, st_ref, state, ext, xbc_s):
        ci = pl.program_id(0)

        @pl.when(ci == 0)
        def _():
            state[...] = jnp.zeros_like(state)
            ext[0:SUBLANES, :] = jnp.zeros((SUBLANES, CONV_DIM), F32)

        ext[SUBLANES:SUBLANES + q, :] = raw_ref[...].astype(F32)
        xc = _conv_taps(_conv_windows(ext, q), cw_ref) + cb_ref[...]
        ext[0:SUBLANES, :] = ext[q:q + SUBLANES, :]
        xbc_s[...] = xc * _sigmoid(xc)
        st_ref[0] = state[...]
        dt, _, cum, cum_t, _, row, col = _ssd_chunk_terms(dtr_ref[...], bias_ref[...], alog_ref[...])
        causal = row >= col
        expand, _ = _head_maps()
        dt_full = _dot_mid(dt, expand)
        cum_full = _dot_mid(cum, expand)
        last_full = cum_full[q - 1:q, :]
        xs = xbc_s[:, 0:di]
        xdt = xs * dt_full
        xd = (xdt * jnp.exp(last_full - cum_full)).astype(BF16)
        xdt_b = xdt.astype(BF16)
        e_full = jnp.exp(cum_full)
        eend_full = jnp.exp(last_full)
        for g in range(SSD_GROUPS):
            slab = slice(g * gw, (g + 1) * gw)
            bg = xbc_s[:, di + g * n:di + (g + 1) * n].astype(BF16)
            cg = xbc_s[:, di + (SSD_GROUPS + g) * n:di + (SSD_GROUPS + g + 1) * n].astype(BF16)
            cb = _dot_nt(cg, bg)
            st_g = state[:, slab]
            pairs = []
            for pr in range(SSD_HPG // 2):
                h0 = g * SSD_HPG + 2 * pr
                m0 = (cb * _ssd_decay(cum, cum_t, h0, causal)).astype(BF16)
                m1 = (cb * _ssd_decay(cum, cum_t, h0 + 1, causal)).astype(BF16)
                pairs.append(_dot(jnp.concatenate([m0, m1], axis=1), _pair_block_diag(xdt_b[:, h0 * p:(h0 + 2) * p])))
            y = jnp.concatenate(pairs, axis=1) + e_full[:, slab] * _dot(cg, st_g.astype(BF16))
            yssd_ref[:, slab] = y + d_ref[:, slab] * xs[:, slab]
            state[:, slab] = eend_full[:, slab] * st_g + _dot_tn(bg, xd[:, slab])
        zv = z_ref[...].astype(F32)
        ya1 = yssd_ref[...] * (zv * _sigmoid(zv))
        for g in range(SSD_GROUPS):
            gsl = slice(g * gw, (g + 1) * gw)
            sl = ya1[:, gsl]
            rg = lax.rsqrt(jnp.mean(sl * sl, axis=-1, keepdims=True) + EPS)
            ya_ref[:, gsl] = (sl * rg * g_ref[:, gsl]).astype(BF16)

    blk = lambda w, j: pl.BlockSpec((q, w), lambda c: (c, j))
    return _tiled_call(
        body, name=name, grid=(nc,),
        in_specs=[blk(CONV_DIM, 0), _full((SUBLANES, CONV_DIM)), _full((1, CONV_DIM)), blk(LANES, 0), blk(di, 0),
                  _full((1, LANES)), _full((1, LANES)), _full((1, di)), _full((1, di))],
        out_specs=[blk(di, 0), blk(di, 0), pl.BlockSpec((1, n, di), lambda c: (c, 0, 0))],
        out_shape=[jax.ShapeDtypeStruct((t, di), BF16), jax.ShapeDtypeStruct((t, di), F32),
                   jax.ShapeDtypeStruct((nc, n, di), F32)],
        scratch_shapes=[pltpu.VMEM((n, di), F32), pltpu.VMEM((SUBLANES + q, CONV_DIM), F32), pltpu.VMEM((q, CONV_DIM), F32)],
        operands=(xbc_raw, conv_w8, conv_b.reshape(1, CONV_DIM), dtr, z, bias_row, alog_row, d_row,
                  norm_g.reshape(1, di)), comm=comm)


def ssd_bwd(dya, yssd, z, xbc_raw, conv_w8, conv_b, dtr, states, bias_row, alog_row, d_row, norm_g, *, name, comm=None):
    t = xbc_raw.shape[0]
    q, p, n = SSD_CHUNK, SSD_HEADDIM, SSD_STATE
    nc = t // q
    di = SSD_HEADS * p
    gw = di // SSD_GROUPS
    per = q // (2 * SUBLANES)

    def body(dya_ref, yssd_ref, z_ref, raw_ref, prev_ref, cw_ref, cb_ref, dtr_ref, st_ref, bias_ref, alog_ref, d_ref,
             g_ref, dz_ref, draw_ref, ddtr_ref, dng_ref, dd_ref, dalog_ref, dbias_ref, dcb_ref, dcw_ref,
             dstate, dyssd, xbc_s, dxbc_ref, ext, aft):
        ci = pl.program_id(0)

        @pl.when(ci == 0)
        def _():
            dstate[...] = jnp.zeros_like(dstate)
            aft[q:q + SUBLANES, :] = jnp.zeros((SUBLANES, CONV_DIM), F32)
            for r in (dng_ref, dd_ref, dalog_ref, dbias_ref, dcb_ref, dcw_ref):
                r[...] = jnp.zeros_like(r)

        ext[0:SUBLANES, :] = jnp.where(ci == nc - 1, 0.0, prev_ref[SUBLANES:2 * SUBLANES, :].astype(F32))
        ext[SUBLANES:SUBLANES + q, :] = raw_ref[...].astype(F32)
        windows = _conv_windows(ext, q)
        xc = _conv_taps(windows, cw_ref) + cb_ref[...]
        sig_c = _sigmoid(xc)
        xbc_s[...] = xc * sig_c

        zv = z_ref[...].astype(F32)
        sz = _sigmoid(zv)
        silu_z = zv * sz
        yv = yssd_ref[...]
        ya1 = yv * silu_z
        dyav = dya_ref[...]
        for g in range(SSD_GROUPS):
            gsl = slice(g * gw, (g + 1) * gw)
            sl = ya1[:, gsl]
            rg = lax.rsqrt(jnp.mean(sl * sl, axis=-1, keepdims=True) + EPS)
            ya2 = sl * rg
            dy_g = dyav[:, gsl]
            dng_ref[:, gsl] += jnp.sum(dy_g * ya2, axis=0, keepdims=True)
            dya2 = dy_g * g_ref[:, gsl]
            dya1 = rg * (dya2 - ya2 * jnp.mean(dya2 * ya2, axis=-1, keepdims=True))
            dyssd[:, gsl] = dya1 * silu_z[:, gsl]
            dz_ref[:, gsl] = (dya1 * yv[:, gsl] * (sz[:, gsl] * (1.0 + zv[:, gsl] * (1.0 - sz[:, gsl])))).astype(BF16)

        dtr_v = dtr_ref[...]
        dt, a_row, cum, cum_t, ltri, row, col = _ssd_chunk_terms(dtr_v, bias_ref[...], alog_ref[...])
        causal = row >= col
        expand, collapse = _head_maps()
        lane = lax.broadcasted_iota(jnp.int32, (1, LANES), 1)
        sub = lax.broadcasted_iota(jnp.int32, (LANES, 1), 0)
        is_last = lax.broadcasted_iota(jnp.int32, (q, 1), 0) == q - 1
        low = lax.broadcasted_iota(jnp.int32, (q, 2 * p), 1) < p
        dt_full = _dot_mid(dt, expand)
        cum_full = _dot_mid(cum, expand)
        last_full = cum_full[q - 1:q, :]
        e_full = jnp.exp(cum_full)
        eend_full = jnp.exp(last_full)
        dend_full = jnp.exp(last_full - cum_full)
        xs = xbc_s[:, 0:di]
        xdt = xs * dt_full
        xdt_b = xdt.astype(BF16)
        xd_b = (xdt * dend_full).astype(BF16)
        dy_all = dyssd[...]
        dy_b = dy_all.astype(BF16)
        dg_b = (dy_all * e_full).astype(BF16)
        dcum_mat = jnp.zeros((q, LANES), F32)
        rmat = jnp.zeros((LANES, q), F32)
        yoff_prod, bds_list, dx_list, dlast_parts = [], [], [], []
        for g in range(SSD_GROUPS):
            slab = slice(g * gw, (g + 1) * gw)
            bg = xbc_s[:, di + g * n:di + (g + 1) * n].astype(BF16)
            cg = xbc_s[:, di + (SSD_GROUPS + g) * n:di + (SSD_GROUPS + g + 1) * n].astype(BF16)
            cb = _dot_nt(cg, bg)
            st_g = st_ref[0, :, slab]
            st_b = st_g.astype(BF16)
            dsn = dstate[:, slab]
            dsn_b = dsn.astype(BF16)
            yoff_prod.append(dy_all[:, slab] * e_full[:, slab] * _dot(cg, st_b))
            dcg = _dot_nt(dg_b[:, slab], st_b)
            dstate[:, slab] = _dot_tn(cg, dg_b[:, slab]) + eend_full[:, slab] * dsn
            bds_list.append(_dot(bg, dsn_b))
            dbg = _dot_nt(xd_b[:, slab], dsn_b)
            dlast_parts.append(jnp.sum(dsn * st_g, axis=0, keepdims=True))
            dcb = jnp.zeros((q, q), F32)
            dx_pairs = []
            for pr in range(SSD_HPG // 2):
                h0 = g * SSD_HPG + 2 * pr
                ps = slice(h0 * p, (h0 + 2) * p)
                l0 = _ssd_decay(cum, cum_t, h0, causal)
                l1 = _ssd_decay(cum, cum_t, h0 + 1, causal)
                m0, m1 = cb * l0, cb * l1
                dyp = dy_b[:, ps]
                zero = jnp.zeros_like(dyp)
                dy0, dy1 = jnp.where(low, dyp, zero), jnp.where(low, zero, dyp)
                dm0 = _dot_nt(dy0, xdt_b[:, ps])
                dm1 = _dot_nt(dy1, xdt_b[:, ps])
                w0, w1 = dm0 * m0, dm1 * m1
                dcum_mat = (dcum_mat + jnp.sum(w0, axis=1, keepdims=True) * (lane == h0).astype(F32)
                            + jnp.sum(w1, axis=1, keepdims=True) * (lane == h0 + 1).astype(F32))
                rmat = (rmat + jnp.where(sub == h0, jnp.sum(w0, axis=0, keepdims=True), 0.0)
                        + jnp.where(sub == h0 + 1, jnp.sum(w1, axis=0, keepdims=True), 0.0))
                dcb = dcb + dm0 * l0 + dm1 * l1
                dx_pairs.append(_dot_tn(jnp.concatenate([m0.astype(BF16), m1.astype(BF16)], axis=0),
                                        jnp.concatenate([dy0, dy1], axis=0)))
            dx_list.append(jnp.concatenate(dx_pairs, axis=1))
            dcb_b = dcb.astype(BF16)
            dxbc_ref[:, di + g * n:di + (g + 1) * n] = dbg + _dot_tn(dcb_b, cg)
            dxbc_ref[:, di + SSD_GROUPS * n + g * n:di + SSD_GROUPS * n + (g + 1) * n] = dcg + _dot(dcb_b, bg)
        bds_all = jnp.concatenate(bds_list, axis=1)
        dx_all = jnp.concatenate(dx_list, axis=1) + dend_full * bds_all
        last_row = cum[q - 1:q, :]
        qv = _dot_mid(xdt * bds_all, collapse) * jnp.exp(last_row - cum)
        dcum_mat = dcum_mat + _dot_mid(jnp.concatenate(yoff_prod, axis=1), collapse) - qv
        dlast_full = jnp.broadcast_to(jnp.concatenate(dlast_parts, axis=1), (SUBLANES, di))
        dlast_row = _dot_mid(dlast_full, collapse)[0:1, :] * jnp.exp(last_row) + jnp.sum(qv, axis=0, keepdims=True)
        dcum_mat = dcum_mat + jnp.where(is_last, dlast_row, 0.0)
        dla = _dot_tn_hi(ltri, dcum_mat) - _dot_nt_hi((row <= col).astype(F32), rmat)
        ddt_mat = _dot_mid(dx_all * xs, collapse) + dla * a_row
        dxbc_ref[:, 0:di] = d_ref[...] * dy_all + dx_all * dt_full
        ddtr = ddt_mat * _sigmoid(dtr_v + bias_ref[...])
        ddtr_ref[...] = ddtr.astype(BF16)
        dd_full = jnp.broadcast_to(jnp.sum(dy_all * xs, axis=0, keepdims=True), (SUBLANES, di))
        dd_ref[...] += _dot_mid(dd_full, collapse)[0:1, :]
        dalog_ref[...] += jnp.sum(dla * dt, axis=0, keepdims=True) * a_row
        dbias_ref[...] += jnp.sum(ddtr, axis=0, keepdims=True)

        dxc = dxbc_ref[...] * (sig_c * (1.0 + xc * (1.0 - sig_c)))
        dcb_ref[...] += jnp.sum(dxc, axis=0, keepdims=True)
        taps = [jnp.sum(dxc * windows[k], axis=0, keepdims=True) for k in range(SSD_CONV)]
        dcw_ref[...] += jnp.concatenate(taps + [jnp.zeros((SUBLANES - SSD_CONV, CONV_DIM), F32)], axis=0)
        aft[0:q, :] = dxc
        after = aft[...]
        draw = dxc * cw_ref[3:4, :]
        for j in range(1, SSD_CONV):
            draw = draw + pltpu.roll(after, q + SUBLANES - j, 0)[0:q, :] * cw_ref[3 - j:4 - j, :]
        draw_ref[...] = draw.astype(BF16)
        aft[q:q + SUBLANES, :] = dxc[0:SUBLANES, :]

    blk = lambda w, j: pl.BlockSpec((q, w), lambda c: (nc - 1 - c, j))
    row_out = lambda w: jax.ShapeDtypeStruct((1, w), F32)
    return _tiled_call(
        body, name=name, grid=(nc,),
        in_specs=[blk(di, 0), blk(di, 0), blk(di, 0), blk(CONV_DIM, 0),
                  pl.BlockSpec((2 * SUBLANES, CONV_DIM), lambda c: (jnp.maximum((nc - 1 - c) * per - 1, 0), 0)),
                  _full((SUBLANES, CONV_DIM)), _full((1, CONV_DIM)), blk(LANES, 0),
                  pl.BlockSpec((1, n, di), lambda c: (nc - 1 - c, 0, 0)),
                  _full((1, LANES)), _full((1, LANES)), _full((1, di)), _full((1, di))],
        out_specs=[blk(di, 0), blk(CONV_DIM, 0), blk(LANES, 0),
                   _full((1, di)), _full((1, LANES)), _full((1, LANES)), _full((1, LANES)),
                   _full((1, CONV_DIM)), _full((SUBLANES, CONV_DIM))],
        out_shape=[jax.ShapeDtypeStruct((t, di), BF16), jax.ShapeDtypeStruct((t, CONV_DIM), BF16),
                   jax.ShapeDtypeStruct((t, LANES), BF16), row_out(di), row_out(LANES), row_out(LANES), row_out(LANES),
                   row_out(CONV_DIM), jax.ShapeDtypeStruct((SUBLANES, CONV_DIM), F32)],
        scratch_shapes=[pltpu.VMEM((n, di), F32), pltpu.VMEM((q, di), F32), pltpu.VMEM((q, CONV_DIM), F32),
                        pltpu.VMEM((q, CONV_DIM), F32), pltpu.VMEM((SUBLANES + q, CONV_DIM), F32),
                        pltpu.VMEM((q + SUBLANES, CONV_DIM), F32)],
        operands=(dya, yssd, z, xbc_raw, xbc_raw, conv_w8, conv_b.reshape(1, CONV_DIM), dtr, states, bias_row,
                  alog_row, d_row, norm_g.reshape(1, di)),
        comm=comm)


S5_BLOCKS = 4
S5_BC = S5_WIDTH // S5_BLOCKS
S5_BS = S5_N // S5_BLOCKS


def _s5_tables(ar, ai, reverse):
    pows = [(ar, ai)]
    for _ in range(SUBLANES - 1):
        pr, pi = pows[-1]
        pows.append((pr * ar - pi * ai, pr * ai + pi * ar))
    row = lax.broadcasted_iota(jnp.int32, (SUBLANES, ar.shape[1]), 0)
    out = []
    for k in (1, 2, 4):
        pr, pi = pows[k - 1]
        mask = (row < SUBLANES - k) if reverse else (row >= k)
        out += [jnp.where(mask, pr, 0.0), jnp.where(mask, pi, 0.0)]
    order = list(range(SUBLANES))
    if reverse:
        order = order[::-1]
    out.append(jnp.concatenate([pows[e][0] for e in order], axis=0))
    out.append(jnp.concatenate([pows[e][1] for e in order], axis=0))
    return out


def s5_interleave(a, tm):
    t, c = a.shape
    return a.reshape(t // tm, SUBLANES, tm // SUBLANES, c).transpose(0, 2, 1, 3).reshape(t, c)


def s5_deinterleave(a, tm):
    t, c = a.shape
    return a.reshape(t // tm, tm // SUBLANES, SUBLANES, c).transpose(0, 2, 1, 3).reshape(t, c)


def _s5_scan_setup(ab_ref, base, tab, seg, reverse):
    ar = ab_ref[0:1, :]
    ai = -ab_ref[1:2, :] if reverse else ab_ref[1:2, :]
    base[0:1, :] = ar
    base[1:2, :] = ai
    assert seg & (seg - 1) == 0
    pr, pi = ar, ai
    for _ in range(seg.bit_length() - 1):
        pr, pi = pr * pr - pi * pi, 2.0 * pr * pi
    for k, tv in enumerate(_s5_tables(pr, pi, reverse)):
        tab[k] = tv


def _s5_scan(s_scr, base, tab, carry, tm, reverse):
    seg = tm // SUBLANES
    width = S5_BS
    first = lax.broadcasted_iota(jnp.int32, (SUBLANES, width), 0) == (SUBLANES - 1 if reverse else 0)

    def rows(i):
        step = (seg - 1 - i) if reverse else i
        return pl.ds(pl.multiple_of(step * SUBLANES, SUBLANES), SUBLANES)

    for lc in range(S5_N // width):
        lr = slice(lc * width, (lc + 1) * width)
        li = slice(S5_N + lc * width, S5_N + (lc + 1) * width)
        ar = jnp.broadcast_to(base[0:1, lr], (SUBLANES, width))
        ai = jnp.broadcast_to(base[1:2, lr], (SUBLANES, width))

        def advance(i, x, lr=lr, li=li, ar=ar, ai=ai):
            xr, xi = x
            return ar * xr - ai * xi + s_scr[rows(i), lr], ar * xi + ai * xr + s_scr[rows(i), li]

        zero = jnp.zeros((SUBLANES, width), F32)
        fr, fi = lax.fori_loop(0, seg, advance, (zero, zero))

        tb = [tab[k, :, lr] for k in range(8)]
        for lvl, k in enumerate((1, 2, 4)):
            sh = (SUBLANES - k) if reverse else k
            pr, pi = tb[2 * lvl], tb[2 * lvl + 1]
            rr, ri = pltpu.roll(fr, sh, 0), pltpu.roll(fi, sh, 0)
            fr, fi = fr + pr * rr - pi * ri, fi + pr * ri + pi * rr
        cr, ci = carry[0:1, lr], carry[0:1, li]
        fr, fi = fr + tb[6] * cr - tb[7] * ci, fi + tb[6] * ci + tb[7] * cr
        last = 0 if reverse else SUBLANES - 1
        carry[0:1, lr] = fr[last:last + 1, :]
        carry[0:1, li] = fi[last:last + 1, :]
        sh = (SUBLANES - 1) if reverse else 1
        entering = (jnp.where(first, cr, pltpu.roll(fr, sh, 0)), jnp.where(first, ci, pltpu.roll(fi, sh, 0)))

        def rerun(i, x, lr=lr, li=li, advance=advance):
            xr, xi = advance(i, x)
            s_scr[rows(i), lr] = xr
            s_scr[rows(i), li] = xi
            return xr, xi

        lax.fori_loop(0, seg, rerun, entering)


def _s5_scratch(tm):
    return [pltpu.VMEM((tm, 2 * S5_N), F32), pltpu.VMEM((SUBLANES, S5_N), F32),
            pltpu.VMEM((8, SUBLANES, S5_N), F32), pltpu.VMEM((SUBLANES, 2 * S5_N), F32)]


def _s5_put(scr, j, val):
    scr[:, j * S5_BS:(j + 1) * S5_BS] = val[:, :S5_BS]
    scr[:, S5_N + j * S5_BS:S5_N + (j + 1) * S5_BS] = val[:, S5_BS:]


def _s5_get(scr, j):
    return jnp.concatenate([scr[:, j * S5_BS:(j + 1) * S5_BS], scr[:, S5_N + j * S5_BS:S5_N + (j + 1) * S5_BS]], axis=1)


def s5_fwd(u5, bbc, ccc, ab8, d_row, wglu, bglu_row, *, tm, name, comm=None):
    t, w = u5.shape

    def body(u_ref, bb_ref, cc_ref, ab_ref, d_ref, wg_ref, bg_ref, s_ref, ys_ref, yb_ref, s_scr, base, tab, carry):
        i = pl.program_id(0)

        @pl.when(i == 0)
        def _():
            carry[...] = jnp.zeros_like(carry)
            _s5_scan_setup(ab_ref, base, tab, tm // SUBLANES, False)

        ub = u_ref[...]
        u = ub.astype(F32)
        for j in range(S5_BLOCKS):
            uj = ub[:, j * S5_BC:(j + 1) * S5_BC]
            _s5_put(s_scr, j, _dot(uj, bb_ref[j * S5_BC:(j + 1) * S5_BC, :]))
        _s5_scan(s_scr, base, tab, carry, tm, False)
        s_ref[...] = s_scr[...]
        ys = []
        for j in range(S5_BLOCKS):
            ys.append(_dot(_s5_get(s_scr, j).astype(BF16), cc_ref[:, j * S5_BC:(j + 1) * S5_BC]))
        y = jnp.concatenate(ys, axis=1) + d_ref[...] * u
        ys_ref[...] = y
        yb1 = _gelu(y)
        tt = _dot(yb1.astype(BF16), wg_ref[...]) + bg_ref[...]
        yb_ref[...] = (yb1 * _sigmoid(tt)).astype(BF16)

    tile = lambda c: pl.BlockSpec((tm, c), lambda i: (i, 0))
    return _tiled_call(
        body, name=name, grid=(t // tm,),
        in_specs=[tile(w), _full((w, 2 * S5_BS)), _full((2 * S5_BS, w)), _full((SUBLANES, S5_N)), _full((1, w)),
                  _full((w, w)), _full((1, w))],
        out_specs=[tile(2 * S5_N), tile(w), tile(w)],
        out_shape=[jax.ShapeDtypeStruct((t, 2 * S5_N), F32), jax.ShapeDtypeStruct((t, w), F32),
                   jax.ShapeDtypeStruct((t, w), BF16)],
        scratch_shapes=_s5_scratch(tm),
        operands=(u5, bbc, ccc, ab8, d_row, wglu, bglu_row), comm=comm)


def s5_bwd(dyb, ys5, u5, s, bbc, ccc, ab8, d_row, wglu, bglu_row, *, tm, name, comm=None):
    t, w = u5.shape
    nt = t // tm
    per = tm // SUBLANES

    def body(dyb_ref, ys_ref, u_ref, s_ref, sprev_ref, bb_ref, cc_ref, ab_ref, d_ref, wg_ref, bg_ref,
             du_ref, dbb_ref, dcc_ref, dab_ref, dd_ref, dwg_ref, dbg_ref, g_scr, base, tab, carry):
        i = pl.program_id(0)

        @pl.when(i == 0)
        def _():
            carry[...] = jnp.zeros_like(carry)
            _s5_scan_setup(ab_ref, base, tab, tm // SUBLANES, True)
            for r in (dbb_ref, dcc_ref, dab_ref, dd_ref, dwg_ref, dbg_ref):
                r[...] = jnp.zeros_like(r)

        y = ys_ref[...]
        ub = u_ref[...]
        u = ub.astype(F32)
        yb1 = _gelu(y)
        yb1b = yb1.astype(BF16)
        sg = _sigmoid(_dot(yb1b, wg_ref[...]) + bg_ref[...])
        dybv = dyb_ref[...]
        dtt = dybv * yb1 * sg * (1.0 - sg)
        dttb = dtt.astype(BF16)
        dyb1 = dybv * sg + _dot_nt(dttb, wg_ref[...])
        dwg_ref[...] += _dot_tn(yb1b, dttb)
        dbg_ref[...] += jnp.sum(dtt, axis=0, keepdims=True)
        dy = dyb1 * _gelu_grad(y)
        dd_ref[...] += jnp.sum(dy * u, axis=0, keepdims=True)
        dyh = dy.astype(BF16)
        for j in range(S5_BLOCKS):
            cs = slice(j * S5_BC, (j + 1) * S5_BC)
            _s5_put(g_scr, j, _dot_nt(dyh[:, cs], cc_ref[:, cs]))
        _s5_scan(g_scr, base, tab, carry, tm, True)
        dus = []
        for j in range(S5_BLOCKS):
            cs = slice(j * S5_BC, (j + 1) * S5_BC)
            gb = _s5_get(g_scr, j).astype(BF16)
            dus.append(_dot_nt(gb, bb_ref[cs, :]))
            dbb_ref[cs, :] += _dot_tn(ub[:, cs], gb)
            dcc_ref[:, cs] += _dot_tn(_s5_get(s_ref, j).astype(BF16), dyh[:, cs])
        du_ref[...] = (jnp.concatenate(dus, axis=1) + d_ref[...] * dy).astype(BF16)
        top = lax.broadcasted_iota(jnp.int32, (SUBLANES, S5_BS), 0) == 0

        def corr(g_sl, s_sl):
            before = jnp.where(i == nt - 1, 0.0, sprev_ref[SUBLANES - 1:SUBLANES, s_sl])
            wrap = jnp.where(top, before, pltpu.roll(s_ref[tm - SUBLANES:tm, s_sl], 1, 0))
            return (jnp.sum(g_scr[SUBLANES:tm, g_sl] * s_ref[0:tm - SUBLANES, s_sl], axis=0, keepdims=True)
                    + jnp.sum(g_scr[0:SUBLANES, g_sl] * wrap, axis=0, keepdims=True))

        for j in range(S5_BLOCKS):
            lr = slice(j * S5_BS, (j + 1) * S5_BS)
            li = slice(S5_N + j * S5_BS, S5_N + (j + 1) * S5_BS)
            dab_ref[0:1, lr] += corr(lr, lr) + corr(li, li)
            dab_ref[1:2, lr] += corr(li, lr) - corr(lr, li)

    tile = lambda c: pl.BlockSpec((tm, c), lambda i: (nt - 1 - i, 0))
    acc = lambda r, c: jax.ShapeDtypeStruct((r, c), F32)
    return _tiled_call(
        body, name=name, grid=(nt,),
        in_specs=[tile(w), tile(w), tile(w), tile(2 * S5_N),
                  pl.BlockSpec((SUBLANES, 2 * S5_N), lambda i: (jnp.maximum((nt - 1 - i) * per - 1, 0), 0)),
                  _full((w, 2 * S5_BS)), _full((2 * S5_BS, w)), _full((SUBLANES, S5_N)), _full((1, w)),
                  _full((w, w)), _full((1, w))],
        out_specs=[tile(w), _full((w, 2 * S5_BS)), _full((2 * S5_BS, w)), _full((SUBLANES, S5_N)), _full((1, w)),
                   _full((w, w)), _full((1, w))],
        out_shape=[jax.ShapeDtypeStruct((t, w), BF16), acc(w, 2 * S5_BS), acc(2 * S5_BS, w), acc(SUBLANES, S5_N),
                   acc(1, w), acc(w, w), acc(1, w)],
        scratch_shapes=_s5_scratch(tm),
        operands=(dyb, ys5, u5, s, s, bbc, ccc, ab8, d_row, wglu, bglu_row), comm=comm)


def s5_discretize(a_re, a_im, log_dt, b_re, b_im, *, name):
    n = a_re.shape[0]

    def body(ar_ref, ai_ref, dt_ref, br_ref, bi_ref, ab_ref, bbr_ref, bbi_ref):
        ar, ai, dtv = ar_ref[...], ai_ref[...], jnp.exp(dt_ref[...])
        mag = jnp.exp(ar * dtv)
        abr = mag * jnp.cos(ai * dtv)
        abi = mag * jnp.sin(ai * dtv)
        den = ar * ar + ai * ai
        nr = abr - 1.0
        cr = (nr * ar + abi * ai) / den
        ci = (abi * ar - nr * ai) / den
        ab_ref[:, 0:1] = abr
        ab_ref[:, 1:2] = abi
        br, bi = br_ref[...], bi_ref[...]
        bbr_ref[...] = cr * br - ci * bi
        bbi_ref[...] = cr * bi + ci * br

    col = jax.ShapeDtypeStruct((n, 2), F32)
    mat = jax.ShapeDtypeStruct((n, S5_GROUP), F32)
    return pl.pallas_call(body, name=name, out_shape=[col, mat, mat])(a_re, a_im, log_dt, b_re, b_im)


def s5_discretize_bwd(a_re, a_im, log_dt, b_re, b_im, dab, dbb_re, dbb_im, *, name):
    n = a_re.shape[0]

    def body(ar_ref, ai_ref, dt_ref, br_ref, bi_ref, dab_ref, dbr_ref, dbi_ref, da_ref, ddt_ref, gbr_ref, gbi_ref):
        ar, ai, dtv = ar_ref[...], ai_ref[...], jnp.exp(dt_ref[...])
        mag = jnp.exp(ar * dtv)
        abr = mag * jnp.cos(ai * dtv)
        abi = mag * jnp.sin(ai * dtv)
        den = ar * ar + ai * ai
        nr = abr - 1.0
        cr = (nr * ar + abi * ai) / den
        ci = (abi * ar - nr * ai) / den
        br, bi = br_ref[...], bi_ref[...]
        dbr, dbi = dbr_ref[...], dbi_ref[...]
        gbr_ref[...] = cr * dbr + ci * dbi
        gbi_ref[...] = cr * dbi - ci * dbr
        gcr = jnp.sum(dbr * br + dbi * bi, axis=1, keepdims=True)
        gci = jnp.sum(dbi * br - dbr * bi, axis=1, keepdims=True)
        ilr, ili = ar / den, -ai / den
        gabr = dab_ref[:, 0:1] + (ilr * gcr + ili * gci)
        gabi = dab_ref[:, 1:2] + (ilr * gci - ili * gcr)
        t1r, t1i = dtv * abr, dtv * abi
        t2r, t2i = -(cr * ilr - ci * ili), -(cr * ili + ci * ilr)
        da_ref[:, 0:1] = (t1r * gabr + t1i * gabi) + (t2r * gcr + t2i * gci)
        da_ref[:, 1:2] = (t1r * gabi - t1i * gabr) + (t2r * gci - t2i * gcr)
        lr, li = ar * abr - ai * abi, ar * abi + ai * abr
        dlog = (lr * gabr + li * gabi) * dtv
        ddt_ref[...] = jnp.sum(dlog.reshape(S5_GROUPS, S5_STATE, 1), axis=1)

    col2 = jax.ShapeDtypeStruct((n, 2), F32)
    col1 = jax.ShapeDtypeStruct((S5_GROUPS, 1), F32)
    mat = jax.ShapeDtypeStruct((n, S5_GROUP), F32)
    return pl.pallas_call(body, name=name, out_shape=[col2, col1, mat, mat])(
        a_re, a_im, log_dt, b_re, b_im, dab, dbb_re, dbb_im)


def merge_fwd(ya, yb, graw, x, wb, wout, *, tm, name):
    t, d = x.shape

    def body(ya_ref, yb_ref, g_ref, x_ref, wb_ref, wo_ref, x1_ref, pa_ref, pb_ref):
        pa = _dot(ya_ref[...], wb_ref[0:d, :])
        pb = _dot(yb_ref[...], wb_ref[d:, :])
        gate = _sigmoid(g_ref[...].astype(F32))
        merged = gate[:, :d] * pa + gate[:, d:] * pb
        x1_ref[...] = x_ref[...] + _dot(merged.astype(BF16), wo_ref[...])
        pa_ref[...] = pa.astype(BF16)
        pb_ref[...] = pb.astype(BF16)

    tile = lambda c: pl.BlockSpec((tm, c), lambda i: (i, 0))
    sds = jax.ShapeDtypeStruct
    return pl.pallas_call(
        body, name=name, grid=(t // tm,),
        in_specs=[tile(d), tile(S5_WIDTH), tile(2 * d), tile(d), _resident((d + S5_WIDTH, d)), _resident((d, d))],
        out_specs=[tile(d), tile(d), tile(d)], out_shape=[sds((t, d), F32), sds((t, d), BF16), sds((t, d), BF16)],
        compiler_params=_params("parallel"),
    )(ya, yb, graw, x, wb, wout)


def merge_bwd(dx1b, graw, pa, pb, wb, wout, *, tm, name):
    t, d = pa.shape

    def body(dx_ref, g_ref, pa_ref, pb_ref, wb_ref, wo_ref,
             mg_ref, dg_ref, dpa_ref, dpb_ref, dya_ref, dyb_ref):
        dm = _dot_nt(dx_ref[...], wo_ref[...])
        gate = _sigmoid(g_ref[...].astype(F32))
        g0, g1 = gate[:, :d], gate[:, d:]
        pa, pb = pa_ref[...].astype(F32), pb_ref[...].astype(F32)
        mg_ref[...] = (g0 * pa + g1 * pb).astype(BF16)
        dg_ref[:, :d] = (dm * pa * g0 * (1.0 - g0)).astype(BF16)
        dg_ref[:, d:] = (dm * pb * g1 * (1.0 - g1)).astype(BF16)
        dpa = (dm * g0).astype(BF16)
        dpb = (dm * g1).astype(BF16)
        dpa_ref[...] = dpa
        dpb_ref[...] = dpb
        dya_ref[...] = _dot_nt(dpa, wb_ref[0:d, :])
        dyb_ref[...] = _dot_nt(dpb, wb_ref[d:, :])

    tile = lambda c: pl.BlockSpec((tm, c), lambda i: (i, 0))
    sds = jax.ShapeDtypeStruct
    return pl.pallas_call(
        body, name=name, grid=(t // tm,),
        in_specs=[tile(d), tile(2 * d), tile(d), tile(d), _resident((d + S5_WIDTH, d)), _resident((d, d))],
        out_specs=[tile(d), tile(2 * d), tile(d), tile(d), tile(d), tile(S5_WIDTH)],
        out_shape=[sds((t, d), BF16), sds((t, 2 * d), BF16), sds((t, d), BF16), sds((t, d), BF16),
                   sds((t, d), F32), sds((t, S5_WIDTH), F32)],
        compiler_params=_params("parallel"),
    )(dx1b, graw, pa, pb, wb, wout)


def mlp_out_loss(a1, x1, w2, gf, target, *, tm, name):
    t, d = x1.shape
    f = a1.shape[1]

    def body(a_ref, x_ref, w_ref, g_ref, tg_ref, dx_ref, dxb_ref, loss_ref, dg_ref):
        i = pl.program_id(0)
        av = jnp.maximum(a_ref[...].astype(F32), 0.0)
        x2 = x_ref[...] + _dot((av * av).astype(BF16), w_ref[...])
        r = lax.rsqrt(jnp.mean(x2 * x2, axis=-1, keepdims=True) + EPS)
        xn = x2 * r
        err = xn * g_ref[...] - tg_ref[...]
        dyf = err * (1.0 / d)
        dxn = dyf * g_ref[...]
        dx = r * (dxn - xn * jnp.mean(dxn * xn, axis=-1, keepdims=True))
        dx_ref[...] = dx
        dxb_ref[...] = dx.astype(BF16)

        @pl.when(i == 0)
        def _():
            loss_ref[...] = jnp.zeros_like(loss_ref)
            dg_ref[...] = jnp.zeros_like(dg_ref)

        loss_ref[...] += jnp.sum(err * err) * (0.5 / d)
        dg_ref[...] += jnp.sum(dyf * xn, axis=0, keepdims=True)

    tile = lambda c: pl.BlockSpec((tm, c), lambda i: (i, 0))
    sds = jax.ShapeDtypeStruct
    return pl.pallas_call(
        body, name=name, grid=(t // tm,),
        in_specs=[tile(f), tile(d), _resident((f, d)), _full((1, d)), tile(d)],
        out_specs=[tile(d), tile(d), _full((1, LANES)), _full((1, d))],
        out_shape=[sds((t, d), F32), sds((t, d), BF16), sds((1, LANES), F32), sds((1, d), F32)],
        compiler_params=_params("arbitrary"),
    )(a1, x1, w2, gf.reshape(1, d), target)


def mlp_bwd_act(dx2b, a1, w2, *, tm, name):
    t, f = a1.shape
    d = dx2b.shape[1]

    def body(dx_ref, a_ref, w_ref, o_ref):
        dact = _dot_nt(dx_ref[...], w_ref[...])
        o_ref[...] = (dact * 2.0 * jnp.maximum(a_ref[...].astype(F32), 0.0)).astype(BF16)

    tile = lambda c: pl.BlockSpec((tm, c), lambda i: (i, 0))
    return pl.pallas_call(
        body, name=name, grid=(t // tm,),
        in_specs=[tile(d), tile(f), _resident((f, d))], out_specs=tile(f),
        out_shape=jax.ShapeDtypeStruct((t, f), BF16),
        compiler_params=_params("parallel"),
    )(dx2b, a1, w2)


ADAM_TILE_ELEMS = 128 * 1024


def _row_tile(rows, cap):
    if rows <= cap:
        return rows
    best = None
    for c in range(16, cap + 1, 16):
        if rows % c == 0:
            best = c
    assert best is not None, rows
    return best


def _device_order_sum(parts_ref):
    total = parts_ref[0].astype(F32)
    for k in range(1, N_DEV):
        total = total + parts_ref[k].astype(F32)
    return total


def _adamw_math(w, m, v, gparts_ref):
    g = _device_order_sum(gparts_ref)
    mn = ADAM_B1 * m + (1.0 - ADAM_B1) * g
    vn = ADAM_B2 * v + (1.0 - ADAM_B2) * (g * g)
    m_hat = mn / (1.0 - ADAM_B1 ** ADAM_STEP)
    v_hat = vn / (1.0 - ADAM_B2 ** ADAM_STEP)
    return g, -ADAM_LR * (m_hat / (jnp.sqrt(v_hat) + ADAM_EPS) + ADAM_WD * w), mn, vn


def adamw(w, m, v, gparts, *, name):
    rows, cols = w.shape
    tr = _row_tile(rows, max(16, ADAM_TILE_ELEMS // cols))

    def body(w_ref, m_ref, v_ref, g_ref, go_ref, d_ref, mo_ref, vo_ref):
        go_ref[...], d_ref[...], mo_ref[...], vo_ref[...] = _adamw_math(w_ref[...], m_ref[...], v_ref[...], g_ref)

    tile = pl.BlockSpec((tr, cols), lambda i: (i, 0))
    out = jax.ShapeDtypeStruct((rows, cols), F32)
    return pl.pallas_call(
        body, name=name, grid=(rows // tr,),
        in_specs=[tile, tile, tile, pl.BlockSpec((N_DEV, tr, cols), lambda i: (0, i, 0))],
        out_specs=[tile, tile, tile, tile], out_shape=[out, out, out, out],
        compiler_params=_params("parallel"),
    )(w, m, v, gparts)


def all_gather_arrays(blocks, *, name):
    na = len(blocks)

    def body(*refs):
        x_refs, out_refs = refs[:na], refs[na:2 * na]
        send_sems, recv_sems, local_sems = refs[2 * na:]
        x, y, c, _ = _place()
        me, sibling = (x, y, c), (x, y, 1 - c)
        chips = [(1 - x, y), (x, 1 - y), (1 - x, 1 - y)]

        def copy(a, k, blk, to, own=False):
            px, py, pc = blk
            dst = out_refs[a].at[4 * px + 2 * py + pc]
            return pltpu.make_async_remote_copy(
                src_ref=x_refs[a] if own else dst, dst_ref=dst,
                send_sem=send_sems.at[7 * a + k], recv_sem=recv_sems.at[7 * a + k], device_id=to, device_id_type=_MESH)

        mine = [pltpu.make_async_copy(x_refs[a], out_refs[a].at[4 * x + 2 * y + c], local_sems.at[a]) for a in range(na)]
        for cp in mine:
            cp.start()
        sent = []
        for a in range(na):
            first = [copy(a, 0, me, sibling, own=True)]
            first += [copy(a, 1 + j, me, (*chip, c), own=True) for j, chip in enumerate(chips)]
            for cp in first:
                cp.start()
            sent += first
        for a in range(na):
            for j, chip in enumerate(chips):
                copy(a, 1 + j, (*chip, c), me).wait_recv()
                fwd = copy(a, 4 + j, (*chip, c), sibling)
                fwd.start()
                sent.append(fwd)
        for a in range(na):
            copy(a, 0, sibling, me).wait_recv()
            for j, chip in enumerate(chips):
                copy(a, 4 + j, (*chip, 1 - c), me).wait_recv()
        for cp in sent:
            cp.wait_send()
        for cp in mine:
            cp.wait()

    any_spec = pl.BlockSpec(memory_space=pl.ANY)
    return pl.pallas_call(
        body, name=name, in_specs=[any_spec] * na, out_specs=[any_spec] * na,
        out_shape=[jax.ShapeDtypeStruct((N_DEV,) + b.shape, b.dtype) for b in blocks],
        scratch_shapes=[pltpu.SemaphoreType.DMA((7 * na,)), pltpu.SemaphoreType.DMA((7 * na,)),
                        pltpu.SemaphoreType.DMA((na,))],
        compiler_params=pltpu.CompilerParams(has_side_effects=True),
    )(*blocks)


def all_to_all(comm, *, name):
    na = len(comm)
    kinds = [k for k, _ in comm]

    def body(*refs):
        local, remote = _comm_copies(kinds, refs[:na], refs[na:2 * na], *refs[2 * na:])
        for cp in local + remote:
            cp.start()
        for cp in remote:
            cp.wait_recv()
        for cp in remote:
            cp.wait_send()
        for cp in local:
            cp.wait()

    any_spec = pl.BlockSpec(memory_space=pl.ANY)
    return pl.pallas_call(
        body, name=name, in_specs=[any_spec] * na, out_specs=[any_spec] * na,
        out_shape=_comm_out_shapes(comm),
        scratch_shapes=[pltpu.SemaphoreType.DMA((na * N_REL,)), pltpu.SemaphoreType.DMA((na * N_REL,)),
                        pltpu.SemaphoreType.DMA((na,))],
        compiler_params=pltpu.CompilerParams(has_side_effects=True),
    )(*[a for _, a in comm])


def adamw_whole(ws, ms, vs, gparts, *, name, totals=()):
    n, nt = len(ws), len(totals)

    def body(*refs):
        w_refs, m_refs, v_refs, g_refs = refs[:n], refs[n:2 * n], refs[2 * n:3 * n], refs[3 * n:4 * n]
        t_refs, refs = refs[4 * n:4 * n + nt], refs[4 * n + nt:]
        outs, t_outs = refs[:4 * n], refs[4 * n:]
        for k in range(n):
            g, delta, mn, vn = _adamw_math(w_refs[k][...], m_refs[k][...], v_refs[k][...], g_refs[k])
            outs[k][...] = g
            outs[n + k][...] = delta
            outs[2 * n + k][...] = mn
            outs[3 * n + k][...] = vn
        for t_ref, o_ref in zip(t_refs, t_outs):
            o_ref[...] = _device_order_sum(t_ref)

    shapes = [jax.ShapeDtypeStruct(w.shape, F32) for w in ws]
    res = pl.pallas_call(
        body, name=name, out_shape=shapes * 4 + [jax.ShapeDtypeStruct(t.shape[1:], F32) for t in totals],
        compiler_params=pltpu.CompilerParams(vmem_limit_bytes=VMEM_LIMIT),
    )(*ws, *ms, *vs, *gparts, *totals)
    return [res[j * n:(j + 1) * n] for j in range(4)], res[4 * n:]


def _lane_row(v):
    return jnp.pad(v.astype(F32), (0, LANES - v.shape[0])).reshape(1, LANES)


def _block_diag_b(bb):
    eye = jnp.eye(SUBLANES, dtype=F32)
    bt = bb.reshape(S5_BLOCKS, 8, S5_STATE, S5_GROUP).transpose(0, 1, 3, 2)
    return (bt[:, :, :, None, :] * eye[None, :, None, :, None]).reshape(S5_WIDTH, S5_BS)


def _block_diag_c(cc):
    eye = jnp.eye(SUBLANES, dtype=F32)
    ct = cc.reshape(S5_BLOCKS, 8, S5_GROUP, S5_STATE).transpose(3, 0, 1, 2)
    return (ct[None, :, :, :, :] * eye[:, None, None, :, None]).reshape(S5_BS, S5_WIDTH)


def _diag_blocks_b(m):
    eye = jnp.eye(SUBLANES, dtype=F32)
    m5 = m.reshape(S5_BLOCKS, 8, S5_GROUP, 8, S5_STATE)
    return jnp.sum(m5 * eye[None, :, None, :, None], axis=3).transpose(0, 1, 3, 2).reshape(S5_GROUPS, S5_STATE, S5_GROUP)


def _diag_blocks_c(m):
    eye = jnp.eye(SUBLANES, dtype=F32)
    m5 = m.reshape(8, S5_STATE, S5_BLOCKS, 8, S5_GROUP)
    return jnp.sum(m5 * eye[:, None, None, :, None], axis=0).transpose(1, 2, 3, 0).reshape(S5_GROUPS, S5_GROUP, S5_STATE)


def train_step(x, target, p, m, v):
    t = x.shape[0]
    tm = min(256, t)
    tb = min(512, t)
    ts = min(512, t)
    bf = lambda n: p[n].astype(BF16)
    slots = lambda n, a: a.reshape((N_DEV,) + _shard_shape(n))
    updated = {}

    def update(names, got):
        for n, parts in zip(names, got):
            updated[n] = adamw(p[n], m[n], v[n], parts, name=f"adamw_{n}")

    st_in, st_conv = all_gather_arrays([bf("w_in"), p["conv_w"]], name="gather_first")
    win_p = jnp.concatenate([st_in[k][:, a:b] for seg in IN_PADDED_ORDER for k, a, b in _shard_pieces(*IN_SEGMENTS[seg])]
                            + [jnp.zeros((D_MODEL, LANES - SSD_HEADS), BF16)], axis=1)
    conv_w8 = jnp.pad(_join_shards("conv_w", st_conv), ((0, SUBLANES - SSD_CONV), (0, 0)))
    (h, xbc_raw, graw, z, u5, dtr), (st_branch, st_out, st_glu) = rms_matmul(
        x, p["norm_mix_g"], win_p, IN_SPLITS, tm=tb, name="in_proj", out_dtypes=(BF16, BF16, BF16, BF16, F32),
        comm=gathers([bf("w_branch"), bf("w_out"), bf("s5_glu_w")]))
    w_branch = st_branch.reshape(D_MODEL + S5_WIDTH, D_MODEL)
    wout, wglu = st_out.reshape(D_MODEL, D_MODEL), st_glu.reshape(S5_WIDTH, S5_WIDTH)
    bias_row, alog_row = _lane_row(p["dt_bias"]), _lane_row(p["a_log"])
    d_row = jnp.repeat(p["d_ssd"], SSD_HEADDIM).reshape(1, SSD_HEADS * SSD_HEADDIM)
    (ya, yssd, states), (w1,) = ssd_fwd(xbc_raw, conv_w8, p["conv_b"], dtr, z, bias_row, alog_row, d_row,
                                        p["ssd_norm_g"], name="ssd_fwd", comm=gathers([bf("w_mlp_in")]))

    n = S5_N
    a_re_c, a_im_c = p["s5_a_re"].reshape(n, 1), p["s5_a_im"].reshape(n, 1)
    dt_c = jnp.repeat(p["s5_log_dt"], S5_STATE).reshape(n, 1)
    b_re_m, b_im_m = p["s5_b_re"].reshape(n, S5_GROUP), p["s5_b_im"].reshape(n, S5_GROUP)
    ab, bb_re, bb_im = s5_discretize(a_re_c, a_im_c, dt_c, b_re_m, b_im_m, name="s5_disc")
    ab8 = jnp.pad(ab.T, ((0, SUBLANES - 2), (0, 0)))
    bbc = jnp.concatenate([_block_diag_b(bb_re.reshape(S5_GROUPS, S5_STATE, S5_GROUP)),
                           _block_diag_b(bb_im.reshape(S5_GROUPS, S5_STATE, S5_GROUP))], axis=1).astype(BF16)
    ccc = jnp.concatenate([_block_diag_c(p["s5_c_re"]), -_block_diag_c(p["s5_c_im"])], axis=0).astype(BF16)
    s5d_row = p["s5_d"].reshape(1, S5_WIDTH)
    bglu_row = p["s5_glu_b"].reshape(1, S5_WIDTH)
    u5 = s5_interleave(u5, ts)
    (s, ys5, yb), (st_w2,) = s5_fwd(u5, bbc, ccc, ab8, s5d_row, wglu, bglu_row, tm=ts, name="s5_fwd",
                                    comm=gathers([bf("w_mlp_out")]))
    yb = s5_deinterleave(yb, ts)
    w2 = st_w2.reshape(D_FF, D_MODEL)

    x1, pa, pb = merge_fwd(ya, yb, graw, x, w_branch, wout, tm=tb, name="merge_fwd")
    (h2, a1), _ = rms_matmul(x1, p["norm_mlp_g"], w1, (D_FF,), tm=tb, name="mlp_in", out_dtypes=(BF16,))
    dx2, dx2b, loss_row, dgf = mlp_out_loss(a1, x1, w2, p["norm_final_g"], target, tm=tb, name="mlp_out_loss")

    g = {}
    g["norm_final_g"] = dgf[0]
    da1 = mlp_bwd_act(dx2b, a1, w2, tm=tb, name="mlp_bwd_act")
    dw2 = slots("w_mlp_out", matmul_tn(a1, dx2b, out_dtype=BF16, a_relu2=True, name="dw_mlp_out"))
    dw1 = matmul_tn(h2, da1, out_dtype=BF16, col_shards=N_DEV, name="dw_mlp_in")
    (dx1, dx1b, dgm), _ = matmul_nt_rms_bwd([da1], w1, x1, p["norm_mlp_g"], dx2, tm=tb, name="mlp_in_bwd")
    g["norm_mlp_g"] = dgm[0]
    merged, dgraw, dpa, dpb, dya, dyb = merge_bwd(dx1b, graw, pa, pb, w_branch, wout, tm=tm, name="merge_bwd")
    dwo = slots("w_out", matmul_tn(merged, dx1b, out_dtype=BF16, name="dw_out"))
    dwb = slots("w_branch", jnp.concatenate([matmul_tn(ya, dpa, out_dtype=BF16, name="dw_branch_a"),
                                             matmul_tn(yb, dpb, out_dtype=BF16, name="dw_branch_b")], axis=0))

    (du5, dbbc, dccc, dab, dd5, dwglu, dbglu), got = s5_bwd(
        s5_interleave(dyb, ts), ys5, u5, s, bbc, ccc, ab8, s5d_row, wglu, bglu_row, tm=ts, name="s5_bwd",
        comm=exchanges([dw2, dw1]))
    du5 = s5_deinterleave(du5, ts)
    update(["w_mlp_out", "w_mlp_in"], got)
    g["s5_glu_b"], g["s5_d"] = dbglu[0], dd5[0]
    g["s5_c_re"] = _diag_blocks_c(dccc[:S5_BS])
    g["s5_c_im"] = -_diag_blocks_c(dccc[S5_BS:])
    dbb_re = _diag_blocks_b(dbbc[:, :S5_BS]).reshape(n, S5_GROUP)
    dbb_im = _diag_blocks_b(dbbc[:, S5_BS:]).reshape(n, S5_GROUP)
    da, dlogdt, gb_re, gb_im = s5_discretize_bwd(a_re_c, a_im_c, dt_c, b_re_m, b_im_m, dab[:2].T, dbb_re, dbb_im,
                                                  name="s5_disc_bwd")
    g["s5_a_re"] = da[:, 0].reshape(S5_GROUPS, S5_STATE)
    g["s5_a_im"] = da[:, 1].reshape(S5_GROUPS, S5_STATE)
    g["s5_log_dt"] = dlogdt[:, 0]
    g["s5_b_re"] = gb_re.reshape(S5_GROUPS, S5_STATE, S5_GROUP)
    g["s5_b_im"] = gb_im.reshape(S5_GROUPS, S5_STATE, S5_GROUP)

    def update_small(names, got, name, totals=()):
        (gs, ds, nm, nv), sums = adamw_whole([p[n] for n in names], [m[n] for n in names], [v[n] for n in names], got,
                                             name=name, totals=totals)
        for k, n in enumerate(names):
            updated[n] = (gs[k], ds[k], nm[k], nv[k])
        return sums

    wide = ["s5_b_re", "s5_b_im"]
    late = ["ssd_norm_g", "d_ssd", "a_log", "dt_bias", "conv_b"]
    early = [n for n in SMALL_ORDER if n not in wide + late and n != "norm_mix_g"]
    (dz, dxbc_raw, ddtr, dng, dd_row, dalog_row, dbias_row, dconv_b, dconv_w8), got = ssd_bwd(
        dya, yssd, z, xbc_raw, conv_w8, p["conv_b"], dtr, states, bias_row, alog_row, d_row, p["ssd_norm_g"],
        name="ssd_bwd",
        comm=exchanges([dwo, dwb, slots("s5_glu_w", dwglu.astype(BF16))])
        + gathers([g[n] for n in wide + early] + [loss_row]))
    update(["w_out", "w_branch", "s5_glu_w"], got[:3])
    update_small(wide, got[3:3 + len(wide)], "adamw_s5_b")
    loss_sum, = update_small(early, got[3 + len(wide):-1], "adamw_small_early", totals=[got[-1]])
    g["ssd_norm_g"] = dng[0]
    g["d_ssd"], g["a_log"], g["dt_bias"] = dd_row[0, :SSD_HEADS], dalog_row[0, :SSD_HEADS], dbias_row[0, :SSD_HEADS]
    g["conv_b"] = dconv_b[0]
    dconv = dconv_w8[:SSD_CONV].reshape(SSD_CONV, N_DEV, CONV_DIM // N_DEV).transpose(1, 0, 2)
    dproj =[dxbc_raw, dgraw, dz, du5, ddtr]
    dxbc_w, dgate_w, dz_w, du5_w, ddt_w = [matmul_tn(h, piece, out_dtype=BF16, name=f"dw_in_{k}")
                                           for k, piece in enumerate(dproj)]
    by_segment = {"z": dz_w, "xbc": dxbc_w, "dt": ddt_w, "u5": du5_w, "gates": dgate_w}
    width = D_IN_PROJ // N_DEV
    dw_in = jnp.stack([jnp.concatenate(
        [by_segment[seg][:, max(k * width, lo) - lo:min((k + 1) * width, hi) - lo]
         for seg, (lo, hi) in IN_SEGMENTS.items() if max(k * width, lo) < min((k + 1) * width, hi)], axis=1)
        for k in range(N_DEV)])

    (grad_x, _, dgx), got = matmul_nt_rms_bwd(
        dproj, win_p, x, p["norm_mix_g"], dx1, tm=tm, name="in_proj_bwd",
        comm=exchanges([dw_in, dconv]) + gathers([g[n] for n in late]))
    update(["w_in", "conv_w"], got[:2])
    update_small(late, got[2:], "adamw_small_late")
    update_small(["norm_mix_g"], all_to_all(gathers([dgx[0]]), name="gather_last"), "adamw_small_last")
    return loss_sum[0, 0], grad_x, updated


WEIGHT_ORDER = ["norm_mix_g", "w_in", "conv_w", "conv_b", "dt_bias", "a_log", "d_ssd", "ssd_norm_g", "s5_a_re",
                "s5_a_im", "s5_log_dt", "s5_b_re", "s5_b_im", "s5_c_re", "s5_c_im", "s5_d", "s5_glu_w", "s5_glu_b",
                "w_branch", "w_out", "norm_mlp_g", "w_mlp_in", "w_mlp_out", "norm_final_g"]
SHARDED = {"w_in": ((D_MODEL, D_IN_PROJ), 1), "conv_w": ((SSD_CONV, CONV_DIM), 1), "s5_glu_w": ((S5_WIDTH, S5_WIDTH), 0),
           "w_branch": ((D_MODEL + S5_WIDTH, D_MODEL), 0), "w_out": ((D_MODEL, D_MODEL), 0),
           "w_mlp_in": ((D_MODEL, D_FF), 1), "w_mlp_out": ((D_FF, D_MODEL), 0)}
SHARDED_ORDER = ["w_in", "conv_w", "s5_glu_w", "w_branch", "w_out", "w_mlp_in", "w_mlp_out"]
SMALL_ORDER = [n for n in WEIGHT_ORDER if n not in SHARDED]


def _shard_shape(name):
    (r, c), ax = SHARDED[name]
    return (r, c // N_DEV) if ax == 1 else (r // N_DEV, c)


def _join_shards(name, stacked):
    (r, c), ax = SHARDED[name]
    if ax == 1:
        return stacked.transpose(1, 0, 2).reshape(r, c)
    return stacked.reshape(r, c)


def _shard_pieces(lo, hi):
    width = D_IN_PROJ // N_DEV
    out = []
    while lo < hi:
        k = lo // width
        end = min(hi, (k + 1) * width)
        out.append((k, lo - k * width, end - k * width))
        lo = end
    return out


IN_SEGMENTS = {"z": (0, 1024), "xbc": (1024, 3072), "dt": (3072, 3088), "u5": (3088, 3600), "gates": (3600, 5648)}
IN_PADDED_ORDER = ("xbc", "gates", "z", "u5", "dt")


def kernel(x, norm_mix_g, w_in, conv_w, conv_b, dt_bias, a_log, d_ssd, ssd_norm_g, s5_a_re, s5_a_im, s5_log_dt, s5_b_re, s5_b_im, s5_c_re, s5_c_im, s5_d, s5_glu_w, s5_glu_b, w_branch, w_out, norm_mlp_g, w_mlp_in, w_mlp_out, norm_final_g, loss_target, m_norm_mix_g, m_w_in, m_conv_w, m_conv_b, m_dt_bias, m_a_log, m_d_ssd, m_ssd_norm_g, m_s5_a_re, m_s5_a_im, m_s5_log_dt, m_s5_b_re, m_s5_b_im, m_s5_c_re, m_s5_c_im, m_s5_d, m_s5_glu_w, m_s5_glu_b, m_w_branch, m_w_out, m_norm_mlp_g, m_w_mlp_in, m_w_mlp_out, m_norm_final_g, v_norm_mix_g, v_w_in, v_conv_w, v_conv_b, v_dt_bias, v_a_log, v_d_ssd, v_ssd_norm_g, v_s5_a_re, v_s5_a_im, v_s5_log_dt, v_s5_b_re, v_s5_b_im, v_s5_c_re, v_s5_c_im, v_s5_d, v_s5_glu_w, v_s5_glu_b, v_w_branch, v_w_out, v_norm_mlp_g, v_w_mlp_in, v_w_mlp_out, v_norm_final_g):
    w = dict(norm_mix_g=norm_mix_g, w_in=w_in, conv_w=conv_w, conv_b=conv_b, dt_bias=dt_bias, a_log=a_log, d_ssd=d_ssd,
             ssd_norm_g=ssd_norm_g, s5_a_re=s5_a_re, s5_a_im=s5_a_im, s5_log_dt=s5_log_dt, s5_b_re=s5_b_re,
             s5_b_im=s5_b_im, s5_c_re=s5_c_re, s5_c_im=s5_c_im, s5_d=s5_d, s5_glu_w=s5_glu_w, s5_glu_b=s5_glu_b,
             w_branch=w_branch, w_out=w_out, norm_mlp_g=norm_mlp_g, w_mlp_in=w_mlp_in, w_mlp_out=w_mlp_out,
             norm_final_g=norm_final_g)
    m = dict(norm_mix_g=m_norm_mix_g, w_in=m_w_in, conv_w=m_conv_w, conv_b=m_conv_b, dt_bias=m_dt_bias, a_log=m_a_log,
             d_ssd=m_d_ssd, ssd_norm_g=m_ssd_norm_g, s5_a_re=m_s5_a_re, s5_a_im=m_s5_a_im, s5_log_dt=m_s5_log_dt,
             s5_b_re=m_s5_b_re, s5_b_im=m_s5_b_im, s5_c_re=m_s5_c_re, s5_c_im=m_s5_c_im, s5_d=m_s5_d,
             s5_glu_w=m_s5_glu_w, s5_glu_b=m_s5_glu_b, w_branch=m_w_branch, w_out=m_w_out, norm_mlp_g=m_norm_mlp_g,
             w_mlp_in=m_w_mlp_in, w_mlp_out=m_w_mlp_out, norm_final_g=m_norm_final_g)
    v = dict(norm_mix_g=v_norm_mix_g, w_in=v_w_in, conv_w=v_conv_w, conv_b=v_conv_b, dt_bias=v_dt_bias, a_log=v_a_log,
             d_ssd=v_d_ssd, ssd_norm_g=v_ssd_norm_g, s5_a_re=v_s5_a_re, s5_a_im=v_s5_a_im, s5_log_dt=v_s5_log_dt,
             s5_b_re=v_s5_b_re, s5_b_im=v_s5_b_im, s5_c_re=v_s5_c_re, s5_c_im=v_s5_c_im, s5_d=v_s5_d,
             s5_glu_w=v_s5_glu_w, s5_glu_b=v_s5_glu_b, w_branch=v_w_branch, w_out=v_w_out, norm_mlp_g=v_norm_mlp_g,
             w_mlp_in=v_w_mlp_in, w_mlp_out=v_w_mlp_out, norm_final_g=v_norm_final_g)

    loss, grad_x, upd = train_step(x[0], loss_target[0], w, m, v)
    return (loss, grad_x.reshape(x.shape), *[upd[n][j] for j in range(4) for n in WEIGHT_ORDER])
```

```python
import math

import jax
import jax.numpy as jnp
from jax import lax
from jax.experimental import pallas as pl
from jax.experimental.pallas import tpu as pltpu

F32 = jnp.float32
BF16 = jnp.bfloat16

N_DEV = 8
D_MODEL = 1024
SSD_HEADS = 16
SSD_HEADDIM = 64
SSD_GROUPS = 4
SSD_HPG = 4
SSD_STATE = 128
SSD_CHUNK = 128
SSD_CONV = 4
CONV_DIM = 2048
S5_WIDTH = 512
S5_GROUP = 16
S5_GROUPS = 32
S5_STATE = 64
S5_N = S5_GROUPS * S5_STATE
D_FF = 4096
D_IN_PROJ = 5648
IN_SPLITS = (2048, 2048, 1024, 512, 128)
EPS = 1e-6
LANES = 128
SUBLANES = 8
VMEM_LIMIT = 56 * 1024 * 1024

ADAM_LR = 0.001
ADAM_B1 = 0.9
ADAM_B2 = 0.999
ADAM_EPS = 1e-08
ADAM_WD = 0.01
ADAM_STEP = 10

GELU_C = math.sqrt(2.0 / math.pi)
GELU_K = 0.044715


def _params(*sem):
    return pltpu.CompilerParams(dimension_semantics=sem, vmem_limit_bytes=VMEM_LIMIT)


def _full(shape):
    nd = len(shape)
    return pl.BlockSpec(shape, lambda *_: (0,) * nd)


def _resident(shape):
    nd = len(shape)
    return pl.BlockSpec(shape, lambda *_: (0,) * nd, pipeline_mode=pl.Buffered(1))


_MESH = pl.DeviceIdType.MESH
_RELATIONS = [(dx, dy, dc) for dx in (0, 1) for dy in (0, 1) for dc in (0, 1)][1:]
N_REL = len(_RELATIONS)


def _place():
    x, y, c = lax.axis_index("x"), lax.axis_index("y"), lax.axis_index("c")
    return x, y, c, 4 * x + 2 * y + c


def gathers(arrays):
    return [("gather", a) for a in arrays]


def exchanges(arrays):
    return [("exchange", a) for a in arrays]


def _comm_out_shapes(comm):
    return [jax.ShapeDtypeStruct(((N_DEV,) if kind == "gather" else ()) + a.shape, a.dtype) for kind, a in comm]


def _comm_copies(kinds, src_refs, dst_refs, send_sems, recv_sems, local_sems):
    x, y, c, idx = _place()
    local, remote = [], []
    for a, (kind, src, dst) in enumerate(zip(kinds, src_refs, dst_refs)):
        local.append(pltpu.make_async_copy(src if kind == "gather" else src.at[idx], dst.at[idx], local_sems.at[a]))
        for k, (dx, dy, dc) in enumerate(_RELATIONS):
            px, py, pc = x ^ dx, y ^ dy, c ^ dc
            remote.append(pltpu.make_async_remote_copy(
                src_ref=src if kind == "gather" else src.at[4 * px + 2 * py + pc], dst_ref=dst.at[idx],
                send_sem=send_sems.at[N_REL * a + k], recv_sem=recv_sems.at[N_REL * a + k],
                device_id=(px, py, pc), device_id_type=_MESH))
    return local, remote


def _tiled_call(body, *, name, grid, in_specs, out_specs, out_shape, operands, scratch_shapes=(), comm=None):
    params = pltpu.CompilerParams(dimension_semantics=("arbitrary",), vmem_limit_bytes=VMEM_LIMIT,
                                  has_side_effects=bool(comm))
    if not comm:
        outs = pl.pallas_call(body, name=name, grid=grid, in_specs=in_specs, out_specs=out_specs, out_shape=out_shape,
                              scratch_shapes=list(scratch_shapes), compiler_params=params)(*operands)
        return outs, []
    kind = [k for k, _ in comm]
    arrays = [a for _, a in comm]
    na, n_in, n_out, n_scr = len(arrays), len(in_specs), len(out_specs), len(scratch_shapes)
    last = grid[0] - 1

    def hosted(*refs):
        ins, refs = refs[:n_in], refs[n_in:]
        cin, refs = refs[:na], refs[na:]
        outs, refs = refs[:n_out], refs[n_out:]
        cout, refs = refs[:na], refs[na:]
        scr, sems = refs[:n_scr], refs[n_scr:]
        i = pl.program_id(0)

        @pl.when(i == 0)
        def _():
            local, remote = _comm_copies(kind, cin, cout, *sems)
            for cp in local + remote:
                cp.start()

        body(*ins, *outs, *scr)

        @pl.when(i == last)
        def _():
            local, remote = _comm_copies(kind, cin, cout, *sems)
            for cp in remote:
                cp.wait_recv()
            for cp in remote:
                cp.wait_send()
            for cp in local:
                cp.wait()

    any_spec = pl.BlockSpec(memory_space=pl.ANY)
    res = pl.pallas_call(
        hosted, name=name, grid=grid,
        in_specs=list(in_specs) + [any_spec] * na, out_specs=list(out_specs) + [any_spec] * na,
        out_shape=list(out_shape) + _comm_out_shapes(comm),
        scratch_shapes=list(scratch_shapes) + [pltpu.SemaphoreType.DMA((N_REL * na,)), pltpu.SemaphoreType.DMA((N_REL * na,)),
                                               pltpu.SemaphoreType.DMA((na,))],
        compiler_params=params)(*operands, *arrays)
    return res[:n_out], res[n_out:]


def _dot(a, b):
    return jnp.dot(a, b, preferred_element_type=F32)


def _dot_nt(a, b):
    return lax.dot_general(a, b, (((1,), (1,)), ((), ())), preferred_element_type=F32)


def _dot_tn(a, b):
    return lax.dot_general(a, b, (((0,), (0,)), ((), ())), preferred_element_type=F32)


def _split2(x):
    hi = x.astype(BF16)
    return hi, (x - hi.astype(F32)).astype(BF16)


def _select_dot(dot, a, b, exact):
    if exact == "a":
        ea, (hi, lo) = a.astype(BF16), _split2(b)
        return dot(ea, hi) + dot(ea, lo)
    eb, (hi, lo) = b.astype(BF16), _split2(a)
    return dot(hi, eb) + dot(lo, eb)


def _dot_sel_a(a, b):
    return _select_dot(_dot, a, b, "a")


def _dot_sel_b(a, b):
    return _select_dot(_dot, a, b, "b")


def _dot_nt_sel_a(a, b):
    return _select_dot(_dot_nt, a, b, "a")


def _dot_tn_sel(a, b, exact):
    return _select_dot(_dot_tn, a, b, exact)


def _sigmoid(x):
    return 1.0 / (1.0 + jnp.exp(-x))


def _softplus(x):
    return jnp.maximum(x, 0.0) + jnp.log(1.0 + jnp.exp(-jnp.abs(x)))


def _gelu(x):
    return 0.5 * x * (1.0 + jnp.tanh(GELU_C * (x + GELU_K * x * x * x)))


def _gelu_grad(x):
    th = jnp.tanh(GELU_C * (x + GELU_K * x * x * x))
    return 0.5 * (1.0 + th) + 0.5 * x * (1.0 - th * th) * GELU_C * (1.0 + 3.0 * GELU_K * x * x)


def rms_matmul(x, g, w, splits, *, tm, name, comm=None, out_dtypes=None):
    t, d = x.shape
    stacked = w.ndim == 3
    n = w.shape[0] * w.shape[2] if stacked else w.shape[1]
    assert sum(splits) == n and (len(splits) == 1 or not stacked)

    def body(x_ref, g_ref, w_ref, h_ref, *o_refs):
        xv = x_ref[...]
        r = lax.rsqrt(jnp.mean(xv * xv, axis=-1, keepdims=True) + EPS)
        h = (xv * r * g_ref[...]).astype(BF16)
        h_ref[...] = h
        if stacked:
            wk = w.shape[2]
            for k in range(w.shape[0]):
                o_refs[0][:, k * wk:(k + 1) * wk] = _dot(h, w_ref[k]).astype(o_refs[0].dtype)
            return
        off = 0
        for o_ref, width in zip(o_refs, splits):
            o_ref[...] = _dot(h, w_ref[:, off:off + width]).astype(o_ref.dtype)
            off += width

    tile = lambda c: pl.BlockSpec((tm, c), lambda i: (i, 0))
    dtypes = out_dtypes or (F32,) * len(splits)
    return _tiled_call(
        body, name=name, grid=(t // tm,),
        in_specs=[tile(d), _full((1, d)), _resident(w.shape)],
        out_specs=[tile(d)] + [tile(c) for c in splits],
        out_shape=[jax.ShapeDtypeStruct((t, d), BF16)] + [jax.ShapeDtypeStruct((t, c), dt) for c, dt in zip(splits, dtypes)],
        operands=(x, g.reshape(1, d), w), comm=comm)


MATMUL_TN_ACC_BYTES = 16 * 1024 * 1024


def _pick(n, cap):
    best = LANES
    for c in range(LANES, cap + 1, LANES):
        if n % c == 0:
            best = c
    return best


def matmul_tn(a, b, *, name, out_dtype=F32, col_shards=None, tk=1024, a_relu2=False):
    t, m = a.shape
    n = b.shape[1]
    tm = _pick(m, 1024)
    whole_n = tm * n * 4 <= MATMUL_TN_ACC_BYTES
    tn = n if whole_n else _pick(n, 1280)
    assert whole_n or not col_shards
    tk = min(tk if tn <= 2048 else tk // 2, t)
    nk = t // tk

    def body(a_ref, b_ref, o_ref, acc):
        k = pl.program_id(2)

        @pl.when(k == 0)
        def _():
            acc[...] = jnp.zeros_like(acc)

        av = a_ref[...]
        if a_relu2:
            pos = jnp.maximum(av.astype(F32), 0.0)
            av = (pos * pos).astype(BF16)
        acc[...] += _dot_tn(av, b_ref[...])

        @pl.when(k == nk - 1)
        def _():
            if col_shards:
                w = n // col_shards
                for s in range(col_shards):
                    o_ref[s] = acc[:, s * w:(s + 1) * w].astype(out_dtype)
            else:
                o_ref[...] = acc[...].astype(out_dtype)

    if col_shards:
        out_spec = pl.BlockSpec((col_shards, tm, n // col_shards), lambda i, j, k: (0, i, 0))
        out_shape = jax.ShapeDtypeStruct((col_shards, m, n // col_shards), out_dtype)
    else:
        out_spec = pl.BlockSpec((tm, tn), lambda i, j, k: (i, j))
        out_shape = jax.ShapeDtypeStruct((m, n), out_dtype)
    return pl.pallas_call(
        body, name=name, grid=(m // tm, n // tn, nk),
        in_specs=[pl.BlockSpec((tk, tm), lambda i, j, k: (k, i)), pl.BlockSpec((tk, tn), lambda i, j, k: (k, j))],
        out_specs=out_spec, out_shape=out_shape,
        scratch_shapes=[pltpu.VMEM((tm, tn), F32)],
        compiler_params=_params("parallel", "parallel", "arbitrary"),
    )(a, b)


def matmul_nt_rms_bwd(dys, w, x, g, dres, *, tm, name, comm=None):
    t = x.shape[0]
    stacked = w.ndim == 3
    d = w.shape[1] if stacked else w.shape[0]
    widths = [dy.shape[1] for dy in dys]
    n_dy = len(dys)

    def body(*refs):
        dy_refs = refs[:n_dy]
        w_ref, x_ref, g_ref, dres_ref, dx_ref, dxb_ref, dg_ref = refs[n_dy:]
        i = pl.program_id(0)
        if stacked:
            wk = w.shape[2]
            dh = _dot_nt(dy_refs[0][:, 0:wk], w_ref[0])
            for k in range(1, w.shape[0]):
                dh = dh + _dot_nt(dy_refs[0][:, k * wk:(k + 1) * wk], w_ref[k])
        else:
            dh, off = None, 0
            for dy_ref, width in zip(dy_refs, widths):
                part = _dot_nt(dy_ref[...], w_ref[:, off:off + width])
                dh = part if dh is None else dh + part
                off += width
        xv = x_ref[...]
        r = lax.rsqrt(jnp.mean(xv * xv, axis=-1, keepdims=True) + EPS)
        xn = xv * r
        dxn = dh * g_ref[...]
        dx = dres_ref[...] + r * (dxn - xn * jnp.mean(dxn * xn, axis=-1, keepdims=True))
        dx_ref[...] = dx
        dxb_ref[...] = dx.astype(BF16)

        @pl.when(i == 0)
        def _():
            dg_ref[...] = jnp.zeros_like(dg_ref)

        dg_ref[...] += jnp.sum(dh * xn, axis=0, keepdims=True)

    tile = lambda c: pl.BlockSpec((tm, c), lambda i: (i, 0))
    return _tiled_call(
        body, name=name, grid=(t // tm,),
        in_specs=[tile(c) for c in widths] + [_resident(w.shape), tile(d), _full((1, d)), tile(d)],
        out_specs=[tile(d), tile(d), _full((1, d))],
        out_shape=[jax.ShapeDtypeStruct((t, d), F32), jax.ShapeDtypeStruct((t, d), BF16),
                   jax.ShapeDtypeStruct((1, d), F32)],
        operands=(*dys, w, x, g.reshape(1, d), dres), comm=comm)


def _conv_windows(ext_ref, tm):
    ext = ext_ref[...]
    return [pltpu.roll(ext, 3 - k, 0)[SUBLANES:SUBLANES + tm, :] for k in range(3)] + [ext[SUBLANES:SUBLANES + tm, :]]


def _conv_taps(windows, w_ref):
    acc = windows[3] * w_ref[3:4, :]
    for k in range(3):
        acc = acc + windows[k] * w_ref[k:k + 1, :]
    return acc


def _ssd_chunk_terms(dtr, bias_row, alog_row):
    q = SSD_CHUNK
    row = lax.broadcasted_iota(jnp.int32, (q, q), 0)
    col = lax.broadcasted_iota(jnp.int32, (q, q), 1)
    ltri = (col <= row).astype(F32)
    dt = _softplus(dtr + bias_row)
    a_row = -jnp.exp(alog_row)
    la = dt * a_row
    cum = _dot_sel_a(ltri, la)
    cum_t = _dot_tn_sel(la, (row <= col).astype(F32), "b")
    return dt, a_row, cum, cum_t, ltri, row, col


def _head_maps():
    di = SSD_HEADS * SSD_HEADDIM
    expand = (lax.broadcasted_iota(jnp.int32, (LANES, di), 1) // SSD_HEADDIM
              == lax.broadcasted_iota(jnp.int32, (LANES, di), 0)).astype(F32)
    collapse = (lax.broadcasted_iota(jnp.int32, (di, LANES), 0) // SSD_HEADDIM
                == lax.broadcasted_iota(jnp.int32, (di, LANES), 1)).astype(F32)
    return expand, collapse


def _ssd_decay(cum, cum_t, h, causal):
    seg = cum[:, h:h + 1] - cum_t[h:h + 1, :]
    return jnp.where(causal, jnp.exp(jnp.where(causal, seg, 0.0)), 0.0)


def _pair_block_diag(x2):
    low = lax.broadcasted_iota(jnp.int32, x2.shape, 1) < SSD_HEADDIM
    zero = jnp.zeros_like(x2)
    return jnp.concatenate([jnp.where(low, x2, zero), jnp.where(low, zero, x2)], axis=0)


def ssd_fwd(xbc_raw, conv_w8, conv_b, dtr, z, bias_row, alog_row, d_row, norm_g, *, name, comm=None):
    t = xbc_raw.shape[0]
    q, p, n = SSD_CHUNK, SSD_HEADDIM, SSD_STATE
    nc = t // q
    di = SSD_HEADS * p
    gw = di // SSD_GROUPS

    def body(raw_ref, cw_ref, cb_ref, dtr_ref, z_ref, bias_ref, alog_ref, d_ref, g_ref,
             ya_ref, yssd_ref, st_ref, state, ext, xbc_s):
        ci = pl.program_id(0)

        @pl.when(ci == 0)
        def _():
            state[...] = jnp.zeros_like(state)
            ext[0:SUBLANES, :] = jnp.zeros((SUBLANES, CONV_DIM), F32)

        ext[SUBLANES:SUBLANES + q, :] = raw_ref[...].astype(F32)
        xc = _conv_taps(_conv_windows(ext, q), cw_ref) + cb_ref[...]
        ext[0:SUBLANES, :] = ext[q:q + SUBLANES, :]
        xbc_s[...] = xc * _sigmoid(xc)
        st_ref[0] = state[...]
        dt, _, cum, cum_t, _, row, col = _ssd_chunk_terms(dtr_ref[...], bias_ref[...], alog_ref[...])
        causal = row >= col
        expand, _ = _head_maps()
        dt_full = _dot_sel_b(dt, expand)
        cum_full = _dot_sel_b(cum, expand)
        last_full = cum_full[q - 1:q, :]
        xs = xbc_s[:, 0:di]
        xdt = xs * dt_full
        xd = (xdt * jnp.exp(last_full - cum_full)).astype(BF16)
        xdt_b = xdt.astype(BF16)
        e_full = jnp.exp(cum_full)
        eend_full = jnp.exp(last_full)
        for g in range(SSD_GROUPS):
            slab = slice(g * gw, (g + 1) * gw)
            bg = xbc_s[:, di + g * n:di + (g + 1) * n].astype(BF16)
            cg = xbc_s[:, di + (SSD_GROUPS + g) * n:di + (SSD_GROUPS + g + 1) * n].astype(BF16)
            cb = _dot_nt(cg, bg)
            st_g = state[:, slab]
            pairs = []
            for pr in range(SSD_HPG // 2):
                h0 = g * SSD_HPG + 2 * pr
                m0 = (cb * _ssd_decay(cum, cum_t, h0, causal)).astype(BF16)
                m1 = (cb * _ssd_decay(cum, cum_t, h0 + 1, causal)).astype(BF16)
                pairs.append(_dot(jnp.concatenate([m0, m1], axis=1), _pair_block_diag(xdt_b[:, h0 * p:(h0 + 2) * p])))
            y = jnp.concatenate(pairs, axis=1) + e_full[:, slab] * _dot(cg, st_g.astype(BF16))
            yssd_ref[:, slab] = y + d_ref[:, slab] * xs[:, slab]
            state[:, slab] = eend_full[:, slab] * st_g + _dot_tn(bg, xd[:, slab])
        zv = z_ref[...].astype(F32)
        ya1 = yssd_ref[...] * (zv * _sigmoid(zv))
        for g in range(SSD_GROUPS):
            gsl = slice(g * gw, (g + 1) * gw)
            sl = ya1[:, gsl]
            rg = lax.rsqrt(jnp.mean(sl * sl, axis=-1, keepdims=True) + EPS)
            ya_ref[:, gsl] = (sl * rg * g_ref[:, gsl]).astype(BF16)

    blk = lambda w, j: pl.BlockSpec((q, w), lambda c: (c, j))
    return _tiled_call(
        body, name=name, grid=(nc,),
        in_specs=[blk(CONV_DIM, 0), _full((SUBLANES, CONV_DIM)), _full((1, CONV_DIM)), blk(LANES, 0), blk(di, 0),
                  _full((1, LANES)), _full((1, LANES)), _full((1, di)), _full((1, di))],
        out_specs=[blk(di, 0), blk(di, 0), pl.BlockSpec((1, n, di), lambda c: (c, 0, 0))],
        out_shape=[jax.ShapeDtypeStruct((t, di), BF16), jax.ShapeDtypeStruct((t, di), F32),
                   jax.ShapeDtypeStruct((nc, n, di), F32)],
        scratch_shapes=[pltpu.VMEM((n, di), F32), pltpu.VMEM((SUBLANES + q, CONV_DIM), F32), pltpu.VMEM((q, CONV_DIM), F32)],
        operands=(xbc_raw, conv_w8, conv_b.reshape(1, CONV_DIM), dtr, z, bias_row, alog_row, d_row,
                  norm_g.reshape(1, di)), comm=comm)


def ssd_bwd(dya, yssd, z, xbc_raw, conv_w8, conv_b, dtr, states, bias_row, alog_row, d_row, norm_g, *, name, comm=None):
    t = xbc_raw.shape[0]
    q, p, n = SSD_CHUNK, SSD_HEADDIM, SSD_STATE
    nc = t // q
    di = SSD_HEADS * p
    gw = di // SSD_GROUPS
    per = q // (2 * SUBLANES)

    def body(dya_ref, yssd_ref, z_ref, raw_ref, prev_ref, cw_ref, cb_ref, dtr_ref, st_ref, bias_ref, alog_ref, d_ref,
             g_ref, dz_ref, draw_ref, ddtr_ref, dng_ref, dd_ref, dalog_ref, dbias_ref, dcb_ref, dcw_ref,
             dstate, dyssd, xbc_s, dxbc_ref, ext, aft):
        ci = pl.program_id(0)

        @pl.when(ci == 0)
        def _():
            dstate[...] = jnp.zeros_like(dstate)
            aft[q:q + SUBLANES, :] = jnp.zeros((SUBLANES, CONV_DIM), F32)
            for r in (dng_ref, dd_ref, dalog_ref, dbias_ref, dcb_ref, dcw_ref):
                r[...] = jnp.zeros_like(r)

        ext[0:SUBLANES, :] = jnp.where(ci == nc - 1, 0.0, prev_ref[SUBLANES:2 * SUBLANES, :].astype(F32))
        ext[SUBLANES:SUBLANES + q, :] = raw_ref[...].astype(F32)
        windows = _conv_windows(ext, q)
        xc = _conv_taps(windows, cw_ref) + cb_ref[...]
        sig_c = _sigmoid(xc)
        xbc_s[...] = xc * sig_c

        zv = z_ref[...].astype(F32)
        sz = _sigmoid(zv)
        silu_z = zv * sz
        yv = yssd_ref[...]
        ya1 = yv * silu_z
        dyav = dya_ref[...]
        for g in range(SSD_GROUPS):
            gsl = slice(g * gw, (g + 1) * gw)
            sl = ya1[:, gsl]
            rg = lax.rsqrt(jnp.mean(sl * sl, axis=-1, keepdims=True) + EPS)
            ya2 = sl * rg
            dy_g = dyav[:, gsl]
            dng_ref[:, gsl] += jnp.sum(dy_g * ya2, axis=0, keepdims=True)
            dya2 = dy_g * g_ref[:, gsl]
            dya1 = rg * (dya2 - ya2 * jnp.mean(dya2 * ya2, axis=-1, keepdims=True))
            dyssd[:, gsl] = dya1 * silu_z[:, gsl]
            dz_ref[:, gsl] = (dya1 * yv[:, gsl] * (sz[:, gsl] * (1.0 + zv[:, gsl] * (1.0 - sz[:, gsl])))).astype(BF16)

        dtr_v = dtr_ref[...]
        dt, a_row, cum, cum_t, ltri, row, col = _ssd_chunk_terms(dtr_v, bias_ref[...], alog_ref[...])
        causal = row >= col
        expand, collapse = _head_maps()
        lane = lax.broadcasted_iota(jnp.int32, (1, LANES), 1)
        sub = lax.broadcasted_iota(jnp.int32, (LANES, 1), 0)
        is_last = lax.broadcasted_iota(jnp.int32, (q, 1), 0) == q - 1
        low = lax.broadcasted_iota(jnp.int32, (q, 2 * p), 1) < p
        dt_full = _dot_sel_b(dt, expand)
        cum_full = _dot_sel_b(cum, expand)
        last_full = cum_full[q - 1:q, :]
        e_full = jnp.exp(cum_full)
        eend_full = jnp.exp(last_full)
        dend_full = jnp.exp(last_full - cum_full)
        xs = xbc_s[:, 0:di]
        xdt = xs * dt_full
        xdt_b = xdt.astype(BF16)
        xd_b = (xdt * dend_full).astype(BF16)
        dy_all = dyssd[...]
        dy_b = dy_all.astype(BF16)
        dg_b = (dy_all * e_full).astype(BF16)
        dcum_mat = jnp.zeros((q, LANES), F32)
        rmat = jnp.zeros((LANES, q), F32)
        yoff_prod, bds_list, dx_list, dlast_parts = [], [], [], []
        for g in range(SSD_GROUPS):
            slab = slice(g * gw, (g + 1) * gw)
            bg = xbc_s[:, di + g * n:di + (g + 1) * n].astype(BF16)
            cg = xbc_s[:, di + (SSD_GROUPS + g) * n:di + (SSD_GROUPS + g + 1) * n].astype(BF16)
            cb = _dot_nt(cg, bg)
            st_g = st_ref[0, :, slab]
            st_b = st_g.astype(BF16)
            dsn = dstate[:, slab]
            dsn_b = dsn.astype(BF16)
            yoff_prod.append(dy_all[:, slab] * e_full[:, slab] * _dot(cg, st_b))
            dcg = _dot_nt(dg_b[:, slab], st_b)
            dstate[:, slab] = _dot_tn(cg, dg_b[:, slab]) + eend_full[:, slab] * dsn
            bds_list.append(_dot(bg, dsn_b))
            dbg = _dot_nt(xd_b[:, slab], dsn_b)
            dlast_parts.append(jnp.sum(dsn * st_g, axis=0, keepdims=True))
            dcb = jnp.zeros((q, q), F32)
            dx_pairs = []
            for pr in range(SSD_HPG // 2):
                h0 = g * SSD_HPG + 2 * pr
                ps = slice(h0 * p, (h0 + 2) * p)
                l0 = _ssd_decay(cum, cum_t, h0, causal)
                l1 = _ssd_decay(cum, cum_t, h0 + 1, causal)
                m0, m1 = cb * l0, cb * l1
                dyp = dy_b[:, ps]
                zero = jnp.zeros_like(dyp)
                dy0, dy1 = jnp.where(low, dyp, zero), jnp.where(low, zero, dyp)
                dm0 = _dot_nt(dy0, xdt_b[:, ps])
                dm1 = _dot_nt(dy1, xdt_b[:, ps])
                w0, w1 = dm0 * m0, dm1 * m1
                dcum_mat = (dcum_mat + jnp.sum(w0, axis=1, keepdims=True) * (lane == h0).astype(F32)
                            + jnp.sum(w1, axis=1, keepdims=True) * (lane == h0 + 1).astype(F32))
                rmat = (rmat + jnp.where(sub == h0, jnp.sum(w0, axis=0, keepdims=True), 0.0)
                        + jnp.where(sub == h0 + 1, jnp.sum(w1, axis=0, keepdims=True), 0.0))
                dcb = dcb + dm0 * l0 + dm1 * l1
                dx_pairs.append(_dot_tn(jnp.concatenate([m0.astype(BF16), m1.astype(BF16)], axis=0),
                                        jnp.concatenate([dy0, dy1], axis=0)))
            dx_list.append(jnp.concatenate(dx_pairs, axis=1))
            dcb_b = dcb.astype(BF16)
            dxbc_ref[:, di + g * n:di + (g + 1) * n] = dbg + _dot_tn(dcb_b, cg)
            dxbc_ref[:, di + SSD_GROUPS * n + g * n:di + SSD_GROUPS * n + (g + 1) * n] = dcg + _dot(dcb_b, bg)
        bds_all = jnp.concatenate(bds_list, axis=1)
        dx_all = jnp.concatenate(dx_list, axis=1) + dend_full * bds_all
        last_row = cum[q - 1:q, :]
        qv = _dot_sel_b(xdt * bds_all, collapse) * jnp.exp(last_row - cum)
        dcum_mat = dcum_mat + _dot_sel_b(jnp.concatenate(yoff_prod, axis=1), collapse) - qv
        dlast_full = jnp.broadcast_to(jnp.concatenate(dlast_parts, axis=1), (SUBLANES, di))
        dlast_row = _dot_sel_b(dlast_full, collapse)[0:1, :] * jnp.exp(last_row) + jnp.sum(qv, axis=0, keepdims=True)
        dcum_mat = dcum_mat + jnp.where(is_last, dlast_row, 0.0)
        dla = _dot_tn_sel(ltri, dcum_mat, "a") - _dot_nt_sel_a((row <= col).astype(F32), rmat)
        ddt_mat = _dot_sel_b(dx_all * xs, collapse) + dla * a_row
        dxbc_ref[:, 0:di] = d_ref[...] * dy_all + dx_all * dt_full
        ddtr = ddt_mat * _sigmoid(dtr_v + bias_ref[...])
        ddtr_ref[...] = ddtr.astype(BF16)
        dd_full = jnp.broadcast_to(jnp.sum(dy_all * xs, axis=0, keepdims=True), (SUBLANES, di))
        dd_ref[...] += _dot_sel_b(dd_full, collapse)[0:1, :]
        dalog_ref[...] += jnp.sum(dla * dt, axis=0, keepdims=True) * a_row
        dbias_ref[...] += jnp.sum(ddtr, axis=0, keepdims=True)

        dxc = dxbc_ref[...] * (sig_c * (1.0 + xc * (1.0 - sig_c)))
        dcb_ref[...] += jnp.sum(dxc, axis=0, keepdims=True)
        taps = [jnp.sum(dxc * windows[k], axis=0, keepdims=True) for k in range(SSD_CONV)]
        dcw_ref[...] += jnp.concatenate(taps + [jnp.zeros((SUBLANES - SSD_CONV, CONV_DIM), F32)], axis=0)
        aft[0:q, :] = dxc
        after = aft[...]
        draw = dxc * cw_ref[3:4, :]
        for j in range(1, SSD_CONV):
            draw = draw + pltpu.roll(after, q + SUBLANES - j, 0)[0:q, :] * cw_ref[3 - j:4 - j, :]
        draw_ref[...] = draw.astype(BF16)
        aft[q:q + SUBLANES, :] = dxc[0:SUBLANES, :]

    blk = lambda w, j: pl.BlockSpec((q, w), lambda c: (nc - 1 - c, j))
    row_out = lambda w: jax.ShapeDtypeStruct((1, w), F32)
    return _tiled_call(
        body, name=name, grid=(nc,),
        in_specs=[blk(di, 0), blk(di, 0), blk(di, 0), blk(CONV_DIM, 0),
                  pl.BlockSpec((2 * SUBLANES, CONV_DIM), lambda c: (jnp.maximum((nc - 1 - c) * per - 1, 0), 0)),
                  _full((SUBLANES, CONV_DIM)), _full((1, CONV_DIM)), blk(LANES, 0),
                  pl.BlockSpec((1, n, di), lambda c: (nc - 1 - c, 0, 0)),
                  _full((1, LANES)), _full((1, LANES)), _full((1, di)), _full((1, di))],
        out_specs=[blk(di, 0), blk(CONV_DIM, 0), blk(LANES, 0),
                   _full((1, di)), _full((1, LANES)), _full((1, LANES)), _full((1, LANES)),
                   _full((1, CONV_DIM)), _full((SUBLANES, CONV_DIM))],
        out_shape=[jax.ShapeDtypeStruct((t, di), BF16), jax.ShapeDtypeStruct((t, CONV_DIM), BF16),
                   jax.ShapeDtypeStruct((t, LANES), BF16), row_out(di), row_out(LANES), row_out(LANES), row_out(LANES),
                   row_out(CONV_DIM), jax.ShapeDtypeStruct((SUBLANES, CONV_DIM), F32)],
        scratch_shapes=[pltpu.VMEM((n, di), F32), pltpu.VMEM((q, di), F32), pltpu.VMEM((q, CONV_DIM), F32),
                        pltpu.VMEM((q, CONV_DIM), F32), pltpu.VMEM((SUBLANES + q, CONV_DIM), F32),
                        pltpu.VMEM((q + SUBLANES, CONV_DIM), F32)],
        operands=(dya, yssd, z, xbc_raw, xbc_raw, conv_w8, conv_b.reshape(1, CONV_DIM), dtr, states, bias_row,
                  alog_row, d_row, norm_g.reshape(1, di)),
        comm=comm)


S5_BLOCKS = 4
S5_BC = S5_WIDTH // S5_BLOCKS
S5_BS = S5_N // S5_BLOCKS


def _s5_tables(ar, ai, reverse):
    pows = [(ar, ai)]
    for _ in range(SUBLANES - 1):
        pr, pi = pows[-1]
        pows.append((pr * ar - pi * ai, pr * ai + pi * ar))
    row = lax.broadcasted_iota(jnp.int32, (SUBLANES, ar.shape[1]), 0)
    out = []
    for k in (1, 2, 4):
        pr, pi = pows[k - 1]
        mask = (row < SUBLANES - k) if reverse else (row >= k)
        out += [jnp.where(mask, pr, 0.0), jnp.where(mask, pi, 0.0)]
    order = list(range(SUBLANES))
    if reverse:
        order = order[::-1]
    out.append(jnp.concatenate([pows[e][0] for e in order], axis=0))
    out.append(jnp.concatenate([pows[e][1] for e in order], axis=0))
    return out


def s5_interleave(a, tm):
    t, c = a.shape
    return a.reshape(t // tm, SUBLANES, tm // SUBLANES, c).transpose(0, 2, 1, 3).reshape(t, c)


def s5_deinterleave(a, tm):
    t, c = a.shape
    return a.reshape(t // tm, tm // SUBLANES, SUBLANES, c).transpose(0, 2, 1, 3).reshape(t, c)


def _s5_scan_setup(ab_ref, base, tab, seg, reverse):
    ar = ab_ref[0:1, :]
    ai = -ab_ref[1:2, :] if reverse else ab_ref[1:2, :]
    base[0:1, :] = ar
    base[1:2, :] = ai
    assert seg & (seg - 1) == 0
    pr, pi = ar, ai
    for _ in range(seg.bit_length() - 1):
        pr, pi = pr * pr - pi * pi, 2.0 * pr * pi
    for k, tv in enumerate(_s5_tables(pr, pi, reverse)):
        tab[k] = tv


def _s5_scan(s_scr, base, tab, carry, tm, reverse):
    seg = tm // SUBLANES
    width = S5_BS
    first = lax.broadcasted_iota(jnp.int32, (SUBLANES, width), 0) == (SUBLANES - 1 if reverse else 0)

    def rows(i):
        step = (seg - 1 - i) if reverse else i
        return pl.ds(pl.multiple_of(step * SUBLANES, SUBLANES), SUBLANES)

    for lc in range(S5_N // width):
        lr = slice(lc * width, (lc + 1) * width)
        li = slice(S5_N + lc * width, S5_N + (lc + 1) * width)
        ar = jnp.broadcast_to(base[0:1, lr], (SUBLANES, width))
        ai = jnp.broadcast_to(base[1:2, lr], (SUBLANES, width))

        def advance(i, x, lr=lr, li=li, ar=ar, ai=ai):
            xr, xi = x
            return ar * xr - ai * xi + s_scr[rows(i), lr], ar * xi + ai * xr + s_scr[rows(i), li]

        zero = jnp.zeros((SUBLANES, width), F32)
        fr, fi = lax.fori_loop(0, seg, advance, (zero, zero))

        tb = [tab[k, :, lr] for k in range(8)]
        for lvl, k in enumerate((1, 2, 4)):
            sh = (SUBLANES - k) if reverse else k
            pr, pi = tb[2 * lvl], tb[2 * lvl + 1]
            rr, ri = pltpu.roll(fr, sh, 0), pltpu.roll(fi, sh, 0)
            fr, fi = fr + pr * rr - pi * ri, fi + pr * ri + pi * rr
        cr, ci = carry[0:1, lr], carry[0:1, li]
        fr, fi = fr + tb[6] * cr - tb[7] * ci, fi + tb[6] * ci + tb[7] * cr
        last = 0 if reverse else SUBLANES - 1
        carry[0:1, lr] = fr[last:last + 1, :]
        carry[0:1, li] = fi[last:last + 1, :]
        sh = (SUBLANES - 1) if reverse else 1
        entering = (jnp.where(first, cr, pltpu.roll(fr, sh, 0)), jnp.where(first, ci, pltpu.roll(fi, sh, 0)))

        def rerun(i, x, lr=lr, li=li, advance=advance):
            xr, xi = advance(i, x)
            s_scr[rows(i), lr] = xr
            s_scr[rows(i), li] = xi
            return xr, xi

        lax.fori_loop(0, seg, rerun, entering)


def _s5_scratch(tm):
    return [pltpu.VMEM((tm, 2 * S5_N), F32), pltpu.VMEM((SUBLANES, S5_N), F32),
            pltpu.VMEM((8, SUBLANES, S5_N), F32), pltpu.VMEM((SUBLANES, 2 * S5_N), F32)]


def _s5_put(scr, j, val):
    scr[:, j * S5_BS:(j + 1) * S5_BS] = val[:, :S5_BS]
    scr[:, S5_N + j * S5_BS:S5_N + (j + 1) * S5_BS] = val[:, S5_BS:]


def _s5_get(scr, j):
    return jnp.concatenate([scr[:, j * S5_BS:(j + 1) * S5_BS], scr[:, S5_N + j * S5_BS:S5_N + (j + 1) * S5_BS]], axis=1)


def s5_fwd(u5, bbc, ccc, ab8, d_row, wglu, bglu_row, *, tm, name, comm=None):
    t, w = u5.shape

    def body(u_ref, bb_ref, cc_ref, ab_ref, d_ref, wg_ref, bg_ref, s_ref, ys_ref, yb_ref, s_scr, base, tab, carry):
        i = pl.program_id(0)

        @pl.when(i == 0)
        def _():
            carry[...] = jnp.zeros_like(carry)
            _s5_scan_setup(ab_ref, base, tab, tm // SUBLANES, False)

        ub = u_ref[...]
        u = ub.astype(F32)
        for j in range(S5_BLOCKS):
            uj = ub[:, j * S5_BC:(j + 1) * S5_BC]
            _s5_put(s_scr, j, _dot(uj, bb_ref[j * S5_BC:(j + 1) * S5_BC, :]))
        _s5_scan(s_scr, base, tab, carry, tm, False)
        s_ref[...] = s_scr[...]
        ys = []
        for j in range(S5_BLOCKS):
            ys.append(_dot(_s5_get(s_scr, j).astype(BF16), cc_ref[:, j * S5_BC:(j + 1) * S5_BC]))
        y = jnp.concatenate(ys, axis=1) + d_ref[...] * u
        ys_ref[...] = y
        yb1 = _gelu(y)
        tt = _dot(yb1.astype(BF16), wg_ref[...]) + bg_ref[...]
        yb_ref[...] = (yb1 * _sigmoid(tt)).astype(BF16)

    tile = lambda c: pl.BlockSpec((tm, c), lambda i: (i, 0))
    return _tiled_call(
        body, name=name, grid=(t // tm,),
        in_specs=[tile(w), _full((w, 2 * S5_BS)), _full((2 * S5_BS, w)), _full((SUBLANES, S5_N)), _full((1, w)),
                  _full((w, w)), _full((1, w))],
        out_specs=[tile(2 * S5_N), tile(w), tile(w)],
        out_shape=[jax.ShapeDtypeStruct((t, 2 * S5_N), F32), jax.ShapeDtypeStruct((t, w), F32),
                   jax.ShapeDtypeStruct((t, w), BF16)],
        scratch_shapes=_s5_scratch(tm),
        operands=(u5, bbc, ccc, ab8, d_row, wglu, bglu_row), comm=comm)


def s5_bwd(dyb, ys5, u5, s, bbc, ccc, ab8, d_row, wglu, bglu_row, *, tm, name, comm=None):
    t, w = u5.shape
    nt = t // tm
    per = tm // SUBLANES

    def body(dyb_ref, ys_ref, u_ref, s_ref, sprev_ref, bb_ref, cc_ref, ab_ref, d_ref, wg_ref, bg_ref,
             du_ref, dbb_ref, dcc_ref, dab_ref, dd_ref, dwg_ref, dbg_ref, g_scr, base, tab, carry):
        i = pl.program_id(0)

        @pl.when(i == 0)
        def _():
            carry[...] = jnp.zeros_like(carry)
            _s5_scan_setup(ab_ref, base, tab, tm // SUBLANES, True)
            for r in (dbb_ref, dcc_ref, dab_ref, dd_ref, dwg_ref, dbg_ref):
                r[...] = jnp.zeros_like(r)

        y = ys_ref[...]
        ub = u_ref[...]
        u = ub.astype(F32)
        yb1 = _gelu(y)
        yb1b = yb1.astype(BF16)
        sg = _sigmoid(_dot(yb1b, wg_ref[...]) + bg_ref[...])
        dybv = dyb_ref[...]
        dtt = dybv * yb1 * sg * (1.0 - sg)
        dttb = dtt.astype(BF16)
        dyb1 = dybv * sg + _dot_nt(dttb, wg_ref[...])
        dwg_ref[...] += _dot_tn(yb1b, dttb)
        dbg_ref[...] += jnp.sum(dtt, axis=0, keepdims=True)
        dy = dyb1 * _gelu_grad(y)
        dd_ref[...] += jnp.sum(dy * u, axis=0, keepdims=True)
        dyh = dy.astype(BF16)
        for j in range(S5_BLOCKS):
            cs = slice(j * S5_BC, (j + 1) * S5_BC)
            _s5_put(g_scr, j, _dot_nt(dyh[:, cs], cc_ref[:, cs]))
        _s5_scan(g_scr, base, tab, carry, tm, True)
        dus = []
        for j in range(S5_BLOCKS):
            cs = slice(j * S5_BC, (j + 1) * S5_BC)
            gb = _s5_get(g_scr, j).astype(BF16)
            dus.append(_dot_nt(gb, bb_ref[cs, :]))
            dbb_ref[cs, :] += _dot_tn(ub[:, cs], gb)
            dcc_ref[:, cs] += _dot_tn(_s5_get(s_ref, j).astype(BF16), dyh[:, cs])
        du_ref[...] = (jnp.concatenate(dus, axis=1) + d_ref[...] * dy).astype(BF16)
        top = lax.broadcasted_iota(jnp.int32, (SUBLANES, S5_BS), 0) == 0

        def corr(g_sl, s_sl):
            before = jnp.where(i == nt - 1, 0.0, sprev_ref[SUBLANES - 1:SUBLANES, s_sl])
            wrap = jnp.where(top, before, pltpu.roll(s_ref[tm - SUBLANES:tm, s_sl], 1, 0))
            return (jnp.sum(g_scr[SUBLANES:tm, g_sl] * s_ref[0:tm - SUBLANES, s_sl], axis=0, keepdims=True)
                    + jnp.sum(g_scr[0:SUBLANES, g_sl] * wrap, axis=0, keepdims=True))

        for j in range(S5_BLOCKS):
            lr = slice(j * S5_BS, (j + 1) * S5_BS)
            li = slice(S5_N + j * S5_BS, S5_N + (j + 1) * S5_BS)
            dab_ref[0:1, lr] += corr(lr, lr) + corr(li, li)
            dab_ref[1:2, lr] += corr(li, lr) - corr(lr, li)

    tile = lambda c: pl.BlockSpec((tm, c), lambda i: (nt - 1 - i, 0))
    acc = lambda r, c: jax.ShapeDtypeStruct((r, c), F32)
    return _tiled_call(
        body, name=name, grid=(nt,),
        in_specs=[tile(w), tile(w), tile(w), tile(2 * S5_N),
                  pl.BlockSpec((SUBLANES, 2 * S5_N), lambda i: (jnp.maximum((nt - 1 - i) * per - 1, 0), 0)),
                  _full((w, 2 * S5_BS)), _full((2 * S5_BS, w)), _full((SUBLANES, S5_N)), _full((1, w)),
                  _full((w, w)), _full((1, w))],
        out_specs=[tile(w), _full((w, 2 * S5_BS)), _full((2 * S5_BS, w)), _full((SUBLANES, S5_N)), _full((1, w)),
                   _full((w, w)), _full((1, w))],
        out_shape=[jax.ShapeDtypeStruct((t, w), BF16), acc(w, 2 * S5_BS), acc(2 * S5_BS, w), acc(SUBLANES, S5_N),
                   acc(1, w), acc(w, w), acc(1, w)],
        scratch_shapes=_s5_scratch(tm),
        operands=(dyb, ys5, u5, s, s, bbc, ccc, ab8, d_row, wglu, bglu_row), comm=comm)


def s5_discretize(a_re, a_im, log_dt, b_re, b_im, *, name):
    n = a_re.shape[0]

    def body(ar_ref, ai_ref, dt_ref, br_ref, bi_ref, ab_ref, bbr_ref, bbi_ref):
        ar, ai, dtv = ar_ref[...], ai_ref[...], jnp.exp(dt_ref[...])
        mag = jnp.exp(ar * dtv)
        abr = mag * jnp.cos(ai * dtv)
        abi = mag * jnp.sin(ai * dtv)
        den = ar * ar + ai * ai
        nr = abr - 1.0
        cr = (nr * ar + abi * ai) / den
        ci = (abi * ar - nr * ai) / den
        ab_ref[:, 0:1] = abr
        ab_ref[:, 1:2] = abi
        br, bi = br_ref[...], bi_ref[...]
        bbr_ref[...] = cr * br - ci * bi
        bbi_ref[...] = cr * bi + ci * br

    col = jax.ShapeDtypeStruct((n, 2), F32)
    mat = jax.ShapeDtypeStruct((n, S5_GROUP), F32)
    return pl.pallas_call(body, name=name, out_shape=[col, mat, mat])(a_re, a_im, log_dt, b_re, b_im)


def s5_discretize_bwd(a_re, a_im, log_dt, b_re, b_im, dab, dbb_re, dbb_im, *, name):
    n = a_re.shape[0]

    def body(ar_ref, ai_ref, dt_ref, br_ref, bi_ref, dab_ref, dbr_ref, dbi_ref, da_ref, ddt_ref, gbr_ref, gbi_ref):
        ar, ai, dtv = ar_ref[...], ai_ref[...], jnp.exp(dt_ref[...])
        mag = jnp.exp(ar * dtv)
        abr = mag * jnp.cos(ai * dtv)
        abi = mag * jnp.sin(ai * dtv)
        den = ar * ar + ai * ai
        nr = abr - 1.0
        cr = (nr * ar + abi * ai) / den
        ci = (abi * ar - nr * ai) / den
        br, bi = br_ref[...], bi_ref[...]
        dbr, dbi = dbr_ref[...], dbi_ref[...]
        gbr_ref[...] = cr * dbr + ci * dbi
        gbi_ref[...] = cr * dbi - ci * dbr
        gcr = jnp.sum(dbr * br + dbi * bi, axis=1, keepdims=True)
        gci = jnp.sum(dbi * br - dbr * bi, axis=1, keepdims=True)
        ilr, ili = ar / den, -ai / den
        gabr = dab_ref[:, 0:1] + (ilr * gcr + ili * gci)
        gabi = dab_ref[:, 1:2] + (ilr * gci - ili * gcr)
        t1r, t1i = dtv * abr, dtv * abi
        t2r, t2i = -(cr * ilr - ci * ili), -(cr * ili + ci * ilr)
        da_ref[:, 0:1] = (t1r * gabr + t1i * gabi) + (t2r * gcr + t2i * gci)
        da_ref[:, 1:2] = (t1r * gabi - t1i * gabr) + (t2r * gci - t2i * gcr)
        lr, li = ar * abr - ai * abi, ar * abi + ai * abr
        dlog = (lr * gabr + li * gabi) * dtv
        ddt_ref[...] = jnp.sum(dlog.reshape(S5_GROUPS, S5_STATE, 1), axis=1)

    col2 = jax.ShapeDtypeStruct((n, 2), F32)
    col1 = jax.ShapeDtypeStruct((S5_GROUPS, 1), F32)
    mat = jax.ShapeDtypeStruct((n, S5_GROUP), F32)
    return pl.pallas_call(body, name=name, out_shape=[col2, col1, mat, mat])(
        a_re, a_im, log_dt, b_re, b_im, dab, dbb_re, dbb_im)


def merge_fwd(ya, yb, graw, x, wb, wout, *, tm, name):
    t, d = x.shape

    def body(ya_ref, yb_ref, g_ref, x_ref, wb_ref, wo_ref, x1_ref, pa_ref, pb_ref):
        pa = _dot(ya_ref[...], wb_ref[0:d, :])
        pb = _dot(yb_ref[...], wb_ref[d:, :])
        gate = _sigmoid(g_ref[...].astype(F32))
        merged = gate[:, :d] * pa + gate[:, d:] * pb
        x1_ref[...] = x_ref[...] + _dot(merged.astype(BF16), wo_ref[...])
        pa_ref[...] = pa.astype(BF16)
        pb_ref[...] = pb.astype(BF16)

    tile = lambda c: pl.BlockSpec((tm, c), lambda i: (i, 0))
    sds = jax.ShapeDtypeStruct
    return pl.pallas_call(
        body, name=name, grid=(t // tm,),
        in_specs=[tile(d), tile(S5_WIDTH), tile(2 * d), tile(d), _resident((d + S5_WIDTH, d)), _resident((d, d))],
        out_specs=[tile(d), tile(d), tile(d)], out_shape=[sds((t, d), F32), sds((t, d), BF16), sds((t, d), BF16)],
        compiler_params=_params("parallel"),
    )(ya, yb, graw, x, wb, wout)


def merge_bwd(dx1b, graw, pa, pb, wb, wout, *, tm, name):
    t, d = pa.shape

    def body(dx_ref, g_ref, pa_ref, pb_ref, wb_ref, wo_ref,
             mg_ref, dg_ref, dpa_ref, dpb_ref, dya_ref, dyb_ref):
        dm = _dot_nt(dx_ref[...], wo_ref[...])
        gate = _sigmoid(g_ref[...].astype(F32))
        g0, g1 = gate[:, :d], gate[:, d:]
        pa, pb = pa_ref[...].astype(F32), pb_ref[...].astype(F32)
        mg_ref[...] = (g0 * pa + g1 * pb).astype(BF16)
        dg_ref[:, :d] = (dm * pa * g0 * (1.0 - g0)).astype(BF16)
        dg_ref[:, d:] = (dm * pb * g1 * (1.0 - g1)).astype(BF16)
        dpa = (dm * g0).astype(BF16)
        dpb = (dm * g1).astype(BF16)
        dpa_ref[...] = dpa
        dpb_ref[...] = dpb
        dya_ref[...] = _dot_nt(dpa, wb_ref[0:d, :])
        dyb_ref[...] = _dot_nt(dpb, wb_ref[d:, :])

    tile = lambda c: pl.BlockSpec((tm, c), lambda i: (i, 0))
    sds = jax.ShapeDtypeStruct
    return pl.pallas_call(
        body, name=name, grid=(t // tm,),
        in_specs=[tile(d), tile(2 * d), tile(d), tile(d), _resident((d + S5_WIDTH, d)), _resident((d, d))],
        out_specs=[tile(d), tile(2 * d), tile(d), tile(d), tile(d), tile(S5_WIDTH)],
        out_shape=[sds((t, d), BF16), sds((t, 2 * d), BF16), sds((t, d), BF16), sds((t, d), BF16),
                   sds((t, d), F32), sds((t, S5_WIDTH), F32)],
        compiler_params=_params("parallel"),
    )(dx1b, graw, pa, pb, wb, wout)


def mlp_out_loss(a1, x1, w2, gf, target, *, tm, name):
    t, d = x1.shape
    f = a1.shape[1]

    def body(a_ref, x_ref, w_ref, g_ref, tg_ref, dx_ref, dxb_ref, loss_ref, dg_ref):
        i = pl.program_id(0)
        av = jnp.maximum(a_ref[...].astype(F32), 0.0)
        x2 = x_ref[...] + _dot((av * av).astype(BF16), w_ref[...])
        r = lax.rsqrt(jnp.mean(x2 * x2, axis=-1, keepdims=True) + EPS)
        xn = x2 * r
        err = xn * g_ref[...] - tg_ref[...]
        dyf = err * (1.0 / d)
        dxn = dyf * g_ref[...]
        dx = r * (dxn - xn * jnp.mean(dxn * xn, axis=-1, keepdims=True))
        dx_ref[...] = dx
        dxb_ref[...] = dx.astype(BF16)

        @pl.when(i == 0)
        def _():
            loss_ref[...] = jnp.zeros_like(loss_ref)
            dg_ref[...] = jnp.zeros_like(dg_ref)

        loss_ref[...] += jnp.sum(err * err) * (0.5 / d)
        dg_ref[...] += jnp.sum(dyf * xn, axis=0, keepdims=True)

    tile = lambda c: pl.BlockSpec((tm, c), lambda i: (i, 0))
    sds = jax.ShapeDtypeStruct
    return pl.pallas_call(
        body, name=name, grid=(t // tm,),
        in_specs=[tile(f), tile(d), _resident((f, d)), _full((1, d)), tile(d)],
        out_specs=[tile(d), tile(d), _full((1, LANES)), _full((1, d))],
        out_shape=[sds((t, d), F32), sds((t, d), BF16), sds((1, LANES), F32), sds((1, d), F32)],
        compiler_params=_params("arbitrary"),
    )(a1, x1, w2, gf.reshape(1, d), target)


def mlp_bwd_act(dx2b, a1, w2, *, tm, name):
    t, f = a1.shape
    d = dx2b.shape[1]

    def body(dx_ref, a_ref, w_ref, o_ref):
        dact = _dot_nt(dx_ref[...], w_ref[...])
        o_ref[...] = (dact * 2.0 * jnp.maximum(a_ref[...].astype(F32), 0.0)).astype(BF16)

    tile = lambda c: pl.BlockSpec((tm, c), lambda i: (i, 0))
    return pl.pallas_call(
        body, name=name, grid=(t // tm,),
        in_specs=[tile(d), tile(f), _resident((f, d))], out_specs=tile(f),
        out_shape=jax.ShapeDtypeStruct((t, f), BF16),
        compiler_params=_params("parallel"),
    )(dx2b, a1, w2)


ADAM_TILE_ELEMS = 128 * 1024


def _row_tile(rows, cap):
    if rows <= cap:
        return rows
    best = None
    for c in range(16, cap + 1, 16):
        if rows % c == 0:
            best = c
    assert best is not None, rows
    return best


def _device_order_sum(parts_ref):
    total = parts_ref[0].astype(F32)
    for k in range(1, N_DEV):
        total = total + parts_ref[k].astype(F32)
    return total


def _adamw_math(w, m, v, gparts_ref):
    g = _device_order_sum(gparts_ref)
    mn = ADAM_B1 * m + (1.0 - ADAM_B1) * g
    vn = ADAM_B2 * v + (1.0 - ADAM_B2) * (g * g)
    m_hat = mn / (1.0 - ADAM_B1 ** ADAM_STEP)
    v_hat = vn / (1.0 - ADAM_B2 ** ADAM_STEP)
    return g, -ADAM_LR * (m_hat / (jnp.sqrt(v_hat) + ADAM_EPS) + ADAM_WD * w), mn, vn


def adamw(w, m, v, gparts, *, name):
    rows, cols = w.shape
    tr = _row_tile(rows, max(16, ADAM_TILE_ELEMS // cols))

    def body(w_ref, m_ref, v_ref, g_ref, go_ref, d_ref, mo_ref, vo_ref):
        go_ref[...], d_ref[...], mo_ref[...], vo_ref[...] = _adamw_math(w_ref[...], m_ref[...], v_ref[...], g_ref)

    tile = pl.BlockSpec((tr, cols), lambda i: (i, 0))
    out = jax.ShapeDtypeStruct((rows, cols), F32)
    return pl.pallas_call(
        body, name=name, grid=(rows // tr,),
        in_specs=[tile, tile, tile, pl.BlockSpec((N_DEV, tr, cols), lambda i: (0, i, 0))],
        out_specs=[tile, tile, tile, tile], out_shape=[out, out, out, out],
        compiler_params=_params("parallel"),
    )(w, m, v, gparts)


def all_gather_arrays(blocks, *, name):
    na = len(blocks)

    def body(*refs):
        x_refs, out_refs = refs[:na], refs[na:2 * na]
        send_sems, recv_sems, local_sems = refs[2 * na:]
        x, y, c, _ = _place()
        me, sibling = (x, y, c), (x, y, 1 - c)
        chips = [(1 - x, y), (x, 1 - y), (1 - x, 1 - y)]

        def copy(a, k, blk, to, own=False):
            px, py, pc = blk
            dst = out_refs[a].at[4 * px + 2 * py + pc]
            return pltpu.make_async_remote_copy(
                src_ref=x_refs[a] if own else dst, dst_ref=dst,
                send_sem=send_sems.at[7 * a + k], recv_sem=recv_sems.at[7 * a + k], device_id=to, device_id_type=_MESH)

        mine = [pltpu.make_async_copy(x_refs[a], out_refs[a].at[4 * x + 2 * y + c], local_sems.at[a]) for a in range(na)]
        for cp in mine:
            cp.start()
        sent = []
        for a in range(na):
            first = [copy(a, 0, me, sibling, own=True)]
            first += [copy(a, 1 + j, me, (*chip, c), own=True) for j, chip in enumerate(chips)]
            for cp in first:
                cp.start()
            sent += first
        for a in range(na):
            for j, chip in enumerate(chips):
                copy(a, 1 + j, (*chip, c), me).wait_recv()
                fwd = copy(a, 4 + j, (*chip, c), sibling)
                fwd.start()
                sent.append(fwd)
        for a in range(na):
            copy(a, 0, sibling, me).wait_recv()
            for j, chip in enumerate(chips):
                copy(a, 4 + j, (*chip, 1 - c), me).wait_recv()
        for cp in sent:
            cp.wait_send()
        for cp in mine:
            cp.wait()

    any_spec = pl.BlockSpec(memory_space=pl.ANY)
    return pl.pallas_call(
        body, name=name, in_specs=[any_spec] * na, out_specs=[any_spec] * na,
        out_shape=[jax.ShapeDtypeStruct((N_DEV,) + b.shape, b.dtype) for b in blocks],
        scratch_shapes=[pltpu.SemaphoreType.DMA((7 * na,)), pltpu.SemaphoreType.DMA((7 * na,)),
                        pltpu.SemaphoreType.DMA((na,))],
        compiler_params=pltpu.CompilerParams(has_side_effects=True),
    )(*blocks)


def all_to_all(comm, *, name):
    na = len(comm)
    kinds = [k for k, _ in comm]

    def body(*refs):
        local, remote = _comm_copies(kinds, refs[:na], refs[na:2 * na], *refs[2 * na:])
        for cp in local + remote:
            cp.start()
        for cp in remote:
            cp.wait_recv()
        for cp in remote:
            cp.wait_send()
        for cp in local:
            cp.wait()

    any_spec = pl.BlockSpec(memory_space=pl.ANY)
    return pl.pallas_call(
        body, name=name, in_specs=[any_spec] * na, out_specs=[any_spec] * na,
        out_shape=_comm_out_shapes(comm),
        scratch_shapes=[pltpu.SemaphoreType.DMA((na * N_REL,)), pltpu.SemaphoreType.DMA((na * N_REL,)),
                        pltpu.SemaphoreType.DMA((na,))],
        compiler_params=pltpu.CompilerParams(has_side_effects=True),
    )(*[a for _, a in comm])


def adamw_whole(ws, ms, vs, gparts, *, name, totals=()):
    n, nt = len(ws), len(totals)

    def body(*refs):
        w_refs, m_refs, v_refs, g_refs = refs[:n], refs[n:2 * n], refs[2 * n:3 * n], refs[3 * n:4 * n]
        t_refs, refs = refs[4 * n:4 * n + nt], refs[4 * n + nt:]
        outs, t_outs = refs[:4 * n], refs[4 * n:]
        for k in range(n):
            g, delta, mn, vn = _adamw_math(w_refs[k][...], m_refs[k][...], v_refs[k][...], g_refs[k])
            outs[k][...] = g
            outs[n + k][...] = delta
            outs[2 * n + k][...] = mn
            outs[3 * n + k][...] = vn
        for t_ref, o_ref in zip(t_refs, t_outs):
            o_ref[...] = _device_order_sum(t_ref)

    shapes = [jax.ShapeDtypeStruct(w.shape, F32) for w in ws]
    res = pl.pallas_call(
        body, name=name, out_shape=shapes * 4 + [jax.ShapeDtypeStruct(t.shape[1:], F32) for t in totals],
        compiler_params=pltpu.CompilerParams(vmem_limit_bytes=VMEM_LIMIT),
    )(*ws, *ms, *vs, *gparts, *totals)
    return [res[j * n:(j + 1) * n] for j in range(4)], res[4 * n:]


def _lane_row(v):
    return jnp.pad(v.astype(F32), (0, LANES - v.shape[0])).reshape(1, LANES)


def _block_diag_b(bb):
    eye = jnp.eye(SUBLANES, dtype=F32)
    bt = bb.reshape(S5_BLOCKS, 8, S5_STATE, S5_GROUP).transpose(0, 1, 3, 2)
    return (bt[:, :, :, None, :] * eye[None, :, None, :, None]).reshape(S5_WIDTH, S5_BS)


def _block_diag_c(cc):
    eye = jnp.eye(SUBLANES, dtype=F32)
    ct = cc.reshape(S5_BLOCKS, 8, S5_GROUP, S5_STATE).transpose(3, 0, 1, 2)
    return (ct[None, :, :, :, :] * eye[:, None, None, :, None]).reshape(S5_BS, S5_WIDTH)


def _diag_blocks_b(m):
    eye = jnp.eye(SUBLANES, dtype=F32)
    m5 = m.reshape(S5_BLOCKS, 8, S5_GROUP, 8, S5_STATE)
    return jnp.sum(m5 * eye[None, :, None, :, None], axis=3).transpose(0, 1, 3, 2).reshape(S5_GROUPS, S5_STATE, S5_GROUP)


def _diag_blocks_c(m):
    eye = jnp.eye(SUBLANES, dtype=F32)
    m5 = m.reshape(8, S5_STATE, S5_BLOCKS, 8, S5_GROUP)
    return jnp.sum(m5 * eye[:, None, None, :, None], axis=0).transpose(1, 2, 3, 0).reshape(S5_GROUPS, S5_GROUP, S5_STATE)


def train_step(x, target, p, m, v):
    t = x.shape[0]
    tm = min(256, t)
    tb = min(512, t)
    ts = min(512, t)
    bf = lambda n: p[n].astype(BF16)
    slots = lambda n, a: a.reshape((N_DEV,) + _shard_shape(n))
    updated = {}

    def update(names, got):
        for n, parts in zip(names, got):
            updated[n] = adamw(p[n], m[n], v[n], parts, name=f"adamw_{n}")

    st_in, st_conv = all_gather_arrays([bf("w_in"), p["conv_w"]], name="gather_first")
    win_p = jnp.concatenate([st_in[k][:, a:b] for seg in IN_PADDED_ORDER for k, a, b in _shard_pieces(*IN_SEGMENTS[seg])]
                            + [jnp.zeros((D_MODEL, LANES - SSD_HEADS), BF16)], axis=1)
    conv_w8 = jnp.pad(_join_shards("conv_w", st_conv), ((0, SUBLANES - SSD_CONV), (0, 0)))
    (h, xbc_raw, graw, z, u5, dtr), (st_branch, st_out, st_glu) = rms_matmul(
        x, p["norm_mix_g"], win_p, IN_SPLITS, tm=tb, name="in_proj", out_dtypes=(BF16, BF16, BF16, BF16, F32),
        comm=gathers([bf("w_branch"), bf("w_out"), bf("s5_glu_w")]))
    w_branch = st_branch.reshape(D_MODEL + S5_WIDTH, D_MODEL)
    wout, wglu = st_out.reshape(D_MODEL, D_MODEL), st_glu.reshape(S5_WIDTH, S5_WIDTH)
    bias_row, alog_row = _lane_row(p["dt_bias"]), _lane_row(p["a_log"])
    d_row = jnp.repeat(p["d_ssd"], SSD_HEADDIM).reshape(1, SSD_HEADS * SSD_HEADDIM)
    (ya, yssd, states), (w1,) = ssd_fwd(xbc_raw, conv_w8, p["conv_b"], dtr, z, bias_row, alog_row, d_row,
                                        p["ssd_norm_g"], name="ssd_fwd", comm=gathers([bf("w_mlp_in")]))

    n = S5_N
    a_re_c, a_im_c = p["s5_a_re"].reshape(n, 1), p["s5_a_im"].reshape(n, 1)
    dt_c = jnp.repeat(p["s5_log_dt"], S5_STATE).reshape(n, 1)
    b_re_m, b_im_m = p["s5_b_re"].reshape(n, S5_GROUP), p["s5_b_im"].reshape(n, S5_GROUP)
    ab, bb_re, bb_im = s5_discretize(a_re_c, a_im_c, dt_c, b_re_m, b_im_m, name="s5_disc")
    ab8 = jnp.pad(ab.T, ((0, SUBLANES - 2), (0, 0)))
    bbc = jnp.concatenate([_block_diag_b(bb_re.reshape(S5_GROUPS, S5_STATE, S5_GROUP)),
                           _block_diag_b(bb_im.reshape(S5_GROUPS, S5_STATE, S5_GROUP))], axis=1).astype(BF16)
    ccc = jnp.concatenate([_block_diag_c(p["s5_c_re"]), -_block_diag_c(p["s5_c_im"])], axis=0).astype(BF16)
    s5d_row = p["s5_d"].reshape(1, S5_WIDTH)
    bglu_row = p["s5_glu_b"].reshape(1, S5_WIDTH)
    u5 = s5_interleave(u5, ts)
    (s, ys5, yb), (st_w2,) = s5_fwd(u5, bbc, ccc, ab8, s5d_row, wglu, bglu_row, tm=ts, name="s5_fwd",
                                    comm=gathers([bf("w_mlp_out")]))
    yb = s5_deinterleave(yb, ts)
    w2 = st_w2.reshape(D_FF, D_MODEL)

    x1, pa, pb = merge_fwd(ya, yb, graw, x, w_branch, wout, tm=tb, name="merge_fwd")
    (h2, a1), _ = rms_matmul(x1, p["norm_mlp_g"], w1, (D_FF,), tm=tb, name="mlp_in", out_dtypes=(BF16,))
    dx2, dx2b, loss_row, dgf = mlp_out_loss(a1, x1, w2, p["norm_final_g"], target, tm=tb, name="mlp_out_loss")

    g = {}
    g["norm_final_g"] = dgf[0]
    da1 = mlp_bwd_act(dx2b, a1, w2, tm=tb, name="mlp_bwd_act")
    dw2 = slots("w_mlp_out", matmul_tn(a1, dx2b, out_dtype=BF16, a_relu2=True, name="dw_mlp_out"))
    dw1 = matmul_tn(h2, da1, out_dtype=BF16, col_shards=N_DEV, name="dw_mlp_in")
    (dx1, dx1b, dgm), _ = matmul_nt_rms_bwd([da1], w1, x1, p["norm_mlp_g"], dx2, tm=tb, name="mlp_in_bwd")
    g["norm_mlp_g"] = dgm[0]
    merged, dgraw, dpa, dpb, dya, dyb = merge_bwd(dx1b, graw, pa, pb, w_branch, wout, tm=tm, name="merge_bwd")
    dwo = slots("w_out", matmul_tn(merged, dx1b, out_dtype=BF16, name="dw_out"))
    dwb = slots("w_branch", jnp.concatenate([matmul_tn(ya, dpa, out_dtype=BF16, name="dw_branch_a"),
                                             matmul_tn(yb, dpb, out_dtype=BF16, name="dw_branch_b")], axis=0))

    (du5, dbbc, dccc, dab, dd5, dwglu, dbglu), got = s5_bwd(
        s5_interleave(dyb, ts), ys5, u5, s, bbc, ccc, ab8, s5d_row, wglu, bglu_row, tm=ts, name="s5_bwd",
        comm=exchanges([dw2, dw1]))
    du5 = s5_deinterleave(du5, ts)
    update(["w_mlp_out", "w_mlp_in"], got)
    g["s5_glu_b"], g["s5_d"] = dbglu[0], dd5[0]
    g["s5_c_re"] = _diag_blocks_c(dccc[:S5_BS])
    g["s5_c_im"] = -_diag_blocks_c(dccc[S5_BS:])
    dbb_re = _diag_blocks_b(dbbc[:, :S5_BS]).reshape(n, S5_GROUP)
    dbb_im = _diag_blocks_b(dbbc[:, S5_BS:]).reshape(n, S5_GROUP)
    da, dlogdt, gb_re, gb_im = s5_discretize_bwd(a_re_c, a_im_c, dt_c, b_re_m, b_im_m, dab[:2].T, dbb_re, dbb_im,
                                                  name="s5_disc_bwd")
    g["s5_a_re"] = da[:, 0].reshape(S5_GROUPS, S5_STATE)
    g["s5_a_im"] = da[:, 1].reshape(S5_GROUPS, S5_STATE)
    g["s5_log_dt"] = dlogdt[:, 0]
    g["s5_b_re"] = gb_re.reshape(S5_GROUPS, S5_STATE, S5_GROUP)
    g["s5_b_im"] = gb_im.reshape(S5_GROUPS, S5_STATE, S5_GROUP)

    def update_small(names, got, name, totals=()):
        (gs, ds, nm, nv), sums = adamw_whole([p[n] for n in names], [m[n] for n in names], [v[n] for n in names], got,
                                             name=name, totals=totals)
        for k, n in enumerate(names):
            updated[n] = (gs[k], ds[k], nm[k], nv[k])
        return sums

    wide = ["s5_b_re", "s5_b_im"]
    late = ["ssd_norm_g", "d_ssd", "a_log", "dt_bias", "conv_b"]
    early = [n for n in SMALL_ORDER if n not in wide + late and n != "norm_mix_g"]
    (dz, dxbc_raw, ddtr, dng, dd_row, dalog_row, dbias_row, dconv_b, dconv_w8), got = ssd_bwd(
        dya, yssd, z, xbc_raw, conv_w8, p["conv_b"], dtr, states, bias_row, alog_row, d_row, p["ssd_norm_g"],
        name="ssd_bwd",
        comm=exchanges([dwo, dwb, slots("s5_glu_w", dwglu.astype(BF16))])
        + gathers([g[n] for n in wide + early] + [loss_row]))
    update(["w_out", "w_branch", "s5_glu_w"], got[:3])
    update_small(wide, got[3:3 + len(wide)], "adamw_s5_b")
    loss_sum, = update_small(early, got[3 + len(wide):-1], "adamw_small_early", totals=[got[-1]])
    g["ssd_norm_g"] = dng[0]
    g["d_ssd"], g["a_log"], g["dt_bias"] = dd_row[0, :SSD_HEADS], dalog_row[0, :SSD_HEADS], dbias_row[0, :SSD_HEADS]
    g["conv_b"] = dconv_b[0]
    dconv = dconv_w8[:SSD_CONV].reshape(SSD_CONV, N_DEV, CONV_DIM // N_DEV).transpose(1, 0, 2)
    dproj =[dxbc_raw, dgraw, dz, du5, ddtr]
    dxbc_w, dgate_w, dz_w, du5_w, ddt_w = [matmul_tn(h, piece, out_dtype=BF16, name=f"dw_in_{k}")
                                           for k, piece in enumerate(dproj)]
    by_segment = {"z": dz_w, "xbc": dxbc_w, "dt": ddt_w, "u5": du5_w, "gates": dgate_w}
    width = D_IN_PROJ // N_DEV
    dw_in = jnp.stack([jnp.concatenate(
        [by_segment[seg][:, max(k * width, lo) - lo:min((k + 1) * width, hi) - lo]
         for seg, (lo, hi) in IN_SEGMENTS.items() if max(k * width, lo) < min((k + 1) * width, hi)], axis=1)
        for k in range(N_DEV)])

    (grad_x, _, dgx), got = matmul_nt_rms_bwd(
        dproj, win_p, x, p["norm_mix_g"], dx1, tm=tm, name="in_proj_bwd",
        comm=exchanges([dw_in, dconv]) + gathers([g[n] for n in late]))
    update(["w_in", "conv_w"], got[:2])
    update_small(late, got[2:], "adamw_small_late")
    update_small(["norm_mix_g"], all_to_all(gathers([dgx[0]]), name="gather_last"), "adamw_small_last")
    return loss_sum[0, 0], grad_x, updated


WEIGHT_ORDER = ["norm_mix_g", "w_in", "conv_w", "conv_b", "dt_bias", "a_log", "d_ssd", "ssd_norm_g", "s5_a_re",
                "s5_a_im", "s5_log_dt", "s5_b_re", "s5_b_im", "s5_c_re", "s5_c_im", "s5_d", "s5_glu_w", "s5_glu_b",
                "w_branch", "w_out", "norm_mlp_g", "w_mlp_in", "w_mlp_out", "norm_final_g"]
SHARDED = {"w_in": ((D_MODEL, D_IN_PROJ), 1), "conv_w": ((SSD_CONV, CONV_DIM), 1), "s5_glu_w": ((S5_WIDTH, S5_WIDTH), 0),
           "w_branch": ((D_MODEL + S5_WIDTH, D_MODEL), 0), "w_out": ((D_MODEL, D_MODEL), 0),
           "w_mlp_in": ((D_MODEL, D_FF), 1), "w_mlp_out": ((D_FF, D_MODEL), 0)}
SMALL_ORDER = [n for n in WEIGHT_ORDER if n not in SHARDED]


def _shard_shape(name):
    (r, c), ax = SHARDED[name]
    return (r, c // N_DEV) if ax == 1 else (r // N_DEV, c)


def _join_shards(name, stacked):
    (r, c), ax = SHARDED[name]
    if ax == 1:
        return stacked.transpose(1, 0, 2).reshape(r, c)
    return stacked.reshape(r, c)


def _shard_pieces(lo, hi):
    width = D_IN_PROJ // N_DEV
    out = []
    while lo < hi:
        k = lo // width
        end = min(hi, (k + 1) * width)
        out.append((k, lo - k * width, end - k * width))
        lo = end
    return out


IN_SEGMENTS = {"z": (0, 1024), "xbc": (1024, 3072), "dt": (3072, 3088), "u5": (3088, 3600), "gates": (3600, 5648)}
IN_PADDED_ORDER = ("xbc", "gates", "z", "u5", "dt")


def kernel(x, norm_mix_g, w_in, conv_w, conv_b, dt_bias, a_log, d_ssd, ssd_norm_g, s5_a_re, s5_a_im, s5_log_dt, s5_b_re, s5_b_im, s5_c_re, s5_c_im, s5_d, s5_glu_w, s5_glu_b, w_branch, w_out, norm_mlp_g, w_mlp_in, w_mlp_out, norm_final_g, loss_target, m_norm_mix_g, m_w_in, m_conv_w, m_conv_b, m_dt_bias, m_a_log, m_d_ssd, m_ssd_norm_g, m_s5_a_re, m_s5_a_im, m_s5_log_dt, m_s5_b_re, m_s5_b_im, m_s5_c_re, m_s5_c_im, m_s5_d, m_s5_glu_w, m_s5_glu_b, m_w_branch, m_w_out, m_norm_mlp_g, m_w_mlp_in, m_w_mlp_out, m_norm_final_g, v_norm_mix_g, v_w_in, v_conv_w, v_conv_b, v_dt_bias, v_a_log, v_d_ssd, v_ssd_norm_g, v_s5_a_re, v_s5_a_im, v_s5_log_dt, v_s5_b_re, v_s5_b_im, v_s5_c_re, v_s5_c_im, v_s5_d, v_s5_glu_w, v_s5_glu_b, v_w_branch, v_w_out, v_norm_mlp_g, v_w_mlp_in, v_w_mlp_out, v_norm_final_g):
    w = dict(norm_mix_g=norm_mix_g, w_in=w_in, conv_w=conv_w, conv_b=conv_b, dt_bias=dt_bias, a_log=a_log, d_ssd=d_ssd,
             ssd_norm_g=ssd_norm_g, s5_a_re=s5_a_re, s5_a_im=s5_a_im, s5_log_dt=s5_log_dt, s5_b_re=s5_b_re,
             s5_b_im=s5_b_im, s5_c_re=s5_c_re, s5_c_im=s5_c_im, s5_d=s5_d, s5_glu_w=s5_glu_w, s5_glu_b=s5_glu_b,
             w_branch=w_branch, w_out=w_out, norm_mlp_g=norm_mlp_g, w_mlp_in=w_mlp_in, w_mlp_out=w_mlp_out,
             norm_final_g=norm_final_g)
    m = dict(norm_mix_g=m_norm_mix_g, w_in=m_w_in, conv_w=m_conv_w, conv_b=m_conv_b, dt_bias=m_dt_bias, a_log=m_a_log,
             d_ssd=m_d_ssd, ssd_norm_g=m_ssd_norm_g, s5_a_re=m_s5_a_re, s5_a_im=m_s5_a_im, s5_log_dt=m_s5_log_dt,
             s5_b_re=m_s5_b_re, s5_b_im=m_s5_b_im, s5_c_re=m_s5_c_re, s5_c_im=m_s5_c_im, s5_d=m_s5_d,
             s5_glu_w=m_s5_glu_w, s5_glu_b=m_s5_glu_b, w_branch=m_w_branch, w_out=m_w_out, norm_mlp_g=m_norm_mlp_g,
             w_mlp_in=m_w_mlp_in, w_mlp_out=m_w_mlp_out, norm_final_g=m_norm_final_g)
    v = dict(norm_mix_g=v_norm_mix_g, w_in=v_w_in, conv_w=v_conv_w, conv_b=v_conv_b, dt_bias=v_dt_bias, a_log=v_a_log,
             d_ssd=v_d_ssd, ssd_norm_g=v_ssd_norm_g, s5_a_re=v_s5_a_re, s5_a_im=v_s5_a_im, s5_log_dt=v_s5_log_dt,
             s5_b_re=v_s5_b_re, s5_b_im=v_s5_b_im, s5_c_re=v_s5_c_re, s5_c_im=v_s5_c_im, s5_d=v_s5_d,
             s5_glu_w=v_s5_glu_w, s5_glu_b=v_s5_glu_b, w_branch=v_w_branch, w_out=v_w_out, norm_mlp_g=v_norm_mlp_g,
             w_mlp_in=v_w_mlp_in, w_mlp_out=v_w_mlp_out, norm_final_g=v_norm_final_g)

    loss, grad_x, upd = train_step(x[0], loss_target[0], w, m, v)
    return (loss, grad_x.reshape(x.shape), *[upd[n][j] for j in range(4) for n in WEIGHT_ORDER])
```

```python
import math

import jax
import jax.numpy as jnp
from jax import lax
from jax.experimental import pallas as pl
from jax.experimental.pallas import tpu as pltpu

F32 = jnp.float32
BF16 = jnp.bfloat16

N_DEV = 8
D_MODEL = 1024
SSD_HEADS = 16
SSD_HEADDIM = 64
SSD_GROUPS = 4
SSD_HPG = 4
SSD_STATE = 128
SSD_CHUNK = 128
SSD_CONV = 4
CONV_DIM = 2048
S5_WIDTH = 512
S5_GROUP = 16
S5_GROUPS = 32
S5_STATE = 64
S5_N = S5_GROUPS * S5_STATE
D_FF = 4096
D_IN_PROJ = 5648
IN_SPLITS = (2048, 2048, 512, 1024, 128)
EPS = 1e-6
LANES = 128
SUBLANES = 8
VMEM_LIMIT = 56 * 1024 * 1024

ADAM_LR = 0.001
ADAM_B1 = 0.9
ADAM_B2 = 0.999
ADAM_EPS = 1e-08
ADAM_WD = 0.01
ADAM_STEP = 10

GELU_C = math.sqrt(2.0 / math.pi)
GELU_K = 0.044715


def _params(*sem):
    return pltpu.CompilerParams(dimension_semantics=sem, vmem_limit_bytes=VMEM_LIMIT)


def _full(shape):
    nd = len(shape)
    return pl.BlockSpec(shape, lambda *_: (0,) * nd)


def _resident(shape):
    nd = len(shape)
    return pl.BlockSpec(shape, lambda *_: (0,) * nd, pipeline_mode=pl.Buffered(1))


_MESH = pl.DeviceIdType.MESH
_RELATIONS = [(dx, dy, dc) for dx in (0, 1) for dy in (0, 1) for dc in (0, 1)][1:]
N_REL = len(_RELATIONS)


def _place():
    x, y, c = lax.axis_index("x"), lax.axis_index("y"), lax.axis_index("c")
    return x, y, c, 4 * x + 2 * y + c


def gathers(arrays):
    return [("gather", a) for a in arrays]


def exchanges(arrays):
    return [("exchange", a) for a in arrays]


def _comm_out_shapes(comm):
    return [jax.ShapeDtypeStruct(((N_DEV,) if kind == "gather" else ()) + a.shape, a.dtype) for kind, a in comm]


def _comm_copies(kinds, src_refs, dst_refs, send_sems, recv_sems, local_sems):
    x, y, c, idx = _place()
    local, remote = [], []
    for a, (kind, src, dst) in enumerate(zip(kinds, src_refs, dst_refs)):
        local.append(pltpu.make_async_copy(src if kind == "gather" else src.at[idx], dst.at[idx], local_sems.at[a]))
        for k, (dx, dy, dc) in enumerate(_RELATIONS):
            px, py, pc = x ^ dx, y ^ dy, c ^ dc
            remote.append(pltpu.make_async_remote_copy(
                src_ref=src if kind == "gather" else src.at[4 * px + 2 * py + pc], dst_ref=dst.at[idx],
                send_sem=send_sems.at[N_REL * a + k], recv_sem=recv_sems.at[N_REL * a + k],
                device_id=(px, py, pc), device_id_type=_MESH))
    return local, remote


def _tiled_call(body, *, name, grid, in_specs, out_specs, out_shape, operands, scratch_shapes=(), comm=None):
    params = pltpu.CompilerParams(dimension_semantics=("arbitrary",), vmem_limit_bytes=VMEM_LIMIT,
                                  has_side_effects=bool(comm))
    if not comm:
        outs = pl.pallas_call(body, name=name, grid=grid, in_specs=in_specs, out_specs=out_specs, out_shape=out_shape,
                              scratch_shapes=list(scratch_shapes), compiler_params=params)(*operands)
        return outs, []
    kind = [k for k, _ in comm]
    arrays = [a for _, a in comm]
    na, n_in, n_out, n_scr = len(arrays), len(in_specs), len(out_specs), len(scratch_shapes)
    last = grid[0] - 1

    def hosted(*refs):
        ins, refs = refs[:n_in], refs[n_in:]
        cin, refs = refs[:na], refs[na:]
        outs, refs = refs[:n_out], refs[n_out:]
        cout, refs = refs[:na], refs[na:]
        scr, sems = refs[:n_scr], refs[n_scr:]
        i = pl.program_id(0)

        @pl.when(i == 0)
        def _():
            local, remote = _comm_copies(kind, cin, cout, *sems)
            for cp in local + remote:
                cp.start()

        body(*ins, *outs, *scr)

        @pl.when(i == last)
        def _():
            local, remote = _comm_copies(kind, cin, cout, *sems)
            for cp in remote:
                cp.wait_recv()
            for cp in remote:
                cp.wait_send()
            for cp in local:
                cp.wait()

    any_spec = pl.BlockSpec(memory_space=pl.ANY)
    res = pl.pallas_call(
        hosted, name=name, grid=grid,
        in_specs=list(in_specs) + [any_spec] * na, out_specs=list(out_specs) + [any_spec] * na,
        out_shape=list(out_shape) + _comm_out_shapes(comm),
        scratch_shapes=list(scratch_shapes) + [pltpu.SemaphoreType.DMA((N_REL * na,)), pltpu.SemaphoreType.DMA((N_REL * na,)),
                                               pltpu.SemaphoreType.DMA((na,))],
        compiler_params=params)(*operands, *arrays)
    return res[:n_out], res[n_out:]


def _dot(a, b):
    return jnp.dot(a, b, preferred_element_type=F32)


def _dot_nt(a, b):
    return lax.dot_general(a, b, (((1,), (1,)), ((), ())), preferred_element_type=F32)


def _dot_tn(a, b):
    return lax.dot_general(a, b, (((0,), (0,)), ((), ())), preferred_element_type=F32)


def _split2(x):
    hi = x.astype(BF16)
    return hi, (x - hi.astype(F32)).astype(BF16)


def _select_dot(dot, a, b, exact):
    if exact == "a":
        ea, (hi, lo) = a.astype(BF16), _split2(b)
        return dot(ea, hi) + dot(ea, lo)
    eb, (hi, lo) = b.astype(BF16), _split2(a)
    return dot(hi, eb) + dot(lo, eb)


def _dot_sel_a(a, b):
    return _select_dot(_dot, a, b, "a")


def _dot_sel_b(a, b):
    return _select_dot(_dot, a, b, "b")


def _dot_nt_sel_a(a, b):
    return _select_dot(_dot_nt, a, b, "a")


def _dot_tn_sel(a, b, exact):
    return _select_dot(_dot_tn, a, b, exact)


def _sigmoid(x):
    return 1.0 / (1.0 + jnp.exp(-x))


def _softplus(x):
    return jnp.maximum(x, 0.0) + jnp.log(1.0 + jnp.exp(-jnp.abs(x)))


def _gelu(x):
    return 0.5 * x * (1.0 + jnp.tanh(GELU_C * (x + GELU_K * x * x * x)))


def _gelu_grad(x):
    th = jnp.tanh(GELU_C * (x + GELU_K * x * x * x))
    return 0.5 * (1.0 + th) + 0.5 * x * (1.0 - th * th) * GELU_C * (1.0 + 3.0 * GELU_K * x * x)


def rms_matmul(x, g, w, splits, *, tm, name, comm=None, out_dtypes=None):
    t, d = x.shape
    stacked = w.ndim == 3
    n = w.shape[0] * w.shape[2] if stacked else w.shape[1]
    assert sum(splits) == n and (len(splits) == 1 or not stacked)

    def body(x_ref, g_ref, w_ref, h_ref, *o_refs):
        xv = x_ref[...]
        r = lax.rsqrt(jnp.mean(xv * xv, axis=-1, keepdims=True) + EPS)
        h = (xv * r * g_ref[...]).astype(BF16)
        h_ref[...] = h
        if stacked:
            wk = w.shape[2]
            for k in range(w.shape[0]):
                o_refs[0][:, k * wk:(k + 1) * wk] = _dot(h, w_ref[k]).astype(o_refs[0].dtype)
            return
        off = 0
        for o_ref, width in zip(o_refs, splits):
            o_ref[...] = _dot(h, w_ref[:, off:off + width]).astype(o_ref.dtype)
            off += width

    tile = lambda c: pl.BlockSpec((tm, c), lambda i: (i, 0))
    dtypes = out_dtypes or (F32,) * len(splits)
    return _tiled_call(
        body, name=name, grid=(t // tm,),
        in_specs=[tile(d), _full((1, d)), _resident(w.shape)],
        out_specs=[tile(d)] + [tile(c) for c in splits],
        out_shape=[jax.ShapeDtypeStruct((t, d), BF16)] + [jax.ShapeDtypeStruct((t, c), dt) for c, dt in zip(splits, dtypes)],
        operands=(x, g.reshape(1, d), w), comm=comm)


MATMUL_TN_ACC_BYTES = 16 * 1024 * 1024


def _pick(n, cap):
    best = LANES
    for c in range(LANES, cap + 1, LANES):
        if n % c == 0:
            best = c
    return best


def matmul_tn(a, b, *, name, out_dtype=F32, col_shards=None, tk=1024, a_relu2=False):
    t, m = a.shape
    n = b.shape[1]
    tm = _pick(m, 1024)
    whole_n = tm * n * 4 <= MATMUL_TN_ACC_BYTES
    tn = n if whole_n else _pick(n, 1280)
    assert whole_n or not col_shards
    tk = min(tk if tn <= 2048 else tk // 2, t)
    nk = t // tk

    def body(a_ref, b_ref, o_ref, acc):
        k = pl.program_id(2)

        @pl.when(k == 0)
        def _():
            acc[...] = jnp.zeros_like(acc)

        av = a_ref[...]
        if a_relu2:
            pos = jnp.maximum(av.astype(F32), 0.0)
            av = (pos * pos).astype(BF16)
        acc[...] += _dot_tn(av, b_ref[...])

        @pl.when(k == nk - 1)
        def _():
            if col_shards:
                w = n // col_shards
                for s in range(col_shards):
                    o_ref[s] = acc[:, s * w:(s + 1) * w].astype(out_dtype)
            else:
                o_ref[...] = acc[...].astype(out_dtype)

    if col_shards:
        out_spec = pl.BlockSpec((col_shards, tm, n // col_shards), lambda i, j, k: (0, i, 0))
        out_shape = jax.ShapeDtypeStruct((col_shards, m, n // col_shards), out_dtype)
    else:
        out_spec = pl.BlockSpec((tm, tn), lambda i, j, k: (i, j))
        out_shape = jax.ShapeDtypeStruct((m, n), out_dtype)
    return pl.pallas_call(
        body, name=name, grid=(m // tm, n // tn, nk),
        in_specs=[pl.BlockSpec((tk, tm), lambda i, j, k: (k, i)), pl.BlockSpec((tk, tn), lambda i, j, k: (k, j))],
        out_specs=out_spec, out_shape=out_shape,
        scratch_shapes=[pltpu.VMEM((tm, tn), F32)],
        compiler_params=_params("parallel", "parallel", "arbitrary"),
    )(a, b)


def matmul_nt_rms_bwd(dys, w, x, g, dres, *, tm, name, comm=None):
    t = x.shape[0]
    stacked = w.ndim == 3
    d = w.shape[1] if stacked else w.shape[0]
    widths = [dy.shape[1] for dy in dys]
    n_dy = len(dys)

    def body(*refs):
        dy_refs = refs[:n_dy]
        w_ref, x_ref, g_ref, dres_ref, dx_ref, dxb_ref, dg_ref = refs[n_dy:]
        i = pl.program_id(0)
        if stacked:
            wk = w.shape[2]
            dh = _dot_nt(dy_refs[0][:, 0:wk], w_ref[0])
            for k in range(1, w.shape[0]):
                dh = dh + _dot_nt(dy_refs[0][:, k * wk:(k + 1) * wk], w_ref[k])
        else:
            dh, off = None, 0
            for dy_ref, width in zip(dy_refs, widths):
                part = _dot_nt(dy_ref[...], w_ref[:, off:off + width])
                dh = part if dh is None else dh + part
                off += width
        xv = x_ref[...]
        r = lax.rsqrt(jnp.mean(xv * xv, axis=-1, keepdims=True) + EPS)
        xn = xv * r
        dxn = dh * g_ref[...]
        dx = dres_ref[...] + r * (dxn - xn * jnp.mean(dxn * xn, axis=-1, keepdims=True))
        dx_ref[...] = dx
        dxb_ref[...] = dx.astype(BF16)

        @pl.when(i == 0)
        def _():
            dg_ref[...] = jnp.zeros_like(dg_ref)

        dg_ref[...] += jnp.sum(dh * xn, axis=0, keepdims=True)

    tile = lambda c: pl.BlockSpec((tm, c), lambda i: (i, 0))
    return _tiled_call(
        body, name=name, grid=(t // tm,),
        in_specs=[tile(c) for c in widths] + [_resident(w.shape), tile(d), _full((1, d)), tile(d)],
        out_specs=[tile(d), tile(d), _full((1, d))],
        out_shape=[jax.ShapeDtypeStruct((t, d), F32), jax.ShapeDtypeStruct((t, d), BF16),
                   jax.ShapeDtypeStruct((1, d), F32)],
        operands=(*dys, w, x, g.reshape(1, d), dres), comm=comm)


def _conv_windows(ext_ref, tm):
    ext = ext_ref[...]
    return [pltpu.roll(ext, 3 - k, 0)[SUBLANES:SUBLANES + tm, :] for k in range(3)] + [ext[SUBLANES:SUBLANES + tm, :]]


def _conv_taps(windows, w_ref):
    acc = windows[3] * w_ref[3:4, :]
    for k in range(3):
        acc = acc + windows[k] * w_ref[k:k + 1, :]
    return acc


def _ssd_chunk_terms(dtr, bias_row, alog_row):
    q = SSD_CHUNK
    row = lax.broadcasted_iota(jnp.int32, (q, q), 0)
    col = lax.broadcasted_iota(jnp.int32, (q, q), 1)
    ltri = (col <= row).astype(F32)
    dt = _softplus(dtr + bias_row)
    a_row = -jnp.exp(alog_row)
    la = dt * a_row
    cum = _dot_sel_a(ltri, la)
    cum_t = _dot_tn_sel(la, (row <= col).astype(F32), "b")
    return dt, a_row, cum, cum_t, ltri, row, col


def _head_maps():
    di = SSD_HEADS * SSD_HEADDIM
    expand = (lax.broadcasted_iota(jnp.int32, (LANES, di), 1) // SSD_HEADDIM
              == lax.broadcasted_iota(jnp.int32, (LANES, di), 0)).astype(F32)
    collapse = (lax.broadcasted_iota(jnp.int32, (di, LANES), 0) // SSD_HEADDIM
                == lax.broadcasted_iota(jnp.int32, (di, LANES), 1)).astype(F32)
    return expand, collapse


def _ssd_decay(cum, cum_t, h, causal):
    seg = cum[:, h:h + 1] - cum_t[h:h + 1, :]
    return jnp.where(causal, jnp.exp(jnp.where(causal, seg, 0.0)), 0.0)


def _pair_block_diag(x2):
    low = lax.broadcasted_iota(jnp.int32, x2.shape, 1) < SSD_HEADDIM
    zero = jnp.zeros_like(x2)
    return jnp.concatenate([jnp.where(low, x2, zero), jnp.where(low, zero, x2)], axis=0)


def ssd_fwd(xbc_raw, conv_w8, conv_b, dtr, z, bias_row, alog_row, d_row, norm_g, *, name, comm=None):
    t = xbc_raw.shape[0]
    q, p, n = SSD_CHUNK, SSD_HEADDIM, SSD_STATE
    nc = t // q
    di = SSD_HEADS * p
    gw = di // SSD_GROUPS

    def body(raw_ref, cw_ref, cb_ref, dtr_ref, z_ref, bias_ref, alog_ref, d_ref, g_ref,
             ya_ref, yssd_ref, st_ref, state, ext, xbc_s):
        ci = pl.program_id(0)

        @pl.when(ci == 0)
        def _():
            state[...] = jnp.zeros_like(state)
            ext[0:SUBLANES, :] = jnp.zeros((SUBLANES, CONV_DIM), F32)

        ext[SUBLANES:SUBLANES + q, :] = raw_ref[...].astype(F32)
        xc = _conv_taps(_conv_windows(ext, q), cw_ref) + cb_ref[...]
        ext[0:SUBLANES, :] = ext[q:q + SUBLANES, :]
        xbc_s[...] = xc * _sigmoid(xc)
        st_ref[0] = state[...]
        dt, _, cum, cum_t, _, row, col = _ssd_chunk_terms(dtr_ref[...], bias_ref[...], alog_ref[...])
        causal = row >= col
        expand, _ = _head_maps()
        dt_full = _dot_sel_b(dt, expand)
        cum_full = _dot_sel_b(cum, expand)
        last_full = cum_full[q - 1:q, :]
        xs = xbc_s[:, 0:di]
        xdt = xs * dt_full
        xd = (xdt * jnp.exp(last_full - cum_full)).astype(BF16)
        xdt_b = xdt.astype(BF16)
        e_full = jnp.exp(cum_full)
        eend_full = jnp.exp(last_full)
        for g in range(SSD_GROUPS):
            slab = slice(g * gw, (g + 1) * gw)
            bg = xbc_s[:, di + g * n:di + (g + 1) * n].astype(BF16)
            cg = xbc_s[:, di + (SSD_GROUPS + g) * n:di + (SSD_GROUPS + g + 1) * n].astype(BF16)
            cb = _dot_nt(cg, bg)
            st_g = state[:, slab]
            pairs = []
            for pr in range(SSD_HPG // 2):
                h0 = g * SSD_HPG + 2 * pr
                m0 = (cb * _ssd_decay(cum, cum_t, h0, causal)).astype(BF16)
                m1 = (cb * _ssd_decay(cum, cum_t, h0 + 1, causal)).astype(BF16)
                pairs.append(_dot(jnp.concatenate([m0, m1], axis=1), _pair_block_diag(xdt_b[:, h0 * p:(h0 + 2) * p])))
            y = jnp.concatenate(pairs, axis=1) + e_full[:, slab] * _dot(cg, st_g.astype(BF16))
            yssd_ref[:, slab] = y + d_ref[:, slab] * xs[:, slab]
            state[:, slab] = eend_full[:, slab] * st_g + _dot_tn(bg, xd[:, slab])
        zv = z_ref[...].astype(F32)
        ya1 = yssd_ref[...] * (zv * _sigmoid(zv))
        for g in range(SSD_GROUPS):
            gsl = slice(g * gw, (g + 1) * gw)
            sl = ya1[:, gsl]
            rg = lax.rsqrt(jnp.mean(sl * sl, axis=-1, keepdims=True) + EPS)
            ya_ref[:, gsl] = (sl * rg * g_ref[:, gsl]).astype(BF16)

    blk = lambda w, j: pl.BlockSpec((q, w), lambda c: (c, j))
    return _tiled_call(
        body, name=name, grid=(nc,),
        in_specs=[blk(CONV_DIM, 0), _full((SUBLANES, CONV_DIM)), _full((1, CONV_DIM)), blk(LANES, 0), blk(di, 0),
                  _full((1, LANES)), _full((1, LANES)), _full((1, di)), _full((1, di))],
        out_specs=[blk(di, 0), blk(di, 0), pl.BlockSpec((1, n, di), lambda c: (c, 0, 0))],
        out_shape=[jax.ShapeDtypeStruct((t, di), BF16), jax.ShapeDtypeStruct((t, di), F32),
                   jax.ShapeDtypeStruct((nc, n, di), F32)],
        scratch_shapes=[pltpu.VMEM((n, di), F32), pltpu.VMEM((SUBLANES + q, CONV_DIM), F32), pltpu.VMEM((q, CONV_DIM), F32)],
        operands=(xbc_raw, conv_w8, conv_b.reshape(1, CONV_DIM), dtr, z, bias_row, alog_row, d_row,
                  norm_g.reshape(1, di)), comm=comm)


def ssd_bwd(dya, yssd, z, xbc_raw, conv_w8, conv_b, dtr, states, bias_row, alog_row, d_row, norm_g, *, name, comm=None):
    t = xbc_raw.shape[0]
    q, p, n = SSD_CHUNK, SSD_HEADDIM, SSD_STATE
    nc = t // q
    di = SSD_HEADS * p
    gw = di // SSD_GROUPS
    per = q // (2 * SUBLANES)

    def body(dya_ref, yssd_ref, z_ref, raw_ref, prev_ref, cw_ref, cb_ref, dtr_ref, st_ref, bias_ref, alog_ref, d_ref,
             g_ref, dzdt_ref, draw_ref, dng_ref, dd_ref, dalog_ref, dbias_ref, dcb_ref, dcw_ref,
             dstate, dyssd, xbc_s, dxbc_ref, ext, aft):
        ci = pl.program_id(0)

        @pl.when(ci == 0)
        def _():
            dstate[...] = jnp.zeros_like(dstate)
            aft[q:q + SUBLANES, :] = jnp.zeros((SUBLANES, CONV_DIM), F32)
            for r in (dng_ref, dd_ref, dalog_ref, dbias_ref, dcb_ref, dcw_ref):
                r[...] = jnp.zeros_like(r)

        ext[0:SUBLANES, :] = jnp.where(ci == nc - 1, 0.0, prev_ref[SUBLANES:2 * SUBLANES, :].astype(F32))
        ext[SUBLANES:SUBLANES + q, :] = raw_ref[...].astype(F32)
        windows = _conv_windows(ext, q)
        xc = _conv_taps(windows, cw_ref) + cb_ref[...]
        sig_c = _sigmoid(xc)
        xbc_s[...] = xc * sig_c

        zv = z_ref[...].astype(F32)
        sz = _sigmoid(zv)
        silu_z = zv * sz
        yv = yssd_ref[...]
        ya1 = yv * silu_z
        dyav = dya_ref[...]
        for g in range(SSD_GROUPS):
            gsl = slice(g * gw, (g + 1) * gw)
            sl = ya1[:, gsl]
            rg = lax.rsqrt(jnp.mean(sl * sl, axis=-1, keepdims=True) + EPS)
            ya2 = sl * rg
            dy_g = dyav[:, gsl]
            dng_ref[:, gsl] += jnp.sum(dy_g * ya2, axis=0, keepdims=True)
            dya2 = dy_g * g_ref[:, gsl]
            dya1 = rg * (dya2 - ya2 * jnp.mean(dya2 * ya2, axis=-1, keepdims=True))
            dyssd[:, gsl] = dya1 * silu_z[:, gsl]
            dzdt_ref[:, gsl] = (dya1 * yv[:, gsl] * (sz[:, gsl] * (1.0 + zv[:, gsl] * (1.0 - sz[:, gsl])))).astype(BF16)

        dtr_v = dtr_ref[...]
        dt, a_row, cum, cum_t, ltri, row, col = _ssd_chunk_terms(dtr_v, bias_ref[...], alog_ref[...])
        causal = row >= col
        expand, collapse = _head_maps()
        lane = lax.broadcasted_iota(jnp.int32, (1, LANES), 1)
        sub = lax.broadcasted_iota(jnp.int32, (LANES, 1), 0)
        is_last = lax.broadcasted_iota(jnp.int32, (q, 1), 0) == q - 1
        low = lax.broadcasted_iota(jnp.int32, (q, 2 * p), 1) < p
        dt_full = _dot_sel_b(dt, expand)
        cum_full = _dot_sel_b(cum, expand)
        last_full = cum_full[q - 1:q, :]
        e_full = jnp.exp(cum_full)
        eend_full = jnp.exp(last_full)
        dend_full = jnp.exp(last_full - cum_full)
        xs = xbc_s[:, 0:di]
        xdt = xs * dt_full
        xdt_b = xdt.astype(BF16)
        xd_b = (xdt * dend_full).astype(BF16)
        dy_all = dyssd[...]
        dy_b = dy_all.astype(BF16)
        dg_b = (dy_all * e_full).astype(BF16)
        dcum_mat = jnp.zeros((q, LANES), F32)
        rmat = jnp.zeros((LANES, q), F32)
        yoff_prod, bds_list, dx_list, dlast_parts = [], [], [], []
        for g in range(SSD_GROUPS):
            slab = slice(g * gw, (g + 1) * gw)
            bg = xbc_s[:, di + g * n:di + (g + 1) * n].astype(BF16)
            cg = xbc_s[:, di + (SSD_GROUPS + g) * n:di + (SSD_GROUPS + g + 1) * n].astype(BF16)
            cb = _dot_nt(cg, bg)
            st_g = st_ref[0, :, slab]
            st_b = st_g.astype(BF16)
            dsn = dstate[:, slab]
            dsn_b = dsn.astype(BF16)
            yoff_prod.append(dy_all[:, slab] * e_full[:, slab] * _dot(cg, st_b))
            dcg = _dot_nt(dg_b[:, slab], st_b)
            dstate[:, slab] = _dot_tn(cg, dg_b[:, slab]) + eend_full[:, slab] * dsn
            bds_list.append(_dot(bg, dsn_b))
            dbg = _dot_nt(xd_b[:, slab], dsn_b)
            dlast_parts.append(jnp.sum(dsn * st_g, axis=0, keepdims=True))
            dcb = jnp.zeros((q, q), F32)
            dx_pairs = []
            for pr in range(SSD_HPG // 2):
                h0 = g * SSD_HPG + 2 * pr
                ps = slice(h0 * p, (h0 + 2) * p)
                l0 = _ssd_decay(cum, cum_t, h0, causal)
                l1 = _ssd_decay(cum, cum_t, h0 + 1, causal)
                m0, m1 = cb * l0, cb * l1
                dyp = dy_b[:, ps]
                zero = jnp.zeros_like(dyp)
                dy0, dy1 = jnp.where(low, dyp, zero), jnp.where(low, zero, dyp)
                dm0 = _dot_nt(dy0, xdt_b[:, ps])
                dm1 = _dot_nt(dy1, xdt_b[:, ps])
                w0, w1 = dm0 * m0, dm1 * m1
                dcum_mat = (dcum_mat + jnp.sum(w0, axis=1, keepdims=True) * (lane == h0).astype(F32)
                            + jnp.sum(w1, axis=1, keepdims=True) * (lane == h0 + 1).astype(F32))
                rmat = (rmat + jnp.where(sub == h0, jnp.sum(w0, axis=0, keepdims=True), 0.0)
                        + jnp.where(sub == h0 + 1, jnp.sum(w1, axis=0, keepdims=True), 0.0))
                dcb = dcb + dm0 * l0 + dm1 * l1
                dx_pairs.append(_dot_tn(jnp.concatenate([m0.astype(BF16), m1.astype(BF16)], axis=0),
                                        jnp.concatenate([dy0, dy1], axis=0)))
            dx_list.append(jnp.concatenate(dx_pairs, axis=1))
            dcb_b = dcb.astype(BF16)
            dxbc_ref[:, di + g * n:di + (g + 1) * n] = dbg + _dot_tn(dcb_b, cg)
            dxbc_ref[:, di + SSD_GROUPS * n + g * n:di + SSD_GROUPS * n + (g + 1) * n] = dcg + _dot(dcb_b, bg)
        bds_all = jnp.concatenate(bds_list, axis=1)
        dx_all = jnp.concatenate(dx_list, axis=1) + dend_full * bds_all
        last_row = cum[q - 1:q, :]
        qv = _dot_sel_b(xdt * bds_all, collapse) * jnp.exp(last_row - cum)
        dcum_mat = dcum_mat + _dot_sel_b(jnp.concatenate(yoff_prod, axis=1), collapse) - qv
        dlast_full = jnp.broadcast_to(jnp.concatenate(dlast_parts, axis=1), (SUBLANES, di))
        dlast_row = _dot_sel_b(dlast_full, collapse)[0:1, :] * jnp.exp(last_row) + jnp.sum(qv, axis=0, keepdims=True)
        dcum_mat = dcum_mat + jnp.where(is_last, dlast_row, 0.0)
        dla = _dot_tn_sel(ltri, dcum_mat, "a") - _dot_nt_sel_a((row <= col).astype(F32), rmat)
        ddt_mat = _dot_sel_b(dx_all * xs, collapse) + dla * a_row
        dxbc_ref[:, 0:di] = d_ref[...] * dy_all + dx_all * dt_full
        ddtr = ddt_mat * _sigmoid(dtr_v + bias_ref[...])
        dzdt_ref[:, di:di + LANES] = ddtr.astype(BF16)
        dd_full = jnp.broadcast_to(jnp.sum(dy_all * xs, axis=0, keepdims=True), (SUBLANES, di))
        dd_ref[...] += _dot_sel_b(dd_full, collapse)[0:1, :]
        dalog_ref[...] += jnp.sum(dla * dt, axis=0, keepdims=True) * a_row
        dbias_ref[...] += jnp.sum(ddtr, axis=0, keepdims=True)

        dxc = dxbc_ref[...] * (sig_c * (1.0 + xc * (1.0 - sig_c)))
        dcb_ref[...] += jnp.sum(dxc, axis=0, keepdims=True)
        taps = [jnp.sum(dxc * windows[k], axis=0, keepdims=True) for k in range(SSD_CONV)]
        dcw_ref[...] += jnp.concatenate(taps + [jnp.zeros((SUBLANES - SSD_CONV, CONV_DIM), F32)], axis=0)
        aft[0:q, :] = dxc
        after = aft[...]
        draw = dxc * cw_ref[3:4, :]
        for j in range(1, SSD_CONV):
            draw = draw + pltpu.roll(after, q + SUBLANES - j, 0)[0:q, :] * cw_ref[3 - j:4 - j, :]
        draw_ref[...] = draw.astype(BF16)
        aft[q:q + SUBLANES, :] = dxc[0:SUBLANES, :]

    blk = lambda w, j: pl.BlockSpec((q, w), lambda c: (nc - 1 - c, j))
    row_out = lambda w: jax.ShapeDtypeStruct((1, w), F32)
    return _tiled_call(
        body, name=name, grid=(nc,),
        in_specs=[blk(di, 0), blk(di, 0), blk(di, 0), blk(CONV_DIM, 0),
                  pl.BlockSpec((2 * SUBLANES, CONV_DIM), lambda c: (jnp.maximum((nc - 1 - c) * per - 1, 0), 0)),
                  _full((SUBLANES, CONV_DIM)), _full((1, CONV_DIM)), blk(LANES, 0),
                  pl.BlockSpec((1, n, di), lambda c: (nc - 1 - c, 0, 0)),
                  _full((1, LANES)), _full((1, LANES)), _full((1, di)), _full((1, di))],
        out_specs=[blk(di + LANES, 0), blk(CONV_DIM, 0),
                   _full((1, di)), _full((1, LANES)), _full((1, LANES)), _full((1, LANES)),
                   _full((1, CONV_DIM)), _full((SUBLANES, CONV_DIM))],
        out_shape=[jax.ShapeDtypeStruct((t, di + LANES), BF16), jax.ShapeDtypeStruct((t, CONV_DIM), BF16),
                   row_out(di), row_out(LANES), row_out(LANES), row_out(LANES),
                   row_out(CONV_DIM), jax.ShapeDtypeStruct((SUBLANES, CONV_DIM), F32)],
        scratch_shapes=[pltpu.VMEM((n, di), F32), pltpu.VMEM((q, di), F32), pltpu.VMEM((q, CONV_DIM), F32),
                        pltpu.VMEM((q, CONV_DIM), F32), pltpu.VMEM((SUBLANES + q, CONV_DIM), F32),
                        pltpu.VMEM((q + SUBLANES, CONV_DIM), F32)],
        operands=(dya, yssd, z, xbc_raw, xbc_raw, conv_w8, conv_b.reshape(1, CONV_DIM), dtr, states, bias_row,
                  alog_row, d_row, norm_g.reshape(1, di)),
        comm=comm)


S5_BLOCKS = 4
S5_BC = S5_WIDTH // S5_BLOCKS
S5_BS = S5_N // S5_BLOCKS


def _s5_tables(ar, ai, reverse):
    pows = [(ar, ai)]
    for _ in range(SUBLANES - 1):
        pr, pi = pows[-1]
        pows.append((pr * ar - pi * ai, pr * ai + pi * ar))
    row = lax.broadcasted_iota(jnp.int32, (SUBLANES, ar.shape[1]), 0)
    out = []
    for k in (1, 2, 4):
        pr, pi = pows[k - 1]
        mask = (row < SUBLANES - k) if reverse else (row >= k)
        out += [jnp.where(mask, pr, 0.0), jnp.where(mask, pi, 0.0)]
    order = list(range(SUBLANES))
    if reverse:
        order = order[::-1]
    out.append(jnp.concatenate([pows[e][0] for e in order], axis=0))
    out.append(jnp.concatenate([pows[e][1] for e in order], axis=0))
    return out


def s5_interleave(a, tm):
    t, c = a.shape
    return a.reshape(t // tm, SUBLANES, tm // SUBLANES, c).transpose(0, 2, 1, 3).reshape(t, c)


def s5_deinterleave(a, tm):
    t, c = a.shape
    return a.reshape(t // tm, tm // SUBLANES, SUBLANES, c).transpose(0, 2, 1, 3).reshape(t, c)


def _s5_scan_setup(ab_ref, base, tab, seg, reverse):
    ar = ab_ref[0:1, :]
    ai = -ab_ref[1:2, :] if reverse else ab_ref[1:2, :]
    base[0:1, :] = ar
    base[1:2, :] = ai
    assert seg & (seg - 1) == 0
    pr, pi = ar, ai
    for _ in range(seg.bit_length() - 1):
        pr, pi = pr * pr - pi * pi, 2.0 * pr * pi
    for k, tv in enumerate(_s5_tables(pr, pi, reverse)):
        tab[k] = tv


def _s5_scan(s_scr, base, tab, carry, tm, reverse):
    seg = tm // SUBLANES
    width = S5_BS
    first = lax.broadcasted_iota(jnp.int32, (SUBLANES, width), 0) == (SUBLANES - 1 if reverse else 0)

    def rows(i):
        step = (seg - 1 - i) if reverse else i
        return pl.ds(pl.multiple_of(step * SUBLANES, SUBLANES), SUBLANES)

    for lc in range(S5_N // width):
        lr = slice(lc * width, (lc + 1) * width)
        li = slice(S5_N + lc * width, S5_N + (lc + 1) * width)
        ar = jnp.broadcast_to(base[0:1, lr], (SUBLANES, width))
        ai = jnp.broadcast_to(base[1:2, lr], (SUBLANES, width))

        def advance(i, x, lr=lr, li=li, ar=ar, ai=ai):
            xr, xi = x
            return ar * xr - ai * xi + s_scr[rows(i), lr], ar * xi + ai * xr + s_scr[rows(i), li]

        zero = jnp.zeros((SUBLANES, width), F32)
        fr, fi = lax.fori_loop(0, seg, advance, (zero, zero))

        tb = [tab[k, :, lr] for k in range(8)]
        for lvl, k in enumerate((1, 2, 4)):
            sh = (SUBLANES - k) if reverse else k
            pr, pi = tb[2 * lvl], tb[2 * lvl + 1]
            rr, ri = pltpu.roll(fr, sh, 0), pltpu.roll(fi, sh, 0)
            fr, fi = fr + pr * rr - pi * ri, fi + pr * ri + pi * rr
        cr, ci = carry[0:1, lr], carry[0:1, li]
        fr, fi = fr + tb[6] * cr - tb[7] * ci, fi + tb[6] * ci + tb[7] * cr
        last = 0 if reverse else SUBLANES - 1
        carry[0:1, lr] = fr[last:last + 1, :]
        carry[0:1, li] = fi[last:last + 1, :]
        sh = (SUBLANES - 1) if reverse else 1
        entering = (jnp.where(first, cr, pltpu.roll(fr, sh, 0)), jnp.where(first, ci, pltpu.roll(fi, sh, 0)))

        def rerun(i, x, lr=lr, li=li, advance=advance):
            xr, xi = advance(i, x)
            s_scr[rows(i), lr] = xr
            s_scr[rows(i), li] = xi
            return xr, xi

        lax.fori_loop(0, seg, rerun, entering)


def _s5_scratch(tm):
    return [pltpu.VMEM((tm, 2 * S5_N), F32), pltpu.VMEM((SUBLANES, S5_N), F32),
            pltpu.VMEM((8, SUBLANES, S5_N), F32), pltpu.VMEM((SUBLANES, 2 * S5_N), F32)]


def _s5_put(scr, j, val):
    scr[:, j * S5_BS:(j + 1) * S5_BS] = val[:, :S5_BS]
    scr[:, S5_N + j * S5_BS:S5_N + (j + 1) * S5_BS] = val[:, S5_BS:]


def _s5_get(scr, j):
    return jnp.concatenate([scr[:, j * S5_BS:(j + 1) * S5_BS], scr[:, S5_N + j * S5_BS:S5_N + (j + 1) * S5_BS]], axis=1)


def s5_fwd(u5, bbc, ccc, ab8, d_row, wglu, bglu_row, *, tm, name, comm=None):
    t, w = u5.shape

    def body(u_ref, bb_ref, cc_ref, ab_ref, d_ref, wg_ref, bg_ref, s_ref, ys_ref, yb_ref, s_scr, base, tab, carry):
        i = pl.program_id(0)

        @pl.when(i == 0)
        def _():
            carry[...] = jnp.zeros_like(carry)
            _s5_scan_setup(ab_ref, base, tab, tm // SUBLANES, False)

        ub = u_ref[...]
        u = ub.astype(F32)
        for j in range(S5_BLOCKS):
            uj = ub[:, j * S5_BC:(j + 1) * S5_BC]
            _s5_put(s_scr, j, _dot(uj, bb_ref[j * S5_BC:(j + 1) * S5_BC, :]))
        _s5_scan(s_scr, base, tab, carry, tm, False)
        s_ref[...] = s_scr[...]
        ys = []
        for j in range(S5_BLOCKS):
            ys.append(_dot(_s5_get(s_scr, j).astype(BF16), cc_ref[:, j * S5_BC:(j + 1) * S5_BC]))
        y = jnp.concatenate(ys, axis=1) + d_ref[...] * u
        ys_ref[...] = y
        yb1 = _gelu(y)
        tt = _dot(yb1.astype(BF16), wg_ref[...]) + bg_ref[...]
        yb_ref[...] = (yb1 * _sigmoid(tt)).astype(BF16)

    tile = lambda c: pl.BlockSpec((tm, c), lambda i: (i, 0))
    return _tiled_call(
        body, name=name, grid=(t // tm,),
        in_specs=[tile(w), _full((w, 2 * S5_BS)), _full((2 * S5_BS, w)), _full((SUBLANES, S5_N)), _full((1, w)),
                  _full((w, w)), _full((1, w))],
        out_specs=[tile(2 * S5_N), tile(w), tile(w)],
        out_shape=[jax.ShapeDtypeStruct((t, 2 * S5_N), F32), jax.ShapeDtypeStruct((t, w), F32),
                   jax.ShapeDtypeStruct((t, w), BF16)],
        scratch_shapes=_s5_scratch(tm),
        operands=(u5, bbc, ccc, ab8, d_row, wglu, bglu_row), comm=comm)


def s5_bwd(dyb, ys5, u5, s, bbc, ccc, ab8, d_row, wglu, bglu_row, *, tm, name, comm=None):
    t, w = u5.shape
    nt = t // tm
    per = tm // SUBLANES

    def body(dyb_ref, ys_ref, u_ref, s_ref, sprev_ref, bb_ref, cc_ref, ab_ref, d_ref, wg_ref, bg_ref,
             du_ref, dbb_ref, dcc_ref, dab_ref, dd_ref, dwg_ref, dbg_ref, g_scr, base, tab, carry):
        i = pl.program_id(0)

        @pl.when(i == 0)
        def _():
            carry[...] = jnp.zeros_like(carry)
            _s5_scan_setup(ab_ref, base, tab, tm // SUBLANES, True)
            for r in (dbb_ref, dcc_ref, dab_ref, dd_ref, dwg_ref, dbg_ref):
                r[...] = jnp.zeros_like(r)

        y = ys_ref[...]
        ub = u_ref[...]
        u = ub.astype(F32)
        yb1 = _gelu(y)
        yb1b = yb1.astype(BF16)
        sg = _sigmoid(_dot(yb1b, wg_ref[...]) + bg_ref[...])
        dybv = dyb_ref[...]
        dtt = dybv * yb1 * sg * (1.0 - sg)
        dttb = dtt.astype(BF16)
        dyb1 = dybv * sg + _dot_nt(dttb, wg_ref[...])
        dwg_ref[...] += _dot_tn(yb1b, dttb)
        dbg_ref[...] += jnp.sum(dtt, axis=0, keepdims=True)
        dy = dyb1 * _gelu_grad(y)
        dd_ref[...] += jnp.sum(dy * u, axis=0, keepdims=True)
        dyh = dy.astype(BF16)
        for j in range(S5_BLOCKS):
            cs = slice(j * S5_BC, (j + 1) * S5_BC)
            _s5_put(g_scr, j, _dot_nt(dyh[:, cs], cc_ref[:, cs]))
        _s5_scan(g_scr, base, tab, carry, tm, True)
        dus = []
        for j in range(S5_BLOCKS):
            cs = slice(j * S5_BC, (j + 1) * S5_BC)
            gb = _s5_get(g_scr, j).astype(BF16)
            dus.append(_dot_nt(gb, bb_ref[cs, :]))
            dbb_ref[cs, :] += _dot_tn(ub[:, cs], gb)
            dcc_ref[:, cs] += _dot_tn(_s5_get(s_ref, j).astype(BF16), dyh[:, cs])
        du_ref[...] = (jnp.concatenate(dus, axis=1) + d_ref[...] * dy).astype(BF16)
        top = lax.broadcasted_iota(jnp.int32, (SUBLANES, S5_BS), 0) == 0

        def corr(g_sl, s_sl):
            before = jnp.where(i == nt - 1, 0.0, sprev_ref[SUBLANES - 1:SUBLANES, s_sl])
            wrap = jnp.where(top, before, pltpu.roll(s_ref[tm - SUBLANES:tm, s_sl], 1, 0))
            return (jnp.sum(g_scr[SUBLANES:tm, g_sl] * s_ref[0:tm - SUBLANES, s_sl], axis=0, keepdims=True)
                    + jnp.sum(g_scr[0:SUBLANES, g_sl] * wrap, axis=0, keepdims=True))

        for j in range(S5_BLOCKS):
            lr = slice(j * S5_BS, (j + 1) * S5_BS)
            li = slice(S5_N + j * S5_BS, S5_N + (j + 1) * S5_BS)
            dab_ref[0:1, lr] += corr(lr, lr) + corr(li, li)
            dab_ref[1:2, lr] += corr(li, lr) - corr(lr, li)

    tile = lambda c: pl.BlockSpec((tm, c), lambda i: (nt - 1 - i, 0))
    acc = lambda r, c: jax.ShapeDtypeStruct((r, c), F32)
    return _tiled_call(
        body, name=name, grid=(nt,),
        in_specs=[tile(w), tile(w), tile(w), tile(2 * S5_N),
                  pl.BlockSpec((SUBLANES, 2 * S5_N), lambda i: (jnp.maximum((nt - 1 - i) * per - 1, 0), 0)),
                  _full((w, 2 * S5_BS)), _full((2 * S5_BS, w)), _full((SUBLANES, S5_N)), _full((1, w)),
                  _full((w, w)), _full((1, w))],
        out_specs=[tile(w), _full((w, 2 * S5_BS)), _full((2 * S5_BS, w)), _full((SUBLANES, S5_N)), _full((1, w)),
                   _full((w, w)), _full((1, w))],
        out_shape=[jax.ShapeDtypeStruct((t, w), BF16), acc(w, 2 * S5_BS), acc(2 * S5_BS, w), acc(SUBLANES, S5_N),
                   acc(1, w), acc(w, w), acc(1, w)],
        scratch_shapes=_s5_scratch(tm),
        operands=(dyb, ys5, u5, s, s, bbc, ccc, ab8, d_row, wglu, bglu_row), comm=comm)


def s5_discretize(a_re, a_im, log_dt, b_re, b_im, *, name):
    n = a_re.shape[0]

    def body(ar_ref, ai_ref, dt_ref, br_ref, bi_ref, ab_ref, bbr_ref, bbi_ref):
        ar, ai, dtv = ar_ref[...], ai_ref[...], jnp.exp(dt_ref[...])
        mag = jnp.exp(ar * dtv)
        abr = mag * jnp.cos(ai * dtv)
        abi = mag * jnp.sin(ai * dtv)
        den = ar * ar + ai * ai
        nr = abr - 1.0
        cr = (nr * ar + abi * ai) / den
        ci = (abi * ar - nr * ai) / den
        ab_ref[:, 0:1] = abr
        ab_ref[:, 1:2] = abi
        br, bi = br_ref[...], bi_ref[...]
        bbr_ref[...] = cr * br - ci * bi
        bbi_ref[...] = cr * bi + ci * br

    col = jax.ShapeDtypeStruct((n, 2), F32)
    mat = jax.ShapeDtypeStruct((n, S5_GROUP), F32)
    return pl.pallas_call(body, name=name, out_shape=[col, mat, mat])(a_re, a_im, log_dt, b_re, b_im)


def s5_discretize_bwd(a_re, a_im, log_dt, b_re, b_im, dab, dbb_re, dbb_im, *, name):
    n = a_re.shape[0]

    def body(ar_ref, ai_ref, dt_ref, br_ref, bi_ref, dab_ref, dbr_ref, dbi_ref, da_ref, ddt_ref, gbr_ref, gbi_ref):
        ar, ai, dtv = ar_ref[...], ai_ref[...], jnp.exp(dt_ref[...])
        mag = jnp.exp(ar * dtv)
        abr = mag * jnp.cos(ai * dtv)
        abi = mag * jnp.sin(ai * dtv)
        den = ar * ar + ai * ai
        nr = abr - 1.0
        cr = (nr * ar + abi * ai) / den
        ci = (abi * ar - nr * ai) / den
        br, bi = br_ref[...], bi_ref[...]
        dbr, dbi = dbr_ref[...], dbi_ref[...]
        gbr_ref[...] = cr * dbr + ci * dbi
        gbi_ref[...] = cr * dbi - ci * dbr
        gcr = jnp.sum(dbr * br + dbi * bi, axis=1, keepdims=True)
        gci = jnp.sum(dbi * br - dbr * bi, axis=1, keepdims=True)
        ilr, ili = ar / den, -ai / den
        gabr = dab_ref[:, 0:1] + (ilr * gcr + ili * gci)
        gabi = dab_ref[:, 1:2] + (ilr * gci - ili * gcr)
        t1r, t1i = dtv * abr, dtv * abi
        t2r, t2i = -(cr * ilr - ci * ili), -(cr * ili + ci * ilr)
        da_ref[:, 0:1] = (t1r * gabr + t1i * gabi) + (t2r * gcr + t2i * gci)
        da_ref[:, 1:2] = (t1r * gabi - t1i * gabr) + (t2r * gci - t2i * gcr)
        lr, li = ar * abr - ai * abi, ar * abi + ai * abr
        dlog = (lr * gabr + li * gabi) * dtv
        ddt_ref[...] = jnp.sum(dlog.reshape(S5_GROUPS, S5_STATE, 1), axis=1)

    col2 = jax.ShapeDtypeStruct((n, 2), F32)
    col1 = jax.ShapeDtypeStruct((S5_GROUPS, 1), F32)
    mat = jax.ShapeDtypeStruct((n, S5_GROUP), F32)
    return pl.pallas_call(body, name=name, out_shape=[col2, col1, mat, mat])(
        a_re, a_im, log_dt, b_re, b_im, dab, dbb_re, dbb_im)


def merge_fwd(ya, yb, graw, x, wb, wout, *, tm, name):
    t, d = x.shape

    def body(ya_ref, yb_ref, g_ref, x_ref, wb_ref, wo_ref, x1_ref, pa_ref, pb_ref):
        pa = _dot(ya_ref[...], wb_ref[0:d, :])
        pb = _dot(yb_ref[...], wb_ref[d:, :])
        gate = _sigmoid(g_ref[...].astype(F32))
        merged = gate[:, :d] * pa + gate[:, d:] * pb
        x1_ref[...] = x_ref[...] + _dot(merged.astype(BF16), wo_ref[...])
        pa_ref[...] = pa.astype(BF16)
        pb_ref[...] = pb.astype(BF16)

    tile = lambda c: pl.BlockSpec((tm, c), lambda i: (i, 0))
    sds = jax.ShapeDtypeStruct
    return pl.pallas_call(
        body, name=name, grid=(t // tm,),
        in_specs=[tile(d), tile(S5_WIDTH), tile(2 * d), tile(d), _resident((d + S5_WIDTH, d)), _resident((d, d))],
        out_specs=[tile(d), tile(d), tile(d)], out_shape=[sds((t, d), F32), sds((t, d), BF16), sds((t, d), BF16)],
        compiler_params=_params("parallel"),
    )(ya, yb, graw, x, wb, wout)


def merge_bwd(dx1b, graw, pa, pb, wb, wout, *, tm, name):
    t, d = pa.shape

    def body(dx_ref, g_ref, pa_ref, pb_ref, wb_ref, wo_ref,
             mg_ref, dg_ref, dpa_ref, dpb_ref, dya_ref, dyb_ref):
        dm = _dot_nt(dx_ref[...], wo_ref[...])
        gate = _sigmoid(g_ref[...].astype(F32))
        g0, g1 = gate[:, :d], gate[:, d:]
        pa, pb = pa_ref[...].astype(F32), pb_ref[...].astype(F32)
        mg_ref[...] = (g0 * pa + g1 * pb).astype(BF16)
        dg_ref[:, :d] = (dm * pa * g0 * (1.0 - g0)).astype(BF16)
        dg_ref[:, d:] = (dm * pb * g1 * (1.0 - g1)).astype(BF16)
        dpa = (dm * g0).astype(BF16)
        dpb = (dm * g1).astype(BF16)
        dpa_ref[...] = dpa
        dpb_ref[...] = dpb
        dya_ref[...] = _dot_nt(dpa, wb_ref[0:d, :])
        dyb_ref[...] = _dot_nt(dpb, wb_ref[d:, :])

    tile = lambda c: pl.BlockSpec((tm, c), lambda i: (i, 0))
    sds = jax.ShapeDtypeStruct
    return pl.pallas_call(
        body, name=name, grid=(t // tm,),
        in_specs=[tile(d), tile(2 * d), tile(d), tile(d), _resident((d + S5_WIDTH, d)), _resident((d, d))],
        out_specs=[tile(d), tile(2 * d), tile(d), tile(d), tile(d), tile(S5_WIDTH)],
        out_shape=[sds((t, d), BF16), sds((t, 2 * d), BF16), sds((t, d), BF16), sds((t, d), BF16),
                   sds((t, d), F32), sds((t, S5_WIDTH), F32)],
        compiler_params=_params("parallel"),
    )(dx1b, graw, pa, pb, wb, wout)


def mlp_out_loss(a1, x1, w2, gf, target, *, tm, name):
    t, d = x1.shape
    f = a1.shape[1]

    def body(a_ref, x_ref, w_ref, g_ref, tg_ref, dx_ref, dxb_ref, loss_ref, dg_ref):
        i = pl.program_id(0)
        av = jnp.maximum(a_ref[...].astype(F32), 0.0)
        x2 = x_ref[...] + _dot((av * av).astype(BF16), w_ref[...])
        r = lax.rsqrt(jnp.mean(x2 * x2, axis=-1, keepdims=True) + EPS)
        xn = x2 * r
        err = xn * g_ref[...] - tg_ref[...]
        dyf = err * (1.0 / d)
        dxn = dyf * g_ref[...]
        dx = r * (dxn - xn * jnp.mean(dxn * xn, axis=-1, keepdims=True))
        dx_ref[...] = dx
        dxb_ref[...] = dx.astype(BF16)

        @pl.when(i == 0)
        def _():
            loss_ref[...] = jnp.zeros_like(loss_ref)
            dg_ref[...] = jnp.zeros_like(dg_ref)

        loss_ref[...] += jnp.sum(err * err) * (0.5 / d)
        dg_ref[...] += jnp.sum(dyf * xn, axis=0, keepdims=True)

    tile = lambda c: pl.BlockSpec((tm, c), lambda i: (i, 0))
    sds = jax.ShapeDtypeStruct
    return pl.pallas_call(
        body, name=name, grid=(t // tm,),
        in_specs=[tile(f), tile(d), _resident((f, d)), _full((1, d)), tile(d)],
        out_specs=[tile(d), tile(d), _full((1, LANES)), _full((1, d))],
        out_shape=[sds((t, d), F32), sds((t, d), BF16), sds((1, LANES), F32), sds((1, d), F32)],
        compiler_params=_params("arbitrary"),
    )(a1, x1, w2, gf.reshape(1, d), target)


def mlp_bwd_act(dx2b, a1, w2, *, tm, name):
    t, f = a1.shape
    d = dx2b.shape[1]

    def body(dx_ref, a_ref, w_ref, o_ref):
        dact = _dot_nt(dx_ref[...], w_ref[...])
        o_ref[...] = (dact * 2.0 * jnp.maximum(a_ref[...].astype(F32), 0.0)).astype(BF16)

    tile = lambda c: pl.BlockSpec((tm, c), lambda i: (i, 0))
    return pl.pallas_call(
        body, name=name, grid=(t // tm,),
        in_specs=[tile(d), tile(f), _resident((f, d))], out_specs=tile(f),
        out_shape=jax.ShapeDtypeStruct((t, f), BF16),
        compiler_params=_params("parallel"),
    )(dx2b, a1, w2)


ADAM_TILE_ELEMS = 128 * 1024


def _row_tile(rows, cap):
    if rows <= cap:
        return rows
    best = None
    for c in range(16, cap + 1, 16):
        if rows % c == 0:
            best = c
    assert best is not None, rows
    return best


def _device_order_sum(parts_ref):
    total = parts_ref[0].astype(F32)
    for k in range(1, N_DEV):
        total = total + parts_ref[k].astype(F32)
    return total


def _adamw_math(w, m, v, gparts_ref):
    g = _device_order_sum(gparts_ref)
    mn = ADAM_B1 * m + (1.0 - ADAM_B1) * g
    vn = ADAM_B2 * v + (1.0 - ADAM_B2) * (g * g)
    m_hat = mn / (1.0 - ADAM_B1 ** ADAM_STEP)
    v_hat = vn / (1.0 - ADAM_B2 ** ADAM_STEP)
    return g, -ADAM_LR * (m_hat / (jnp.sqrt(v_hat) + ADAM_EPS) + ADAM_WD * w), mn, vn


def adamw(w, m, v, gparts, *, name):
    rows, cols = w.shape
    tr = _row_tile(rows, max(16, ADAM_TILE_ELEMS // cols))

    def body(w_ref, m_ref, v_ref, g_ref, go_ref, d_ref, mo_ref, vo_ref):
        go_ref[...], d_ref[...], mo_ref[...], vo_ref[...] = _adamw_math(w_ref[...], m_ref[...], v_ref[...], g_ref)

    tile = pl.BlockSpec((tr, cols), lambda i: (i, 0))
    out = jax.ShapeDtypeStruct((rows, cols), F32)
    return pl.pallas_call(
        body, name=name, grid=(rows // tr,),
        in_specs=[tile, tile, tile, pl.BlockSpec((N_DEV, tr, cols), lambda i: (0, i, 0))],
        out_specs=[tile, tile, tile, tile], out_shape=[out, out, out, out],
        compiler_params=_params("parallel"),
    )(w, m, v, gparts)


def all_gather_arrays(blocks, *, name):
    na = len(blocks)

    def body(*refs):
        x_refs, out_refs = refs[:na], refs[na:2 * na]
        send_sems, recv_sems, local_sems = refs[2 * na:]
        x, y, c, _ = _place()
        me, sibling = (x, y, c), (x, y, 1 - c)
        chips = [(1 - x, y), (x, 1 - y), (1 - x, 1 - y)]

        def copy(a, k, blk, to, own=False):
            px, py, pc = blk
            dst = out_refs[a].at[4 * px + 2 * py + pc]
            return pltpu.make_async_remote_copy(
                src_ref=x_refs[a] if own else dst, dst_ref=dst,
                send_sem=send_sems.at[7 * a + k], recv_sem=recv_sems.at[7 * a + k], device_id=to, device_id_type=_MESH)

        mine = [pltpu.make_async_copy(x_refs[a], out_refs[a].at[4 * x + 2 * y + c], local_sems.at[a]) for a in range(na)]
        for cp in mine:
            cp.start()
        sent = []
        for a in range(na):
            first = [copy(a, 0, me, sibling, own=True)]
            first += [copy(a, 1 + j, me, (*chip, c), own=True) for j, chip in enumerate(chips)]
            for cp in first:
                cp.start()
            sent += first
        for a in range(na):
            for j, chip in enumerate(chips):
                copy(a, 1 + j, (*chip, c), me).wait_recv()
                fwd = copy(a, 4 + j, (*chip, c), sibling)
                fwd.start()
                sent.append(fwd)
        for a in range(na):
            copy(a, 0, sibling, me).wait_recv()
            for j, chip in enumerate(chips):
                copy(a, 4 + j, (*chip, 1 - c), me).wait_recv()
        for cp in sent:
            cp.wait_send()
        for cp in mine:
            cp.wait()

    any_spec = pl.BlockSpec(memory_space=pl.ANY)
    return pl.pallas_call(
        body, name=name, in_specs=[any_spec] * na, out_specs=[any_spec] * na,
        out_shape=[jax.ShapeDtypeStruct((N_DEV,) + b.shape, b.dtype) for b in blocks],
        scratch_shapes=[pltpu.SemaphoreType.DMA((7 * na,)), pltpu.SemaphoreType.DMA((7 * na,)),
                        pltpu.SemaphoreType.DMA((na,))],
        compiler_params=pltpu.CompilerParams(has_side_effects=True),
    )(*blocks)


def all_to_all(comm, *, name):
    na = len(comm)
    kinds = [k for k, _ in comm]

    def body(*refs):
        local, remote = _comm_copies(kinds, refs[:na], refs[na:2 * na], *refs[2 * na:])
        for cp in local + remote:
            cp.start()
        for cp in remote:
            cp.wait_recv()
        for cp in remote:
            cp.wait_send()
        for cp in local:
            cp.wait()

    any_spec = pl.BlockSpec(memory_space=pl.ANY)
    return pl.pallas_call(
        body, name=name, in_specs=[any_spec] * na, out_specs=[any_spec] * na,
        out_shape=_comm_out_shapes(comm),
        scratch_shapes=[pltpu.SemaphoreType.DMA((na * N_REL,)), pltpu.SemaphoreType.DMA((na * N_REL,)),
                        pltpu.SemaphoreType.DMA((na,))],
        compiler_params=pltpu.CompilerParams(has_side_effects=True),
    )(*[a for _, a in comm])


def adamw_whole(ws, ms, vs, gparts, *, name, totals=()):
    n, nt = len(ws), len(totals)

    def body(*refs):
        w_refs, m_refs, v_refs, g_refs = refs[:n], refs[n:2 * n], refs[2 * n:3 * n], refs[3 * n:4 * n]
        t_refs, refs = refs[4 * n:4 * n + nt], refs[4 * n + nt:]
        outs, t_outs = refs[:4 * n], refs[4 * n:]
        for k in range(n):
            g, delta, mn, vn = _adamw_math(w_refs[k][...], m_refs[k][...], v_refs[k][...], g_refs[k])
            outs[k][...] = g
            outs[n + k][...] = delta
            outs[2 * n + k][...] = mn
            outs[3 * n + k][...] = vn
        for t_ref, o_ref in zip(t_refs, t_outs):
            o_ref[...] = _device_order_sum(t_ref)

    shapes = [jax.ShapeDtypeStruct(w.shape, F32) for w in ws]
    res = pl.pallas_call(
        body, name=name, out_shape=shapes * 4 + [jax.ShapeDtypeStruct(t.shape[1:], F32) for t in totals],
        compiler_params=pltpu.CompilerParams(vmem_limit_bytes=VMEM_LIMIT),
    )(*ws, *ms, *vs, *gparts, *totals)
    return [res[j * n:(j + 1) * n] for j in range(4)], res[4 * n:]


def _lane_row(v):
    return jnp.pad(v.astype(F32), (0, LANES - v.shape[0])).reshape(1, LANES)


def _block_diag_b(bb):
    eye = jnp.eye(SUBLANES, dtype=F32)
    bt = bb.reshape(S5_BLOCKS, 8, S5_STATE, S5_GROUP).transpose(0, 1, 3, 2)
    return (bt[:, :, :, None, :] * eye[None, :, None, :, None]).reshape(S5_WIDTH, S5_BS)


def _block_diag_c(cc):
    eye = jnp.eye(SUBLANES, dtype=F32)
    ct = cc.reshape(S5_BLOCKS, 8, S5_GROUP, S5_STATE).transpose(3, 0, 1, 2)
    return (ct[None, :, :, :, :] * eye[:, None, None, :, None]).reshape(S5_BS, S5_WIDTH)


def _diag_blocks_b(m):
    eye = jnp.eye(SUBLANES, dtype=F32)
    m5 = m.reshape(S5_BLOCKS, 8, S5_GROUP, 8, S5_STATE)
    return jnp.sum(m5 * eye[None, :, None, :, None], axis=3).transpose(0, 1, 3, 2).reshape(S5_GROUPS, S5_STATE, S5_GROUP)


def _diag_blocks_c(m):
    eye = jnp.eye(SUBLANES, dtype=F32)
    m5 = m.reshape(8, S5_STATE, S5_BLOCKS, 8, S5_GROUP)
    return jnp.sum(m5 * eye[:, None, None, :, None], axis=0).transpose(1, 2, 3, 0).reshape(S5_GROUPS, S5_GROUP, S5_STATE)


def train_step(x, target, p, m, v):
    t = x.shape[0]
    tm = min(256, t)
    tb = min(512, t)
    ts = min(512, t)
    bf = lambda n: p[n].astype(BF16)
    slots = lambda n, a: a.reshape((N_DEV,) + _shard_shape(n))
    updated = {}

    def update(names, got):
        for n, parts in zip(names, got):
            updated[n] = adamw(p[n], m[n], v[n], parts, name=f"adamw_{n}")

    st_in, st_conv = all_gather_arrays([bf("w_in"), p["conv_w"]], name="gather_first")
    win_p = jnp.concatenate([st_in[k][:, a:b] for seg in IN_PADDED_ORDER for k, a, b in _shard_pieces(*IN_SEGMENTS[seg])]
                            + [jnp.zeros((D_MODEL, LANES - SSD_HEADS), BF16)], axis=1)
    conv_w8 = jnp.pad(_join_shards("conv_w", st_conv), ((0, SUBLANES - SSD_CONV), (0, 0)))
    (h, xbc_raw, graw, u5, z, dtr), (st_branch, st_out, st_glu) = rms_matmul(
        x, p["norm_mix_g"], win_p, IN_SPLITS, tm=tb, name="in_proj", out_dtypes=(BF16, BF16, BF16, BF16, F32),
        comm=gathers([bf("w_branch"), bf("w_out"), bf("s5_glu_w")]))
    w_branch = st_branch.reshape(D_MODEL + S5_WIDTH, D_MODEL)
    wout, wglu = st_out.reshape(D_MODEL, D_MODEL), st_glu.reshape(S5_WIDTH, S5_WIDTH)
    bias_row, alog_row = _lane_row(p["dt_bias"]), _lane_row(p["a_log"])
    d_row = jnp.repeat(p["d_ssd"], SSD_HEADDIM).reshape(1, SSD_HEADS * SSD_HEADDIM)
    (ya, yssd, states), (w1,) = ssd_fwd(xbc_raw, conv_w8, p["conv_b"], dtr, z, bias_row, alog_row, d_row,
                                        p["ssd_norm_g"], name="ssd_fwd", comm=gathers([bf("w_mlp_in")]))

    n = S5_N
    a_re_c, a_im_c = p["s5_a_re"].reshape(n, 1), p["s5_a_im"].reshape(n, 1)
    dt_c = jnp.repeat(p["s5_log_dt"], S5_STATE).reshape(n, 1)
    b_re_m, b_im_m = p["s5_b_re"].reshape(n, S5_GROUP), p["s5_b_im"].reshape(n, S5_GROUP)
    ab, bb_re, bb_im = s5_discretize(a_re_c, a_im_c, dt_c, b_re_m, b_im_m, name="s5_disc")
    ab8 = jnp.pad(ab.T, ((0, SUBLANES - 2), (0, 0)))
    bbc = jnp.concatenate([_block_diag_b(bb_re.reshape(S5_GROUPS, S5_STATE, S5_GROUP)),
                           _block_diag_b(bb_im.reshape(S5_GROUPS, S5_STATE, S5_GROUP))], axis=1).astype(BF16)
    ccc = jnp.concatenate([_block_diag_c(p["s5_c_re"]), -_block_diag_c(p["s5_c_im"])], axis=0).astype(BF16)
    s5d_row = p["s5_d"].reshape(1, S5_WIDTH)
    bglu_row = p["s5_glu_b"].reshape(1, S5_WIDTH)
    u5 = s5_interleave(u5, ts)
    (s, ys5, yb), (st_w2,) = s5_fwd(u5, bbc, ccc, ab8, s5d_row, wglu, bglu_row, tm=ts, name="s5_fwd",
                                    comm=gathers([bf("w_mlp_out")]))
    yb = s5_deinterleave(yb, ts)
    w2 = st_w2.reshape(D_FF, D_MODEL)

    x1, pa, pb = merge_fwd(ya, yb, graw, x, w_branch, wout, tm=tb, name="merge_fwd")
    (h2, a1), _ = rms_matmul(x1, p["norm_mlp_g"], w1, (D_FF,), tm=tb, name="mlp_in", out_dtypes=(BF16,))
    dx2, dx2b, loss_row, dgf = mlp_out_loss(a1, x1, w2, p["norm_final_g"], target, tm=tb, name="mlp_out_loss")

    g = {}
    g["norm_final_g"] = dgf[0]
    da1 = mlp_bwd_act(dx2b, a1, w2, tm=tb, name="mlp_bwd_act")
    dw2 = slots("w_mlp_out", matmul_tn(a1, dx2b, out_dtype=BF16, a_relu2=True, name="dw_mlp_out"))
    dw1 = matmul_tn(h2, da1, out_dtype=BF16, col_shards=N_DEV, name="dw_mlp_in")
    (dx1, dx1b, dgm), _ = matmul_nt_rms_bwd([da1], w1, x1, p["norm_mlp_g"], dx2, tm=tb, name="mlp_in_bwd")
    g["norm_mlp_g"] = dgm[0]
    merged, dgraw, dpa, dpb, dya, dyb = merge_bwd(dx1b, graw, pa, pb, w_branch, wout, tm=tm, name="merge_bwd")
    dwo = slots("w_out", matmul_tn(merged, dx1b, out_dtype=BF16, name="dw_out"))
    dwb = slots("w_branch", jnp.concatenate([matmul_tn(ya, dpa, out_dtype=BF16, name="dw_branch_a"),
                                             matmul_tn(yb, dpb, out_dtype=BF16, name="dw_branch_b")], axis=0))

    (du5, dbbc, dccc, dab, dd5, dwglu, dbglu), got = s5_bwd(
        s5_interleave(dyb, ts), ys5, u5, s, bbc, ccc, ab8, s5d_row, wglu, bglu_row, tm=ts, name="s5_bwd",
        comm=exchanges([dw2, dw1]))
    du5 = s5_deinterleave(du5, ts)
    update(["w_mlp_out", "w_mlp_in"], got)
    g["s5_glu_b"], g["s5_d"] = dbglu[0], dd5[0]
    g["s5_c_re"] = _diag_blocks_c(dccc[:S5_BS])
    g["s5_c_im"] = -_diag_blocks_c(dccc[S5_BS:])
    dbb_re = _diag_blocks_b(dbbc[:, :S5_BS]).reshape(n, S5_GROUP)
    dbb_im = _diag_blocks_b(dbbc[:, S5_BS:]).reshape(n, S5_GROUP)
    da, dlogdt, gb_re, gb_im = s5_discretize_bwd(a_re_c, a_im_c, dt_c, b_re_m, b_im_m, dab[:2].T, dbb_re, dbb_im,
                                                  name="s5_disc_bwd")
    g["s5_a_re"] = da[:, 0].reshape(S5_GROUPS, S5_STATE)
    g["s5_a_im"] = da[:, 1].reshape(S5_GROUPS, S5_STATE)
    g["s5_log_dt"] = dlogdt[:, 0]
    g["s5_b_re"] = gb_re.reshape(S5_GROUPS, S5_STATE, S5_GROUP)
    g["s5_b_im"] = gb_im.reshape(S5_GROUPS, S5_STATE, S5_GROUP)

    def update_small(names, got, name, totals=()):
        (gs, ds, nm, nv), sums = adamw_whole([p[n] for n in names], [m[n] for n in names], [v[n] for n in names], got,
                                             name=name, totals=totals)
        for k, n in enumerate(names):
            updated[n] = (gs[k], ds[k], nm[k], nv[k])
        return sums

    wide = ["s5_b_re", "s5_b_im"]
    late = ["ssd_norm_g", "d_ssd", "a_log", "dt_bias", "conv_b"]
    early = [n for n in SMALL_ORDER if n not in wide + late and n != "norm_mix_g"]
    (dzdt, dxbc_raw, dng, dd_row, dalog_row, dbias_row, dconv_b, dconv_w8), got = ssd_bwd(
        dya, yssd, z, xbc_raw, conv_w8, p["conv_b"], dtr, states, bias_row, alog_row, d_row, p["ssd_norm_g"],
        name="ssd_bwd",
        comm=exchanges([dwo, dwb, slots("s5_glu_w", dwglu.astype(BF16))])
        + gathers([g[n] for n in wide + early] + [loss_row]))
    update(["w_out", "w_branch", "s5_glu_w"], got[:3])
    update_small(wide, got[3:3 + len(wide)], "adamw_s5_b")
    loss_sum, = update_small(early, got[3 + len(wide):-1], "adamw_small_early", totals=[got[-1]])
    g["ssd_norm_g"] = dng[0]
    g["d_ssd"], g["a_log"], g["dt_bias"] = dd_row[0, :SSD_HEADS], dalog_row[0, :SSD_HEADS], dbias_row[0, :SSD_HEADS]
    g["conv_b"] = dconv_b[0]
    dconv = dconv_w8[:SSD_CONV].reshape(SSD_CONV, N_DEV, CONV_DIM // N_DEV).transpose(1, 0, 2)
    dproj = [dxbc_raw, dgraw, du5, dzdt]
    dxbc_w, dgate_w, du5_w, dzdt_w = [matmul_tn(h, piece, out_dtype=BF16, name=f"dw_in_{k}")
                                      for k, piece in enumerate(dproj)]
    by_segment = {"z": dzdt_w[:, :D_MODEL], "xbc": dxbc_w, "dt": dzdt_w[:, D_MODEL:], "u5": du5_w, "gates": dgate_w}
    width = D_IN_PROJ // N_DEV
    dw_in = jnp.stack([jnp.concatenate(
        [by_segment[seg][:, max(k * width, lo) - lo:min((k + 1) * width, hi) - lo]
         for seg, (lo, hi) in IN_SEGMENTS.items() if max(k * width, lo) < min((k + 1) * width, hi)], axis=1)
        for k in range(N_DEV)])

    (grad_x, _, dgx), got = matmul_nt_rms_bwd(
        dproj, win_p, x, p["norm_mix_g"], dx1, tm=tm, name="in_proj_bwd",
        comm=exchanges([dw_in, dconv]) + gathers([g[n] for n in late]))
    update(["w_in", "conv_w"], got[:2])
    update_small(late, got[2:], "adamw_small_late")
    update_small(["norm_mix_g"], all_to_all(gathers([dgx[0]]), name="gather_last"), "adamw_small_last")
    return loss_sum[0, 0], grad_x, updated


WEIGHT_ORDER = ["norm_mix_g", "w_in", "conv_w", "conv_b", "dt_bias", "a_log", "d_ssd", "ssd_norm_g", "s5_a_re",
                "s5_a_im", "s5_log_dt", "s5_b_re", "s5_b_im", "s5_c_re", "s5_c_im", "s5_d", "s5_glu_w", "s5_glu_b",
                "w_branch", "w_out", "norm_mlp_g", "w_mlp_in", "w_mlp_out", "norm_final_g"]
SHARDED = {"w_in": ((D_MODEL, D_IN_PROJ), 1), "conv_w": ((SSD_CONV, CONV_DIM), 1), "s5_glu_w": ((S5_WIDTH, S5_WIDTH), 0),
           "w_branch": ((D_MODEL + S5_WIDTH, D_MODEL), 0), "w_out": ((D_MODEL, D_MODEL), 0),
           "w_mlp_in": ((D_MODEL, D_FF), 1), "w_mlp_out": ((D_FF, D_MODEL), 0)}
SMALL_ORDER = [n for n in WEIGHT_ORDER if n not in SHARDED]


def _shard_shape(name):
    (r, c), ax = SHARDED[name]
    return (r, c // N_DEV) if ax == 1 else (r // N_DEV, c)


def _join_shards(name, stacked):
    (r, c), ax = SHARDED[name]
    if ax == 1:
        return stacked.transpose(1, 0, 2).reshape(r, c)
    return stacked.reshape(r, c)


def _shard_pieces(lo, hi):
    width = D_IN_PROJ // N_DEV
    out = []
    while lo < hi:
        k = lo // width
        end = min(hi, (k + 1) * width)
        out.append((k, lo - k * width, end - k * width))
        lo = end
    return out


IN_SEGMENTS = {"z": (0, 1024), "xbc": (1024, 3072), "dt": (3072, 3088), "u5": (3088, 3600), "gates": (3600, 5648)}
IN_PADDED_ORDER = ("xbc", "gates", "u5", "z", "dt")


def kernel(x, norm_mix_g, w_in, conv_w, conv_b, dt_bias, a_log, d_ssd, ssd_norm_g, s5_a_re, s5_a_im, s5_log_dt, s5_b_re, s5_b_im, s5_c_re, s5_c_im, s5_d, s5_glu_w, s5_glu_b, w_branch, w_out, norm_mlp_g, w_mlp_in, w_mlp_out, norm_final_g, loss_target, m_norm_mix_g, m_w_in, m_conv_w, m_conv_b, m_dt_bias, m_a_log, m_d_ssd, m_ssd_norm_g, m_s5_a_re, m_s5_a_im, m_s5_log_dt, m_s5_b_re, m_s5_b_im, m_s5_c_re, m_s5_c_im, m_s5_d, m_s5_glu_w, m_s5_glu_b, m_w_branch, m_w_out, m_norm_mlp_g, m_w_mlp_in, m_w_mlp_out, m_norm_final_g, v_norm_mix_g, v_w_in, v_conv_w, v_conv_b, v_dt_bias, v_a_log, v_d_ssd, v_ssd_norm_g, v_s5_a_re, v_s5_a_im, v_s5_log_dt, v_s5_b_re, v_s5_b_im, v_s5_c_re, v_s5_c_im, v_s5_d, v_s5_glu_w, v_s5_glu_b, v_w_branch, v_w_out, v_norm_mlp_g, v_w_mlp_in, v_w_mlp_out, v_norm_final_g):
    w = dict(norm_mix_g=norm_mix_g, w_in=w_in, conv_w=conv_w, conv_b=conv_b, dt_bias=dt_bias, a_log=a_log, d_ssd=d_ssd,
             ssd_norm_g=ssd_norm_g, s5_a_re=s5_a_re, s5_a_im=s5_a_im, s5_log_dt=s5_log_dt, s5_b_re=s5_b_re,
             s5_b_im=s5_b_im, s5_c_re=s5_c_re, s5_c_im=s5_c_im, s5_d=s5_d, s5_glu_w=s5_glu_w, s5_glu_b=s5_glu_b,
             w_branch=w_branch, w_out=w_out, norm_mlp_g=norm_mlp_g, w_mlp_in=w_mlp_in, w_mlp_out=w_mlp_out,
             norm_final_g=norm_final_g)
    m = dict(norm_mix_g=m_norm_mix_g, w_in=m_w_in, conv_w=m_conv_w, conv_b=m_conv_b, dt_bias=m_dt_bias, a_log=m_a_log,
             d_ssd=m_d_ssd, ssd_norm_g=m_ssd_norm_g, s5_a_re=m_s5_a_re, s5_a_im=m_s5_a_im, s5_log_dt=m_s5_log_dt,
             s5_b_re=m_s5_b_re, s5_b_im=m_s5_b_im, s5_c_re=m_s5_c_re, s5_c_im=m_s5_c_im, s5_d=m_s5_d,
             s5_glu_w=m_s5_glu_w, s5_glu_b=m_s5_glu_b, w_branch=m_w_branch, w_out=m_w_out, norm_mlp_g=m_norm_mlp_g,
             w_mlp_in=m_w_mlp_in, w_mlp_out=m_w_mlp_out, norm_final_g=m_norm_final_g)
    v = dict(norm_mix_g=v_norm_mix_g, w_in=v_w_in, conv_w=v_conv_w, conv_b=v_conv_b, dt_bias=v_dt_bias, a_log=v_a_log,
             d_ssd=v_d_ssd, ssd_norm_g=v_ssd_norm_g, s5_a_re=v_s5_a_re, s5_a_im=v_s5_a_im, s5_log_dt=v_s5_log_dt,
             s5_b_re=v_s5_b_re, s5_b_im=v_s5_b_im, s5_c_re=v_s5_c_re, s5_c_im=v_s5_c_im, s5_d=v_s5_d,
             s5_glu_w=v_s5_glu_w, s5_glu_b=v_s5_glu_b, w_branch=v_w_branch, w_out=v_w_out, norm_mlp_g=v_norm_mlp_g,
             w_mlp_in=v_w_mlp_in, w_mlp_out=v_w_mlp_out, norm_final_g=v_norm_final_g)

    loss, grad_x, upd = train_step(x[0], loss_target[0], w, m, v)
    return (loss, grad_x.reshape(x.shape), *[upd[n][j] for j in range(4) for n in WEIGHT_ORDER])
```

```python
import math

import jax
import jax.numpy as jnp
from jax import lax
from jax.experimental import pallas as pl
from jax.experimental.pallas import tpu as pltpu

F32 = jnp.float32
BF16 = jnp.bfloat16

N_DEV = 8
D_MODEL = 1024
SSD_HEADS = 16
SSD_HEADDIM = 64
SSD_GROUPS = 4
SSD_HPG = 4
SSD_STATE = 128
SSD_CHUNK = 128
SSD_CONV = 4
CONV_DIM = 2048
S5_WIDTH = 512
S5_GROUP = 16
S5_GROUPS = 32
S5_STATE = 64
S5_N = S5_GROUPS * S5_STATE
D_FF = 4096
D_IN_PROJ = 5648
IN_SPLITS = (2048, 2048, 512, 1024, 128)
EPS = 1e-6
LANES = 128
SUBLANES = 8
VMEM_LIMIT = 56 * 1024 * 1024

ADAM_LR = 0.001
ADAM_B1 = 0.9
ADAM_B2 = 0.999
ADAM_EPS = 1e-08
ADAM_WD = 0.01
ADAM_STEP = 10

GELU_C = math.sqrt(2.0 / math.pi)
GELU_K = 0.044715


def _params(*sem):
    return pltpu.CompilerParams(dimension_semantics=sem, vmem_limit_bytes=VMEM_LIMIT)


def _full(shape):
    nd = len(shape)
    return pl.BlockSpec(shape, lambda *_: (0,) * nd)


def _resident(shape):
    nd = len(shape)
    return pl.BlockSpec(shape, lambda *_: (0,) * nd, pipeline_mode=pl.Buffered(1))


_MESH = pl.DeviceIdType.MESH
_RELATIONS = [(dx, dy, dc) for dx in (0, 1) for dy in (0, 1) for dc in (0, 1)][1:]
N_REL = len(_RELATIONS)


def _place():
    x, y, c = lax.axis_index("x"), lax.axis_index("y"), lax.axis_index("c")
    return x, y, c, 4 * x + 2 * y + c


def gathers(arrays):
    return [("gather", a) for a in arrays]


def exchanges(arrays):
    return [("exchange", a) for a in arrays]


def _comm_out_shapes(comm):
    return [jax.ShapeDtypeStruct(((N_DEV,) if kind == "gather" else ()) + a.shape, a.dtype) for kind, a in comm]


def _comm_copies(kinds, src_refs, dst_refs, send_sems, recv_sems, local_sems):
    x, y, c, idx = _place()
    local, remote = [], []
    for a, (kind, src, dst) in enumerate(zip(kinds, src_refs, dst_refs)):
        local.append(pltpu.make_async_copy(src if kind == "gather" else src.at[idx], dst.at[idx], local_sems.at[a]))
        for k, (dx, dy, dc) in enumerate(_RELATIONS):
            px, py, pc = x ^ dx, y ^ dy, c ^ dc
            remote.append(pltpu.make_async_remote_copy(
                src_ref=src if kind == "gather" else src.at[4 * px + 2 * py + pc], dst_ref=dst.at[idx],
                send_sem=send_sems.at[N_REL * a + k], recv_sem=recv_sems.at[N_REL * a + k],
                device_id=(px, py, pc), device_id_type=_MESH))
    return local, remote


def _tiled_call(body, *, name, grid, in_specs, out_specs, out_shape, operands, scratch_shapes=(), comm=None):
    params = pltpu.CompilerParams(dimension_semantics=("arbitrary",), vmem_limit_bytes=VMEM_LIMIT,
                                  has_side_effects=bool(comm))
    if not comm:
        outs = pl.pallas_call(body, name=name, grid=grid, in_specs=in_specs, out_specs=out_specs, out_shape=out_shape,
                              scratch_shapes=list(scratch_shapes), compiler_params=params)(*operands)
        return outs, []
    kind = [k for k, _ in comm]
    arrays = [a for _, a in comm]
    na, n_in, n_out, n_scr = len(arrays), len(in_specs), len(out_specs), len(scratch_shapes)
    last = grid[0] - 1

    def hosted(*refs):
        ins, refs = refs[:n_in], refs[n_in:]
        cin, refs = refs[:na], refs[na:]
        outs, refs = refs[:n_out], refs[n_out:]
        cout, refs = refs[:na], refs[na:]
        scr, sems = refs[:n_scr], refs[n_scr:]
        i = pl.program_id(0)

        @pl.when(i == 0)
        def _():
            local, remote = _comm_copies(kind, cin, cout, *sems)
            for cp in local + remote:
                cp.start()

        body(*ins, *outs, *scr)

        @pl.when(i == last)
        def _():
            local, remote = _comm_copies(kind, cin, cout, *sems)
            for cp in remote:
                cp.wait_recv()
            for cp in remote:
                cp.wait_send()
            for cp in local:
                cp.wait()

    any_spec = pl.BlockSpec(memory_space=pl.ANY)
    res = pl.pallas_call(
        hosted, name=name, grid=grid,
        in_specs=list(in_specs) + [any_spec] * na, out_specs=list(out_specs) + [any_spec] * na,
        out_shape=list(out_shape) + _comm_out_shapes(comm),
        scratch_shapes=list(scratch_shapes) + [pltpu.SemaphoreType.DMA((N_REL * na,)), pltpu.SemaphoreType.DMA((N_REL * na,)),
                                               pltpu.SemaphoreType.DMA((na,))],
        compiler_params=params)(*operands, *arrays)
    return res[:n_out], res[n_out:]


def _dot(a, b):
    return jnp.dot(a, b, preferred_element_type=F32)


def _dot_nt(a, b):
    return lax.dot_general(a, b, (((1,), (1,)), ((), ())), preferred_element_type=F32)


def _dot_tn(a, b):
    return lax.dot_general(a, b, (((0,), (0,)), ((), ())), preferred_element_type=F32)


def _split2(x):
    hi = x.astype(BF16)
    return hi, (x - hi.astype(F32)).astype(BF16)


def _select_dot(dot, a, b, exact):
    if exact == "a":
        ea, (hi, lo) = a.astype(BF16), _split2(b)
        return dot(ea, hi) + dot(ea, lo)
    eb, (hi, lo) = b.astype(BF16), _split2(a)
    return dot(hi, eb) + dot(lo, eb)


def _dot_sel_a(a, b):
    return _select_dot(_dot, a, b, "a")


def _dot_sel_b(a, b):
    return _select_dot(_dot, a, b, "b")


def _dot_nt_sel_a(a, b):
    return _select_dot(_dot_nt, a, b, "a")


def _dot_tn_sel(a, b, exact):
    return _select_dot(_dot_tn, a, b, exact)


def _sigmoid(x):
    return 1.0 / (1.0 + jnp.exp(-x))


def _softplus(x):
    return jnp.maximum(x, 0.0) + jnp.log(1.0 + jnp.exp(-jnp.abs(x)))


def _gelu(x):
    return 0.5 * x * (1.0 + jnp.tanh(GELU_C * (x + GELU_K * x * x * x)))


def _gelu_grad(x):
    th = jnp.tanh(GELU_C * (x + GELU_K * x * x * x))
    return 0.5 * (1.0 + th) + 0.5 * x * (1.0 - th * th) * GELU_C * (1.0 + 3.0 * GELU_K * x * x)


def rms_matmul(x, g, w, splits, *, tm, name, comm=None, out_dtypes=None):
    t, d = x.shape
    stacked = w.ndim == 3
    n = w.shape[0] * w.shape[2] if stacked else w.shape[1]
    assert sum(splits) == n and (len(splits) == 1 or not stacked)

    def body(x_ref, g_ref, w_ref, h_ref, *o_refs):
        xv = x_ref[...]
        r = lax.rsqrt(jnp.mean(xv * xv, axis=-1, keepdims=True) + EPS)
        h = (xv * r * g_ref[...]).astype(BF16)
        h_ref[...] = h
        if stacked:
            wk = w.shape[2]
            for k in range(w.shape[0]):
                o_refs[0][:, k * wk:(k + 1) * wk] = _dot(h, w_ref[k]).astype(o_refs[0].dtype)
            return
        off = 0
        for o_ref, width in zip(o_refs, splits):
            o_ref[...] = _dot(h, w_ref[:, off:off + width]).astype(o_ref.dtype)
            off += width

    tile = lambda c: pl.BlockSpec((tm, c), lambda i: (i, 0))
    dtypes = out_dtypes or (F32,) * len(splits)
    return _tiled_call(
        body, name=name, grid=(t // tm,),
        in_specs=[tile(d), _full((1, d)), _resident(w.shape)],
        out_specs=[tile(d)] + [tile(c) for c in splits],
        out_shape=[jax.ShapeDtypeStruct((t, d), BF16)] + [jax.ShapeDtypeStruct((t, c), dt) for c, dt in zip(splits, dtypes)],
        operands=(x, g.reshape(1, d), w), comm=comm)


MATMUL_TN_ACC_BYTES = 16 * 1024 * 1024


def _pick(n, cap):
    best = LANES
    for c in range(LANES, cap + 1, LANES):
        if n % c == 0:
            best = c
    return best


def matmul_tn(a, b, *, name, out_dtype=F32, col_shards=None, tk=1024, a_relu2=False):
    t, m = a.shape
    n = b.shape[1]
    tm = _pick(m, 1024)
    whole_n = tm * n * 4 <= MATMUL_TN_ACC_BYTES
    tn = n if whole_n else _pick(n, 1280)
    assert whole_n or not col_shards
    tk = min(tk if tn <= 2048 else tk // 2, t)
    nk = t // tk

    def body(a_ref, b_ref, o_ref, acc):
        k = pl.program_id(2)

        @pl.when(k == 0)
        def _():
            acc[...] = jnp.zeros_like(acc)

        av = a_ref[...]
        if a_relu2:
            pos = jnp.maximum(av.astype(F32), 0.0)
            av = (pos * pos).astype(BF16)
        acc[...] += _dot_tn(av, b_ref[...])

        @pl.when(k == nk - 1)
        def _():
            if col_shards:
                w = n // col_shards
                for s in range(col_shards):
                    o_ref[s] = acc[:, s * w:(s + 1) * w].astype(out_dtype)
            else:
                o_ref[...] = acc[...].astype(out_dtype)

    if col_shards:
        out_spec = pl.BlockSpec((col_shards, tm, n // col_shards), lambda i, j, k: (0, i, 0))
        out_shape = jax.ShapeDtypeStruct((col_shards, m, n // col_shards), out_dtype)
    else:
        out_spec = pl.BlockSpec((tm, tn), lambda i, j, k: (i, j))
        out_shape = jax.ShapeDtypeStruct((m, n), out_dtype)
    return pl.pallas_call(
        body, name=name, grid=(m // tm, n // tn, nk),
        in_specs=[pl.BlockSpec((tk, tm), lambda i, j, k: (k, i)), pl.BlockSpec((tk, tn), lambda i, j, k: (k, j))],
        out_specs=out_spec, out_shape=out_shape,
        scratch_shapes=[pltpu.VMEM((tm, tn), F32)],
        compiler_params=_params("parallel", "parallel", "arbitrary"),
    )(a, b)


def matmul_nt_rms_bwd(dys, w, x, g, dres, *, tm, name, comm=None, bf16_copy=True):
    t = x.shape[0]
    stacked = w.ndim == 3
    d = w.shape[1] if stacked else w.shape[0]
    widths = [dy.shape[1] for dy in dys]
    n_dy = len(dys)

    def body(*refs):
        dy_refs = refs[:n_dy]
        w_ref, x_ref, g_ref, dres_ref, dx_ref = refs[n_dy:n_dy + 5]
        dxb_ref = refs[n_dy + 5] if bf16_copy else None
        dg_ref = refs[-1]
        i = pl.program_id(0)
        if stacked:
            wk = w.shape[2]
            dh = _dot_nt(dy_refs[0][:, 0:wk], w_ref[0])
            for k in range(1, w.shape[0]):
                dh = dh + _dot_nt(dy_refs[0][:, k * wk:(k + 1) * wk], w_ref[k])
        else:
            dh, off = None, 0
            for dy_ref, width in zip(dy_refs, widths):
                part = _dot_nt(dy_ref[...], w_ref[:, off:off + width])
                dh = part if dh is None else dh + part
                off += width
        xv = x_ref[...]
        r = lax.rsqrt(jnp.mean(xv * xv, axis=-1, keepdims=True) + EPS)
        xn = xv * r
        dxn = dh * g_ref[...]
        dx = dres_ref[...] + r * (dxn - xn * jnp.mean(dxn * xn, axis=-1, keepdims=True))
        dx_ref[...] = dx
        if bf16_copy:
            dxb_ref[...] = dx.astype(BF16)

        @pl.when(i == 0)
        def _():
            dg_ref[...] = jnp.zeros_like(dg_ref)

        dg_ref[...] += jnp.sum(dh * xn, axis=0, keepdims=True)

    tile = lambda c: pl.BlockSpec((tm, c), lambda i: (i, 0))
    copies = [BF16] if bf16_copy else []
    return _tiled_call(
        body, name=name, grid=(t // tm,),
        in_specs=[tile(c) for c in widths] + [_resident(w.shape), tile(d), _full((1, d)), tile(d)],
        out_specs=[tile(d)] + [tile(d) for _ in copies] + [_full((1, d))],
        out_shape=[jax.ShapeDtypeStruct((t, d), F32)] + [jax.ShapeDtypeStruct((t, d), dt) for dt in copies]
        + [jax.ShapeDtypeStruct((1, d), F32)],
        operands=(*dys, w, x, g.reshape(1, d), dres), comm=comm)


def _conv_windows(ext_ref, tm):
    ext = ext_ref[...]
    return [pltpu.roll(ext, 3 - k, 0)[SUBLANES:SUBLANES + tm, :] for k in range(3)] + [ext[SUBLANES:SUBLANES + tm, :]]


def _conv_taps(windows, w_ref):
    acc = windows[3] * w_ref[3:4, :]
    for k in range(3):
        acc = acc + windows[k] * w_ref[k:k + 1, :]
    return acc


def _ssd_chunk_terms(dtr, bias_row, alog_row):
    q = SSD_CHUNK
    row = lax.broadcasted_iota(jnp.int32, (q, q), 0)
    col = lax.broadcasted_iota(jnp.int32, (q, q), 1)
    ltri = (col <= row).astype(F32)
    dt = _softplus(dtr + bias_row)
    a_row = -jnp.exp(alog_row)
    la = dt * a_row
    cum = _dot_sel_a(ltri, la)
    cum_t = _dot_tn_sel(la, (row <= col).astype(F32), "b")
    return dt, a_row, cum, cum_t, ltri, row, col


def _head_maps():
    di = SSD_HEADS * SSD_HEADDIM
    expand = (lax.broadcasted_iota(jnp.int32, (LANES, di), 1) // SSD_HEADDIM
              == lax.broadcasted_iota(jnp.int32, (LANES, di), 0)).astype(F32)
    collapse = (lax.broadcasted_iota(jnp.int32, (di, LANES), 0) // SSD_HEADDIM
                == lax.broadcasted_iota(jnp.int32, (di, LANES), 1)).astype(F32)
    return expand, collapse


def _ssd_decay(cum, cum_t, h, causal):
    seg = cum[:, h:h + 1] - cum_t[h:h + 1, :]
    return jnp.where(causal, jnp.exp(jnp.where(causal, seg, 0.0)), 0.0)


def _pair_block_diag(x2):
    low = lax.broadcasted_iota(jnp.int32, x2.shape, 1) < SSD_HEADDIM
    zero = jnp.zeros_like(x2)
    return jnp.concatenate([jnp.where(low, x2, zero), jnp.where(low, zero, x2)], axis=0)


def ssd_fwd(xbc_raw, conv_w8, conv_b, dtr, z, bias_row, alog_row, d_row, norm_g, *, name, comm=None):
    t = xbc_raw.shape[0]
    q, p, n = SSD_CHUNK, SSD_HEADDIM, SSD_STATE
    nc = t // q
    di = SSD_HEADS * p
    gw = di // SSD_GROUPS

    def body(raw_ref, cw_ref, cb_ref, dtr_ref, z_ref, bias_ref, alog_ref, d_ref, g_ref,
             ya_ref, yssd_ref, st_ref, state, ext, xbc_s):
        ci = pl.program_id(0)

        @pl.when(ci == 0)
        def _():
            state[...] = jnp.zeros_like(state)
            ext[0:SUBLANES, :] = jnp.zeros((SUBLANES, CONV_DIM), F32)

        ext[SUBLANES:SUBLANES + q, :] = raw_ref[...].astype(F32)
        xc = _conv_taps(_conv_windows(ext, q), cw_ref) + cb_ref[...]
        ext[0:SUBLANES, :] = ext[q:q + SUBLANES, :]
        xbc_s[...] = xc * _sigmoid(xc)
        st_ref[0] = state[...]
        dt, _, cum, cum_t, _, row, col = _ssd_chunk_terms(dtr_ref[...], bias_ref[...], alog_ref[...])
        causal = row >= col
        expand, _ = _head_maps()
        dt_full = _dot_sel_b(dt, expand)
        cum_full = _dot_sel_b(cum, expand)
        last_full = cum_full[q - 1:q, :]
        xs = xbc_s[:, 0:di]
        xdt = xs * dt_full
        xd = (xdt * jnp.exp(last_full - cum_full)).astype(BF16)
        xdt_b = xdt.astype(BF16)
        e_full = jnp.exp(cum_full)
        eend_full = jnp.exp(last_full)
        for g in range(SSD_GROUPS):
            slab = slice(g * gw, (g + 1) * gw)
            bg = xbc_s[:, di + g * n:di + (g + 1) * n].astype(BF16)
            cg = xbc_s[:, di + (SSD_GROUPS + g) * n:di + (SSD_GROUPS + g + 1) * n].astype(BF16)
            cb = _dot_nt(cg, bg)
            st_g = state[:, slab]
            pairs = []
            for pr in range(SSD_HPG // 2):
                h0 = g * SSD_HPG + 2 * pr
                m0 = (cb * _ssd_decay(cum, cum_t, h0, causal)).astype(BF16)
                m1 = (cb * _ssd_decay(cum, cum_t, h0 + 1, causal)).astype(BF16)
                pairs.append(_dot(jnp.concatenate([m0, m1], axis=1), _pair_block_diag(xdt_b[:, h0 * p:(h0 + 2) * p])))
            y = jnp.concatenate(pairs, axis=1) + e_full[:, slab] * _dot(cg, st_g.astype(BF16))
            yssd_ref[:, slab] = y + d_ref[:, slab] * xs[:, slab]
            state[:, slab] = eend_full[:, slab] * st_g + _dot_tn(bg, xd[:, slab])
        zv = z_ref[...].astype(F32)
        ya1 = yssd_ref[...] * (zv * _sigmoid(zv))
        for g in range(SSD_GROUPS):
            gsl = slice(g * gw, (g + 1) * gw)
            sl = ya1[:, gsl]
            rg = lax.rsqrt(jnp.mean(sl * sl, axis=-1, keepdims=True) + EPS)
            ya_ref[:, gsl] = (sl * rg * g_ref[:, gsl]).astype(BF16)

    blk = lambda w, j: pl.BlockSpec((q, w), lambda c: (c, j))
    return _tiled_call(
        body, name=name, grid=(nc,),
        in_specs=[blk(CONV_DIM, 0), _full((SUBLANES, CONV_DIM)), _full((1, CONV_DIM)), blk(LANES, 0), blk(di, 0),
                  _full((1, LANES)), _full((1, LANES)), _full((1, di)), _full((1, di))],
        out_specs=[blk(di, 0), blk(di, 0), pl.BlockSpec((1, n, di), lambda c: (c, 0, 0))],
        out_shape=[jax.ShapeDtypeStruct((t, di), BF16), jax.ShapeDtypeStruct((t, di), F32),
                   jax.ShapeDtypeStruct((nc, n, di), F32)],
        scratch_shapes=[pltpu.VMEM((n, di), F32), pltpu.VMEM((SUBLANES + q, CONV_DIM), F32), pltpu.VMEM((q, CONV_DIM), F32)],
        operands=(xbc_raw, conv_w8, conv_b.reshape(1, CONV_DIM), dtr, z, bias_row, alog_row, d_row,
                  norm_g.reshape(1, di)), comm=comm)


def ssd_bwd(dya, yssd, z, xbc_raw, conv_w8, conv_b, dtr, states, bias_row, alog_row, d_row, norm_g, *, name, comm=None):
    t = xbc_raw.shape[0]
    q, p, n = SSD_CHUNK, SSD_HEADDIM, SSD_STATE
    nc = t // q
    di = SSD_HEADS * p
    gw = di // SSD_GROUPS
    per = q // (2 * SUBLANES)

    def body(dya_ref, yssd_ref, z_ref, raw_ref, prev_ref, cw_ref, cb_ref, dtr_ref, st_ref, bias_ref, alog_ref, d_ref,
             g_ref, dzdt_ref, draw_ref, dng_ref, dd_ref, dalog_ref, dbias_ref, dcb_ref, dcw_ref,
             dstate, dyssd, xbc_s, dxbc_ref, ext, aft):
        ci = pl.program_id(0)

        @pl.when(ci == 0)
        def _():
            dstate[...] = jnp.zeros_like(dstate)
            aft[q:q + SUBLANES, :] = jnp.zeros((SUBLANES, CONV_DIM), F32)
            for r in (dng_ref, dd_ref, dalog_ref, dbias_ref, dcb_ref, dcw_ref):
                r[...] = jnp.zeros_like(r)

        ext[0:SUBLANES, :] = jnp.where(ci == nc - 1, 0.0, prev_ref[SUBLANES:2 * SUBLANES, :].astype(F32))
        ext[SUBLANES:SUBLANES + q, :] = raw_ref[...].astype(F32)
        windows = _conv_windows(ext, q)
        xc = _conv_taps(windows, cw_ref) + cb_ref[...]
        sig_c = _sigmoid(xc)
        xbc_s[...] = xc * sig_c

        zv = z_ref[...].astype(F32)
        sz = _sigmoid(zv)
        silu_z = zv * sz
        yv = yssd_ref[...]
        ya1 = yv * silu_z
        dyav = dya_ref[...]
        for g in range(SSD_GROUPS):
            gsl = slice(g * gw, (g + 1) * gw)
            sl = ya1[:, gsl]
            rg = lax.rsqrt(jnp.mean(sl * sl, axis=-1, keepdims=True) + EPS)
            ya2 = sl * rg
            dy_g = dyav[:, gsl]
            dng_ref[:, gsl] += jnp.sum(dy_g * ya2, axis=0, keepdims=True)
            dya2 = dy_g * g_ref[:, gsl]
            dya1 = rg * (dya2 - ya2 * jnp.mean(dya2 * ya2, axis=-1, keepdims=True))
            dyssd[:, gsl] = dya1 * silu_z[:, gsl]
            dzdt_ref[:, gsl] = (dya1 * yv[:, gsl] * (sz[:, gsl] * (1.0 + zv[:, gsl] * (1.0 - sz[:, gsl])))).astype(BF16)

        dtr_v = dtr_ref[...]
        dt, a_row, cum, cum_t, ltri, row, col = _ssd_chunk_terms(dtr_v, bias_ref[...], alog_ref[...])
        causal = row >= col
        expand, collapse = _head_maps()
        lane = lax.broadcasted_iota(jnp.int32, (1, LANES), 1)
        sub = lax.broadcasted_iota(jnp.int32, (LANES, 1), 0)
        is_last = lax.broadcasted_iota(jnp.int32, (q, 1), 0) == q - 1
        low = lax.broadcasted_iota(jnp.int32, (q, 2 * p), 1) < p
        dt_full = _dot_sel_b(dt, expand)
        cum_full = _dot_sel_b(cum, expand)
        last_full = cum_full[q - 1:q, :]
        e_full = jnp.exp(cum_full)
        eend_full = jnp.exp(last_full)
        dend_full = jnp.exp(last_full - cum_full)
        xs = xbc_s[:, 0:di]
        xdt = xs * dt_full
        xdt_b = xdt.astype(BF16)
        xd_b = (xdt * dend_full).astype(BF16)
        dy_all = dyssd[...]
        dy_b = dy_all.astype(BF16)
        dg_b = (dy_all * e_full).astype(BF16)
        dcum_mat = jnp.zeros((q, LANES), F32)
        rmat = jnp.zeros((LANES, q), F32)
        yoff_prod, bds_list, dx_list, dlast_parts = [], [], [], []
        for g in range(SSD_GROUPS):
            slab = slice(g * gw, (g + 1) * gw)
            bg = xbc_s[:, di + g * n:di + (g + 1) * n].astype(BF16)
            cg = xbc_s[:, di + (SSD_GROUPS + g) * n:di + (SSD_GROUPS + g + 1) * n].astype(BF16)
            cb = _dot_nt(cg, bg)
            st_g = st_ref[0, :, slab]
            st_b = st_g.astype(BF16)
            dsn = dstate[:, slab]
            dsn_b = dsn.astype(BF16)
            yoff_prod.append(dy_all[:, slab] * e_full[:, slab] * _dot(cg, st_b))
            dcg = _dot_nt(dg_b[:, slab], st_b)
            dstate[:, slab] = _dot_tn(cg, dg_b[:, slab]) + eend_full[:, slab] * dsn
            bds_list.append(_dot(bg, dsn_b))
            dbg = _dot_nt(xd_b[:, slab], dsn_b)
            dlast_parts.append(jnp.sum(dsn * st_g, axis=0, keepdims=True))
            dcb = jnp.zeros((q, q), F32)
            dx_pairs = []
            for pr in range(SSD_HPG // 2):
                h0 = g * SSD_HPG + 2 * pr
                ps = slice(h0 * p, (h0 + 2) * p)
                l0 = _ssd_decay(cum, cum_t, h0, causal)
                l1 = _ssd_decay(cum, cum_t, h0 + 1, causal)
                m0, m1 = cb * l0, cb * l1
                dyp = dy_b[:, ps]
                zero = jnp.zeros_like(dyp)
                dy0, dy1 = jnp.where(low, dyp, zero), jnp.where(low, zero, dyp)
                dm0 = _dot_nt(dy0, xdt_b[:, ps])
                dm1 = _dot_nt(dy1, xdt_b[:, ps])
                w0, w1 = dm0 * m0, dm1 * m1
                dcum_mat = (dcum_mat + jnp.sum(w0, axis=1, keepdims=True) * (lane == h0).astype(F32)
                            + jnp.sum(w1, axis=1, keepdims=True) * (lane == h0 + 1).astype(F32))
                rmat = (rmat + jnp.where(sub == h0, jnp.sum(w0, axis=0, keepdims=True), 0.0)
                        + jnp.where(sub == h0 + 1, jnp.sum(w1, axis=0, keepdims=True), 0.0))
                dcb = dcb + dm0 * l0 + dm1 * l1
                dx_pairs.append(_dot_tn(jnp.concatenate([m0.astype(BF16), m1.astype(BF16)], axis=0),
                                        jnp.concatenate([dy0, dy1], axis=0)))
            dx_list.append(jnp.concatenate(dx_pairs, axis=1))
            dcb_b = dcb.astype(BF16)
            dxbc_ref[:, di + g * n:di + (g + 1) * n] = dbg + _dot_tn(dcb_b, cg)
            dxbc_ref[:, di + SSD_GROUPS * n + g * n:di + SSD_GROUPS * n + (g + 1) * n] = dcg + _dot(dcb_b, bg)
        bds_all = jnp.concatenate(bds_list, axis=1)
        dx_all = jnp.concatenate(dx_list, axis=1) + dend_full * bds_all
        last_row = cum[q - 1:q, :]
        qv = _dot_sel_b(xdt * bds_all, collapse) * jnp.exp(last_row - cum)
        dcum_mat = dcum_mat + _dot_sel_b(jnp.concatenate(yoff_prod, axis=1), collapse) - qv
        dlast_full = jnp.broadcast_to(jnp.concatenate(dlast_parts, axis=1), (SUBLANES, di))
        dlast_row = _dot_sel_b(dlast_full, collapse)[0:1, :] * jnp.exp(last_row) + jnp.sum(qv, axis=0, keepdims=True)
        dcum_mat = dcum_mat + jnp.where(is_last, dlast_row, 0.0)
        dla = _dot_tn_sel(ltri, dcum_mat, "a") - _dot_nt_sel_a((row <= col).astype(F32), rmat)
        ddt_mat = _dot_sel_b(dx_all * xs, collapse) + dla * a_row
        dxbc_ref[:, 0:di] = d_ref[...] * dy_all + dx_all * dt_full
        ddtr = ddt_mat * _sigmoid(dtr_v + bias_ref[...])
        dzdt_ref[:, di:di + LANES] = ddtr.astype(BF16)
        dd_full = jnp.broadcast_to(jnp.sum(dy_all * xs, axis=0, keepdims=True), (SUBLANES, di))
        dd_ref[...] += _dot_sel_b(dd_full, collapse)[0:1, :]
        dalog_ref[...] += jnp.sum(dla * dt, axis=0, keepdims=True) * a_row
        dbias_ref[...] += jnp.sum(ddtr, axis=0, keepdims=True)

        dxc = dxbc_ref[...] * (sig_c * (1.0 + xc * (1.0 - sig_c)))
        dcb_ref[...] += jnp.sum(dxc, axis=0, keepdims=True)
        taps = [jnp.sum(dxc * windows[k], axis=0, keepdims=True) for k in range(SSD_CONV)]
        dcw_ref[...] += jnp.concatenate(taps + [jnp.zeros((SUBLANES - SSD_CONV, CONV_DIM), F32)], axis=0)
        aft[0:q, :] = dxc
        after = aft[...]
        draw = dxc * cw_ref[3:4, :]
        for j in range(1, SSD_CONV):
            draw = draw + pltpu.roll(after, q + SUBLANES - j, 0)[0:q, :] * cw_ref[3 - j:4 - j, :]
        draw_ref[...] = draw.astype(BF16)
        aft[q:q + SUBLANES, :] = dxc[0:SUBLANES, :]

    blk = lambda w, j: pl.BlockSpec((q, w), lambda c: (nc - 1 - c, j))
    row_out = lambda w: jax.ShapeDtypeStruct((1, w), F32)
    return _tiled_call(
        body, name=name, grid=(nc,),
        in_specs=[blk(di, 0), blk(di, 0), blk(di, 0), blk(CONV_DIM, 0),
                  pl.BlockSpec((2 * SUBLANES, CONV_DIM), lambda c: (jnp.maximum((nc - 1 - c) * per - 1, 0), 0)),
                  _full((SUBLANES, CONV_DIM)), _full((1, CONV_DIM)), blk(LANES, 0),
                  pl.BlockSpec((1, n, di), lambda c: (nc - 1 - c, 0, 0)),
                  _full((1, LANES)), _full((1, LANES)), _full((1, di)), _full((1, di))],
        out_specs=[blk(di + LANES, 0), blk(CONV_DIM, 0),
                   _full((1, di)), _full((1, LANES)), _full((1, LANES)), _full((1, LANES)),
                   _full((1, CONV_DIM)), _full((SUBLANES, CONV_DIM))],
        out_shape=[jax.ShapeDtypeStruct((t, di + LANES), BF16), jax.ShapeDtypeStruct((t, CONV_DIM), BF16),
                   row_out(di), row_out(LANES), row_out(LANES), row_out(LANES),
                   row_out(CONV_DIM), jax.ShapeDtypeStruct((SUBLANES, CONV_DIM), F32)],
        scratch_shapes=[pltpu.VMEM((n, di), F32), pltpu.VMEM((q, di), F32), pltpu.VMEM((q, CONV_DIM), F32),
                        pltpu.VMEM((q, CONV_DIM), F32), pltpu.VMEM((SUBLANES + q, CONV_DIM), F32),
                        pltpu.VMEM((q + SUBLANES, CONV_DIM), F32)],
        operands=(dya, yssd, z, xbc_raw, xbc_raw, conv_w8, conv_b.reshape(1, CONV_DIM), dtr, states, bias_row,
                  alog_row, d_row, norm_g.reshape(1, di)),
        comm=comm)


S5_BLOCKS = 4
S5_BC = S5_WIDTH // S5_BLOCKS
S5_BS = S5_N // S5_BLOCKS


def _s5_tables(ar, ai, reverse):
    pows = [(ar, ai)]
    for _ in range(SUBLANES - 1):
        pr, pi = pows[-1]
        pows.append((pr * ar - pi * ai, pr * ai + pi * ar))
    row = lax.broadcasted_iota(jnp.int32, (SUBLANES, ar.shape[1]), 0)
    out = []
    for k in (1, 2, 4):
        pr, pi = pows[k - 1]
        mask = (row < SUBLANES - k) if reverse else (row >= k)
        out += [jnp.where(mask, pr, 0.0), jnp.where(mask, pi, 0.0)]
    order = list(range(SUBLANES))
    if reverse:
        order = order[::-1]
    out.append(jnp.concatenate([pows[e][0] for e in order], axis=0))
    out.append(jnp.concatenate([pows[e][1] for e in order], axis=0))
    return out


def s5_interleave(a, tm):
    t, c = a.shape
    return a.reshape(t // tm, SUBLANES, tm // SUBLANES, c).transpose(0, 2, 1, 3).reshape(t, c)


def s5_deinterleave(a, tm):
    t, c = a.shape
    return a.reshape(t // tm, tm // SUBLANES, SUBLANES, c).transpose(0, 2, 1, 3).reshape(t, c)


def _s5_scan_setup(ab_ref, base, tab, seg, reverse):
    ar = ab_ref[0:1, :]
    ai = -ab_ref[1:2, :] if reverse else ab_ref[1:2, :]
    base[0:1, :] = ar
    base[1:2, :] = ai
    assert seg & (seg - 1) == 0
    pr, pi = ar, ai
    for _ in range(seg.bit_length() - 1):
        pr, pi = pr * pr - pi * pi, 2.0 * pr * pi
    for k, tv in enumerate(_s5_tables(pr, pi, reverse)):
        tab[k] = tv


def _s5_scan(s_scr, base, tab, carry, tm, reverse):
    seg = tm // SUBLANES
    width = S5_BS
    first = lax.broadcasted_iota(jnp.int32, (SUBLANES, width), 0) == (SUBLANES - 1 if reverse else 0)

    def rows(i):
        step = (seg - 1 - i) if reverse else i
        return pl.ds(pl.multiple_of(step * SUBLANES, SUBLANES), SUBLANES)

    for lc in range(S5_N // width):
        lr = slice(lc * width, (lc + 1) * width)
        li = slice(S5_N + lc * width, S5_N + (lc + 1) * width)
        ar = jnp.broadcast_to(base[0:1, lr], (SUBLANES, width))
        ai = jnp.broadcast_to(base[1:2, lr], (SUBLANES, width))

        def advance(i, x, lr=lr, li=li, ar=ar, ai=ai):
            xr, xi = x
            return ar * xr - ai * xi + s_scr[rows(i), lr], ar * xi + ai * xr + s_scr[rows(i), li]

        zero = jnp.zeros((SUBLANES, width), F32)
        fr, fi = lax.fori_loop(0, seg, advance, (zero, zero))

        tb = [tab[k, :, lr] for k in range(8)]
        for lvl, k in enumerate((1, 2, 4)):
            sh = (SUBLANES - k) if reverse else k
            pr, pi = tb[2 * lvl], tb[2 * lvl + 1]
            rr, ri = pltpu.roll(fr, sh, 0), pltpu.roll(fi, sh, 0)
            fr, fi = fr + pr * rr - pi * ri, fi + pr * ri + pi * rr
        cr, ci = carry[0:1, lr], carry[0:1, li]
        fr, fi = fr + tb[6] * cr - tb[7] * ci, fi + tb[6] * ci + tb[7] * cr
        last = 0 if reverse else SUBLANES - 1
        carry[0:1, lr] = fr[last:last + 1, :]
        carry[0:1, li] = fi[last:last + 1, :]
        sh = (SUBLANES - 1) if reverse else 1
        entering = (jnp.where(first, cr, pltpu.roll(fr, sh, 0)), jnp.where(first, ci, pltpu.roll(fi, sh, 0)))

        def rerun(i, x, lr=lr, li=li, advance=advance):
            xr, xi = advance(i, x)
            s_scr[rows(i), lr] = xr
            s_scr[rows(i), li] = xi
            return xr, xi

        lax.fori_loop(0, seg, rerun, entering)


def _s5_scratch(tm):
    return [pltpu.VMEM((tm, 2 * S5_N), F32), pltpu.VMEM((SUBLANES, S5_N), F32),
            pltpu.VMEM((8, SUBLANES, S5_N), F32), pltpu.VMEM((SUBLANES, 2 * S5_N), F32)]


def _s5_put(scr, j, val):
    scr[:, j * S5_BS:(j + 1) * S5_BS] = val[:, :S5_BS]
    scr[:, S5_N + j * S5_BS:S5_N + (j + 1) * S5_BS] = val[:, S5_BS:]


def _s5_get(scr, j):
    return jnp.concatenate([scr[:, j * S5_BS:(j + 1) * S5_BS], scr[:, S5_N + j * S5_BS:S5_N + (j + 1) * S5_BS]], axis=1)


def s5_fwd(u5, bbc, ccc, ab8, d_row, wglu, bglu_row, *, tm, name, comm=None):
    t, w = u5.shape

    def body(u_ref, bb_ref, cc_ref, ab_ref, d_ref, wg_ref, bg_ref, s_ref, ys_ref, yb_ref, s_scr, base, tab, carry):
        i = pl.program_id(0)

        @pl.when(i == 0)
        def _():
            carry[...] = jnp.zeros_like(carry)
            _s5_scan_setup(ab_ref, base, tab, tm // SUBLANES, False)

        ub = u_ref[...]
        u = ub.astype(F32)
        for j in range(S5_BLOCKS):
            uj = ub[:, j * S5_BC:(j + 1) * S5_BC]
            _s5_put(s_scr, j, _dot(uj, bb_ref[j * S5_BC:(j + 1) * S5_BC, :]))
        _s5_scan(s_scr, base, tab, carry, tm, False)
        s_ref[...] = s_scr[...]
        ys = []
        for j in range(S5_BLOCKS):
            ys.append(_dot(_s5_get(s_scr, j).astype(BF16), cc_ref[:, j * S5_BC:(j + 1) * S5_BC]))
        y = jnp.concatenate(ys, axis=1) + d_ref[...] * u
        ys_ref[...] = y
        yb1 = _gelu(y)
        tt = _dot(yb1.astype(BF16), wg_ref[...]) + bg_ref[...]
        yb_ref[...] = (yb1 * _sigmoid(tt)).astype(BF16)

    tile = lambda c: pl.BlockSpec((tm, c), lambda i: (i, 0))
    return _tiled_call(
        body, name=name, grid=(t // tm,),
        in_specs=[tile(w), _full((w, 2 * S5_BS)), _full((2 * S5_BS, w)), _full((SUBLANES, S5_N)), _full((1, w)),
                  _full((w, w)), _full((1, w))],
        out_specs=[tile(2 * S5_N), tile(w), tile(w)],
        out_shape=[jax.ShapeDtypeStruct((t, 2 * S5_N), F32), jax.ShapeDtypeStruct((t, w), F32),
                   jax.ShapeDtypeStruct((t, w), BF16)],
        scratch_shapes=_s5_scratch(tm),
        operands=(u5, bbc, ccc, ab8, d_row, wglu, bglu_row), comm=comm)


def s5_bwd(dyb, ys5, u5, s, bbc, ccc, ab8, d_row, wglu, bglu_row, *, tm, name, comm=None):
    t, w = u5.shape
    nt = t // tm
    per = tm // SUBLANES

    def body(dyb_ref, ys_ref, u_ref, s_ref, sprev_ref, bb_ref, cc_ref, ab_ref, d_ref, wg_ref, bg_ref,
             du_ref, dbb_ref, dcc_ref, dab_ref, dd_ref, dwg_ref, dbg_ref, g_scr, base, tab, carry):
        i = pl.program_id(0)

        @pl.when(i == 0)
        def _():
            carry[...] = jnp.zeros_like(carry)
            _s5_scan_setup(ab_ref, base, tab, tm // SUBLANES, True)
            for r in (dbb_ref, dcc_ref, dab_ref, dd_ref, dwg_ref, dbg_ref):
                r[...] = jnp.zeros_like(r)

        y = ys_ref[...]
        ub = u_ref[...]
        u = ub.astype(F32)
        yb1 = _gelu(y)
        yb1b = yb1.astype(BF16)
        sg = _sigmoid(_dot(yb1b, wg_ref[...]) + bg_ref[...])
        dybv = dyb_ref[...]
        dtt = dybv * yb1 * sg * (1.0 - sg)
        dttb = dtt.astype(BF16)
        dyb1 = dybv * sg + _dot_nt(dttb, wg_ref[...])
        dwg_ref[...] += _dot_tn(yb1b, dttb)
        dbg_ref[...] += jnp.sum(dtt, axis=0, keepdims=True)
        dy = dyb1 * _gelu_grad(y)
        dd_ref[...] += jnp.sum(dy * u, axis=0, keepdims=True)
        dyh = dy.astype(BF16)
        for j in range(S5_BLOCKS):
            cs = slice(j * S5_BC, (j + 1) * S5_BC)
            _s5_put(g_scr, j, _dot_nt(dyh[:, cs], cc_ref[:, cs]))
        _s5_scan(g_scr, base, tab, carry, tm, True)
        dus = []
        for j in range(S5_BLOCKS):
            cs = slice(j * S5_BC, (j + 1) * S5_BC)
            gb = _s5_get(g_scr, j).astype(BF16)
            dus.append(_dot_nt(gb, bb_ref[cs, :]))
            dbb_ref[cs, :] += _dot_tn(ub[:, cs], gb)
            dcc_ref[:, cs] += _dot_tn(_s5_get(s_ref, j).astype(BF16), dyh[:, cs])
        du_ref[...] = (jnp.concatenate(dus, axis=1) + d_ref[...] * dy).astype(BF16)
        top = lax.broadcasted_iota(jnp.int32, (SUBLANES, S5_BS), 0) == 0

        def corr(g_sl, s_sl):
            before = jnp.where(i == nt - 1, 0.0, sprev_ref[SUBLANES - 1:SUBLANES, s_sl])
            wrap = jnp.where(top, before, pltpu.roll(s_ref[tm - SUBLANES:tm, s_sl], 1, 0))
            return (jnp.sum(g_scr[SUBLANES:tm, g_sl] * s_ref[0:tm - SUBLANES, s_sl], axis=0, keepdims=True)
                    + jnp.sum(g_scr[0:SUBLANES, g_sl] * wrap, axis=0, keepdims=True))

        for j in range(S5_BLOCKS):
            lr = slice(j * S5_BS, (j + 1) * S5_BS)
            li = slice(S5_N + j * S5_BS, S5_N + (j + 1) * S5_BS)
            dab_ref[0:1, lr] += corr(lr, lr) + corr(li, li)
            dab_ref[1:2, lr] += corr(li, lr) - corr(lr, li)

    tile = lambda c: pl.BlockSpec((tm, c), lambda i: (nt - 1 - i, 0))
    acc = lambda r, c: jax.ShapeDtypeStruct((r, c), F32)
    return _tiled_call(
        body, name=name, grid=(nt,),
        in_specs=[tile(w), tile(w), tile(w), tile(2 * S5_N),
                  pl.BlockSpec((SUBLANES, 2 * S5_N), lambda i: (jnp.maximum((nt - 1 - i) * per - 1, 0), 0)),
                  _full((w, 2 * S5_BS)), _full((2 * S5_BS, w)), _full((SUBLANES, S5_N)), _full((1, w)),
                  _full((w, w)), _full((1, w))],
        out_specs=[tile(w), _full((w, 2 * S5_BS)), _full((2 * S5_BS, w)), _full((SUBLANES, S5_N)), _full((1, w)),
                   _full((w, w)), _full((1, w))],
        out_shape=[jax.ShapeDtypeStruct((t, w), BF16), acc(w, 2 * S5_BS), acc(2 * S5_BS, w), acc(SUBLANES, S5_N),
                   acc(1, w), acc(w, w), acc(1, w)],
        scratch_shapes=_s5_scratch(tm),
        operands=(dyb, ys5, u5, s, s, bbc, ccc, ab8, d_row, wglu, bglu_row), comm=comm)


def s5_discretize(a_re, a_im, log_dt, b_re, b_im, *, name):
    n = a_re.shape[0]

    def body(ar_ref, ai_ref, dt_ref, br_ref, bi_ref, ab_ref, bbr_ref, bbi_ref):
        ar, ai, dtv = ar_ref[...], ai_ref[...], jnp.exp(dt_ref[...])
        mag = jnp.exp(ar * dtv)
        abr = mag * jnp.cos(ai * dtv)
        abi = mag * jnp.sin(ai * dtv)
        den = ar * ar + ai * ai
        nr = abr - 1.0
        cr = (nr * ar + abi * ai) / den
        ci = (abi * ar - nr * ai) / den
        ab_ref[:, 0:1] = abr
        ab_ref[:, 1:2] = abi
        br, bi = br_ref[...], bi_ref[...]
        bbr_ref[...] = cr * br - ci * bi
        bbi_ref[...] = cr * bi + ci * br

    col = jax.ShapeDtypeStruct((n, 2), F32)
    mat = jax.ShapeDtypeStruct((n, S5_GROUP), F32)
    return pl.pallas_call(body, name=name, out_shape=[col, mat, mat])(a_re, a_im, log_dt, b_re, b_im)


def s5_discretize_bwd(a_re, a_im, log_dt, b_re, b_im, dab, dbb_re, dbb_im, *, name):
    n = a_re.shape[0]

    def body(ar_ref, ai_ref, dt_ref, br_ref, bi_ref, dab_ref, dbr_ref, dbi_ref, da_ref, ddt_ref, gbr_ref, gbi_ref):
        ar, ai, dtv = ar_ref[...], ai_ref[...], jnp.exp(dt_ref[...])
        mag = jnp.exp(ar * dtv)
        abr = mag * jnp.cos(ai * dtv)
        abi = mag * jnp.sin(ai * dtv)
        den = ar * ar + ai * ai
        nr = abr - 1.0
        cr = (nr * ar + abi * ai) / den
        ci = (abi * ar - nr * ai) / den
        br, bi = br_ref[...], bi_ref[...]
        dbr, dbi = dbr_ref[...], dbi_ref[...]
        gbr_ref[...] = cr * dbr + ci * dbi
        gbi_ref[...] = cr * dbi - ci * dbr
        gcr = jnp.sum(dbr * br + dbi * bi, axis=1, keepdims=True)
        gci = jnp.sum(dbi * br - dbr * bi, axis=1, keepdims=True)
        ilr, ili = ar / den, -ai / den
        gabr = dab_ref[:, 0:1] + (ilr * gcr + ili * gci)
        gabi = dab_ref[:, 1:2] + (ilr * gci - ili * gcr)
        t1r, t1i = dtv * abr, dtv * abi
        t2r, t2i = -(cr * ilr - ci * ili), -(cr * ili + ci * ilr)
        da_ref[:, 0:1] = (t1r * gabr + t1i * gabi) + (t2r * gcr + t2i * gci)
        da_ref[:, 1:2] = (t1r * gabi - t1i * gabr) + (t2r * gci - t2i * gcr)
        lr, li = ar * abr - ai * abi, ar * abi + ai * abr
        dlog = (lr * gabr + li * gabi) * dtv
        ddt_ref[...] = jnp.sum(dlog.reshape(S5_GROUPS, S5_STATE, 1), axis=1)

    col2 = jax.ShapeDtypeStruct((n, 2), F32)
    col1 = jax.ShapeDtypeStruct((S5_GROUPS, 1), F32)
    mat = jax.ShapeDtypeStruct((n, S5_GROUP), F32)
    return pl.pallas_call(body, name=name, out_shape=[col2, col1, mat, mat])(
        a_re, a_im, log_dt, b_re, b_im, dab, dbb_re, dbb_im)


def merge_fwd(ya, yb, graw, x, wb, wout, *, tm, name):
    t, d = x.shape

    def body(ya_ref, yb_ref, g_ref, x_ref, wb_ref, wo_ref, x1_ref, pa_ref, pb_ref):
        pa = _dot(ya_ref[...], wb_ref[0:d, :])
        pb = _dot(yb_ref[...], wb_ref[d:, :])
        gate = _sigmoid(g_ref[...].astype(F32))
        merged = gate[:, :d] * pa + gate[:, d:] * pb
        x1_ref[...] = x_ref[...] + _dot(merged.astype(BF16), wo_ref[...])
        pa_ref[...] = pa.astype(BF16)
        pb_ref[...] = pb.astype(BF16)

    tile = lambda c: pl.BlockSpec((tm, c), lambda i: (i, 0))
    sds = jax.ShapeDtypeStruct
    return pl.pallas_call(
        body, name=name, grid=(t // tm,),
        in_specs=[tile(d), tile(S5_WIDTH), tile(2 * d), tile(d), _resident((d + S5_WIDTH, d)), _resident((d, d))],
        out_specs=[tile(d), tile(d), tile(d)], out_shape=[sds((t, d), F32), sds((t, d), BF16), sds((t, d), BF16)],
        compiler_params=_params("parallel"),
    )(ya, yb, graw, x, wb, wout)


def merge_bwd(dx1b, graw, pa, pb, wb, wout, *, tm, name):
    t, d = pa.shape

    def body(dx_ref, g_ref, pa_ref, pb_ref, wb_ref, wo_ref,
             mg_ref, dg_ref, dpa_ref, dpb_ref, dya_ref, dyb_ref):
        dm = _dot_nt(dx_ref[...], wo_ref[...])
        gate = _sigmoid(g_ref[...].astype(F32))
        g0, g1 = gate[:, :d], gate[:, d:]
        pa, pb = pa_ref[...].astype(F32), pb_ref[...].astype(F32)
        mg_ref[...] = (g0 * pa + g1 * pb).astype(BF16)
        dg_ref[:, :d] = (dm * pa * g0 * (1.0 - g0)).astype(BF16)
        dg_ref[:, d:] = (dm * pb * g1 * (1.0 - g1)).astype(BF16)
        dpa = (dm * g0).astype(BF16)
        dpb = (dm * g1).astype(BF16)
        dpa_ref[...] = dpa
        dpb_ref[...] = dpb
        dya_ref[...] = _dot_nt(dpa, wb_ref[0:d, :])
        dyb_ref[...] = _dot_nt(dpb, wb_ref[d:, :])

    tile = lambda c: pl.BlockSpec((tm, c), lambda i: (i, 0))
    sds = jax.ShapeDtypeStruct
    return pl.pallas_call(
        body, name=name, grid=(t // tm,),
        in_specs=[tile(d), tile(2 * d), tile(d), tile(d), _resident((d + S5_WIDTH, d)), _resident((d, d))],
        out_specs=[tile(d), tile(2 * d), tile(d), tile(d), tile(d), tile(S5_WIDTH)],
        out_shape=[sds((t, d), BF16), sds((t, 2 * d), BF16), sds((t, d), BF16), sds((t, d), BF16),
                   sds((t, d), F32), sds((t, S5_WIDTH), F32)],
        compiler_params=_params("parallel"),
    )(dx1b, graw, pa, pb, wb, wout)


def mlp_out_loss(a1, x1, w2, gf, target, *, tm, name):
    t, d = x1.shape
    f = a1.shape[1]

    def body(a_ref, x_ref, w_ref, g_ref, tg_ref, dx_ref, dxb_ref, loss_ref, dg_ref):
        i = pl.program_id(0)
        av = jnp.maximum(a_ref[...].astype(F32), 0.0)
        x2 = x_ref[...] + _dot((av * av).astype(BF16), w_ref[...])
        r = lax.rsqrt(jnp.mean(x2 * x2, axis=-1, keepdims=True) + EPS)
        xn = x2 * r
        err = xn * g_ref[...] - tg_ref[...]
        dyf = err * (1.0 / d)
        dxn = dyf * g_ref[...]
        dx = r * (dxn - xn * jnp.mean(dxn * xn, axis=-1, keepdims=True))
        dx_ref[...] = dx
        dxb_ref[...] = dx.astype(BF16)

        @pl.when(i == 0)
        def _():
            loss_ref[...] = jnp.zeros_like(loss_ref)
            dg_ref[...] = jnp.zeros_like(dg_ref)

        loss_ref[...] += jnp.sum(err * err) * (0.5 / d)
        dg_ref[...] += jnp.sum(dyf * xn, axis=0, keepdims=True)

    tile = lambda c: pl.BlockSpec((tm, c), lambda i: (i, 0))
    sds = jax.ShapeDtypeStruct
    return pl.pallas_call(
        body, name=name, grid=(t // tm,),
        in_specs=[tile(f), tile(d), _resident((f, d)), _full((1, d)), tile(d)],
        out_specs=[tile(d), tile(d), _full((1, LANES)), _full((1, d))],
        out_shape=[sds((t, d), F32), sds((t, d), BF16), sds((1, LANES), F32), sds((1, d), F32)],
        compiler_params=_params("arbitrary"),
    )(a1, x1, w2, gf.reshape(1, d), target)


def mlp_bwd_act(dx2b, a1, w2, *, tm, name):
    t, f = a1.shape
    d = dx2b.shape[1]

    def body(dx_ref, a_ref, w_ref, o_ref):
        dact = _dot_nt(dx_ref[...], w_ref[...])
        o_ref[...] = (dact * 2.0 * jnp.maximum(a_ref[...].astype(F32), 0.0)).astype(BF16)

    tile = lambda c: pl.BlockSpec((tm, c), lambda i: (i, 0))
    return pl.pallas_call(
        body, name=name, grid=(t // tm,),
        in_specs=[tile(d), tile(f), _resident((f, d))], out_specs=tile(f),
        out_shape=jax.ShapeDtypeStruct((t, f), BF16),
        compiler_params=_params("parallel"),
    )(dx2b, a1, w2)


ADAM_TILE_ELEMS = 128 * 1024


def _row_tile(rows, cap):
    if rows <= cap:
        return rows
    best = None
    for c in range(16, cap + 1, 16):
        if rows % c == 0:
            best = c
    assert best is not None, rows
    return best


def _device_order_sum(parts_ref):
    total = parts_ref[0].astype(F32)
    for k in range(1, N_DEV):
        total = total + parts_ref[k].astype(F32)
    return total


def _adamw_math(w, m, v, gparts_ref):
    g = _device_order_sum(gparts_ref)
    mn = ADAM_B1 * m + (1.0 - ADAM_B1) * g
    vn = ADAM_B2 * v + (1.0 - ADAM_B2) * (g * g)
    m_hat = mn / (1.0 - ADAM_B1 ** ADAM_STEP)
    v_hat = vn / (1.0 - ADAM_B2 ** ADAM_STEP)
    return g, -ADAM_LR * (m_hat / (jnp.sqrt(v_hat) + ADAM_EPS) + ADAM_WD * w), mn, vn


def adamw(w, m, v, gparts, *, name):
    rows, cols = w.shape
    tr = _row_tile(rows, max(16, ADAM_TILE_ELEMS // cols))

    def body(w_ref, m_ref, v_ref, g_ref, go_ref, d_ref, mo_ref, vo_ref):
        go_ref[...], d_ref[...], mo_ref[...], vo_ref[...] = _adamw_math(w_ref[...], m_ref[...], v_ref[...], g_ref)

    tile = pl.BlockSpec((tr, cols), lambda i: (i, 0))
    out = jax.ShapeDtypeStruct((rows, cols), F32)
    return pl.pallas_call(
        body, name=name, grid=(rows // tr,),
        in_specs=[tile, tile, tile, pl.BlockSpec((N_DEV, tr, cols), lambda i: (0, i, 0))],
        out_specs=[tile, tile, tile, tile], out_shape=[out, out, out, out],
        compiler_params=_params("parallel"),
    )(w, m, v, gparts)


def all_gather_arrays(blocks, *, name):
    na = len(blocks)

    def body(*refs):
        x_refs, out_refs = refs[:na], refs[na:2 * na]
        send_sems, recv_sems, local_sems = refs[2 * na:]
        x, y, c, _ = _place()
        me, sibling = (x, y, c), (x, y, 1 - c)
        chips = [(1 - x, y), (x, 1 - y), (1 - x, 1 - y)]

        def copy(a, k, blk, to, own=False):
            px, py, pc = blk
            dst = out_refs[a].at[4 * px + 2 * py + pc]
            return pltpu.make_async_remote_copy(
                src_ref=x_refs[a] if own else dst, dst_ref=dst,
                send_sem=send_sems.at[7 * a + k], recv_sem=recv_sems.at[7 * a + k], device_id=to, device_id_type=_MESH)

        mine = [pltpu.make_async_copy(x_refs[a], out_refs[a].at[4 * x + 2 * y + c], local_sems.at[a]) for a in range(na)]
        for cp in mine:
            cp.start()
        sent = []
        for a in range(na):
            first = [copy(a, 0, me, sibling, own=True)]
            first += [copy(a, 1 + j, me, (*chip, c), own=True) for j, chip in enumerate(chips)]
            for cp in first:
                cp.start()
            sent += first
        for a in range(na):
            for j, chip in enumerate(chips):
                copy(a, 1 + j, (*chip, c), me).wait_recv()
                fwd = copy(a, 4 + j, (*chip, c), sibling)
                fwd.start()
                sent.append(fwd)
        for a in range(na):
            copy(a, 0, sibling, me).wait_recv()
            for j, chip in enumerate(chips):
                copy(a, 4 + j, (*chip, 1 - c), me).wait_recv()
        for cp in sent:
            cp.wait_send()
        for cp in mine:
            cp.wait()

    any_spec = pl.BlockSpec(memory_space=pl.ANY)
    return pl.pallas_call(
        body, name=name, in_specs=[any_spec] * na, out_specs=[any_spec] * na,
        out_shape=[jax.ShapeDtypeStruct((N_DEV,) + b.shape, b.dtype) for b in blocks],
        scratch_shapes=[pltpu.SemaphoreType.DMA((7 * na,)), pltpu.SemaphoreType.DMA((7 * na,)),
                        pltpu.SemaphoreType.DMA((na,))],
        compiler_params=pltpu.CompilerParams(has_side_effects=True),
    )(*blocks)


def all_to_all(comm, *, name):
    na = len(comm)
    kinds = [k for k, _ in comm]

    def body(*refs):
        local, remote = _comm_copies(kinds, refs[:na], refs[na:2 * na], *refs[2 * na:])
        for cp in local + remote:
            cp.start()
        for cp in remote:
            cp.wait_recv()
        for cp in remote:
            cp.wait_send()
        for cp in local:
            cp.wait()

    any_spec = pl.BlockSpec(memory_space=pl.ANY)
    return pl.pallas_call(
        body, name=name, in_specs=[any_spec] * na, out_specs=[any_spec] * na,
        out_shape=_comm_out_shapes(comm),
        scratch_shapes=[pltpu.SemaphoreType.DMA((na * N_REL,)), pltpu.SemaphoreType.DMA((na * N_REL,)),
                        pltpu.SemaphoreType.DMA((na,))],
        compiler_params=pltpu.CompilerParams(has_side_effects=True),
    )(*[a for _, a in comm])


def adamw_whole(ws, ms, vs, gparts, *, name, totals=()):
    n, nt = len(ws), len(totals)

    def body(*refs):
        w_refs, m_refs, v_refs, g_refs = refs[:n], refs[n:2 * n], refs[2 * n:3 * n], refs[3 * n:4 * n]
        t_refs, refs = refs[4 * n:4 * n + nt], refs[4 * n + nt:]
        outs, t_outs = refs[:4 * n], refs[4 * n:]
        for k in range(n):
            g, delta, mn, vn = _adamw_math(w_refs[k][...], m_refs[k][...], v_refs[k][...], g_refs[k])
            outs[k][...] = g
            outs[n + k][...] = delta
            outs[2 * n + k][...] = mn
            outs[3 * n + k][...] = vn
        for t_ref, o_ref in zip(t_refs, t_outs):
            o_ref[...] = _device_order_sum(t_ref)

    shapes = [jax.ShapeDtypeStruct(w.shape, F32) for w in ws]
    res = pl.pallas_call(
        body, name=name, out_shape=shapes * 4 + [jax.ShapeDtypeStruct(t.shape[1:], F32) for t in totals],
        compiler_params=pltpu.CompilerParams(vmem_limit_bytes=VMEM_LIMIT),
    )(*ws, *ms, *vs, *gparts, *totals)
    return [res[j * n:(j + 1) * n] for j in range(4)], res[4 * n:]


def _lane_row(v):
    return jnp.pad(v.astype(F32), (0, LANES - v.shape[0])).reshape(1, LANES)


def _block_diag_b(bb):
    eye = jnp.eye(SUBLANES, dtype=F32)
    bt = bb.reshape(S5_BLOCKS, 8, S5_STATE, S5_GROUP).transpose(0, 1, 3, 2)
    return (bt[:, :, :, None, :] * eye[None, :, None, :, None]).reshape(S5_WIDTH, S5_BS)


def _block_diag_c(cc):
    eye = jnp.eye(SUBLANES, dtype=F32)
    ct = cc.reshape(S5_BLOCKS, 8, S5_GROUP, S5_STATE).transpose(3, 0, 1, 2)
    return (ct[None, :, :, :, :] * eye[:, None, None, :, None]).reshape(S5_BS, S5_WIDTH)


def _diag_blocks_b(m):
    eye = jnp.eye(SUBLANES, dtype=F32)
    m5 = m.reshape(S5_BLOCKS, 8, S5_GROUP, 8, S5_STATE)
    return jnp.sum(m5 * eye[None, :, None, :, None], axis=3).transpose(0, 1, 3, 2).reshape(S5_GROUPS, S5_STATE, S5_GROUP)


def _diag_blocks_c(m):
    eye = jnp.eye(SUBLANES, dtype=F32)
    m5 = m.reshape(8, S5_STATE, S5_BLOCKS, 8, S5_GROUP)
    return jnp.sum(m5 * eye[:, None, None, :, None], axis=0).transpose(1, 2, 3, 0).reshape(S5_GROUPS, S5_GROUP, S5_STATE)


def train_step(x, target, p, m, v):
    t = x.shape[0]
    tm = min(256, t)
    tb = min(512, t)
    ts = min(512, t)
    bf = lambda n: p[n].astype(BF16)
    slots = lambda n, a: a.reshape((N_DEV,) + _shard_shape(n))
    updated = {}

    def update(names, got):
        for n, parts in zip(names, got):
            updated[n] = adamw(p[n], m[n], v[n], parts, name=f"adamw_{n}")

    st_in, st_conv = all_gather_arrays([bf("w_in"), p["conv_w"]], name="gather_first")
    win_p = jnp.concatenate([st_in[k][:, a:b] for seg in IN_PADDED_ORDER for k, a, b in _shard_pieces(*IN_SEGMENTS[seg])]
                            + [jnp.zeros((D_MODEL, LANES - SSD_HEADS), BF16)], axis=1)
    conv_w8 = jnp.pad(_join_shards("conv_w", st_conv), ((0, SUBLANES - SSD_CONV), (0, 0)))
    (h, xbc_raw, graw, u5, z, dtr), (st_branch, st_out, st_glu) = rms_matmul(
        x, p["norm_mix_g"], win_p, IN_SPLITS, tm=tb, name="in_proj", out_dtypes=(BF16, BF16, BF16, BF16, F32),
        comm=gathers([bf("w_branch"), bf("w_out"), bf("s5_glu_w")]))
    w_branch = st_branch.reshape(D_MODEL + S5_WIDTH, D_MODEL)
    wout, wglu = st_out.reshape(D_MODEL, D_MODEL), st_glu.reshape(S5_WIDTH, S5_WIDTH)
    bias_row, alog_row = _lane_row(p["dt_bias"]), _lane_row(p["a_log"])
    d_row = jnp.repeat(p["d_ssd"], SSD_HEADDIM).reshape(1, SSD_HEADS * SSD_HEADDIM)
    (ya, yssd, states), (w1,) = ssd_fwd(xbc_raw, conv_w8, p["conv_b"], dtr, z, bias_row, alog_row, d_row,
                                        p["ssd_norm_g"], name="ssd_fwd", comm=gathers([bf("w_mlp_in")]))

    n = S5_N
    a_re_c, a_im_c = p["s5_a_re"].reshape(n, 1), p["s5_a_im"].reshape(n, 1)
    dt_c = jnp.repeat(p["s5_log_dt"], S5_STATE).reshape(n, 1)
    b_re_m, b_im_m = p["s5_b_re"].reshape(n, S5_GROUP), p["s5_b_im"].reshape(n, S5_GROUP)
    ab, bb_re, bb_im = s5_discretize(a_re_c, a_im_c, dt_c, b_re_m, b_im_m, name="s5_disc")
    ab8 = jnp.pad(ab.T, ((0, SUBLANES - 2), (0, 0)))
    bbc = jnp.concatenate([_block_diag_b(bb_re.reshape(S5_GROUPS, S5_STATE, S5_GROUP)),
                           _block_diag_b(bb_im.reshape(S5_GROUPS, S5_STATE, S5_GROUP))], axis=1).astype(BF16)
    ccc = jnp.concatenate([_block_diag_c(p["s5_c_re"]), -_block_diag_c(p["s5_c_im"])], axis=0).astype(BF16)
    s5d_row = p["s5_d"].reshape(1, S5_WIDTH)
    bglu_row = p["s5_glu_b"].reshape(1, S5_WIDTH)
    u5 = s5_interleave(u5, ts)
    (s, ys5, yb), (st_w2,) = s5_fwd(u5, bbc, ccc, ab8, s5d_row, wglu, bglu_row, tm=ts, name="s5_fwd",
                                    comm=gathers([bf("w_mlp_out")]))
    yb = s5_deinterleave(yb, ts)
    w2 = st_w2.reshape(D_FF, D_MODEL)

    x1, pa, pb = merge_fwd(ya, yb, graw, x, w_branch, wout, tm=tb, name="merge_fwd")
    (h2, a1), _ = rms_matmul(x1, p["norm_mlp_g"], w1, (D_FF,), tm=tb, name="mlp_in", out_dtypes=(BF16,))
    dx2, dx2b, loss_row, dgf = mlp_out_loss(a1, x1, w2, p["norm_final_g"], target, tm=tb, name="mlp_out_loss")

    g = {}
    g["norm_final_g"] = dgf[0]
    da1 = mlp_bwd_act(dx2b, a1, w2, tm=tb, name="mlp_bwd_act")
    dw2 = slots("w_mlp_out", matmul_tn(a1, dx2b, out_dtype=BF16, a_relu2=True, name="dw_mlp_out"))
    dw1 = matmul_tn(h2, da1, out_dtype=BF16, col_shards=N_DEV, name="dw_mlp_in")
    (dx1, dx1b, dgm), _ = matmul_nt_rms_bwd([da1], w1, x1, p["norm_mlp_g"], dx2, tm=tb, name="mlp_in_bwd")
    g["norm_mlp_g"] = dgm[0]
    merged, dgraw, dpa, dpb, dya, dyb = merge_bwd(dx1b, graw, pa, pb, w_branch, wout, tm=tm, name="merge_bwd")
    dwo = slots("w_out", matmul_tn(merged, dx1b, out_dtype=BF16, name="dw_out"))
    dwb = slots("w_branch", jnp.concatenate([matmul_tn(ya, dpa, out_dtype=BF16, name="dw_branch_a"),
                                             matmul_tn(yb, dpb, out_dtype=BF16, name="dw_branch_b")], axis=0))

    (du5, dbbc, dccc, dab, dd5, dwglu, dbglu), got = s5_bwd(
        s5_interleave(dyb, ts), ys5, u5, s, bbc, ccc, ab8, s5d_row, wglu, bglu_row, tm=ts, name="s5_bwd",
        comm=exchanges([dw2, dw1]))
    du5 = s5_deinterleave(du5, ts)
    update(["w_mlp_out", "w_mlp_in"], got)
    g["s5_glu_b"], g["s5_d"] = dbglu[0], dd5[0]
    g["s5_c_re"] = _diag_blocks_c(dccc[:S5_BS])
    g["s5_c_im"] = -_diag_blocks_c(dccc[S5_BS:])
    dbb_re = _diag_blocks_b(dbbc[:, :S5_BS]).reshape(n, S5_GROUP)
    dbb_im = _diag_blocks_b(dbbc[:, S5_BS:]).reshape(n, S5_GROUP)
    da, dlogdt, gb_re, gb_im = s5_discretize_bwd(a_re_c, a_im_c, dt_c, b_re_m, b_im_m, dab[:2].T, dbb_re, dbb_im,
                                                  name="s5_disc_bwd")
    g["s5_a_re"] = da[:, 0].reshape(S5_GROUPS, S5_STATE)
    g["s5_a_im"] = da[:, 1].reshape(S5_GROUPS, S5_STATE)
    g["s5_log_dt"] = dlogdt[:, 0]
    g["s5_b_re"] = gb_re.reshape(S5_GROUPS, S5_STATE, S5_GROUP)
    g["s5_b_im"] = gb_im.reshape(S5_GROUPS, S5_STATE, S5_GROUP)

    def update_small(names, got, name, totals=()):
        (gs, ds, nm, nv), sums = adamw_whole([p[n] for n in names], [m[n] for n in names], [v[n] for n in names], got,
                                             name=name, totals=totals)
        for k, n in enumerate(names):
            updated[n] = (gs[k], ds[k], nm[k], nv[k])
        return sums

    wide = ["s5_b_re", "s5_b_im"]
    late = ["ssd_norm_g", "d_ssd", "a_log", "dt_bias", "conv_b"]
    early = [n for n in SMALL_ORDER if n not in wide + late and n != "norm_mix_g"]
    (dzdt, dxbc_raw, dng, dd_row, dalog_row, dbias_row, dconv_b, dconv_w8), got = ssd_bwd(
        dya, yssd, z, xbc_raw, conv_w8, p["conv_b"], dtr, states, bias_row, alog_row, d_row, p["ssd_norm_g"],
        name="ssd_bwd",
        comm=exchanges([dwo, dwb, slots("s5_glu_w", dwglu.astype(BF16))])
        + gathers([g[n] for n in wide + early] + [loss_row]))
    update(["w_out", "w_branch", "s5_glu_w"], got[:3])
    update_small(wide, got[3:3 + len(wide)], "adamw_s5_b")
    loss_sum, = update_small(early, got[3 + len(wide):-1], "adamw_small_early", totals=[got[-1]])
    g["ssd_norm_g"] = dng[0]
    g["d_ssd"], g["a_log"], g["dt_bias"] = dd_row[0, :SSD_HEADS], dalog_row[0, :SSD_HEADS], dbias_row[0, :SSD_HEADS]
    g["conv_b"] = dconv_b[0]
    dconv = dconv_w8[:SSD_CONV].reshape(SSD_CONV, N_DEV, CONV_DIM // N_DEV).transpose(1, 0, 2)
    dproj = [dxbc_raw, dgraw, du5, dzdt]
    dxbc_w, dgate_w, du5_w, dzdt_w = [matmul_tn(h, piece, out_dtype=BF16, name=f"dw_in_{k}")
                                      for k, piece in enumerate(dproj)]
    by_segment = {"z": dzdt_w[:, :D_MODEL], "xbc": dxbc_w, "dt": dzdt_w[:, D_MODEL:], "u5": du5_w, "gates": dgate_w}
    width = D_IN_PROJ // N_DEV
    dw_in = jnp.stack([jnp.concatenate(
        [by_segment[seg][:, max(k * width, lo) - lo:min((k + 1) * width, hi) - lo]
         for seg, (lo, hi) in IN_SEGMENTS.items() if max(k * width, lo) < min((k + 1) * width, hi)], axis=1)
        for k in range(N_DEV)])

    (grad_x, dgx), got = matmul_nt_rms_bwd(
        dproj, win_p, x, p["norm_mix_g"], dx1, tm=tm, name="in_proj_bwd", bf16_copy=False,
        comm=exchanges([dw_in, dconv]) + gathers([g[n] for n in late]))
    update(["w_in", "conv_w"], got[:2])
    update_small(late, got[2:], "adamw_small_late")
    update_small(["norm_mix_g"], all_to_all(gathers([dgx[0]]), name="gather_last"), "adamw_small_last")
    return loss_sum[0, 0], grad_x, updated


WEIGHT_ORDER = ["norm_mix_g", "w_in", "conv_w", "conv_b", "dt_bias", "a_log", "d_ssd", "ssd_norm_g", "s5_a_re",
                "s5_a_im", "s5_log_dt", "s5_b_re", "s5_b_im", "s5_c_re", "s5_c_im", "s5_d", "s5_glu_w", "s5_glu_b",
                "w_branch", "w_out", "norm_mlp_g", "w_mlp_in", "w_mlp_out", "norm_final_g"]
SHARDED = {"w_in": ((D_MODEL, D_IN_PROJ), 1), "conv_w": ((SSD_CONV, CONV_DIM), 1), "s5_glu_w": ((S5_WIDTH, S5_WIDTH), 0),
           "w_branch": ((D_MODEL + S5_WIDTH, D_MODEL), 0), "w_out": ((D_MODEL, D_MODEL), 0),
           "w_mlp_in": ((D_MODEL, D_FF), 1), "w_mlp_out": ((D_FF, D_MODEL), 0)}
SMALL_ORDER = [n for n in WEIGHT_ORDER if n not in SHARDED]


def _shard_shape(name):
    (r, c), ax = SHARDED[name]
    return (r, c // N_DEV) if ax == 1 else (r // N_DEV, c)


def _join_shards(name, stacked):
    (r, c), ax = SHARDED[name]
    if ax == 1:
        return stacked.transpose(1, 0, 2).reshape(r, c)
    return stacked.reshape(r, c)


def _shard_pieces(lo, hi):
    width = D_IN_PROJ // N_DEV
    out = []
    while lo < hi:
        k = lo // width
        end = min(hi, (k + 1) * width)
        out.append((k, lo - k * width, end - k * width))
        lo = end
    return out


IN_SEGMENTS = {"z": (0, 1024), "xbc": (1024, 3072), "dt": (3072, 3088), "u5": (3088, 3600), "gates": (3600, 5648)}
IN_PADDED_ORDER = ("xbc", "gates", "u5", "z", "dt")


def kernel(x, norm_mix_g, w_in, conv_w, conv_b, dt_bias, a_log, d_ssd, ssd_norm_g, s5_a_re, s5_a_im, s5_log_dt, s5_b_re, s5_b_im, s5_c_re, s5_c_im, s5_d, s5_glu_w, s5_glu_b, w_branch, w_out, norm_mlp_g, w_mlp_in, w_mlp_out, norm_final_g, loss_target, m_norm_mix_g, m_w_in, m_conv_w, m_conv_b, m_dt_bias, m_a_log, m_d_ssd, m_ssd_norm_g, m_s5_a_re, m_s5_a_im, m_s5_log_dt, m_s5_b_re, m_s5_b_im, m_s5_c_re, m_s5_c_im, m_s5_d, m_s5_glu_w, m_s5_glu_b, m_w_branch, m_w_out, m_norm_mlp_g, m_w_mlp_in, m_w_mlp_out, m_norm_final_g, v_norm_mix_g, v_w_in, v_conv_w, v_conv_b, v_dt_bias, v_a_log, v_d_ssd, v_ssd_norm_g, v_s5_a_re, v_s5_a_im, v_s5_log_dt, v_s5_b_re, v_s5_b_im, v_s5_c_re, v_s5_c_im, v_s5_d, v_s5_glu_w, v_s5_glu_b, v_w_branch, v_w_out, v_norm_mlp_g, v_w_mlp_in, v_w_mlp_out, v_norm_final_g):
    w = dict(norm_mix_g=norm_mix_g, w_in=w_in, conv_w=conv_w, conv_b=conv_b, dt_bias=dt_bias, a_log=a_log, d_ssd=d_ssd,
             ssd_norm_g=ssd_norm_g, s5_a_re=s5_a_re, s5_a_im=s5_a_im, s5_log_dt=s5_log_dt, s5_b_re=s5_b_re,
             s5_b_im=s5_b_im, s5_c_re=s5_c_re, s5_c_im=s5_c_im, s5_d=s5_d, s5_glu_w=s5_glu_w, s5_glu_b=s5_glu_b,
             w_branch=w_branch, w_out=w_out, norm_mlp_g=norm_mlp_g, w_mlp_in=w_mlp_in, w_mlp_out=w_mlp_out,
             norm_final_g=norm_final_g)
    m = dict(norm_mix_g=m_norm_mix_g, w_in=m_w_in, conv_w=m_conv_w, conv_b=m_conv_b, dt_bias=m_dt_bias, a_log=m_a_log,
             d_ssd=m_d_ssd, ssd_norm_g=m_ssd_norm_g, s5_a_re=m_s5_a_re, s5_a_im=m_s5_a_im, s5_log_dt=m_s5_log_dt,
             s5_b_re=m_s5_b_re, s5_b_im=m_s5_b_im, s5_c_re=m_s5_c_re, s5_c_im=m_s5_c_im, s5_d=m_s5_d,
             s5_glu_w=m_s5_glu_w, s5_glu_b=m_s5_glu_b, w_branch=m_w_branch, w_out=m_w_out, norm_mlp_g=m_norm_mlp_g,
             w_mlp_in=m_w_mlp_in, w_mlp_out=m_w_mlp_out, norm_final_g=m_norm_final_g)
    v = dict(norm_mix_g=v_norm_mix_g, w_in=v_w_in, conv_w=v_conv_w, conv_b=v_conv_b, dt_bias=v_dt_bias, a_log=v_a_log,
             d_ssd=v_d_ssd, ssd_norm_g=v_ssd_norm_g, s5_a_re=v_s5_a_re, s5_a_im=v_s5_a_im, s5_log_dt=v_s5_log_dt,
             s5_b_re=v_s5_b_re, s5_b_im=v_s5_b_im, s5_c_re=v_s5_c_re, s5_c_im=v_s5_c_im, s5_d=v_s5_d,
             s5_glu_w=v_s5_glu_w, s5_glu_b=v_s5_glu_b, w_branch=v_w_branch, w_out=v_w_out, norm_mlp_g=v_norm_mlp_g,
             w_mlp_in=v_w_mlp_in, w_mlp_out=v_w_mlp_out, norm_final_g=v_norm_final_g)

    loss, grad_x, upd = train_step(x[0], loss_target[0], w, m, v)
    return (loss, grad_x.reshape(x.shape), *[upd[n][j] for j in range(4) for n in WEIGHT_ORDER])
```

```python
import math

import jax
import jax.numpy as jnp
from jax import lax
from jax.experimental import pallas as pl
from jax.experimental.pallas import tpu as pltpu

F32 = jnp.float32
BF16 = jnp.bfloat16

N_DEV = 8
D_MODEL = 1024
SSD_HEADS = 16
SSD_HEADDIM = 64
SSD_GROUPS = 4
SSD_HPG = 4
SSD_STATE = 128
SSD_CHUNK = 128
SSD_CONV = 4
CONV_DIM = 2048
S5_WIDTH = 512
S5_GROUP = 16
S5_GROUPS = 32
S5_STATE = 64
S5_N = S5_GROUPS * S5_STATE
D_FF = 4096
D_IN_PROJ = 5648
IN_SPLITS = (2048, 2048, 512, 1024, 128)
EPS = 1e-6
LANES = 128
SUBLANES = 8
VMEM_LIMIT = 56 * 1024 * 1024

ADAM_LR = 0.001
ADAM_B1 = 0.9
ADAM_B2 = 0.999
ADAM_EPS = 1e-08
ADAM_WD = 0.01
ADAM_STEP = 10

GELU_C = math.sqrt(2.0 / math.pi)
GELU_K = 0.044715


def _params(*sem):
    return pltpu.CompilerParams(dimension_semantics=sem, vmem_limit_bytes=VMEM_LIMIT)


def _full(shape):
    nd = len(shape)
    return pl.BlockSpec(shape, lambda *_: (0,) * nd)


def _resident(shape):
    nd = len(shape)
    return pl.BlockSpec(shape, lambda *_: (0,) * nd, pipeline_mode=pl.Buffered(1))


_MESH = pl.DeviceIdType.MESH
_RELATIONS = [(dx, dy, dc) for dx in (0, 1) for dy in (0, 1) for dc in (0, 1)][1:]
N_REL = len(_RELATIONS)


def _place():
    x, y, c = lax.axis_index("x"), lax.axis_index("y"), lax.axis_index("c")
    return x, y, c, 4 * x + 2 * y + c


def gathers(arrays):
    return [("gather", a) for a in arrays]


def exchanges(arrays):
    return [("exchange", a) for a in arrays]


def exchanges_to(lo, hi, arrays):
    return [(("exchange_to", lo, hi), a) for a in arrays]


def _comm_out_shapes(comm):
    def shape(kind, a):
        if kind == "gather":
            return (N_DEV,) + a.shape
        return (N_DEV,) + a.shape[1:]

    return [jax.ShapeDtypeStruct(shape(kind, a), a.dtype) for kind, a in comm]


def _comm_copies(kinds, src_refs, dst_refs, send_sems, recv_sems, local_sems):
    x, y, c, idx = _place()
    local, remote = [], []
    for a, (kind, src, dst) in enumerate(zip(kinds, src_refs, dst_refs)):
        lo, hi = (kind[1], kind[2]) if isinstance(kind, tuple) else (0, N_DEV)
        partial = isinstance(kind, tuple)
        mine = (idx >= lo) & (idx < hi) if partial else None

        def slot(dev, src=src, kind=kind, lo=lo, hi=hi):
            if kind == "gather":
                return src
            return src.at[jnp.clip(dev - lo, 0, hi - lo - 1)]

        local.append((pltpu.make_async_copy(slot(idx), dst.at[idx], local_sems.at[a]), mine))
        for k, (dx, dy, dc) in enumerate(_RELATIONS):
            px, py, pc = x ^ dx, y ^ dy, c ^ dc
            peer = 4 * px + 2 * py + pc
            cp = pltpu.make_async_remote_copy(
                src_ref=slot(peer), dst_ref=dst.at[idx],
                send_sem=send_sems.at[N_REL * a + k], recv_sem=recv_sems.at[N_REL * a + k],
                device_id=(px, py, pc), device_id_type=_MESH)
            remote.append((cp, (peer >= lo) & (peer < hi) if partial else None, mine))
    return local, remote


def _when(cond, action):
    if cond is None:
        action()
    else:
        pl.when(cond)(action)


def _comm_start(local, remote):
    for cp, cond in local:
        _when(cond, cp.start)
    for cp, send, _ in remote:
        _when(send, cp.start)


def _comm_wait(local, remote):
    for cp, _, recv in remote:
        _when(recv, cp.wait_recv)
    for cp, send, _ in remote:
        _when(send, cp.wait_send)
    for cp, cond in local:
        _when(cond, cp.wait)


def _tiled_call(body, *, name, grid, in_specs, out_specs, out_shape, operands, scratch_shapes=(), comm=None):
    params = pltpu.CompilerParams(dimension_semantics=("arbitrary",), vmem_limit_bytes=VMEM_LIMIT,
                                  has_side_effects=bool(comm))
    if not comm:
        outs = pl.pallas_call(body, name=name, grid=grid, in_specs=in_specs, out_specs=out_specs, out_shape=out_shape,
                              scratch_shapes=list(scratch_shapes), compiler_params=params)(*operands)
        return outs, []
    kind = [k for k, _ in comm]
    arrays = [a for _, a in comm]
    na, n_in, n_out, n_scr = len(arrays), len(in_specs), len(out_specs), len(scratch_shapes)
    last = grid[0] - 1

    def hosted(*refs):
        ins, refs = refs[:n_in], refs[n_in:]
        cin, refs = refs[:na], refs[na:]
        outs, refs = refs[:n_out], refs[n_out:]
        cout, refs = refs[:na], refs[na:]
        scr, sems = refs[:n_scr], refs[n_scr:]
        i = pl.program_id(0)

        @pl.when(i == 0)
        def _():
            _comm_start(*_comm_copies(kind, cin, cout, *sems))

        body(*ins, *outs, *scr)

        @pl.when(i == last)
        def _():
            _comm_wait(*_comm_copies(kind, cin, cout, *sems))

    any_spec = pl.BlockSpec(memory_space=pl.ANY)
    res = pl.pallas_call(
        hosted, name=name, grid=grid,
        in_specs=list(in_specs) + [any_spec] * na, out_specs=list(out_specs) + [any_spec] * na,
        out_shape=list(out_shape) + _comm_out_shapes(comm),
        scratch_shapes=list(scratch_shapes) + [pltpu.SemaphoreType.DMA((N_REL * na,)), pltpu.SemaphoreType.DMA((N_REL * na,)),
                                               pltpu.SemaphoreType.DMA((na,))],
        compiler_params=params)(*operands, *arrays)
    return res[:n_out], res[n_out:]


def _dot(a, b):
    return jnp.dot(a, b, preferred_element_type=F32)


def _dot_nt(a, b):
    return lax.dot_general(a, b, (((1,), (1,)), ((), ())), preferred_element_type=F32)


def _dot_tn(a, b):
    return lax.dot_general(a, b, (((0,), (0,)), ((), ())), preferred_element_type=F32)


def _split2(x):
    hi = x.astype(BF16)
    return hi, (x - hi.astype(F32)).astype(BF16)


def _select_dot(dot, a, b, exact):
    if exact == "a":
        ea, (hi, lo) = a.astype(BF16), _split2(b)
        return dot(ea, hi) + dot(ea, lo)
    eb, (hi, lo) = b.astype(BF16), _split2(a)
    return dot(hi, eb) + dot(lo, eb)


def _dot_sel_a(a, b):
    return _select_dot(_dot, a, b, "a")


def _dot_sel_b(a, b):
    return _select_dot(_dot, a, b, "b")


def _dot_nt_sel_a(a, b):
    return _select_dot(_dot_nt, a, b, "a")


def _dot_tn_sel(a, b, exact):
    return _select_dot(_dot_tn, a, b, exact)


def _sigmoid(x):
    return 1.0 / (1.0 + jnp.exp(-x))


def _softplus(x):
    return jnp.maximum(x, 0.0) + jnp.log(1.0 + jnp.exp(-jnp.abs(x)))


def _gelu(x):
    return 0.5 * x * (1.0 + jnp.tanh(GELU_C * (x + GELU_K * x * x * x)))


def _gelu_grad(x):
    th = jnp.tanh(GELU_C * (x + GELU_K * x * x * x))
    return 0.5 * (1.0 + th) + 0.5 * x * (1.0 - th * th) * GELU_C * (1.0 + 3.0 * GELU_K * x * x)


def rms_matmul(x, g, w, splits, *, tm, name, comm=None, out_dtypes=None):
    t, d = x.shape
    stacked = w.ndim == 3
    n = w.shape[0] * w.shape[2] if stacked else w.shape[1]
    assert sum(splits) == n and (len(splits) == 1 or not stacked)

    def body(x_ref, g_ref, w_ref, h_ref, *o_refs):
        xv = x_ref[...]
        r = lax.rsqrt(jnp.mean(xv * xv, axis=-1, keepdims=True) + EPS)
        h = (xv * r * g_ref[...]).astype(BF16)
        h_ref[...] = h
        if stacked:
            wk = w.shape[2]
            for k in range(w.shape[0]):
                o_refs[0][:, k * wk:(k + 1) * wk] = _dot(h, w_ref[k]).astype(o_refs[0].dtype)
            return
        off = 0
        for o_ref, width in zip(o_refs, splits):
            o_ref[...] = _dot(h, w_ref[:, off:off + width]).astype(o_ref.dtype)
            off += width

    tile = lambda c: pl.BlockSpec((tm, c), lambda i: (i, 0))
    dtypes = out_dtypes or (F32,) * len(splits)
    return _tiled_call(
        body, name=name, grid=(t // tm,),
        in_specs=[tile(d), _full((1, d)), _resident(w.shape)],
        out_specs=[tile(d)] + [tile(c) for c in splits],
        out_shape=[jax.ShapeDtypeStruct((t, d), BF16)] + [jax.ShapeDtypeStruct((t, c), dt) for c, dt in zip(splits, dtypes)],
        operands=(x, g.reshape(1, d), w), comm=comm)


MATMUL_TN_ACC_BYTES = 16 * 1024 * 1024


def _pick(n, cap):
    best = LANES
    for c in range(LANES, cap + 1, LANES):
        if n % c == 0:
            best = c
    return best


def matmul_tn(a, b, *, name, out_dtype=F32, col_shards=None, tk=1024, a_relu2=False):
    t, m = a.shape
    n = b.shape[1]
    tm = _pick(m, 1024)
    whole_n = tm * n * 4 <= MATMUL_TN_ACC_BYTES
    tn = n if whole_n else _pick(n, 1280)
    assert whole_n or not col_shards
    tk = min(tk if tn <= 2048 else tk // 2, t)
    nk = t // tk

    def body(a_ref, b_ref, o_ref, acc):
        k = pl.program_id(2)

        @pl.when(k == 0)
        def _():
            acc[...] = jnp.zeros_like(acc)

        av = a_ref[...]
        if a_relu2:
            pos = jnp.maximum(av.astype(F32), 0.0)
            av = (pos * pos).astype(BF16)
        acc[...] += _dot_tn(av, b_ref[...])

        @pl.when(k == nk - 1)
        def _():
            if col_shards:
                w = n // col_shards
                for s in range(col_shards):
                    o_ref[s] = acc[:, s * w:(s + 1) * w].astype(out_dtype)
            else:
                o_ref[...] = acc[...].astype(out_dtype)

    if col_shards:
        out_spec = pl.BlockSpec((col_shards, tm, n // col_shards), lambda i, j, k: (0, i, 0))
        out_shape = jax.ShapeDtypeStruct((col_shards, m, n // col_shards), out_dtype)
    else:
        out_spec = pl.BlockSpec((tm, tn), lambda i, j, k: (i, j))
        out_shape = jax.ShapeDtypeStruct((m, n), out_dtype)
    return pl.pallas_call(
        body, name=name, grid=(m // tm, n // tn, nk),
        in_specs=[pl.BlockSpec((tk, tm), lambda i, j, k: (k, i)), pl.BlockSpec((tk, tn), lambda i, j, k: (k, j))],
        out_specs=out_spec, out_shape=out_shape,
        scratch_shapes=[pltpu.VMEM((tm, tn), F32)],
        compiler_params=_params("parallel", "parallel", "arbitrary"),
    )(a, b)


def matmul_nt_rms_bwd(dys, w, x, g, dres, *, tm, name, comm=None, bf16_copy=True):
    t = x.shape[0]
    stacked = w.ndim == 3
    d = w.shape[1] if stacked else w.shape[0]
    widths = [dy.shape[1] for dy in dys]
    n_dy = len(dys)

    def body(*refs):
        dy_refs = refs[:n_dy]
        w_ref, x_ref, g_ref, dres_ref, dx_ref = refs[n_dy:n_dy + 5]
        dxb_ref = refs[n_dy + 5] if bf16_copy else None
        dg_ref = refs[-1]
        i = pl.program_id(0)
        if stacked:
            wk = w.shape[2]
            dh = _dot_nt(dy_refs[0][:, 0:wk], w_ref[0])
            for k in range(1, w.shape[0]):
                dh = dh + _dot_nt(dy_refs[0][:, k * wk:(k + 1) * wk], w_ref[k])
        else:
            dh, off = None, 0
            for dy_ref, width in zip(dy_refs, widths):
                part = _dot_nt(dy_ref[...], w_ref[:, off:off + width])
                dh = part if dh is None else dh + part
                off += width
        xv = x_ref[...]
        r = lax.rsqrt(jnp.mean(xv * xv, axis=-1, keepdims=True) + EPS)
        xn = xv * r
        dxn = dh * g_ref[...]
        dx = dres_ref[...] + r * (dxn - xn * jnp.mean(dxn * xn, axis=-1, keepdims=True))
        dx_ref[...] = dx
        if bf16_copy:
            dxb_ref[...] = dx.astype(BF16)

        @pl.when(i == 0)
        def _():
            dg_ref[...] = jnp.zeros_like(dg_ref)

        dg_ref[...] += jnp.sum(dh * xn, axis=0, keepdims=True)

    tile = lambda c: pl.BlockSpec((tm, c), lambda i: (i, 0))
    copies = [BF16] if bf16_copy else []
    return _tiled_call(
        body, name=name, grid=(t // tm,),
        in_specs=[tile(c) for c in widths] + [_resident(w.shape), tile(d), _full((1, d)), tile(d)],
        out_specs=[tile(d)] + [tile(d) for _ in copies] + [_full((1, d))],
        out_shape=[jax.ShapeDtypeStruct((t, d), F32)] + [jax.ShapeDtypeStruct((t, d), dt) for dt in copies]
        + [jax.ShapeDtypeStruct((1, d), F32)],
        operands=(*dys, w, x, g.reshape(1, d), dres), comm=comm)


def _conv_windows(ext_ref, tm):
    ext = ext_ref[...]
    return [pltpu.roll(ext, 3 - k, 0)[SUBLANES:SUBLANES + tm, :] for k in range(3)] + [ext[SUBLANES:SUBLANES + tm, :]]


def _conv_taps(windows, w_ref):
    acc = windows[3] * w_ref[3:4, :]
    for k in range(3):
        acc = acc + windows[k] * w_ref[k:k + 1, :]
    return acc


def _ssd_chunk_terms(dtr, bias_row, alog_row):
    q = SSD_CHUNK
    row = lax.broadcasted_iota(jnp.int32, (q, q), 0)
    col = lax.broadcasted_iota(jnp.int32, (q, q), 1)
    ltri = (col <= row).astype(F32)
    dt = _softplus(dtr + bias_row)
    a_row = -jnp.exp(alog_row)
    la = dt * a_row
    cum = _dot_sel_a(ltri, la)
    cum_t = _dot_tn_sel(la, (row <= col).astype(F32), "b")
    return dt, a_row, cum, cum_t, ltri, row, col


def _head_maps():
    di = SSD_HEADS * SSD_HEADDIM
    expand = (lax.broadcasted_iota(jnp.int32, (LANES, di), 1) // SSD_HEADDIM
              == lax.broadcasted_iota(jnp.int32, (LANES, di), 0)).astype(F32)
    collapse = (lax.broadcasted_iota(jnp.int32, (di, LANES), 0) // SSD_HEADDIM
                == lax.broadcasted_iota(jnp.int32, (di, LANES), 1)).astype(F32)
    return expand, collapse


def _ssd_decay(cum, cum_t, h, causal):
    seg = cum[:, h:h + 1] - cum_t[h:h + 1, :]
    return jnp.where(causal, jnp.exp(jnp.where(causal, seg, 0.0)), 0.0)


def _pair_block_diag(x2):
    low = lax.broadcasted_iota(jnp.int32, x2.shape, 1) < SSD_HEADDIM
    zero = jnp.zeros_like(x2)
    return jnp.concatenate([jnp.where(low, x2, zero), jnp.where(low, zero, x2)], axis=0)


def ssd_fwd(xbc_raw, conv_w8, conv_b, dtr, z, bias_row, alog_row, d_row, norm_g, *, name, comm=None):
    t = xbc_raw.shape[0]
    q, p, n = SSD_CHUNK, SSD_HEADDIM, SSD_STATE
    nc = t // q
    di = SSD_HEADS * p
    gw = di // SSD_GROUPS

    def body(raw_ref, cw_ref, cb_ref, dtr_ref, z_ref, bias_ref, alog_ref, d_ref, g_ref,
             ya_ref, yssd_ref, st_ref, state, ext, xbc_s):
        ci = pl.program_id(0)

        @pl.when(ci == 0)
        def _():
            state[...] = jnp.zeros_like(state)
            ext[0:SUBLANES, :] = jnp.zeros((SUBLANES, CONV_DIM), F32)

        ext[SUBLANES:SUBLANES + q, :] = raw_ref[...].astype(F32)
        xc = _conv_taps(_conv_windows(ext, q), cw_ref) + cb_ref[...]
        ext[0:SUBLANES, :] = ext[q:q + SUBLANES, :]
        xbc_s[...] = xc * _sigmoid(xc)
        st_ref[0] = state[...]
        dt, _, cum, cum_t, _, row, col = _ssd_chunk_terms(dtr_ref[...], bias_ref[...], alog_ref[...])
        causal = row >= col
        expand, _ = _head_maps()
        dt_full = _dot_sel_b(dt, expand)
        cum_full = _dot_sel_b(cum, expand)
        last_full = cum_full[q - 1:q, :]
        xs = xbc_s[:, 0:di]
        xdt = xs * dt_full
        xd = (xdt * jnp.exp(last_full - cum_full)).astype(BF16)
        xdt_b = xdt.astype(BF16)
        e_full = jnp.exp(cum_full)
        eend_full = jnp.exp(last_full)
        for g in range(SSD_GROUPS):
            slab = slice(g * gw, (g + 1) * gw)
            bg = xbc_s[:, di + g * n:di + (g + 1) * n].astype(BF16)
            cg = xbc_s[:, di + (SSD_GROUPS + g) * n:di + (SSD_GROUPS + g + 1) * n].astype(BF16)
            cb = _dot_nt(cg, bg)
            st_g = state[:, slab]
            pairs = []
            for pr in range(SSD_HPG // 2):
                h0 = g * SSD_HPG + 2 * pr
                m0 = (cb * _ssd_decay(cum, cum_t, h0, causal)).astype(BF16)
                m1 = (cb * _ssd_decay(cum, cum_t, h0 + 1, causal)).astype(BF16)
                pairs.append(_dot(jnp.concatenate([m0, m1], axis=1), _pair_block_diag(xdt_b[:, h0 * p:(h0 + 2) * p])))
            y = jnp.concatenate(pairs, axis=1) + e_full[:, slab] * _dot(cg, st_g.astype(BF16))
            yssd_ref[:, slab] = y + d_ref[:, slab] * xs[:, slab]
            state[:, slab] = eend_full[:, slab] * st_g + _dot_tn(bg, xd[:, slab])
        zv = z_ref[...].astype(F32)
        ya1 = yssd_ref[...] * (zv * _sigmoid(zv))
        for g in range(SSD_GROUPS):
            gsl = slice(g * gw, (g + 1) * gw)
            sl = ya1[:, gsl]
            rg = lax.rsqrt(jnp.mean(sl * sl, axis=-1, keepdims=True) + EPS)
            ya_ref[:, gsl] = (sl * rg * g_ref[:, gsl]).astype(BF16)

    blk = lambda w, j: pl.BlockSpec((q, w), lambda c: (c, j))
    return _tiled_call(
        body, name=name, grid=(nc,),
        in_specs=[blk(CONV_DIM, 0), _full((SUBLANES, CONV_DIM)), _full((1, CONV_DIM)), blk(LANES, 0), blk(di, 0),
                  _full((1, LANES)), _full((1, LANES)), _full((1, di)), _full((1, di))],
        out_specs=[blk(di, 0), blk(di, 0), pl.BlockSpec((1, n, di), lambda c: (c, 0, 0))],
        out_shape=[jax.ShapeDtypeStruct((t, di), BF16), jax.ShapeDtypeStruct((t, di), F32),
                   jax.ShapeDtypeStruct((nc, n, di), F32)],
        scratch_shapes=[pltpu.VMEM((n, di), F32), pltpu.VMEM((SUBLANES + q, CONV_DIM), F32), pltpu.VMEM((q, CONV_DIM), F32)],
        operands=(xbc_raw, conv_w8, conv_b.reshape(1, CONV_DIM), dtr, z, bias_row, alog_row, d_row,
                  norm_g.reshape(1, di)), comm=comm)


def ssd_bwd(dya, yssd, z, xbc_raw, conv_w8, conv_b, dtr, states, bias_row, alog_row, d_row, norm_g, *, name, comm=None):
    t = xbc_raw.shape[0]
    q, p, n = SSD_CHUNK, SSD_HEADDIM, SSD_STATE
    nc = t // q
    di = SSD_HEADS * p
    gw = di // SSD_GROUPS
    per = q // (2 * SUBLANES)

    def body(dya_ref, yssd_ref, z_ref, raw_ref, prev_ref, cw_ref, cb_ref, dtr_ref, st_ref, bias_ref, alog_ref, d_ref,
             g_ref, dzdt_ref, draw_ref, dng_ref, dd_ref, dalog_ref, dbias_ref, dcb_ref, dcw_ref,
             dstate, dyssd, xbc_s, dxbc_ref, ext, aft):
        ci = pl.program_id(0)

        @pl.when(ci == 0)
        def _():
            dstate[...] = jnp.zeros_like(dstate)
            aft[q:q + SUBLANES, :] = jnp.zeros((SUBLANES, CONV_DIM), F32)
            for r in (dng_ref, dd_ref, dalog_ref, dbias_ref, dcb_ref, dcw_ref):
                r[...] = jnp.zeros_like(r)

        ext[0:SUBLANES, :] = jnp.where(ci == nc - 1, 0.0, prev_ref[SUBLANES:2 * SUBLANES, :].astype(F32))
        ext[SUBLANES:SUBLANES + q, :] = raw_ref[...].astype(F32)
        windows = _conv_windows(ext, q)
        xc = _conv_taps(windows, cw_ref) + cb_ref[...]
        sig_c = _sigmoid(xc)
        xbc_s[...] = xc * sig_c

        zv = z_ref[...].astype(F32)
        sz = _sigmoid(zv)
        silu_z = zv * sz
        yv = yssd_ref[...]
        ya1 = yv * silu_z
        dyav = dya_ref[...]
        for g in range(SSD_GROUPS):
            gsl = slice(g * gw, (g + 1) * gw)
            sl = ya1[:, gsl]
            rg = lax.rsqrt(jnp.mean(sl * sl, axis=-1, keepdims=True) + EPS)
            ya2 = sl * rg
            dy_g = dyav[:, gsl]
            dng_ref[:, gsl] += jnp.sum(dy_g * ya2, axis=0, keepdims=True)
            dya2 = dy_g * g_ref[:, gsl]
            dya1 = rg * (dya2 - ya2 * jnp.mean(dya2 * ya2, axis=-1, keepdims=True))
            dyssd[:, gsl] = dya1 * silu_z[:, gsl]
            dzdt_ref[:, gsl] = (dya1 * yv[:, gsl] * (sz[:, gsl] * (1.0 + zv[:, gsl] * (1.0 - sz[:, gsl])))).astype(BF16)

        dtr_v = dtr_ref[...]
        dt, a_row, cum, cum_t, ltri, row, col = _ssd_chunk_terms(dtr_v, bias_ref[...], alog_ref[...])
        causal = row >= col
        expand, collapse = _head_maps()
        lane = lax.broadcasted_iota(jnp.int32, (1, LANES), 1)
        sub = lax.broadcasted_iota(jnp.int32, (LANES, 1), 0)
        is_last = lax.broadcasted_iota(jnp.int32, (q, 1), 0) == q - 1
        low = lax.broadcasted_iota(jnp.int32, (q, 2 * p), 1) < p
        dt_full = _dot_sel_b(dt, expand)
        cum_full = _dot_sel_b(cum, expand)
        last_full = cum_full[q - 1:q, :]
        e_full = jnp.exp(cum_full)
        eend_full = jnp.exp(last_full)
        dend_full = jnp.exp(last_full - cum_full)
        xs = xbc_s[:, 0:di]
        xdt = xs * dt_full
        xdt_b = xdt.astype(BF16)
        xd_b = (xdt * dend_full).astype(BF16)
        dy_all = dyssd[...]
        dy_b = dy_all.astype(BF16)
        dg_b = (dy_all * e_full).astype(BF16)
        dcum_mat = jnp.zeros((q, LANES), F32)
        rmat = jnp.zeros((LANES, q), F32)
        yoff_prod, bds_list, dx_list, dlast_parts = [], [], [], []
        for g in range(SSD_GROUPS):
            slab = slice(g * gw, (g + 1) * gw)
            bg = xbc_s[:, di + g * n:di + (g + 1) * n].astype(BF16)
            cg = xbc_s[:, di + (SSD_GROUPS + g) * n:di + (SSD_GROUPS + g + 1) * n].astype(BF16)
            cb = _dot_nt(cg, bg)
            st_g = st_ref[0, :, slab]
            st_b = st_g.astype(BF16)
            dsn = dstate[:, slab]
            dsn_b = dsn.astype(BF16)
            yoff_prod.append(dy_all[:, slab] * e_full[:, slab] * _dot(cg, st_b))
            dcg = _dot_nt(dg_b[:, slab], st_b)
            dstate[:, slab] = _dot_tn(cg, dg_b[:, slab]) + eend_full[:, slab] * dsn
            bds_list.append(_dot(bg, dsn_b))
            dbg = _dot_nt(xd_b[:, slab], dsn_b)
            dlast_parts.append(jnp.sum(dsn * st_g, axis=0, keepdims=True))
            dcb = jnp.zeros((q, q), F32)
            dx_pairs = []
            for pr in range(SSD_HPG // 2):
                h0 = g * SSD_HPG + 2 * pr
                ps = slice(h0 * p, (h0 + 2) * p)
                l0 = _ssd_decay(cum, cum_t, h0, causal)
                l1 = _ssd_decay(cum, cum_t, h0 + 1, causal)
                m0, m1 = cb * l0, cb * l1
                dyp = dy_b[:, ps]
                zero = jnp.zeros_like(dyp)
                dy0, dy1 = jnp.where(low, dyp, zero), jnp.where(low, zero, dyp)
                dm0 = _dot_nt(dy0, xdt_b[:, ps])
                dm1 = _dot_nt(dy1, xdt_b[:, ps])
                w0, w1 = dm0 * m0, dm1 * m1
                dcum_mat = (dcum_mat + jnp.sum(w0, axis=1, keepdims=True) * (lane == h0).astype(F32)
                            + jnp.sum(w1, axis=1, keepdims=True) * (lane == h0 + 1).astype(F32))
                rmat = (rmat + jnp.where(sub == h0, jnp.sum(w0, axis=0, keepdims=True), 0.0)
                        + jnp.where(sub == h0 + 1, jnp.sum(w1, axis=0, keepdims=True), 0.0))
                dcb = dcb + dm0 * l0 + dm1 * l1
                dx_pairs.append(_dot_tn(jnp.concatenate([m0.astype(BF16), m1.astype(BF16)], axis=0),
                                        jnp.concatenate([dy0, dy1], axis=0)))
            dx_list.append(jnp.concatenate(dx_pairs, axis=1))
            dcb_b = dcb.astype(BF16)
            dxbc_ref[:, di + g * n:di + (g + 1) * n] = dbg + _dot_tn(dcb_b, cg)
            dxbc_ref[:, di + SSD_GROUPS * n + g * n:di + SSD_GROUPS * n + (g + 1) * n] = dcg + _dot(dcb_b, bg)
        bds_all = jnp.concatenate(bds_list, axis=1)
        dx_all = jnp.concatenate(dx_list, axis=1) + dend_full * bds_all
        last_row = cum[q - 1:q, :]
        qv = _dot_sel_b(xdt * bds_all, collapse) * jnp.exp(last_row - cum)
        dcum_mat = dcum_mat + _dot_sel_b(jnp.concatenate(yoff_prod, axis=1), collapse) - qv
        dlast_full = jnp.broadcast_to(jnp.concatenate(dlast_parts, axis=1), (SUBLANES, di))
        dlast_row = _dot_sel_b(dlast_full, collapse)[0:1, :] * jnp.exp(last_row) + jnp.sum(qv, axis=0, keepdims=True)
        dcum_mat = dcum_mat + jnp.where(is_last, dlast_row, 0.0)
        dla = _dot_tn_sel(ltri, dcum_mat, "a") - _dot_nt_sel_a((row <= col).astype(F32), rmat)
        ddt_mat = _dot_sel_b(dx_all * xs, collapse) + dla * a_row
        dxbc_ref[:, 0:di] = d_ref[...] * dy_all + dx_all * dt_full
        ddtr = ddt_mat * _sigmoid(dtr_v + bias_ref[...])
        dzdt_ref[:, di:di + LANES] = ddtr.astype(BF16)
        dd_full = jnp.broadcast_to(jnp.sum(dy_all * xs, axis=0, keepdims=True), (SUBLANES, di))
        dd_ref[...] += _dot_sel_b(dd_full, collapse)[0:1, :]
        dalog_ref[...] += jnp.sum(dla * dt, axis=0, keepdims=True) * a_row
        dbias_ref[...] += jnp.sum(ddtr, axis=0, keepdims=True)

        dxc = dxbc_ref[...] * (sig_c * (1.0 + xc * (1.0 - sig_c)))
        dcb_ref[...] += jnp.sum(dxc, axis=0, keepdims=True)
        taps = [jnp.sum(dxc * windows[k], axis=0, keepdims=True) for k in range(SSD_CONV)]
        dcw_ref[...] += jnp.concatenate(taps + [jnp.zeros((SUBLANES - SSD_CONV, CONV_DIM), F32)], axis=0)
        aft[0:q, :] = dxc
        after = aft[...]
        draw = dxc * cw_ref[3:4, :]
        for j in range(1, SSD_CONV):
            draw = draw + pltpu.roll(after, q + SUBLANES - j, 0)[0:q, :] * cw_ref[3 - j:4 - j, :]
        draw_ref[...] = draw.astype(BF16)
        aft[q:q + SUBLANES, :] = dxc[0:SUBLANES, :]

    blk = lambda w, j: pl.BlockSpec((q, w), lambda c: (nc - 1 - c, j))
    row_out = lambda w: jax.ShapeDtypeStruct((1, w), F32)
    return _tiled_call(
        body, name=name, grid=(nc,),
        in_specs=[blk(di, 0), blk(di, 0), blk(di, 0), blk(CONV_DIM, 0),
                  pl.BlockSpec((2 * SUBLANES, CONV_DIM), lambda c: (jnp.maximum((nc - 1 - c) * per - 1, 0), 0)),
                  _full((SUBLANES, CONV_DIM)), _full((1, CONV_DIM)), blk(LANES, 0),
                  pl.BlockSpec((1, n, di), lambda c: (nc - 1 - c, 0, 0)),
                  _full((1, LANES)), _full((1, LANES)), _full((1, di)), _full((1, di))],
        out_specs=[blk(di + LANES, 0), blk(CONV_DIM, 0),
                   _full((1, di)), _full((1, LANES)), _full((1, LANES)), _full((1, LANES)),
                   _full((1, CONV_DIM)), _full((SUBLANES, CONV_DIM))],
        out_shape=[jax.ShapeDtypeStruct((t, di + LANES), BF16), jax.ShapeDtypeStruct((t, CONV_DIM), BF16),
                   row_out(di), row_out(LANES), row_out(LANES), row_out(LANES),
                   row_out(CONV_DIM), jax.ShapeDtypeStruct((SUBLANES, CONV_DIM), F32)],
        scratch_shapes=[pltpu.VMEM((n, di), F32), pltpu.VMEM((q, di), F32), pltpu.VMEM((q, CONV_DIM), F32),
                        pltpu.VMEM((q, CONV_DIM), F32), pltpu.VMEM((SUBLANES + q, CONV_DIM), F32),
                        pltpu.VMEM((q + SUBLANES, CONV_DIM), F32)],
        operands=(dya, yssd, z, xbc_raw, xbc_raw, conv_w8, conv_b.reshape(1, CONV_DIM), dtr, states, bias_row,
                  alog_row, d_row, norm_g.reshape(1, di)),
        comm=comm)


S5_BLOCKS = 4
S5_BC = S5_WIDTH // S5_BLOCKS
S5_BS = S5_N // S5_BLOCKS


def _s5_tables(ar, ai, reverse):
    pows = [(ar, ai)]
    for _ in range(SUBLANES - 1):
        pr, pi = pows[-1]
        pows.append((pr * ar - pi * ai, pr * ai + pi * ar))
    row = lax.broadcasted_iota(jnp.int32, (SUBLANES, ar.shape[1]), 0)
    out = []
    for k in (1, 2, 4):
        pr, pi = pows[k - 1]
        mask = (row < SUBLANES - k) if reverse else (row >= k)
        out += [jnp.where(mask, pr, 0.0), jnp.where(mask, pi, 0.0)]
    order = list(range(SUBLANES))
    if reverse:
        order = order[::-1]
    out.append(jnp.concatenate([pows[e][0] for e in order], axis=0))
    out.append(jnp.concatenate([pows[e][1] for e in order], axis=0))
    return out


def s5_interleave(a, tm):
    t, c = a.shape
    return a.reshape(t // tm, SUBLANES, tm // SUBLANES, c).transpose(0, 2, 1, 3).reshape(t, c)


def s5_deinterleave(a, tm):
    t, c = a.shape
    return a.reshape(t // tm, tm // SUBLANES, SUBLANES, c).transpose(0, 2, 1, 3).reshape(t, c)


def _s5_scan_setup(ab_ref, base, tab, seg, reverse):
    ar = ab_ref[0:1, :]
    ai = -ab_ref[1:2, :] if reverse else ab_ref[1:2, :]
    base[0:1, :] = ar
    base[1:2, :] = ai
    assert seg & (seg - 1) == 0
    pr, pi = ar, ai
    for _ in range(seg.bit_length() - 1):
        pr, pi = pr * pr - pi * pi, 2.0 * pr * pi
    for k, tv in enumerate(_s5_tables(pr, pi, reverse)):
        tab[k] = tv


def _s5_scan(s_scr, base, tab, carry, tm, reverse):
    seg = tm // SUBLANES
    width = S5_BS
    first = lax.broadcasted_iota(jnp.int32, (SUBLANES, width), 0) == (SUBLANES - 1 if reverse else 0)

    def rows(i):
        step = (seg - 1 - i) if reverse else i
        return pl.ds(pl.multiple_of(step * SUBLANES, SUBLANES), SUBLANES)

    for lc in range(S5_N // width):
        lr = slice(lc * width, (lc + 1) * width)
        li = slice(S5_N + lc * width, S5_N + (lc + 1) * width)
        ar = jnp.broadcast_to(base[0:1, lr], (SUBLANES, width))
        ai = jnp.broadcast_to(base[1:2, lr], (SUBLANES, width))

        def advance(i, x, lr=lr, li=li, ar=ar, ai=ai):
            xr, xi = x
            return ar * xr - ai * xi + s_scr[rows(i), lr], ar * xi + ai * xr + s_scr[rows(i), li]

        zero = jnp.zeros((SUBLANES, width), F32)
        fr, fi = lax.fori_loop(0, seg, advance, (zero, zero))

        tb = [tab[k, :, lr] for k in range(8)]
        for lvl, k in enumerate((1, 2, 4)):
            sh = (SUBLANES - k) if reverse else k
            pr, pi = tb[2 * lvl], tb[2 * lvl + 1]
            rr, ri = pltpu.roll(fr, sh, 0), pltpu.roll(fi, sh, 0)
            fr, fi = fr + pr * rr - pi * ri, fi + pr * ri + pi * rr
        cr, ci = carry[0:1, lr], carry[0:1, li]
        fr, fi = fr + tb[6] * cr - tb[7] * ci, fi + tb[6] * ci + tb[7] * cr
        last = 0 if reverse else SUBLANES - 1
        carry[0:1, lr] = fr[last:last + 1, :]
        carry[0:1, li] = fi[last:last + 1, :]
        sh = (SUBLANES - 1) if reverse else 1
        entering = (jnp.where(first, cr, pltpu.roll(fr, sh, 0)), jnp.where(first, ci, pltpu.roll(fi, sh, 0)))

        def rerun(i, x, lr=lr, li=li, advance=advance):
            xr, xi = advance(i, x)
            s_scr[rows(i), lr] = xr
            s_scr[rows(i), li] = xi
            return xr, xi

        lax.fori_loop(0, seg, rerun, entering)


def _s5_scratch(tm):
    return [pltpu.VMEM((tm, 2 * S5_N), F32), pltpu.VMEM((SUBLANES, S5_N), F32),
            pltpu.VMEM((8, SUBLANES, S5_N), F32), pltpu.VMEM((SUBLANES, 2 * S5_N), F32)]


def _s5_put(scr, j, val):
    scr[:, j * S5_BS:(j + 1) * S5_BS] = val[:, :S5_BS]
    scr[:, S5_N + j * S5_BS:S5_N + (j + 1) * S5_BS] = val[:, S5_BS:]


def _s5_get(scr, j):
    return jnp.concatenate([scr[:, j * S5_BS:(j + 1) * S5_BS], scr[:, S5_N + j * S5_BS:S5_N + (j + 1) * S5_BS]], axis=1)


def s5_fwd(u5, bbc, ccc, ab8, d_row, wglu, bglu_row, *, tm, name, comm=None):
    t, w = u5.shape

    def body(u_ref, bb_ref, cc_ref, ab_ref, d_ref, wg_ref, bg_ref, s_ref, ys_ref, yb_ref, s_scr, base, tab, carry):
        i = pl.program_id(0)

        @pl.when(i == 0)
        def _():
            carry[...] = jnp.zeros_like(carry)
            _s5_scan_setup(ab_ref, base, tab, tm // SUBLANES, False)

        ub = u_ref[...]
        u = ub.astype(F32)
        for j in range(S5_BLOCKS):
            uj = ub[:, j * S5_BC:(j + 1) * S5_BC]
            _s5_put(s_scr, j, _dot(uj, bb_ref[j * S5_BC:(j + 1) * S5_BC, :]))
        _s5_scan(s_scr, base, tab, carry, tm, False)
        s_ref[...] = s_scr[...]
        ys = []
        for j in range(S5_BLOCKS):
            ys.append(_dot(_s5_get(s_scr, j).astype(BF16), cc_ref[:, j * S5_BC:(j + 1) * S5_BC]))
        y = jnp.concatenate(ys, axis=1) + d_ref[...] * u
        ys_ref[...] = y
        yb1 = _gelu(y)
        tt = _dot(yb1.astype(BF16), wg_ref[...]) + bg_ref[...]
        yb_ref[...] = (yb1 * _sigmoid(tt)).astype(BF16)

    tile = lambda c: pl.BlockSpec((tm, c), lambda i: (i, 0))
    return _tiled_call(
        body, name=name, grid=(t // tm,),
        in_specs=[tile(w), _full((w, 2 * S5_BS)), _full((2 * S5_BS, w)), _full((SUBLANES, S5_N)), _full((1, w)),
                  _full((w, w)), _full((1, w))],
        out_specs=[tile(2 * S5_N), tile(w), tile(w)],
        out_shape=[jax.ShapeDtypeStruct((t, 2 * S5_N), F32), jax.ShapeDtypeStruct((t, w), F32),
                   jax.ShapeDtypeStruct((t, w), BF16)],
        scratch_shapes=_s5_scratch(tm),
        operands=(u5, bbc, ccc, ab8, d_row, wglu, bglu_row), comm=comm)


def s5_bwd(dyb, ys5, u5, s, bbc, ccc, ab8, d_row, wglu, bglu_row, *, tm, name, comm=None):
    t, w = u5.shape
    nt = t // tm
    per = tm // SUBLANES

    def body(dyb_ref, ys_ref, u_ref, s_ref, sprev_ref, bb_ref, cc_ref, ab_ref, d_ref, wg_ref, bg_ref,
             du_ref, dbb_ref, dcc_ref, dab_ref, dd_ref, dwg_ref, dbg_ref, g_scr, base, tab, carry):
        i = pl.program_id(0)

        @pl.when(i == 0)
        def _():
            carry[...] = jnp.zeros_like(carry)
            _s5_scan_setup(ab_ref, base, tab, tm // SUBLANES, True)
            for r in (dbb_ref, dcc_ref, dab_ref, dd_ref, dwg_ref, dbg_ref):
                r[...] = jnp.zeros_like(r)

        y = ys_ref[...]
        ub = u_ref[...]
        u = ub.astype(F32)
        yb1 = _gelu(y)
        yb1b = yb1.astype(BF16)
        sg = _sigmoid(_dot(yb1b, wg_ref[...]) + bg_ref[...])
        dybv = dyb_ref[...]
        dtt = dybv * yb1 * sg * (1.0 - sg)
        dttb = dtt.astype(BF16)
        dyb1 = dybv * sg + _dot_nt(dttb, wg_ref[...])
        dwg_ref[...] += _dot_tn(yb1b, dttb)
        dbg_ref[...] += jnp.sum(dtt, axis=0, keepdims=True)
        dy = dyb1 * _gelu_grad(y)
        dd_ref[...] += jnp.sum(dy * u, axis=0, keepdims=True)
        dyh = dy.astype(BF16)
        for j in range(S5_BLOCKS):
            cs = slice(j * S5_BC, (j + 1) * S5_BC)
            _s5_put(g_scr, j, _dot_nt(dyh[:, cs], cc_ref[:, cs]))
        _s5_scan(g_scr, base, tab, carry, tm, True)
        dus = []
        for j in range(S5_BLOCKS):
            cs = slice(j * S5_BC, (j + 1) * S5_BC)
            gb = _s5_get(g_scr, j).astype(BF16)
            dus.append(_dot_nt(gb, bb_ref[cs, :]))
            dbb_ref[cs, :] += _dot_tn(ub[:, cs], gb)
            dcc_ref[:, cs] += _dot_tn(_s5_get(s_ref, j).astype(BF16), dyh[:, cs])
        du_ref[...] = (jnp.concatenate(dus, axis=1) + d_ref[...] * dy).astype(BF16)
        top = lax.broadcasted_iota(jnp.int32, (SUBLANES, S5_BS), 0) == 0

        def corr(g_sl, s_sl):
            before = jnp.where(i == nt - 1, 0.0, sprev_ref[SUBLANES - 1:SUBLANES, s_sl])
            wrap = jnp.where(top, before, pltpu.roll(s_ref[tm - SUBLANES:tm, s_sl], 1, 0))
            return (jnp.sum(g_scr[SUBLANES:tm, g_sl] * s_ref[0:tm - SUBLANES, s_sl], axis=0, keepdims=True)
                    + jnp.sum(g_scr[0:SUBLANES, g_sl] * wrap, axis=0, keepdims=True))

        for j in range(S5_BLOCKS):
            lr = slice(j * S5_BS, (j + 1) * S5_BS)
            li = slice(S5_N + j * S5_BS, S5_N + (j + 1) * S5_BS)
            dab_ref[0:1, lr] += corr(lr, lr) + corr(li, li)
            dab_ref[1:2, lr] += corr(li, lr) - corr(lr, li)

    tile = lambda c: pl.BlockSpec((tm, c), lambda i: (nt - 1 - i, 0))
    acc = lambda r, c: jax.ShapeDtypeStruct((r, c), F32)
    return _tiled_call(
        body, name=name, grid=(nt,),
        in_specs=[tile(w), tile(w), tile(w), tile(2 * S5_N),
                  pl.BlockSpec((SUBLANES, 2 * S5_N), lambda i: (jnp.maximum((nt - 1 - i) * per - 1, 0), 0)),
                  _full((w, 2 * S5_BS)), _full((2 * S5_BS, w)), _full((SUBLANES, S5_N)), _full((1, w)),
                  _full((w, w)), _full((1, w))],
        out_specs=[tile(w), _full((w, 2 * S5_BS)), _full((2 * S5_BS, w)), _full((SUBLANES, S5_N)), _full((1, w)),
                   _full((w, w)), _full((1, w))],
        out_shape=[jax.ShapeDtypeStruct((t, w), BF16), acc(w, 2 * S5_BS), acc(2 * S5_BS, w), acc(SUBLANES, S5_N),
                   acc(1, w), acc(w, w), acc(1, w)],
        scratch_shapes=_s5_scratch(tm),
        operands=(dyb, ys5, u5, s, s, bbc, ccc, ab8, d_row, wglu, bglu_row), comm=comm)


def s5_discretize(a_re, a_im, log_dt, b_re, b_im, *, name):
    n = a_re.shape[0]

    def body(ar_ref, ai_ref, dt_ref, br_ref, bi_ref, ab_ref, bbr_ref, bbi_ref):
        ar, ai, dtv = ar_ref[...], ai_ref[...], jnp.exp(dt_ref[...])
        mag = jnp.exp(ar * dtv)
        abr = mag * jnp.cos(ai * dtv)
        abi = mag * jnp.sin(ai * dtv)
        den = ar * ar + ai * ai
        nr = abr - 1.0
        cr = (nr * ar + abi * ai) / den
        ci = (abi * ar - nr * ai) / den
        ab_ref[:, 0:1] = abr
        ab_ref[:, 1:2] = abi
        br, bi = br_ref[...], bi_ref[...]
        bbr_ref[...] = cr * br - ci * bi
        bbi_ref[...] = cr * bi + ci * br

    col = jax.ShapeDtypeStruct((n, 2), F32)
    mat = jax.ShapeDtypeStruct((n, S5_GROUP), F32)
    return pl.pallas_call(body, name=name, out_shape=[col, mat, mat])(a_re, a_im, log_dt, b_re, b_im)


def s5_discretize_bwd(a_re, a_im, log_dt, b_re, b_im, dab, dbb_re, dbb_im, *, name):
    n = a_re.shape[0]

    def body(ar_ref, ai_ref, dt_ref, br_ref, bi_ref, dab_ref, dbr_ref, dbi_ref, da_ref, ddt_ref, gbr_ref, gbi_ref):
        ar, ai, dtv = ar_ref[...], ai_ref[...], jnp.exp(dt_ref[...])
        mag = jnp.exp(ar * dtv)
        abr = mag * jnp.cos(ai * dtv)
        abi = mag * jnp.sin(ai * dtv)
        den = ar * ar + ai * ai
        nr = abr - 1.0
        cr = (nr * ar + abi * ai) / den
        ci = (abi * ar - nr * ai) / den
        br, bi = br_ref[...], bi_ref[...]
        dbr, dbi = dbr_ref[...], dbi_ref[...]
        gbr_ref[...] = cr * dbr + ci * dbi
        gbi_ref[...] = cr * dbi - ci * dbr
        gcr = jnp.sum(dbr * br + dbi * bi, axis=1, keepdims=True)
        gci = jnp.sum(dbi * br - dbr * bi, axis=1, keepdims=True)
        ilr, ili = ar / den, -ai / den
        gabr = dab_ref[:, 0:1] + (ilr * gcr + ili * gci)
        gabi = dab_ref[:, 1:2] + (ilr * gci - ili * gcr)
        t1r, t1i = dtv * abr, dtv * abi
        t2r, t2i = -(cr * ilr - ci * ili), -(cr * ili + ci * ilr)
        da_ref[:, 0:1] = (t1r * gabr + t1i * gabi) + (t2r * gcr + t2i * gci)
        da_ref[:, 1:2] = (t1r * gabi - t1i * gabr) + (t2r * gci - t2i * gcr)
        lr, li = ar * abr - ai * abi, ar * abi + ai * abr
        dlog = (lr * gabr + li * gabi) * dtv
        ddt_ref[...] = jnp.sum(dlog.reshape(S5_GROUPS, S5_STATE, 1), axis=1)

    col2 = jax.ShapeDtypeStruct((n, 2), F32)
    col1 = jax.ShapeDtypeStruct((S5_GROUPS, 1), F32)
    mat = jax.ShapeDtypeStruct((n, S5_GROUP), F32)
    return pl.pallas_call(body, name=name, out_shape=[col2, col1, mat, mat])(
        a_re, a_im, log_dt, b_re, b_im, dab, dbb_re, dbb_im)


def merge_fwd(ya, yb, graw, x, wb, wout, *, tm, name):
    t, d = x.shape

    def body(ya_ref, yb_ref, g_ref, x_ref, wb_ref, wo_ref, x1_ref, pa_ref, pb_ref):
        pa = _dot(ya_ref[...], wb_ref[0:d, :])
        pb = _dot(yb_ref[...], wb_ref[d:, :])
        gate = _sigmoid(g_ref[...].astype(F32))
        merged = gate[:, :d] * pa + gate[:, d:] * pb
        x1_ref[...] = x_ref[...] + _dot(merged.astype(BF16), wo_ref[...])
        pa_ref[...] = pa.astype(BF16)
        pb_ref[...] = pb.astype(BF16)

    tile = lambda c: pl.BlockSpec((tm, c), lambda i: (i, 0))
    sds = jax.ShapeDtypeStruct
    return pl.pallas_call(
        body, name=name, grid=(t // tm,),
        in_specs=[tile(d), tile(S5_WIDTH), tile(2 * d), tile(d), _resident((d + S5_WIDTH, d)), _resident((d, d))],
        out_specs=[tile(d), tile(d), tile(d)], out_shape=[sds((t, d), F32), sds((t, d), BF16), sds((t, d), BF16)],
        compiler_params=_params("parallel"),
    )(ya, yb, graw, x, wb, wout)


def merge_bwd(dx1b, graw, pa, pb, wb, wout, *, tm, name):
    t, d = pa.shape

    def body(dx_ref, g_ref, pa_ref, pb_ref, wb_ref, wo_ref,
             mg_ref, dg_ref, dpa_ref, dpb_ref, dya_ref, dyb_ref):
        dm = _dot_nt(dx_ref[...], wo_ref[...])
        gate = _sigmoid(g_ref[...].astype(F32))
        g0, g1 = gate[:, :d], gate[:, d:]
        pa, pb = pa_ref[...].astype(F32), pb_ref[...].astype(F32)
        mg_ref[...] = (g0 * pa + g1 * pb).astype(BF16)
        dg_ref[:, :d] = (dm * pa * g0 * (1.0 - g0)).astype(BF16)
        dg_ref[:, d:] = (dm * pb * g1 * (1.0 - g1)).astype(BF16)
        dpa = (dm * g0).astype(BF16)
        dpb = (dm * g1).astype(BF16)
        dpa_ref[...] = dpa
        dpb_ref[...] = dpb
        dya_ref[...] = _dot_nt(dpa, wb_ref[0:d, :])
        dyb_ref[...] = _dot_nt(dpb, wb_ref[d:, :])

    tile = lambda c: pl.BlockSpec((tm, c), lambda i: (i, 0))
    sds = jax.ShapeDtypeStruct
    return pl.pallas_call(
        body, name=name, grid=(t // tm,),
        in_specs=[tile(d), tile(2 * d), tile(d), tile(d), _resident((d + S5_WIDTH, d)), _resident((d, d))],
        out_specs=[tile(d), tile(2 * d), tile(d), tile(d), tile(d), tile(S5_WIDTH)],
        out_shape=[sds((t, d), BF16), sds((t, 2 * d), BF16), sds((t, d), BF16), sds((t, d), BF16),
                   sds((t, d), F32), sds((t, S5_WIDTH), F32)],
        compiler_params=_params("parallel"),
    )(dx1b, graw, pa, pb, wb, wout)


def mlp_out_loss(a1, x1, w2, gf, target, *, tm, name):
    t, d = x1.shape
    f = a1.shape[1]

    def body(a_ref, x_ref, w_ref, g_ref, tg_ref, dx_ref, dxb_ref, loss_ref, dg_ref):
        i = pl.program_id(0)
        av = jnp.maximum(a_ref[...].astype(F32), 0.0)
        x2 = x_ref[...] + _dot((av * av).astype(BF16), w_ref[...])
        r = lax.rsqrt(jnp.mean(x2 * x2, axis=-1, keepdims=True) + EPS)
        xn = x2 * r
        err = xn * g_ref[...] - tg_ref[...]
        dyf = err * (1.0 / d)
        dxn = dyf * g_ref[...]
        dx = r * (dxn - xn * jnp.mean(dxn * xn, axis=-1, keepdims=True))
        dx_ref[...] = dx
        dxb_ref[...] = dx.astype(BF16)

        @pl.when(i == 0)
        def _():
            loss_ref[...] = jnp.zeros_like(loss_ref)
            dg_ref[...] = jnp.zeros_like(dg_ref)

        loss_ref[...] += jnp.sum(err * err) * (0.5 / d)
        dg_ref[...] += jnp.sum(dyf * xn, axis=0, keepdims=True)

    tile = lambda c: pl.BlockSpec((tm, c), lambda i: (i, 0))
    sds = jax.ShapeDtypeStruct
    return pl.pallas_call(
        body, name=name, grid=(t // tm,),
        in_specs=[tile(f), tile(d), _resident((f, d)), _full((1, d)), tile(d)],
        out_specs=[tile(d), tile(d), _full((1, LANES)), _full((1, d))],
        out_shape=[sds((t, d), F32), sds((t, d), BF16), sds((1, LANES), F32), sds((1, d), F32)],
        compiler_params=_params("arbitrary"),
    )(a1, x1, w2, gf.reshape(1, d), target)


def mlp_bwd_act(dx2b, a1, w2, *, tm, name):
    t, f = a1.shape
    d = dx2b.shape[1]

    def body(dx_ref, a_ref, w_ref, o_ref):
        dact = _dot_nt(dx_ref[...], w_ref[...])
        o_ref[...] = (dact * 2.0 * jnp.maximum(a_ref[...].astype(F32), 0.0)).astype(BF16)

    tile = lambda c: pl.BlockSpec((tm, c), lambda i: (i, 0))
    return pl.pallas_call(
        body, name=name, grid=(t // tm,),
        in_specs=[tile(d), tile(f), _resident((f, d))], out_specs=tile(f),
        out_shape=jax.ShapeDtypeStruct((t, f), BF16),
        compiler_params=_params("parallel"),
    )(dx2b, a1, w2)


ADAM_TILE_ELEMS = 128 * 1024


def _row_tile(rows, cap):
    if rows <= cap:
        return rows
    best = None
    for c in range(16, cap + 1, 16):
        if rows % c == 0:
            best = c
    assert best is not None, rows
    return best


def _device_order_sum(parts_ref):
    total = parts_ref[0].astype(F32)
    for k in range(1, N_DEV):
        total = total + parts_ref[k].astype(F32)
    return total


def _adamw_math(w, m, v, gparts_ref):
    g = _device_order_sum(gparts_ref)
    mn = ADAM_B1 * m + (1.0 - ADAM_B1) * g
    vn = ADAM_B2 * v + (1.0 - ADAM_B2) * (g * g)
    m_hat = mn / (1.0 - ADAM_B1 ** ADAM_STEP)
    v_hat = vn / (1.0 - ADAM_B2 ** ADAM_STEP)
    return g, -ADAM_LR * (m_hat / (jnp.sqrt(v_hat) + ADAM_EPS) + ADAM_WD * w), mn, vn


def adamw(w, m, v, gparts, *, name):
    rows, cols = w.shape
    tr = _row_tile(rows, max(16, ADAM_TILE_ELEMS // cols))

    def body(w_ref, m_ref, v_ref, g_ref, go_ref, d_ref, mo_ref, vo_ref):
        go_ref[...], d_ref[...], mo_ref[...], vo_ref[...] = _adamw_math(w_ref[...], m_ref[...], v_ref[...], g_ref)

    tile = pl.BlockSpec((tr, cols), lambda i: (i, 0))
    out = jax.ShapeDtypeStruct((rows, cols), F32)
    return pl.pallas_call(
        body, name=name, grid=(rows // tr,),
        in_specs=[tile, tile, tile, pl.BlockSpec((N_DEV, tr, cols), lambda i: (0, i, 0))],
        out_specs=[tile, tile, tile, tile], out_shape=[out, out, out, out],
        compiler_params=_params("parallel"),
    )(w, m, v, gparts)


def all_gather_arrays(blocks, *, name):
    na = len(blocks)

    def body(*refs):
        x_refs, out_refs = refs[:na], refs[na:2 * na]
        send_sems, recv_sems, local_sems = refs[2 * na:]
        x, y, c, _ = _place()
        me, sibling = (x, y, c), (x, y, 1 - c)
        chips = [(1 - x, y), (x, 1 - y), (1 - x, 1 - y)]

        def copy(a, k, blk, to, own=False):
            px, py, pc = blk
            dst = out_refs[a].at[4 * px + 2 * py + pc]
            return pltpu.make_async_remote_copy(
                src_ref=x_refs[a] if own else dst, dst_ref=dst,
                send_sem=send_sems.at[7 * a + k], recv_sem=recv_sems.at[7 * a + k], device_id=to, device_id_type=_MESH)

        mine = [pltpu.make_async_copy(x_refs[a], out_refs[a].at[4 * x + 2 * y + c], local_sems.at[a]) for a in range(na)]
        for cp in mine:
            cp.start()
        sent = []
        for a in range(na):
            first = [copy(a, 0, me, sibling, own=True)]
            first += [copy(a, 1 + j, me, (*chip, c), own=True) for j, chip in enumerate(chips)]
            for cp in first:
                cp.start()
            sent += first
        for a in range(na):
            for j, chip in enumerate(chips):
                copy(a, 1 + j, (*chip, c), me).wait_recv()
                fwd = copy(a, 4 + j, (*chip, c), sibling)
                fwd.start()
                sent.append(fwd)
        for a in range(na):
            copy(a, 0, sibling, me).wait_recv()
            for j, chip in enumerate(chips):
                copy(a, 4 + j, (*chip, 1 - c), me).wait_recv()
        for cp in sent:
            cp.wait_send()
        for cp in mine:
            cp.wait()

    any_spec = pl.BlockSpec(memory_space=pl.ANY)
    return pl.pallas_call(
        body, name=name, in_specs=[any_spec] * na, out_specs=[any_spec] * na,
        out_shape=[jax.ShapeDtypeStruct((N_DEV,) + b.shape, b.dtype) for b in blocks],
        scratch_shapes=[pltpu.SemaphoreType.DMA((7 * na,)), pltpu.SemaphoreType.DMA((7 * na,)),
                        pltpu.SemaphoreType.DMA((na,))],
        compiler_params=pltpu.CompilerParams(has_side_effects=True),
    )(*blocks)


def all_to_all(comm, *, name):
    na = len(comm)
    kinds = [k for k, _ in comm]

    def body(*refs):
        local, remote = _comm_copies(kinds, refs[:na], refs[na:2 * na], *refs[2 * na:])
        _comm_start(local, remote)
        _comm_wait(local, remote)

    any_spec = pl.BlockSpec(memory_space=pl.ANY)
    return pl.pallas_call(
        body, name=name, in_specs=[any_spec] * na, out_specs=[any_spec] * na,
        out_shape=_comm_out_shapes(comm),
        scratch_shapes=[pltpu.SemaphoreType.DMA((na * N_REL,)), pltpu.SemaphoreType.DMA((na * N_REL,)),
                        pltpu.SemaphoreType.DMA((na,))],
        compiler_params=pltpu.CompilerParams(has_side_effects=True),
    )(*[a for _, a in comm])


def adamw_whole(ws, ms, vs, gparts, *, name, totals=()):
    n, nt = len(ws), len(totals)

    def body(*refs):
        w_refs, m_refs, v_refs, g_refs = refs[:n], refs[n:2 * n], refs[2 * n:3 * n], refs[3 * n:4 * n]
        t_refs, refs = refs[4 * n:4 * n + nt], refs[4 * n + nt:]
        outs, t_outs = refs[:4 * n], refs[4 * n:]
        for k in range(n):
            g, delta, mn, vn = _adamw_math(w_refs[k][...], m_refs[k][...], v_refs[k][...], g_refs[k])
            outs[k][...] = g
            outs[n + k][...] = delta
            outs[2 * n + k][...] = mn
            outs[3 * n + k][...] = vn
        for t_ref, o_ref in zip(t_refs, t_outs):
            o_ref[...] = _device_order_sum(t_ref)

    shapes = [jax.ShapeDtypeStruct(w.shape, F32) for w in ws]
    res = pl.pallas_call(
        body, name=name, out_shape=shapes * 4 + [jax.ShapeDtypeStruct(t.shape[1:], F32) for t in totals],
        compiler_params=pltpu.CompilerParams(vmem_limit_bytes=VMEM_LIMIT),
    )(*ws, *ms, *vs, *gparts, *totals)
    return [res[j * n:(j + 1) * n] for j in range(4)], res[4 * n:]


def _lane_row(v):
    return jnp.pad(v.astype(F32), (0, LANES - v.shape[0])).reshape(1, LANES)


def _block_diag_b(bb):
    eye = jnp.eye(SUBLANES, dtype=F32)
    bt = bb.reshape(S5_BLOCKS, 8, S5_STATE, S5_GROUP).transpose(0, 1, 3, 2)
    return (bt[:, :, :, None, :] * eye[None, :, None, :, None]).reshape(S5_WIDTH, S5_BS)


def _block_diag_c(cc):
    eye = jnp.eye(SUBLANES, dtype=F32)
    ct = cc.reshape(S5_BLOCKS, 8, S5_GROUP, S5_STATE).transpose(3, 0, 1, 2)
    return (ct[None, :, :, :, :] * eye[:, None, None, :, None]).reshape(S5_BS, S5_WIDTH)


def _diag_blocks_b(m):
    eye = jnp.eye(SUBLANES, dtype=F32)
    m5 = m.reshape(S5_BLOCKS, 8, S5_GROUP, 8, S5_STATE)
    return jnp.sum(m5 * eye[None, :, None, :, None], axis=3).transpose(0, 1, 3, 2).reshape(S5_GROUPS, S5_STATE, S5_GROUP)


def _diag_blocks_c(m):
    eye = jnp.eye(SUBLANES, dtype=F32)
    m5 = m.reshape(8, S5_STATE, S5_BLOCKS, 8, S5_GROUP)
    return jnp.sum(m5 * eye[:, None, None, :, None], axis=0).transpose(1, 2, 3, 0).reshape(S5_GROUPS, S5_GROUP, S5_STATE)


def train_step(x, target, p, m, v):
    t = x.shape[0]
    tm = min(256, t)
    tb = min(512, t)
    ts = min(512, t)
    bf = lambda n: p[n].astype(BF16)
    slots = lambda n, a: a.reshape((N_DEV,) + _shard_shape(n))
    updated = {}

    def update(names, got):
        for n, parts in zip(names, got):
            updated[n] = adamw(p[n], m[n], v[n], parts, name=f"adamw_{n}")

    st_in, st_conv = all_gather_arrays([bf("w_in"), p["conv_w"]], name="gather_first")
    win_p = jnp.concatenate([st_in[k][:, a:b] for seg in IN_PADDED_ORDER for k, a, b in _shard_pieces(*IN_SEGMENTS[seg])]
                            + [jnp.zeros((D_MODEL, LANES - SSD_HEADS), BF16)], axis=1)
    conv_w8 = jnp.pad(_join_shards("conv_w", st_conv), ((0, SUBLANES - SSD_CONV), (0, 0)))
    (h, xbc_raw, graw, u5, z, dtr), (st_branch, st_out, st_glu) = rms_matmul(
        x, p["norm_mix_g"], win_p, IN_SPLITS, tm=tb, name="in_proj", out_dtypes=(BF16, BF16, BF16, BF16, F32),
        comm=gathers([bf("w_branch"), bf("w_out"), bf("s5_glu_w")]))
    w_branch = st_branch.reshape(D_MODEL + S5_WIDTH, D_MODEL)
    wout, wglu = st_out.reshape(D_MODEL, D_MODEL), st_glu.reshape(S5_WIDTH, S5_WIDTH)
    bias_row, alog_row = _lane_row(p["dt_bias"]), _lane_row(p["a_log"])
    d_row = jnp.repeat(p["d_ssd"], SSD_HEADDIM).reshape(1, SSD_HEADS * SSD_HEADDIM)
    (ya, yssd, states), (w1,) = ssd_fwd(xbc_raw, conv_w8, p["conv_b"], dtr, z, bias_row, alog_row, d_row,
                                        p["ssd_norm_g"], name="ssd_fwd", comm=gathers([bf("w_mlp_in")]))

    n = S5_N
    a_re_c, a_im_c = p["s5_a_re"].reshape(n, 1), p["s5_a_im"].reshape(n, 1)
    dt_c = jnp.repeat(p["s5_log_dt"], S5_STATE).reshape(n, 1)
    b_re_m, b_im_m = p["s5_b_re"].reshape(n, S5_GROUP), p["s5_b_im"].reshape(n, S5_GROUP)
    ab, bb_re, bb_im = s5_discretize(a_re_c, a_im_c, dt_c, b_re_m, b_im_m, name="s5_disc")
    ab8 = jnp.pad(ab.T, ((0, SUBLANES - 2), (0, 0)))
    bbc = jnp.concatenate([_block_diag_b(bb_re.reshape(S5_GROUPS, S5_STATE, S5_GROUP)),
                           _block_diag_b(bb_im.reshape(S5_GROUPS, S5_STATE, S5_GROUP))], axis=1).astype(BF16)
    ccc = jnp.concatenate([_block_diag_c(p["s5_c_re"]), -_block_diag_c(p["s5_c_im"])], axis=0).astype(BF16)
    s5d_row = p["s5_d"].reshape(1, S5_WIDTH)
    bglu_row = p["s5_glu_b"].reshape(1, S5_WIDTH)
    u5 = s5_interleave(u5, ts)
    (s, ys5, yb), (st_w2,) = s5_fwd(u5, bbc, ccc, ab8, s5d_row, wglu, bglu_row, tm=ts, name="s5_fwd",
                                    comm=gathers([bf("w_mlp_out")]))
    yb = s5_deinterleave(yb, ts)
    w2 = st_w2.reshape(D_FF, D_MODEL)

    x1, pa, pb = merge_fwd(ya, yb, graw, x, w_branch, wout, tm=tb, name="merge_fwd")
    (h2, a1), _ = rms_matmul(x1, p["norm_mlp_g"], w1, (D_FF,), tm=tb, name="mlp_in", out_dtypes=(BF16,))
    dx2, dx2b, loss_row, dgf = mlp_out_loss(a1, x1, w2, p["norm_final_g"], target, tm=tb, name="mlp_out_loss")

    g = {}
    g["norm_final_g"] = dgf[0]
    da1 = mlp_bwd_act(dx2b, a1, w2, tm=tb, name="mlp_bwd_act")
    dw2 = slots("w_mlp_out", matmul_tn(a1, dx2b, out_dtype=BF16, a_relu2=True, name="dw_mlp_out"))
    dw1 = matmul_tn(h2, da1, out_dtype=BF16, col_shards=N_DEV, name="dw_mlp_in")
    (dx1, dx1b, dgm), _ = matmul_nt_rms_bwd([da1], w1, x1, p["norm_mlp_g"], dx2, tm=tb, name="mlp_in_bwd")
    g["norm_mlp_g"] = dgm[0]
    merged, dgraw, dpa, dpb, dya, dyb = merge_bwd(dx1b, graw, pa, pb, w_branch, wout, tm=tm, name="merge_bwd")
    dwo = slots("w_out", matmul_tn(merged, dx1b, out_dtype=BF16, name="dw_out"))
    dwb = slots("w_branch", jnp.concatenate([matmul_tn(ya, dpa, out_dtype=BF16, name="dw_branch_a"),
                                             matmul_tn(yb, dpb, out_dtype=BF16, name="dw_branch_b")], axis=0))
    dgate_w = matmul_tn(h, dgraw, out_dtype=BF16, name="dw_in_gates")
    width = D_IN_PROJ // N_DEV
    early_from = -(-IN_SEGMENTS["gates"][0] // width)
    gate_lo = IN_SEGMENTS["gates"][0]
    dw_in_early = jnp.stack([dgate_w[:, k * width - gate_lo:(k + 1) * width - gate_lo] for k in range(early_from, N_DEV)])

    (du5, dbbc, dccc, dab, dd5, dwglu, dbglu), got = s5_bwd(
        s5_interleave(dyb, ts), ys5, u5, s, bbc, ccc, ab8, s5d_row, wglu, bglu_row, tm=ts, name="s5_bwd",
        comm=exchanges([dw2, dw1]))
    du5 = s5_deinterleave(du5, ts)
    update(["w_mlp_out", "w_mlp_in"], got)
    g["s5_glu_b"], g["s5_d"] = dbglu[0], dd5[0]
    g["s5_c_re"] = _diag_blocks_c(dccc[:S5_BS])
    g["s5_c_im"] = -_diag_blocks_c(dccc[S5_BS:])
    dbb_re = _diag_blocks_b(dbbc[:, :S5_BS]).reshape(n, S5_GROUP)
    dbb_im = _diag_blocks_b(dbbc[:, S5_BS:]).reshape(n, S5_GROUP)
    da, dlogdt, gb_re, gb_im = s5_discretize_bwd(a_re_c, a_im_c, dt_c, b_re_m, b_im_m, dab[:2].T, dbb_re, dbb_im,
                                                  name="s5_disc_bwd")
    g["s5_a_re"] = da[:, 0].reshape(S5_GROUPS, S5_STATE)
    g["s5_a_im"] = da[:, 1].reshape(S5_GROUPS, S5_STATE)
    g["s5_log_dt"] = dlogdt[:, 0]
    g["s5_b_re"] = gb_re.reshape(S5_GROUPS, S5_STATE, S5_GROUP)
    g["s5_b_im"] = gb_im.reshape(S5_GROUPS, S5_STATE, S5_GROUP)

    def update_small(names, got, name, totals=()):
        (gs, ds, nm, nv), sums = adamw_whole([p[n] for n in names], [m[n] for n in names], [v[n] for n in names], got,
                                             name=name, totals=totals)
        for k, n in enumerate(names):
            updated[n] = (gs[k], ds[k], nm[k], nv[k])
        return sums

    wide = ["s5_b_re", "s5_b_im"]
    late = ["ssd_norm_g", "d_ssd", "a_log", "dt_bias", "conv_b"]
    early = [n for n in SMALL_ORDER if n not in wide + late and n != "norm_mix_g"]
    (dzdt, dxbc_raw, dng, dd_row, dalog_row, dbias_row, dconv_b, dconv_w8), got = ssd_bwd(
        dya, yssd, z, xbc_raw, conv_w8, p["conv_b"], dtr, states, bias_row, alog_row, d_row, p["ssd_norm_g"],
        name="ssd_bwd",
        comm=exchanges([dwo, dwb, slots("s5_glu_w", dwglu.astype(BF16))])
        + exchanges_to(early_from, N_DEV, [dw_in_early])
        + gathers([g[n] for n in wide + early] + [loss_row]))
    update(["w_out", "w_branch", "s5_glu_w"], got[:3])
    got_w_in_early = got[3]
    got = got[4:]
    update_small(wide, got[:len(wide)], "adamw_s5_b")
    loss_sum, = update_small(early, got[len(wide):-1], "adamw_small_early", totals=[got[-1]])
    g["ssd_norm_g"] = dng[0]
    g["d_ssd"], g["a_log"], g["dt_bias"] = dd_row[0, :SSD_HEADS], dalog_row[0, :SSD_HEADS], dbias_row[0, :SSD_HEADS]
    g["conv_b"] = dconv_b[0]
    dconv = dconv_w8[:SSD_CONV].reshape(SSD_CONV, N_DEV, CONV_DIM // N_DEV).transpose(1, 0, 2)
    dproj = [dxbc_raw, dgraw, du5, dzdt]
    dxbc_w, du5_w, dzdt_w = [matmul_tn(h, piece, out_dtype=BF16, name=f"dw_in_{k}")
                             for k, piece in ((0, dxbc_raw), (2, du5), (3, dzdt))]
    by_segment = {"z": dzdt_w[:, :D_MODEL], "xbc": dxbc_w, "dt": dzdt_w[:, D_MODEL:], "u5": du5_w, "gates": dgate_w}
    dw_in_late = jnp.stack([jnp.concatenate(
        [by_segment[seg][:, max(k * width, lo) - lo:min((k + 1) * width, hi) - lo]
         for seg, (lo, hi) in IN_SEGMENTS.items() if max(k * width, lo) < min((k + 1) * width, hi)], axis=1)
        for k in range(early_from)])

    (grad_x, dgx), got = matmul_nt_rms_bwd(
        dproj, win_p, x, p["norm_mix_g"], dx1, tm=tm, name="in_proj_bwd", bf16_copy=False,
        comm=exchanges_to(0, early_from, [dw_in_late]) + exchanges([dconv]) + gathers([g[n] for n in late]))
    _, _, _, me = _place()
    got_w_in = jnp.where(me >= early_from, got_w_in_early, got[0])
    update(["w_in", "conv_w"], [got_w_in, got[1]])
    update_small(late, got[2:], "adamw_small_late")
    update_small(["norm_mix_g"], all_to_all(gathers([dgx[0]]), name="gather_last"), "adamw_small_last")
    return loss_sum[0, 0], grad_x, updated


WEIGHT_ORDER = ["norm_mix_g", "w_in", "conv_w", "conv_b", "dt_bias", "a_log", "d_ssd", "ssd_norm_g", "s5_a_re",
                "s5_a_im", "s5_log_dt", "s5_b_re", "s5_b_im", "s5_c_re", "s5_c_im", "s5_d", "s5_glu_w", "s5_glu_b",
                "w_branch", "w_out", "norm_mlp_g", "w_mlp_in", "w_mlp_out", "norm_final_g"]
SHARDED = {"w_in": ((D_MODEL, D_IN_PROJ), 1), "conv_w": ((SSD_CONV, CONV_DIM), 1), "s5_glu_w": ((S5_WIDTH, S5_WIDTH), 0),
           "w_branch": ((D_MODEL + S5_WIDTH, D_MODEL), 0), "w_out": ((D_MODEL, D_MODEL), 0),
           "w_mlp_in": ((D_MODEL, D_FF), 1), "w_mlp_out": ((D_FF, D_MODEL), 0)}
SMALL_ORDER = [n for n in WEIGHT_ORDER if n not in SHARDED]


def _shard_shape(name):
    (r, c), ax = SHARDED[name]
    return (r, c // N_DEV) if ax == 1 else (r // N_DEV, c)


def _join_shards(name, stacked):
    (r, c), ax = SHARDED[name]
    if ax == 1:
        return stacked.transpose(1, 0, 2).reshape(r, c)
    return stacked.reshape(r, c)


def _shard_pieces(lo, hi):
    width = D_IN_PROJ // N_DEV
    out = []
    while lo < hi:
        k = lo // width
        end = min(hi, (k + 1) * width)
        out.append((k, lo - k * width, end - k * width))
        lo = end
    return out


IN_SEGMENTS = {"z": (0, 1024), "xbc": (1024, 3072), "dt": (3072, 3088), "u5": (3088, 3600), "gates": (3600, 5648)}
IN_PADDED_ORDER = ("xbc", "gates", "u5", "z", "dt")


def kernel(x, norm_mix_g, w_in, conv_w, conv_b, dt_bias, a_log, d_ssd, ssd_norm_g, s5_a_re, s5_a_im, s5_log_dt, s5_b_re, s5_b_im, s5_c_re, s5_c_im, s5_d, s5_glu_w, s5_glu_b, w_branch, w_out, norm_mlp_g, w_mlp_in, w_mlp_out, norm_final_g, loss_target, m_norm_mix_g, m_w_in, m_conv_w, m_conv_b, m_dt_bias, m_a_log, m_d_ssd, m_ssd_norm_g, m_s5_a_re, m_s5_a_im, m_s5_log_dt, m_s5_b_re, m_s5_b_im, m_s5_c_re, m_s5_c_im, m_s5_d, m_s5_glu_w, m_s5_glu_b, m_w_branch, m_w_out, m_norm_mlp_g, m_w_mlp_in, m_w_mlp_out, m_norm_final_g, v_norm_mix_g, v_w_in, v_conv_w, v_conv_b, v_dt_bias, v_a_log, v_d_ssd, v_ssd_norm_g, v_s5_a_re, v_s5_a_im, v_s5_log_dt, v_s5_b_re, v_s5_b_im, v_s5_c_re, v_s5_c_im, v_s5_d, v_s5_glu_w, v_s5_glu_b, v_w_branch, v_w_out, v_norm_mlp_g, v_w_mlp_in, v_w_mlp_out, v_norm_final_g):
    w = dict(norm_mix_g=norm_mix_g, w_in=w_in, conv_w=conv_w, conv_b=conv_b, dt_bias=dt_bias, a_log=a_log, d_ssd=d_ssd,
             ssd_norm_g=ssd_norm_g, s5_a_re=s5_a_re, s5_a_im=s5_a_im, s5_log_dt=s5_log_dt, s5_b_re=s5_b_re,
             s5_b_im=s5_b_im, s5_c_re=s5_c_re, s5_c_im=s5_c_im, s5_d=s5_d, s5_glu_w=s5_glu_w, s5_glu_b=s5_glu_b,
             w_branch=w_branch, w_out=w_out, norm_mlp_g=norm_mlp_g, w_mlp_in=w_mlp_in, w_mlp_out=w_mlp_out,
             norm_final_g=norm_final_g)
    m = dict(norm_mix_g=m_norm_mix_g, w_in=m_w_in, conv_w=m_conv_w, conv_b=m_conv_b, dt_bias=m_dt_bias, a_log=m_a_log,
             d_ssd=m_d_ssd, ssd_norm_g=m_ssd_norm_g, s5_a_re=m_s5_a_re, s5_a_im=m_s5_a_im, s5_log_dt=m_s5_log_dt,
             s5_b_re=m_s5_b_re, s5_b_im=m_s5_b_im, s5_c_re=m_s5_c_re, s5_c_im=m_s5_c_im, s5_d=m_s5_d,
             s5_glu_w=m_s5_glu_w, s5_glu_b=m_s5_glu_b, w_branch=m_w_branch, w_out=m_w_out, norm_mlp_g=m_norm_mlp_g,
             w_mlp_in=m_w_mlp_in, w_mlp_out=m_w_mlp_out, norm_final_g=m_norm_final_g)
    v = dict(norm_mix_g=v_norm_mix_g, w_in=v_w_in, conv_w=v_conv_w, conv_b=v_conv_b, dt_bias=v_dt_bias, a_log=v_a_log,
             d_ssd=v_d_ssd, ssd_norm_g=v_ssd_norm_g, s5_a_re=v_s5_a_re, s5_a_im=v_s5_a_im, s5_log_dt=v_s5_log_dt,
             s5_b_re=v_s5_b_re, s5_b_im=v_s5_b_im, s5_c_re=v_s5_c_re, s5_c_im=v_s5_c_im, s5_d=v_s5_d,
             s5_glu_w=v_s5_glu_w, s5_glu_b=v_s5_glu_b, w_branch=v_w_branch, w_out=v_w_out, norm_mlp_g=v_norm_mlp_g,
             w_mlp_in=v_w_mlp_in, w_mlp_out=v_w_mlp_out, norm_final_g=v_norm_final_g)

    loss, grad_x, upd = train_step(x[0], loss_target[0], w, m, v)
    return (loss, grad_x.reshape(x.shape), *[upd[n][j] for j in range(4) for n in WEIGHT_ORDER])
```

```python
import math

import jax
import jax.numpy as jnp
from jax import lax
from jax.experimental import pallas as pl
from jax.experimental.pallas import tpu as pltpu

F32 = jnp.float32
BF16 = jnp.bfloat16

N_DEV = 8
D_MODEL = 1024
SSD_HEADS = 16
SSD_HEADDIM = 64
SSD_GROUPS = 4
SSD_HPG = 4
SSD_STATE = 128
SSD_CHUNK = 128
SSD_CONV = 4
CONV_DIM = 2048
S5_WIDTH = 512
S5_GROUP = 16
S5_GROUPS = 32
S5_STATE = 64
S5_N = S5_GROUPS * S5_STATE
D_FF = 4096
D_IN_PROJ = 5648
IN_SPLITS = (2048, 2048, 512, 1024, 128)
EPS = 1e-6
LANES = 128
SUBLANES = 8
VMEM_LIMIT = 56 * 1024 * 1024

ADAM_LR = 0.001
ADAM_B1 = 0.9
ADAM_B2 = 0.999
ADAM_EPS = 1e-08
ADAM_WD = 0.01
ADAM_STEP = 10

GELU_C = math.sqrt(2.0 / math.pi)
GELU_K = 0.044715


def _params(*sem):
    return pltpu.CompilerParams(dimension_semantics=sem, vmem_limit_bytes=VMEM_LIMIT)


def _full(shape):
    nd = len(shape)
    return pl.BlockSpec(shape, lambda *_: (0,) * nd)


def _resident(shape):
    nd = len(shape)
    return pl.BlockSpec(shape, lambda *_: (0,) * nd, pipeline_mode=pl.Buffered(1))


_MESH = pl.DeviceIdType.MESH
_RELATIONS = [(dx, dy, dc) for dx in (0, 1) for dy in (0, 1) for dc in (0, 1)][1:]
N_REL = len(_RELATIONS)


def _place():
    x, y, c = lax.axis_index("x"), lax.axis_index("y"), lax.axis_index("c")
    return x, y, c, 4 * x + 2 * y + c


def gathers(arrays):
    return [("gather", a) for a in arrays]


def exchanges(arrays):
    return [("exchange", a) for a in arrays]


def _comm_out_shapes(comm):
    return [jax.ShapeDtypeStruct(((N_DEV,) if kind == "gather" else ()) + a.shape, a.dtype) for kind, a in comm]


def _comm_copies(kinds, src_refs, dst_refs, send_sems, recv_sems, local_sems):
    x, y, c, idx = _place()
    local, remote = [], []
    for a, (kind, src, dst) in enumerate(zip(kinds, src_refs, dst_refs)):
        local.append(pltpu.make_async_copy(src if kind == "gather" else src.at[idx], dst.at[idx], local_sems.at[a]))
        for k, (dx, dy, dc) in enumerate(_RELATIONS):
            px, py, pc = x ^ dx, y ^ dy, c ^ dc
            remote.append(pltpu.make_async_remote_copy(
                src_ref=src if kind == "gather" else src.at[4 * px + 2 * py + pc], dst_ref=dst.at[idx],
                send_sem=send_sems.at[N_REL * a + k], recv_sem=recv_sems.at[N_REL * a + k],
                device_id=(px, py, pc), device_id_type=_MESH))
    return local, remote


def _tiled_call(body, *, name, grid, in_specs, out_specs, out_shape, operands, scratch_shapes=(), comm=None):
    params = pltpu.CompilerParams(dimension_semantics=("arbitrary",), vmem_limit_bytes=VMEM_LIMIT,
                                  has_side_effects=bool(comm))
    if not comm:
        outs = pl.pallas_call(body, name=name, grid=grid, in_specs=in_specs, out_specs=out_specs, out_shape=out_shape,
                              scratch_shapes=list(scratch_shapes), compiler_params=params)(*operands)
        return outs, []
    kind = [k for k, _ in comm]
    arrays = [a for _, a in comm]
    na, n_in, n_out, n_scr = len(arrays), len(in_specs), len(out_specs), len(scratch_shapes)
    last = grid[0] - 1

    def hosted(*refs):
        ins, refs = refs[:n_in], refs[n_in:]
        cin, refs = refs[:na], refs[na:]
        outs, refs = refs[:n_out], refs[n_out:]
        cout, refs = refs[:na], refs[na:]
        scr, sems = refs[:n_scr], refs[n_scr:]
        i = pl.program_id(0)

        @pl.when(i == 0)
        def _():
            local, remote = _comm_copies(kind, cin, cout, *sems)
            for cp in local + remote:
                cp.start()

        body(*ins, *outs, *scr)

        @pl.when(i == last)
        def _():
            local, remote = _comm_copies(kind, cin, cout, *sems)
            for cp in remote:
                cp.wait_recv()
            for cp in remote:
                cp.wait_send()
            for cp in local:
                cp.wait()

    any_spec = pl.BlockSpec(memory_space=pl.ANY)
    res = pl.pallas_call(
        hosted, name=name, grid=grid,
        in_specs=list(in_specs) + [any_spec] * na, out_specs=list(out_specs) + [any_spec] * na,
        out_shape=list(out_shape) + _comm_out_shapes(comm),
        scratch_shapes=list(scratch_shapes) + [pltpu.SemaphoreType.DMA((N_REL * na,)), pltpu.SemaphoreType.DMA((N_REL * na,)),
                                               pltpu.SemaphoreType.DMA((na,))],
        compiler_params=params)(*operands, *arrays)
    return res[:n_out], res[n_out:]


def _dot(a, b):
    return jnp.dot(a, b, preferred_element_type=F32)


def _dot_nt(a, b):
    return lax.dot_general(a, b, (((1,), (1,)), ((), ())), preferred_element_type=F32)


def _dot_tn(a, b):
    return lax.dot_general(a, b, (((0,), (0,)), ((), ())), preferred_element_type=F32)


def _split2(x):
    hi = x.astype(BF16)
    return hi, (x - hi.astype(F32)).astype(BF16)


def _select_dot(dot, a, b, exact):
    if exact == "a":
        ea, (hi, lo) = a.astype(BF16), _split2(b)
        return dot(ea, hi) + dot(ea, lo)
    eb, (hi, lo) = b.astype(BF16), _split2(a)
    return dot(hi, eb) + dot(lo, eb)


def _dot_sel_a(a, b):
    return _select_dot(_dot, a, b, "a")


def _dot_sel_b(a, b):
    return _select_dot(_dot, a, b, "b")


def _dot_nt_sel_a(a, b):
    return _select_dot(_dot_nt, a, b, "a")


def _dot_tn_sel(a, b, exact):
    return _select_dot(_dot_tn, a, b, exact)


def _sigmoid(x):
    return 1.0 / (1.0 + jnp.exp(-x))


def _softplus(x):
    return jnp.maximum(x, 0.0) + jnp.log(1.0 + jnp.exp(-jnp.abs(x)))


def _gelu(x):
    return 0.5 * x * (1.0 + jnp.tanh(GELU_C * (x + GELU_K * x * x * x)))


def _gelu_grad(x):
    th = jnp.tanh(GELU_C * (x + GELU_K * x * x * x))
    return 0.5 * (1.0 + th) + 0.5 * x * (1.0 - th * th) * GELU_C * (1.0 + 3.0 * GELU_K * x * x)


def rms_matmul(x, g, w, splits, *, tm, name, comm=None, out_dtypes=None):
    t, d = x.shape
    stacked = w.ndim == 3
    n = w.shape[0] * w.shape[2] if stacked else w.shape[1]
    assert sum(splits) == n and (len(splits) == 1 or not stacked)

    def body(x_ref, g_ref, w_ref, h_ref, *o_refs):
        xv = x_ref[...]
        r = lax.rsqrt(jnp.mean(xv * xv, axis=-1, keepdims=True) + EPS)
        h = (xv * r * g_ref[...]).astype(BF16)
        h_ref[...] = h
        if stacked:
            wk = w.shape[2]
            for k in range(w.shape[0]):
                o_refs[0][:, k * wk:(k + 1) * wk] = _dot(h, w_ref[k]).astype(o_refs[0].dtype)
            return
        off = 0
        for o_ref, width in zip(o_refs, splits):
            o_ref[...] = _dot(h, w_ref[:, off:off + width]).astype(o_ref.dtype)
            off += width

    tile = lambda c: pl.BlockSpec((tm, c), lambda i: (i, 0))
    dtypes = out_dtypes or (F32,) * len(splits)
    return _tiled_call(
        body, name=name, grid=(t // tm,),
        in_specs=[tile(d), _full((1, d)), _resident(w.shape)],
        out_specs=[tile(d)] + [tile(c) for c in splits],
        out_shape=[jax.ShapeDtypeStruct((t, d), BF16)] + [jax.ShapeDtypeStruct((t, c), dt) for c, dt in zip(splits, dtypes)],
        operands=(x, g.reshape(1, d), w), comm=comm)


MATMUL_TN_ACC_BYTES = 16 * 1024 * 1024


def _pick(n, cap):
    best = LANES
    for c in range(LANES, cap + 1, LANES):
        if n % c == 0:
            best = c
    return best


def matmul_tn(a, b, *, name, out_dtype=F32, col_shards=None, tk=1024, a_relu2=False):
    t, m = a.shape
    n = b.shape[1]
    tm = _pick(m, 1024)
    whole_n = tm * n * 4 <= MATMUL_TN_ACC_BYTES
    tn = n if whole_n else _pick(n, 1280)
    assert whole_n or not col_shards
    tk = min(tk if tn <= 2048 else tk // 2, t)
    nk = t // tk

    def body(a_ref, b_ref, o_ref, acc):
        k = pl.program_id(2)

        @pl.when(k == 0)
        def _():
            acc[...] = jnp.zeros_like(acc)

        av = a_ref[...]
        if a_relu2:
            pos = jnp.maximum(av.astype(F32), 0.0)
            av = (pos * pos).astype(BF16)
        acc[...] += _dot_tn(av, b_ref[...])

        @pl.when(k == nk - 1)
        def _():
            if col_shards:
                w = n // col_shards
                for s in range(col_shards):
                    o_ref[s] = acc[:, s * w:(s + 1) * w].astype(out_dtype)
            else:
                o_ref[...] = acc[...].astype(out_dtype)

    if col_shards:
        out_spec = pl.BlockSpec((col_shards, tm, n // col_shards), lambda i, j, k: (0, i, 0))
        out_shape = jax.ShapeDtypeStruct((col_shards, m, n // col_shards), out_dtype)
    else:
        out_spec = pl.BlockSpec((tm, tn), lambda i, j, k: (i, j))
        out_shape = jax.ShapeDtypeStruct((m, n), out_dtype)
    return pl.pallas_call(
        body, name=name, grid=(m // tm, n // tn, nk),
        in_specs=[pl.BlockSpec((tk, tm), lambda i, j, k: (k, i)), pl.BlockSpec((tk, tn), lambda i, j, k: (k, j))],
        out_specs=out_spec, out_shape=out_shape,
        scratch_shapes=[pltpu.VMEM((tm, tn), F32)],
        compiler_params=_params("parallel", "parallel", "arbitrary"),
    )(a, b)


def matmul_nt_rms_bwd(dys, w, x, g, dres, *, tm, name, comm=None, bf16_copy=True):
    t = x.shape[0]
    stacked = w.ndim == 3
    d = w.shape[1] if stacked else w.shape[0]
    widths = [dy.shape[1] for dy in dys]
    n_dy = len(dys)

    def body(*refs):
        dy_refs = refs[:n_dy]
        w_ref, x_ref, g_ref, dres_ref, dx_ref = refs[n_dy:n_dy + 5]
        dxb_ref = refs[n_dy + 5] if bf16_copy else None
        dg_ref = refs[-1]
        i = pl.program_id(0)
        if stacked:
            wk = w.shape[2]
            dh = _dot_nt(dy_refs[0][:, 0:wk], w_ref[0])
            for k in range(1, w.shape[0]):
                dh = dh + _dot_nt(dy_refs[0][:, k * wk:(k + 1) * wk], w_ref[k])
        else:
            dh, off = None, 0
            for dy_ref, width in zip(dy_refs, widths):
                part = _dot_nt(dy_ref[...], w_ref[:, off:off + width])
                dh = part if dh is None else dh + part
                off += width
        xv = x_ref[...]
        r = lax.rsqrt(jnp.mean(xv * xv, axis=-1, keepdims=True) + EPS)
        xn = xv * r
        dxn = dh * g_ref[...]
        dx = dres_ref[...] + r * (dxn - xn * jnp.mean(dxn * xn, axis=-1, keepdims=True))
        dx_ref[...] = dx
        if bf16_copy:
            dxb_ref[...] = dx.astype(BF16)

        @pl.when(i == 0)
        def _():
            dg_ref[...] = jnp.zeros_like(dg_ref)

        dg_ref[...] += jnp.sum(dh * xn, axis=0, keepdims=True)

    tile = lambda c: pl.BlockSpec((tm, c), lambda i: (i, 0))
    copies = [BF16] if bf16_copy else []
    return _tiled_call(
        body, name=name, grid=(t // tm,),
        in_specs=[tile(c) for c in widths] + [_resident(w.shape), tile(d), _full((1, d)), tile(d)],
        out_specs=[tile(d)] + [tile(d) for _ in copies] + [_full((1, d))],
        out_shape=[jax.ShapeDtypeStruct((t, d), F32)] + [jax.ShapeDtypeStruct((t, d), dt) for dt in copies]
        + [jax.ShapeDtypeStruct((1, d), F32)],
        operands=(*dys, w, x, g.reshape(1, d), dres), comm=comm)


def _conv_windows(ext_ref, tm):
    ext = ext_ref[...]
    return [pltpu.roll(ext, 3 - k, 0)[SUBLANES:SUBLANES + tm, :] for k in range(3)] + [ext[SUBLANES:SUBLANES + tm, :]]


def _conv_taps(windows, w_ref):
    acc = windows[3] * w_ref[3:4, :]
    for k in range(3):
        acc = acc + windows[k] * w_ref[k:k + 1, :]
    return acc


def _ssd_chunk_terms(dtr, bias_row, alog_row):
    q = SSD_CHUNK
    row = lax.broadcasted_iota(jnp.int32, (q, q), 0)
    col = lax.broadcasted_iota(jnp.int32, (q, q), 1)
    ltri = (col <= row).astype(F32)
    dt = _softplus(dtr + bias_row)
    a_row = -jnp.exp(alog_row)
    la = dt * a_row
    cum = _dot_sel_a(ltri, la)
    cum_t = _dot_tn_sel(la, (row <= col).astype(F32), "b")
    return dt, a_row, cum, cum_t, ltri, row, col


def _head_maps():
    di = SSD_HEADS * SSD_HEADDIM
    expand = (lax.broadcasted_iota(jnp.int32, (LANES, di), 1) // SSD_HEADDIM
              == lax.broadcasted_iota(jnp.int32, (LANES, di), 0)).astype(F32)
    collapse = (lax.broadcasted_iota(jnp.int32, (di, LANES), 0) // SSD_HEADDIM
                == lax.broadcasted_iota(jnp.int32, (di, LANES), 1)).astype(F32)
    return expand, collapse


def _ssd_decay(cum, cum_t, h, causal):
    seg = cum[:, h:h + 1] - cum_t[h:h + 1, :]
    return jnp.where(causal, jnp.exp(jnp.where(causal, seg, 0.0)), 0.0)


def _pair_block_diag(x2):
    low = lax.broadcasted_iota(jnp.int32, x2.shape, 1) < SSD_HEADDIM
    zero = jnp.zeros_like(x2)
    return jnp.concatenate([jnp.where(low, x2, zero), jnp.where(low, zero, x2)], axis=0)


def ssd_fwd(xbc_raw, conv_w8, conv_b, dtr, z, bias_row, alog_row, d_row, norm_g, *, name, comm=None):
    t = xbc_raw.shape[0]
    q, p, n = SSD_CHUNK, SSD_HEADDIM, SSD_STATE
    nc = t // q
    di = SSD_HEADS * p
    gw = di // SSD_GROUPS

    def body(raw_ref, cw_ref, cb_ref, dtr_ref, z_ref, bias_ref, alog_ref, d_ref, g_ref,
             ya_ref, yssd_ref, st_ref, state, ext, xbc_s):
        ci = pl.program_id(0)

        @pl.when(ci == 0)
        def _():
            state[...] = jnp.zeros_like(state)
            ext[0:SUBLANES, :] = jnp.zeros((SUBLANES, CONV_DIM), F32)

        ext[SUBLANES:SUBLANES + q, :] = raw_ref[...].astype(F32)
        xc = _conv_taps(_conv_windows(ext, q), cw_ref) + cb_ref[...]
        ext[0:SUBLANES, :] = ext[q:q + SUBLANES, :]
        xbc_s[...] = xc * _sigmoid(xc)
        st_ref[0] = state[...]
        dt, _, cum, cum_t, _, row, col = _ssd_chunk_terms(dtr_ref[...], bias_ref[...], alog_ref[...])
        causal = row >= col
        expand, _ = _head_maps()
        dt_full = _dot_sel_b(dt, expand)
        cum_full = _dot_sel_b(cum, expand)
        last_full = cum_full[q - 1:q, :]
        xs = xbc_s[:, 0:di]
        xdt = xs * dt_full
        xd = (xdt * jnp.exp(last_full - cum_full)).astype(BF16)
        xdt_b = xdt.astype(BF16)
        e_full = jnp.exp(cum_full)
        eend_full = jnp.exp(last_full)
        for g in range(SSD_GROUPS):
            slab = slice(g * gw, (g + 1) * gw)
            bg = xbc_s[:, di + g * n:di + (g + 1) * n].astype(BF16)
            cg = xbc_s[:, di + (SSD_GROUPS + g) * n:di + (SSD_GROUPS + g + 1) * n].astype(BF16)
            cb = _dot_nt(cg, bg)
            st_g = state[:, slab]
            pairs = []
            for pr in range(SSD_HPG // 2):
                h0 = g * SSD_HPG + 2 * pr
                m0 = (cb * _ssd_decay(cum, cum_t, h0, causal)).astype(BF16)
                m1 = (cb * _ssd_decay(cum, cum_t, h0 + 1, causal)).astype(BF16)
                pairs.append(_dot(jnp.concatenate([m0, m1], axis=1), _pair_block_diag(xdt_b[:, h0 * p:(h0 + 2) * p])))
            y = jnp.concatenate(pairs, axis=1) + e_full[:, slab] * _dot(cg, st_g.astype(BF16))
            yssd_ref[:, slab] = y + d_ref[:, slab] * xs[:, slab]
            state[:, slab] = eend_full[:, slab] * st_g + _dot_tn(bg, xd[:, slab])
        zv = z_ref[...].astype(F32)
        ya1 = yssd_ref[...] * (zv * _sigmoid(zv))
        for g in range(SSD_GROUPS):
            gsl = slice(g * gw, (g + 1) * gw)
            sl = ya1[:, gsl]
            rg = lax.rsqrt(jnp.mean(sl * sl, axis=-1, keepdims=True) + EPS)
            ya_ref[:, gsl] = (sl * rg * g_ref[:, gsl]).astype(BF16)

    blk = lambda w, j: pl.BlockSpec((q, w), lambda c: (c, j))
    return _tiled_call(
        body, name=name, grid=(nc,),
        in_specs=[blk(CONV_DIM, 0), _full((SUBLANES, CONV_DIM)), _full((1, CONV_DIM)), blk(LANES, 0), blk(di, 0),
                  _full((1, LANES)), _full((1, LANES)), _full((1, di)), _full((1, di))],
        out_specs=[blk(di, 0), blk(di, 0), pl.BlockSpec((1, n, di), lambda c: (c, 0, 0))],
        out_shape=[jax.ShapeDtypeStruct((t, di), BF16), jax.ShapeDtypeStruct((t, di), F32),
                   jax.ShapeDtypeStruct((nc, n, di), F32)],
        scratch_shapes=[pltpu.VMEM((n, di), F32), pltpu.VMEM((SUBLANES + q, CONV_DIM), F32), pltpu.VMEM((q, CONV_DIM), F32)],
        operands=(xbc_raw, conv_w8, conv_b.reshape(1, CONV_DIM), dtr, z, bias_row, alog_row, d_row,
                  norm_g.reshape(1, di)), comm=comm)


def ssd_bwd(dya, yssd, z, xbc_raw, conv_w8, conv_b, dtr, states, bias_row, alog_row, d_row, norm_g, *, name, comm=None):
    t = xbc_raw.shape[0]
    q, p, n = SSD_CHUNK, SSD_HEADDIM, SSD_STATE
    nc = t // q
    di = SSD_HEADS * p
    gw = di // SSD_GROUPS
    per = q // (2 * SUBLANES)

    def body(dya_ref, yssd_ref, z_ref, raw_ref, prev_ref, cw_ref, cb_ref, dtr_ref, st_ref, bias_ref, alog_ref, d_ref,
             g_ref, dzdt_ref, draw_ref, dng_ref, dd_ref, dalog_ref, dbias_ref, dcb_ref, dcw_ref,
             dstate, dyssd, xbc_s, dxbc_ref, ext, aft):
        ci = pl.program_id(0)

        @pl.when(ci == 0)
        def _():
            dstate[...] = jnp.zeros_like(dstate)
            aft[q:q + SUBLANES, :] = jnp.zeros((SUBLANES, CONV_DIM), F32)
            for r in (dng_ref, dd_ref, dalog_ref, dbias_ref, dcb_ref, dcw_ref):
                r[...] = jnp.zeros_like(r)

        ext[0:SUBLANES, :] = jnp.where(ci == nc - 1, 0.0, prev_ref[SUBLANES:2 * SUBLANES, :].astype(F32))
        ext[SUBLANES:SUBLANES + q, :] = raw_ref[...].astype(F32)
        windows = _conv_windows(ext, q)
        xc = _conv_taps(windows, cw_ref) + cb_ref[...]
        sig_c = _sigmoid(xc)
        xbc_s[...] = xc * sig_c

        zv = z_ref[...].astype(F32)
        sz = _sigmoid(zv)
        silu_z = zv * sz
        yv = yssd_ref[...]
        ya1 = yv * silu_z
        dyav = dya_ref[...]
        for g in range(SSD_GROUPS):
            gsl = slice(g * gw, (g + 1) * gw)
            sl = ya1[:, gsl]
            rg = lax.rsqrt(jnp.mean(sl * sl, axis=-1, keepdims=True) + EPS)
            ya2 = sl * rg
            dy_g = dyav[:, gsl]
            dng_ref[:, gsl] += jnp.sum(dy_g * ya2, axis=0, keepdims=True)
            dya2 = dy_g * g_ref[:, gsl]
            dya1 = rg * (dya2 - ya2 * jnp.mean(dya2 * ya2, axis=-1, keepdims=True))
            dyssd[:, gsl] = dya1 * silu_z[:, gsl]
            dzdt_ref[:, gsl] = (dya1 * yv[:, gsl] * (sz[:, gsl] * (1.0 + zv[:, gsl] * (1.0 - sz[:, gsl])))).astype(BF16)

        dtr_v = dtr_ref[...]
        dt, a_row, cum, cum_t, ltri, row, col = _ssd_chunk_terms(dtr_v, bias_ref[...], alog_ref[...])
        causal = row >= col
        expand, collapse = _head_maps()
        lane = lax.broadcasted_iota(jnp.int32, (1, LANES), 1)
        sub = lax.broadcasted_iota(jnp.int32, (LANES, 1), 0)
        is_last = lax.broadcasted_iota(jnp.int32, (q, 1), 0) == q - 1
        low = lax.broadcasted_iota(jnp.int32, (q, 2 * p), 1) < p
        dt_full = _dot_sel_b(dt, expand)
        cum_full = _dot_sel_b(cum, expand)
        last_full = cum_full[q - 1:q, :]
        e_full = jnp.exp(cum_full)
        eend_full = jnp.exp(last_full)
        dend_full = jnp.exp(last_full - cum_full)
        xs = xbc_s[:, 0:di]
        xdt = xs * dt_full
        xdt_b = xdt.astype(BF16)
        xd_b = (xdt * dend_full).astype(BF16)
        dy_all = dyssd[...]
        dy_b = dy_all.astype(BF16)
        dg_b = (dy_all * e_full).astype(BF16)
        dcum_mat = jnp.zeros((q, LANES), F32)
        rmat = jnp.zeros((LANES, q), F32)
        yoff_prod, bds_list, dx_list, dlast_parts = [], [], [], []
        for g in range(SSD_GROUPS):
            slab = slice(g * gw, (g + 1) * gw)
            bg = xbc_s[:, di + g * n:di + (g + 1) * n].astype(BF16)
            cg = xbc_s[:, di + (SSD_GROUPS + g) * n:di + (SSD_GROUPS + g + 1) * n].astype(BF16)
            cb = _dot_nt(cg, bg)
            st_g = st_ref[0, :, slab]
            st_b = st_g.astype(BF16)
            dsn = dstate[:, slab]
            dsn_b = dsn.astype(BF16)
            yoff_prod.append(dy_all[:, slab] * e_full[:, slab] * _dot(cg, st_b))
            dcg = _dot_nt(dg_b[:, slab], st_b)
            dstate[:, slab] = _dot_tn(cg, dg_b[:, slab]) + eend_full[:, slab] * dsn
            bds_list.append(_dot(bg, dsn_b))
            dbg = _dot_nt(xd_b[:, slab], dsn_b)
            dlast_parts.append(jnp.sum(dsn * st_g, axis=0, keepdims=True))
            dcb = jnp.zeros((q, q), F32)
            dx_pairs = []
            for pr in range(SSD_HPG // 2):
                h0 = g * SSD_HPG + 2 * pr
                ps = slice(h0 * p, (h0 + 2) * p)
                l0 = _ssd_decay(cum, cum_t, h0, causal)
                l1 = _ssd_decay(cum, cum_t, h0 + 1, causal)
                m0, m1 = cb * l0, cb * l1
                dyp = dy_b[:, ps]
                zero = jnp.zeros_like(dyp)
                dy0, dy1 = jnp.where(low, dyp, zero), jnp.where(low, zero, dyp)
                dm0 = _dot_nt(dy0, xdt_b[:, ps])
                dm1 = _dot_nt(dy1, xdt_b[:, ps])
                w0, w1 = dm0 * m0, dm1 * m1
                dcum_mat = (dcum_mat + jnp.sum(w0, axis=1, keepdims=True) * (lane == h0).astype(F32)
                            + jnp.sum(w1, axis=1, keepdims=True) * (lane == h0 + 1).astype(F32))
                rmat = (rmat + jnp.where(sub == h0, jnp.sum(w0, axis=0, keepdims=True), 0.0)
                        + jnp.where(sub == h0 + 1, jnp.sum(w1, axis=0, keepdims=True), 0.0))
                dcb = dcb + dm0 * l0 + dm1 * l1
                dx_pairs.append(_dot_tn(jnp.concatenate([m0.astype(BF16), m1.astype(BF16)], axis=0),
                                        jnp.concatenate([dy0, dy1], axis=0)))
            dx_list.append(jnp.concatenate(dx_pairs, axis=1))
            dcb_b = dcb.astype(BF16)
            dxbc_ref[:, di + g * n:di + (g + 1) * n] = dbg + _dot_tn(dcb_b, cg)
            dxbc_ref[:, di + SSD_GROUPS * n + g * n:di + SSD_GROUPS * n + (g + 1) * n] = dcg + _dot(dcb_b, bg)
        bds_all = jnp.concatenate(bds_list, axis=1)
        dx_all = jnp.concatenate(dx_list, axis=1) + dend_full * bds_all
        last_row = cum[q - 1:q, :]
        qv = _dot_sel_b(xdt * bds_all, collapse) * jnp.exp(last_row - cum)
        dcum_mat = dcum_mat + _dot_sel_b(jnp.concatenate(yoff_prod, axis=1), collapse) - qv
        dlast_full = jnp.broadcast_to(jnp.concatenate(dlast_parts, axis=1), (SUBLANES, di))
        dlast_row = _dot_sel_b(dlast_full, collapse)[0:1, :] * jnp.exp(last_row) + jnp.sum(qv, axis=0, keepdims=True)
        dcum_mat = dcum_mat + jnp.where(is_last, dlast_row, 0.0)
        dla = _dot_tn_sel(ltri, dcum_mat, "a") - _dot_nt_sel_a((row <= col).astype(F32), rmat)
        ddt_mat = _dot_sel_b(dx_all * xs, collapse) + dla * a_row
        dxbc_ref[:, 0:di] = d_ref[...] * dy_all + dx_all * dt_full
        ddtr = ddt_mat * _sigmoid(dtr_v + bias_ref[...])
        dzdt_ref[:, di:di + LANES] = ddtr.astype(BF16)
        dd_full = jnp.broadcast_to(jnp.sum(dy_all * xs, axis=0, keepdims=True), (SUBLANES, di))
        dd_ref[...] += _dot_sel_b(dd_full, collapse)[0:1, :]
        dalog_ref[...] += jnp.sum(dla * dt, axis=0, keepdims=True) * a_row
        dbias_ref[...] += jnp.sum(ddtr, axis=0, keepdims=True)

        dxc = dxbc_ref[...] * (sig_c * (1.0 + xc * (1.0 - sig_c)))
        dcb_ref[...] += jnp.sum(dxc, axis=0, keepdims=True)
        taps = [jnp.sum(dxc * windows[k], axis=0, keepdims=True) for k in range(SSD_CONV)]
        dcw_ref[...] += jnp.concatenate(taps + [jnp.zeros((SUBLANES - SSD_CONV, CONV_DIM), F32)], axis=0)
        aft[0:q, :] = dxc
        after = aft[...]
        draw = dxc * cw_ref[3:4, :]
        for j in range(1, SSD_CONV):
            draw = draw + pltpu.roll(after, q + SUBLANES - j, 0)[0:q, :] * cw_ref[3 - j:4 - j, :]
        draw_ref[...] = draw.astype(BF16)
        aft[q:q + SUBLANES, :] = dxc[0:SUBLANES, :]

    blk = lambda w, j: pl.BlockSpec((q, w), lambda c: (nc - 1 - c, j))
    row_out = lambda w: jax.ShapeDtypeStruct((1, w), F32)
    return _tiled_call(
        body, name=name, grid=(nc,),
        in_specs=[blk(di, 0), blk(di, 0), blk(di, 0), blk(CONV_DIM, 0),
                  pl.BlockSpec((2 * SUBLANES, CONV_DIM), lambda c: (jnp.maximum((nc - 1 - c) * per - 1, 0), 0)),
                  _full((SUBLANES, CONV_DIM)), _full((1, CONV_DIM)), blk(LANES, 0),
                  pl.BlockSpec((1, n, di), lambda c: (nc - 1 - c, 0, 0)),
                  _full((1, LANES)), _full((1, LANES)), _full((1, di)), _full((1, di))],
        out_specs=[blk(di + LANES, 0), blk(CONV_DIM, 0),
                   _full((1, di)), _full((1, LANES)), _full((1, LANES)), _full((1, LANES)),
                   _full((1, CONV_DIM)), _full((SUBLANES, CONV_DIM))],
        out_shape=[jax.ShapeDtypeStruct((t, di + LANES), BF16), jax.ShapeDtypeStruct((t, CONV_DIM), BF16),
                   row_out(di), row_out(LANES), row_out(LANES), row_out(LANES),
                   row_out(CONV_DIM), jax.ShapeDtypeStruct((SUBLANES, CONV_DIM), F32)],
        scratch_shapes=[pltpu.VMEM((n, di), F32), pltpu.VMEM((q, di), F32), pltpu.VMEM((q, CONV_DIM), F32),
                        pltpu.VMEM((q, CONV_DIM), F32), pltpu.VMEM((SUBLANES + q, CONV_DIM), F32),
                        pltpu.VMEM((q + SUBLANES, CONV_DIM), F32)],
        operands=(dya, yssd, z, xbc_raw, xbc_raw, conv_w8, conv_b.reshape(1, CONV_DIM), dtr, states, bias_row,
                  alog_row, d_row, norm_g.reshape(1, di)),
        comm=comm)


S5_BLOCKS = 4
S5_BC = S5_WIDTH // S5_BLOCKS
S5_BS = S5_N // S5_BLOCKS


def _s5_tables(ar, ai, reverse):
    pows = [(ar, ai)]
    for _ in range(SUBLANES - 1):
        pr, pi = pows[-1]
        pows.append((pr * ar - pi * ai, pr * ai + pi * ar))
    row = lax.broadcasted_iota(jnp.int32, (SUBLANES, ar.shape[1]), 0)
    out = []
    for k in (1, 2, 4):
        pr, pi = pows[k - 1]
        mask = (row < SUBLANES - k) if reverse else (row >= k)
        out += [jnp.where(mask, pr, 0.0), jnp.where(mask, pi, 0.0)]
    order = list(range(SUBLANES))
    if reverse:
        order = order[::-1]
    out.append(jnp.concatenate([pows[e][0] for e in order], axis=0))
    out.append(jnp.concatenate([pows[e][1] for e in order], axis=0))
    return out


def s5_interleave(a, tm):
    t, c = a.shape
    return a.reshape(t // tm, SUBLANES, tm // SUBLANES, c).transpose(0, 2, 1, 3).reshape(t, c)


def s5_deinterleave(a, tm):
    t, c = a.shape
    return a.reshape(t // tm, tm // SUBLANES, SUBLANES, c).transpose(0, 2, 1, 3).reshape(t, c)


def _s5_scan_setup(ab_ref, base, tab, seg, reverse):
    ar = ab_ref[0:1, :]
    ai = -ab_ref[1:2, :] if reverse else ab_ref[1:2, :]
    base[0:1, :] = ar
    base[1:2, :] = ai
    assert seg & (seg - 1) == 0
    pr, pi = ar, ai
    for _ in range(seg.bit_length() - 1):
        pr, pi = pr * pr - pi * pi, 2.0 * pr * pi
    for k, tv in enumerate(_s5_tables(pr, pi, reverse)):
        tab[k] = tv


def _s5_scan(s_scr, base, tab, carry, tm, reverse):
    seg = tm // SUBLANES
    width = S5_BS
    first = lax.broadcasted_iota(jnp.int32, (SUBLANES, width), 0) == (SUBLANES - 1 if reverse else 0)

    def rows(i):
        step = (seg - 1 - i) if reverse else i
        return pl.ds(pl.multiple_of(step * SUBLANES, SUBLANES), SUBLANES)

    for lc in range(S5_N // width):
        lr = slice(lc * width, (lc + 1) * width)
        li = slice(S5_N + lc * width, S5_N + (lc + 1) * width)
        ar = jnp.broadcast_to(base[0:1, lr], (SUBLANES, width))
        ai = jnp.broadcast_to(base[1:2, lr], (SUBLANES, width))

        def advance(i, x, lr=lr, li=li, ar=ar, ai=ai):
            xr, xi = x
            return ar * xr - ai * xi + s_scr[rows(i), lr], ar * xi + ai * xr + s_scr[rows(i), li]

        zero = jnp.zeros((SUBLANES, width), F32)
        fr, fi = lax.fori_loop(0, seg, advance, (zero, zero))

        tb = [tab[k, :, lr] for k in range(8)]
        for lvl, k in enumerate((1, 2, 4)):
            sh = (SUBLANES - k) if reverse else k
            pr, pi = tb[2 * lvl], tb[2 * lvl + 1]
            rr, ri = pltpu.roll(fr, sh, 0), pltpu.roll(fi, sh, 0)
            fr, fi = fr + pr * rr - pi * ri, fi + pr * ri + pi * rr
        cr, ci = carry[0:1, lr], carry[0:1, li]
        fr, fi = fr + tb[6] * cr - tb[7] * ci, fi + tb[6] * ci + tb[7] * cr
        last = 0 if reverse else SUBLANES - 1
        carry[0:1, lr] = fr[last:last + 1, :]
        carry[0:1, li] = fi[last:last + 1, :]
        sh = (SUBLANES - 1) if reverse else 1
        entering = (jnp.where(first, cr, pltpu.roll(fr, sh, 0)), jnp.where(first, ci, pltpu.roll(fi, sh, 0)))

        def rerun(i, x, lr=lr, li=li, advance=advance):
            xr, xi = advance(i, x)
            s_scr[rows(i), lr] = xr
            s_scr[rows(i), li] = xi
            return xr, xi

        lax.fori_loop(0, seg, rerun, entering)


def _s5_scratch(tm):
    return [pltpu.VMEM((tm, 2 * S5_N), F32), pltpu.VMEM((SUBLANES, S5_N), F32),
            pltpu.VMEM((8, SUBLANES, S5_N), F32), pltpu.VMEM((SUBLANES, 2 * S5_N), F32)]


def _s5_put(scr, j, val):
    scr[:, j * S5_BS:(j + 1) * S5_BS] = val[:, :S5_BS]
    scr[:, S5_N + j * S5_BS:S5_N + (j + 1) * S5_BS] = val[:, S5_BS:]


def _s5_get(scr, j):
    return jnp.concatenate([scr[:, j * S5_BS:(j + 1) * S5_BS], scr[:, S5_N + j * S5_BS:S5_N + (j + 1) * S5_BS]], axis=1)


def s5_fwd(u5, bbc, ccc, ab8, d_row, wglu, bglu_row, *, tm, name, comm=None):
    t, w = u5.shape

    def body(u_ref, bb_ref, cc_ref, ab_ref, d_ref, wg_ref, bg_ref, s_ref, ys_ref, yb_ref, s_scr, base, tab, carry):
        i = pl.program_id(0)

        @pl.when(i == 0)
        def _():
            carry[...] = jnp.zeros_like(carry)
            _s5_scan_setup(ab_ref, base, tab, tm // SUBLANES, False)

        ub = u_ref[...]
        u = ub.astype(F32)
        for j in range(S5_BLOCKS):
            uj = ub[:, j * S5_BC:(j + 1) * S5_BC]
            _s5_put(s_scr, j, _dot(uj, bb_ref[j * S5_BC:(j + 1) * S5_BC, :]))
        _s5_scan(s_scr, base, tab, carry, tm, False)
        s_ref[...] = s_scr[...]
        ys = []
        for j in range(S5_BLOCKS):
            ys.append(_dot(_s5_get(s_scr, j).astype(BF16), cc_ref[:, j * S5_BC:(j + 1) * S5_BC]))
        y = jnp.concatenate(ys, axis=1) + d_ref[...] * u
        ys_ref[...] = y
        yb1 = _gelu(y)
        tt = _dot(yb1.astype(BF16), wg_ref[...]) + bg_ref[...]
        yb_ref[...] = (yb1 * _sigmoid(tt)).astype(BF16)

    tile = lambda c: pl.BlockSpec((tm, c), lambda i: (i, 0))
    return _tiled_call(
        body, name=name, grid=(t // tm,),
        in_specs=[tile(w), _full((w, 2 * S5_BS)), _full((2 * S5_BS, w)), _full((SUBLANES, S5_N)), _full((1, w)),
                  _full((w, w)), _full((1, w))],
        out_specs=[tile(2 * S5_N), tile(w), tile(w)],
        out_shape=[jax.ShapeDtypeStruct((t, 2 * S5_N), F32), jax.ShapeDtypeStruct((t, w), F32),
                   jax.ShapeDtypeStruct((t, w), BF16)],
        scratch_shapes=_s5_scratch(tm),
        operands=(u5, bbc, ccc, ab8, d_row, wglu, bglu_row), comm=comm)


def s5_bwd(dyb, ys5, u5, s, bbc, ccc, ab8, d_row, wglu, bglu_row, *, tm, name, comm=None):
    t, w = u5.shape
    nt = t // tm
    per = tm // SUBLANES

    def body(dyb_ref, ys_ref, u_ref, s_ref, sprev_ref, bb_ref, cc_ref, ab_ref, d_ref, wg_ref, bg_ref,
             du_ref, dbb_ref, dcc_ref, dab_ref, dd_ref, dwg_ref, dbg_ref, g_scr, base, tab, carry):
        i = pl.program_id(0)

        @pl.when(i == 0)
        def _():
            carry[...] = jnp.zeros_like(carry)
            _s5_scan_setup(ab_ref, base, tab, tm // SUBLANES, True)
            for r in (dbb_ref, dcc_ref, dab_ref, dd_ref, dwg_ref, dbg_ref):
                r[...] = jnp.zeros_like(r)

        y = ys_ref[...]
        ub = u_ref[...]
        u = ub.astype(F32)
        yb1 = _gelu(y)
        yb1b = yb1.astype(BF16)
        sg = _sigmoid(_dot(yb1b, wg_ref[...]) + bg_ref[...])
        dybv = dyb_ref[...]
        dtt = dybv * yb1 * sg * (1.0 - sg)
        dttb = dtt.astype(BF16)
        dyb1 = dybv * sg + _dot_nt(dttb, wg_ref[...])
        dwg_ref[...] += _dot_tn(yb1b, dttb)
        dbg_ref[...] += jnp.sum(dtt, axis=0, keepdims=True)
        dy = dyb1 * _gelu_grad(y)
        dd_ref[...] += jnp.sum(dy * u, axis=0, keepdims=True)
        dyh = dy.astype(BF16)
        for j in range(S5_BLOCKS):
            cs = slice(j * S5_BC, (j + 1) * S5_BC)
            _s5_put(g_scr, j, _dot_nt(dyh[:, cs], cc_ref[:, cs]))
        _s5_scan(g_scr, base, tab, carry, tm, True)
        dus = []
        for j in range(S5_BLOCKS):
            cs = slice(j * S5_BC, (j + 1) * S5_BC)
            gb = _s5_get(g_scr, j).astype(BF16)
            dus.append(_dot_nt(gb, bb_ref[cs, :]))
            dbb_ref[cs, :] += _dot_tn(ub[:, cs], gb)
            dcc_ref[:, cs] += _dot_tn(_s5_get(s_ref, j).astype(BF16), dyh[:, cs])
        du_ref[...] = (jnp.concatenate(dus, axis=1) + d_ref[...] * dy).astype(BF16)
        top = lax.broadcasted_iota(jnp.int32, (SUBLANES, S5_BS), 0) == 0

        def corr(g_sl, s_sl):
            before = jnp.where(i == nt - 1, 0.0, sprev_ref[SUBLANES - 1:SUBLANES, s_sl])
            wrap = jnp.where(top, before, pltpu.roll(s_ref[tm - SUBLANES:tm, s_sl], 1, 0))
            return (jnp.sum(g_scr[SUBLANES:tm, g_sl] * s_ref[0:tm - SUBLANES, s_sl], axis=0, keepdims=True)
                    + jnp.sum(g_scr[0:SUBLANES, g_sl] * wrap, axis=0, keepdims=True))

        for j in range(S5_BLOCKS):
            lr = slice(j * S5_BS, (j + 1) * S5_BS)
            li = slice(S5_N + j * S5_BS, S5_N + (j + 1) * S5_BS)
            dab_ref[0:1, lr] += corr(lr, lr) + corr(li, li)
            dab_ref[1:2, lr] += corr(li, lr) - corr(lr, li)

    tile = lambda c: pl.BlockSpec((tm, c), lambda i: (nt - 1 - i, 0))
    acc = lambda r, c: jax.ShapeDtypeStruct((r, c), F32)
    return _tiled_call(
        body, name=name, grid=(nt,),
        in_specs=[tile(w), tile(w), tile(w), tile(2 * S5_N),
                  pl.BlockSpec((SUBLANES, 2 * S5_N), lambda i: (jnp.maximum((nt - 1 - i) * per - 1, 0), 0)),
                  _full((w, 2 * S5_BS)), _full((2 * S5_BS, w)), _full((SUBLANES, S5_N)), _full((1, w)),
                  _full((w, w)), _full((1, w))],
        out_specs=[tile(w), _full((w, 2 * S5_BS)), _full((2 * S5_BS, w)), _full((SUBLANES, S5_N)), _full((1, w)),
                   _full((w, w)), _full((1, w))],
        out_shape=[jax.ShapeDtypeStruct((t, w), BF16), acc(w, 2 * S5_BS), acc(2 * S5_BS, w), acc(SUBLANES, S5_N),
                   acc(1, w), acc(w, w), acc(1, w)],
        scratch_shapes=_s5_scratch(tm),
        operands=(dyb, ys5, u5, s, s, bbc, ccc, ab8, d_row, wglu, bglu_row), comm=comm)


def s5_discretize(a_re, a_im, log_dt, b_re, b_im, *, name):
    n = a_re.shape[0]

    def body(ar_ref, ai_ref, dt_ref, br_ref, bi_ref, ab_ref, bbr_ref, bbi_ref):
        ar, ai, dtv = ar_ref[...], ai_ref[...], jnp.exp(dt_ref[...])
        mag = jnp.exp(ar * dtv)
        abr = mag * jnp.cos(ai * dtv)
        abi = mag * jnp.sin(ai * dtv)
        den = ar * ar + ai * ai
        nr = abr - 1.0
        cr = (nr * ar + abi * ai) / den
        ci = (abi * ar - nr * ai) / den
        ab_ref[:, 0:1] = abr
        ab_ref[:, 1:2] = abi
        br, bi = br_ref[...], bi_ref[...]
        bbr_ref[...] = cr * br - ci * bi
        bbi_ref[...] = cr * bi + ci * br

    col = jax.ShapeDtypeStruct((n, 2), F32)
    mat = jax.ShapeDtypeStruct((n, S5_GROUP), F32)
    return pl.pallas_call(body, name=name, out_shape=[col, mat, mat])(a_re, a_im, log_dt, b_re, b_im)


def s5_discretize_bwd(a_re, a_im, log_dt, b_re, b_im, dab, dbb_re, dbb_im, *, name):
    n = a_re.shape[0]

    def body(ar_ref, ai_ref, dt_ref, br_ref, bi_ref, dab_ref, dbr_ref, dbi_ref, da_ref, ddt_ref, gbr_ref, gbi_ref):
        ar, ai, dtv = ar_ref[...], ai_ref[...], jnp.exp(dt_ref[...])
        mag = jnp.exp(ar * dtv)
        abr = mag * jnp.cos(ai * dtv)
        abi = mag * jnp.sin(ai * dtv)
        den = ar * ar + ai * ai
        nr = abr - 1.0
        cr = (nr * ar + abi * ai) / den
        ci = (abi * ar - nr * ai) / den
        br, bi = br_ref[...], bi_ref[...]
        dbr, dbi = dbr_ref[...], dbi_ref[...]
        gbr_ref[...] = cr * dbr + ci * dbi
        gbi_ref[...] = cr * dbi - ci * dbr
        gcr = jnp.sum(dbr * br + dbi * bi, axis=1, keepdims=True)
        gci = jnp.sum(dbi * br - dbr * bi, axis=1, keepdims=True)
        ilr, ili = ar / den, -ai / den
        gabr = dab_ref[:, 0:1] + (ilr * gcr + ili * gci)
        gabi = dab_ref[:, 1:2] + (ilr * gci - ili * gcr)
        t1r, t1i = dtv * abr, dtv * abi
        t2r, t2i = -(cr * ilr - ci * ili), -(cr * ili + ci * ilr)
        da_ref[:, 0:1] = (t1r * gabr + t1i * gabi) + (t2r * gcr + t2i * gci)
        da_ref[:, 1:2] = (t1r * gabi - t1i * gabr) + (t2r * gci - t2i * gcr)
        lr, li = ar * abr - ai * abi, ar * abi + ai * abr
        dlog = (lr * gabr + li * gabi) * dtv
        ddt_ref[...] = jnp.sum(dlog.reshape(S5_GROUPS, S5_STATE, 1), axis=1)

    col2 = jax.ShapeDtypeStruct((n, 2), F32)
    col1 = jax.ShapeDtypeStruct((S5_GROUPS, 1), F32)
    mat = jax.ShapeDtypeStruct((n, S5_GROUP), F32)
    return pl.pallas_call(body, name=name, out_shape=[col2, col1, mat, mat])(
        a_re, a_im, log_dt, b_re, b_im, dab, dbb_re, dbb_im)


def merge_fwd(ya, yb, graw, x, wb, wout, *, tm, name):
    t, d = x.shape

    def body(ya_ref, yb_ref, g_ref, x_ref, wb_ref, wo_ref, x1_ref, pa_ref, pb_ref):
        pa = _dot(ya_ref[...], wb_ref[0:d, :])
        pb = _dot(yb_ref[...], wb_ref[d:, :])
        gate = _sigmoid(g_ref[...].astype(F32))
        merged = gate[:, :d] * pa + gate[:, d:] * pb
        x1_ref[...] = x_ref[...] + _dot(merged.astype(BF16), wo_ref[...])
        pa_ref[...] = pa.astype(BF16)
        pb_ref[...] = pb.astype(BF16)

    tile = lambda c: pl.BlockSpec((tm, c), lambda i: (i, 0))
    sds = jax.ShapeDtypeStruct
    return pl.pallas_call(
        body, name=name, grid=(t // tm,),
        in_specs=[tile(d), tile(S5_WIDTH), tile(2 * d), tile(d), _resident((d + S5_WIDTH, d)), _resident((d, d))],
        out_specs=[tile(d), tile(d), tile(d)], out_shape=[sds((t, d), F32), sds((t, d), BF16), sds((t, d), BF16)],
        compiler_params=_params("parallel"),
    )(ya, yb, graw, x, wb, wout)


def merge_bwd(dx1b, graw, pa, pb, wb, wout, *, tm, name):
    t, d = pa.shape

    def body(dx_ref, g_ref, pa_ref, pb_ref, wb_ref, wo_ref,
             mg_ref, dg_ref, dpa_ref, dpb_ref, dya_ref, dyb_ref):
        dm = _dot_nt(dx_ref[...], wo_ref[...])
        gate = _sigmoid(g_ref[...].astype(F32))
        g0, g1 = gate[:, :d], gate[:, d:]
        pa, pb = pa_ref[...].astype(F32), pb_ref[...].astype(F32)
        mg_ref[...] = (g0 * pa + g1 * pb).astype(BF16)
        dg_ref[:, :d] = (dm * pa * g0 * (1.0 - g0)).astype(BF16)
        dg_ref[:, d:] = (dm * pb * g1 * (1.0 - g1)).astype(BF16)
        dpa = (dm * g0).astype(BF16)
        dpb = (dm * g1).astype(BF16)
        dpa_ref[...] = dpa
        dpb_ref[...] = dpb
        dya_ref[...] = _dot_nt(dpa, wb_ref[0:d, :])
        dyb_ref[...] = _dot_nt(dpb, wb_ref[d:, :])

    tile = lambda c: pl.BlockSpec((tm, c), lambda i: (i, 0))
    sds = jax.ShapeDtypeStruct
    return pl.pallas_call(
        body, name=name, grid=(t // tm,),
        in_specs=[tile(d), tile(2 * d), tile(d), tile(d), _resident((d + S5_WIDTH, d)), _resident((d, d))],
        out_specs=[tile(d), tile(2 * d), tile(d), tile(d), tile(d), tile(S5_WIDTH)],
        out_shape=[sds((t, d), BF16), sds((t, 2 * d), BF16), sds((t, d), BF16), sds((t, d), BF16),
                   sds((t, d), F32), sds((t, S5_WIDTH), F32)],
        compiler_params=_params("parallel"),
    )(dx1b, graw, pa, pb, wb, wout)


def mlp_out_loss(a1, x1, w2, gf, target, *, tm, name):
    t, d = x1.shape
    f = a1.shape[1]

    def body(a_ref, x_ref, w_ref, g_ref, tg_ref, dx_ref, dxb_ref, loss_ref, dg_ref):
        i = pl.program_id(0)
        av = jnp.maximum(a_ref[...].astype(F32), 0.0)
        x2 = x_ref[...] + _dot((av * av).astype(BF16), w_ref[...])
        r = lax.rsqrt(jnp.mean(x2 * x2, axis=-1, keepdims=True) + EPS)
        xn = x2 * r
        err = xn * g_ref[...] - tg_ref[...]
        dyf = err * (1.0 / d)
        dxn = dyf * g_ref[...]
        dx = r * (dxn - xn * jnp.mean(dxn * xn, axis=-1, keepdims=True))
        dx_ref[...] = dx
        dxb_ref[...] = dx.astype(BF16)

        @pl.when(i == 0)
        def _():
            loss_ref[...] = jnp.zeros_like(loss_ref)
            dg_ref[...] = jnp.zeros_like(dg_ref)

        loss_ref[...] += jnp.sum(err * err) * (0.5 / d)
        dg_ref[...] += jnp.sum(dyf * xn, axis=0, keepdims=True)

    tile = lambda c: pl.BlockSpec((tm, c), lambda i: (i, 0))
    sds = jax.ShapeDtypeStruct
    return pl.pallas_call(
        body, name=name, grid=(t // tm,),
        in_specs=[tile(f), tile(d), _resident((f, d)), _full((1, d)), tile(d)],
        out_specs=[tile(d), tile(d), _full((1, LANES)), _full((1, d))],
        out_shape=[sds((t, d), F32), sds((t, d), BF16), sds((1, LANES), F32), sds((1, d), F32)],
        compiler_params=_params("arbitrary"),
    )(a1, x1, w2, gf.reshape(1, d), target)


def mlp_bwd_act(dx2b, a1, w2, *, tm, name):
    t, f = a1.shape
    d = dx2b.shape[1]

    def body(dx_ref, a_ref, w_ref, o_ref):
        dact = _dot_nt(dx_ref[...], w_ref[...])
        o_ref[...] = (dact * 2.0 * jnp.maximum(a_ref[...].astype(F32), 0.0)).astype(BF16)

    tile = lambda c: pl.BlockSpec((tm, c), lambda i: (i, 0))
    return pl.pallas_call(
        body, name=name, grid=(t // tm,),
        in_specs=[tile(d), tile(f), _resident((f, d))], out_specs=tile(f),
        out_shape=jax.ShapeDtypeStruct((t, f), BF16),
        compiler_params=_params("parallel"),
    )(dx2b, a1, w2)


ADAM_TILE_ELEMS = 128 * 1024


def _row_tile(rows, cap):
    if rows <= cap:
        return rows
    best = None
    for c in range(16, cap + 1, 16):
        if rows % c == 0:
            best = c
    assert best is not None, rows
    return best


def _device_order_sum(parts_ref):
    total = parts_ref[0].astype(F32)
    for k in range(1, N_DEV):
        total = total + parts_ref[k].astype(F32)
    return total


def _adamw_math(w, m, v, gparts_ref):
    g = _device_order_sum(gparts_ref)
    mn = ADAM_B1 * m + (1.0 - ADAM_B1) * g
    vn = ADAM_B2 * v + (1.0 - ADAM_B2) * (g * g)
    m_hat = mn / (1.0 - ADAM_B1 ** ADAM_STEP)
    v_hat = vn / (1.0 - ADAM_B2 ** ADAM_STEP)
    return g, -ADAM_LR * (m_hat / (jnp.sqrt(v_hat) + ADAM_EPS) + ADAM_WD * w), mn, vn


def adamw(w, m, v, gparts, *, name):
    rows, cols = w.shape
    tr = _row_tile(rows, max(16, ADAM_TILE_ELEMS // cols))

    def body(w_ref, m_ref, v_ref, g_ref, go_ref, d_ref, mo_ref, vo_ref):
        go_ref[...], d_ref[...], mo_ref[...], vo_ref[...] = _adamw_math(w_ref[...], m_ref[...], v_ref[...], g_ref)

    tile = pl.BlockSpec((tr, cols), lambda i: (i, 0))
    out = jax.ShapeDtypeStruct((rows, cols), F32)
    return pl.pallas_call(
        body, name=name, grid=(rows // tr,),
        in_specs=[tile, tile, tile, pl.BlockSpec((N_DEV, tr, cols), lambda i: (0, i, 0))],
        out_specs=[tile, tile, tile, tile], out_shape=[out, out, out, out],
        compiler_params=_params("parallel"),
    )(w, m, v, gparts)


def all_gather_arrays(blocks, *, name):
    na = len(blocks)

    def body(*refs):
        x_refs, out_refs = refs[:na], refs[na:2 * na]
        send_sems, recv_sems, local_sems = refs[2 * na:]
        x, y, c, _ = _place()
        me, sibling = (x, y, c), (x, y, 1 - c)
        chips = [(1 - x, y), (x, 1 - y), (1 - x, 1 - y)]

        def copy(a, k, blk, to, own=False):
            px, py, pc = blk
            dst = out_refs[a].at[4 * px + 2 * py + pc]
            return pltpu.make_async_remote_copy(
                src_ref=x_refs[a] if own else dst, dst_ref=dst,
                send_sem=send_sems.at[7 * a + k], recv_sem=recv_sems.at[7 * a + k], device_id=to, device_id_type=_MESH)

        mine = [pltpu.make_async_copy(x_refs[a], out_refs[a].at[4 * x + 2 * y + c], local_sems.at[a]) for a in range(na)]
        for cp in mine:
            cp.start()
        sent = []
        for a in range(na):
            first = [copy(a, 0, me, sibling, own=True)]
            first += [copy(a, 1 + j, me, (*chip, c), own=True) for j, chip in enumerate(chips)]
            for cp in first:
                cp.start()
            sent += first
        for a in range(na):
            for j, chip in enumerate(chips):
                copy(a, 1 + j, (*chip, c), me).wait_recv()
                fwd = copy(a, 4 + j, (*chip, c), sibling)
                fwd.start()
                sent.append(fwd)
        for a in range(na):
            copy(a, 0, sibling, me).wait_recv()
            for j, chip in enumerate(chips):
                copy(a, 4 + j, (*chip, 1 - c), me).wait_recv()
        for cp in sent:
            cp.wait_send()
        for cp in mine:
            cp.wait()

    any_spec = pl.BlockSpec(memory_space=pl.ANY)
    return pl.pallas_call(
        body, name=name, in_specs=[any_spec] * na, out_specs=[any_spec] * na,
        out_shape=[jax.ShapeDtypeStruct((N_DEV,) + b.shape, b.dtype) for b in blocks],
        scratch_shapes=[pltpu.SemaphoreType.DMA((7 * na,)), pltpu.SemaphoreType.DMA((7 * na,)),
                        pltpu.SemaphoreType.DMA((na,))],
        compiler_params=pltpu.CompilerParams(has_side_effects=True),
    )(*blocks)


def all_to_all(comm, *, name):
    na = len(comm)
    kinds = [k for k, _ in comm]

    def body(*refs):
        local, remote = _comm_copies(kinds, refs[:na], refs[na:2 * na], *refs[2 * na:])
        for cp in local + remote:
            cp.start()
        for cp in remote:
            cp.wait_recv()
        for cp in remote:
            cp.wait_send()
        for cp in local:
            cp.wait()

    any_spec = pl.BlockSpec(memory_space=pl.ANY)
    return pl.pallas_call(
        body, name=name, in_specs=[any_spec] * na, out_specs=[any_spec] * na,
        out_shape=_comm_out_shapes(comm),
        scratch_shapes=[pltpu.SemaphoreType.DMA((na * N_REL,)), pltpu.SemaphoreType.DMA((na * N_REL,)),
                        pltpu.SemaphoreType.DMA((na,))],
        compiler_params=pltpu.CompilerParams(has_side_effects=True),
    )(*[a for _, a in comm])


def adamw_whole(ws, ms, vs, gparts, *, name, totals=()):
    n, nt = len(ws), len(totals)

    def body(*refs):
        w_refs, m_refs, v_refs, g_refs = refs[:n], refs[n:2 * n], refs[2 * n:3 * n], refs[3 * n:4 * n]
        t_refs, refs = refs[4 * n:4 * n + nt], refs[4 * n + nt:]
        outs, t_outs = refs[:4 * n], refs[4 * n:]
        for k in range(n):
            g, delta, mn, vn = _adamw_math(w_refs[k][...], m_refs[k][...], v_refs[k][...], g_refs[k])
            outs[k][...] = g
            outs[n + k][...] = delta
            outs[2 * n + k][...] = mn
            outs[3 * n + k][...] = vn
        for t_ref, o_ref in zip(t_refs, t_outs):
            o_ref[...] = _device_order_sum(t_ref)

    shapes = [jax.ShapeDtypeStruct(w.shape, F32) for w in ws]
    res = pl.pallas_call(
        body, name=name, out_shape=shapes * 4 + [jax.ShapeDtypeStruct(t.shape[1:], F32) for t in totals],
        compiler_params=pltpu.CompilerParams(vmem_limit_bytes=VMEM_LIMIT),
    )(*ws, *ms, *vs, *gparts, *totals)
    return [res[j * n:(j + 1) * n] for j in range(4)], res[4 * n:]


def _lane_row(v):
    return jnp.pad(v.astype(F32), (0, LANES - v.shape[0])).reshape(1, LANES)


def _block_diag_b(bb):
    eye = jnp.eye(SUBLANES, dtype=F32)
    bt = bb.reshape(S5_BLOCKS, 8, S5_STATE, S5_GROUP).transpose(0, 1, 3, 2)
    return (bt[:, :, :, None, :] * eye[None, :, None, :, None]).reshape(S5_WIDTH, S5_BS)


def _block_diag_c(cc):
    eye = jnp.eye(SUBLANES, dtype=F32)
    ct = cc.reshape(S5_BLOCKS, 8, S5_GROUP, S5_STATE).transpose(3, 0, 1, 2)
    return (ct[None, :, :, :, :] * eye[:, None, None, :, None]).reshape(S5_BS, S5_WIDTH)


def _diag_blocks_b(m):
    eye = jnp.eye(SUBLANES, dtype=F32)
    m5 = m.reshape(S5_BLOCKS, 8, S5_GROUP, 8, S5_STATE)
    return jnp.sum(m5 * eye[None, :, None, :, None], axis=3).transpose(0, 1, 3, 2).reshape(S5_GROUPS, S5_STATE, S5_GROUP)


def _diag_blocks_c(m):
    eye = jnp.eye(SUBLANES, dtype=F32)
    m5 = m.reshape(8, S5_STATE, S5_BLOCKS, 8, S5_GROUP)
    return jnp.sum(m5 * eye[:, None, None, :, None], axis=0).transpose(1, 2, 3, 0).reshape(S5_GROUPS, S5_GROUP, S5_STATE)


def train_step(x, target, p, m, v):
    t = x.shape[0]
    tm = min(256, t)
    tb = min(512, t)
    ts = min(512, t)
    bf = lambda n: p[n].astype(BF16)
    slots = lambda n, a: a.reshape((N_DEV,) + _shard_shape(n))
    updated = {}

    def update(names, got):
        for n, parts in zip(names, got):
            updated[n] = adamw(p[n], m[n], v[n], parts, name=f"adamw_{n}")

    st_in, st_conv = all_gather_arrays([bf("w_in"), p["conv_w"]], name="gather_first")
    win_p = jnp.concatenate([st_in[k][:, a:b] for seg in IN_PADDED_ORDER for k, a, b in _shard_pieces(*IN_SEGMENTS[seg])]
                            + [jnp.zeros((D_MODEL, LANES - SSD_HEADS), BF16)], axis=1)
    conv_w8 = jnp.pad(_join_shards("conv_w", st_conv), ((0, SUBLANES - SSD_CONV), (0, 0)))
    (h, xbc_raw, graw, u5, z, dtr), (st_branch, st_out, st_glu) = rms_matmul(
        x, p["norm_mix_g"], win_p, IN_SPLITS, tm=tb, name="in_proj", out_dtypes=(BF16, BF16, BF16, BF16, F32),
        comm=gathers([bf("w_branch"), bf("w_out"), bf("s5_glu_w")]))
    w_branch = st_branch.reshape(D_MODEL + S5_WIDTH, D_MODEL)
    wout, wglu = st_out.reshape(D_MODEL, D_MODEL), st_glu.reshape(S5_WIDTH, S5_WIDTH)
    bias_row, alog_row = _lane_row(p["dt_bias"]), _lane_row(p["a_log"])
    d_row = jnp.repeat(p["d_ssd"], SSD_HEADDIM).reshape(1, SSD_HEADS * SSD_HEADDIM)
    (ya, yssd, states), (w1,) = ssd_fwd(xbc_raw, conv_w8, p["conv_b"], dtr, z, bias_row, alog_row, d_row,
                                        p["ssd_norm_g"], name="ssd_fwd", comm=gathers([bf("w_mlp_in")]))

    n = S5_N
    a_re_c, a_im_c = p["s5_a_re"].reshape(n, 1), p["s5_a_im"].reshape(n, 1)
    dt_c = jnp.repeat(p["s5_log_dt"], S5_STATE).reshape(n, 1)
    b_re_m, b_im_m = p["s5_b_re"].reshape(n, S5_GROUP), p["s5_b_im"].reshape(n, S5_GROUP)
    ab, bb_re, bb_im = s5_discretize(a_re_c, a_im_c, dt_c, b_re_m, b_im_m, name="s5_disc")
    ab8 = jnp.pad(ab.T, ((0, SUBLANES - 2), (0, 0)))
    bbc = jnp.concatenate([_block_diag_b(bb_re.reshape(S5_GROUPS, S5_STATE, S5_GROUP)),
                           _block_diag_b(bb_im.reshape(S5_GROUPS, S5_STATE, S5_GROUP))], axis=1).astype(BF16)
    ccc = jnp.concatenate([_block_diag_c(p["s5_c_re"]), -_block_diag_c(p["s5_c_im"])], axis=0).astype(BF16)
    s5d_row = p["s5_d"].reshape(1, S5_WIDTH)
    bglu_row = p["s5_glu_b"].reshape(1, S5_WIDTH)
    u5 = s5_interleave(u5, ts)
    (s, ys5, yb), (st_w2,) = s5_fwd(u5, bbc, ccc, ab8, s5d_row, wglu, bglu_row, tm=ts, name="s5_fwd",
                                    comm=gathers([bf("w_mlp_out")]))
    yb = s5_deinterleave(yb, ts)
    w2 = st_w2.reshape(D_FF, D_MODEL)

    x1, pa, pb = merge_fwd(ya, yb, graw, x, w_branch, wout, tm=tb, name="merge_fwd")
    (h2, a1), _ = rms_matmul(x1, p["norm_mlp_g"], w1, (D_FF,), tm=tb, name="mlp_in", out_dtypes=(BF16,))
    dx2, dx2b, loss_row, dgf = mlp_out_loss(a1, x1, w2, p["norm_final_g"], target, tm=tb, name="mlp_out_loss")

    g = {}
    g["norm_final_g"] = dgf[0]
    da1 = mlp_bwd_act(dx2b, a1, w2, tm=tb, name="mlp_bwd_act")
    dw2 = slots("w_mlp_out", matmul_tn(a1, dx2b, out_dtype=BF16, a_relu2=True, name="dw_mlp_out"))
    dw1 = matmul_tn(h2, da1, out_dtype=BF16, col_shards=N_DEV, name="dw_mlp_in")
    (dx1, dx1b, dgm), _ = matmul_nt_rms_bwd([da1], w1, x1, p["norm_mlp_g"], dx2, tm=tb, name="mlp_in_bwd")
    g["norm_mlp_g"] = dgm[0]
    merged, dgraw, dpa, dpb, dya, dyb = merge_bwd(dx1b, graw, pa, pb, w_branch, wout, tm=tm, name="merge_bwd")
    dwo = slots("w_out", matmul_tn(merged, dx1b, out_dtype=BF16, name="dw_out"))
    dwb = slots("w_branch", jnp.concatenate([matmul_tn(ya, dpa, out_dtype=BF16, name="dw_branch_a"),
                                             matmul_tn(yb, dpb, out_dtype=BF16, name="dw_branch_b")], axis=0))

    (du5, dbbc, dccc, dab, dd5, dwglu, dbglu), got = s5_bwd(
        s5_interleave(dyb, ts), ys5, u5, s, bbc, ccc, ab8, s5d_row, wglu, bglu_row, tm=ts, name="s5_bwd",
        comm=exchanges([dw2]))
    du5 = s5_deinterleave(du5, ts)
    update(["w_mlp_out"], got)
    g["s5_glu_b"], g["s5_d"] = dbglu[0], dd5[0]
    g["s5_c_re"] = _diag_blocks_c(dccc[:S5_BS])
    g["s5_c_im"] = -_diag_blocks_c(dccc[S5_BS:])
    dbb_re = _diag_blocks_b(dbbc[:, :S5_BS]).reshape(n, S5_GROUP)
    dbb_im = _diag_blocks_b(dbbc[:, S5_BS:]).reshape(n, S5_GROUP)
    da, dlogdt, gb_re, gb_im = s5_discretize_bwd(a_re_c, a_im_c, dt_c, b_re_m, b_im_m, dab[:2].T, dbb_re, dbb_im,
                                                  name="s5_disc_bwd")
    g["s5_a_re"] = da[:, 0].reshape(S5_GROUPS, S5_STATE)
    g["s5_a_im"] = da[:, 1].reshape(S5_GROUPS, S5_STATE)
    g["s5_log_dt"] = dlogdt[:, 0]
    g["s5_b_re"] = gb_re.reshape(S5_GROUPS, S5_STATE, S5_GROUP)
    g["s5_b_im"] = gb_im.reshape(S5_GROUPS, S5_STATE, S5_GROUP)

    def update_small(names, got, name, totals=()):
        (gs, ds, nm, nv), sums = adamw_whole([p[n] for n in names], [m[n] for n in names], [v[n] for n in names], got,
                                             name=name, totals=totals)
        for k, n in enumerate(names):
            updated[n] = (gs[k], ds[k], nm[k], nv[k])
        return sums

    wide = ["s5_b_re", "s5_b_im"]
    late = ["ssd_norm_g", "d_ssd", "a_log", "dt_bias", "conv_b"]
    early = [n for n in SMALL_ORDER if n not in wide + late and n != "norm_mix_g"]
    (dzdt, dxbc_raw, dng, dd_row, dalog_row, dbias_row, dconv_b, dconv_w8), got = ssd_bwd(
        dya, yssd, z, xbc_raw, conv_w8, p["conv_b"], dtr, states, bias_row, alog_row, d_row, p["ssd_norm_g"],
        name="ssd_bwd",
        comm=exchanges([dwo, dwb, slots("s5_glu_w", dwglu.astype(BF16)), dw1])
        + gathers([g[n] for n in wide + early] + [loss_row]))
    update(["w_out", "w_branch", "s5_glu_w", "w_mlp_in"], got[:4])
    update_small(wide, got[4:4 + len(wide)], "adamw_s5_b")
    loss_sum, = update_small(early, got[4 + len(wide):-1], "adamw_small_early", totals=[got[-1]])
    g["ssd_norm_g"] = dng[0]
    g["d_ssd"], g["a_log"], g["dt_bias"] = dd_row[0, :SSD_HEADS], dalog_row[0, :SSD_HEADS], dbias_row[0, :SSD_HEADS]
    g["conv_b"] = dconv_b[0]
    dconv = dconv_w8[:SSD_CONV].reshape(SSD_CONV, N_DEV, CONV_DIM // N_DEV).transpose(1, 0, 2)
    dproj = [dxbc_raw, dgraw, du5, dzdt]
    dxbc_w, dgate_w, du5_w, dzdt_w = [matmul_tn(h, piece, out_dtype=BF16, name=f"dw_in_{k}")
                                      for k, piece in enumerate(dproj)]
    by_segment = {"z": dzdt_w[:, :D_MODEL], "xbc": dxbc_w, "dt": dzdt_w[:, D_MODEL:], "u5": du5_w, "gates": dgate_w}
    width = D_IN_PROJ // N_DEV
    dw_in = jnp.stack([jnp.concatenate(
        [by_segment[seg][:, max(k * width, lo) - lo:min((k + 1) * width, hi) - lo]
         for seg, (lo, hi) in IN_SEGMENTS.items() if max(k * width, lo) < min((k + 1) * width, hi)], axis=1)
        for k in range(N_DEV)])

    (grad_x, dgx), got = matmul_nt_rms_bwd(
        dproj, win_p, x, p["norm_mix_g"], dx1, tm=tm, name="in_proj_bwd", bf16_copy=False,
        comm=exchanges([dw_in, dconv]) + gathers([g[n] for n in late]))
    update(["w_in", "conv_w"], got[:2])
    update_small(late, got[2:], "adamw_small_late")
    update_small(["norm_mix_g"], all_to_all(gathers([dgx[0]]), name="gather_last"), "adamw_small_last")
    return loss_sum[0, 0], grad_x, updated


WEIGHT_ORDER = ["norm_mix_g", "w_in", "conv_w", "conv_b", "dt_bias", "a_log", "d_ssd", "ssd_norm_g", "s5_a_re",
                "s5_a_im", "s5_log_dt", "s5_b_re", "s5_b_im", "s5_c_re", "s5_c_im", "s5_d", "s5_glu_w", "s5_glu_b",
                "w_branch", "w_out", "norm_mlp_g", "w_mlp_in", "w_mlp_out", "norm_final_g"]
SHARDED = {"w_in": ((D_MODEL, D_IN_PROJ), 1), "conv_w": ((SSD_CONV, CONV_DIM), 1), "s5_glu_w": ((S5_WIDTH, S5_WIDTH), 0),
           "w_branch": ((D_MODEL + S5_WIDTH, D_MODEL), 0), "w_out": ((D_MODEL, D_MODEL), 0),
           "w_mlp_in": ((D_MODEL, D_FF), 1), "w_mlp_out": ((D_FF, D_MODEL), 0)}
SMALL_ORDER = [n for n in WEIGHT_ORDER if n not in SHARDED]


def _shard_shape(name):
    (r, c), ax = SHARDED[name]
    return (r, c // N_DEV) if ax == 1 else (r // N_DEV, c)


def _join_shards(name, stacked):
    (r, c), ax = SHARDED[name]
    if ax == 1:
        return stacked.transpose(1, 0, 2).reshape(r, c)
    return stacked.reshape(r, c)


def _shard_pieces(lo, hi):
    width = D_IN_PROJ // N_DEV
    out = []
    while lo < hi:
        k = lo // width
        end = min(hi, (k + 1) * width)
        out.append((k, lo - k * width, end - k * width))
        lo = end
    return out


IN_SEGMENTS = {"z": (0, 1024), "xbc": (1024, 3072), "dt": (3072, 3088), "u5": (3088, 3600), "gates": (3600, 5648)}
IN_PADDED_ORDER = ("xbc", "gates", "u5", "z", "dt")


def kernel(x, norm_mix_g, w_in, conv_w, conv_b, dt_bias, a_log, d_ssd, ssd_norm_g, s5_a_re, s5_a_im, s5_log_dt, s5_b_re, s5_b_im, s5_c_re, s5_c_im, s5_d, s5_glu_w, s5_glu_b, w_branch, w_out, norm_mlp_g, w_mlp_in, w_mlp_out, norm_final_g, loss_target, m_norm_mix_g, m_w_in, m_conv_w, m_conv_b, m_dt_bias, m_a_log, m_d_ssd, m_ssd_norm_g, m_s5_a_re, m_s5_a_im, m_s5_log_dt, m_s5_b_re, m_s5_b_im, m_s5_c_re, m_s5_c_im, m_s5_d, m_s5_glu_w, m_s5_glu_b, m_w_branch, m_w_out, m_norm_mlp_g, m_w_mlp_in, m_w_mlp_out, m_norm_final_g, v_norm_mix_g, v_w_in, v_conv_w, v_conv_b, v_dt_bias, v_a_log, v_d_ssd, v_ssd_norm_g, v_s5_a_re, v_s5_a_im, v_s5_log_dt, v_s5_b_re, v_s5_b_im, v_s5_c_re, v_s5_c_im, v_s5_d, v_s5_glu_w, v_s5_glu_b, v_w_branch, v_w_out, v_norm_mlp_g, v_w_mlp_in, v_w_mlp_out, v_norm_final_g):
    w = dict(norm_mix_g=norm_mix_g, w_in=w_in, conv_w=conv_w, conv_b=conv_b, dt_bias=dt_bias, a_log=a_log, d_ssd=d_ssd,
             ssd_norm_g=ssd_norm_g, s5_a_re=s5_a_re, s5_a_im=s5_a_im, s5_log_dt=s5_log_dt, s5_b_re=s5_b_re,
             s5_b_im=s5_b_im, s5_c_re=s5_c_re, s5_c_im=s5_c_im, s5_d=s5_d, s5_glu_w=s5_glu_w, s5_glu_b=s5_glu_b,
             w_branch=w_branch, w_out=w_out, norm_mlp_g=norm_mlp_g, w_mlp_in=w_mlp_in, w_mlp_out=w_mlp_out,
             norm_final_g=norm_final_g)
    m = dict(norm_mix_g=m_norm_mix_g, w_in=m_w_in, conv_w=m_conv_w, conv_b=m_conv_b, dt_bias=m_dt_bias, a_log=m_a_log,
             d_ssd=m_d_ssd, ssd_norm_g=m_ssd_norm_g, s5_a_re=m_s5_a_re, s5_a_im=m_s5_a_im, s5_log_dt=m_s5_log_dt,
             s5_b_re=m_s5_b_re, s5_b_im=m_s5_b_im, s5_c_re=m_s5_c_re, s5_c_im=m_s5_c_im, s5_d=m_s5_d,
             s5_glu_w=m_s5_glu_w, s5_glu_b=m_s5_glu_b, w_branch=m_w_branch, w_out=m_w_out, norm_mlp_g=m_norm_mlp_g,
             w_mlp_in=m_w_mlp_in, w_mlp_out=m_w_mlp_out, norm_final_g=m_norm_final_g)
    v = dict(norm_mix_g=v_norm_mix_g, w_in=v_w_in, conv_w=v_conv_w, conv_b=v_conv_b, dt_bias=v_dt_bias, a_log=v_a_log,
             d_ssd=v_d_ssd, ssd_norm_g=v_ssd_norm_g, s5_a_re=v_s5_a_re, s5_a_im=v_s5_a_im, s5_log_dt=v_s5_log_dt,
             s5_b_re=v_s5_b_re, s5_b_im=v_s5_b_im, s5_c_re=v_s5_c_re, s5_c_im=v_s5_c_im, s5_d=v_s5_d,
             s5_glu_w=v_s5_glu_w, s5_glu_b=v_s5_glu_b, w_branch=v_w_branch, w_out=v_w_out, norm_mlp_g=v_norm_mlp_g,
             w_mlp_in=v_w_mlp_in, w_mlp_out=v_w_mlp_out, norm_final_g=v_norm_final_g)

    loss, grad_x, upd = train_step(x[0], loss_target[0], w, m, v)
    return (loss, grad_x.reshape(x.shape), *[upd[n][j] for j in range(4) for n in WEIGHT_ORDER])
```

```python
import math

import jax
import jax.numpy as jnp
from jax import lax
from jax.experimental import pallas as pl
from jax.experimental.pallas import tpu as pltpu

F32 = jnp.float32
BF16 = jnp.bfloat16

N_DEV = 8
D_MODEL = 1024
SSD_HEADS = 16
SSD_HEADDIM = 64
SSD_GROUPS = 4
SSD_HPG = 4
SSD_STATE = 128
SSD_CHUNK = 128
SSD_CONV = 4
CONV_DIM = 2048
S5_WIDTH = 512
S5_GROUP = 16
S5_GROUPS = 32
S5_STATE = 64
S5_N = S5_GROUPS * S5_STATE
D_FF = 4096
D_IN_PROJ = 5648
IN_SPLITS = (2048, 2048, 512, 1024, 128)
EPS = 1e-6
LANES = 128
SUBLANES = 8
VMEM_LIMIT = 56 * 1024 * 1024

ADAM_LR = 0.001
ADAM_B1 = 0.9
ADAM_B2 = 0.999
ADAM_EPS = 1e-08
ADAM_WD = 0.01
ADAM_STEP = 10

GELU_C = math.sqrt(2.0 / math.pi)
GELU_K = 0.044715


def _params(*sem):
    return pltpu.CompilerParams(dimension_semantics=sem, vmem_limit_bytes=VMEM_LIMIT)


def _full(shape):
    nd = len(shape)
    return pl.BlockSpec(shape, lambda *_: (0,) * nd)


def _resident(shape):
    nd = len(shape)
    return pl.BlockSpec(shape, lambda *_: (0,) * nd, pipeline_mode=pl.Buffered(1))


_MESH = pl.DeviceIdType.MESH
_RELATIONS = [(dx, dy, dc) for dx in (0, 1) for dy in (0, 1) for dc in (0, 1)][1:]
N_REL = len(_RELATIONS)


def _place():
    x, y, c = lax.axis_index("x"), lax.axis_index("y"), lax.axis_index("c")
    return x, y, c, 4 * x + 2 * y + c


def gathers(arrays):
    return [("gather", a) for a in arrays]


def exchanges(arrays):
    return [("exchange", a) for a in arrays]


def _comm_out_shapes(comm):
    return [jax.ShapeDtypeStruct(((N_DEV,) if kind == "gather" else ()) + a.shape, a.dtype) for kind, a in comm]


def _comm_copies(kinds, src_refs, dst_refs, send_sems, recv_sems, local_sems):
    x, y, c, idx = _place()
    local, remote = [], []
    for a, (kind, src, dst) in enumerate(zip(kinds, src_refs, dst_refs)):
        local.append(pltpu.make_async_copy(src if kind == "gather" else src.at[idx], dst.at[idx], local_sems.at[a]))
        for k, (dx, dy, dc) in enumerate(_RELATIONS):
            px, py, pc = x ^ dx, y ^ dy, c ^ dc
            remote.append(pltpu.make_async_remote_copy(
                src_ref=src if kind == "gather" else src.at[4 * px + 2 * py + pc], dst_ref=dst.at[idx],
                send_sem=send_sems.at[N_REL * a + k], recv_sem=recv_sems.at[N_REL * a + k],
                device_id=(px, py, pc), device_id_type=_MESH))
    return local, remote


def _tiled_call(body, *, name, grid, in_specs, out_specs, out_shape, operands, scratch_shapes=(), comm=None):
    params = pltpu.CompilerParams(dimension_semantics=("arbitrary",), vmem_limit_bytes=VMEM_LIMIT,
                                  has_side_effects=bool(comm))
    if not comm:
        outs = pl.pallas_call(body, name=name, grid=grid, in_specs=in_specs, out_specs=out_specs, out_shape=out_shape,
                              scratch_shapes=list(scratch_shapes), compiler_params=params)(*operands)
        return outs, []
    kind = [k for k, _ in comm]
    arrays = [a for _, a in comm]
    na, n_in, n_out, n_scr = len(arrays), len(in_specs), len(out_specs), len(scratch_shapes)
    last = grid[0] - 1

    def hosted(*refs):
        ins, refs = refs[:n_in], refs[n_in:]
        cin, refs = refs[:na], refs[na:]
        outs, refs = refs[:n_out], refs[n_out:]
        cout, refs = refs[:na], refs[na:]
        scr, sems = refs[:n_scr], refs[n_scr:]
        i = pl.program_id(0)

        @pl.when(i == 0)
        def _():
            local, remote = _comm_copies(kind, cin, cout, *sems)
            for cp in local + remote:
                cp.start()

        body(*ins, *outs, *scr)

        @pl.when(i == last)
        def _():
            local, remote = _comm_copies(kind, cin, cout, *sems)
            for cp in remote:
                cp.wait_recv()
            for cp in remote:
                cp.wait_send()
            for cp in local:
                cp.wait()

    any_spec = pl.BlockSpec(memory_space=pl.ANY)
    res = pl.pallas_call(
        hosted, name=name, grid=grid,
        in_specs=list(in_specs) + [any_spec] * na, out_specs=list(out_specs) + [any_spec] * na,
        out_shape=list(out_shape) + _comm_out_shapes(comm),
        scratch_shapes=list(scratch_shapes) + [pltpu.SemaphoreType.DMA((N_REL * na,)), pltpu.SemaphoreType.DMA((N_REL * na,)),
                                               pltpu.SemaphoreType.DMA((na,))],
        compiler_params=params)(*operands, *arrays)
    return res[:n_out], res[n_out:]


def _dot(a, b):
    return jnp.dot(a, b, preferred_element_type=F32)


def _dot_nt(a, b):
    return lax.dot_general(a, b, (((1,), (1,)), ((), ())), preferred_element_type=F32)


def _dot_tn(a, b):
    return lax.dot_general(a, b, (((0,), (0,)), ((), ())), preferred_element_type=F32)


def _split2(x):
    hi = x.astype(BF16)
    return hi, (x - hi.astype(F32)).astype(BF16)


def _select_dot(dot, a, b, exact):
    if exact == "a":
        ea, (hi, lo) = a.astype(BF16), _split2(b)
        return dot(ea, hi) + dot(ea, lo)
    eb, (hi, lo) = b.astype(BF16), _split2(a)
    return dot(hi, eb) + dot(lo, eb)


def _dot_sel_a(a, b):
    return _select_dot(_dot, a, b, "a")


def _dot_sel_b(a, b):
    return _select_dot(_dot, a, b, "b")


def _dot_nt_sel_a(a, b):
    return _select_dot(_dot_nt, a, b, "a")


def _dot_tn_sel(a, b, exact):
    return _select_dot(_dot_tn, a, b, exact)


def _sigmoid(x):
    return 1.0 / (1.0 + jnp.exp(-x))


def _softplus(x):
    return jnp.maximum(x, 0.0) + jnp.log(1.0 + jnp.exp(-jnp.abs(x)))


def _gelu(x):
    return 0.5 * x * (1.0 + jnp.tanh(GELU_C * (x + GELU_K * x * x * x)))


def _gelu_grad(x):
    th = jnp.tanh(GELU_C * (x + GELU_K * x * x * x))
    return 0.5 * (1.0 + th) + 0.5 * x * (1.0 - th * th) * GELU_C * (1.0 + 3.0 * GELU_K * x * x)


def rms_matmul(x, g, w, splits, *, tm, name, comm=None, out_dtypes=None):
    t, d = x.shape
    stacked = w.ndim == 3
    n = w.shape[0] * w.shape[2] if stacked else w.shape[1]
    assert sum(splits) == n and (len(splits) == 1 or not stacked)

    def body(x_ref, g_ref, w_ref, h_ref, *o_refs):
        xv = x_ref[...]
        r = lax.rsqrt(jnp.mean(xv * xv, axis=-1, keepdims=True) + EPS)
        h = (xv * r * g_ref[...]).astype(BF16)
        h_ref[...] = h
        if stacked:
            wk = w.shape[2]
            for k in range(w.shape[0]):
                o_refs[0][:, k * wk:(k + 1) * wk] = _dot(h, w_ref[k]).astype(o_refs[0].dtype)
            return
        off = 0
        for o_ref, width in zip(o_refs, splits):
            o_ref[...] = _dot(h, w_ref[:, off:off + width]).astype(o_ref.dtype)
            off += width

    tile = lambda c: pl.BlockSpec((tm, c), lambda i: (i, 0))
    dtypes = out_dtypes or (F32,) * len(splits)
    return _tiled_call(
        body, name=name, grid=(t // tm,),
        in_specs=[tile(d), _full((1, d)), _resident(w.shape)],
        out_specs=[tile(d)] + [tile(c) for c in splits],
        out_shape=[jax.ShapeDtypeStruct((t, d), BF16)] + [jax.ShapeDtypeStruct((t, c), dt) for c, dt in zip(splits, dtypes)],
        operands=(x, g.reshape(1, d), w), comm=comm)


MATMUL_TN_ACC_BYTES = 16 * 1024 * 1024


def _pick(n, cap):
    best = LANES
    for c in range(LANES, cap + 1, LANES):
        if n % c == 0:
            best = c
    return best


def matmul_tn(a, b, *, name, out_dtype=F32, col_shards=None, tk=1024, a_relu2=False):
    t, m = a.shape
    n = b.shape[1]
    tm = _pick(m, 1024)
    whole_n = tm * n * 4 <= MATMUL_TN_ACC_BYTES
    tn = n if whole_n else _pick(n, 1280)
    assert whole_n or not col_shards
    tk = min(tk if tn <= 2048 else tk // 2, t)
    nk = t // tk

    def body(a_ref, b_ref, o_ref, acc):
        k = pl.program_id(2)

        @pl.when(k == 0)
        def _():
            acc[...] = jnp.zeros_like(acc)

        av = a_ref[...]
        if a_relu2:
            pos = jnp.maximum(av.astype(F32), 0.0)
            av = (pos * pos).astype(BF16)
        acc[...] += _dot_tn(av, b_ref[...])

        @pl.when(k == nk - 1)
        def _():
            if col_shards:
                w = n // col_shards
                for s in range(col_shards):
                    o_ref[s] = acc[:, s * w:(s + 1) * w].astype(out_dtype)
            else:
                o_ref[...] = acc[...].astype(out_dtype)

    if col_shards:
        out_spec = pl.BlockSpec((col_shards, tm, n // col_shards), lambda i, j, k: (0, i, 0))
        out_shape = jax.ShapeDtypeStruct((col_shards, m, n // col_shards), out_dtype)
    else:
        out_spec = pl.BlockSpec((tm, tn), lambda i, j, k: (i, j))
        out_shape = jax.ShapeDtypeStruct((m, n), out_dtype)
    return pl.pallas_call(
        body, name=name, grid=(m // tm, n // tn, nk),
        in_specs=[pl.BlockSpec((tk, tm), lambda i, j, k: (k, i)), pl.BlockSpec((tk, tn), lambda i, j, k: (k, j))],
        out_specs=out_spec, out_shape=out_shape,
        scratch_shapes=[pltpu.VMEM((tm, tn), F32)],
        compiler_params=_params("parallel", "parallel", "arbitrary"),
    )(a, b)


def matmul_nt_rms_bwd(dys, w, x, g, dres, *, tm, name, comm=None, bf16_copy=True):
    t = x.shape[0]
    stacked = w.ndim == 3
    d = w.shape[1] if stacked else w.shape[0]
    widths = [dy.shape[1] for dy in dys]
    n_dy = len(dys)

    def body(*refs):
        dy_refs = refs[:n_dy]
        w_ref, x_ref, g_ref, dres_ref, dx_ref = refs[n_dy:n_dy + 5]
        dxb_ref = refs[n_dy + 5] if bf16_copy else None
        dg_ref = refs[-1]
        i = pl.program_id(0)
        if stacked:
            wk = w.shape[2]
            dh = _dot_nt(dy_refs[0][:, 0:wk], w_ref[0])
            for k in range(1, w.shape[0]):
                dh = dh + _dot_nt(dy_refs[0][:, k * wk:(k + 1) * wk], w_ref[k])
        else:
            dh, off = None, 0
            for dy_ref, width in zip(dy_refs, widths):
                part = _dot_nt(dy_ref[...], w_ref[:, off:off + width])
                dh = part if dh is None else dh + part
                off += width
        xv = x_ref[...]
        r = lax.rsqrt(jnp.mean(xv * xv, axis=-1, keepdims=True) + EPS)
        xn = xv * r
        dxn = dh * g_ref[...]
        dx = dres_ref[...] + r * (dxn - xn * jnp.mean(dxn * xn, axis=-1, keepdims=True))
        dx_ref[...] = dx
        if bf16_copy:
            dxb_ref[...] = dx.astype(BF16)

        @pl.when(i == 0)
        def _():
            dg_ref[...] = jnp.zeros_like(dg_ref)

        dg_ref[...] += jnp.sum(dh * xn, axis=0, keepdims=True)

    tile = lambda c: pl.BlockSpec((tm, c), lambda i: (i, 0))
    copies = [BF16] if bf16_copy else []
    return _tiled_call(
        body, name=name, grid=(t // tm,),
        in_specs=[tile(c) for c in widths] + [_resident(w.shape), tile(d), _full((1, d)), tile(d)],
        out_specs=[tile(d)] + [tile(d) for _ in copies] + [_full((1, d))],
        out_shape=[jax.ShapeDtypeStruct((t, d), F32)] + [jax.ShapeDtypeStruct((t, d), dt) for dt in copies]
        + [jax.ShapeDtypeStruct((1, d), F32)],
        operands=(*dys, w, x, g.reshape(1, d), dres), comm=comm)


def _conv_windows(ext_ref, tm):
    ext = ext_ref[...]
    return [pltpu.roll(ext, 3 - k, 0)[SUBLANES:SUBLANES + tm, :] for k in range(3)] + [ext[SUBLANES:SUBLANES + tm, :]]


def _conv_taps(windows, w_ref):
    acc = windows[3] * w_ref[3:4, :]
    for k in range(3):
        acc = acc + windows[k] * w_ref[k:k + 1, :]
    return acc


def _ssd_chunk_terms(dtr, bias_row, alog_row):
    q = SSD_CHUNK
    row = lax.broadcasted_iota(jnp.int32, (q, q), 0)
    col = lax.broadcasted_iota(jnp.int32, (q, q), 1)
    ltri = (col <= row).astype(F32)
    dt = _softplus(dtr + bias_row)
    a_row = -jnp.exp(alog_row)
    la = dt * a_row
    cum = _dot_sel_a(ltri, la)
    cum_t = _dot_tn_sel(la, (row <= col).astype(F32), "b")
    return dt, a_row, cum, cum_t, ltri, row, col


def _head_maps():
    di = SSD_HEADS * SSD_HEADDIM
    expand = (lax.broadcasted_iota(jnp.int32, (LANES, di), 1) // SSD_HEADDIM
              == lax.broadcasted_iota(jnp.int32, (LANES, di), 0)).astype(F32)
    collapse = (lax.broadcasted_iota(jnp.int32, (di, LANES), 0) // SSD_HEADDIM
                == lax.broadcasted_iota(jnp.int32, (di, LANES), 1)).astype(F32)
    return expand, collapse


def _ssd_decay(cum, cum_t, h, causal):
    seg = cum[:, h:h + 1] - cum_t[h:h + 1, :]
    return jnp.where(causal, jnp.exp(jnp.where(causal, seg, 0.0)), 0.0)


def _pair_block_diag(x2):
    low = lax.broadcasted_iota(jnp.int32, x2.shape, 1) < SSD_HEADDIM
    zero = jnp.zeros_like(x2)
    return jnp.concatenate([jnp.where(low, x2, zero), jnp.where(low, zero, x2)], axis=0)


def ssd_fwd(xbc_raw, conv_w8, conv_b, dtr, z, bias_row, alog_row, d_row, norm_g, *, name, comm=None):
    t = xbc_raw.shape[0]
    q, p, n = SSD_CHUNK, SSD_HEADDIM, SSD_STATE
    nc = t // q
    di = SSD_HEADS * p
    gw = di // SSD_GROUPS

    def body(raw_ref, cw_ref, cb_ref, dtr_ref, z_ref, bias_ref, alog_ref, d_ref, g_ref,
             ya_ref, yssd_ref, st_ref, state, ext, xbc_s):
        ci = pl.program_id(0)

        @pl.when(ci == 0)
        def _():
            state[...] = jnp.zeros_like(state)
            ext[0:SUBLANES, :] = jnp.zeros((SUBLANES, CONV_DIM), F32)

        ext[SUBLANES:SUBLANES + q, :] = raw_ref[...].astype(F32)
        xc = _conv_taps(_conv_windows(ext, q), cw_ref) + cb_ref[...]
        ext[0:SUBLANES, :] = ext[q:q + SUBLANES, :]
        xbc_s[...] = xc * _sigmoid(xc)
        st_ref[0] = state[...]
        dt, _, cum, cum_t, _, row, col = _ssd_chunk_terms(dtr_ref[...], bias_ref[...], alog_ref[...])
        causal = row >= col
        expand, _ = _head_maps()
        dt_full = _dot_sel_b(dt, expand)
        cum_full = _dot_sel_b(cum, expand)
        last_full = cum_full[q - 1:q, :]
        xs = xbc_s[:, 0:di]
        xdt = xs * dt_full
        xd = (xdt * jnp.exp(last_full - cum_full)).astype(BF16)
        xdt_b = xdt.astype(BF16)
        e_full = jnp.exp(cum_full)
        eend_full = jnp.exp(last_full)
        for g in range(SSD_GROUPS):
            slab = slice(g * gw, (g + 1) * gw)
            bg = xbc_s[:, di + g * n:di + (g + 1) * n].astype(BF16)
            cg = xbc_s[:, di + (SSD_GROUPS + g) * n:di + (SSD_GROUPS + g + 1) * n].astype(BF16)
            cb = _dot_nt(cg, bg)
            st_g = state[:, slab]
            pairs = []
            for pr in range(SSD_HPG // 2):
                h0 = g * SSD_HPG + 2 * pr
                m0 = (cb * _ssd_decay(cum, cum_t, h0, causal)).astype(BF16)
                m1 = (cb * _ssd_decay(cum, cum_t, h0 + 1, causal)).astype(BF16)
                pairs.append(_dot(jnp.concatenate([m0, m1], axis=1), _pair_block_diag(xdt_b[:, h0 * p:(h0 + 2) * p])))
            y = jnp.concatenate(pairs, axis=1) + e_full[:, slab] * _dot(cg, st_g.astype(BF16))
            yssd_ref[:, slab] = y + d_ref[:, slab] * xs[:, slab]
            state[:, slab] = eend_full[:, slab] * st_g + _dot_tn(bg, xd[:, slab])
        zv = z_ref[...].astype(F32)
        ya1 = yssd_ref[...] * (zv * _sigmoid(zv))
        for g in range(SSD_GROUPS):
            gsl = slice(g * gw, (g + 1) * gw)
            sl = ya1[:, gsl]
            rg = lax.rsqrt(jnp.mean(sl * sl, axis=-1, keepdims=True) + EPS)
            ya_ref[:, gsl] = (sl * rg * g_ref[:, gsl]).astype(BF16)

    blk = lambda w, j: pl.BlockSpec((q, w), lambda c: (c, j))
    return _tiled_call(
        body, name=name, grid=(nc,),
        in_specs=[blk(CONV_DIM, 0), _full((SUBLANES, CONV_DIM)), _full((1, CONV_DIM)), blk(LANES, 0), blk(di, 0),
                  _full((1, LANES)), _full((1, LANES)), _full((1, di)), _full((1, di))],
        out_specs=[blk(di, 0), blk(di, 0), pl.BlockSpec((1, n, di), lambda c: (c, 0, 0))],
        out_shape=[jax.ShapeDtypeStruct((t, di), BF16), jax.ShapeDtypeStruct((t, di), F32),
                   jax.ShapeDtypeStruct((nc, n, di), F32)],
        scratch_shapes=[pltpu.VMEM((n, di), F32), pltpu.VMEM((SUBLANES + q, CONV_DIM), F32), pltpu.VMEM((q, CONV_DIM), F32)],
        operands=(xbc_raw, conv_w8, conv_b.reshape(1, CONV_DIM), dtr, z, bias_row, alog_row, d_row,
                  norm_g.reshape(1, di)), comm=comm)


def ssd_bwd(dya, yssd, z, xbc_raw, conv_w8, conv_b, dtr, states, bias_row, alog_row, d_row, norm_g, *, name, comm=None):
    t = xbc_raw.shape[0]
    q, p, n = SSD_CHUNK, SSD_HEADDIM, SSD_STATE
    nc = t // q
    di = SSD_HEADS * p
    gw = di // SSD_GROUPS
    per = q // (2 * SUBLANES)

    def body(dya_ref, yssd_ref, z_ref, raw_ref, prev_ref, cw_ref, cb_ref, dtr_ref, st_ref, bias_ref, alog_ref, d_ref,
             g_ref, dzdt_ref, draw_ref, dng_ref, dd_ref, dalog_ref, dbias_ref, dcb_ref, dcw_ref,
             dstate, dyssd, xbc_s, dxbc_ref, ext, aft):
        ci = pl.program_id(0)

        @pl.when(ci == 0)
        def _():
            dstate[...] = jnp.zeros_like(dstate)
            aft[q:q + SUBLANES, :] = jnp.zeros((SUBLANES, CONV_DIM), F32)
            for r in (dng_ref, dd_ref, dalog_ref, dbias_ref, dcb_ref, dcw_ref):
                r[...] = jnp.zeros_like(r)

        ext[0:SUBLANES, :] = jnp.where(ci == nc - 1, 0.0, prev_ref[SUBLANES:2 * SUBLANES, :].astype(F32))
        ext[SUBLANES:SUBLANES + q, :] = raw_ref[...].astype(F32)
        windows = _conv_windows(ext, q)
        xc = _conv_taps(windows, cw_ref) + cb_ref[...]
        sig_c = _sigmoid(xc)
        xbc_s[...] = xc * sig_c

        zv = z_ref[...].astype(F32)
        sz = _sigmoid(zv)
        silu_z = zv * sz
        yv = yssd_ref[...]
        ya1 = yv * silu_z
        dyav = dya_ref[...]
        for g in range(SSD_GROUPS):
            gsl = slice(g * gw, (g + 1) * gw)
            sl = ya1[:, gsl]
            rg = lax.rsqrt(jnp.mean(sl * sl, axis=-1, keepdims=True) + EPS)
            ya2 = sl * rg
            dy_g = dyav[:, gsl]
            dng_ref[:, gsl] += jnp.sum(dy_g * ya2, axis=0, keepdims=True)
            dya2 = dy_g * g_ref[:, gsl]
            dya1 = rg * (dya2 - ya2 * jnp.mean(dya2 * ya2, axis=-1, keepdims=True))
            dyssd[:, gsl] = dya1 * silu_z[:, gsl]
            dzdt_ref[:, gsl] = (dya1 * yv[:, gsl] * (sz[:, gsl] * (1.0 + zv[:, gsl] * (1.0 - sz[:, gsl])))).astype(BF16)

        dtr_v = dtr_ref[...]
        dt, a_row, cum, cum_t, ltri, row, col = _ssd_chunk_terms(dtr_v, bias_ref[...], alog_ref[...])
        causal = row >= col
        expand, collapse = _head_maps()
        lane = lax.broadcasted_iota(jnp.int32, (1, LANES), 1)
        sub = lax.broadcasted_iota(jnp.int32, (LANES, 1), 0)
        is_last = lax.broadcasted_iota(jnp.int32, (q, 1), 0) == q - 1
        low = lax.broadcasted_iota(jnp.int32, (q, 2 * p), 1) < p
        dt_full = _dot_sel_b(dt, expand)
        cum_full = _dot_sel_b(cum, expand)
        last_full = cum_full[q - 1:q, :]
        e_full = jnp.exp(cum_full)
        eend_full = jnp.exp(last_full)
        dend_full = jnp.exp(last_full - cum_full)
        xs = xbc_s[:, 0:di]
        xdt = xs * dt_full
        xdt_b = xdt.astype(BF16)
        xd_b = (xdt * dend_full).astype(BF16)
        dy_all = dyssd[...]
        dy_b = dy_all.astype(BF16)
        dg_b = (dy_all * e_full).astype(BF16)
        dcum_mat = jnp.zeros((q, LANES), F32)
        rmat = jnp.zeros((LANES, q), F32)
        yoff_prod, bds_list, dx_list, dlast_parts = [], [], [], []
        for g in range(SSD_GROUPS):
            slab = slice(g * gw, (g + 1) * gw)
            bg = xbc_s[:, di + g * n:di + (g + 1) * n].astype(BF16)
            cg = xbc_s[:, di + (SSD_GROUPS + g) * n:di + (SSD_GROUPS + g + 1) * n].astype(BF16)
            cb = _dot_nt(cg, bg)
            st_g = st_ref[0, :, slab]
            st_b = st_g.astype(BF16)
            dsn = dstate[:, slab]
            dsn_b = dsn.astype(BF16)
            yoff_prod.append(dy_all[:, slab] * e_full[:, slab] * _dot(cg, st_b))
            dcg = _dot_nt(dg_b[:, slab], st_b)
            dstate[:, slab] = _dot_tn(cg, dg_b[:, slab]) + eend_full[:, slab] * dsn
            bds_list.append(_dot(bg, dsn_b))
            dbg = _dot_nt(xd_b[:, slab], dsn_b)
            dlast_parts.append(jnp.sum(dsn * st_g, axis=0, keepdims=True))
            dcb = jnp.zeros((q, q), F32)
            dx_pairs = []
            for pr in range(SSD_HPG // 2):
                h0 = g * SSD_HPG + 2 * pr
                ps = slice(h0 * p, (h0 + 2) * p)
                l0 = _ssd_decay(cum, cum_t, h0, causal)
                l1 = _ssd_decay(cum, cum_t, h0 + 1, causal)
                m0, m1 = cb * l0, cb * l1
                dyp = dy_b[:, ps]
                zero = jnp.zeros_like(dyp)
                dy0, dy1 = jnp.where(low, dyp, zero), jnp.where(low, zero, dyp)
                dm0 = _dot_nt(dy0, xdt_b[:, ps])
                dm1 = _dot_nt(dy1, xdt_b[:, ps])
                w0, w1 = dm0 * m0, dm1 * m1
                dcum_mat = (dcum_mat + jnp.sum(w0, axis=1, keepdims=True) * (lane == h0).astype(F32)
                            + jnp.sum(w1, axis=1, keepdims=True) * (lane == h0 + 1).astype(F32))
                rmat = (rmat + jnp.where(sub == h0, jnp.sum(w0, axis=0, keepdims=True), 0.0)
                        + jnp.where(sub == h0 + 1, jnp.sum(w1, axis=0, keepdims=True), 0.0))
                dcb = dcb + dm0 * l0 + dm1 * l1
                dx_pairs.append(_dot_tn(jnp.concatenate([m0.astype(BF16), m1.astype(BF16)], axis=0),
                                        jnp.concatenate([dy0, dy1], axis=0)))
            dx_list.append(jnp.concatenate(dx_pairs, axis=1))
            dcb_b = dcb.astype(BF16)
            dxbc_ref[:, di + g * n:di + (g + 1) * n] = dbg + _dot_tn(dcb_b, cg)
            dxbc_ref[:, di + SSD_GROUPS * n + g * n:di + SSD_GROUPS * n + (g + 1) * n] = dcg + _dot(dcb_b, bg)
        bds_all = jnp.concatenate(bds_list, axis=1)
        dx_all = jnp.concatenate(dx_list, axis=1) + dend_full * bds_all
        last_row = cum[q - 1:q, :]
        qv = _dot_sel_b(xdt * bds_all, collapse) * jnp.exp(last_row - cum)
        dcum_mat = dcum_mat + _dot_sel_b(jnp.concatenate(yoff_prod, axis=1), collapse) - qv
        dlast_full = jnp.broadcast_to(jnp.concatenate(dlast_parts, axis=1), (SUBLANES, di))
        dlast_row = _dot_sel_b(dlast_full, collapse)[0:1, :] * jnp.exp(last_row) + jnp.sum(qv, axis=0, keepdims=True)
        dcum_mat = dcum_mat + jnp.where(is_last, dlast_row, 0.0)
        dla = _dot_tn_sel(ltri, dcum_mat, "a") - _dot_nt_sel_a((row <= col).astype(F32), rmat)
        ddt_mat = _dot_sel_b(dx_all * xs, collapse) + dla * a_row
        dxbc_ref[:, 0:di] = d_ref[...] * dy_all + dx_all * dt_full
        ddtr = ddt_mat * _sigmoid(dtr_v + bias_ref[...])
        dzdt_ref[:, di:di + LANES] = ddtr.astype(BF16)
        dd_full = jnp.broadcast_to(jnp.sum(dy_all * xs, axis=0, keepdims=True), (SUBLANES, di))
        dd_ref[...] += _dot_sel_b(dd_full, collapse)[0:1, :]
        dalog_ref[...] += jnp.sum(dla * dt, axis=0, keepdims=True) * a_row
        dbias_ref[...] += jnp.sum(ddtr, axis=0, keepdims=True)

        dxc = dxbc_ref[...] * (sig_c * (1.0 + xc * (1.0 - sig_c)))
        dcb_ref[...] += jnp.sum(dxc, axis=0, keepdims=True)
        taps = [jnp.sum(dxc * windows[k], axis=0, keepdims=True) for k in range(SSD_CONV)]
        dcw_ref[...] += jnp.concatenate(taps + [jnp.zeros((SUBLANES - SSD_CONV, CONV_DIM), F32)], axis=0)
        aft[0:q, :] = dxc
        after = aft[...]
        draw = dxc * cw_ref[3:4, :]
        for j in range(1, SSD_CONV):
            draw = draw + pltpu.roll(after, q + SUBLANES - j, 0)[0:q, :] * cw_ref[3 - j:4 - j, :]
        draw_ref[...] = draw.astype(BF16)
        aft[q:q + SUBLANES, :] = dxc[0:SUBLANES, :]

    blk = lambda w, j: pl.BlockSpec((q, w), lambda c: (nc - 1 - c, j))
    row_out = lambda w: jax.ShapeDtypeStruct((1, w), F32)
    return _tiled_call(
        body, name=name, grid=(nc,),
        in_specs=[blk(di, 0), blk(di, 0), blk(di, 0), blk(CONV_DIM, 0),
                  pl.BlockSpec((2 * SUBLANES, CONV_DIM), lambda c: (jnp.maximum((nc - 1 - c) * per - 1, 0), 0)),
                  _full((SUBLANES, CONV_DIM)), _full((1, CONV_DIM)), blk(LANES, 0),
                  pl.BlockSpec((1, n, di), lambda c: (nc - 1 - c, 0, 0)),
                  _full((1, LANES)), _full((1, LANES)), _full((1, di)), _full((1, di))],
        out_specs=[blk(di + LANES, 0), blk(CONV_DIM, 0),
                   _full((1, di)), _full((1, LANES)), _full((1, LANES)), _full((1, LANES)),
                   _full((1, CONV_DIM)), _full((SUBLANES, CONV_DIM))],
        out_shape=[jax.ShapeDtypeStruct((t, di + LANES), BF16), jax.ShapeDtypeStruct((t, CONV_DIM), BF16),
                   row_out(di), row_out(LANES), row_out(LANES), row_out(LANES),
                   row_out(CONV_DIM), jax.ShapeDtypeStruct((SUBLANES, CONV_DIM), F32)],
        scratch_shapes=[pltpu.VMEM((n, di), F32), pltpu.VMEM((q, di), F32), pltpu.VMEM((q, CONV_DIM), F32),
                        pltpu.VMEM((q, CONV_DIM), F32), pltpu.VMEM((SUBLANES + q, CONV_DIM), F32),
                        pltpu.VMEM((q + SUBLANES, CONV_DIM), F32)],
        operands=(dya, yssd, z, xbc_raw, xbc_raw, conv_w8, conv_b.reshape(1, CONV_DIM), dtr, states, bias_row,
                  alog_row, d_row, norm_g.reshape(1, di)),
        comm=comm)


S5_BLOCKS = 4
S5_BC = S5_WIDTH // S5_BLOCKS
S5_BS = S5_N // S5_BLOCKS


def _s5_tables(ar, ai, reverse):
    pows = [(ar, ai)]
    for _ in range(SUBLANES - 1):
        pr, pi = pows[-1]
        pows.append((pr * ar - pi * ai, pr * ai + pi * ar))
    row = lax.broadcasted_iota(jnp.int32, (SUBLANES, ar.shape[1]), 0)
    out = []
    for k in (1, 2, 4):
        pr, pi = pows[k - 1]
        mask = (row < SUBLANES - k) if reverse else (row >= k)
        out += [jnp.where(mask, pr, 0.0), jnp.where(mask, pi, 0.0)]
    order = list(range(SUBLANES))
    if reverse:
        order = order[::-1]
    out.append(jnp.concatenate([pows[e][0] for e in order], axis=0))
    out.append(jnp.concatenate([pows[e][1] for e in order], axis=0))
    return out


def s5_interleave(a, tm):
    t, c = a.shape
    return a.reshape(t // tm, SUBLANES, tm // SUBLANES, c).transpose(0, 2, 1, 3).reshape(t, c)


def s5_deinterleave(a, tm):
    t, c = a.shape
    return a.reshape(t // tm, tm // SUBLANES, SUBLANES, c).transpose(0, 2, 1, 3).reshape(t, c)


def _s5_scan_setup(ab_ref, base, tab, seg, reverse):
    ar = ab_ref[0:1, :]
    ai = -ab_ref[1:2, :] if reverse else ab_ref[1:2, :]
    base[0:1, :] = ar
    base[1:2, :] = ai
    assert seg & (seg - 1) == 0
    pr, pi = ar, ai
    for _ in range(seg.bit_length() - 1):
        pr, pi = pr * pr - pi * pi, 2.0 * pr * pi
    for k, tv in enumerate(_s5_tables(pr, pi, reverse)):
        tab[k] = tv


def _s5_scan(s_scr, base, tab, carry, tm, reverse):
    seg = tm // SUBLANES
    width = S5_BS
    first = lax.broadcasted_iota(jnp.int32, (SUBLANES, width), 0) == (SUBLANES - 1 if reverse else 0)

    def rows(i):
        step = (seg - 1 - i) if reverse else i
        return pl.ds(pl.multiple_of(step * SUBLANES, SUBLANES), SUBLANES)

    for lc in range(S5_N // width):
        lr = slice(lc * width, (lc + 1) * width)
        li = slice(S5_N + lc * width, S5_N + (lc + 1) * width)
        ar = jnp.broadcast_to(base[0:1, lr], (SUBLANES, width))
        ai = jnp.broadcast_to(base[1:2, lr], (SUBLANES, width))

        def advance(i, x, lr=lr, li=li, ar=ar, ai=ai):
            xr, xi = x
            return ar * xr - ai * xi + s_scr[rows(i), lr], ar * xi + ai * xr + s_scr[rows(i), li]

        zero = jnp.zeros((SUBLANES, width), F32)
        fr, fi = lax.fori_loop(0, seg, advance, (zero, zero))

        tb = [tab[k, :, lr] for k in range(8)]
        for lvl, k in enumerate((1, 2, 4)):
            sh = (SUBLANES - k) if reverse else k
            pr, pi = tb[2 * lvl], tb[2 * lvl + 1]
            rr, ri = pltpu.roll(fr, sh, 0), pltpu.roll(fi, sh, 0)
            fr, fi = fr + pr * rr - pi * ri, fi + pr * ri + pi * rr
        cr, ci = carry[0:1, lr], carry[0:1, li]
        fr, fi = fr + tb[6] * cr - tb[7] * ci, fi + tb[6] * ci + tb[7] * cr
        last = 0 if reverse else SUBLANES - 1
        carry[0:1, lr] = fr[last:last + 1, :]
        carry[0:1, li] = fi[last:last + 1, :]
        sh = (SUBLANES - 1) if reverse else 1
        entering = (jnp.where(first, cr, pltpu.roll(fr, sh, 0)), jnp.where(first, ci, pltpu.roll(fi, sh, 0)))

        def rerun(i, x, lr=lr, li=li, advance=advance):
            xr, xi = advance(i, x)
            s_scr[rows(i), lr] = xr
            s_scr[rows(i), li] = xi
            return xr, xi

        lax.fori_loop(0, seg, rerun, entering)


def _s5_scratch(tm):
    return [pltpu.VMEM((tm, 2 * S5_N), F32), pltpu.VMEM((SUBLANES, S5_N), F32),
            pltpu.VMEM((8, SUBLANES, S5_N), F32), pltpu.VMEM((SUBLANES, 2 * S5_N), F32)]


def _s5_put(scr, j, val):
    scr[:, j * S5_BS:(j + 1) * S5_BS] = val[:, :S5_BS]
    scr[:, S5_N + j * S5_BS:S5_N + (j + 1) * S5_BS] = val[:, S5_BS:]


def _s5_get(scr, j):
    return jnp.concatenate([scr[:, j * S5_BS:(j + 1) * S5_BS], scr[:, S5_N + j * S5_BS:S5_N + (j + 1) * S5_BS]], axis=1)


def s5_fwd(u5, bbc, ccc, ab8, d_row, wglu, bglu_row, *, tm, name, comm=None):
    t, w = u5.shape

    def body(u_ref, bb_ref, cc_ref, ab_ref, d_ref, wg_ref, bg_ref, s_ref, ys_ref, yb_ref, s_scr, base, tab, carry):
        i = pl.program_id(0)

        @pl.when(i == 0)
        def _():
            carry[...] = jnp.zeros_like(carry)
            _s5_scan_setup(ab_ref, base, tab, tm // SUBLANES, False)

        ub = u_ref[...]
        u = ub.astype(F32)
        for j in range(S5_BLOCKS):
            uj = ub[:, j * S5_BC:(j + 1) * S5_BC]
            _s5_put(s_scr, j, _dot(uj, bb_ref[j * S5_BC:(j + 1) * S5_BC, :]))
        _s5_scan(s_scr, base, tab, carry, tm, False)
        s_ref[...] = s_scr[...]
        ys = []
        for j in range(S5_BLOCKS):
            ys.append(_dot(_s5_get(s_scr, j).astype(BF16), cc_ref[:, j * S5_BC:(j + 1) * S5_BC]))
        y = jnp.concatenate(ys, axis=1) + d_ref[...] * u
        ys_ref[...] = y
        yb1 = _gelu(y)
        tt = _dot(yb1.astype(BF16), wg_ref[...]) + bg_ref[...]
        yb_ref[...] = (yb1 * _sigmoid(tt)).astype(BF16)

    tile = lambda c: pl.BlockSpec((tm, c), lambda i: (i, 0))
    return _tiled_call(
        body, name=name, grid=(t // tm,),
        in_specs=[tile(w), _full((w, 2 * S5_BS)), _full((2 * S5_BS, w)), _full((SUBLANES, S5_N)), _full((1, w)),
                  _full((w, w)), _full((1, w))],
        out_specs=[tile(2 * S5_N), tile(w), tile(w)],
        out_shape=[jax.ShapeDtypeStruct((t, 2 * S5_N), F32), jax.ShapeDtypeStruct((t, w), F32),
                   jax.ShapeDtypeStruct((t, w), BF16)],
        scratch_shapes=_s5_scratch(tm),
        operands=(u5, bbc, ccc, ab8, d_row, wglu, bglu_row), comm=comm)


def s5_bwd(dyb, ys5, u5, s, bbc, ccc, ab8, d_row, wglu, bglu_row, *, tm, name, comm=None):
    t, w = u5.shape
    nt = t // tm
    per = tm // SUBLANES

    def body(dyb_ref, ys_ref, u_ref, s_ref, sprev_ref, bb_ref, cc_ref, ab_ref, d_ref, wg_ref, bg_ref,
             du_ref, dbb_ref, dcc_ref, dab_ref, dd_ref, dwg_ref, dbg_ref, g_scr, base, tab, carry):
        i = pl.program_id(0)

        @pl.when(i == 0)
        def _():
            carry[...] = jnp.zeros_like(carry)
            _s5_scan_setup(ab_ref, base, tab, tm // SUBLANES, True)
            for r in (dbb_ref, dcc_ref, dab_ref, dd_ref, dwg_ref, dbg_ref):
                r[...] = jnp.zeros_like(r)

        y = ys_ref[...]
        ub = u_ref[...]
        u = ub.astype(F32)
        yb1 = _gelu(y)
        yb1b = yb1.astype(BF16)
        sg = _sigmoid(_dot(yb1b, wg_ref[...]) + bg_ref[...])
        dybv = dyb_ref[...]
        dtt = dybv * yb1 * sg * (1.0 - sg)
        dttb = dtt.astype(BF16)
        dyb1 = dybv * sg + _dot_nt(dttb, wg_ref[...])
        dwg_ref[...] += _dot_tn(yb1b, dttb)
        dbg_ref[...] += jnp.sum(dtt, axis=0, keepdims=True)
        dy = dyb1 * _gelu_grad(y)
        dd_ref[...] += jnp.sum(dy * u, axis=0, keepdims=True)
        dyh = dy.astype(BF16)
        for j in range(S5_BLOCKS):
            cs = slice(j * S5_BC, (j + 1) * S5_BC)
            _s5_put(g_scr, j, _dot_nt(dyh[:, cs], cc_ref[:, cs]))
        _s5_scan(g_scr, base, tab, carry, tm, True)
        dus = []
        for j in range(S5_BLOCKS):
            cs = slice(j * S5_BC, (j + 1) * S5_BC)
            gb = _s5_get(g_scr, j).astype(BF16)
            dus.append(_dot_nt(gb, bb_ref[cs, :]))
            dbb_ref[cs, :] += _dot_tn(ub[:, cs], gb)
            dcc_ref[:, cs] += _dot_tn(_s5_get(s_ref, j).astype(BF16), dyh[:, cs])
        du_ref[...] = (jnp.concatenate(dus, axis=1) + d_ref[...] * dy).astype(BF16)
        top = lax.broadcasted_iota(jnp.int32, (SUBLANES, S5_BS), 0) == 0

        def corr(g_sl, s_sl):
            before = jnp.where(i == nt - 1, 0.0, sprev_ref[SUBLANES - 1:SUBLANES, s_sl])
            wrap = jnp.where(top, before, pltpu.roll(s_ref[tm - SUBLANES:tm, s_sl], 1, 0))
            return (jnp.sum(g_scr[SUBLANES:tm, g_sl] * s_ref[0:tm - SUBLANES, s_sl], axis=0, keepdims=True)
                    + jnp.sum(g_scr[0:SUBLANES, g_sl] * wrap, axis=0, keepdims=True))

        for j in range(S5_BLOCKS):
            lr = slice(j * S5_BS, (j + 1) * S5_BS)
            li = slice(S5_N + j * S5_BS, S5_N + (j + 1) * S5_BS)
            dab_ref[0:1, lr] += corr(lr, lr) + corr(li, li)
            dab_ref[1:2, lr] += corr(li, lr) - corr(lr, li)

    tile = lambda c: pl.BlockSpec((tm, c), lambda i: (nt - 1 - i, 0))
    acc = lambda r, c: jax.ShapeDtypeStruct((r, c), F32)
    return _tiled_call(
        body, name=name, grid=(nt,),
        in_specs=[tile(w), tile(w), tile(w), tile(2 * S5_N),
                  pl.BlockSpec((SUBLANES, 2 * S5_N), lambda i: (jnp.maximum((nt - 1 - i) * per - 1, 0), 0)),
                  _full((w, 2 * S5_BS)), _full((2 * S5_BS, w)), _full((SUBLANES, S5_N)), _full((1, w)),
                  _full((w, w)), _full((1, w))],
        out_specs=[tile(w), _full((w, 2 * S5_BS)), _full((2 * S5_BS, w)), _full((SUBLANES, S5_N)), _full((1, w)),
                   _full((w, w)), _full((1, w))],
        out_shape=[jax.ShapeDtypeStruct((t, w), BF16), acc(w, 2 * S5_BS), acc(2 * S5_BS, w), acc(SUBLANES, S5_N),
                   acc(1, w), acc(w, w), acc(1, w)],
        scratch_shapes=_s5_scratch(tm),
        operands=(dyb, ys5, u5, s, s, bbc, ccc, ab8, d_row, wglu, bglu_row), comm=comm)


def s5_discretize(a_re, a_im, log_dt, b_re, b_im, *, name):
    n = a_re.shape[0]

    def body(ar_ref, ai_ref, dt_ref, br_ref, bi_ref, ab_ref, bbr_ref, bbi_ref):
        ar, ai, dtv = ar_ref[...], ai_ref[...], jnp.exp(dt_ref[...])
        mag = jnp.exp(ar * dtv)
        abr = mag * jnp.cos(ai * dtv)
        abi = mag * jnp.sin(ai * dtv)
        den = ar * ar + ai * ai
        nr = abr - 1.0
        cr = (nr * ar + abi * ai) / den
        ci = (abi * ar - nr * ai) / den
        ab_ref[:, 0:1] = abr
        ab_ref[:, 1:2] = abi
        br, bi = br_ref[...], bi_ref[...]
        bbr_ref[...] = cr * br - ci * bi
        bbi_ref[...] = cr * bi + ci * br

    col = jax.ShapeDtypeStruct((n, 2), F32)
    mat = jax.ShapeDtypeStruct((n, S5_GROUP), F32)
    return pl.pallas_call(body, name=name, out_shape=[col, mat, mat])(a_re, a_im, log_dt, b_re, b_im)


def s5_discretize_bwd(a_re, a_im, log_dt, b_re, b_im, dab, dbb_re, dbb_im, *, name):
    n = a_re.shape[0]

    def body(ar_ref, ai_ref, dt_ref, br_ref, bi_ref, dab_ref, dbr_ref, dbi_ref, da_ref, ddt_ref, gbr_ref, gbi_ref):
        ar, ai, dtv = ar_ref[...], ai_ref[...], jnp.exp(dt_ref[...])
        mag = jnp.exp(ar * dtv)
        abr = mag * jnp.cos(ai * dtv)
        abi = mag * jnp.sin(ai * dtv)
        den = ar * ar + ai * ai
        nr = abr - 1.0
        cr = (nr * ar + abi * ai) / den
        ci = (abi * ar - nr * ai) / den
        br, bi = br_ref[...], bi_ref[...]
        dbr, dbi = dbr_ref[...], dbi_ref[...]
        gbr_ref[...] = cr * dbr + ci * dbi
        gbi_ref[...] = cr * dbi - ci * dbr
        gcr = jnp.sum(dbr * br + dbi * bi, axis=1, keepdims=True)
        gci = jnp.sum(dbi * br - dbr * bi, axis=1, keepdims=True)
        ilr, ili = ar / den, -ai / den
        gabr = dab_ref[:, 0:1] + (ilr * gcr + ili * gci)
        gabi = dab_ref[:, 1:2] + (ilr * gci - ili * gcr)
        t1r, t1i = dtv * abr, dtv * abi
        t2r, t2i = -(cr * ilr - ci * ili), -(cr * ili + ci * ilr)
        da_ref[:, 0:1] = (t1r * gabr + t1i * gabi) + (t2r * gcr + t2i * gci)
        da_ref[:, 1:2] = (t1r * gabi - t1i * gabr) + (t2r * gci - t2i * gcr)
        lr, li = ar * abr - ai * abi, ar * abi + ai * abr
        dlog = (lr * gabr + li * gabi) * dtv
        ddt_ref[...] = jnp.sum(dlog.reshape(S5_GROUPS, S5_STATE, 1), axis=1)

    col2 = jax.ShapeDtypeStruct((n, 2), F32)
    col1 = jax.ShapeDtypeStruct((S5_GROUPS, 1), F32)
    mat = jax.ShapeDtypeStruct((n, S5_GROUP), F32)
    return pl.pallas_call(body, name=name, out_shape=[col2, col1, mat, mat])(
        a_re, a_im, log_dt, b_re, b_im, dab, dbb_re, dbb_im)


def merge_fwd(ya, yb, graw, x, wb, wout, *, tm, name):
    t, d = x.shape

    def body(ya_ref, yb_ref, g_ref, x_ref, wb_ref, wo_ref, x1_ref, pa_ref, pb_ref):
        pa = _dot(ya_ref[...], wb_ref[0:d, :])
        pb = _dot(yb_ref[...], wb_ref[d:, :])
        gate = _sigmoid(g_ref[...].astype(F32))
        merged = gate[:, :d] * pa + gate[:, d:] * pb
        x1_ref[...] = x_ref[...] + _dot(merged.astype(BF16), wo_ref[...])
        pa_ref[...] = pa.astype(BF16)
        pb_ref[...] = pb.astype(BF16)

    tile = lambda c: pl.BlockSpec((tm, c), lambda i: (i, 0))
    sds = jax.ShapeDtypeStruct
    return pl.pallas_call(
        body, name=name, grid=(t // tm,),
        in_specs=[tile(d), tile(S5_WIDTH), tile(2 * d), tile(d), _resident((d + S5_WIDTH, d)), _resident((d, d))],
        out_specs=[tile(d), tile(d), tile(d)], out_shape=[sds((t, d), F32), sds((t, d), BF16), sds((t, d), BF16)],
        compiler_params=_params("parallel"),
    )(ya, yb, graw, x, wb, wout)


def merge_bwd(dx1b, graw, pa, pb, wb, wout, *, tm, name):
    t, d = pa.shape

    def body(dx_ref, g_ref, pa_ref, pb_ref, wb_ref, wo_ref,
             mg_ref, dg_ref, dpa_ref, dpb_ref, dya_ref, dyb_ref):
        dm = _dot_nt(dx_ref[...], wo_ref[...])
        gate = _sigmoid(g_ref[...].astype(F32))
        g0, g1 = gate[:, :d], gate[:, d:]
        pa, pb = pa_ref[...].astype(F32), pb_ref[...].astype(F32)
        mg_ref[...] = (g0 * pa + g1 * pb).astype(BF16)
        dg_ref[:, :d] = (dm * pa * g0 * (1.0 - g0)).astype(BF16)
        dg_ref[:, d:] = (dm * pb * g1 * (1.0 - g1)).astype(BF16)
        dpa = (dm * g0).astype(BF16)
        dpb = (dm * g1).astype(BF16)
        dpa_ref[...] = dpa
        dpb_ref[...] = dpb
        dya_ref[...] = _dot_nt(dpa, wb_ref[0:d, :])
        dyb_ref[...] = _dot_nt(dpb, wb_ref[d:, :])

    tile = lambda c: pl.BlockSpec((tm, c), lambda i: (i, 0))
    sds = jax.ShapeDtypeStruct
    return pl.pallas_call(
        body, name=name, grid=(t // tm,),
        in_specs=[tile(d), tile(2 * d), tile(d), tile(d), _resident((d + S5_WIDTH, d)), _resident((d, d))],
        out_specs=[tile(d), tile(2 * d), tile(d), tile(d), tile(d), tile(S5_WIDTH)],
        out_shape=[sds((t, d), BF16), sds((t, 2 * d), BF16), sds((t, d), BF16), sds((t, d), BF16),
                   sds((t, d), F32), sds((t, S5_WIDTH), F32)],
        compiler_params=_params("parallel"),
    )(dx1b, graw, pa, pb, wb, wout)


def mlp_out_loss(a1, x1, w2, gf, target, *, tm, name):
    t, d = x1.shape
    f = a1.shape[1]

    def body(a_ref, x_ref, w_ref, g_ref, tg_ref, dx_ref, dxb_ref, loss_ref, dg_ref):
        i = pl.program_id(0)
        av = jnp.maximum(a_ref[...].astype(F32), 0.0)
        x2 = x_ref[...] + _dot((av * av).astype(BF16), w_ref[...])
        r = lax.rsqrt(jnp.mean(x2 * x2, axis=-1, keepdims=True) + EPS)
        xn = x2 * r
        err = xn * g_ref[...] - tg_ref[...]
        dyf = err * (1.0 / d)
        dxn = dyf * g_ref[...]
        dx = r * (dxn - xn * jnp.mean(dxn * xn, axis=-1, keepdims=True))
        dx_ref[...] = dx
        dxb_ref[...] = dx.astype(BF16)

        @pl.when(i == 0)
        def _():
            loss_ref[...] = jnp.zeros_like(loss_ref)
            dg_ref[...] = jnp.zeros_like(dg_ref)

        loss_ref[...] += jnp.sum(err * err) * (0.5 / d)
        dg_ref[...] += jnp.sum(dyf * xn, axis=0, keepdims=True)

    tile = lambda c: pl.BlockSpec((tm, c), lambda i: (i, 0))
    sds = jax.ShapeDtypeStruct
    return pl.pallas_call(
        body, name=name, grid=(t // tm,),
        in_specs=[tile(f), tile(d), _resident((f, d)), _full((1, d)), tile(d)],
        out_specs=[tile(d), tile(d), _full((1, LANES)), _full((1, d))],
        out_shape=[sds((t, d), F32), sds((t, d), BF16), sds((1, LANES), F32), sds((1, d), F32)],
        compiler_params=_params("arbitrary"),
    )(a1, x1, w2, gf.reshape(1, d), target)


def mlp_bwd_act(dx2b, a1, w2, *, tm, name):
    t, f = a1.shape
    d = dx2b.shape[1]

    def body(dx_ref, a_ref, w_ref, o_ref):
        dact = _dot_nt(dx_ref[...], w_ref[...])
        o_ref[...] = (dact * 2.0 * jnp.maximum(a_ref[...].astype(F32), 0.0)).astype(BF16)

    tile = lambda c: pl.BlockSpec((tm, c), lambda i: (i, 0))
    return pl.pallas_call(
        body, name=name, grid=(t // tm,),
        in_specs=[tile(d), tile(f), _resident((f, d))], out_specs=tile(f),
        out_shape=jax.ShapeDtypeStruct((t, f), BF16),
        compiler_params=_params("parallel"),
    )(dx2b, a1, w2)


ADAM_TILE_ELEMS = 128 * 1024


def _row_tile(rows, cap):
    if rows <= cap:
        return rows
    best = None
    for c in range(16, cap + 1, 16):
        if rows % c == 0:
            best = c
    assert best is not None, rows
    return best


def _device_order_sum(parts_ref):
    total = parts_ref[0].astype(F32)
    for k in range(1, N_DEV):
        total = total + parts_ref[k].astype(F32)
    return total


def _adamw_math(w, m, v, gparts_ref):
    g = _device_order_sum(gparts_ref)
    mn = ADAM_B1 * m + (1.0 - ADAM_B1) * g
    vn = ADAM_B2 * v + (1.0 - ADAM_B2) * (g * g)
    m_hat = mn / (1.0 - ADAM_B1 ** ADAM_STEP)
    v_hat = vn / (1.0 - ADAM_B2 ** ADAM_STEP)
    return g, -ADAM_LR * (m_hat / (jnp.sqrt(v_hat) + ADAM_EPS) + ADAM_WD * w), mn, vn


def adamw(w, m, v, gparts, *, name):
    rows, cols = w.shape
    tr = _row_tile(rows, max(16, ADAM_TILE_ELEMS // cols))

    def body(w_ref, m_ref, v_ref, g_ref, go_ref, d_ref, mo_ref, vo_ref):
        go_ref[...], d_ref[...], mo_ref[...], vo_ref[...] = _adamw_math(w_ref[...], m_ref[...], v_ref[...], g_ref)

    tile = pl.BlockSpec((tr, cols), lambda i: (i, 0))
    out = jax.ShapeDtypeStruct((rows, cols), F32)
    return pl.pallas_call(
        body, name=name, grid=(rows // tr,),
        in_specs=[tile, tile, tile, pl.BlockSpec((N_DEV, tr, cols), lambda i: (0, i, 0))],
        out_specs=[tile, tile, tile, tile], out_shape=[out, out, out, out],
        compiler_params=_params("parallel"),
    )(w, m, v, gparts)


def all_gather_arrays(blocks, *, name):
    na = len(blocks)

    def body(*refs):
        x_refs, out_refs = refs[:na], refs[na:2 * na]
        send_sems, recv_sems, local_sems = refs[2 * na:]
        x, y, c, _ = _place()
        me, sibling = (x, y, c), (x, y, 1 - c)
        chips = [(1 - x, y), (x, 1 - y), (1 - x, 1 - y)]

        def copy(a, k, blk, to, own=False):
            px, py, pc = blk
            dst = out_refs[a].at[4 * px + 2 * py + pc]
            return pltpu.make_async_remote_copy(
                src_ref=x_refs[a] if own else dst, dst_ref=dst,
                send_sem=send_sems.at[7 * a + k], recv_sem=recv_sems.at[7 * a + k], device_id=to, device_id_type=_MESH)

        mine = [pltpu.make_async_copy(x_refs[a], out_refs[a].at[4 * x + 2 * y + c], local_sems.at[a]) for a in range(na)]
        for cp in mine:
            cp.start()
        sent = []
        for a in range(na):
            first = [copy(a, 0, me, sibling, own=True)]
            first += [copy(a, 1 + j, me, (*chip, c), own=True) for j, chip in enumerate(chips)]
            for cp in first:
                cp.start()
            sent += first
        for a in range(na):
            for j, chip in enumerate(chips):
                copy(a, 1 + j, (*chip, c), me).wait_recv()
                fwd = copy(a, 4 + j, (*chip, c), sibling)
                fwd.start()
                sent.append(fwd)
        for a in range(na):
            copy(a, 0, sibling, me).wait_recv()
            for j, chip in enumerate(chips):
                copy(a, 4 + j, (*chip, 1 - c), me).wait_recv()
        for cp in sent:
            cp.wait_send()
        for cp in mine:
            cp.wait()

    any_spec = pl.BlockSpec(memory_space=pl.ANY)
    return pl.pallas_call(
        body, name=name, in_specs=[any_spec] * na, out_specs=[any_spec] * na,
        out_shape=[jax.ShapeDtypeStruct((N_DEV,) + b.shape, b.dtype) for b in blocks],
        scratch_shapes=[pltpu.SemaphoreType.DMA((7 * na,)), pltpu.SemaphoreType.DMA((7 * na,)),
                        pltpu.SemaphoreType.DMA((na,))],
        compiler_params=pltpu.CompilerParams(has_side_effects=True),
    )(*blocks)


def all_to_all(comm, *, name):
    na = len(comm)
    kinds = [k for k, _ in comm]

    def body(*refs):
        local, remote = _comm_copies(kinds, refs[:na], refs[na:2 * na], *refs[2 * na:])
        for cp in local + remote:
            cp.start()
        for cp in remote:
            cp.wait_recv()
        for cp in remote:
            cp.wait_send()
        for cp in local:
            cp.wait()

    any_spec = pl.BlockSpec(memory_space=pl.ANY)
    return pl.pallas_call(
        body, name=name, in_specs=[any_spec] * na, out_specs=[any_spec] * na,
        out_shape=_comm_out_shapes(comm),
        scratch_shapes=[pltpu.SemaphoreType.DMA((na * N_REL,)), pltpu.SemaphoreType.DMA((na * N_REL,)),
                        pltpu.SemaphoreType.DMA((na,))],
        compiler_params=pltpu.CompilerParams(has_side_effects=True),
    )(*[a for _, a in comm])


def adamw_whole(ws, ms, vs, gparts, *, name, totals=()):
    n, nt = len(ws), len(totals)

    def body(*refs):
        w_refs, m_refs, v_refs, g_refs = refs[:n], refs[n:2 * n], refs[2 * n:3 * n], refs[3 * n:4 * n]
        t_refs, refs = refs[4 * n:4 * n + nt], refs[4 * n + nt:]
        outs, t_outs = refs[:4 * n], refs[4 * n:]
        for k in range(n):
            g, delta, mn, vn = _adamw_math(w_refs[k][...], m_refs[k][...], v_refs[k][...], g_refs[k])
            outs[k][...] = g
            outs[n + k][...] = delta
            outs[2 * n + k][...] = mn
            outs[3 * n + k][...] = vn
        for t_ref, o_ref in zip(t_refs, t_outs):
            o_ref[...] = _device_order_sum(t_ref)

    shapes = [jax.ShapeDtypeStruct(w.shape, F32) for w in ws]
    res = pl.pallas_call(
        body, name=name, out_shape=shapes * 4 + [jax.ShapeDtypeStruct(t.shape[1:], F32) for t in totals],
        compiler_params=pltpu.CompilerParams(vmem_limit_bytes=VMEM_LIMIT),
    )(*ws, *ms, *vs, *gparts, *totals)
    return [res[j * n:(j + 1) * n] for j in range(4)], res[4 * n:]


def _lane_row(v):
    return jnp.pad(v.astype(F32), (0, LANES - v.shape[0])).reshape(1, LANES)


def _block_diag_b(bb):
    eye = jnp.eye(SUBLANES, dtype=F32)
    bt = bb.reshape(S5_BLOCKS, 8, S5_STATE, S5_GROUP).transpose(0, 1, 3, 2)
    return (bt[:, :, :, None, :] * eye[None, :, None, :, None]).reshape(S5_WIDTH, S5_BS)


def _block_diag_c(cc):
    eye = jnp.eye(SUBLANES, dtype=F32)
    ct = cc.reshape(S5_BLOCKS, 8, S5_GROUP, S5_STATE).transpose(3, 0, 1, 2)
    return (ct[None, :, :, :, :] * eye[:, None, None, :, None]).reshape(S5_BS, S5_WIDTH)


def _diag_blocks_b(m):
    eye = jnp.eye(SUBLANES, dtype=F32)
    m5 = m.reshape(S5_BLOCKS, 8, S5_GROUP, 8, S5_STATE)
    return jnp.sum(m5 * eye[None, :, None, :, None], axis=3).transpose(0, 1, 3, 2).reshape(S5_GROUPS, S5_STATE, S5_GROUP)


def _diag_blocks_c(m):
    eye = jnp.eye(SUBLANES, dtype=F32)
    m5 = m.reshape(8, S5_STATE, S5_BLOCKS, 8, S5_GROUP)
    return jnp.sum(m5 * eye[:, None, None, :, None], axis=0).transpose(1, 2, 3, 0).reshape(S5_GROUPS, S5_GROUP, S5_STATE)


def train_step(x, target, p, m, v):
    t = x.shape[0]
    tm = min(256, t)
    tb = min(512, t)
    ts = min(512, t)
    bf = lambda n: p[n].astype(BF16)
    slots = lambda n, a: a.reshape((N_DEV,) + _shard_shape(n))
    updated = {}

    def update(names, got):
        for n, parts in zip(names, got):
            updated[n] = adamw(p[n], m[n], v[n], parts, name=f"adamw_{n}")

    st_in, st_conv = all_gather_arrays([bf("w_in"), p["conv_w"]], name="gather_first")
    win_p = jnp.concatenate([st_in[k][:, a:b] for seg in IN_PADDED_ORDER for k, a, b in _shard_pieces(*IN_SEGMENTS[seg])]
                            + [jnp.zeros((D_MODEL, LANES - SSD_HEADS), BF16)], axis=1)
    conv_w8 = jnp.pad(_join_shards("conv_w", st_conv), ((0, SUBLANES - SSD_CONV), (0, 0)))
    (h, xbc_raw, graw, u5, z, dtr), (st_branch, st_out, st_glu) = rms_matmul(
        x, p["norm_mix_g"], win_p, IN_SPLITS, tm=tb, name="in_proj", out_dtypes=(BF16, BF16, BF16, BF16, F32),
        comm=gathers([bf("w_branch"), bf("w_out"), bf("s5_glu_w")]))
    w_branch = st_branch.reshape(D_MODEL + S5_WIDTH, D_MODEL)
    wout, wglu = st_out.reshape(D_MODEL, D_MODEL), st_glu.reshape(S5_WIDTH, S5_WIDTH)
    bias_row, alog_row = _lane_row(p["dt_bias"]), _lane_row(p["a_log"])
    d_row = jnp.repeat(p["d_ssd"], SSD_HEADDIM).reshape(1, SSD_HEADS * SSD_HEADDIM)
    (ya, yssd, states), (w1,) = ssd_fwd(xbc_raw, conv_w8, p["conv_b"], dtr, z, bias_row, alog_row, d_row,
                                        p["ssd_norm_g"], name="ssd_fwd", comm=gathers([bf("w_mlp_in")]))

    n = S5_N
    a_re_c, a_im_c = p["s5_a_re"].reshape(n, 1), p["s5_a_im"].reshape(n, 1)
    dt_c = jnp.repeat(p["s5_log_dt"], S5_STATE).reshape(n, 1)
    b_re_m, b_im_m = p["s5_b_re"].reshape(n, S5_GROUP), p["s5_b_im"].reshape(n, S5_GROUP)
    ab, bb_re, bb_im = s5_discretize(a_re_c, a_im_c, dt_c, b_re_m, b_im_m, name="s5_disc")
    ab8 = jnp.pad(ab.T, ((0, SUBLANES - 2), (0, 0)))
    bbc = jnp.concatenate([_block_diag_b(bb_re.reshape(S5_GROUPS, S5_STATE, S5_GROUP)),
                           _block_diag_b(bb_im.reshape(S5_GROUPS, S5_STATE, S5_GROUP))], axis=1).astype(BF16)
    ccc = jnp.concatenate([_block_diag_c(p["s5_c_re"]), -_block_diag_c(p["s5_c_im"])], axis=0).astype(BF16)
    s5d_row = p["s5_d"].reshape(1, S5_WIDTH)
    bglu_row = p["s5_glu_b"].reshape(1, S5_WIDTH)
    u5 = s5_interleave(u5, ts)
    (s, ys5, yb), (st_w2,) = s5_fwd(u5, bbc, ccc, ab8, s5d_row, wglu, bglu_row, tm=ts, name="s5_fwd",
                                    comm=gathers([bf("w_mlp_out")]))
    yb = s5_deinterleave(yb, ts)
    w2 = st_w2.reshape(D_FF, D_MODEL)

    x1, pa, pb = merge_fwd(ya, yb, graw, x, w_branch, wout, tm=tb, name="merge_fwd")
    (h2, a1), _ = rms_matmul(x1, p["norm_mlp_g"], w1, (D_FF,), tm=tb, name="mlp_in", out_dtypes=(BF16,))
    dx2, dx2b, loss_row, dgf = mlp_out_loss(a1, x1, w2, p["norm_final_g"], target, tm=tb, name="mlp_out_loss")

    g = {}
    g["norm_final_g"] = dgf[0]
    da1 = mlp_bwd_act(dx2b, a1, w2, tm=tb, name="mlp_bwd_act")
    dw2 = slots("w_mlp_out", matmul_tn(a1, dx2b, out_dtype=BF16, a_relu2=True, name="dw_mlp_out"))
    dw1 = matmul_tn(h2, da1, out_dtype=BF16, col_shards=N_DEV, name="dw_mlp_in")
    (dx1, dx1b, dgm), _ = matmul_nt_rms_bwd([da1], w1, x1, p["norm_mlp_g"], dx2, tm=tb, name="mlp_in_bwd")
    g["norm_mlp_g"] = dgm[0]
    merged, dgraw, dpa, dpb, dya, dyb = merge_bwd(dx1b, graw, pa, pb, w_branch, wout, tm=tm, name="merge_bwd")
    dwo = slots("w_out", matmul_tn(merged, dx1b, out_dtype=BF16, name="dw_out"))
    dwb = slots("w_branch", jnp.concatenate([matmul_tn(ya, dpa, out_dtype=BF16, name="dw_branch_a"),
                                             matmul_tn(yb, dpb, out_dtype=BF16, name="dw_branch_b")], axis=0))

    (du5, dbbc, dccc, dab, dd5, dwglu, dbglu), got = s5_bwd(
        s5_interleave(dyb, ts), ys5, u5, s, bbc, ccc, ab8, s5d_row, wglu, bglu_row, tm=ts, name="s5_bwd",
        comm=exchanges([dw2, dw1, dwo, dwb]))
    du5 = s5_deinterleave(du5, ts)
    update(["w_mlp_out", "w_mlp_in", "w_out", "w_branch"], got)
    g["s5_glu_b"], g["s5_d"] = dbglu[0], dd5[0]
    g["s5_c_re"] = _diag_blocks_c(dccc[:S5_BS])
    g["s5_c_im"] = -_diag_blocks_c(dccc[S5_BS:])
    dbb_re = _diag_blocks_b(dbbc[:, :S5_BS]).reshape(n, S5_GROUP)
    dbb_im = _diag_blocks_b(dbbc[:, S5_BS:]).reshape(n, S5_GROUP)
    da, dlogdt, gb_re, gb_im = s5_discretize_bwd(a_re_c, a_im_c, dt_c, b_re_m, b_im_m, dab[:2].T, dbb_re, dbb_im,
                                                  name="s5_disc_bwd")
    g["s5_a_re"] = da[:, 0].reshape(S5_GROUPS, S5_STATE)
    g["s5_a_im"] = da[:, 1].reshape(S5_GROUPS, S5_STATE)
    g["s5_log_dt"] = dlogdt[:, 0]
    g["s5_b_re"] = gb_re.reshape(S5_GROUPS, S5_STATE, S5_GROUP)
    g["s5_b_im"] = gb_im.reshape(S5_GROUPS, S5_STATE, S5_GROUP)

    def update_small(names, got, name, totals=()):
        (gs, ds, nm, nv), sums = adamw_whole([p[n] for n in names], [m[n] for n in names], [v[n] for n in names], got,
                                             name=name, totals=totals)
        for k, n in enumerate(names):
            updated[n] = (gs[k], ds[k], nm[k], nv[k])
        return sums

    wide = ["s5_b_re", "s5_b_im"]
    late = ["ssd_norm_g", "d_ssd", "a_log", "dt_bias", "conv_b"]
    early = [n for n in SMALL_ORDER if n not in wide + late and n != "norm_mix_g"]
    (dzdt, dxbc_raw, dng, dd_row, dalog_row, dbias_row, dconv_b, dconv_w8), got = ssd_bwd(
        dya, yssd, z, xbc_raw, conv_w8, p["conv_b"], dtr, states, bias_row, alog_row, d_row, p["ssd_norm_g"],
        name="ssd_bwd",
        comm=exchanges([slots("s5_glu_w", dwglu.astype(BF16))])
        + gathers([g[n] for n in wide + early] + [loss_row]))
    update(["s5_glu_w"], got[:1])
    update_small(wide, got[1:1 + len(wide)], "adamw_s5_b")
    loss_sum, = update_small(early, got[1 + len(wide):-1], "adamw_small_early", totals=[got[-1]])
    g["ssd_norm_g"] = dng[0]
    g["d_ssd"], g["a_log"], g["dt_bias"] = dd_row[0, :SSD_HEADS], dalog_row[0, :SSD_HEADS], dbias_row[0, :SSD_HEADS]
    g["conv_b"] = dconv_b[0]
    dconv = dconv_w8[:SSD_CONV].reshape(SSD_CONV, N_DEV, CONV_DIM // N_DEV).transpose(1, 0, 2)
    dproj = [dxbc_raw, dgraw, du5, dzdt]
    dxbc_w, dgate_w, du5_w, dzdt_w = [matmul_tn(h, piece, out_dtype=BF16, name=f"dw_in_{k}")
                                      for k, piece in enumerate(dproj)]
    by_segment = {"z": dzdt_w[:, :D_MODEL], "xbc": dxbc_w, "dt": dzdt_w[:, D_MODEL:], "u5": du5_w, "gates": dgate_w}
    width = D_IN_PROJ // N_DEV
    dw_in = jnp.stack([jnp.concatenate(
        [by_segment[seg][:, max(k * width, lo) - lo:min((k + 1) * width, hi) - lo]
         for seg, (lo, hi) in IN_SEGMENTS.items() if max(k * width, lo) < min((k + 1) * width, hi)], axis=1)
        for k in range(N_DEV)])

    (grad_x, dgx), got = matmul_nt_rms_bwd(
        dproj, win_p, x, p["norm_mix_g"], dx1, tm=tm, name="in_proj_bwd", bf16_copy=False,
        comm=exchanges([dw_in, dconv]) + gathers([g[n] for n in late]))
    update(["w_in", "conv_w"], got[:2])
    update_small(late, got[2:], "adamw_small_late")
    update_small(["norm_mix_g"], all_to_all(gathers([dgx[0]]), name="gather_last"), "adamw_small_last")
    return loss_sum[0, 0], grad_x, updated


WEIGHT_ORDER = ["norm_mix_g", "w_in", "conv_w", "conv_b", "dt_bias", "a_log", "d_ssd", "ssd_norm_g", "s5_a_re",
                "s5_a_im", "s5_log_dt", "s5_b_re", "s5_b_im", "s5_c_re", "s5_c_im", "s5_d", "s5_glu_w", "s5_glu_b",
                "w_branch", "w_out", "norm_mlp_g", "w_mlp_in", "w_mlp_out", "norm_final_g"]
SHARDED = {"w_in": ((D_MODEL, D_IN_PROJ), 1), "conv_w": ((SSD_CONV, CONV_DIM), 1), "s5_glu_w": ((S5_WIDTH, S5_WIDTH), 0),
           "w_branch": ((D_MODEL + S5_WIDTH, D_MODEL), 0), "w_out": ((D_MODEL, D_MODEL), 0),
           "w_mlp_in": ((D_MODEL, D_FF), 1), "w_mlp_out": ((D_FF, D_MODEL), 0)}
SMALL_ORDER = [n for n in WEIGHT_ORDER if n not in SHARDED]


def _shard_shape(name):
    (r, c), ax = SHARDED[name]
    return (r, c // N_DEV) if ax == 1 else (r // N_DEV, c)


def _join_shards(name, stacked):
    (r, c), ax = SHARDED[name]
    if ax == 1:
        return stacked.transpose(1, 0, 2).reshape(r, c)
    return stacked.reshape(r, c)


def _shard_pieces(lo, hi):
    width = D_IN_PROJ // N_DEV
    out = []
    while lo < hi:
        k = lo // width
        end = min(hi, (k + 1) * width)
        out.append((k, lo - k * width, end - k * width))
        lo = end
    return out


IN_SEGMENTS = {"z": (0, 1024), "xbc": (1024, 3072), "dt": (3072, 3088), "u5": (3088, 3600), "gates": (3600, 5648)}
IN_PADDED_ORDER = ("xbc", "gates", "u5", "z", "dt")


def kernel(x, norm_mix_g, w_in, conv_w, conv_b, dt_bias, a_log, d_ssd, ssd_norm_g, s5_a_re, s5_a_im, s5_log_dt, s5_b_re, s5_b_im, s5_c_re, s5_c_im, s5_d, s5_glu_w, s5_glu_b, w_branch, w_out, norm_mlp_g, w_mlp_in, w_mlp_out, norm_final_g, loss_target, m_norm_mix_g, m_w_in, m_conv_w, m_conv_b, m_dt_bias, m_a_log, m_d_ssd, m_ssd_norm_g, m_s5_a_re, m_s5_a_im, m_s5_log_dt, m_s5_b_re, m_s5_b_im, m_s5_c_re, m_s5_c_im, m_s5_d, m_s5_glu_w, m_s5_glu_b, m_w_branch, m_w_out, m_norm_mlp_g, m_w_mlp_in, m_w_mlp_out, m_norm_final_g, v_norm_mix_g, v_w_in, v_conv_w, v_conv_b, v_dt_bias, v_a_log, v_d_ssd, v_ssd_norm_g, v_s5_a_re, v_s5_a_im, v_s5_log_dt, v_s5_b_re, v_s5_b_im, v_s5_c_re, v_s5_c_im, v_s5_d, v_s5_glu_w, v_s5_glu_b, v_w_branch, v_w_out, v_norm_mlp_g, v_w_mlp_in, v_w_mlp_out, v_norm_final_g):
    w = dict(norm_mix_g=norm_mix_g, w_in=w_in, conv_w=conv_w, conv_b=conv_b, dt_bias=dt_bias, a_log=a_log, d_ssd=d_ssd,
             ssd_norm_g=ssd_norm_g, s5_a_re=s5_a_re, s5_a_im=s5_a_im, s5_log_dt=s5_log_dt, s5_b_re=s5_b_re,
             s5_b_im=s5_b_im, s5_c_re=s5_c_re, s5_c_im=s5_c_im, s5_d=s5_d, s5_glu_w=s5_glu_w, s5_glu_b=s5_glu_b,
             w_branch=w_branch, w_out=w_out, norm_mlp_g=norm_mlp_g, w_mlp_in=w_mlp_in, w_mlp_out=w_mlp_out,
             norm_final_g=norm_final_g)
    m = dict(norm_mix_g=m_norm_mix_g, w_in=m_w_in, conv_w=m_conv_w, conv_b=m_conv_b, dt_bias=m_dt_bias, a_log=m_a_log,
             d_ssd=m_d_ssd, ssd_norm_g=m_ssd_norm_g, s5_a_re=m_s5_a_re, s5_a_im=m_s5_a_im, s5_log_dt=m_s5_log_dt,
             s5_b_re=m_s5_b_re, s5_b_im=m_s5_b_im, s5_c_re=m_s5_c_re, s5_c_im=m_s5_c_im, s5_d=m_s5_d,
             s5_glu_w=m_s5_glu_w, s5_glu_b=m_s5_glu_b, w_branch=m_w_branch, w_out=m_w_out, norm_mlp_g=m_norm_mlp_g,
             w_mlp_in=m_w_mlp_in, w_mlp_out=m_w_mlp_out, norm_final_g=m_norm_final_g)
    v = dict(norm_mix_g=v_norm_mix_g, w_in=v_w_in, conv_w=v_conv_w, conv_b=v_conv_b, dt_bias=v_dt_bias, a_log=v_a_log,
             d_ssd=v_d_ssd, ssd_norm_g=v_ssd_norm_g, s5_a_re=v_s5_a_re, s5_a_im=v_s5_a_im, s5_log_dt=v_s5_log_dt,
             s5_b_re=v_s5_b_re, s5_b_im=v_s5_b_im, s5_c_re=v_s5_c_re, s5_c_im=v_s5_c_im, s5_d=v_s5_d,
             s5_glu_w=v_s5_glu_w, s5_glu_b=v_s5_glu_b, w_branch=v_w_branch, w_out=v_w_out, norm_mlp_g=v_norm_mlp_g,
             w_mlp_in=v_w_mlp_in, w_mlp_out=v_w_mlp_out, norm_final_g=v_norm_final_g)

    loss, grad_x, upd = train_step(x[0], loss_target[0], w, m, v)
    return (loss, grad_x.reshape(x.shape), *[upd[n][j] for j in range(4) for n in WEIGHT_ORDER])
```
